```python
import jax, jax.numpy as jnp
from jax import lax
import numpy as np

D_MODEL = 1024
BATCH = 8
SEQ = 2048
DEPTH = 2

N_META = 16
BLOCK = 128
META_PAD = BLOCK - N_META
HEAD_DIM = 64
ROPE_THETA = 10000.0
NORM_EPS = 1e-6
NEG_INF = -1e30
SWA_HEADS = D_MODEL // (2 * HEAD_DIM)
SWA_KV_HEADS = SWA_HEADS // 4
SWA_GROUP = SWA_HEADS // SWA_KV_HEADS
SWA_WINDOW = 128
SWA_WIDTH = SWA_HEADS * HEAD_DIM
SWA_KV_WIDTH = SWA_KV_HEADS * HEAD_DIM
CONV_CHANNELS = D_MODEL // 2
CONV_WIDTH = 31
CONV_LN_EPS = 1e-5
SB_HEADS = D_MODEL // HEAD_DIM
SB_WIDTH = SB_HEADS * HEAD_DIM
AB_SPLITS = (SWA_WIDTH, SWA_KV_WIDTH, SWA_KV_WIDTH, SWA_WIDTH, 2 * CONV_CHANNELS, CONV_CHANNELS)
AB_IN = sum(AB_SPLITS)
AB_MIX = SWA_WIDTH + CONV_CHANNELS
SB_SPLITS = (SB_WIDTH, SB_WIDTH, SB_WIDTH, SB_WIDTH)
SB_IN = sum(SB_SPLITS)
N_EVEN = (DEPTH + 1) // 2
N_ODD = DEPTH // 2

kernel_name = "hybrid_swa_conformer_stickbreaking_trunk"


def _split(x, sizes):
    idx = [int(i) for i in np.cumsum(sizes)[:-1]]
    return jnp.split(x, idx, axis=-1)


def rms_norm(x, g):
    xf = x.astype(jnp.float32)
    y = xf * lax.rsqrt(jnp.mean(xf * xf, axis=-1, keepdims=True) + NORM_EPS)
    return (y * g.astype(jnp.float32)).astype(x.dtype)


def layer_norm(x, g, b):
    xf = x.astype(jnp.float32)
    mu = jnp.mean(xf, axis=-1, keepdims=True)
    xc = xf - mu
    y = xc * lax.rsqrt(jnp.mean(xc * xc, axis=-1, keepdims=True) + CONV_LN_EPS)
    return (y * g.astype(jnp.float32) + b.astype(jnp.float32)).astype(x.dtype)


def apply_rope(x, pos):
    half = x.shape[-1] // 2
    inv = ROPE_THETA ** (-jnp.arange(half, dtype=jnp.float32) / half)
    ang = pos.astype(jnp.float32)[:, None] * inv[None, :]
    cos = jnp.cos(ang)[None, :, None, :]
    sin = jnp.sin(ang)[None, :, None, :]
    xf = x.astype(jnp.float32)
    x1, x2 = xf[..., :half], xf[..., half:]
    return jnp.concatenate([x1 * cos - x2 * sin, x2 * cos + x1 * sin], axis=-1).astype(x.dtype)


def sliding_window_sink_attention(q, k, v, sinks):
    b, l = q.shape[0], q.shape[1]
    lp = l + META_PAD
    nb = lp // BLOCK
    padw = ((0, 0), (META_PAD, 0), (0, 0), (0, 0))
    qb = jnp.pad(q, padw).reshape(b, nb, BLOCK, SWA_KV_HEADS, SWA_GROUP, HEAD_DIM)
    kb = jnp.pad(k, padw).reshape(b, nb, BLOCK, SWA_KV_HEADS, HEAD_DIM)
    vb = jnp.pad(v, padw).reshape(b, nb, BLOCK, SWA_KV_HEADS, HEAD_DIM)

    def band(t):
        prev = jnp.concatenate([jnp.zeros_like(t[:, :1]), t[:, :-1]], axis=1)
        meta = jnp.broadcast_to(t[:, :1], t.shape)
        return jnp.concatenate([meta, prev, t], axis=2)

    kk, vv = band(kb), band(vb)
    scale = HEAD_DIM ** -0.5
    s = jnp.einsum('bnqgrd,bnkgd->bngrqk', qb, kk).astype(jnp.float32) * scale
    blk = jnp.arange(nb)[:, None, None]
    r = jnp.arange(BLOCK)
    qpos = blk * BLOCK + r[None, :, None]
    mpos = r[None, None, :]
    bpos = (blk - 1) * BLOCK + jnp.arange(2 * BLOCK)[None, None, :]
    meta_ok = (mpos >= META_PAD) & (qpos - mpos >= SWA_WINDOW)
    band_ok = (bpos >= META_PAD) & (qpos >= bpos) & (qpos - bpos < SWA_WINDOW)
    mask = jnp.concatenate([jnp.broadcast_to(meta_ok, (nb, BLOCK, BLOCK)), band_ok], axis=-1)
    s = jnp.where(mask[None, :, None, None], s, NEG_INF)
    sink = jnp.broadcast_to(sinks.astype(jnp.float32).reshape(1, 1, SWA_KV_HEADS, SWA_GROUP, 1, 1),
                            s.shape[:-1] + (1,))
    p = jax.nn.softmax(jnp.concatenate([s, sink], axis=-1), axis=-1)[..., :-1]
    o = jnp.einsum('bngrqk,bnkgd->bnqgrd', p.astype(v.dtype), vv)
    return o.reshape(b, lp, SWA_HEADS, HEAD_DIM)[:, META_PAD:]


def causal_depthwise_conv(u, w, bias):
    y = lax.conv_general_dilated(u, w[:, None, :].astype(u.dtype), window_strides=(1,),
                                 padding=((CONV_WIDTH - 1, 0),),
                                 dimension_numbers=('NWC', 'WIO', 'NWC'),
                                 feature_group_count=u.shape[-1])
    return y + bias


def stick_breaking_attention(q, k, v):
    b, l, h, d = q.shape
    lp = l + META_PAD
    nb = lp // BLOCK
    padw = ((0, 0), (META_PAD, 0), (0, 0), (0, 0))
    qp, kp, vp = jnp.pad(q, padw), jnp.pad(k, padw), jnp.pad(v, padw)
    scale = d ** -0.5
    outs = []
    for i in range(nb):
        kend = (i + 1) * BLOCK
        z = jnp.einsum('bqhd,bkhd->bhqk', qp[:, i * BLOCK:kend], kp[:, :kend]).astype(jnp.float32) * scale
        qpos = i * BLOCK + jnp.arange(BLOCK)[:, None]
        kpos = jnp.arange(kend)[None, :]
        valid = (kpos >= META_PAD) & (kpos < qpos)
        log_beta = jax.nn.log_sigmoid(z)
        log_1m = jnp.where(valid, jax.nn.log_sigmoid(-z), 0.0)
        after = lax.cumsum(log_1m, axis=3, reverse=True) - log_1m
        a = jnp.where(valid, jnp.exp(log_beta + after), 0.0)
        outs.append(jnp.einsum('bhqk,bkhd->bqhd', a.astype(v.dtype), vp[:, :kend]))
    return jnp.concatenate(outs, axis=1)[:, META_PAD:]


def swa_conv_mixer(h, pos, w_in, sinks, conv_w, conv_b, ln_g, ln_b, w_pw2, w_out):
    b, l, _ = h.shape
    q, k, v, g_a, glu_in, g_b = _split(h @ w_in, AB_SPLITS)
    q = apply_rope(q.reshape(b, l, SWA_HEADS, HEAD_DIM), pos)
    k = apply_rope(k.reshape(b, l, SWA_KV_HEADS, HEAD_DIM), pos)
    v = v.reshape(b, l, SWA_KV_HEADS, HEAD_DIM)
    a = sliding_window_sink_attention(q, k, v, sinks).reshape(b, l, SWA_WIDTH) * jax.nn.silu(g_a)
    u = glu_in[..., :CONV_CHANNELS] * jax.nn.sigmoid(glu_in[..., CONV_CHANNELS:])
    c = jax.nn.silu(layer_norm(causal_depthwise_conv(u, conv_w, conv_b), ln_g, ln_b))
    c = (c @ w_pw2) * jax.nn.silu(g_b)
    return jnp.concatenate([a, c], axis=-1) @ w_out


def stick_breaking_mixer(h, w_in, w_out):
    b, l, _ = h.shape
    q, k, v, g = _split(h @ w_in, SB_SPLITS)
    shp = (b, l, SB_HEADS, HEAD_DIM)
    o = stick_breaking_attention(q.reshape(shp), k.reshape(shp), v.reshape(shp))
    return (o.reshape(b, l, SB_WIDTH) * jax.nn.silu(g)) @ w_out


def _fwd_setup_inputs(seed: int = 0) -> dict:
    key = jax.random.key(seed)
    ks = jax.random.split(key, 16)
    f32 = jnp.float32
    nrm = lambda k, s: jax.random.normal(k, s, dtype=f32)
    return {
        "x": nrm(ks[0], (BATCH, SEQ, D_MODEL)),
        "meta_tokens": nrm(ks[1], (N_META, D_MODEL)),
        "ab_pre_norm": 1.0 + 0.05 * nrm(ks[2], (N_EVEN, D_MODEL)),
        "ab_w_in": nrm(ks[3], (N_EVEN, D_MODEL, AB_IN)) * D_MODEL ** -0.5,
        "ab_sinks": nrm(ks[4], (N_EVEN, SWA_HEADS)),
        "ab_conv_w": nrm(ks[5], (N_EVEN, CONV_WIDTH, CONV_CHANNELS)) * CONV_WIDTH ** -0.5,
        "ab_conv_b": 0.02 * nrm(ks[6], (N_EVEN, CONV_CHANNELS)),
        "ab_conv_ln_g": 1.0 + 0.05 * nrm(ks[7], (N_EVEN, CONV_CHANNELS)),
        "ab_conv_ln_b": 0.02 * nrm(ks[8], (N_EVEN, CONV_CHANNELS)),
        "ab_w_pw2": nrm(ks[9], (N_EVEN, CONV_CHANNELS, CONV_CHANNELS)) * CONV_CHANNELS ** -0.5,
        "ab_w_out": nrm(ks[10], (N_EVEN, AB_MIX, D_MODEL)) * AB_MIX ** -0.5,
        "ab_post_norm": 1.0 + 0.05 * nrm(ks[11], (N_EVEN, D_MODEL)),
        "sb_pre_norm": 1.0 + 0.05 * nrm(ks[12], (N_ODD, D_MODEL)),
        "sb_w_in": nrm(ks[13], (N_ODD, D_MODEL, SB_IN)) * D_MODEL ** -0.5,
        "sb_w_out": nrm(ks[14], (N_ODD, SB_WIDTH, D_MODEL)) * SB_WIDTH ** -0.5,
        "sb_post_norm": 1.0 + 0.05 * nrm(ks[15], (N_ODD, D_MODEL)),
    }


def _fwd_reference(x, meta_tokens, ab_pre_norm, ab_w_in, ab_sinks, ab_conv_w, ab_conv_b, ab_conv_ln_g,
              ab_conv_ln_b, ab_w_pw2, ab_w_out, ab_post_norm, sb_pre_norm, sb_w_in, sb_w_out,
              sb_post_norm):
    b = x.shape[0]
    meta = jnp.broadcast_to(meta_tokens[None].astype(x.dtype), (b, N_META, D_MODEL))
    h = jnp.concatenate([meta, x], axis=1)
    pos = jnp.arange(h.shape[1])
    for layer in range(DEPTH):
        i = layer // 2
        if layer % 2 == 0:
            y = swa_conv_mixer(rms_norm(h, ab_pre_norm[i]), pos, ab_w_in[i], ab_sinks[i],
                               ab_conv_w[i], ab_conv_b[i], ab_conv_ln_g[i], ab_conv_ln_b[i],
                               ab_w_pw2[i], ab_w_out[i])
            h = h + rms_norm(y, ab_post_norm[i])
        else:
            y = stick_breaking_mixer(rms_norm(h, sb_pre_norm[i]), sb_w_in[i], sb_w_out[i])
            h = h + rms_norm(y, sb_post_norm[i])
    return h[:, N_META:]


import jax as _jax
import jax.numpy as _jnp

TWIN_FORMAT = 'train_step'
FWD_PARAMS = ['x', 'meta_tokens', 'ab_pre_norm', 'ab_w_in', 'ab_sinks', 'ab_conv_w', 'ab_conv_b', 'ab_conv_ln_g', 'ab_conv_ln_b', 'ab_w_pw2', 'ab_w_out', 'ab_post_norm', 'sb_pre_norm', 'sb_w_in', 'sb_w_out', 'sb_post_norm']
TWIN_WEIGHTS = ['meta_tokens', 'ab_pre_norm', 'ab_w_in', 'ab_sinks', 'ab_conv_w', 'ab_conv_b', 'ab_conv_ln_g', 'ab_conv_ln_b', 'ab_w_pw2', 'ab_w_out', 'ab_post_norm', 'sb_pre_norm', 'sb_w_in', 'sb_w_out', 'sb_post_norm']
TWIN_DIFF_INPUT = 'x'
TWIN_INPUTS = ['x', 'meta_tokens', 'ab_pre_norm', 'ab_w_in', 'ab_sinks', 'ab_conv_w', 'ab_conv_b', 'ab_conv_ln_g', 'ab_conv_ln_b', 'ab_w_pw2', 'ab_w_out', 'ab_post_norm', 'sb_pre_norm', 'sb_w_in', 'sb_w_out', 'sb_post_norm', 'loss_target', 'm_meta_tokens', 'm_ab_pre_norm', 'm_ab_w_in', 'm_ab_sinks', 'm_ab_conv_w', 'm_ab_conv_b', 'm_ab_conv_ln_g', 'm_ab_conv_ln_b', 'm_ab_w_pw2', 'm_ab_w_out', 'm_ab_post_norm', 'm_sb_pre_norm', 'm_sb_w_in', 'm_sb_w_out', 'm_sb_post_norm', 'v_meta_tokens', 'v_ab_pre_norm', 'v_ab_w_in', 'v_ab_sinks', 'v_ab_conv_w', 'v_ab_conv_b', 'v_ab_conv_ln_g', 'v_ab_conv_ln_b', 'v_ab_w_pw2', 'v_ab_w_out', 'v_ab_post_norm', 'v_sb_pre_norm', 'v_sb_w_in', 'v_sb_w_out', 'v_sb_post_norm']
TWIN_OUTPUTS = ['loss', 'grad_x', 'grad_meta_tokens', 'grad_ab_pre_norm', 'grad_ab_w_in', 'grad_ab_sinks', 'grad_ab_conv_w', 'grad_ab_conv_b', 'grad_ab_conv_ln_g', 'grad_ab_conv_ln_b', 'grad_ab_w_pw2', 'grad_ab_w_out', 'grad_ab_post_norm', 'grad_sb_pre_norm', 'grad_sb_w_in', 'grad_sb_w_out', 'grad_sb_post_norm', 'delta_meta_tokens', 'delta_ab_pre_norm', 'delta_ab_w_in', 'delta_ab_sinks', 'delta_ab_conv_w', 'delta_ab_conv_b', 'delta_ab_conv_ln_g', 'delta_ab_conv_ln_b', 'delta_ab_w_pw2', 'delta_ab_w_out', 'delta_ab_post_norm', 'delta_sb_pre_norm', 'delta_sb_w_in', 'delta_sb_w_out', 'delta_sb_post_norm', 'new_m_meta_tokens', 'new_m_ab_pre_norm', 'new_m_ab_w_in', 'new_m_ab_sinks', 'new_m_ab_conv_w', 'new_m_ab_conv_b', 'new_m_ab_conv_ln_g', 'new_m_ab_conv_ln_b', 'new_m_ab_w_pw2', 'new_m_ab_w_out', 'new_m_ab_post_norm', 'new_m_sb_pre_norm', 'new_m_sb_w_in', 'new_m_sb_w_out', 'new_m_sb_post_norm', 'new_v_meta_tokens', 'new_v_ab_pre_norm', 'new_v_ab_w_in', 'new_v_ab_sinks', 'new_v_ab_conv_w', 'new_v_ab_conv_b', 'new_v_ab_conv_ln_g', 'new_v_ab_conv_ln_b', 'new_v_ab_w_pw2', 'new_v_ab_w_out', 'new_v_ab_post_norm', 'new_v_sb_pre_norm', 'new_v_sb_w_in', 'new_v_sb_w_out', 'new_v_sb_post_norm']
TWIN_LEAF_KINDS = {'loss': 'loss', 'grad_x': 'grad_x', 'grad_meta_tokens': 'grad_w', 'grad_ab_pre_norm': 'grad_w', 'grad_ab_w_in': 'grad_w', 'grad_ab_sinks': 'grad_w', 'grad_ab_conv_w': 'grad_w', 'grad_ab_conv_b': 'grad_w', 'grad_ab_conv_ln_g': 'grad_w', 'grad_ab_conv_ln_b': 'grad_w', 'grad_ab_w_pw2': 'grad_w', 'grad_ab_w_out': 'grad_w', 'grad_ab_post_norm': 'grad_w', 'grad_sb_pre_norm': 'grad_w', 'grad_sb_w_in': 'grad_w', 'grad_sb_w_out': 'grad_w', 'grad_sb_post_norm': 'grad_w', 'delta_meta_tokens': 'delta_w', 'delta_ab_pre_norm': 'delta_w', 'delta_ab_w_in': 'delta_w', 'delta_ab_sinks': 'delta_w', 'delta_ab_conv_w': 'delta_w', 'delta_ab_conv_b': 'delta_w', 'delta_ab_conv_ln_g': 'delta_w', 'delta_ab_conv_ln_b': 'delta_w', 'delta_ab_w_pw2': 'delta_w', 'delta_ab_w_out': 'delta_w', 'delta_ab_post_norm': 'delta_w', 'delta_sb_pre_norm': 'delta_w', 'delta_sb_w_in': 'delta_w', 'delta_sb_w_out': 'delta_w', 'delta_sb_post_norm': 'delta_w', 'new_m_meta_tokens': 'new_m', 'new_m_ab_pre_norm': 'new_m', 'new_m_ab_w_in': 'new_m', 'new_m_ab_sinks': 'new_m', 'new_m_ab_conv_w': 'new_m', 'new_m_ab_conv_b': 'new_m', 'new_m_ab_conv_ln_g': 'new_m', 'new_m_ab_conv_ln_b': 'new_m', 'new_m_ab_w_pw2': 'new_m', 'new_m_ab_w_out': 'new_m', 'new_m_ab_post_norm': 'new_m', 'new_m_sb_pre_norm': 'new_m', 'new_m_sb_w_in': 'new_m', 'new_m_sb_w_out': 'new_m', 'new_m_sb_post_norm': 'new_m', 'new_v_meta_tokens': 'new_v', 'new_v_ab_pre_norm': 'new_v', 'new_v_ab_w_in': 'new_v', 'new_v_ab_sinks': 'new_v', 'new_v_ab_conv_w': 'new_v', 'new_v_ab_conv_b': 'new_v', 'new_v_ab_conv_ln_g': 'new_v', 'new_v_ab_conv_ln_b': 'new_v', 'new_v_ab_w_pw2': 'new_v', 'new_v_ab_w_out': 'new_v', 'new_v_ab_post_norm': 'new_v', 'new_v_sb_pre_norm': 'new_v', 'new_v_sb_w_in': 'new_v', 'new_v_sb_w_out': 'new_v', 'new_v_sb_post_norm': 'new_v'}


def _forward(args):
    return _fwd_reference(*[args[k] for k in FWD_PARAMS])


def _output_shape():
    out = _jax.eval_shape(lambda: _forward(_fwd_setup_inputs(0)))
    return out.shape, out.dtype

N_MICROBATCH = 1
ADAM_LR = 0.001
ADAM_B1 = 0.9
ADAM_B2 = 0.999
ADAM_EPS = 1e-08
ADAM_WD = 0.01
ADAM_STEP = 10
PER_EXAMPLE_BATCH_AXIS = {'x': 0, 'loss_target': 0}
SHARED_INPUTS = []
_WEIGHT_DTYPES = {'meta_tokens': _jnp.float32, 'ab_pre_norm': _jnp.float32, 'ab_w_in': _jnp.float32, 'ab_sinks': _jnp.float32, 'ab_conv_w': _jnp.float32, 'ab_conv_b': _jnp.float32, 'ab_conv_ln_g': _jnp.float32, 'ab_conv_ln_b': _jnp.float32, 'ab_w_pw2': _jnp.float32, 'ab_w_out': _jnp.float32, 'ab_post_norm': _jnp.float32, 'sb_pre_norm': _jnp.float32, 'sb_w_in': _jnp.float32, 'sb_w_out': _jnp.float32, 'sb_post_norm': _jnp.float32}
MOMENT_SCALE = {'meta_tokens': 2.563273e-02, 'ab_pre_norm': 5.101297e-01, 'ab_w_in': 3.115606e-01, 'ab_sinks': 2.503547e-02, 'ab_conv_w': 4.650976e-01, 'ab_conv_b': 1.318337e+00, 'ab_conv_ln_g': 6.215069e-01, 'ab_conv_ln_b': 7.234851e-01, 'ab_w_pw2': 5.044326e-01, 'ab_w_out': 3.684235e-01, 'ab_post_norm': 1.602901e+01, 'sb_pre_norm': 3.719669e-01, 'sb_w_in': 1.812472e-01, 'sb_w_out': 2.390592e-01, 'sb_post_norm': 1.602351e+01}


def _to_microbatches(a, axis):
    t = _jnp.moveaxis(a, axis, 0)
    t = t.reshape((N_MICROBATCH, t.shape[0] // N_MICROBATCH) + t.shape[1:])
    return _jnp.moveaxis(t, 1, axis + 1)


def setup_inputs(seed: int = 0) -> dict:
    inp = _fwd_setup_inputs(seed)
    key = _jax.random.fold_in(_jax.random.key(seed), 7919)
    shape, _ = _output_shape()
    out = dict(inp)
    out["loss_target"] = _jax.random.normal(_jax.random.fold_in(key, 0), shape, _jnp.float32)
    for i, name in enumerate(TWIN_WEIGHTS):
        w = inp[name].astype(_jnp.float32)
        if MOMENT_SCALE is None:
            s = _jnp.sqrt(_jnp.mean(_jnp.square(w)) + 1e-30)
        else:
            s = MOMENT_SCALE[name]
        km, kv = _jax.random.split(_jax.random.fold_in(key, i + 1))
        out[name] = w
        out["m_" + name] = s * _jax.random.normal(km, w.shape, _jnp.float32)
        out["v_" + name] = (s * s) * _jax.random.uniform(kv, w.shape, _jnp.float32, 0.5, 1.5)
    if N_MICROBATCH > 1:
        for name, axis in PER_EXAMPLE_BATCH_AXIS.items():
            out[name] = _to_microbatches(out[name], axis)
    return {'x': out['x'], 'meta_tokens': out['meta_tokens'], 'ab_pre_norm': out['ab_pre_norm'], 'ab_w_in': out['ab_w_in'], 'ab_sinks': out['ab_sinks'], 'ab_conv_w': out['ab_conv_w'], 'ab_conv_b': out['ab_conv_b'], 'ab_conv_ln_g': out['ab_conv_ln_g'], 'ab_conv_ln_b': out['ab_conv_ln_b'], 'ab_w_pw2': out['ab_w_pw2'], 'ab_w_out': out['ab_w_out'], 'ab_post_norm': out['ab_post_norm'], 'sb_pre_norm': out['sb_pre_norm'], 'sb_w_in': out['sb_w_in'], 'sb_w_out': out['sb_w_out'], 'sb_post_norm': out['sb_post_norm'], 'loss_target': out['loss_target'], 'm_meta_tokens': out['m_meta_tokens'], 'm_ab_pre_norm': out['m_ab_pre_norm'], 'm_ab_w_in': out['m_ab_w_in'], 'm_ab_sinks': out['m_ab_sinks'], 'm_ab_conv_w': out['m_ab_conv_w'], 'm_ab_conv_b': out['m_ab_conv_b'], 'm_ab_conv_ln_g': out['m_ab_conv_ln_g'], 'm_ab_conv_ln_b': out['m_ab_conv_ln_b'], 'm_ab_w_pw2': out['m_ab_w_pw2'], 'm_ab_w_out': out['m_ab_w_out'], 'm_ab_post_norm': out['m_ab_post_norm'], 'm_sb_pre_norm': out['m_sb_pre_norm'], 'm_sb_w_in': out['m_sb_w_in'], 'm_sb_w_out': out['m_sb_w_out'], 'm_sb_post_norm': out['m_sb_post_norm'], 'v_meta_tokens': out['v_meta_tokens'], 'v_ab_pre_norm': out['v_ab_pre_norm'], 'v_ab_w_in': out['v_ab_w_in'], 'v_ab_sinks': out['v_ab_sinks'], 'v_ab_conv_w': out['v_ab_conv_w'], 'v_ab_conv_b': out['v_ab_conv_b'], 'v_ab_conv_ln_g': out['v_ab_conv_ln_g'], 'v_ab_conv_ln_b': out['v_ab_conv_ln_b'], 'v_ab_w_pw2': out['v_ab_w_pw2'], 'v_ab_w_out': out['v_ab_w_out'], 'v_ab_post_norm': out['v_ab_post_norm'], 'v_sb_pre_norm': out['v_sb_pre_norm'], 'v_sb_w_in': out['v_sb_w_in'], 'v_sb_w_out': out['v_sb_w_out'], 'v_sb_post_norm': out['v_sb_post_norm']}


def _loss(weights, diff, rest, loss_target):
    with _jax.named_scope("forward"):
        args = {**rest, TWIN_DIFF_INPUT: diff, **{k: w.astype(_WEIGHT_DTYPES[k]) for k, w in weights.items()}}
        y = _forward(args)
    with _jax.named_scope("loss_head"):
        err = _jnp.square(y.astype(_jnp.float32) - loss_target)
        return 0.5 * _jnp.sum(_jnp.mean(err, axis=-1)) if err.ndim else 0.5 * err


def _adamw(w, g, m, v):
    m = ADAM_B1 * m + (1.0 - ADAM_B1) * g
    v = ADAM_B2 * v + (1.0 - ADAM_B2) * _jnp.square(g)
    m_hat = m / (1.0 - ADAM_B1 ** ADAM_STEP)
    v_hat = v / (1.0 - ADAM_B2 ** ADAM_STEP)
    delta = -ADAM_LR * (m_hat / (_jnp.sqrt(v_hat) + ADAM_EPS) + ADAM_WD * w)
    return delta, m, v


def reference(x, meta_tokens, ab_pre_norm, ab_w_in, ab_sinks, ab_conv_w, ab_conv_b, ab_conv_ln_g, ab_conv_ln_b, ab_w_pw2, ab_w_out, ab_post_norm, sb_pre_norm, sb_w_in, sb_w_out, sb_post_norm, loss_target, m_meta_tokens, m_ab_pre_norm, m_ab_w_in, m_ab_sinks, m_ab_conv_w, m_ab_conv_b, m_ab_conv_ln_g, m_ab_conv_ln_b, m_ab_w_pw2, m_ab_w_out, m_ab_post_norm, m_sb_pre_norm, m_sb_w_in, m_sb_w_out, m_sb_post_norm, v_meta_tokens, v_ab_pre_norm, v_ab_w_in, v_ab_sinks, v_ab_conv_w, v_ab_conv_b, v_ab_conv_ln_g, v_ab_conv_ln_b, v_ab_w_pw2, v_ab_w_out, v_ab_post_norm, v_sb_pre_norm, v_sb_w_in, v_sb_w_out, v_sb_post_norm):
    given = dict(x=x, meta_tokens=meta_tokens, ab_pre_norm=ab_pre_norm, ab_w_in=ab_w_in, ab_sinks=ab_sinks, ab_conv_w=ab_conv_w, ab_conv_b=ab_conv_b, ab_conv_ln_g=ab_conv_ln_g, ab_conv_ln_b=ab_conv_ln_b, ab_w_pw2=ab_w_pw2, ab_w_out=ab_w_out, ab_post_norm=ab_post_norm, sb_pre_norm=sb_pre_norm, sb_w_in=sb_w_in, sb_w_out=sb_w_out, sb_post_norm=sb_post_norm, loss_target=loss_target, m_meta_tokens=m_meta_tokens, m_ab_pre_norm=m_ab_pre_norm, m_ab_w_in=m_ab_w_in, m_ab_sinks=m_ab_sinks, m_ab_conv_w=m_ab_conv_w, m_ab_conv_b=m_ab_conv_b, m_ab_conv_ln_g=m_ab_conv_ln_g, m_ab_conv_ln_b=m_ab_conv_ln_b, m_ab_w_pw2=m_ab_w_pw2, m_ab_w_out=m_ab_w_out, m_ab_post_norm=m_ab_post_norm, m_sb_pre_norm=m_sb_pre_norm, m_sb_w_in=m_sb_w_in, m_sb_w_out=m_sb_w_out, m_sb_post_norm=m_sb_post_norm, v_meta_tokens=v_meta_tokens, v_ab_pre_norm=v_ab_pre_norm, v_ab_w_in=v_ab_w_in, v_ab_sinks=v_ab_sinks, v_ab_conv_w=v_ab_conv_w, v_ab_conv_b=v_ab_conv_b, v_ab_conv_ln_g=v_ab_conv_ln_g, v_ab_conv_ln_b=v_ab_conv_ln_b, v_ab_w_pw2=v_ab_w_pw2, v_ab_w_out=v_ab_w_out, v_ab_post_norm=v_ab_post_norm, v_sb_pre_norm=v_sb_pre_norm, v_sb_w_in=v_sb_w_in, v_sb_w_out=v_sb_w_out, v_sb_post_norm=v_sb_post_norm)
    weights = {n: given[n] for n in TWIN_WEIGHTS}
    shared = {n: given[n] for n in SHARED_INPUTS}
    per_example = {n: given[n] for n in ['x']}
    grad_fn = _jax.value_and_grad(_loss, argnums=(0, 1))

    def one_microbatch(ex, loss_target):
        ex = dict(ex)
        diff = ex.pop(TWIN_DIFF_INPUT)
        return grad_fn(weights, diff, {**shared, **ex}, loss_target)

    if N_MICROBATCH == 1:
        loss, (grad_w, grad_x) = one_microbatch(per_example, given["loss_target"])
    else:
        def body(carry, xs):
            loss_sum, grad_sum = carry
            l_k, (gw_k, gx_k) = one_microbatch(xs[0], xs[1])
            with _jax.named_scope("update"):
                return (loss_sum + l_k, _jax.tree.map(_jnp.add, grad_sum, gw_k)), gx_k

        init = (_jnp.zeros((), _jnp.float32), _jax.tree.map(_jnp.zeros_like, weights))
        (loss, grad_w), grad_x = _jax.lax.scan(body, init, (per_example, given["loss_target"]))
    with _jax.named_scope("update"):
        delta_w, new_m, new_v = {}, {}, {}
        for n in TWIN_WEIGHTS:
            delta_w[n], new_m[n], new_v[n] = _adamw(weights[n], grad_w[n], given["m_" + n], given["v_" + n])
    return (loss, grad_x, *[grad_w[n] for n in TWIN_WEIGHTS], *[delta_w[n] for n in TWIN_WEIGHTS],
            *[new_m[n] for n in TWIN_WEIGHTS], *[new_v[n] for n in TWIN_WEIGHTS])
```

```python
import functools

import jax
import jax.numpy as jnp
from jax import lax
from jax.experimental import pallas as pl
from jax.experimental.pallas import tpu as pltpu

F32 = jnp.float32
BF16 = jnp.bfloat16

D = 1024
SEQ = 2048
N_META = 16
TB = 128
PAD = TB - N_META
R = SEQ + TB
NB = R // TB
HD = 64
ROPE_THETA = 10000.0
NORM_EPS = 1e-6
LN_EPS = 1e-5
NEG = -1e30
SWA_HEADS = 8
CONV_W = 31
SCALE = HD ** -0.5
N_CHIPS = 4

C_Q, C_GA, C_GLU, C_GB, C_K, C_V = 0, 512, 1024, 2048, 2560, 2688
AB_IN = 2816

ADAM_LR, ADAM_B1, ADAM_B2, ADAM_EPS, ADAM_WD, ADAM_STEP = 0.001, 0.9, 0.999, 1e-08, 0.01, 10

VMEM_LIMIT = 56 * 1024 * 1024


def _cp(sem):
    return pltpu.CompilerParams(dimension_semantics=sem, vmem_limit_bytes=VMEM_LIMIT)


def _sig(x):
    return 1.0 / (1.0 + jnp.exp(-x))


def _dot(a, b):
    return lax.dot_general(a, b, (((1,), (0,)), ((), ())), preferred_element_type=F32)


def _dot_nt(a, b):
    return lax.dot_general(a, b, (((1,), (1,)), ((), ())), preferred_element_type=F32)


def _dot_tn(a, b):
    return lax.dot_general(a, b, (((0,), (0,)), ((), ())), preferred_element_type=F32)


def _mm(pairs, out_dtype, name, tm, tn):
    m, n = pairs[0][0].shape[0], pairs[0][1].shape[1]
    npairs = len(pairs)

    def body(*refs):
        o_ref = refs[2 * npairs]
        acc = None
        for i in range(npairs):
            t = _dot(refs[2 * i][...].astype(BF16), refs[2 * i + 1][...].astype(BF16))
            acc = t if acc is None else acc + t
        o_ref[...] = acc.astype(out_dtype)

    in_specs, args = [], []
    for a, b in pairs:
        k = a.shape[1]
        in_specs += [pl.BlockSpec((tm, k), lambda i, j: (i, 0)), pl.BlockSpec((k, tn), lambda i, j: (0, j))]
        args += [a, b]
    return pl.pallas_call(
        body, grid=(m // tm, n // tn), in_specs=in_specs,
        out_specs=pl.BlockSpec((tm, tn), lambda i, j: (i, j)),
        out_shape=jax.ShapeDtypeStruct((m, n), out_dtype), name=name,
        compiler_params=_cp(("parallel", "parallel")))(*args)


def _rms_fwd(h, g, name):
    def body(h_ref, g_ref, o_ref):
        x = h_ref[...]
        r = lax.rsqrt(jnp.mean(x * x, axis=1, keepdims=True) + NORM_EPS)
        o_ref[...] = (x * r * g_ref[...]).astype(BF16)

    return pl.pallas_call(
        body, grid=(NB,),
        in_specs=[pl.BlockSpec((TB, D), lambda n: (n, 0)), pl.BlockSpec((1, D), lambda n: (0, 0))],
        out_specs=pl.BlockSpec((TB, D), lambda n: (n, 0)),
        out_shape=jax.ShapeDtypeStruct((R, D), BF16), name=name, compiler_params=_cp(("parallel",)))(h, g)


def _post_fwd(h, y, g, name):
    def body(h_ref, y_ref, g_ref, o_ref):
        yv = y_ref[...]
        r = lax.rsqrt(jnp.mean(yv * yv, axis=1, keepdims=True) + NORM_EPS)
        o_ref[...] = h_ref[...] + yv * r * g_ref[...]

    blk = pl.BlockSpec((TB, D), lambda n: (n, 0))
    return pl.pallas_call(
        body, grid=(NB,), in_specs=[blk, blk, pl.BlockSpec((1, D), lambda n: (0, 0))], out_specs=blk,
        out_shape=jax.ShapeDtypeStruct((R, D), F32), name=name, compiler_params=_cp(("parallel",)))(h, y, g)


def _rms_bwd(dout, x, g, res, out_dtype, name):
    has_res = res is not None

    def body(*refs):
        if has_res:
            d_ref, x_ref, g_ref, r_ref, dx_ref, dg_ref = refs
        else:
            d_ref, x_ref, g_ref, dx_ref, dg_ref = refs
        n = pl.program_id(0)
        xv = x_ref[...]
        dv = d_ref[...]
        r = lax.rsqrt(jnp.mean(xv * xv, axis=1, keepdims=True) + NORM_EPS)
        xh = xv * r
        dxh = dv * g_ref[...]
        dx = r * (dxh - xh * jnp.mean(dxh * xh, axis=1, keepdims=True))
        if has_res:
            dx = dx + r_ref[...]
        row = lax.broadcasted_iota(jnp.int32, (TB, D), 0) + n * TB
        dx_ref[...] = jnp.where(row >= PAD, dx, 0.0).astype(out_dtype)

        @pl.when(n == 0)
        def _():
            dg_ref[...] = jnp.zeros_like(dg_ref)

        dg_ref[...] += jnp.sum(dv * xh, axis=0, keepdims=True)

    blk = pl.BlockSpec((TB, D), lambda n: (n, 0))
    vec = pl.BlockSpec((1, D), lambda n: (0, 0))
    ins = [dout, x, g] + ([res] if has_res else [])
    in_specs = [blk, blk, vec] + ([blk] if has_res else [])
    return pl.pallas_call(
        body, grid=(NB,), in_specs=in_specs, out_specs=[blk, vec],
        out_shape=[jax.ShapeDtypeStruct((R, D), out_dtype), jax.ShapeDtypeStruct((1, D), F32)],
        name=name, compiler_params=_cp(("arbitrary",)))(*ins)


GW = 512


def _gate_fwd(o, gsrc, goff, name):
    w = o.shape[1]

    def body(o_ref, g_ref, m_ref):
        gv = g_ref[...]
        m_ref[...] = (o_ref[...] * (gv * _sig(gv))).astype(BF16)

    gb = goff // GW
    return pl.pallas_call(
        body, grid=(NB, w // GW),
        in_specs=[pl.BlockSpec((TB, GW), lambda n, j: (n, j)), pl.BlockSpec((TB, GW), lambda n, j: (n, gb + j))],
        out_specs=pl.BlockSpec((TB, GW), lambda n, j: (n, j)),
        out_shape=jax.ShapeDtypeStruct((R, w), BF16), name=name,
        compiler_params=_cp(("parallel", "parallel")))(o, gsrc)


def _gate_bwd(dsrc, doff, o, gsrc, goff, name):
    w = o.shape[1]

    def body(d_ref, o_ref, g_ref, do_ref, dg_ref):
        gv = g_ref[...]
        dv = d_ref[...]
        s = _sig(gv)
        do_ref[...] = dv * (gv * s)
        dg_ref[...] = (dv * o_ref[...] * (s * (1.0 + gv * (1.0 - s)))).astype(BF16)

    db, gb = doff // GW, goff // GW
    blk = pl.BlockSpec((TB, GW), lambda n, j: (n, j))
    return pl.pallas_call(
        body, grid=(NB, w // GW),
        in_specs=[pl.BlockSpec((TB, GW), lambda n, j: (n, db + j)), blk,
                  pl.BlockSpec((TB, GW), lambda n, j: (n, gb + j))],
        out_specs=[blk, blk],
        out_shape=[jax.ShapeDtypeStruct((R, w), F32), jax.ShapeDtypeStruct((R, w), BF16)], name=name,
        compiler_params=_cp(("parallel", "parallel")))(dsrc, o, gsrc)


def _loss_bwd(h, target):
    def body(h_ref, t_ref, d_ref, l_ref):
        n = pl.program_id(0)

        @pl.when(n == 0)
        def _():
            d_ref[...] = jnp.zeros_like(d_ref)
            l_ref[...] = jnp.zeros_like(l_ref)

        @pl.when(n > 0)
        def _():
            err = h_ref[...] - t_ref[...]
            d_ref[...] = err * (1.0 / D)
            l_ref[...] += jnp.sum(err * err, axis=0, keepdims=True)

        @pl.when(n == NB - 1)
        def _():
            tot = jnp.sum(l_ref[...], axis=1, keepdims=True) * (0.5 / D)
            l_ref[...] = jnp.broadcast_to(tot, (1, D))

    blk = pl.BlockSpec((TB, D), lambda n: (n, 0))
    return pl.pallas_call(
        body, grid=(NB,),
        in_specs=[blk, pl.BlockSpec((TB, D), lambda n: (jnp.maximum(n - 1, 0), 0))],
        out_specs=[blk, pl.BlockSpec((1, D), lambda n: (0, 0))],
        out_shape=[jax.ShapeDtypeStruct((R, D), F32), jax.ShapeDtypeStruct((1, D), F32)],
        name="loss_bwd", compiler_params=_cp(("arbitrary",)))(h, target)


def _lane_row(shape):
    return lax.broadcasted_iota(jnp.int32, shape, 1), lax.broadcasted_iota(jnp.int32, shape, 0)


def _rot_half(x, lane):
    return jnp.where(lane % HD < HD // 2, pltpu.roll(x, 128 - HD // 2, 1), pltpu.roll(x, HD // 2, 1))


def _swa_blocks(n):
    return (0, jnp.maximum(n - 1, 0), n)


def _swa_masks(n, lane, row):
    qpos = n * TB + row
    kp = (n - 1) * TB + lane
    kc = n * TB + lane
    m0 = (lane >= PAD) & (qpos - lane >= TB)
    mp = (kp >= PAD) & (qpos >= kp) & (qpos - kp < TB)
    mc = (kc >= PAD) & (qpos >= kc)
    return (m0, mp, mc)


def _swa_load(n, zq_ref, zkv_ref, cs_ref, sn_ref, lane):
    r0 = pl.multiple_of(n * TB, TB)
    csq, snq = cs_ref[pl.ds(r0, TB), :], sn_ref[pl.ds(r0, TB), :]
    qc = []
    for c in range(4):
        x = zq_ref[:, c * 128:(c + 1) * 128]
        qc.append(x * csq + _rot_half(x, lane) * snq)
    kvs = []
    for b in _swa_blocks(n):
        b0 = pl.multiple_of(b * TB, TB)
        csb, snb = cs_ref[pl.ds(b0, TB), :], sn_ref[pl.ds(b0, TB), :]
        kx = zkv_ref[pl.ds(b0, TB), 0:128]
        kr = kx * csb + _rot_half(kx, lane) * snb
        vx = zkv_ref[pl.ds(b0, TB), 128:256]
        kvs.append((kr.astype(BF16), pltpu.roll(kr, HD, 1).astype(BF16),
                    vx.astype(BF16), pltpu.roll(vx, HD, 1).astype(BF16), csb, snb, b0))
    return qc, (csq, snq), kvs


def _swa_fwd(z0, cs, sn, sinks):
    def body(zq_ref, zkv_ref, cs_ref, sn_ref, sk_ref, o_ref, lse_ref):
        n = pl.program_id(0)
        lane, row = _lane_row((TB, 128))
        lo = lane < HD
        masks = _swa_masks(n, lane, row)
        qc, _, kvs = _swa_load(n, zq_ref, zkv_ref, cs_ref, sn_ref, lane)
        oh = []
        lse_t = jnp.zeros((TB, 128), F32)
        for h in range(SWA_HEADS):
            c, par, g = h // 2, h % 2, h // 4
            half = lo if par == 0 else jnp.logical_not(lo)
            qm = jnp.where(half, qc[c], 0.0).astype(BF16)
            sink = sk_ref[0, h]
            ss = []
            for (k, ka, v, va, _, _, _), m in zip(kvs, masks):
                ksel = k if par == g else ka
                ss.append(jnp.where(m, _dot_nt(qm, ksel) * SCALE, NEG))
            mx = jnp.maximum(jnp.maximum(jnp.max(ss[0], axis=1, keepdims=True), jnp.max(ss[1], axis=1, keepdims=True)),
                             jnp.max(ss[2], axis=1, keepdims=True))
            mx = jnp.maximum(mx, sink)
            es = [jnp.exp(s - mx) for s in ss]
            den = (jnp.sum(es[0], axis=1, keepdims=True) + jnp.sum(es[1], axis=1, keepdims=True)
                   + jnp.sum(es[2], axis=1, keepdims=True) + jnp.exp(sink - mx))
            inv = 1.0 / den
            t = jnp.zeros((TB, 128), F32)
            for (k, ka, v, va, _, _, _), e in zip(kvs, es):
                t = t + _dot((e * inv).astype(BF16), v if par == g else va)
            oh.append(t)
            lse_t = jnp.where(lane == h, mx + jnp.log(den), lse_t)
        oc = [jnp.where(lo, oh[2 * c], oh[2 * c + 1]) for c in range(4)]
        for c in range(4):
            o_ref[:, c * 128:(c + 1) * 128] = oc[c]
        lse_ref[...] = lse_t

    full = pl.BlockSpec((R, 128), lambda n: (0, 0))
    return pl.pallas_call(
        body, grid=(NB,),
        in_specs=[pl.BlockSpec((TB, 512), lambda n: (n, C_Q // 512)),
                  pl.BlockSpec((R, 256), lambda n: (0, C_K // 256)), full, full,
                  pl.BlockSpec(memory_space=pltpu.SMEM)],
        out_specs=[pl.BlockSpec((TB, 512), lambda n: (n, 0)), pl.BlockSpec((TB, 128), lambda n: (n, 0))],
        out_shape=[jax.ShapeDtypeStruct((R, 512), F32), jax.ShapeDtypeStruct((R, 128), F32)],
        name="swa_fwd", compiler_params=_cp(("parallel",)))(z0, z0, cs, sn, sinks)


def _swa_bwd(z0, cs, sn, sinks, o, do, lse):
    def body(zq_ref, zkv_ref, cs_ref, sn_ref, sk_ref, o_ref, do_ref, lse_ref, dq_ref, dkv_ref, dsk_ref):
        n = pl.program_id(0)

        @pl.when(n == 0)
        def _():
            dkv_ref[...] = jnp.zeros_like(dkv_ref)
            dsk_ref[...] = jnp.zeros_like(dsk_ref)

        lane, row = _lane_row((TB, 128))
        lo = lane < HD
        masks = _swa_masks(n, lane, row)
        qc, (csq, snq), kvs = _swa_load(n, zq_ref, zkv_ref, cs_ref, sn_ref, lane)
        lse_t = lse_ref[...]
        dqh = []
        dk_al = [jnp.zeros((TB, 128), F32) for _ in range(3)]
        dk_mis = [jnp.zeros((TB, 128), F32) for _ in range(3)]
        dv_al = [jnp.zeros((TB, 128), F32) for _ in range(3)]
        dv_mis = [jnp.zeros((TB, 128), F32) for _ in range(3)]
        dsk_t = jnp.zeros((TB, 128), F32)
        for h in range(SWA_HEADS):
            c, par, g = h // 2, h % 2, h // 4
            half = lo if par == 0 else jnp.logical_not(lo)
            qm = jnp.where(half, qc[c], 0.0).astype(BF16)
            dov = do_ref[:, c * 128:(c + 1) * 128]
            dom = jnp.where(half, dov, 0.0)
            delta = jnp.sum(dom * o_ref[:, c * 128:(c + 1) * 128], axis=1, keepdims=True)
            dob = dom.astype(BF16)
            lse_h = jnp.sum(jnp.where(lane == h, lse_t, 0.0), axis=1, keepdims=True)
            sink = sk_ref[0, h]
            dqt = jnp.zeros((TB, 128), F32)
            for bi, ((k, ka, v, va, _, _, _), m) in enumerate(zip(kvs, masks)):
                ksel = k if par == g else ka
                vsel = v if par == g else va
                s = jnp.where(m, _dot_nt(qm, ksel) * SCALE, NEG)
                p = jnp.exp(s - lse_h)
                dp = _dot_nt(dob, vsel)
                ds = (p * (dp - delta) * SCALE).astype(BF16)
                dqt = dqt + _dot(ds, ksel)
                dkh = _dot_tn(ds, qm)
                dvh = _dot_tn(p.astype(BF16), dob)
                if par == g:
                    dk_al[bi] = dk_al[bi] + dkh
                    dv_al[bi] = dv_al[bi] + dvh
                else:
                    dk_mis[bi] = dk_mis[bi] + dkh
                    dv_mis[bi] = dv_mis[bi] + dvh
            dqh.append(dqt)
            dsk_t = jnp.where(lane == h, -jnp.exp(sink - lse_h) * delta, dsk_t)
        dqc = [jnp.where(lo, dqh[2 * c], dqh[2 * c + 1]) for c in range(4)]
        for c in range(4):
            t = dqc[c] * snq
            dq_ref[:, c * 128:(c + 1) * 128] = dqc[c] * csq + _rot_half(t, lane)
        for bi, (_, _, _, _, csb, snb, b0) in enumerate(kvs):
            dk = dk_al[bi] + pltpu.roll(dk_mis[bi], HD, 1)
            dv = dv_al[bi] + pltpu.roll(dv_mis[bi], HD, 1)
            dkv_ref[pl.ds(b0, TB), 0:128] += dk * csb + _rot_half(dk * snb, lane)
            dkv_ref[pl.ds(b0, TB), 128:256] += dv
        dsk_ref[0:1, :] += jnp.sum(dsk_t, axis=0, keepdims=True)

    full = pl.BlockSpec((R, 128), lambda n: (0, 0))
    b512 = pl.BlockSpec((TB, 512), lambda n: (n, 0))
    return pl.pallas_call(
        body, grid=(NB,),
        in_specs=[pl.BlockSpec((TB, 512), lambda n: (n, C_Q // 512)),
                  pl.BlockSpec((R, 256), lambda n: (0, C_K // 256)), full, full,
                  pl.BlockSpec(memory_space=pltpu.SMEM), b512, b512, pl.BlockSpec((TB, 128), lambda n: (n, 0))],
        out_specs=[b512, pl.BlockSpec((R, 256), lambda n: (0, 0)), pl.BlockSpec((8, 128), lambda n: (0, 0))],
        out_shape=[jax.ShapeDtypeStruct((R, 512), F32), jax.ShapeDtypeStruct((R, 256), F32),
                   jax.ShapeDtypeStruct((8, 128), F32)],
        name="swa_bwd", compiler_params=_cp(("arbitrary",)))(z0, z0, cs, sn, sinks, o, do, lse)


CC = 512
HALO = CONV_W - 1


def _conv_fwd(z0, conv_w, conv_b, ln_g, ln_b):
    def body(g_ref, w_ref, cb_ref, lg_ref, lb_ref, cv_ref, s_ref, ubuf):
        n = pl.program_id(0)

        @pl.when(n == 0)
        def _():
            ubuf[0:TB, :] = jnp.zeros((TB, CC), F32)

        u = g_ref[:, 0:CC] * _sig(g_ref[:, CC:2 * CC])
        ubuf[TB:2 * TB, :] = u
        acc = jnp.zeros((TB, CC), F32)
        for w in range(CONV_W):
            acc = acc + ubuf[pl.ds(TB - HALO + w, TB), :] * w_ref[w:w + 1, :]
        cv = acc + cb_ref[...]
        cv_ref[...] = cv
        xc = cv - jnp.mean(cv, axis=1, keepdims=True)
        rs = lax.rsqrt(jnp.mean(xc * xc, axis=1, keepdims=True) + LN_EPS)
        ln = xc * rs * lg_ref[...] + lb_ref[...]
        s_ref[...] = (ln * _sig(ln)).astype(BF16)
        ubuf[0:TB, :] = u

    vec = pl.BlockSpec((1, CC), lambda n: (0, 0))
    blk = pl.BlockSpec((TB, CC), lambda n: (n, 0))
    return pl.pallas_call(
        body, grid=(NB,),
        in_specs=[pl.BlockSpec((TB, 2 * CC), lambda n: (n, C_GLU // (2 * CC))),
                  pl.BlockSpec((32, CC), lambda n: (0, 0)), vec, vec, vec],
        out_specs=[blk, blk],
        out_shape=[jax.ShapeDtypeStruct((R, CC), F32), jax.ShapeDtypeStruct((R, CC), BF16)],
        scratch_shapes=[pltpu.VMEM((2 * TB, CC), F32)],
        name="conv_fwd", compiler_params=_cp(("arbitrary",)))(z0, conv_w, conv_b, ln_g, ln_b)


def _conv_bwd(ds, cv, z0, conv_w, ln_g, ln_b):
    def body(ds_ref, cv_ref, g_ref, w_ref, lg_ref, lb_ref, dglu_ref, dw_ref, dsm_ref, dbuf):
        n = pl.program_id(0)

        @pl.when(n == 0)
        def _():
            dbuf[TB:2 * TB, :] = jnp.zeros((TB, CC), F32)
            dw_ref[...] = jnp.zeros_like(dw_ref)
            dsm_ref[...] = jnp.zeros_like(dsm_ref)

        cv = cv_ref[...]
        xc = cv - jnp.mean(cv, axis=1, keepdims=True)
        rs = lax.rsqrt(jnp.mean(xc * xc, axis=1, keepdims=True) + LN_EPS)
        xh = xc * rs
        ln = xh * lg_ref[...] + lb_ref[...]
        sg = _sig(ln)
        dln = ds_ref[...] * (sg * (1.0 + ln * (1.0 - sg)))
        dxh = dln * lg_ref[...]
        dcv = rs * (dxh - jnp.mean(dxh, axis=1, keepdims=True) - xh * jnp.mean(dxh * xh, axis=1, keepdims=True))
        dsm_ref[0:1, :] += jnp.sum(dcv, axis=0, keepdims=True)
        dsm_ref[1:2, :] += jnp.sum(dln * xh, axis=0, keepdims=True)
        dsm_ref[2:3, :] += jnp.sum(dln, axis=0, keepdims=True)
        dbuf[0:TB, :] = dcv
        a = g_ref[:, 0:CC]
        sb = _sig(g_ref[:, CC:2 * CC])
        u = a * sb
        du = jnp.zeros((TB, CC), F32)
        for w in range(CONV_W):
            sh = dbuf[pl.ds(HALO - w, TB), :]
            du = du + sh * w_ref[w:w + 1, :]
            dw_ref[w:w + 1, :] += jnp.sum(u * sh, axis=0, keepdims=True)
        dglu_ref[:, 0:CC] = (du * sb).astype(BF16)
        dglu_ref[:, CC:2 * CC] = (du * a * sb * (1.0 - sb)).astype(BF16)
        dbuf[TB:2 * TB, :] = dcv

    rev = lambda n: (NB - 1 - n, 0)
    vec = pl.BlockSpec((1, CC), lambda n: (0, 0))
    blk = pl.BlockSpec((TB, CC), rev)
    return pl.pallas_call(
        body, grid=(NB,),
        in_specs=[blk, blk, pl.BlockSpec((TB, 2 * CC), lambda n: (NB - 1 - n, C_GLU // (2 * CC))),
                  pl.BlockSpec((32, CC), lambda n: (0, 0)), vec, vec],
        out_specs=[pl.BlockSpec((TB, 2 * CC), rev), pl.BlockSpec((32, CC), lambda n: (0, 0)),
                   pl.BlockSpec((8, CC), lambda n: (0, 0))],
        out_shape=[jax.ShapeDtypeStruct((R, 2 * CC), BF16), jax.ShapeDtypeStruct((32, CC), F32),
                   jax.ShapeDtypeStruct((8, CC), F32)],
        scratch_shapes=[pltpu.VMEM((2 * TB, CC), F32)],
        name="conv_bwd", compiler_params=_cp(("arbitrary",)))(ds, cv, z0, conv_w, ln_g, ln_b)


def _split_dot(x, t):
    hi = x.astype(BF16)
    lo = (x - hi.astype(F32)).astype(BF16)
    return _dot(hi, t) + _dot(lo, t)


def _sb_tile(qh, kj, valid, tri_gt):
    z = _dot_nt(qh, kj) * SCALE
    e = jnp.exp(-jnp.abs(z))
    lb = jnp.minimum(z, 0.0) - jnp.log(1.0 + e)
    l1 = jnp.where(valid, lb - z, 0.0)
    return z, e, lb, l1, _split_dot(l1, tri_gt)


def _sb_fwd(q, k, v):
    def body(q_ref, k_ref, v_ref, o_ref, c_ref):
        i = pl.program_id(1)
        lane, row = _lane_row((TB, 128))
        lo = lane < HD
        tri_gt = (row > lane).astype(BF16)
        qv = q_ref[...].astype(F32)
        qs = (jnp.where(lo, qv, 0.0).astype(BF16), jnp.where(lo, 0.0, qv).astype(BF16))
        qpos = i * TB + row

        def step(t, carry):
            acc, ca, cb = carry
            j = i - t
            j0 = pl.multiple_of(j * TB, TB)
            kj = k_ref[pl.ds(j0, TB), :]
            vj = v_ref[pl.ds(j0, TB), :]
            kpos = j * TB + lane
            valid = (kpos >= PAD) & (kpos < qpos)
            cs = [ca, cb]
            av = []
            for hh in range(2):
                _, _, lb, l1, sfx = _sb_tile(qs[hh], kj, valid, tri_gt)
                a = jnp.where(valid, jnp.exp(lb + sfx + cs[hh]), 0.0)
                av.append(_dot(a.astype(BF16), vj))
                c_ref[...] = jnp.where(lane == 2 * j + hh, cs[hh], c_ref[...])
                cs[hh] = cs[hh] + jnp.sum(l1, axis=1, keepdims=True)
            return acc + jnp.where(lo, av[0], av[1]), cs[0], cs[1]

        z1 = jnp.zeros((TB, 1), F32)
        c_ref[...] = jnp.zeros((TB, 128), F32)
        acc, _, _ = lax.fori_loop(0, i + 1, step, (jnp.zeros((TB, 128), F32), z1, z1))
        o_ref[...] = acc

    slab = pl.BlockSpec((R, 128), lambda p, i: (0, p))
    blk = pl.BlockSpec((TB, 128), lambda p, i: (i, p))
    sd = jax.ShapeDtypeStruct((R, D), F32)
    return pl.pallas_call(
        body, grid=(D // 128, NB), in_specs=[blk, slab, slab], out_specs=[blk, blk],
        out_shape=[sd, sd], name="sb_fwd",
        compiler_params=_cp(("parallel", "parallel")))(q, k, v)


def _sb_bwd(q, k, v, car, do):
    def body(q_ref, k_ref, v_ref, c_ref, do_ref, dq_ref, dk_ref, dv_ref):
        i = pl.program_id(1)

        @pl.when(i == 0)
        def _():
            dk_ref[...] = jnp.zeros_like(dk_ref)
            dv_ref[...] = jnp.zeros_like(dv_ref)

        lane, row = _lane_row((TB, 128))
        lo = lane < HD
        tri_gt = (row > lane).astype(BF16)
        tri_lt = (row < lane).astype(BF16)
        qv = q_ref[...].astype(F32)
        qs = (jnp.where(lo, qv, 0.0).astype(BF16), jnp.where(lo, 0.0, qv).astype(BF16))
        dov = do_ref[...]
        ct = c_ref[...]
        dos = (jnp.where(lo, dov, 0.0).astype(BF16), jnp.where(lo, 0.0, dov).astype(BF16))
        qpos = i * TB + row

        def step(j, carry):
            dq, ga, gb = carry
            j0 = pl.multiple_of(j * TB, TB)
            kj = k_ref[pl.ds(j0, TB), :]
            vj = v_ref[pl.ds(j0, TB), :]
            kpos = j * TB + lane
            valid = (kpos >= PAD) & (kpos < qpos)
            gs = [ga, gb]
            dqs, dks, dvs = [], [], []
            for hh in range(2):
                z, e, lb, l1, sfx = _sb_tile(qs[hh], kj, valid, tri_gt)
                later = jnp.sum(jnp.where(lane == 2 * j + hh, ct, 0.0), axis=1, keepdims=True)
                a = jnp.where(valid, jnp.exp(lb + sfx + later), 0.0)
                gmat = _dot_nt(dos[hh], vj) * a
                pre = gs[hh] + _split_dot(gmat, tri_lt)
                r = 1.0 / (1.0 + e)
                big = z >= 0.0
                beta = jnp.where(big, r, e * r)
                omb = jnp.where(big, e * r, r)
                dz = (jnp.where(valid, gmat * omb - beta * pre, 0.0) * SCALE).astype(BF16)
                dqs.append(_dot(dz, kj))
                dks.append(_dot_tn(dz, qs[hh]))
                dvs.append(_dot_tn(a.astype(BF16), dos[hh]))
                gs[hh] = gs[hh] + jnp.sum(gmat, axis=1, keepdims=True)
            dk_ref[pl.ds(j0, TB), :] += dks[0] + dks[1]
            dv_ref[pl.ds(j0, TB), :] += dvs[0] + dvs[1]
            return dq + jnp.where(lo, dqs[0], dqs[1]), gs[0], gs[1]

        z1 = jnp.zeros((TB, 1), F32)
        dq, _, _ = lax.fori_loop(0, i + 1, step, (jnp.zeros((TB, 128), F32), z1, z1))
        dq_ref[...] = dq

    slab = pl.BlockSpec((R, 128), lambda p, i: (0, p))
    blk = pl.BlockSpec((TB, 128), lambda p, i: (i, p))
    sd = jax.ShapeDtypeStruct((R, D), F32)
    return pl.pallas_call(
        body, grid=(D // 128, NB), in_specs=[blk, slab, slab, blk, blk], out_specs=[blk, slab, slab],
        out_shape=[sd, sd, sd], name="sb_bwd",
        compiler_params=_cp(("parallel", "arbitrary")))(q, k, v, car, do)


def _adamw(w, parts, m, v, name):
    rows, cols = w.shape
    tr = 256 if rows % 256 == 0 else rows
    nparts = len(parts)

    def body(*refs):
        w_ref = refs[0]
        p_refs = refs[1:1 + nparts]
        m_ref, v_ref, g_ref, d_ref, nm_ref, nv_ref = refs[1 + nparts:]
        g = p_refs[0][...]
        for p_ref in p_refs[1:]:
            g = g + p_ref[...]
        nm = ADAM_B1 * m_ref[...] + (1.0 - ADAM_B1) * g
        nv = ADAM_B2 * v_ref[...] + (1.0 - ADAM_B2) * (g * g)
        m_hat = nm / (1.0 - ADAM_B1 ** ADAM_STEP)
        v_hat = nv / (1.0 - ADAM_B2 ** ADAM_STEP)
        g_ref[...] = g
        d_ref[...] = -ADAM_LR * (m_hat / (jnp.sqrt(v_hat) + ADAM_EPS) + ADAM_WD * w_ref[...])
        nm_ref[...] = nm
        nv_ref[...] = nv

    blk = pl.BlockSpec((tr, cols), lambda i: (i, 0))
    sd = jax.ShapeDtypeStruct((rows, cols), F32)
    return pl.pallas_call(
        body, grid=(rows // tr,), in_specs=[blk] * (3 + nparts), out_specs=[blk] * 4, out_shape=[sd] * 4,
        name=name, compiler_params=_cp(("parallel",)))(w, *parts, m, v)


def _sum4(buf, name):
    _, rows, cols = buf.shape
    tr = 256 if rows % 256 == 0 else rows

    def body(b_ref, o_ref):
        o_ref[...] = ((b_ref[0] + b_ref[1]) + b_ref[2]) + b_ref[3]

    return pl.pallas_call(
        body, grid=(rows // tr,), in_specs=[pl.BlockSpec((4, tr, cols), lambda i: (0, i, 0))],
        out_specs=pl.BlockSpec((tr, cols), lambda i: (i, 0)), out_shape=jax.ShapeDtypeStruct((rows, cols), F32),
        name=name, compiler_params=_cp(("parallel",)))(buf)


def _sum8(buf, name):
    _, rows, cols = buf.shape

    def body(b_ref, o_ref):
        acc = b_ref[0]
        for i in range(1, 8):
            acc = acc + b_ref[i]
        o_ref[...] = acc

    return pl.pallas_call(
        body, out_shape=jax.ShapeDtypeStruct((rows, cols), F32), name=name,
        compiler_params=pltpu.CompilerParams(vmem_limit_bytes=VMEM_LIMIT))(buf)


MESH = pl.DeviceIdType.MESH
ANY = pl.BlockSpec(memory_space=pl.ANY)


def _chip_peers():
    x, y = lax.axis_index("x"), lax.axis_index("y")
    return [(1 - x, y), (x, 1 - y), (1 - x, 1 - y)]


def _gather_chips(shards):
    n = len(shards)

    def body(*refs):
        ins, outs = refs[:n], refs[n:2 * n]
        lsem, ssem, rsem = refs[2 * n:]
        x, y, c = lax.axis_index("x"), lax.axis_index("y"), lax.axis_index("c")
        me = 2 * x + y
        local = [pltpu.make_async_copy(ins[a], outs[a].at[me], lsem.at[a]) for a in range(n)]
        for cp in local:
            cp.start()
        sends = []
        for j, (px, py) in enumerate(_chip_peers()):
            for a in range(n):
                sends.append(pltpu.make_async_remote_copy(
                    src_ref=ins[a], dst_ref=outs[a].at[me], send_sem=ssem.at[j * n + a], recv_sem=rsem.at[j * n + a],
                    device_id=(px, py, c), device_id_type=MESH))
        for cp in sends:
            cp.start()
        for j, (px, py) in enumerate(_chip_peers()):
            for a in range(n):
                pltpu.make_async_remote_copy(
                    src_ref=ins[a], dst_ref=outs[a].at[2 * px + py], send_sem=ssem.at[j * n + a],
                    recv_sem=rsem.at[j * n + a], device_id=(px, py, c), device_id_type=MESH).wait_recv()
        for cp in sends:
            cp.wait_send()
        for cp in local:
            cp.wait()

    return pl.pallas_call(
        body, in_specs=[ANY] * n, out_specs=[ANY] * n,
        out_shape=[jax.ShapeDtypeStruct((N_CHIPS,) + s.shape, s.dtype) for s in shards],
        scratch_shapes=[pltpu.SemaphoreType.DMA((n,)), pltpu.SemaphoreType.DMA((3 * n,)),
                        pltpu.SemaphoreType.DMA((3 * n,))],
        name="gather_chips")(*shards)


def _scatter_chips(grads):
    n = len(grads)

    def body(*refs):
        ins, outs = refs[:n], refs[n:2 * n]
        lsem, ssem, rsem = refs[2 * n:]
        x, y, c = lax.axis_index("x"), lax.axis_index("y"), lax.axis_index("c")
        me = 2 * x + y
        local = [pltpu.make_async_copy(ins[a].at[me], outs[a].at[3], lsem.at[a]) for a in range(n)]
        for cp in local:
            cp.start()
        sends = []
        for j, (px, py) in enumerate(_chip_peers()):
            for a in range(n):
                sends.append(pltpu.make_async_remote_copy(
                    src_ref=ins[a].at[2 * px + py], dst_ref=outs[a].at[j], send_sem=ssem.at[j * n + a],
                    recv_sem=rsem.at[j * n + a], device_id=(px, py, c), device_id_type=MESH))
        for cp in sends:
            cp.start()
        for cp in sends:
            cp.wait()
        for cp in local:
            cp.wait()

    return pl.pallas_call(
        body, in_specs=[ANY] * n, out_specs=[ANY] * n,
        out_shape=[jax.ShapeDtypeStruct(g.shape, g.dtype) for g in grads],
        scratch_shapes=[pltpu.SemaphoreType.DMA((n,)), pltpu.SemaphoreType.DMA((3 * n,)),
                        pltpu.SemaphoreType.DMA((3 * n,))],
        name="scatter_chips")(*grads)


def _swap_cores(parts):
    n = len(parts)

    def body(*refs):
        ins, outs = refs[:n], refs[n:2 * n]
        ssem, rsem = refs[2 * n:]
        x, y, c = lax.axis_index("x"), lax.axis_index("y"), lax.axis_index("c")
        cps = [pltpu.make_async_remote_copy(
            src_ref=ins[a], dst_ref=outs[a], send_sem=ssem.at[a], recv_sem=rsem.at[a],
            device_id=(x, y, 1 - c), device_id_type=MESH) for a in range(n)]
        for cp in cps:
            cp.start()
        for cp in cps:
            cp.wait()

    return pl.pallas_call(
        body, in_specs=[ANY] * n, out_specs=[ANY] * n,
        out_shape=[jax.ShapeDtypeStruct(p.shape, p.dtype) for p in parts],
        scratch_shapes=[pltpu.SemaphoreType.DMA((n,)), pltpu.SemaphoreType.DMA((n,))],
        name="swap_cores")(*parts)


def _gather_all(vec):
    def body(v_ref, o_ref, lsem, ssem, rsem):
        x, y, c = lax.axis_index("x"), lax.axis_index("y"), lax.axis_index("c")
        me = 4 * x + 2 * y + c
        local = pltpu.make_async_copy(v_ref, o_ref.at[me], lsem)
        local.start()
        cps = []
        for k in range(1, 8):
            px, py, pc = x ^ (k >> 2), y ^ ((k >> 1) & 1), c ^ (k & 1)
            cps.append(pltpu.make_async_remote_copy(
                src_ref=v_ref, dst_ref=o_ref.at[me], send_sem=ssem.at[k - 1], recv_sem=rsem.at[k - 1],
                device_id=(px, py, pc), device_id_type=MESH))
        for cp in cps:
            cp.start()
        for k in range(1, 8):
            px, py, pc = x ^ (k >> 2), y ^ ((k >> 1) & 1), c ^ (k & 1)
            pltpu.make_async_remote_copy(
                src_ref=v_ref, dst_ref=o_ref.at[4 * px + 2 * py + pc], send_sem=ssem.at[k - 1],
                recv_sem=rsem.at[k - 1], device_id=(px, py, pc), device_id_type=MESH).wait_recv()
        for cp in cps:
            cp.wait_send()
        local.wait()

    return pl.pallas_call(
        body, in_specs=[ANY], out_specs=ANY, out_shape=jax.ShapeDtypeStruct((8,) + vec.shape, vec.dtype),
        scratch_shapes=[pltpu.SemaphoreType.DMA, pltpu.SemaphoreType.DMA((7,)), pltpu.SemaphoreType.DMA((7,))],
        name="gather_all")(vec)


def _rope_tables():
    pos = (jnp.arange(R, dtype=jnp.int32) - PAD).astype(F32)
    half = HD // 2
    inv = ROPE_THETA ** (-jnp.arange(half, dtype=F32) / half)
    ang = pos[:, None] * inv[None, :]
    cos, sin = jnp.cos(ang), jnp.sin(ang)
    cs = jnp.tile(cos, (1, 4))
    sn = jnp.tile(jnp.concatenate([-sin, sin], axis=1), (1, 2))
    return cs, sn


def _perm_cols(w):
    return jnp.concatenate([w[:, 0:512], w[:, 768:1280], w[:, 1280:2304], w[:, 2304:2816], w[:, 512:640],
                            w[:, 640:768]], axis=1)


def _unperm_cols(w):
    return jnp.concatenate([w[:, C_Q:C_Q + 512], w[:, C_K:C_K + 128], w[:, C_V:C_V + 128], w[:, C_GA:C_GA + 512],
                            w[:, C_GLU:C_GLU + 1024], w[:, C_GB:C_GB + 512]], axis=1)


def _local_step(x, target, p):
    w0 = _perm_cols(p["ab_w_in"])
    wo0, wpw, wo1 = p["ab_w_out"], p["ab_w_pw2"], p["sb_w_out"]
    w1 = p["sb_w_in"]
    conv_w = jnp.concatenate([p["ab_conv_w"], jnp.zeros((1, CC), F32)], axis=0)
    cs, sn = _rope_tables()

    h0 = jnp.concatenate([jnp.zeros((PAD, D), F32), p["meta_tokens"], x], axis=0)

    xn0 = _rms_fwd(h0, p["ab_pre_norm"], "rms_fwd0")
    z0 = _mm([(xn0, w0)], F32, "in_proj0", 544, 256)
    o0, lse0 = _swa_fwd(z0, cs, sn, p["ab_sinks"])
    a0 = _gate_fwd(o0, z0, C_GA, "gate_a_fwd")
    cv0, s0 = _conv_fwd(z0, conv_w, p["ab_conv_b"], p["ab_conv_ln_g"], p["ab_conv_ln_b"])
    t0 = _mm([(s0, wpw)], F32, "pw2", 544, 512)
    c0 = _gate_fwd(t0, z0, C_GB, "gate_b_fwd")
    mix0 = jnp.concatenate([a0, c0], axis=1)
    y0 = _mm([(mix0, wo0)], F32, "out_proj0", 544, 512)
    h1 = _post_fwd(h0, y0, p["ab_post_norm"], "post_fwd0")

    xn1 = _rms_fwd(h1, p["sb_pre_norm"], "rms_fwd1")
    q1 = _mm([(xn1, w1[0])], BF16, "in_proj1_q", 544, 512)
    k1 = _mm([(xn1, w1[1])], BF16, "in_proj1_k", 544, 512)
    v1 = _mm([(xn1, w1[2])], BF16, "in_proj1_v", 544, 512)
    g1 = _mm([(xn1, w1[3])], F32, "in_proj1_g", 544, 512)
    o1, car1 = _sb_fwd(q1, k1, v1)
    m1 = _gate_fwd(o1, g1, 0, "gate_sb_fwd")
    y1 = _mm([(m1, wo1)], F32, "out_proj1", 544, 512)
    h2 = _post_fwd(h1, y1, p["sb_post_norm"], "post_fwd1")

    dh2, loss_row = _loss_bwd(h2, target)

    dy1, d_sb_post = _rms_bwd(dh2, y1, p["sb_post_norm"], None, BF16, "post_bwd1")
    dm1 = _mm([(dy1, wo1.T)], F32, "out_proj1_dx", 544, 512)
    d_wo1 = _mm([(m1.T, dy1)], F32, "out_proj1_dw", 512, 512)
    do1, dg1 = _gate_bwd(dm1, 0, o1, g1, 0, "gate_sb_bwd")
    dq1, dk1, dv1 = _sb_bwd(q1, k1, v1, car1, do1)
    dz1 = [dq1, dk1, dv1, dg1]
    dxn1 = _mm([(dz1[j], w1[j].T) for j in range(4)], F32, "in_proj1_dx", 544, 512)
    xn1t = xn1.T
    d_w1 = jnp.stack([_mm([(xn1t, dz1[j])], F32, "in_proj1_dw%d" % j, 512, 512) for j in range(4)])
    dh1, d_sb_pre = _rms_bwd(dxn1, h1, p["sb_pre_norm"], dh2, F32, "rms_bwd1")

    dy0, d_ab_post = _rms_bwd(dh1, y0, p["ab_post_norm"], None, BF16, "post_bwd0")
    dmix0 = _mm([(dy0, wo0.T)], F32, "out_proj0_dx", 544, 512)
    d_wo0 = _mm([(mix0.T, dy0)], F32, "out_proj0_dw", 512, 512)
    dt0, dgb0 = _gate_bwd(dmix0, 512, t0, z0, C_GB, "gate_b_bwd")
    ds0 = _mm([(dt0, wpw.T)], F32, "pw2_dx", 544, 512)
    d_wpw = _mm([(s0.T, dt0)], F32, "pw2_dw", 512, 512)
    dglu0, d_convw, d_small = _conv_bwd(ds0, cv0, z0, conv_w, p["ab_conv_ln_g"], p["ab_conv_ln_b"])
    do0, dga0 = _gate_bwd(dmix0, 0, o0, z0, C_GA, "gate_a_bwd")
    dq0, dkv0, d_sinks = _swa_bwd(z0, cs, sn, p["ab_sinks"], o0, do0, lse0)
    dz0 = jnp.concatenate([dq0.astype(BF16), dga0, dglu0, dgb0, dkv0.astype(BF16)], axis=1)
    dxn0 = _mm([(dz0, w0.T)], F32, "in_proj0_dx", 544, 512)
    d_w0 = _unperm_cols(_mm([(xn0.T, dz0)], F32, "in_proj0_dw", 512, 256))
    dh0, d_ab_pre = _rms_bwd(dxn0, h0, p["ab_pre_norm"], dh1, F32, "rms_bwd0")

    grads = {
        "meta_tokens": dh0[PAD:TB], "ab_pre_norm": d_ab_pre, "ab_w_in": d_w0, "ab_sinks": d_sinks[0:1, 0:8],
        "ab_conv_w": d_convw[0:CONV_W], "ab_conv_b": d_small[0:1], "ab_conv_ln_g": d_small[1:2],
        "ab_conv_ln_b": d_small[2:3], "ab_w_pw2": d_wpw, "ab_w_out": d_wo0, "ab_post_norm": d_ab_post,
        "sb_pre_norm": d_sb_pre, "sb_w_in": d_w1, "sb_w_out": d_wo1, "sb_post_norm": d_sb_post,
    }
    return loss_row, dh0[TB:], grads


SMALL_ROWS = 72
REP_ROWS = 32

WEIGHTS = ["meta_tokens", "ab_pre_norm", "ab_w_in", "ab_sinks", "ab_conv_w", "ab_conv_b", "ab_conv_ln_g",
           "ab_conv_ln_b", "ab_w_pw2", "ab_w_out", "ab_post_norm", "sb_pre_norm", "sb_w_in", "sb_w_out",
           "sb_post_norm"]
BIG = ["ab_w_in", "ab_w_out", "ab_w_pw2", "sb_w_in", "sb_w_out"]


def _pack_small(conv_w, meta, sb_pre, sb_post):
    rows = jnp.concatenate([conv_w, meta.reshape(32, 128), sb_pre.reshape(2, 128), sb_post.reshape(2, 128)], axis=0)
    return jnp.concatenate([rows, jnp.zeros((SMALL_ROWS - rows.shape[0], 128), F32)], axis=0)


def _unpack_small(s):
    return s[0:31], s[31:63].reshape(16, 256), s[63:65].reshape(1, 256), s[65:67].reshape(1, 256)


def _pack_rep(pre, post, conv_b, ln_g, ln_b, sinks):
    flat = jnp.concatenate([pre.reshape(-1), post.reshape(-1), conv_b.reshape(-1), ln_g.reshape(-1),
                            ln_b.reshape(-1), sinks.reshape(-1)])
    flat = jnp.concatenate([flat, jnp.zeros((REP_ROWS * 128 - flat.shape[0],), F32)])
    return flat.reshape(REP_ROWS, 128)


def _unpack_rep(r):
    f = r.reshape(-1)
    return (f[0:1024].reshape(1, 1024), f[1024:2048].reshape(1, 1024), f[2048:2560].reshape(1, 512),
            f[2560:3072].reshape(1, 512), f[3072:3584].reshape(1, 512), f[3584:3592].reshape(1, 8))


def _cols_to_chips(w, width):
    return w.reshape(w.shape[0], N_CHIPS, width).transpose(1, 0, 2)


def _chips_to_cols(w):
    return w.transpose(1, 0, 2).reshape(w.shape[1], -1)


def kernel(x, meta_tokens, ab_pre_norm, ab_w_in, ab_sinks, ab_conv_w, ab_conv_b, ab_conv_ln_g, ab_conv_ln_b, ab_w_pw2, ab_w_out, ab_post_norm, sb_pre_norm, sb_w_in, sb_w_out, sb_post_norm, loss_target, m_meta_tokens, m_ab_pre_norm, m_ab_w_in, m_ab_sinks, m_ab_conv_w, m_ab_conv_b, m_ab_conv_ln_g, m_ab_conv_ln_b, m_ab_w_pw2, m_ab_w_out, m_ab_post_norm, m_sb_pre_norm, m_sb_w_in, m_sb_w_out, m_sb_post_norm, v_meta_tokens, v_ab_pre_norm, v_ab_w_in, v_ab_sinks, v_ab_conv_w, v_ab_conv_b, v_ab_conv_ln_g, v_ab_conv_ln_b, v_ab_w_pw2, v_ab_w_out, v_ab_post_norm, v_sb_pre_norm, v_sb_w_in, v_sb_w_out, v_sb_post_norm):
    w = dict(meta_tokens=meta_tokens, ab_pre_norm=ab_pre_norm, ab_w_in=ab_w_in, ab_sinks=ab_sinks,
             ab_conv_w=ab_conv_w, ab_conv_b=ab_conv_b, ab_conv_ln_g=ab_conv_ln_g, ab_conv_ln_b=ab_conv_ln_b,
             ab_w_pw2=ab_w_pw2, ab_w_out=ab_w_out, ab_post_norm=ab_post_norm, sb_pre_norm=sb_pre_norm,
             sb_w_in=sb_w_in, sb_w_out=sb_w_out, sb_post_norm=sb_post_norm)
    m = dict(meta_tokens=m_meta_tokens, ab_pre_norm=m_ab_pre_norm, ab_w_in=m_ab_w_in, ab_sinks=m_ab_sinks,
             ab_conv_w=m_ab_conv_w, ab_conv_b=m_ab_conv_b, ab_conv_ln_g=m_ab_conv_ln_g,
             ab_conv_ln_b=m_ab_conv_ln_b, ab_w_pw2=m_ab_w_pw2, ab_w_out=m_ab_w_out, ab_post_norm=m_ab_post_norm,
             sb_pre_norm=m_sb_pre_norm, sb_w_in=m_sb_w_in, sb_w_out=m_sb_w_out, sb_post_norm=m_sb_post_norm)
    v = dict(meta_tokens=v_meta_tokens, ab_pre_norm=v_ab_pre_norm, ab_w_in=v_ab_w_in, ab_sinks=v_ab_sinks,
             ab_conv_w=v_ab_conv_w, ab_conv_b=v_ab_conv_b, ab_conv_ln_g=v_ab_conv_ln_g,
             ab_conv_ln_b=v_ab_conv_ln_b, ab_w_pw2=v_ab_w_pw2, ab_w_out=v_ab_w_out, ab_post_norm=v_ab_post_norm,
             sb_pre_norm=v_sb_pre_norm, sb_w_in=v_sb_w_in, sb_w_out=v_sb_w_out, sb_post_norm=v_sb_post_norm)

    def small_of(d):
        return _pack_small(d["ab_conv_w"][0], d["meta_tokens"], d["sb_pre_norm"], d["sb_post_norm"])

    def rep_of(d):
        return _pack_rep(d["ab_pre_norm"], d["ab_post_norm"], d["ab_conv_b"], d["ab_conv_ln_g"], d["ab_conv_ln_b"],
                         d["ab_sinks"])

    gathered = _gather_chips([w[k][0].astype(BF16) for k in BIG] + [small_of(w)])
    g_in0, g_out0, g_pw2, g_in1, g_out1, g_small = gathered
    conv_w_f = _chips_to_cols(g_small[:, 0:31])
    meta_f = _chips_to_cols(g_small[:, 31:63].reshape(N_CHIPS, 16, 256))
    sb_pre_f = g_small[:, 63:65].reshape(1, D)
    sb_post_f = g_small[:, 65:67].reshape(1, D)
    full = {
        "meta_tokens": meta_f, "ab_pre_norm": ab_pre_norm, "ab_w_in": _chips_to_cols(g_in0),
        "ab_sinks": ab_sinks, "ab_conv_w": conv_w_f, "ab_conv_b": ab_conv_b, "ab_conv_ln_g": ab_conv_ln_g,
        "ab_conv_ln_b": ab_conv_ln_b, "ab_w_pw2": g_pw2.reshape(CC, CC), "ab_w_out": g_out0.reshape(D, D),
        "ab_post_norm": ab_post_norm, "sb_pre_norm": sb_pre_f, "sb_w_in": g_in1, "sb_w_out": g_out1.reshape(D, D),
        "sb_post_norm": sb_post_f,
    }

    loss_row, grad_x, g = _local_step(x[0], loss_target[0], full)
    loss = lax.psum(loss_row[0, 0], ("x", "y", "c"))

    send = [_cols_to_chips(g["ab_w_in"], 704), g["ab_w_out"].reshape(N_CHIPS, 256, D),
            g["ab_w_pw2"].reshape(N_CHIPS, 128, CC), g["sb_w_in"], g["sb_w_out"].reshape(N_CHIPS, 256, D)]
    gs_conv = _cols_to_chips(g["ab_conv_w"], 128)
    gs_meta = _cols_to_chips(g["meta_tokens"], 256)
    gs_pre = g["sb_pre_norm"].reshape(N_CHIPS, 1, 256)
    gs_post = g["sb_post_norm"].reshape(N_CHIPS, 1, 256)
    send.append(jnp.stack([_pack_small(gs_conv[j], gs_meta[j], gs_pre[j], gs_post[j]) for j in range(N_CHIPS)]))
    landed = _scatter_chips(send)
    names = BIG + ["small"]
    part = [_sum4(b, "sum4_" + nm) for b, nm in zip(landed, names)]
    other = _swap_cores(part)

    rep_g = _pack_rep(g["ab_pre_norm"], g["ab_post_norm"], g["ab_conv_b"], g["ab_conv_ln_g"], g["ab_conv_ln_b"],
                      g["ab_sinks"])
    rep_sum = _sum8(_gather_all(rep_g), "sum8_rep")

    out_g, out_d, out_m, out_v = {}, {}, {}, {}
    for i, k in enumerate(BIG):
        shp = w[k].shape
        res = _adamw(w[k][0], [part[i], other[i]], m[k][0], v[k][0], "adamw_" + k)
        out_g[k], out_d[k], out_m[k], out_v[k] = [r.reshape(shp) for r in res]
    res = _adamw(small_of(w), [part[5], other[5]], small_of(m), small_of(v), "adamw_small")
    for dst, r in zip((out_g, out_d, out_m, out_v), res):
        cw, mt, pre, post = _unpack_small(r)
        dst["ab_conv_w"], dst["meta_tokens"], dst["sb_pre_norm"], dst["sb_post_norm"] = cw[None], mt, pre, post
    res = _adamw(rep_of(w), [rep_sum], rep_of(m), rep_of(v), "adamw_rep")
    for dst, r in zip((out_g, out_d, out_m, out_v), res):
        (dst["ab_pre_norm"], dst["ab_post_norm"], dst["ab_conv_b"], dst["ab_conv_ln_g"], dst["ab_conv_ln_b"],
         dst["ab_sinks"]) = _unpack_rep(r)

    return (loss, grad_x[None], *[out_g[k] for k in WEIGHTS], *[out_d[k] for k in WEIGHTS],
            *[out_m[k] for k in WEIGHTS], *[out_v[k] for k in WEIGHTS])
```

```python
import functools

import jax
import jax.numpy as jnp
from jax import lax
from jax.experimental import pallas as pl
from jax.experimental.pallas import tpu as pltpu

F32 = jnp.float32
BF16 = jnp.bfloat16

D = 1024
SEQ = 2048
N_META = 16
TB = 128
PAD = TB - N_META
R = SEQ + TB
NB = R // TB
HD = 64
ROPE_THETA = 10000.0
NORM_EPS = 1e-6
LN_EPS = 1e-5
NEG = -1e30
SWA_HEADS = 8
CONV_W = 31
SCALE = HD ** -0.5
N_CHIPS = 4

C_Q, C_GA, C_GLU, C_GB, C_K, C_V = 0, 512, 1024, 2048, 2560, 2688
AB_IN = 2816

ADAM_LR, ADAM_B1, ADAM_B2, ADAM_EPS, ADAM_WD, ADAM_STEP = 0.001, 0.9, 0.999, 1e-08, 0.01, 10

VMEM_LIMIT = 56 * 1024 * 1024


def _cp(sem):
    return pltpu.CompilerParams(dimension_semantics=sem, vmem_limit_bytes=VMEM_LIMIT)


def _sig(x):
    return 1.0 / (1.0 + jnp.exp(-x))


def _dot(a, b):
    return lax.dot_general(a, b, (((1,), (0,)), ((), ())), preferred_element_type=F32)


def _dot_nt(a, b):
    return lax.dot_general(a, b, (((1,), (1,)), ((), ())), preferred_element_type=F32)


def _dot_tn(a, b):
    return lax.dot_general(a, b, (((0,), (0,)), ((), ())), preferred_element_type=F32)


def _mm(pairs, out_dtype, name, tm, tn):
    m, n = pairs[0][0].shape[0], pairs[0][1].shape[1]
    npairs = len(pairs)

    def body(*refs):
        o_ref = refs[2 * npairs]
        acc = None
        for i in range(npairs):
            t = _dot(refs[2 * i][...].astype(BF16), refs[2 * i + 1][...].astype(BF16))
            acc = t if acc is None else acc + t
        o_ref[...] = acc.astype(out_dtype)

    in_specs, args = [], []
    for a, b in pairs:
        k = a.shape[1]
        in_specs += [pl.BlockSpec((tm, k), lambda i, j: (i, 0)), pl.BlockSpec((k, tn), lambda i, j: (0, j))]
        args += [a, b]
    return pl.pallas_call(
        body, grid=(m // tm, n // tn), in_specs=in_specs,
        out_specs=pl.BlockSpec((tm, tn), lambda i, j: (i, j)),
        out_shape=jax.ShapeDtypeStruct((m, n), out_dtype), name=name,
        compiler_params=_cp(("parallel", "parallel")))(*args)


def _rms_fwd(h, g, name):
    def body(h_ref, g_ref, o_ref):
        x = h_ref[...]
        r = lax.rsqrt(jnp.mean(x * x, axis=1, keepdims=True) + NORM_EPS)
        o_ref[...] = (x * r * g_ref[...]).astype(BF16)

    return pl.pallas_call(
        body, grid=(NB,),
        in_specs=[pl.BlockSpec((TB, D), lambda n: (n, 0)), pl.BlockSpec((1, D), lambda n: (0, 0))],
        out_specs=pl.BlockSpec((TB, D), lambda n: (n, 0)),
        out_shape=jax.ShapeDtypeStruct((R, D), BF16), name=name, compiler_params=_cp(("parallel",)))(h, g)


def _post_fwd(h, y, g, name):
    def body(h_ref, y_ref, g_ref, o_ref):
        yv = y_ref[...]
        r = lax.rsqrt(jnp.mean(yv * yv, axis=1, keepdims=True) + NORM_EPS)
        o_ref[...] = h_ref[...] + yv * r * g_ref[...]

    blk = pl.BlockSpec((TB, D), lambda n: (n, 0))
    return pl.pallas_call(
        body, grid=(NB,), in_specs=[blk, blk, pl.BlockSpec((1, D), lambda n: (0, 0))], out_specs=blk,
        out_shape=jax.ShapeDtypeStruct((R, D), F32), name=name, compiler_params=_cp(("parallel",)))(h, y, g)


def _rms_bwd(dout, x, g, res, out_dtype, name):
    has_res = res is not None

    def body(*refs):
        if has_res:
            d_ref, x_ref, g_ref, r_ref, dx_ref, dg_ref = refs
        else:
            d_ref, x_ref, g_ref, dx_ref, dg_ref = refs
        n = pl.program_id(0)
        xv = x_ref[...]
        dv = d_ref[...]
        r = lax.rsqrt(jnp.mean(xv * xv, axis=1, keepdims=True) + NORM_EPS)
        xh = xv * r
        dxh = dv * g_ref[...]
        dx = r * (dxh - xh * jnp.mean(dxh * xh, axis=1, keepdims=True))
        if has_res:
            dx = dx + r_ref[...]
        row = lax.broadcasted_iota(jnp.int32, (TB, D), 0) + n * TB
        dx_ref[...] = jnp.where(row >= PAD, dx, 0.0).astype(out_dtype)

        @pl.when(n == 0)
        def _():
            dg_ref[...] = jnp.zeros_like(dg_ref)

        dg_ref[...] += jnp.sum(dv * xh, axis=0, keepdims=True)

    blk = pl.BlockSpec((TB, D), lambda n: (n, 0))
    vec = pl.BlockSpec((1, D), lambda n: (0, 0))
    ins = [dout, x, g] + ([res] if has_res else [])
    in_specs = [blk, blk, vec] + ([blk] if has_res else [])
    return pl.pallas_call(
        body, grid=(NB,), in_specs=in_specs, out_specs=[blk, vec],
        out_shape=[jax.ShapeDtypeStruct((R, D), out_dtype), jax.ShapeDtypeStruct((1, D), F32)],
        name=name, compiler_params=_cp(("arbitrary",)))(*ins)


GW = 512


def _gate_fwd(o, gsrc, goff, name):
    w = o.shape[1]

    def body(o_ref, g_ref, m_ref):
        gv = g_ref[...]
        m_ref[...] = (o_ref[...] * (gv * _sig(gv))).astype(BF16)

    gb = goff // GW
    return pl.pallas_call(
        body, grid=(NB, w // GW),
        in_specs=[pl.BlockSpec((TB, GW), lambda n, j: (n, j)), pl.BlockSpec((TB, GW), lambda n, j: (n, gb + j))],
        out_specs=pl.BlockSpec((TB, GW), lambda n, j: (n, j)),
        out_shape=jax.ShapeDtypeStruct((R, w), BF16), name=name,
        compiler_params=_cp(("parallel", "parallel")))(o, gsrc)


def _gate_bwd(dsrc, doff, o, gsrc, goff, name):
    w = o.shape[1]

    def body(d_ref, o_ref, g_ref, do_ref, dg_ref):
        gv = g_ref[...]
        dv = d_ref[...]
        s = _sig(gv)
        do_ref[...] = dv * (gv * s)
        dg_ref[...] = (dv * o_ref[...] * (s * (1.0 + gv * (1.0 - s)))).astype(BF16)

    db, gb = doff // GW, goff // GW
    blk = pl.BlockSpec((TB, GW), lambda n, j: (n, j))
    return pl.pallas_call(
        body, grid=(NB, w // GW),
        in_specs=[pl.BlockSpec((TB, GW), lambda n, j: (n, db + j)), blk,
                  pl.BlockSpec((TB, GW), lambda n, j: (n, gb + j))],
        out_specs=[blk, blk],
        out_shape=[jax.ShapeDtypeStruct((R, w), F32), jax.ShapeDtypeStruct((R, w), BF16)], name=name,
        compiler_params=_cp(("parallel", "parallel")))(dsrc, o, gsrc)


def _loss_bwd(h, target):
    def body(h_ref, t_ref, d_ref, l_ref):
        n = pl.program_id(0)

        @pl.when(n == 0)
        def _():
            d_ref[...] = jnp.zeros_like(d_ref)
            l_ref[...] = jnp.zeros_like(l_ref)

        @pl.when(n > 0)
        def _():
            err = h_ref[...] - t_ref[...]
            d_ref[...] = err * (1.0 / D)
            l_ref[...] += jnp.sum(err * err, axis=0, keepdims=True)

        @pl.when(n == NB - 1)
        def _():
            tot = jnp.sum(l_ref[...], axis=1, keepdims=True) * (0.5 / D)
            l_ref[...] = jnp.broadcast_to(tot, (1, D))

    blk = pl.BlockSpec((TB, D), lambda n: (n, 0))
    return pl.pallas_call(
        body, grid=(NB,),
        in_specs=[blk, pl.BlockSpec((TB, D), lambda n: (jnp.maximum(n - 1, 0), 0))],
        out_specs=[blk, pl.BlockSpec((1, D), lambda n: (0, 0))],
        out_shape=[jax.ShapeDtypeStruct((R, D), F32), jax.ShapeDtypeStruct((1, D), F32)],
        name="loss_bwd", compiler_params=_cp(("arbitrary",)))(h, target)


def _lane_row(shape):
    return lax.broadcasted_iota(jnp.int32, shape, 1), lax.broadcasted_iota(jnp.int32, shape, 0)


def _rot_half(x, lane):
    return jnp.where(lane % HD < HD // 2, pltpu.roll(x, 128 - HD // 2, 1), pltpu.roll(x, HD // 2, 1))


def _swa_blocks(n):
    return (0, jnp.maximum(n - 1, 0), n)


def _swa_masks(n, lane, row):
    qpos = n * TB + row
    kp = (n - 1) * TB + lane
    kc = n * TB + lane
    m0 = (lane >= PAD) & (qpos - lane >= TB)
    mp = (kp >= PAD) & (qpos >= kp) & (qpos - kp < TB)
    mc = (kc >= PAD) & (qpos >= kc)
    return (m0, mp, mc)


def _swa_load(n, zq_ref, zkv_ref, cs_ref, sn_ref, lane):
    r0 = pl.multiple_of(n * TB, TB)
    csq, snq = cs_ref[pl.ds(r0, TB), :], sn_ref[pl.ds(r0, TB), :]
    qc = []
    for c in range(4):
        x = zq_ref[:, c * 128:(c + 1) * 128]
        qc.append(x * csq + _rot_half(x, lane) * snq)
    kvs = []
    for b in _swa_blocks(n):
        b0 = pl.multiple_of(b * TB, TB)
        csb, snb = cs_ref[pl.ds(b0, TB), :], sn_ref[pl.ds(b0, TB), :]
        kx = zkv_ref[pl.ds(b0, TB), 0:128]
        kr = kx * csb + _rot_half(kx, lane) * snb
        vx = zkv_ref[pl.ds(b0, TB), 128:256]
        kvs.append((kr.astype(BF16), pltpu.roll(kr, HD, 1).astype(BF16),
                    vx.astype(BF16), pltpu.roll(vx, HD, 1).astype(BF16), csb, snb, b0))
    return qc, (csq, snq), kvs


def _swa_fwd(z0, cs, sn, sinks):
    def body(zq_ref, zkv_ref, cs_ref, sn_ref, sk_ref, o_ref, lse_ref):
        n = pl.program_id(0)
        lane, row = _lane_row((TB, 128))
        lo = lane < HD
        masks = _swa_masks(n, lane, row)
        qc, _, kvs = _swa_load(n, zq_ref, zkv_ref, cs_ref, sn_ref, lane)
        oh = []
        lse_t = jnp.zeros((TB, 128), F32)
        for h in range(SWA_HEADS):
            c, par, g = h // 2, h % 2, h // 4
            half = lo if par == 0 else jnp.logical_not(lo)
            qm = jnp.where(half, qc[c], 0.0).astype(BF16)
            sink = sk_ref[0, h]
            ss = []
            for (k, ka, v, va, _, _, _), m in zip(kvs, masks):
                ksel = k if par == g else ka
                ss.append(jnp.where(m, _dot_nt(qm, ksel) * SCALE, NEG))
            mx = jnp.maximum(jnp.maximum(jnp.max(ss[0], axis=1, keepdims=True), jnp.max(ss[1], axis=1, keepdims=True)),
                             jnp.max(ss[2], axis=1, keepdims=True))
            mx = jnp.maximum(mx, sink)
            es = [jnp.exp(s - mx) for s in ss]
            den = (jnp.sum(es[0], axis=1, keepdims=True) + jnp.sum(es[1], axis=1, keepdims=True)
                   + jnp.sum(es[2], axis=1, keepdims=True) + jnp.exp(sink - mx))
            inv = 1.0 / den
            t = jnp.zeros((TB, 128), F32)
            for (k, ka, v, va, _, _, _), e in zip(kvs, es):
                t = t + _dot((e * inv).astype(BF16), v if par == g else va)
            oh.append(t)
            lse_t = jnp.where(lane == h, mx + jnp.log(den), lse_t)
        oc = [jnp.where(lo, oh[2 * c], oh[2 * c + 1]) for c in range(4)]
        for c in range(4):
            o_ref[:, c * 128:(c + 1) * 128] = oc[c]
        lse_ref[...] = lse_t

    full = pl.BlockSpec((R, 128), lambda n: (0, 0))
    return pl.pallas_call(
        body, grid=(NB,),
        in_specs=[pl.BlockSpec((TB, 512), lambda n: (n, C_Q // 512)),
                  pl.BlockSpec((R, 256), lambda n: (0, C_K // 256)), full, full,
                  pl.BlockSpec(memory_space=pltpu.SMEM)],
        out_specs=[pl.BlockSpec((TB, 512), lambda n: (n, 0)), pl.BlockSpec((TB, 128), lambda n: (n, 0))],
        out_shape=[jax.ShapeDtypeStruct((R, 512), F32), jax.ShapeDtypeStruct((R, 128), F32)],
        name="swa_fwd", compiler_params=_cp(("parallel",)))(z0, z0, cs, sn, sinks)


def _swa_bwd(z0, cs, sn, sinks, o, do, lse):
    def body(zq_ref, zkv_ref, cs_ref, sn_ref, sk_ref, o_ref, do_ref, lse_ref, dq_ref, dkv_ref, dsk_ref):
        n = pl.program_id(0)

        @pl.when(n == 0)
        def _():
            dkv_ref[...] = jnp.zeros_like(dkv_ref)
            dsk_ref[...] = jnp.zeros_like(dsk_ref)

        lane, row = _lane_row((TB, 128))
        lo = lane < HD
        masks = _swa_masks(n, lane, row)
        qc, (csq, snq), kvs = _swa_load(n, zq_ref, zkv_ref, cs_ref, sn_ref, lane)
        lse_t = lse_ref[...]
        dqh = []
        dk_al = [jnp.zeros((TB, 128), F32) for _ in range(3)]
        dk_mis = [jnp.zeros((TB, 128), F32) for _ in range(3)]
        dv_al = [jnp.zeros((TB, 128), F32) for _ in range(3)]
        dv_mis = [jnp.zeros((TB, 128), F32) for _ in range(3)]
        dsk_t = jnp.zeros((TB, 128), F32)
        for h in range(SWA_HEADS):
            c, par, g = h // 2, h % 2, h // 4
            half = lo if par == 0 else jnp.logical_not(lo)
            qm = jnp.where(half, qc[c], 0.0).astype(BF16)
            dov = do_ref[:, c * 128:(c + 1) * 128]
            dom = jnp.where(half, dov, 0.0)
            delta = jnp.sum(dom * o_ref[:, c * 128:(c + 1) * 128], axis=1, keepdims=True)
            dob = dom.astype(BF16)
            lse_h = jnp.sum(jnp.where(lane == h, lse_t, 0.0), axis=1, keepdims=True)
            sink = sk_ref[0, h]
            dqt = jnp.zeros((TB, 128), F32)
            for bi, ((k, ka, v, va, _, _, _), m) in enumerate(zip(kvs, masks)):
                ksel = k if par == g else ka
                vsel = v if par == g else va
                s = jnp.where(m, _dot_nt(qm, ksel) * SCALE, NEG)
                p = jnp.exp(s - lse_h)
                dp = _dot_nt(dob, vsel)
                ds = (p * (dp - delta) * SCALE).astype(BF16)
                dqt = dqt + _dot(ds, ksel)
                dkh = _dot_tn(ds, qm)
                dvh = _dot_tn(p.astype(BF16), dob)
                if par == g:
                    dk_al[bi] = dk_al[bi] + dkh
                    dv_al[bi] = dv_al[bi] + dvh
                else:
                    dk_mis[bi] = dk_mis[bi] + dkh
                    dv_mis[bi] = dv_mis[bi] + dvh
            dqh.append(dqt)
            dsk_t = jnp.where(lane == h, -jnp.exp(sink - lse_h) * delta, dsk_t)
        dqc = [jnp.where(lo, dqh[2 * c], dqh[2 * c + 1]) for c in range(4)]
        for c in range(4):
            t = dqc[c] * snq
            dq_ref[:, c * 128:(c + 1) * 128] = dqc[c] * csq + _rot_half(t, lane)
        for bi, (_, _, _, _, csb, snb, b0) in enumerate(kvs):
            dk = dk_al[bi] + pltpu.roll(dk_mis[bi], HD, 1)
            dv = dv_al[bi] + pltpu.roll(dv_mis[bi], HD, 1)
            dkv_ref[pl.ds(b0, TB), 0:128] += dk * csb + _rot_half(dk * snb, lane)
            dkv_ref[pl.ds(b0, TB), 128:256] += dv
        dsk_ref[0:1, :] += jnp.sum(dsk_t, axis=0, keepdims=True)

    full = pl.BlockSpec((R, 128), lambda n: (0, 0))
    b512 = pl.BlockSpec((TB, 512), lambda n: (n, 0))
    return pl.pallas_call(
        body, grid=(NB,),
        in_specs=[pl.BlockSpec((TB, 512), lambda n: (n, C_Q // 512)),
                  pl.BlockSpec((R, 256), lambda n: (0, C_K // 256)), full, full,
                  pl.BlockSpec(memory_space=pltpu.SMEM), b512, b512, pl.BlockSpec((TB, 128), lambda n: (n, 0))],
        out_specs=[b512, pl.BlockSpec((R, 256), lambda n: (0, 0)), pl.BlockSpec((8, 128), lambda n: (0, 0))],
        out_shape=[jax.ShapeDtypeStruct((R, 512), F32), jax.ShapeDtypeStruct((R, 256), F32),
                   jax.ShapeDtypeStruct((8, 128), F32)],
        name="swa_bwd", compiler_params=_cp(("arbitrary",)))(z0, z0, cs, sn, sinks, o, do, lse)


CC = 512
HALO = CONV_W - 1


def _conv_fwd(z0, conv_w, conv_b, ln_g, ln_b):
    def body(g_ref, w_ref, cb_ref, lg_ref, lb_ref, cv_ref, s_ref, ubuf):
        n = pl.program_id(0)

        @pl.when(n == 0)
        def _():
            ubuf[0:TB, :] = jnp.zeros((TB, CC), F32)

        u = g_ref[:, 0:CC] * _sig(g_ref[:, CC:2 * CC])
        ubuf[TB:2 * TB, :] = u
        acc = jnp.zeros((TB, CC), F32)
        for w in range(CONV_W):
            acc = acc + ubuf[pl.ds(TB - HALO + w, TB), :] * w_ref[w:w + 1, :]
        cv = acc + cb_ref[...]
        cv_ref[...] = cv
        xc = cv - jnp.mean(cv, axis=1, keepdims=True)
        rs = lax.rsqrt(jnp.mean(xc * xc, axis=1, keepdims=True) + LN_EPS)
        ln = xc * rs * lg_ref[...] + lb_ref[...]
        s_ref[...] = (ln * _sig(ln)).astype(BF16)
        ubuf[0:TB, :] = u

    vec = pl.BlockSpec((1, CC), lambda n: (0, 0))
    blk = pl.BlockSpec((TB, CC), lambda n: (n, 0))
    return pl.pallas_call(
        body, grid=(NB,),
        in_specs=[pl.BlockSpec((TB, 2 * CC), lambda n: (n, C_GLU // (2 * CC))),
                  pl.BlockSpec((32, CC), lambda n: (0, 0)), vec, vec, vec],
        out_specs=[blk, blk],
        out_shape=[jax.ShapeDtypeStruct((R, CC), F32), jax.ShapeDtypeStruct((R, CC), BF16)],
        scratch_shapes=[pltpu.VMEM((2 * TB, CC), F32)],
        name="conv_fwd", compiler_params=_cp(("arbitrary",)))(z0, conv_w, conv_b, ln_g, ln_b)


def _conv_bwd(ds, cv, z0, conv_w, ln_g, ln_b):
    def body(ds_ref, cv_ref, g_ref, w_ref, lg_ref, lb_ref, dglu_ref, dw_ref, dsm_ref, dbuf):
        n = pl.program_id(0)

        @pl.when(n == 0)
        def _():
            dbuf[TB:2 * TB, :] = jnp.zeros((TB, CC), F32)
            dw_ref[...] = jnp.zeros_like(dw_ref)
            dsm_ref[...] = jnp.zeros_like(dsm_ref)

        cv = cv_ref[...]
        xc = cv - jnp.mean(cv, axis=1, keepdims=True)
        rs = lax.rsqrt(jnp.mean(xc * xc, axis=1, keepdims=True) + LN_EPS)
        xh = xc * rs
        ln = xh * lg_ref[...] + lb_ref[...]
        sg = _sig(ln)
        dln = ds_ref[...] * (sg * (1.0 + ln * (1.0 - sg)))
        dxh = dln * lg_ref[...]
        dcv = rs * (dxh - jnp.mean(dxh, axis=1, keepdims=True) - xh * jnp.mean(dxh * xh, axis=1, keepdims=True))
        dsm_ref[0:1, :] += jnp.sum(dcv, axis=0, keepdims=True)
        dsm_ref[1:2, :] += jnp.sum(dln * xh, axis=0, keepdims=True)
        dsm_ref[2:3, :] += jnp.sum(dln, axis=0, keepdims=True)
        dbuf[0:TB, :] = dcv
        a = g_ref[:, 0:CC]
        sb = _sig(g_ref[:, CC:2 * CC])
        u = a * sb
        du = jnp.zeros((TB, CC), F32)
        for w in range(CONV_W):
            sh = dbuf[pl.ds(HALO - w, TB), :]
            du = du + sh * w_ref[w:w + 1, :]
            dw_ref[w:w + 1, :] += jnp.sum(u * sh, axis=0, keepdims=True)
        dglu_ref[:, 0:CC] = (du * sb).astype(BF16)
        dglu_ref[:, CC:2 * CC] = (du * a * sb * (1.0 - sb)).astype(BF16)
        dbuf[TB:2 * TB, :] = dcv

    rev = lambda n: (NB - 1 - n, 0)
    vec = pl.BlockSpec((1, CC), lambda n: (0, 0))
    blk = pl.BlockSpec((TB, CC), rev)
    return pl.pallas_call(
        body, grid=(NB,),
        in_specs=[blk, blk, pl.BlockSpec((TB, 2 * CC), lambda n: (NB - 1 - n, C_GLU // (2 * CC))),
                  pl.BlockSpec((32, CC), lambda n: (0, 0)), vec, vec],
        out_specs=[pl.BlockSpec((TB, 2 * CC), rev), pl.BlockSpec((32, CC), lambda n: (0, 0)),
                   pl.BlockSpec((8, CC), lambda n: (0, 0))],
        out_shape=[jax.ShapeDtypeStruct((R, 2 * CC), BF16), jax.ShapeDtypeStruct((32, CC), F32),
                   jax.ShapeDtypeStruct((8, CC), F32)],
        scratch_shapes=[pltpu.VMEM((2 * TB, CC), F32)],
        name="conv_bwd", compiler_params=_cp(("arbitrary",)))(ds, cv, z0, conv_w, ln_g, ln_b)


def _split_dot(x, t):
    hi = x.astype(BF16)
    lo = (x - hi.astype(F32)).astype(BF16)
    return _dot(hi, t) + _dot(lo, t)


def _sb_tile(qh, kj, valid, tri_gt):
    z = _dot_nt(qh, kj) * SCALE
    e = jnp.exp(-jnp.abs(z))
    lb = jnp.minimum(z, 0.0) - jnp.log(1.0 + e)
    l1 = jnp.where(valid, lb - z, 0.0)
    return z, e, lb, l1, _split_dot(l1, tri_gt)


SB_U = 3
SB_DEAD = -104.0


def _sb_fwd(q, k, v):
    def body(q_ref, k_ref, v_ref, o_ref, c_ref, n_ref):
        p, i = pl.program_id(0), pl.program_id(1)
        lane, row = _lane_row((TB, 128))
        lo = lane < HD
        tri_gt = (row > lane).astype(BF16)
        qv = q_ref[...].astype(F32)
        qs = (jnp.where(lo, qv, 0.0).astype(BF16), jnp.where(lo, 0.0, qv).astype(BF16))
        qpos = i * TB + row

        def cond(st):
            t, _, ca, cb = st
            return jnp.logical_and(i - SB_U * t >= 0, jnp.maximum(jnp.max(ca), jnp.max(cb)) > SB_DEAD)

        def step(st):
            t, acc, ca, cb = st
            cs = [ca, cb]
            for u in range(SB_U):
                jr = i - SB_U * t - u
                j0 = pl.multiple_of(jnp.maximum(jr, 0) * TB, TB)
                kj = k_ref[pl.ds(j0, TB), :]
                vj = v_ref[pl.ds(j0, TB), :]
                kpos = jr * TB + lane
                valid = (kpos >= PAD) & (kpos < qpos)
                av = []
                for hh in range(2):
                    _, _, lb, l1, sfx = _sb_tile(qs[hh], kj, valid, tri_gt)
                    a = jnp.where(valid, jnp.exp(lb + sfx + cs[hh]), 0.0)
                    av.append(_dot(a.astype(BF16), vj))
                    c_ref[...] = jnp.where(lane == 2 * jr + hh, cs[hh], c_ref[...])
                    cs[hh] = cs[hh] + jnp.sum(l1, axis=1, keepdims=True)
                acc = acc + jnp.where(lo, av[0], av[1])
            return t + 1, acc, cs[0], cs[1]

        z1 = jnp.zeros((TB, 1), F32)
        c_ref[...] = jnp.zeros((TB, 128), F32)
        t, acc, _, _ = lax.while_loop(cond, step, (jnp.int32(0), jnp.zeros((TB, 128), F32), z1, z1))
        o_ref[...] = acc
        n_ref[p, i] = t

    slab = pl.BlockSpec((R, 128), lambda p, i: (0, p))
    blk = pl.BlockSpec((TB, 128), lambda p, i: (i, p))
    sd = jax.ShapeDtypeStruct((R, D), F32)
    return pl.pallas_call(
        body, grid=(D // 128, NB), in_specs=[blk, slab, slab],
        out_specs=[blk, blk, pl.BlockSpec(memory_space=pltpu.SMEM)],
        out_shape=[sd, sd, jax.ShapeDtypeStruct((D // 128, NB), jnp.int32)], name="sb_fwd",
        compiler_params=_cp(("arbitrary", "arbitrary")))(q, k, v)


def _sb_bwd(trips, q, k, v, car, do):
    def body(n_ref, q_ref, k_ref, v_ref, c_ref, do_ref, dq_ref, dk_ref, dv_ref):
        p, i = pl.program_id(0), pl.program_id(1)

        @pl.when(i == 0)
        def _():
            dk_ref[...] = jnp.zeros_like(dk_ref)
            dv_ref[...] = jnp.zeros_like(dv_ref)

        lane, row = _lane_row((TB, 128))
        lo = lane < HD
        tri_gt = (row > lane).astype(BF16)
        tri_lt = (row < lane).astype(BF16)
        qv = q_ref[...].astype(F32)
        qs = (jnp.where(lo, qv, 0.0).astype(BF16), jnp.where(lo, 0.0, qv).astype(BF16))
        dov = do_ref[...]
        ct = c_ref[...]
        dos = (jnp.where(lo, dov, 0.0).astype(BF16), jnp.where(lo, 0.0, dov).astype(BF16))
        qpos = i * TB + row
        trips_i = n_ref[p, i]
        first = jnp.maximum(i + 1 - SB_U * trips_i, 0)

        def step(t, carry):
            dq, ga, gb = carry
            gs = [ga, gb]
            for u in range(SB_U):
                jr = first + SB_U * t + u
                j0 = pl.multiple_of(jnp.minimum(jr, i) * TB, TB)
                kj = k_ref[pl.ds(j0, TB), :]
                vj = v_ref[pl.ds(j0, TB), :]
                kpos = jr * TB + lane
                valid = (kpos >= PAD) & (kpos < qpos)
                dqs, dks, dvs = [], [], []
                for hh in range(2):
                    z, e, lb, l1, sfx = _sb_tile(qs[hh], kj, valid, tri_gt)
                    later = jnp.sum(jnp.where(lane == 2 * jr + hh, ct, 0.0), axis=1, keepdims=True)
                    a = jnp.where(valid, jnp.exp(lb + sfx + later), 0.0)
                    gmat = _dot_nt(dos[hh], vj) * a
                    pre = gs[hh] + _split_dot(gmat, tri_lt)
                    r = 1.0 / (1.0 + e)
                    big = z >= 0.0
                    beta = jnp.where(big, r, e * r)
                    omb = jnp.where(big, e * r, r)
                    dz = (jnp.where(valid, gmat * omb - beta * pre, 0.0) * SCALE).astype(BF16)
                    dqs.append(_dot(dz, kj))
                    dks.append(_dot_tn(dz, qs[hh]))
                    dvs.append(_dot_tn(a.astype(BF16), dos[hh]))
                    gs[hh] = gs[hh] + jnp.sum(gmat, axis=1, keepdims=True)
                dk_ref[pl.ds(j0, TB), :] += dks[0] + dks[1]
                dv_ref[pl.ds(j0, TB), :] += dvs[0] + dvs[1]
                dq = dq + jnp.where(lo, dqs[0], dqs[1])
            return dq, gs[0], gs[1]

        z1 = jnp.zeros((TB, 1), F32)
        dq, _, _ = lax.fori_loop(0, trips_i, step, (jnp.zeros((TB, 128), F32), z1, z1))
        dq_ref[...] = dq

    slab = pl.BlockSpec((R, 128), lambda p, i: (0, p))
    blk = pl.BlockSpec((TB, 128), lambda p, i: (i, p))
    sd = jax.ShapeDtypeStruct((R, D), F32)
    return pl.pallas_call(
        body, grid=(D // 128, NB),
        in_specs=[pl.BlockSpec(memory_space=pltpu.SMEM), blk, slab, slab, blk, blk], out_specs=[blk, slab, slab],
        out_shape=[sd, sd, sd], name="sb_bwd",
        compiler_params=_cp(("arbitrary", "arbitrary")))(trips, q, k, v, car, do)


def _adamw(w, parts, m, v, name):
    rows, cols = w.shape
    tr = 256 if rows % 256 == 0 else rows
    nparts = len(parts)

    def body(*refs):
        w_ref = refs[0]
        p_refs = refs[1:1 + nparts]
        m_ref, v_ref, g_ref, d_ref, nm_ref, nv_ref = refs[1 + nparts:]
        g = p_refs[0][...]
        for p_ref in p_refs[1:]:
            g = g + p_ref[...]
        nm = ADAM_B1 * m_ref[...] + (1.0 - ADAM_B1) * g
        nv = ADAM_B2 * v_ref[...] + (1.0 - ADAM_B2) * (g * g)
        m_hat = nm / (1.0 - ADAM_B1 ** ADAM_STEP)
        v_hat = nv / (1.0 - ADAM_B2 ** ADAM_STEP)
        g_ref[...] = g
        d_ref[...] = -ADAM_LR * (m_hat / (jnp.sqrt(v_hat) + ADAM_EPS) + ADAM_WD * w_ref[...])
        nm_ref[...] = nm
        nv_ref[...] = nv

    blk = pl.BlockSpec((tr, cols), lambda i: (i, 0))
    sd = jax.ShapeDtypeStruct((rows, cols), F32)
    return pl.pallas_call(
        body, grid=(rows // tr,), in_specs=[blk] * (3 + nparts), out_specs=[blk] * 4, out_shape=[sd] * 4,
        name=name, compiler_params=_cp(("parallel",)))(w, *parts, m, v)


def _sum4(buf, name):
    _, rows, cols = buf.shape
    tr = 256 if rows % 256 == 0 else rows

    def body(b_ref, o_ref):
        o_ref[...] = ((b_ref[0] + b_ref[1]) + b_ref[2]) + b_ref[3]

    return pl.pallas_call(
        body, grid=(rows // tr,), in_specs=[pl.BlockSpec((4, tr, cols), lambda i: (0, i, 0))],
        out_specs=pl.BlockSpec((tr, cols), lambda i: (i, 0)), out_shape=jax.ShapeDtypeStruct((rows, cols), F32),
        name=name, compiler_params=_cp(("parallel",)))(buf)


def _sum8(buf, name):
    _, rows, cols = buf.shape

    def body(b_ref, o_ref):
        acc = b_ref[0]
        for i in range(1, 8):
            acc = acc + b_ref[i]
        o_ref[...] = acc

    return pl.pallas_call(
        body, out_shape=jax.ShapeDtypeStruct((rows, cols), F32), name=name,
        compiler_params=pltpu.CompilerParams(vmem_limit_bytes=VMEM_LIMIT))(buf)


MESH = pl.DeviceIdType.MESH
ANY = pl.BlockSpec(memory_space=pl.ANY)


def _chip_peers():
    x, y = lax.axis_index("x"), lax.axis_index("y")
    return [(1 - x, y), (x, 1 - y), (1 - x, 1 - y)]


def _gather_chips(shards):
    n = len(shards)

    def body(*refs):
        ins, outs = refs[:n], refs[n:2 * n]
        lsem, ssem, rsem = refs[2 * n:]
        x, y, c = lax.axis_index("x"), lax.axis_index("y"), lax.axis_index("c")
        me = 2 * x + y
        local = [pltpu.make_async_copy(ins[a], outs[a].at[me], lsem.at[a]) for a in range(n)]
        for cp in local:
            cp.start()
        sends = []
        for j, (px, py) in enumerate(_chip_peers()):
            for a in range(n):
                sends.append(pltpu.make_async_remote_copy(
                    src_ref=ins[a], dst_ref=outs[a].at[me], send_sem=ssem.at[j * n + a], recv_sem=rsem.at[j * n + a],
                    device_id=(px, py, c), device_id_type=MESH))
        for cp in sends:
            cp.start()
        for j, (px, py) in enumerate(_chip_peers()):
            for a in range(n):
                pltpu.make_async_remote_copy(
                    src_ref=ins[a], dst_ref=outs[a].at[2 * px + py], send_sem=ssem.at[j * n + a],
                    recv_sem=rsem.at[j * n + a], device_id=(px, py, c), device_id_type=MESH).wait_recv()
        for cp in sends:
            cp.wait_send()
        for cp in local:
            cp.wait()

    return pl.pallas_call(
        body, in_specs=[ANY] * n, out_specs=[ANY] * n,
        out_shape=[jax.ShapeDtypeStruct((N_CHIPS,) + s.shape, s.dtype) for s in shards],
        scratch_shapes=[pltpu.SemaphoreType.DMA((n,)), pltpu.SemaphoreType.DMA((3 * n,)),
                        pltpu.SemaphoreType.DMA((3 * n,))],
        name="gather_chips")(*shards)


def _scatter_chips(grads):
    n = len(grads)

    def body(*refs):
        ins, outs = refs[:n], refs[n:2 * n]
        lsem, ssem, rsem = refs[2 * n:]
        x, y, c = lax.axis_index("x"), lax.axis_index("y"), lax.axis_index("c")
        me = 2 * x + y
        local = [pltpu.make_async_copy(ins[a].at[me], outs[a].at[3], lsem.at[a]) for a in range(n)]
        for cp in local:
            cp.start()
        sends = []
        for j, (px, py) in enumerate(_chip_peers()):
            for a in range(n):
                sends.append(pltpu.make_async_remote_copy(
                    src_ref=ins[a].at[2 * px + py], dst_ref=outs[a].at[j], send_sem=ssem.at[j * n + a],
                    recv_sem=rsem.at[j * n + a], device_id=(px, py, c), device_id_type=MESH))
        for cp in sends:
            cp.start()
        for cp in sends:
            cp.wait()
        for cp in local:
            cp.wait()

    return pl.pallas_call(
        body, in_specs=[ANY] * n, out_specs=[ANY] * n,
        out_shape=[jax.ShapeDtypeStruct(g.shape, g.dtype) for g in grads],
        scratch_shapes=[pltpu.SemaphoreType.DMA((n,)), pltpu.SemaphoreType.DMA((3 * n,)),
                        pltpu.SemaphoreType.DMA((3 * n,))],
        name="scatter_chips")(*grads)


def _swap_cores(parts):
    n = len(parts)

    def body(*refs):
        ins, outs = refs[:n], refs[n:2 * n]
        ssem, rsem = refs[2 * n:]
        x, y, c = lax.axis_index("x"), lax.axis_index("y"), lax.axis_index("c")
        cps = [pltpu.make_async_remote_copy(
            src_ref=ins[a], dst_ref=outs[a], send_sem=ssem.at[a], recv_sem=rsem.at[a],
            device_id=(x, y, 1 - c), device_id_type=MESH) for a in range(n)]
        for cp in cps:
            cp.start()
        for cp in cps:
            cp.wait()

    return pl.pallas_call(
        body, in_specs=[ANY] * n, out_specs=[ANY] * n,
        out_shape=[jax.ShapeDtypeStruct(p.shape, p.dtype) for p in parts],
        scratch_shapes=[pltpu.SemaphoreType.DMA((n,)), pltpu.SemaphoreType.DMA((n,))],
        name="swap_cores")(*parts)


def _gather_all(vec):
    def body(v_ref, o_ref, lsem, ssem, rsem):
        x, y, c = lax.axis_index("x"), lax.axis_index("y"), lax.axis_index("c")
        me = 4 * x + 2 * y + c
        local = pltpu.make_async_copy(v_ref, o_ref.at[me], lsem)
        local.start()
        cps = []
        for k in range(1, 8):
            px, py, pc = x ^ (k >> 2), y ^ ((k >> 1) & 1), c ^ (k & 1)
            cps.append(pltpu.make_async_remote_copy(
                src_ref=v_ref, dst_ref=o_ref.at[me], send_sem=ssem.at[k - 1], recv_sem=rsem.at[k - 1],
                device_id=(px, py, pc), device_id_type=MESH))
        for cp in cps:
            cp.start()
        for k in range(1, 8):
            px, py, pc = x ^ (k >> 2), y ^ ((k >> 1) & 1), c ^ (k & 1)
            pltpu.make_async_remote_copy(
                src_ref=v_ref, dst_ref=o_ref.at[4 * px + 2 * py + pc], send_sem=ssem.at[k - 1],
                recv_sem=rsem.at[k - 1], device_id=(px, py, pc), device_id_type=MESH).wait_recv()
        for cp in cps:
            cp.wait_send()
        local.wait()

    return pl.pallas_call(
        body, in_specs=[ANY], out_specs=ANY, out_shape=jax.ShapeDtypeStruct((8,) + vec.shape, vec.dtype),
        scratch_shapes=[pltpu.SemaphoreType.DMA, pltpu.SemaphoreType.DMA((7,)), pltpu.SemaphoreType.DMA((7,))],
        name="gather_all")(vec)


def _rope_tables():
    pos = (jnp.arange(R, dtype=jnp.int32) - PAD).astype(F32)
    half = HD // 2
    inv = ROPE_THETA ** (-jnp.arange(half, dtype=F32) / half)
    ang = pos[:, None] * inv[None, :]
    cos, sin = jnp.cos(ang), jnp.sin(ang)
    cs = jnp.tile(cos, (1, 4))
    sn = jnp.tile(jnp.concatenate([-sin, sin], axis=1), (1, 2))
    return cs, sn


def _perm_cols(w):
    return jnp.concatenate([w[:, 0:512], w[:, 768:1280], w[:, 1280:2304], w[:, 2304:2816], w[:, 512:640],
                            w[:, 640:768]], axis=1)


def _unperm_cols(w):
    return jnp.concatenate([w[:, C_Q:C_Q + 512], w[:, C_K:C_K + 128], w[:, C_V:C_V + 128], w[:, C_GA:C_GA + 512],
                            w[:, C_GLU:C_GLU + 1024], w[:, C_GB:C_GB + 512]], axis=1)


def _local_step(x, target, p):
    w0 = _perm_cols(p["ab_w_in"])
    wo0, wpw, wo1 = p["ab_w_out"], p["ab_w_pw2"], p["sb_w_out"]
    w1 = p["sb_w_in"]
    conv_w = jnp.concatenate([p["ab_conv_w"], jnp.zeros((1, CC), F32)], axis=0)
    cs, sn = _rope_tables()

    h0 = jnp.concatenate([jnp.zeros((PAD, D), F32), p["meta_tokens"], x], axis=0)

    xn0 = _rms_fwd(h0, p["ab_pre_norm"], "rms_fwd0")
    z0 = _mm([(xn0, w0)], F32, "in_proj0", 544, 256)
    o0, lse0 = _swa_fwd(z0, cs, sn, p["ab_sinks"])
    a0 = _gate_fwd(o0, z0, C_GA, "gate_a_fwd")
    cv0, s0 = _conv_fwd(z0, conv_w, p["ab_conv_b"], p["ab_conv_ln_g"], p["ab_conv_ln_b"])
    t0 = _mm([(s0, wpw)], F32, "pw2", 544, 512)
    c0 = _gate_fwd(t0, z0, C_GB, "gate_b_fwd")
    mix0 = jnp.concatenate([a0, c0], axis=1)
    y0 = _mm([(mix0, wo0)], F32, "out_proj0", 544, 512)
    h1 = _post_fwd(h0, y0, p["ab_post_norm"], "post_fwd0")

    xn1 = _rms_fwd(h1, p["sb_pre_norm"], "rms_fwd1")
    q1 = _mm([(xn1, w1[0])], BF16, "in_proj1_q", 544, 512)
    k1 = _mm([(xn1, w1[1])], BF16, "in_proj1_k", 544, 512)
    v1 = _mm([(xn1, w1[2])], BF16, "in_proj1_v", 544, 512)
    g1 = _mm([(xn1, w1[3])], F32, "in_proj1_g", 544, 512)
    o1, car1, trips1 = _sb_fwd(q1, k1, v1)
    m1 = _gate_fwd(o1, g1, 0, "gate_sb_fwd")
    y1 = _mm([(m1, wo1)], F32, "out_proj1", 544, 512)
    h2 = _post_fwd(h1, y1, p["sb_post_norm"], "post_fwd1")

    dh2, loss_row = _loss_bwd(h2, target)

    dy1, d_sb_post = _rms_bwd(dh2, y1, p["sb_post_norm"], None, BF16, "post_bwd1")
    dm1 = _mm([(dy1, wo1.T)], F32, "out_proj1_dx", 544, 512)
    d_wo1 = _mm([(m1.T, dy1)], F32, "out_proj1_dw", 512, 512)
    do1, dg1 = _gate_bwd(dm1, 0, o1, g1, 0, "gate_sb_bwd")
    dq1, dk1, dv1 = _sb_bwd(trips1, q1, k1, v1, car1, do1)
    dz1 = [dq1, dk1, dv1, dg1]
    dxn1 = _mm([(dz1[j], w1[j].T) for j in range(4)], F32, "in_proj1_dx", 544, 512)
    xn1t = xn1.T
    d_w1 = jnp.stack([_mm([(xn1t, dz1[j])], F32, "in_proj1_dw%d" % j, 512, 512) for j in range(4)])
    dh1, d_sb_pre = _rms_bwd(dxn1, h1, p["sb_pre_norm"], dh2, F32, "rms_bwd1")

    dy0, d_ab_post = _rms_bwd(dh1, y0, p["ab_post_norm"], None, BF16, "post_bwd0")
    dmix0 = _mm([(dy0, wo0.T)], F32, "out_proj0_dx", 544, 512)
    d_wo0 = _mm([(mix0.T, dy0)], F32, "out_proj0_dw", 512, 512)
    dt0, dgb0 = _gate_bwd(dmix0, 512, t0, z0, C_GB, "gate_b_bwd")
    ds0 = _mm([(dt0, wpw.T)], F32, "pw2_dx", 544, 512)
    d_wpw = _mm([(s0.T, dt0)], F32, "pw2_dw", 512, 512)
    dglu0, d_convw, d_small = _conv_bwd(ds0, cv0, z0, conv_w, p["ab_conv_ln_g"], p["ab_conv_ln_b"])
    do0, dga0 = _gate_bwd(dmix0, 0, o0, z0, C_GA, "gate_a_bwd")
    dq0, dkv0, d_sinks = _swa_bwd(z0, cs, sn, p["ab_sinks"], o0, do0, lse0)
    dz0 = jnp.concatenate([dq0.astype(BF16), dga0, dglu0, dgb0, dkv0.astype(BF16)], axis=1)
    dxn0 = _mm([(dz0, w0.T)], F32, "in_proj0_dx", 544, 512)
    d_w0 = _unperm_cols(_mm([(xn0.T, dz0)], F32, "in_proj0_dw", 512, 256))
    dh0, d_ab_pre = _rms_bwd(dxn0, h0, p["ab_pre_norm"], dh1, F32, "rms_bwd0")

    grads = {
        "meta_tokens": dh0[PAD:TB], "ab_pre_norm": d_ab_pre, "ab_w_in": d_w0, "ab_sinks": d_sinks[0:1, 0:8],
        "ab_conv_w": d_convw[0:CONV_W], "ab_conv_b": d_small[0:1], "ab_conv_ln_g": d_small[1:2],
        "ab_conv_ln_b": d_small[2:3], "ab_w_pw2": d_wpw, "ab_w_out": d_wo0, "ab_post_norm": d_ab_post,
        "sb_pre_norm": d_sb_pre, "sb_w_in": d_w1, "sb_w_out": d_wo1, "sb_post_norm": d_sb_post,
    }
    return loss_row, dh0[TB:], grads


SMALL_ROWS = 72
REP_ROWS = 32

WEIGHTS = ["meta_tokens", "ab_pre_norm", "ab_w_in", "ab_sinks", "ab_conv_w", "ab_conv_b", "ab_conv_ln_g",
           "ab_conv_ln_b", "ab_w_pw2", "ab_w_out", "ab_post_norm", "sb_pre_norm", "sb_w_in", "sb_w_out",
           "sb_post_norm"]
BIG = ["ab_w_in", "ab_w_out", "ab_w_pw2", "sb_w_in", "sb_w_out"]


def _pack_small(conv_w, meta, sb_pre, sb_post):
    rows = jnp.concatenate([conv_w, meta.reshape(32, 128), sb_pre.reshape(2, 128), sb_post.reshape(2, 128)], axis=0)
    return jnp.concatenate([rows, jnp.zeros((SMALL_ROWS - rows.shape[0], 128), F32)], axis=0)


def _unpack_small(s):
    return s[0:31], s[31:63].reshape(16, 256), s[63:65].reshape(1, 256), s[65:67].reshape(1, 256)


def _pack_rep(pre, post, conv_b, ln_g, ln_b, sinks):
    flat = jnp.concatenate([pre.reshape(-1), post.reshape(-1), conv_b.reshape(-1), ln_g.reshape(-1),
                            ln_b.reshape(-1), sinks.reshape(-1)])
    flat = jnp.concatenate([flat, jnp.zeros((REP_ROWS * 128 - flat.shape[0],), F32)])
    return flat.reshape(REP_ROWS, 128)


def _unpack_rep(r):
    f = r.reshape(-1)
    return (f[0:1024].reshape(1, 1024), f[1024:2048].reshape(1, 1024), f[2048:2560].reshape(1, 512),
            f[2560:3072].reshape(1, 512), f[3072:3584].reshape(1, 512), f[3584:3592].reshape(1, 8))


def _cols_to_chips(w, width):
    return w.reshape(w.shape[0], N_CHIPS, width).transpose(1, 0, 2)


def _chips_to_cols(w):
    return w.transpose(1, 0, 2).reshape(w.shape[1], -1)


def kernel(x, meta_tokens, ab_pre_norm, ab_w_in, ab_sinks, ab_conv_w, ab_conv_b, ab_conv_ln_g, ab_conv_ln_b, ab_w_pw2, ab_w_out, ab_post_norm, sb_pre_norm, sb_w_in, sb_w_out, sb_post_norm, loss_target, m_meta_tokens, m_ab_pre_norm, m_ab_w_in, m_ab_sinks, m_ab_conv_w, m_ab_conv_b, m_ab_conv_ln_g, m_ab_conv_ln_b, m_ab_w_pw2, m_ab_w_out, m_ab_post_norm, m_sb_pre_norm, m_sb_w_in, m_sb_w_out, m_sb_post_norm, v_meta_tokens, v_ab_pre_norm, v_ab_w_in, v_ab_sinks, v_ab_conv_w, v_ab_conv_b, v_ab_conv_ln_g, v_ab_conv_ln_b, v_ab_w_pw2, v_ab_w_out, v_ab_post_norm, v_sb_pre_norm, v_sb_w_in, v_sb_w_out, v_sb_post_norm):
    w = dict(meta_tokens=meta_tokens, ab_pre_norm=ab_pre_norm, ab_w_in=ab_w_in, ab_sinks=ab_sinks,
             ab_conv_w=ab_conv_w, ab_conv_b=ab_conv_b, ab_conv_ln_g=ab_conv_ln_g, ab_conv_ln_b=ab_conv_ln_b,
             ab_w_pw2=ab_w_pw2, ab_w_out=ab_w_out, ab_post_norm=ab_post_norm, sb_pre_norm=sb_pre_norm,
             sb_w_in=sb_w_in, sb_w_out=sb_w_out, sb_post_norm=sb_post_norm)
    m = dict(meta_tokens=m_meta_tokens, ab_pre_norm=m_ab_pre_norm, ab_w_in=m_ab_w_in, ab_sinks=m_ab_sinks,
             ab_conv_w=m_ab_conv_w, ab_conv_b=m_ab_conv_b, ab_conv_ln_g=m_ab_conv_ln_g,
             ab_conv_ln_b=m_ab_conv_ln_b, ab_w_pw2=m_ab_w_pw2, ab_w_out=m_ab_w_out, ab_post_norm=m_ab_post_norm,
             sb_pre_norm=m_sb_pre_norm, sb_w_in=m_sb_w_in, sb_w_out=m_sb_w_out, sb_post_norm=m_sb_post_norm)
    v = dict(meta_tokens=v_meta_tokens, ab_pre_norm=v_ab_pre_norm, ab_w_in=v_ab_w_in, ab_sinks=v_ab_sinks,
             ab_conv_w=v_ab_conv_w, ab_conv_b=v_ab_conv_b, ab_conv_ln_g=v_ab_conv_ln_g,
             ab_conv_ln_b=v_ab_conv_ln_b, ab_w_pw2=v_ab_w_pw2, ab_w_out=v_ab_w_out, ab_post_norm=v_ab_post_norm,
             sb_pre_norm=v_sb_pre_norm, sb_w_in=v_sb_w_in, sb_w_out=v_sb_w_out, sb_post_norm=v_sb_post_norm)

    def small_of(d):
        return _pack_small(d["ab_conv_w"][0], d["meta_tokens"], d["sb_pre_norm"], d["sb_post_norm"])

    def rep_of(d):
        return _pack_rep(d["ab_pre_norm"], d["ab_post_norm"], d["ab_conv_b"], d["ab_conv_ln_g"], d["ab_conv_ln_b"],
                         d["ab_sinks"])

    gathered = _gather_chips([w[k][0].astype(BF16) for k in BIG] + [small_of(w)])
    g_in0, g_out0, g_pw2, g_in1, g_out1, g_small = gathered
    conv_w_f = _chips_to_cols(g_small[:, 0:31])
    meta_f = _chips_to_cols(g_small[:, 31:63].reshape(N_CHIPS, 16, 256))
    sb_pre_f = g_small[:, 63:65].reshape(1, D)
    sb_post_f = g_small[:, 65:67].reshape(1, D)
    full = {
        "meta_tokens": meta_f, "ab_pre_norm": ab_pre_norm, "ab_w_in": _chips_to_cols(g_in0),
        "ab_sinks": ab_sinks, "ab_conv_w": conv_w_f, "ab_conv_b": ab_conv_b, "ab_conv_ln_g": ab_conv_ln_g,
        "ab_conv_ln_b": ab_conv_ln_b, "ab_w_pw2": g_pw2.reshape(CC, CC), "ab_w_out": g_out0.reshape(D, D),
        "ab_post_norm": ab_post_norm, "sb_pre_norm": sb_pre_f, "sb_w_in": g_in1, "sb_w_out": g_out1.reshape(D, D),
        "sb_post_norm": sb_post_f,
    }

    loss_row, grad_x, g = _local_step(x[0], loss_target[0], full)
    loss = lax.psum(loss_row[0, 0], ("x", "y", "c"))

    send = [_cols_to_chips(g["ab_w_in"], 704), g["ab_w_out"].reshape(N_CHIPS, 256, D),
            g["ab_w_pw2"].reshape(N_CHIPS, 128, CC), g["sb_w_in"], g["sb_w_out"].reshape(N_CHIPS, 256, D)]
    gs_conv = _cols_to_chips(g["ab_conv_w"], 128)
    gs_meta = _cols_to_chips(g["meta_tokens"], 256)
    gs_pre = g["sb_pre_norm"].reshape(N_CHIPS, 1, 256)
    gs_post = g["sb_post_norm"].reshape(N_CHIPS, 1, 256)
    send.append(jnp.stack([_pack_small(gs_conv[j], gs_meta[j], gs_pre[j], gs_post[j]) for j in range(N_CHIPS)]))
    landed = _scatter_chips(send)
    names = BIG + ["small"]
    part = [_sum4(b, "sum4_" + nm) for b, nm in zip(landed, names)]
    other = _swap_cores(part)

    rep_g = _pack_rep(g["ab_pre_norm"], g["ab_post_norm"], g["ab_conv_b"], g["ab_conv_ln_g"], g["ab_conv_ln_b"],
                      g["ab_sinks"])
    rep_sum = _sum8(_gather_all(rep_g), "sum8_rep")

    out_g, out_d, out_m, out_v = {}, {}, {}, {}
    for i, k in enumerate(BIG):
        shp = w[k].shape
        res = _adamw(w[k][0], [part[i], other[i]], m[k][0], v[k][0], "adamw_" + k)
        out_g[k], out_d[k], out_m[k], out_v[k] = [r.reshape(shp) for r in res]
    res = _adamw(small_of(w), [part[5], other[5]], small_of(m), small_of(v), "adamw_small")
    for dst, r in zip((out_g, out_d, out_m, out_v), res):
        cw, mt, pre, post = _unpack_small(r)
        dst["ab_conv_w"], dst["meta_tokens"], dst["sb_pre_norm"], dst["sb_post_norm"] = cw[None], mt, pre, post
    res = _adamw(rep_of(w), [rep_sum], rep_of(m), rep_of(v), "adamw_rep")
    for dst, r in zip((out_g, out_d, out_m, out_v), res):
        (dst["ab_pre_norm"], dst["ab_post_norm"], dst["ab_conv_b"], dst["ab_conv_ln_g"], dst["ab_conv_ln_b"],
         dst["ab_sinks"]) = _unpack_rep(r)

    return (loss, grad_x[None], *[out_g[k] for k in WEIGHTS], *[out_d[k] for k in WEIGHTS],
            *[out_m[k] for k in WEIGHTS], *[out_v[k] for k in WEIGHTS])
```

```python
import functools

import jax
import jax.numpy as jnp
from jax import lax
from jax.experimental import pallas as pl
from jax.experimental.pallas import tpu as pltpu

F32 = jnp.float32
BF16 = jnp.bfloat16

D = 1024
SEQ = 2048
N_META = 16
TB = 128
PAD = TB - N_META
R = SEQ + TB
NB = R // TB
HD = 64
ROPE_THETA = 10000.0
NORM_EPS = 1e-6
LN_EPS = 1e-5
NEG = -1e30
SWA_HEADS = 8
CONV_W = 31
SCALE = HD ** -0.5
N_CHIPS = 4

C_Q, C_GA, C_GLU, C_GB, C_K, C_V = 0, 512, 1024, 2048, 2560, 2688
AB_IN = 2816

ADAM_LR, ADAM_B1, ADAM_B2, ADAM_EPS, ADAM_WD, ADAM_STEP = 0.001, 0.9, 0.999, 1e-08, 0.01, 10

VMEM_LIMIT = 56 * 1024 * 1024


def _cp(sem):
    return pltpu.CompilerParams(dimension_semantics=sem, vmem_limit_bytes=VMEM_LIMIT)


def _sig(x):
    return 1.0 / (1.0 + jnp.exp(-x))


def _dot(a, b):
    return lax.dot_general(a, b, (((1,), (0,)), ((), ())), preferred_element_type=F32)


def _dot_nt(a, b):
    return lax.dot_general(a, b, (((1,), (1,)), ((), ())), preferred_element_type=F32)


def _dot_tn(a, b):
    return lax.dot_general(a, b, (((0,), (0,)), ((), ())), preferred_element_type=F32)


def _mm(pairs, out_dtype, name, tm, tn):
    m, n = pairs[0][0].shape[0], pairs[0][1].shape[1]
    npairs = len(pairs)

    def body(*refs):
        o_ref = refs[2 * npairs]
        acc = None
        for i in range(npairs):
            t = _dot(refs[2 * i][...].astype(BF16), refs[2 * i + 1][...].astype(BF16))
            acc = t if acc is None else acc + t
        o_ref[...] = acc.astype(out_dtype)

    in_specs, args = [], []
    for a, b in pairs:
        k = a.shape[1]
        in_specs += [pl.BlockSpec((tm, k), lambda i, j: (i, 0)), pl.BlockSpec((k, tn), lambda i, j: (0, j))]
        args += [a, b]
    return pl.pallas_call(
        body, grid=(m // tm, n // tn), in_specs=in_specs,
        out_specs=pl.BlockSpec((tm, tn), lambda i, j: (i, j)),
        out_shape=jax.ShapeDtypeStruct((m, n), out_dtype), name=name,
        compiler_params=_cp(("parallel", "parallel")))(*args)


def _rms_fwd(h, g, name):
    def body(h_ref, g_ref, o_ref):
        x = h_ref[...]
        r = lax.rsqrt(jnp.mean(x * x, axis=1, keepdims=True) + NORM_EPS)
        o_ref[...] = (x * r * g_ref[...]).astype(BF16)

    return pl.pallas_call(
        body, grid=(NB,),
        in_specs=[pl.BlockSpec((TB, D), lambda n: (n, 0)), pl.BlockSpec((1, D), lambda n: (0, 0))],
        out_specs=pl.BlockSpec((TB, D), lambda n: (n, 0)),
        out_shape=jax.ShapeDtypeStruct((R, D), BF16), name=name, compiler_params=_cp(("parallel",)))(h, g)


def _post_fwd(h, y, g, name):
    def body(h_ref, y_ref, g_ref, o_ref):
        yv = y_ref[...]
        r = lax.rsqrt(jnp.mean(yv * yv, axis=1, keepdims=True) + NORM_EPS)
        o_ref[...] = h_ref[...] + yv * r * g_ref[...]

    blk = pl.BlockSpec((TB, D), lambda n: (n, 0))
    return pl.pallas_call(
        body, grid=(NB,), in_specs=[blk, blk, pl.BlockSpec((1, D), lambda n: (0, 0))], out_specs=blk,
        out_shape=jax.ShapeDtypeStruct((R, D), F32), name=name, compiler_params=_cp(("parallel",)))(h, y, g)


def _rms_bwd(dout, x, g, res, out_dtype, name):
    has_res = res is not None

    def body(*refs):
        if has_res:
            d_ref, x_ref, g_ref, r_ref, dx_ref, dg_ref = refs
        else:
            d_ref, x_ref, g_ref, dx_ref, dg_ref = refs
        n = pl.program_id(0)
        xv = x_ref[...]
        dv = d_ref[...]
        r = lax.rsqrt(jnp.mean(xv * xv, axis=1, keepdims=True) + NORM_EPS)
        xh = xv * r
        dxh = dv * g_ref[...]
        dx = r * (dxh - xh * jnp.mean(dxh * xh, axis=1, keepdims=True))
        if has_res:
            dx = dx + r_ref[...]
        row = lax.broadcasted_iota(jnp.int32, (TB, D), 0) + n * TB
        dx_ref[...] = jnp.where(row >= PAD, dx, 0.0).astype(out_dtype)

        @pl.when(n == 0)
        def _():
            dg_ref[...] = jnp.zeros_like(dg_ref)

        dg_ref[...] += jnp.sum(dv * xh, axis=0, keepdims=True)

    blk = pl.BlockSpec((TB, D), lambda n: (n, 0))
    vec = pl.BlockSpec((1, D), lambda n: (0, 0))
    ins = [dout, x, g] + ([res] if has_res else [])
    in_specs = [blk, blk, vec] + ([blk] if has_res else [])
    return pl.pallas_call(
        body, grid=(NB,), in_specs=in_specs, out_specs=[blk, vec],
        out_shape=[jax.ShapeDtypeStruct((R, D), out_dtype), jax.ShapeDtypeStruct((1, D), F32)],
        name=name, compiler_params=_cp(("arbitrary",)))(*ins)


GW = 512


def _gate_fwd(o, gsrc, goff, name):
    w = o.shape[1]

    def body(o_ref, g_ref, m_ref):
        gv = g_ref[...]
        m_ref[...] = (o_ref[...] * (gv * _sig(gv))).astype(BF16)

    gb = goff // GW
    return pl.pallas_call(
        body, grid=(NB, w // GW),
        in_specs=[pl.BlockSpec((TB, GW), lambda n, j: (n, j)), pl.BlockSpec((TB, GW), lambda n, j: (n, gb + j))],
        out_specs=pl.BlockSpec((TB, GW), lambda n, j: (n, j)),
        out_shape=jax.ShapeDtypeStruct((R, w), BF16), name=name,
        compiler_params=_cp(("parallel", "parallel")))(o, gsrc)


def _gate_bwd(dsrc, doff, o, gsrc, goff, name):
    w = o.shape[1]

    def body(d_ref, o_ref, g_ref, do_ref, dg_ref):
        gv = g_ref[...]
        dv = d_ref[...]
        s = _sig(gv)
        do_ref[...] = dv * (gv * s)
        dg_ref[...] = (dv * o_ref[...] * (s * (1.0 + gv * (1.0 - s)))).astype(BF16)

    db, gb = doff // GW, goff // GW
    blk = pl.BlockSpec((TB, GW), lambda n, j: (n, j))
    return pl.pallas_call(
        body, grid=(NB, w // GW),
        in_specs=[pl.BlockSpec((TB, GW), lambda n, j: (n, db + j)), blk,
                  pl.BlockSpec((TB, GW), lambda n, j: (n, gb + j))],
        out_specs=[blk, blk],
        out_shape=[jax.ShapeDtypeStruct((R, w), F32), jax.ShapeDtypeStruct((R, w), BF16)], name=name,
        compiler_params=_cp(("parallel", "parallel")))(dsrc, o, gsrc)


def _loss_bwd(h, target):
    def body(h_ref, t_ref, d_ref, l_ref):
        n = pl.program_id(0)

        @pl.when(n == 0)
        def _():
            d_ref[...] = jnp.zeros_like(d_ref)
            l_ref[...] = jnp.zeros_like(l_ref)

        @pl.when(n > 0)
        def _():
            err = h_ref[...] - t_ref[...]
            d_ref[...] = err * (1.0 / D)
            l_ref[...] += jnp.sum(err * err, axis=0, keepdims=True)

        @pl.when(n == NB - 1)
        def _():
            tot = jnp.sum(l_ref[...], axis=1, keepdims=True) * (0.5 / D)
            l_ref[...] = jnp.broadcast_to(tot, (1, D))

    blk = pl.BlockSpec((TB, D), lambda n: (n, 0))
    return pl.pallas_call(
        body, grid=(NB,),
        in_specs=[blk, pl.BlockSpec((TB, D), lambda n: (jnp.maximum(n - 1, 0), 0))],
        out_specs=[blk, pl.BlockSpec((1, D), lambda n: (0, 0))],
        out_shape=[jax.ShapeDtypeStruct((R, D), F32), jax.ShapeDtypeStruct((1, D), F32)],
        name="loss_bwd", compiler_params=_cp(("arbitrary",)))(h, target)


def _lane_row(shape):
    return lax.broadcasted_iota(jnp.int32, shape, 1), lax.broadcasted_iota(jnp.int32, shape, 0)


def _rot_half(x, lane):
    return jnp.where(lane % HD < HD // 2, pltpu.roll(x, 128 - HD // 2, 1), pltpu.roll(x, HD // 2, 1))


def _swa_blocks(n):
    return (0, jnp.maximum(n - 1, 0), n)


def _swa_masks(n, lane, row):
    qpos = n * TB + row
    kp = (n - 1) * TB + lane
    kc = n * TB + lane
    m0 = (lane >= PAD) & (qpos - lane >= TB)
    mp = (kp >= PAD) & (qpos >= kp) & (qpos - kp < TB)
    mc = (kc >= PAD) & (qpos >= kc)
    return (m0, mp, mc)


def _swa_load(n, zq_ref, zkv_ref, cs_ref, sn_ref, lane):
    r0 = pl.multiple_of(n * TB, TB)
    csq, snq = cs_ref[pl.ds(r0, TB), :], sn_ref[pl.ds(r0, TB), :]
    qc = []
    for c in range(4):
        x = zq_ref[:, c * 128:(c + 1) * 128]
        qc.append(x * csq + _rot_half(x, lane) * snq)
    kvs = []
    for b in _swa_blocks(n):
        b0 = pl.multiple_of(b * TB, TB)
        csb, snb = cs_ref[pl.ds(b0, TB), :], sn_ref[pl.ds(b0, TB), :]
        kx = zkv_ref[pl.ds(b0, TB), 0:128]
        kr = kx * csb + _rot_half(kx, lane) * snb
        vx = zkv_ref[pl.ds(b0, TB), 128:256]
        kvs.append((kr.astype(BF16), pltpu.roll(kr, HD, 1).astype(BF16),
                    vx.astype(BF16), pltpu.roll(vx, HD, 1).astype(BF16), csb, snb, b0))
    return qc, (csq, snq), kvs


def _swa_fwd(z0, cs, sn, sinks):
    def body(zq_ref, zkv_ref, cs_ref, sn_ref, sk_ref, o_ref, lse_ref):
        n = pl.program_id(0)
        lane, row = _lane_row((TB, 128))
        lo = lane < HD
        masks = _swa_masks(n, lane, row)
        qc, _, kvs = _swa_load(n, zq_ref, zkv_ref, cs_ref, sn_ref, lane)
        oh = []
        lse_t = jnp.zeros((TB, 128), F32)
        for h in range(SWA_HEADS):
            c, par, g = h // 2, h % 2, h // 4
            half = lo if par == 0 else jnp.logical_not(lo)
            qm = jnp.where(half, qc[c], 0.0).astype(BF16)
            sink = sk_ref[0, h]
            ss = []
            for (k, ka, v, va, _, _, _), m in zip(kvs, masks):
                ksel = k if par == g else ka
                ss.append(jnp.where(m, _dot_nt(qm, ksel) * SCALE, NEG))
            mx = jnp.maximum(jnp.maximum(jnp.max(ss[0], axis=1, keepdims=True), jnp.max(ss[1], axis=1, keepdims=True)),
                             jnp.max(ss[2], axis=1, keepdims=True))
            mx = jnp.maximum(mx, sink)
            es = [jnp.exp(s - mx) for s in ss]
            den = (jnp.sum(es[0], axis=1, keepdims=True) + jnp.sum(es[1], axis=1, keepdims=True)
                   + jnp.sum(es[2], axis=1, keepdims=True) + jnp.exp(sink - mx))
            inv = 1.0 / den
            t = jnp.zeros((TB, 128), F32)
            for (k, ka, v, va, _, _, _), e in zip(kvs, es):
                t = t + _dot((e * inv).astype(BF16), v if par == g else va)
            oh.append(t)
            lse_t = jnp.where(lane == h, mx + jnp.log(den), lse_t)
        oc = [jnp.where(lo, oh[2 * c], oh[2 * c + 1]) for c in range(4)]
        for c in range(4):
            o_ref[:, c * 128:(c + 1) * 128] = oc[c]
        lse_ref[...] = lse_t

    full = pl.BlockSpec((R, 128), lambda n: (0, 0))
    return pl.pallas_call(
        body, grid=(NB,),
        in_specs=[pl.BlockSpec((TB, 512), lambda n: (n, C_Q // 512)),
                  pl.BlockSpec((R, 256), lambda n: (0, C_K // 256)), full, full,
                  pl.BlockSpec(memory_space=pltpu.SMEM)],
        out_specs=[pl.BlockSpec((TB, 512), lambda n: (n, 0)), pl.BlockSpec((TB, 128), lambda n: (n, 0))],
        out_shape=[jax.ShapeDtypeStruct((R, 512), F32), jax.ShapeDtypeStruct((R, 128), F32)],
        name="swa_fwd", compiler_params=_cp(("parallel",)))(z0, z0, cs, sn, sinks)


def _swa_bwd(z0, cs, sn, sinks, o, do, lse):
    def body(zq_ref, zkv_ref, cs_ref, sn_ref, sk_ref, o_ref, do_ref, lse_ref, dq_ref, dkv_ref, dsk_ref):
        n = pl.program_id(0)

        @pl.when(n == 0)
        def _():
            dkv_ref[...] = jnp.zeros_like(dkv_ref)
            dsk_ref[...] = jnp.zeros_like(dsk_ref)

        lane, row = _lane_row((TB, 128))
        lo = lane < HD
        masks = _swa_masks(n, lane, row)
        qc, (csq, snq), kvs = _swa_load(n, zq_ref, zkv_ref, cs_ref, sn_ref, lane)
        lse_t = lse_ref[...]
        dqh = []
        dk_al = [jnp.zeros((TB, 128), F32) for _ in range(3)]
        dk_mis = [jnp.zeros((TB, 128), F32) for _ in range(3)]
        dv_al = [jnp.zeros((TB, 128), F32) for _ in range(3)]
        dv_mis = [jnp.zeros((TB, 128), F32) for _ in range(3)]
        dsk_t = jnp.zeros((TB, 128), F32)
        for h in range(SWA_HEADS):
            c, par, g = h // 2, h % 2, h // 4
            half = lo if par == 0 else jnp.logical_not(lo)
            qm = jnp.where(half, qc[c], 0.0).astype(BF16)
            dov = do_ref[:, c * 128:(c + 1) * 128]
            dom = jnp.where(half, dov, 0.0)
            delta = jnp.sum(dom * o_ref[:, c * 128:(c + 1) * 128], axis=1, keepdims=True)
            dob = dom.astype(BF16)
            lse_h = jnp.sum(jnp.where(lane == h, lse_t, 0.0), axis=1, keepdims=True)
            sink = sk_ref[0, h]
            dqt = jnp.zeros((TB, 128), F32)
            for bi, ((k, ka, v, va, _, _, _), m) in enumerate(zip(kvs, masks)):
                ksel = k if par == g else ka
                vsel = v if par == g else va
                s = jnp.where(m, _dot_nt(qm, ksel) * SCALE, NEG)
                p = jnp.exp(s - lse_h)
                dp = _dot_nt(dob, vsel)
                ds = (p * (dp - delta) * SCALE).astype(BF16)
                dqt = dqt + _dot(ds, ksel)
                dkh = _dot_tn(ds, qm)
                dvh = _dot_tn(p.astype(BF16), dob)
                if par == g:
                    dk_al[bi] = dk_al[bi] + dkh
                    dv_al[bi] = dv_al[bi] + dvh
                else:
                    dk_mis[bi] = dk_mis[bi] + dkh
                    dv_mis[bi] = dv_mis[bi] + dvh
            dqh.append(dqt)
            dsk_t = jnp.where(lane == h, -jnp.exp(sink - lse_h) * delta, dsk_t)
        dqc = [jnp.where(lo, dqh[2 * c], dqh[2 * c + 1]) for c in range(4)]
        for c in range(4):
            t = dqc[c] * snq
            dq_ref[:, c * 128:(c + 1) * 128] = dqc[c] * csq + _rot_half(t, lane)
        for bi, (_, _, _, _, csb, snb, b0) in enumerate(kvs):
            dk = dk_al[bi] + pltpu.roll(dk_mis[bi], HD, 1)
            dv = dv_al[bi] + pltpu.roll(dv_mis[bi], HD, 1)
            dkv_ref[pl.ds(b0, TB), 0:128] += dk * csb + _rot_half(dk * snb, lane)
            dkv_ref[pl.ds(b0, TB), 128:256] += dv
        dsk_ref[0:1, :] += jnp.sum(dsk_t, axis=0, keepdims=True)

    full = pl.BlockSpec((R, 128), lambda n: (0, 0))
    b512 = pl.BlockSpec((TB, 512), lambda n: (n, 0))
    return pl.pallas_call(
        body, grid=(NB,),
        in_specs=[pl.BlockSpec((TB, 512), lambda n: (n, C_Q // 512)),
                  pl.BlockSpec((R, 256), lambda n: (0, C_K // 256)), full, full,
                  pl.BlockSpec(memory_space=pltpu.SMEM), b512, b512, pl.BlockSpec((TB, 128), lambda n: (n, 0))],
        out_specs=[b512, pl.BlockSpec((R, 256), lambda n: (0, 0)), pl.BlockSpec((8, 128), lambda n: (0, 0))],
        out_shape=[jax.ShapeDtypeStruct((R, 512), F32), jax.ShapeDtypeStruct((R, 256), F32),
                   jax.ShapeDtypeStruct((8, 128), F32)],
        name="swa_bwd", compiler_params=_cp(("arbitrary",)))(z0, z0, cs, sn, sinks, o, do, lse)


CC = 512
HALO = CONV_W - 1


def _conv_fwd(z0, conv_w, conv_b, ln_g, ln_b):
    def body(g_ref, w_ref, cb_ref, lg_ref, lb_ref, cv_ref, s_ref, ubuf):
        n = pl.program_id(0)

        @pl.when(n == 0)
        def _():
            ubuf[0:TB, :] = jnp.zeros((TB, CC), F32)

        u = g_ref[:, 0:CC] * _sig(g_ref[:, CC:2 * CC])
        ubuf[TB:2 * TB, :] = u
        acc = jnp.zeros((TB, CC), F32)
        for w in range(CONV_W):
            acc = acc + ubuf[pl.ds(TB - HALO + w, TB), :] * w_ref[w:w + 1, :]
        cv = acc + cb_ref[...]
        cv_ref[...] = cv
        xc = cv - jnp.mean(cv, axis=1, keepdims=True)
        rs = lax.rsqrt(jnp.mean(xc * xc, axis=1, keepdims=True) + LN_EPS)
        ln = xc * rs * lg_ref[...] + lb_ref[...]
        s_ref[...] = (ln * _sig(ln)).astype(BF16)
        ubuf[0:TB, :] = u

    vec = pl.BlockSpec((1, CC), lambda n: (0, 0))
    blk = pl.BlockSpec((TB, CC), lambda n: (n, 0))
    return pl.pallas_call(
        body, grid=(NB,),
        in_specs=[pl.BlockSpec((TB, 2 * CC), lambda n: (n, C_GLU // (2 * CC))),
                  pl.BlockSpec((32, CC), lambda n: (0, 0)), vec, vec, vec],
        out_specs=[blk, blk],
        out_shape=[jax.ShapeDtypeStruct((R, CC), F32), jax.ShapeDtypeStruct((R, CC), BF16)],
        scratch_shapes=[pltpu.VMEM((2 * TB, CC), F32)],
        name="conv_fwd", compiler_params=_cp(("arbitrary",)))(z0, conv_w, conv_b, ln_g, ln_b)


def _conv_bwd(ds, cv, z0, conv_w, ln_g, ln_b):
    def body(ds_ref, cv_ref, g_ref, w_ref, lg_ref, lb_ref, dglu_ref, dw_ref, dsm_ref, dbuf):
        n = pl.program_id(0)

        @pl.when(n == 0)
        def _():
            dbuf[TB:2 * TB, :] = jnp.zeros((TB, CC), F32)
            dw_ref[...] = jnp.zeros_like(dw_ref)
            dsm_ref[...] = jnp.zeros_like(dsm_ref)

        cv = cv_ref[...]
        xc = cv - jnp.mean(cv, axis=1, keepdims=True)
        rs = lax.rsqrt(jnp.mean(xc * xc, axis=1, keepdims=True) + LN_EPS)
        xh = xc * rs
        ln = xh * lg_ref[...] + lb_ref[...]
        sg = _sig(ln)
        dln = ds_ref[...] * (sg * (1.0 + ln * (1.0 - sg)))
        dxh = dln * lg_ref[...]
        dcv = rs * (dxh - jnp.mean(dxh, axis=1, keepdims=True) - xh * jnp.mean(dxh * xh, axis=1, keepdims=True))
        dsm_ref[0:1, :] += jnp.sum(dcv, axis=0, keepdims=True)
        dsm_ref[1:2, :] += jnp.sum(dln * xh, axis=0, keepdims=True)
        dsm_ref[2:3, :] += jnp.sum(dln, axis=0, keepdims=True)
        dbuf[0:TB, :] = dcv
        a = g_ref[:, 0:CC]
        sb = _sig(g_ref[:, CC:2 * CC])
        u = a * sb
        du = jnp.zeros((TB, CC), F32)
        for w in range(CONV_W):
            sh = dbuf[pl.ds(HALO - w, TB), :]
            du = du + sh * w_ref[w:w + 1, :]
            dw_ref[w:w + 1, :] += jnp.sum(u * sh, axis=0, keepdims=True)
        dglu_ref[:, 0:CC] = (du * sb).astype(BF16)
        dglu_ref[:, CC:2 * CC] = (du * a * sb * (1.0 - sb)).astype(BF16)
        dbuf[TB:2 * TB, :] = dcv

    rev = lambda n: (NB - 1 - n, 0)
    vec = pl.BlockSpec((1, CC), lambda n: (0, 0))
    blk = pl.BlockSpec((TB, CC), rev)
    return pl.pallas_call(
        body, grid=(NB,),
        in_specs=[blk, blk, pl.BlockSpec((TB, 2 * CC), lambda n: (NB - 1 - n, C_GLU // (2 * CC))),
                  pl.BlockSpec((32, CC), lambda n: (0, 0)), vec, vec],
        out_specs=[pl.BlockSpec((TB, 2 * CC), rev), pl.BlockSpec((32, CC), lambda n: (0, 0)),
                   pl.BlockSpec((8, CC), lambda n: (0, 0))],
        out_shape=[jax.ShapeDtypeStruct((R, 2 * CC), BF16), jax.ShapeDtypeStruct((32, CC), F32),
                   jax.ShapeDtypeStruct((8, CC), F32)],
        scratch_shapes=[pltpu.VMEM((2 * TB, CC), F32)],
        name="conv_bwd", compiler_params=_cp(("arbitrary",)))(ds, cv, z0, conv_w, ln_g, ln_b)


def _split_dot(x, t):
    hi = x.astype(BF16)
    lo = (x - hi.astype(F32)).astype(BF16)
    return _dot(hi, t) + _dot(lo, t)


def _stack_heads(x):
    lane = lax.broadcasted_iota(jnp.int32, (TB, 128), 1)
    return jnp.concatenate([jnp.where(lane < HD, x, 0.0), jnp.where(lane < HD, 0.0, x)], axis=0).astype(BF16)


def _sb_stack(qv, i):
    lane2, row2 = _lane_row((2 * TB, 128))
    qpos2 = i * TB + (row2 & (TB - 1))
    lane, row = _lane_row((TB, 128))
    return _stack_heads(qv), lane2, qpos2, (row > lane).astype(BF16)


SB_U = 3
SB_DEAD = -104.0


def _sb_fwd(q, k, v):
    def body(q_ref, k_ref, v_ref, o_ref, c_ref, n_ref):
        p, i = pl.program_id(0), pl.program_id(1)
        lane, row = _lane_row((TB, 128))
        lo = lane < HD
        q2, lane2, qpos2, tri_gt = _sb_stack(q_ref[...].astype(F32), i)

        def cond(st):
            t, _, c2 = st
            return jnp.logical_and(i - SB_U * t >= 0, jnp.max(c2) > SB_DEAD)

        def step(st):
            t, acc, c2 = st
            jrs = [i - SB_U * t - u for u in range(SB_U)]
            j0s = [pl.multiple_of(jnp.maximum(jr, 0) * TB, TB) for jr in jrs]
            ks = [k_ref[pl.ds(j0, TB), :] for j0 in j0s]
            zs = [_dot_nt(q2, kj) * SCALE for kj in ks]
            valids, lbs, l1s = [], [], []
            for jr, z in zip(jrs, zs):
                kpos = jr * TB + lane2
                valid = (kpos >= PAD) & (kpos < qpos2)
                lb = jnp.minimum(z, 0.0) - jnp.log(1.0 + jnp.exp(-jnp.abs(z)))
                valids.append(valid)
                lbs.append(lb)
                l1s.append(jnp.where(valid, lb - z, 0.0))
            sfxs = [_split_dot(l1, tri_gt) for l1 in l1s]
            carries = []
            for jr, l1 in zip(jrs, l1s):
                carries.append(c2)
                c_ref[...] = jnp.where(lane == 2 * jr, c2[0:TB], jnp.where(lane == 2 * jr + 1, c2[TB:2 * TB], c_ref[...]))
                c2 = c2 + jnp.sum(l1, axis=1, keepdims=True)
            for j0, valid, lb, sfx, cu in zip(j0s, valids, lbs, sfxs, carries):
                a = jnp.where(valid, jnp.exp(lb + sfx + cu), 0.0).astype(BF16)
                av = _dot(a, v_ref[pl.ds(j0, TB), :])
                acc = acc + jnp.where(lo, av[0:TB], av[TB:2 * TB])
            return t + 1, acc, c2

        c_ref[...] = jnp.zeros((TB, 128), F32)
        t, acc, _ = lax.while_loop(cond, step, (jnp.int32(0), jnp.zeros((TB, 128), F32), jnp.zeros((2 * TB, 1), F32)))
        o_ref[...] = acc
        n_ref[p, i] = t

    slab = pl.BlockSpec((R, 128), lambda p, i: (0, p))
    blk = pl.BlockSpec((TB, 128), lambda p, i: (i, p))
    sd = jax.ShapeDtypeStruct((R, D), F32)
    return pl.pallas_call(
        body, grid=(D // 128, NB), in_specs=[blk, slab, slab],
        out_specs=[blk, blk, pl.BlockSpec(memory_space=pltpu.SMEM)],
        out_shape=[sd, sd, jax.ShapeDtypeStruct((D // 128, NB), jnp.int32)], name="sb_fwd",
        compiler_params=_cp(("arbitrary", "arbitrary")))(q, k, v)


def _sb_bwd(trips, q, k, v, car, do):
    def body(n_ref, q_ref, k_ref, v_ref, c_ref, do_ref, dq_ref, dk_ref, dv_ref):
        p, i = pl.program_id(0), pl.program_id(1)

        @pl.when(i == 0)
        def _():
            dk_ref[...] = jnp.zeros_like(dk_ref)
            dv_ref[...] = jnp.zeros_like(dv_ref)

        lane, row = _lane_row((TB, 128))
        lo = lane < HD
        tri_lt = (row < lane).astype(BF16)
        q2, lane2, qpos2, tri_gt = _sb_stack(q_ref[...].astype(F32), i)
        do2 = _stack_heads(do_ref[...])
        ct = c_ref[...]
        trips_i = n_ref[p, i]
        first = jnp.maximum(i + 1 - SB_U * trips_i, 0)

        def step(t, carry):
            dq, g2 = carry
            jrs = [first + SB_U * t + u for u in range(SB_U)]
            j0s = [pl.multiple_of(jnp.minimum(jr, i) * TB, TB) for jr in jrs]
            ks = [k_ref[pl.ds(j0, TB), :] for j0 in j0s]
            vs = [v_ref[pl.ds(j0, TB), :] for j0 in j0s]
            zs = [_dot_nt(q2, kj) * SCALE for kj in ks]
            das = [_dot_nt(do2, vj) for vj in vs]
            valids, es, lbs, l1s = [], [], [], []
            for jr, z in zip(jrs, zs):
                kpos = jr * TB + lane2
                valid = (kpos >= PAD) & (kpos < qpos2)
                e = jnp.exp(-jnp.abs(z))
                lb = jnp.minimum(z, 0.0) - jnp.log(1.0 + e)
                valids.append(valid)
                es.append(e)
                lbs.append(lb)
                l1s.append(jnp.where(valid, lb - z, 0.0))
            sfxs = [_split_dot(l1, tri_gt) for l1 in l1s]
            a_s, gmats, gpre = [], [], []
            for jr, valid, lb, sfx, da in zip(jrs, valids, lbs, sfxs, das):
                later = jnp.concatenate(
                    [jnp.sum(jnp.where(lane == 2 * jr + hh, ct, 0.0), axis=1, keepdims=True) for hh in range(2)], axis=0)
                a = jnp.where(valid, jnp.exp(lb + sfx + later), 0.0)
                gmat = da * a
                a_s.append(a.astype(BF16))
                gmats.append(gmat)
                gpre.append(g2)
                g2 = g2 + jnp.sum(gmat, axis=1, keepdims=True)
            pres = [gp + _split_dot(gmat, tri_lt) for gp, gmat in zip(gpre, gmats)]
            for j0, kj, valid, z, e, gmat, pre, a in zip(j0s, ks, valids, zs, es, gmats, pres, a_s):
                r = 1.0 / (1.0 + e)
                big = z >= 0.0
                beta = jnp.where(big, r, e * r)
                omb = jnp.where(big, e * r, r)
                dz = (jnp.where(valid, gmat * omb - beta * pre, 0.0) * SCALE).astype(BF16)
                dq2 = _dot(dz, kj)
                dq = dq + jnp.where(lo, dq2[0:TB], dq2[TB:2 * TB])
                dk_ref[pl.ds(j0, TB), :] += _dot_tn(dz, q2)
                dv_ref[pl.ds(j0, TB), :] += _dot_tn(a, do2)
            return dq, g2

        dq, _ = lax.fori_loop(0, trips_i, step, (jnp.zeros((TB, 128), F32), jnp.zeros((2 * TB, 1), F32)))
        dq_ref[...] = dq

    slab = pl.BlockSpec((R, 128), lambda p, i: (0, p))
    blk = pl.BlockSpec((TB, 128), lambda p, i: (i, p))
    sd = jax.ShapeDtypeStruct((R, D), F32)
    return pl.pallas_call(
        body, grid=(D // 128, NB),
        in_specs=[pl.BlockSpec(memory_space=pltpu.SMEM), blk, slab, slab, blk, blk], out_specs=[blk, slab, slab],
        out_shape=[sd, sd, sd], name="sb_bwd",
        compiler_params=_cp(("arbitrary", "arbitrary")))(trips, q, k, v, car, do)


def _adamw(w, parts, m, v, name):
    rows, cols = w.shape
    tr = 256 if rows % 256 == 0 else rows
    nparts = len(parts)

    def body(*refs):
        w_ref = refs[0]
        p_refs = refs[1:1 + nparts]
        m_ref, v_ref, g_ref, d_ref, nm_ref, nv_ref = refs[1 + nparts:]
        g = p_refs[0][...]
        for p_ref in p_refs[1:]:
            g = g + p_ref[...]
        nm = ADAM_B1 * m_ref[...] + (1.0 - ADAM_B1) * g
        nv = ADAM_B2 * v_ref[...] + (1.0 - ADAM_B2) * (g * g)
        m_hat = nm / (1.0 - ADAM_B1 ** ADAM_STEP)
        v_hat = nv / (1.0 - ADAM_B2 ** ADAM_STEP)
        g_ref[...] = g
        d_ref[...] = -ADAM_LR * (m_hat / (jnp.sqrt(v_hat) + ADAM_EPS) + ADAM_WD * w_ref[...])
        nm_ref[...] = nm
        nv_ref[...] = nv

    blk = pl.BlockSpec((tr, cols), lambda i: (i, 0))
    sd = jax.ShapeDtypeStruct((rows, cols), F32)
    return pl.pallas_call(
        body, grid=(rows // tr,), in_specs=[blk] * (3 + nparts), out_specs=[blk] * 4, out_shape=[sd] * 4,
        name=name, compiler_params=_cp(("parallel",)))(w, *parts, m, v)


def _sum8(buf, name):
    _, rows, cols = buf.shape

    def body(b_ref, o_ref):
        acc = b_ref[0]
        for i in range(1, 8):
            acc = acc + b_ref[i]
        o_ref[...] = acc

    return pl.pallas_call(
        body, out_shape=jax.ShapeDtypeStruct((rows, cols), F32), name=name,
        compiler_params=pltpu.CompilerParams(vmem_limit_bytes=VMEM_LIMIT))(buf)


MESH = pl.DeviceIdType.MESH
ANY = pl.BlockSpec(memory_space=pl.ANY)


def _chip_peers():
    x, y = lax.axis_index("x"), lax.axis_index("y")
    return [(1 - x, y), (x, 1 - y), (1 - x, 1 - y)]


def _gather_chips(shards):
    n = len(shards)
    hs = [s.shape[0] // 2 for s in shards]

    def body(*refs):
        ins, outs = refs[:n], refs[n:2 * n]
        lsem, s1, r1, s2, r2 = refs[2 * n:]
        x, y, c = lax.axis_index("x"), lax.axis_index("y"), lax.axis_index("c")
        me = 2 * x + y
        peers = _chip_peers()

        def half(ref, a, cc):
            return ref.at[pl.ds(cc * hs[a], hs[a])]

        local = [pltpu.make_async_copy(ins[a], outs[a].at[me], lsem.at[a]) for a in range(n)]
        for cp in local:
            cp.start()
        first = []
        for j, (px, py) in enumerate(peers):
            for a in range(n):
                first.append(pltpu.make_async_remote_copy(
                    src_ref=half(ins[a], a, c), dst_ref=half(outs[a].at[me], a, c), send_sem=s1.at[j * n + a],
                    recv_sem=r1.at[j * n + a], device_id=(px, py, c), device_id_type=MESH))
        for cp in first:
            cp.start()
        passed = []
        for j, (px, py) in enumerate(peers):
            for a in range(n):
                got = half(outs[a].at[2 * px + py], a, c)
                pltpu.make_async_remote_copy(
                    src_ref=got, dst_ref=got, send_sem=s1.at[j * n + a], recv_sem=r1.at[j * n + a],
                    device_id=(px, py, c), device_id_type=MESH).wait_recv()
                cp = pltpu.make_async_remote_copy(
                    src_ref=got, dst_ref=got, send_sem=s2.at[j * n + a], recv_sem=r2.at[j * n + a],
                    device_id=(x, y, 1 - c), device_id_type=MESH)
                cp.start()
                passed.append(cp)
        for j, (px, py) in enumerate(peers):
            for a in range(n):
                theirs = half(outs[a].at[2 * px + py], a, 1 - c)
                pltpu.make_async_remote_copy(
                    src_ref=theirs, dst_ref=theirs, send_sem=s2.at[j * n + a], recv_sem=r2.at[j * n + a],
                    device_id=(x, y, 1 - c), device_id_type=MESH).wait_recv()
        for cp in first + passed:
            cp.wait_send()
        for cp in local:
            cp.wait()

    sems = [pltpu.SemaphoreType.DMA((n,))] + [pltpu.SemaphoreType.DMA((3 * n,))] * 4
    return pl.pallas_call(
        body, in_specs=[ANY] * n, out_specs=[ANY] * n,
        out_shape=[jax.ShapeDtypeStruct((N_CHIPS,) + s.shape, s.dtype) for s in shards],
        scratch_shapes=sems, name="gather_chips")(*shards)


def _pair_exchange(grads):
    n = len(grads)
    hs = [g.shape[1] // 2 for g in grads]

    def body(*refs):
        ins, own, got = refs[:n], refs[n:2 * n], refs[2 * n:3 * n]
        lsem, ssem, rsem = refs[3 * n:]
        x, y, c = lax.axis_index("x"), lax.axis_index("y"), lax.axis_index("c")
        local = [pltpu.make_async_copy(ins[a].at[:, pl.ds(c * hs[a], hs[a])], own[a], lsem.at[a]) for a in range(n)]
        sends = [pltpu.make_async_remote_copy(
            src_ref=ins[a].at[:, pl.ds((1 - c) * hs[a], hs[a])], dst_ref=got[a], send_sem=ssem.at[a],
            recv_sem=rsem.at[a], device_id=(x, y, 1 - c), device_id_type=MESH) for a in range(n)]
        for cp in local + sends:
            cp.start()
        for cp in sends + local:
            cp.wait()

    half_shapes = [jax.ShapeDtypeStruct((N_CHIPS, h) + g.shape[2:], g.dtype) for g, h in zip(grads, hs)]
    res = pl.pallas_call(
        body, in_specs=[ANY] * n, out_specs=[ANY] * (2 * n), out_shape=half_shapes * 2,
        scratch_shapes=[pltpu.SemaphoreType.DMA((n,))] * 3, name="pair_exchange")(*grads)
    return res[:n], res[n:]


def _sum_pair(own, got, send_dtype, name):
    _, rows, cols = own.shape
    tr = 256 if rows % 256 == 0 else rows

    def body(a_ref, b_ref, f_ref, s_ref):
        t = a_ref[...].astype(F32) + b_ref[...].astype(F32)
        f_ref[...] = t
        s_ref[...] = t.astype(send_dtype)

    blk = pl.BlockSpec((N_CHIPS, tr, cols), lambda i: (0, i, 0))
    return pl.pallas_call(
        body, grid=(rows // tr,), in_specs=[blk, blk], out_specs=[blk, blk],
        out_shape=[jax.ShapeDtypeStruct(own.shape, F32), jax.ShapeDtypeStruct(own.shape, send_dtype)],
        name=name, compiler_params=_cp(("parallel",)))(own, got)


def _scatter_chips(keep, send):
    n = len(send)

    def body(*refs):
        kin, sin, mine, land = refs[:n], refs[n:2 * n], refs[2 * n:3 * n], refs[3 * n:4 * n]
        lsem, ssem, rsem = refs[4 * n:]
        x, y, c = lax.axis_index("x"), lax.axis_index("y"), lax.axis_index("c")
        me = 2 * x + y
        local = [pltpu.make_async_copy(kin[a].at[me], mine[a], lsem.at[a]) for a in range(n)]
        sends = []
        for j, (px, py) in enumerate(_chip_peers()):
            for a in range(n):
                sends.append(pltpu.make_async_remote_copy(
                    src_ref=sin[a].at[2 * px + py], dst_ref=land[a].at[j], send_sem=ssem.at[j * n + a],
                    recv_sem=rsem.at[j * n + a], device_id=(px, py, c), device_id_type=MESH))
        for cp in local + sends:
            cp.start()
        for cp in sends + local:
            cp.wait()

    out_shape = ([jax.ShapeDtypeStruct(k.shape[1:], F32) for k in keep]
                 + [jax.ShapeDtypeStruct((3,) + s.shape[1:], s.dtype) for s in send])
    res = pl.pallas_call(
        body, in_specs=[ANY] * (2 * n), out_specs=[ANY] * (2 * n), out_shape=out_shape,
        scratch_shapes=[pltpu.SemaphoreType.DMA((n,)), pltpu.SemaphoreType.DMA((3 * n,)),
                        pltpu.SemaphoreType.DMA((3 * n,))],
        name="scatter_chips")(*keep, *send)
    return res[:n], res[n:]


def _sum_shard(mine, land, name):
    rows, cols = mine.shape
    tr = 256 if rows % 256 == 0 else rows

    def body(m_ref, l_ref, o_ref):
        o_ref[...] = ((m_ref[...] + l_ref[0].astype(F32)) + l_ref[1].astype(F32)) + l_ref[2].astype(F32)

    return pl.pallas_call(
        body, grid=(rows // tr,),
        in_specs=[pl.BlockSpec((tr, cols), lambda i: (i, 0)), pl.BlockSpec((3, tr, cols), lambda i: (0, i, 0))],
        out_specs=pl.BlockSpec((tr, cols), lambda i: (i, 0)), out_shape=jax.ShapeDtypeStruct((rows, cols), F32),
        name=name, compiler_params=_cp(("parallel",)))(mine, land)


def _join_cores(halves):
    n = len(halves)

    def body(*refs):
        ins, outs = refs[:n], refs[n:2 * n]
        lsem, ssem, rsem = refs[2 * n:]
        x, y, c = lax.axis_index("x"), lax.axis_index("y"), lax.axis_index("c")

        def half(a, cc):
            h = halves[a].shape[0]
            return outs[a].at[pl.ds(cc * h, h)]

        local = [pltpu.make_async_copy(ins[a], half(a, c), lsem.at[a]) for a in range(n)]
        sends = [pltpu.make_async_remote_copy(
            src_ref=ins[a], dst_ref=half(a, c), send_sem=ssem.at[a], recv_sem=rsem.at[a],
            device_id=(x, y, 1 - c), device_id_type=MESH) for a in range(n)]
        for cp in local + sends:
            cp.start()
        for a in range(n):
            sends[a].wait_send()
            pltpu.make_async_remote_copy(
                src_ref=ins[a], dst_ref=half(a, 1 - c), send_sem=ssem.at[a], recv_sem=rsem.at[a],
                device_id=(x, y, 1 - c), device_id_type=MESH).wait_recv()
        for cp in local:
            cp.wait()

    return pl.pallas_call(
        body, in_specs=[ANY] * n, out_specs=[ANY] * n,
        out_shape=[jax.ShapeDtypeStruct((2 * h.shape[0],) + h.shape[1:], h.dtype) for h in halves],
        scratch_shapes=[pltpu.SemaphoreType.DMA((n,))] * 3, name="join_cores")(*halves)


def _gather_all(vec):
    def body(v_ref, o_ref, lsem, ssem, rsem):
        x, y, c = lax.axis_index("x"), lax.axis_index("y"), lax.axis_index("c")
        me = 4 * x + 2 * y + c
        local = pltpu.make_async_copy(v_ref, o_ref.at[me], lsem)
        local.start()
        cps = []
        for k in range(1, 8):
            px, py, pc = x ^ (k >> 2), y ^ ((k >> 1) & 1), c ^ (k & 1)
            cps.append(pltpu.make_async_remote_copy(
                src_ref=v_ref, dst_ref=o_ref.at[me], send_sem=ssem.at[k - 1], recv_sem=rsem.at[k - 1],
                device_id=(px, py, pc), device_id_type=MESH))
        for cp in cps:
            cp.start()
        for k in range(1, 8):
            px, py, pc = x ^ (k >> 2), y ^ ((k >> 1) & 1), c ^ (k & 1)
            pltpu.make_async_remote_copy(
                src_ref=v_ref, dst_ref=o_ref.at[4 * px + 2 * py + pc], send_sem=ssem.at[k - 1],
                recv_sem=rsem.at[k - 1], device_id=(px, py, pc), device_id_type=MESH).wait_recv()
        for cp in cps:
            cp.wait_send()
        local.wait()

    return pl.pallas_call(
        body, in_specs=[ANY], out_specs=ANY, out_shape=jax.ShapeDtypeStruct((8,) + vec.shape, vec.dtype),
        scratch_shapes=[pltpu.SemaphoreType.DMA, pltpu.SemaphoreType.DMA((7,)), pltpu.SemaphoreType.DMA((7,))],
        name="gather_all")(vec)


def _rope_tables():
    pos = (jnp.arange(R, dtype=jnp.int32) - PAD).astype(F32)
    half = HD // 2
    inv = ROPE_THETA ** (-jnp.arange(half, dtype=F32) / half)
    ang = pos[:, None] * inv[None, :]
    cos, sin = jnp.cos(ang), jnp.sin(ang)
    cs = jnp.tile(cos, (1, 4))
    sn = jnp.tile(jnp.concatenate([-sin, sin], axis=1), (1, 2))
    return cs, sn


def _perm_cols(w):
    return jnp.concatenate([w[:, 0:512], w[:, 768:1280], w[:, 1280:2304], w[:, 2304:2816], w[:, 512:640],
                            w[:, 640:768]], axis=1)


def _unperm_cols(w):
    return jnp.concatenate([w[:, C_Q:C_Q + 512], w[:, C_K:C_K + 128], w[:, C_V:C_V + 128], w[:, C_GA:C_GA + 512],
                            w[:, C_GLU:C_GLU + 1024], w[:, C_GB:C_GB + 512]], axis=1)


def _local_step(x, target, p):
    w0 = _perm_cols(p["ab_w_in"])
    wo0, wpw, wo1 = p["ab_w_out"], p["ab_w_pw2"], p["sb_w_out"]
    w1 = p["sb_w_in"]
    conv_w = jnp.concatenate([p["ab_conv_w"], jnp.zeros((1, CC), F32)], axis=0)
    cs, sn = _rope_tables()

    h0 = jnp.concatenate([jnp.zeros((PAD, D), F32), p["meta_tokens"], x], axis=0)

    xn0 = _rms_fwd(h0, p["ab_pre_norm"], "rms_fwd0")
    z0 = _mm([(xn0, w0)], F32, "in_proj0", 544, 256)
    o0, lse0 = _swa_fwd(z0, cs, sn, p["ab_sinks"])
    a0 = _gate_fwd(o0, z0, C_GA, "gate_a_fwd")
    cv0, s0 = _conv_fwd(z0, conv_w, p["ab_conv_b"], p["ab_conv_ln_g"], p["ab_conv_ln_b"])
    t0 = _mm([(s0, wpw)], F32, "pw2", 544, 512)
    c0 = _gate_fwd(t0, z0, C_GB, "gate_b_fwd")
    mix0 = jnp.concatenate([a0, c0], axis=1)
    y0 = _mm([(mix0, wo0)], F32, "out_proj0", 544, 512)
    h1 = _post_fwd(h0, y0, p["ab_post_norm"], "post_fwd0")

    xn1 = _rms_fwd(h1, p["sb_pre_norm"], "rms_fwd1")
    q1 = _mm([(xn1, w1[0])], BF16, "in_proj1_q", 544, 512)
    k1 = _mm([(xn1, w1[1])], BF16, "in_proj1_k", 544, 512)
    v1 = _mm([(xn1, w1[2])], BF16, "in_proj1_v", 544, 512)
    g1 = _mm([(xn1, w1[3])], F32, "in_proj1_g", 544, 512)
    o1, car1, trips1 = _sb_fwd(q1, k1, v1)
    m1 = _gate_fwd(o1, g1, 0, "gate_sb_fwd")
    y1 = _mm([(m1, wo1)], F32, "out_proj1", 544, 512)
    h2 = _post_fwd(h1, y1, p["sb_post_norm"], "post_fwd1")

    dh2, loss_row = _loss_bwd(h2, target)

    dy1, d_sb_post = _rms_bwd(dh2, y1, p["sb_post_norm"], None, BF16, "post_bwd1")
    dm1 = _mm([(dy1, wo1.T)], F32, "out_proj1_dx", 544, 512)
    d_wo1 = _mm([(m1.T, dy1)], BF16, "out_proj1_dw", 512, 512)
    do1, dg1 = _gate_bwd(dm1, 0, o1, g1, 0, "gate_sb_bwd")
    dq1, dk1, dv1 = _sb_bwd(trips1, q1, k1, v1, car1, do1)
    dz1 = [dq1, dk1, dv1, dg1]
    dxn1 = _mm([(dz1[j], w1[j].T) for j in range(4)], F32, "in_proj1_dx", 544, 512)
    xn1t = xn1.T
    d_w1 = jnp.stack([_mm([(xn1t, dz1[j])], BF16, "in_proj1_dw%d" % j, 512, 512) for j in range(4)])
    dh1, d_sb_pre = _rms_bwd(dxn1, h1, p["sb_pre_norm"], dh2, F32, "rms_bwd1")

    dy0, d_ab_post = _rms_bwd(dh1, y0, p["ab_post_norm"], None, BF16, "post_bwd0")
    dmix0 = _mm([(dy0, wo0.T)], F32, "out_proj0_dx", 544, 512)
    d_wo0 = _mm([(mix0.T, dy0)], BF16, "out_proj0_dw", 512, 512)
    dt0, dgb0 = _gate_bwd(dmix0, 512, t0, z0, C_GB, "gate_b_bwd")
    ds0 = _mm([(dt0, wpw.T)], F32, "pw2_dx", 544, 512)
    d_wpw = _mm([(s0.T, dt0)], BF16, "pw2_dw", 512, 512)
    dglu0, d_convw, d_small = _conv_bwd(ds0, cv0, z0, conv_w, p["ab_conv_ln_g"], p["ab_conv_ln_b"])
    do0, dga0 = _gate_bwd(dmix0, 0, o0, z0, C_GA, "gate_a_bwd")
    dq0, dkv0, d_sinks = _swa_bwd(z0, cs, sn, p["ab_sinks"], o0, do0, lse0)
    dz0 = jnp.concatenate([dq0.astype(BF16), dga0, dglu0, dgb0, dkv0.astype(BF16)], axis=1)
    dxn0 = _mm([(dz0, w0.T)], F32, "in_proj0_dx", 544, 512)
    d_w0 = _unperm_cols(_mm([(xn0.T, dz0)], BF16, "in_proj0_dw", 512, 256))
    dh0, d_ab_pre = _rms_bwd(dxn0, h0, p["ab_pre_norm"], dh1, F32, "rms_bwd0")

    grads = {
        "meta_tokens": dh0[PAD:TB], "ab_pre_norm": d_ab_pre, "ab_w_in": d_w0, "ab_sinks": d_sinks[0:1, 0:8],
        "ab_conv_w": d_convw[0:CONV_W], "ab_conv_b": d_small[0:1], "ab_conv_ln_g": d_small[1:2],
        "ab_conv_ln_b": d_small[2:3], "ab_w_pw2": d_wpw, "ab_w_out": d_wo0, "ab_post_norm": d_ab_post,
        "sb_pre_norm": d_sb_pre, "sb_w_in": d_w1, "sb_w_out": d_wo1, "sb_post_norm": d_sb_post,
    }
    return loss_row, dh0[TB:], grads


SMALL_ROWS = 80
REP_ROWS = 32

WEIGHTS = ["meta_tokens", "ab_pre_norm", "ab_w_in", "ab_sinks", "ab_conv_w", "ab_conv_b", "ab_conv_ln_g",
           "ab_conv_ln_b", "ab_w_pw2", "ab_w_out", "ab_post_norm", "sb_pre_norm", "sb_w_in", "sb_w_out",
           "sb_post_norm"]
BIG = ["ab_w_in", "ab_w_out", "ab_w_pw2", "sb_w_in", "sb_w_out"]


def _pack_small(conv_w, meta, sb_pre, sb_post):
    rows = jnp.concatenate([conv_w, meta.reshape(32, 128), sb_pre.reshape(2, 128), sb_post.reshape(2, 128)], axis=0)
    return jnp.concatenate([rows, jnp.zeros((SMALL_ROWS - rows.shape[0], 128), F32)], axis=0)


def _unpack_small(s):
    return s[0:31], s[31:63].reshape(16, 256), s[63:65].reshape(1, 256), s[65:67].reshape(1, 256)


def _pack_rep(pre, post, conv_b, ln_g, ln_b, sinks):
    flat = jnp.concatenate([pre.reshape(-1), post.reshape(-1), conv_b.reshape(-1), ln_g.reshape(-1),
                            ln_b.reshape(-1), sinks.reshape(-1)])
    flat = jnp.concatenate([flat, jnp.zeros((REP_ROWS * 128 - flat.shape[0],), F32)])
    return flat.reshape(REP_ROWS, 128)


def _unpack_rep(r):
    f = r.reshape(-1)
    return (f[0:1024].reshape(1, 1024), f[1024:2048].reshape(1, 1024), f[2048:2560].reshape(1, 512),
            f[2560:3072].reshape(1, 512), f[3072:3584].reshape(1, 512), f[3584:3592].reshape(1, 8))


def _cols_to_chips(w, width):
    return w.reshape(w.shape[0], N_CHIPS, width).transpose(1, 0, 2)


def _chips_to_cols(w):
    return w.transpose(1, 0, 2).reshape(w.shape[1], -1)


def kernel(x, meta_tokens, ab_pre_norm, ab_w_in, ab_sinks, ab_conv_w, ab_conv_b, ab_conv_ln_g, ab_conv_ln_b, ab_w_pw2, ab_w_out, ab_post_norm, sb_pre_norm, sb_w_in, sb_w_out, sb_post_norm, loss_target, m_meta_tokens, m_ab_pre_norm, m_ab_w_in, m_ab_sinks, m_ab_conv_w, m_ab_conv_b, m_ab_conv_ln_g, m_ab_conv_ln_b, m_ab_w_pw2, m_ab_w_out, m_ab_post_norm, m_sb_pre_norm, m_sb_w_in, m_sb_w_out, m_sb_post_norm, v_meta_tokens, v_ab_pre_norm, v_ab_w_in, v_ab_sinks, v_ab_conv_w, v_ab_conv_b, v_ab_conv_ln_g, v_ab_conv_ln_b, v_ab_w_pw2, v_ab_w_out, v_ab_post_norm, v_sb_pre_norm, v_sb_w_in, v_sb_w_out, v_sb_post_norm):
    w = dict(meta_tokens=meta_tokens, ab_pre_norm=ab_pre_norm, ab_w_in=ab_w_in, ab_sinks=ab_sinks,
             ab_conv_w=ab_conv_w, ab_conv_b=ab_conv_b, ab_conv_ln_g=ab_conv_ln_g, ab_conv_ln_b=ab_conv_ln_b,
             ab_w_pw2=ab_w_pw2, ab_w_out=ab_w_out, ab_post_norm=ab_post_norm, sb_pre_norm=sb_pre_norm,
             sb_w_in=sb_w_in, sb_w_out=sb_w_out, sb_post_norm=sb_post_norm)
    m = dict(meta_tokens=m_meta_tokens, ab_pre_norm=m_ab_pre_norm, ab_w_in=m_ab_w_in, ab_sinks=m_ab_sinks,
             ab_conv_w=m_ab_conv_w, ab_conv_b=m_ab_conv_b, ab_conv_ln_g=m_ab_conv_ln_g,
             ab_conv_ln_b=m_ab_conv_ln_b, ab_w_pw2=m_ab_w_pw2, ab_w_out=m_ab_w_out, ab_post_norm=m_ab_post_norm,
             sb_pre_norm=m_sb_pre_norm, sb_w_in=m_sb_w_in, sb_w_out=m_sb_w_out, sb_post_norm=m_sb_post_norm)
    v = dict(meta_tokens=v_meta_tokens, ab_pre_norm=v_ab_pre_norm, ab_w_in=v_ab_w_in, ab_sinks=v_ab_sinks,
             ab_conv_w=v_ab_conv_w, ab_conv_b=v_ab_conv_b, ab_conv_ln_g=v_ab_conv_ln_g,
             ab_conv_ln_b=v_ab_conv_ln_b, ab_w_pw2=v_ab_w_pw2, ab_w_out=v_ab_w_out, ab_post_norm=v_ab_post_norm,
             sb_pre_norm=v_sb_pre_norm, sb_w_in=v_sb_w_in, sb_w_out=v_sb_w_out, sb_post_norm=v_sb_post_norm)

    def small_of(d):
        return _pack_small(d["ab_conv_w"][0], d["meta_tokens"], d["sb_pre_norm"], d["sb_post_norm"])

    def rep_of(d):
        return _pack_rep(d["ab_pre_norm"], d["ab_post_norm"], d["ab_conv_b"], d["ab_conv_ln_g"], d["ab_conv_ln_b"],
                         d["ab_sinks"])

    gathered = _gather_chips([w[k][0].astype(BF16) for k in BIG] + [small_of(w)])
    g_in0, g_out0, g_pw2, g_in1, g_out1, g_small = gathered
    conv_w_f = _chips_to_cols(g_small[:, 0:31])
    meta_f = _chips_to_cols(g_small[:, 31:63].reshape(N_CHIPS, 16, 256))
    sb_pre_f = g_small[:, 63:65].reshape(1, D)
    sb_post_f = g_small[:, 65:67].reshape(1, D)
    full = {
        "meta_tokens": meta_f, "ab_pre_norm": ab_pre_norm, "ab_w_in": _chips_to_cols(g_in0),
        "ab_sinks": ab_sinks, "ab_conv_w": conv_w_f, "ab_conv_b": ab_conv_b, "ab_conv_ln_g": ab_conv_ln_g,
        "ab_conv_ln_b": ab_conv_ln_b, "ab_w_pw2": g_pw2.reshape(CC, CC), "ab_w_out": g_out0.reshape(D, D),
        "ab_post_norm": ab_post_norm, "sb_pre_norm": sb_pre_f, "sb_w_in": g_in1, "sb_w_out": g_out1.reshape(D, D),
        "sb_post_norm": sb_post_f,
    }

    loss_row, grad_x, g = _local_step(x[0], loss_target[0], full)
    loss = lax.psum(loss_row[0, 0], ("x", "y", "c"))

    send = [_cols_to_chips(g["ab_w_in"], 704), g["ab_w_out"].reshape(N_CHIPS, 256, D),
            g["ab_w_pw2"].reshape(N_CHIPS, 128, CC), g["sb_w_in"], g["sb_w_out"].reshape(N_CHIPS, 256, D)]
    gs_conv = _cols_to_chips(g["ab_conv_w"], 128)
    gs_meta = _cols_to_chips(g["meta_tokens"], 256)
    gs_pre = g["sb_pre_norm"].reshape(N_CHIPS, 1, 256)
    gs_post = g["sb_post_norm"].reshape(N_CHIPS, 1, 256)
    send.append(jnp.stack([_pack_small(gs_conv[j], gs_meta[j], gs_pre[j], gs_post[j]) for j in range(N_CHIPS)]))
    names = BIG + ["small"]
    own, got = _pair_exchange(send)
    pair = [_sum_pair(o, t, o.dtype, "sum_pair_" + nm) for o, t, nm in zip(own, got, names)]
    mine, land = _scatter_chips([pr[0] for pr in pair], [pr[1] for pr in pair])
    halves = [_sum_shard(mi, la, "sum_shard_" + nm) for mi, la, nm in zip(mine, land, names)]
    total = _join_cores(halves)

    rep_g = _pack_rep(g["ab_pre_norm"], g["ab_post_norm"], g["ab_conv_b"], g["ab_conv_ln_g"], g["ab_conv_ln_b"],
                      g["ab_sinks"])
    rep_sum = _sum8(_gather_all(rep_g), "sum8_rep")

    out_g, out_d, out_m, out_v = {}, {}, {}, {}
    for i, k in enumerate(BIG):
        shp = w[k].shape
        res = _adamw(w[k][0], [total[i]], m[k][0], v[k][0], "adamw_" + k)
        out_g[k], out_d[k], out_m[k], out_v[k] = [r.reshape(shp) for r in res]
    res = _adamw(small_of(w), [total[5]], small_of(m), small_of(v), "adamw_small")
    for dst, r in zip((out_g, out_d, out_m, out_v), res):
        cw, mt, pre, post = _unpack_small(r)
        dst["ab_conv_w"], dst["meta_tokens"], dst["sb_pre_norm"], dst["sb_post_norm"] = cw[None], mt, pre, post
    res = _adamw(rep_of(w), [rep_sum], rep_of(m), rep_of(v), "adamw_rep")
    for dst, r in zip((out_g, out_d, out_m, out_v), res):
        (dst["ab_pre_norm"], dst["ab_post_norm"], dst["ab_conv_b"], dst["ab_conv_ln_g"], dst["ab_conv_ln_b"],
         dst["ab_sinks"]) = _unpack_rep(r)

    return (loss, grad_x[None], *[out_g[k] for k in WEIGHTS], *[out_d[k] for k in WEIGHTS],
            *[out_m[k] for k in WEIGHTS], *[out_v[k] for k in WEIGHTS])
```

```python
import functools

import jax
import jax.numpy as jnp
from jax import lax
from jax.experimental import pallas as pl
from jax.experimental.pallas import tpu as pltpu

F32 = jnp.float32
BF16 = jnp.bfloat16

D = 1024
SEQ = 2048
N_META = 16
TB = 128
PAD = TB - N_META
R = SEQ + TB
NB = R // TB
HD = 64
ROPE_THETA = 10000.0
NORM_EPS = 1e-6
LN_EPS = 1e-5
NEG = -1e30
SWA_HEADS = 8
CONV_W = 31
SCALE = HD ** -0.5
N_CHIPS = 4

C_Q, C_GA, C_GLU, C_GB, C_K, C_V = 0, 512, 1024, 2048, 2560, 2688
AB_IN = 2816

ADAM_LR, ADAM_B1, ADAM_B2, ADAM_EPS, ADAM_WD, ADAM_STEP = 0.001, 0.9, 0.999, 1e-08, 0.01, 10

VMEM_LIMIT = 56 * 1024 * 1024


def _cp(sem):
    return pltpu.CompilerParams(dimension_semantics=sem, vmem_limit_bytes=VMEM_LIMIT)


def _sig(x):
    return 1.0 / (1.0 + jnp.exp(-x))


def _dot(a, b):
    return lax.dot_general(a, b, (((1,), (0,)), ((), ())), preferred_element_type=F32)


def _dot_nt(a, b):
    return lax.dot_general(a, b, (((1,), (1,)), ((), ())), preferred_element_type=F32)


def _dot_tn(a, b):
    return lax.dot_general(a, b, (((0,), (0,)), ((), ())), preferred_element_type=F32)


def _mm(pairs, out_dtype, name, tm, tn):
    m, n = pairs[0][0].shape[0], pairs[0][1].shape[1]
    npairs = len(pairs)

    def body(*refs):
        o_ref = refs[2 * npairs]
        acc = None
        for i in range(npairs):
            t = _dot(refs[2 * i][...].astype(BF16), refs[2 * i + 1][...].astype(BF16))
            acc = t if acc is None else acc + t
        o_ref[...] = acc.astype(out_dtype)

    in_specs, args = [], []
    for a, b in pairs:
        k = a.shape[1]
        in_specs += [pl.BlockSpec((tm, k), lambda i, j: (i, 0)), pl.BlockSpec((k, tn), lambda i, j: (0, j))]
        args += [a, b]
    return pl.pallas_call(
        body, grid=(m // tm, n // tn), in_specs=in_specs,
        out_specs=pl.BlockSpec((tm, tn), lambda i, j: (i, j)),
        out_shape=jax.ShapeDtypeStruct((m, n), out_dtype), name=name,
        compiler_params=_cp(("parallel", "parallel")))(*args)


def _rms_fwd(h, g, name):
    def body(h_ref, g_ref, o_ref):
        x = h_ref[...]
        r = lax.rsqrt(jnp.mean(x * x, axis=1, keepdims=True) + NORM_EPS)
        o_ref[...] = (x * r * g_ref[...]).astype(BF16)

    return pl.pallas_call(
        body, grid=(NB,),
        in_specs=[pl.BlockSpec((TB, D), lambda n: (n, 0)), pl.BlockSpec((1, D), lambda n: (0, 0))],
        out_specs=pl.BlockSpec((TB, D), lambda n: (n, 0)),
        out_shape=jax.ShapeDtypeStruct((R, D), BF16), name=name, compiler_params=_cp(("parallel",)))(h, g)


def _post_fwd(h, y, g, name):
    def body(h_ref, y_ref, g_ref, o_ref):
        yv = y_ref[...]
        r = lax.rsqrt(jnp.mean(yv * yv, axis=1, keepdims=True) + NORM_EPS)
        o_ref[...] = h_ref[...] + yv * r * g_ref[...]

    blk = pl.BlockSpec((TB, D), lambda n: (n, 0))
    return pl.pallas_call(
        body, grid=(NB,), in_specs=[blk, blk, pl.BlockSpec((1, D), lambda n: (0, 0))], out_specs=blk,
        out_shape=jax.ShapeDtypeStruct((R, D), F32), name=name, compiler_params=_cp(("parallel",)))(h, y, g)


def _rms_bwd(dout, x, g, res, out_dtype, name):
    has_res = res is not None

    def body(*refs):
        if has_res:
            d_ref, x_ref, g_ref, r_ref, dx_ref, dg_ref = refs
        else:
            d_ref, x_ref, g_ref, dx_ref, dg_ref = refs
        n = pl.program_id(0)
        xv = x_ref[...]
        dv = d_ref[...]
        r = lax.rsqrt(jnp.mean(xv * xv, axis=1, keepdims=True) + NORM_EPS)
        xh = xv * r
        dxh = dv * g_ref[...]
        dx = r * (dxh - xh * jnp.mean(dxh * xh, axis=1, keepdims=True))
        if has_res:
            dx = dx + r_ref[...]
        row = lax.broadcasted_iota(jnp.int32, (TB, D), 0) + n * TB
        dx_ref[...] = jnp.where(row >= PAD, dx, 0.0).astype(out_dtype)

        @pl.when(n == 0)
        def _():
            dg_ref[...] = jnp.zeros_like(dg_ref)

        dg_ref[...] += jnp.sum(dv * xh, axis=0, keepdims=True)

    blk = pl.BlockSpec((TB, D), lambda n: (n, 0))
    vec = pl.BlockSpec((1, D), lambda n: (0, 0))
    ins = [dout, x, g] + ([res] if has_res else [])
    in_specs = [blk, blk, vec] + ([blk] if has_res else [])
    return pl.pallas_call(
        body, grid=(NB,), in_specs=in_specs, out_specs=[blk, vec],
        out_shape=[jax.ShapeDtypeStruct((R, D), out_dtype), jax.ShapeDtypeStruct((1, D), F32)],
        name=name, compiler_params=_cp(("arbitrary",)))(*ins)


GW = 512


def _gate_fwd(o, gsrc, goff, name):
    w = o.shape[1]

    def body(o_ref, g_ref, m_ref):
        gv = g_ref[...]
        m_ref[...] = (o_ref[...] * (gv * _sig(gv))).astype(BF16)

    gb = goff // GW
    return pl.pallas_call(
        body, grid=(NB, w // GW),
        in_specs=[pl.BlockSpec((TB, GW), lambda n, j: (n, j)), pl.BlockSpec((TB, GW), lambda n, j: (n, gb + j))],
        out_specs=pl.BlockSpec((TB, GW), lambda n, j: (n, j)),
        out_shape=jax.ShapeDtypeStruct((R, w), BF16), name=name,
        compiler_params=_cp(("parallel", "parallel")))(o, gsrc)


def _gate_bwd(dsrc, doff, o, gsrc, goff, name):
    w = o.shape[1]

    def body(d_ref, o_ref, g_ref, do_ref, dg_ref):
        gv = g_ref[...]
        dv = d_ref[...]
        s = _sig(gv)
        do_ref[...] = dv * (gv * s)
        dg_ref[...] = (dv * o_ref[...] * (s * (1.0 + gv * (1.0 - s)))).astype(BF16)

    db, gb = doff // GW, goff // GW
    blk = pl.BlockSpec((TB, GW), lambda n, j: (n, j))
    return pl.pallas_call(
        body, grid=(NB, w // GW),
        in_specs=[pl.BlockSpec((TB, GW), lambda n, j: (n, db + j)), blk,
                  pl.BlockSpec((TB, GW), lambda n, j: (n, gb + j))],
        out_specs=[blk, blk],
        out_shape=[jax.ShapeDtypeStruct((R, w), F32), jax.ShapeDtypeStruct((R, w), BF16)], name=name,
        compiler_params=_cp(("parallel", "parallel")))(dsrc, o, gsrc)


def _loss_bwd(h, target):
    def body(h_ref, t_ref, d_ref, l_ref):
        n = pl.program_id(0)

        @pl.when(n == 0)
        def _():
            d_ref[...] = jnp.zeros_like(d_ref)
            l_ref[...] = jnp.zeros_like(l_ref)

        @pl.when(n > 0)
        def _():
            err = h_ref[...] - t_ref[...]
            d_ref[...] = err * (1.0 / D)
            l_ref[...] += jnp.sum(err * err, axis=0, keepdims=True)

        @pl.when(n == NB - 1)
        def _():
            tot = jnp.sum(l_ref[...], axis=1, keepdims=True) * (0.5 / D)
            l_ref[...] = jnp.broadcast_to(tot, (1, D))

    blk = pl.BlockSpec((TB, D), lambda n: (n, 0))
    return pl.pallas_call(
        body, grid=(NB,),
        in_specs=[blk, pl.BlockSpec((TB, D), lambda n: (jnp.maximum(n - 1, 0), 0))],
        out_specs=[blk, pl.BlockSpec((1, D), lambda n: (0, 0))],
        out_shape=[jax.ShapeDtypeStruct((R, D), F32), jax.ShapeDtypeStruct((1, D), F32)],
        name="loss_bwd", compiler_params=_cp(("arbitrary",)))(h, target)


def _lane_row(shape):
    return lax.broadcasted_iota(jnp.int32, shape, 1), lax.broadcasted_iota(jnp.int32, shape, 0)


def _rot_half(x, lane):
    return jnp.where(lane % HD < HD // 2, pltpu.roll(x, 128 - HD // 2, 1), pltpu.roll(x, HD // 2, 1))


def _swa_blocks(n):
    return (0, jnp.maximum(n - 1, 0), n)


def _swa_masks(n, lane, row):
    qpos = n * TB + row
    kp = (n - 1) * TB + lane
    kc = n * TB + lane
    m0 = (lane >= PAD) & (qpos - lane >= TB)
    mp = (kp >= PAD) & (qpos >= kp) & (qpos - kp < TB)
    mc = (kc >= PAD) & (qpos >= kc)
    return (m0, mp, mc)


def _swa_load(n, zq_ref, zkv_ref, cs_ref, sn_ref, lane):
    r0 = pl.multiple_of(n * TB, TB)
    csq, snq = cs_ref[pl.ds(r0, TB), :], sn_ref[pl.ds(r0, TB), :]
    qc = []
    for c in range(4):
        x = zq_ref[:, c * 128:(c + 1) * 128]
        qc.append(x * csq + _rot_half(x, lane) * snq)
    kvs = []
    for b in _swa_blocks(n):
        b0 = pl.multiple_of(b * TB, TB)
        csb, snb = cs_ref[pl.ds(b0, TB), :], sn_ref[pl.ds(b0, TB), :]
        kx = zkv_ref[pl.ds(b0, TB), 0:128]
        kr = kx * csb + _rot_half(kx, lane) * snb
        vx = zkv_ref[pl.ds(b0, TB), 128:256]
        kvs.append((kr.astype(BF16), pltpu.roll(kr, HD, 1).astype(BF16),
                    vx.astype(BF16), pltpu.roll(vx, HD, 1).astype(BF16), csb, snb, b0))
    return qc, (csq, snq), kvs


def _swa_fwd(z0, cs, sn, sinks):
    def body(zq_ref, zkv_ref, cs_ref, sn_ref, sk_ref, o_ref, lse_ref):
        n = pl.program_id(0)
        lane, row = _lane_row((TB, 128))
        lo = lane < HD
        masks = _swa_masks(n, lane, row)
        qc, _, kvs = _swa_load(n, zq_ref, zkv_ref, cs_ref, sn_ref, lane)
        oh = []
        lse_t = jnp.zeros((TB, 128), F32)
        for h in range(SWA_HEADS):
            c, par, g = h // 2, h % 2, h // 4
            half = lo if par == 0 else jnp.logical_not(lo)
            qm = jnp.where(half, qc[c], 0.0).astype(BF16)
            sink = sk_ref[0, h]
            ss = []
            for (k, ka, v, va, _, _, _), m in zip(kvs, masks):
                ksel = k if par == g else ka
                ss.append(jnp.where(m, _dot_nt(qm, ksel) * SCALE, NEG))
            mx = jnp.maximum(jnp.maximum(jnp.max(ss[0], axis=1, keepdims=True), jnp.max(ss[1], axis=1, keepdims=True)),
                             jnp.max(ss[2], axis=1, keepdims=True))
            mx = jnp.maximum(mx, sink)
            es = [jnp.exp(s - mx) for s in ss]
            den = (jnp.sum(es[0], axis=1, keepdims=True) + jnp.sum(es[1], axis=1, keepdims=True)
                   + jnp.sum(es[2], axis=1, keepdims=True) + jnp.exp(sink - mx))
            inv = 1.0 / den
            t = jnp.zeros((TB, 128), F32)
            for (k, ka, v, va, _, _, _), e in zip(kvs, es):
                t = t + _dot((e * inv).astype(BF16), v if par == g else va)
            oh.append(t)
            lse_t = jnp.where(lane == h, mx + jnp.log(den), lse_t)
        oc = [jnp.where(lo, oh[2 * c], oh[2 * c + 1]) for c in range(4)]
        for c in range(4):
            o_ref[:, c * 128:(c + 1) * 128] = oc[c]
        lse_ref[...] = lse_t

    full = pl.BlockSpec((R, 128), lambda n: (0, 0))
    return pl.pallas_call(
        body, grid=(NB,),
        in_specs=[pl.BlockSpec((TB, 512), lambda n: (n, C_Q // 512)),
                  pl.BlockSpec((R, 256), lambda n: (0, C_K // 256)), full, full,
                  pl.BlockSpec(memory_space=pltpu.SMEM)],
        out_specs=[pl.BlockSpec((TB, 512), lambda n: (n, 0)), pl.BlockSpec((TB, 128), lambda n: (n, 0))],
        out_shape=[jax.ShapeDtypeStruct((R, 512), F32), jax.ShapeDtypeStruct((R, 128), F32)],
        name="swa_fwd", compiler_params=_cp(("parallel",)))(z0, z0, cs, sn, sinks)


def _swa_bwd(z0, cs, sn, sinks, o, do, lse):
    def body(zq_ref, zkv_ref, cs_ref, sn_ref, sk_ref, o_ref, do_ref, lse_ref, dq_ref, dkv_ref, dsk_ref):
        n = pl.program_id(0)

        @pl.when(n == 0)
        def _():
            dkv_ref[...] = jnp.zeros_like(dkv_ref)
            dsk_ref[...] = jnp.zeros_like(dsk_ref)

        lane, row = _lane_row((TB, 128))
        lo = lane < HD
        masks = _swa_masks(n, lane, row)
        qc, (csq, snq), kvs = _swa_load(n, zq_ref, zkv_ref, cs_ref, sn_ref, lane)
        lse_t = lse_ref[...]
        dqh = []
        dk_al = [jnp.zeros((TB, 128), F32) for _ in range(3)]
        dk_mis = [jnp.zeros((TB, 128), F32) for _ in range(3)]
        dv_al = [jnp.zeros((TB, 128), F32) for _ in range(3)]
        dv_mis = [jnp.zeros((TB, 128), F32) for _ in range(3)]
        dsk_t = jnp.zeros((TB, 128), F32)
        for h in range(SWA_HEADS):
            c, par, g = h // 2, h % 2, h // 4
            half = lo if par == 0 else jnp.logical_not(lo)
            qm = jnp.where(half, qc[c], 0.0).astype(BF16)
            dov = do_ref[:, c * 128:(c + 1) * 128]
            dom = jnp.where(half, dov, 0.0)
            delta = jnp.sum(dom * o_ref[:, c * 128:(c + 1) * 128], axis=1, keepdims=True)
            dob = dom.astype(BF16)
            lse_h = jnp.sum(jnp.where(lane == h, lse_t, 0.0), axis=1, keepdims=True)
            sink = sk_ref[0, h]
            dqt = jnp.zeros((TB, 128), F32)
            for bi, ((k, ka, v, va, _, _, _), m) in enumerate(zip(kvs, masks)):
                ksel = k if par == g else ka
                vsel = v if par == g else va
                s = jnp.where(m, _dot_nt(qm, ksel) * SCALE, NEG)
                p = jnp.exp(s - lse_h)
                dp = _dot_nt(dob, vsel)
                ds = (p * (dp - delta) * SCALE).astype(BF16)
                dqt = dqt + _dot(ds, ksel)
                dkh = _dot_tn(ds, qm)
                dvh = _dot_tn(p.astype(BF16), dob)
                if par == g:
                    dk_al[bi] = dk_al[bi] + dkh
                    dv_al[bi] = dv_al[bi] + dvh
                else:
                    dk_mis[bi] = dk_mis[bi] + dkh
                    dv_mis[bi] = dv_mis[bi] + dvh
            dqh.append(dqt)
            dsk_t = jnp.where(lane == h, -jnp.exp(sink - lse_h) * delta, dsk_t)
        dqc = [jnp.where(lo, dqh[2 * c], dqh[2 * c + 1]) for c in range(4)]
        for c in range(4):
            t = dqc[c] * snq
            dq_ref[:, c * 128:(c + 1) * 128] = dqc[c] * csq + _rot_half(t, lane)
        for bi, (_, _, _, _, csb, snb, b0) in enumerate(kvs):
            dk = dk_al[bi] + pltpu.roll(dk_mis[bi], HD, 1)
            dv = dv_al[bi] + pltpu.roll(dv_mis[bi], HD, 1)
            dkv_ref[pl.ds(b0, TB), 0:128] += dk * csb + _rot_half(dk * snb, lane)
            dkv_ref[pl.ds(b0, TB), 128:256] += dv
        dsk_ref[0:1, :] += jnp.sum(dsk_t, axis=0, keepdims=True)

    full = pl.BlockSpec((R, 128), lambda n: (0, 0))
    b512 = pl.BlockSpec((TB, 512), lambda n: (n, 0))
    return pl.pallas_call(
        body, grid=(NB,),
        in_specs=[pl.BlockSpec((TB, 512), lambda n: (n, C_Q // 512)),
                  pl.BlockSpec((R, 256), lambda n: (0, C_K // 256)), full, full,
                  pl.BlockSpec(memory_space=pltpu.SMEM), b512, b512, pl.BlockSpec((TB, 128), lambda n: (n, 0))],
        out_specs=[b512, pl.BlockSpec((R, 256), lambda n: (0, 0)), pl.BlockSpec((8, 128), lambda n: (0, 0))],
        out_shape=[jax.ShapeDtypeStruct((R, 512), F32), jax.ShapeDtypeStruct((R, 256), F32),
                   jax.ShapeDtypeStruct((8, 128), F32)],
        name="swa_bwd", compiler_params=_cp(("arbitrary",)))(z0, z0, cs, sn, sinks, o, do, lse)


CC = 512
HALO = CONV_W - 1


def _conv_fwd(z0, conv_w, conv_b, ln_g, ln_b):
    def body(g_ref, w_ref, cb_ref, lg_ref, lb_ref, cv_ref, s_ref, ubuf):
        n = pl.program_id(0)

        @pl.when(n == 0)
        def _():
            ubuf[0:TB, :] = jnp.zeros((TB, CC), F32)

        u = g_ref[:, 0:CC] * _sig(g_ref[:, CC:2 * CC])
        ubuf[TB:2 * TB, :] = u
        acc = jnp.zeros((TB, CC), F32)
        for w in range(CONV_W):
            acc = acc + ubuf[pl.ds(TB - HALO + w, TB), :] * w_ref[w:w + 1, :]
        cv = acc + cb_ref[...]
        cv_ref[...] = cv
        xc = cv - jnp.mean(cv, axis=1, keepdims=True)
        rs = lax.rsqrt(jnp.mean(xc * xc, axis=1, keepdims=True) + LN_EPS)
        ln = xc * rs * lg_ref[...] + lb_ref[...]
        s_ref[...] = (ln * _sig(ln)).astype(BF16)
        ubuf[0:TB, :] = u

    vec = pl.BlockSpec((1, CC), lambda n: (0, 0))
    blk = pl.BlockSpec((TB, CC), lambda n: (n, 0))
    return pl.pallas_call(
        body, grid=(NB,),
        in_specs=[pl.BlockSpec((TB, 2 * CC), lambda n: (n, C_GLU // (2 * CC))),
                  pl.BlockSpec((32, CC), lambda n: (0, 0)), vec, vec, vec],
        out_specs=[blk, blk],
        out_shape=[jax.ShapeDtypeStruct((R, CC), F32), jax.ShapeDtypeStruct((R, CC), BF16)],
        scratch_shapes=[pltpu.VMEM((2 * TB, CC), F32)],
        name="conv_fwd", compiler_params=_cp(("arbitrary",)))(z0, conv_w, conv_b, ln_g, ln_b)


def _conv_bwd(ds, cv, z0, conv_w, ln_g, ln_b):
    def body(ds_ref, cv_ref, g_ref, w_ref, lg_ref, lb_ref, dglu_ref, dw_ref, dsm_ref, dbuf):
        n = pl.program_id(0)

        @pl.when(n == 0)
        def _():
            dbuf[TB:2 * TB, :] = jnp.zeros((TB, CC), F32)
            dw_ref[...] = jnp.zeros_like(dw_ref)
            dsm_ref[...] = jnp.zeros_like(dsm_ref)

        cv = cv_ref[...]
        xc = cv - jnp.mean(cv, axis=1, keepdims=True)
        rs = lax.rsqrt(jnp.mean(xc * xc, axis=1, keepdims=True) + LN_EPS)
        xh = xc * rs
        ln = xh * lg_ref[...] + lb_ref[...]
        sg = _sig(ln)
        dln = ds_ref[...] * (sg * (1.0 + ln * (1.0 - sg)))
        dxh = dln * lg_ref[...]
        dcv = rs * (dxh - jnp.mean(dxh, axis=1, keepdims=True) - xh * jnp.mean(dxh * xh, axis=1, keepdims=True))
        dsm_ref[0:1, :] += jnp.sum(dcv, axis=0, keepdims=True)
        dsm_ref[1:2, :] += jnp.sum(dln * xh, axis=0, keepdims=True)
        dsm_ref[2:3, :] += jnp.sum(dln, axis=0, keepdims=True)
        dbuf[0:TB, :] = dcv
        a = g_ref[:, 0:CC]
        sb = _sig(g_ref[:, CC:2 * CC])
        u = a * sb
        du = jnp.zeros((TB, CC), F32)
        for w in range(CONV_W):
            sh = dbuf[pl.ds(HALO - w, TB), :]
            du = du + sh * w_ref[w:w + 1, :]
            dw_ref[w:w + 1, :] += jnp.sum(u * sh, axis=0, keepdims=True)
        dglu_ref[:, 0:CC] = (du * sb).astype(BF16)
        dglu_ref[:, CC:2 * CC] = (du * a * sb * (1.0 - sb)).astype(BF16)
        dbuf[TB:2 * TB, :] = dcv

    rev = lambda n: (NB - 1 - n, 0)
    vec = pl.BlockSpec((1, CC), lambda n: (0, 0))
    blk = pl.BlockSpec((TB, CC), rev)
    return pl.pallas_call(
        body, grid=(NB,),
        in_specs=[blk, blk, pl.BlockSpec((TB, 2 * CC), lambda n: (NB - 1 - n, C_GLU // (2 * CC))),
                  pl.BlockSpec((32, CC), lambda n: (0, 0)), vec, vec],
        out_specs=[pl.BlockSpec((TB, 2 * CC), rev), pl.BlockSpec((32, CC), lambda n: (0, 0)),
                   pl.BlockSpec((8, CC), lambda n: (0, 0))],
        out_shape=[jax.ShapeDtypeStruct((R, 2 * CC), BF16), jax.ShapeDtypeStruct((32, CC), F32),
                   jax.ShapeDtypeStruct((8, CC), F32)],
        scratch_shapes=[pltpu.VMEM((2 * TB, CC), F32)],
        name="conv_bwd", compiler_params=_cp(("arbitrary",)))(ds, cv, z0, conv_w, ln_g, ln_b)


def _split_dot(x, t):
    hi = x.astype(BF16)
    lo = (x - hi.astype(F32)).astype(BF16)
    return _dot(hi, t) + _dot(lo, t)


def _stack_heads(x):
    lane = lax.broadcasted_iota(jnp.int32, (TB, 128), 1)
    return jnp.concatenate([jnp.where(lane < HD, x, 0.0), jnp.where(lane < HD, 0.0, x)], axis=0).astype(BF16)


def _sb_stack(qv, i):
    lane2, row2 = _lane_row((2 * TB, 128))
    qpos2 = i * TB + (row2 & (TB - 1))
    lane, row = _lane_row((TB, 128))
    return _stack_heads(qv), lane2, qpos2, (row > lane).astype(BF16)


SB_U = 3
SB_DEAD = -104.0


def _sb_fwd(q, k, v):
    def body(q_ref, k_ref, v_ref, o_ref, c_ref, n_ref):
        p, i = pl.program_id(0), pl.program_id(1)
        lane, row = _lane_row((TB, 128))
        lo = lane < HD
        q2, lane2, qpos2, tri_gt = _sb_stack(q_ref[...].astype(F32), i)

        def cond(st):
            t, _, c2 = st
            return jnp.logical_and(i - SB_U * t >= 0, jnp.max(c2) > SB_DEAD)

        def step(st):
            t, acc, c2 = st
            jrs = [i - SB_U * t - u for u in range(SB_U)]
            j0s = [pl.multiple_of(jnp.maximum(jr, 0) * TB, TB) for jr in jrs]
            ks = [k_ref[pl.ds(j0, TB), :] for j0 in j0s]
            zs = [_dot_nt(q2, kj) * SCALE for kj in ks]
            valids, lbs, l1s = [], [], []
            for jr, z in zip(jrs, zs):
                kpos = jr * TB + lane2
                valid = (kpos >= PAD) & (kpos < qpos2)
                lb = jnp.minimum(z, 0.0) - jnp.log(1.0 + jnp.exp(-jnp.abs(z)))
                valids.append(valid)
                lbs.append(lb)
                l1s.append(jnp.where(valid, lb - z, 0.0))
            sfxs = [_split_dot(l1, tri_gt) for l1 in l1s]
            carries = []
            for jr, l1 in zip(jrs, l1s):
                carries.append(c2)
                c_ref[...] = jnp.where(lane == 2 * jr, c2[0:TB], jnp.where(lane == 2 * jr + 1, c2[TB:2 * TB], c_ref[...]))
                c2 = c2 + jnp.sum(l1, axis=1, keepdims=True)
            for j0, valid, lb, sfx, cu in zip(j0s, valids, lbs, sfxs, carries):
                a = jnp.where(valid, jnp.exp(lb + sfx + cu), 0.0).astype(BF16)
                av = _dot(a, v_ref[pl.ds(j0, TB), :])
                acc = acc + jnp.where(lo, av[0:TB], av[TB:2 * TB])
            return t + 1, acc, c2

        c_ref[...] = jnp.zeros((TB, 128), F32)
        t, acc, _ = lax.while_loop(cond, step, (jnp.int32(0), jnp.zeros((TB, 128), F32), jnp.zeros((2 * TB, 1), F32)))
        o_ref[...] = acc
        n_ref[p, i] = t

    slab = pl.BlockSpec((R, 128), lambda p, i: (0, p))
    blk = pl.BlockSpec((TB, 128), lambda p, i: (i, p))
    sd = jax.ShapeDtypeStruct((R, D), F32)
    return pl.pallas_call(
        body, grid=(D // 128, NB), in_specs=[blk, slab, slab],
        out_specs=[blk, blk, pl.BlockSpec(memory_space=pltpu.SMEM)],
        out_shape=[sd, sd, jax.ShapeDtypeStruct((D // 128, NB), jnp.int32)], name="sb_fwd",
        compiler_params=_cp(("arbitrary", "arbitrary")))(q, k, v)


def _sb_bwd(trips, q, k, v, car, do):
    def body(n_ref, q_ref, k_ref, v_ref, c_ref, do_ref, dq_ref, dk_ref, dv_ref):
        p, i = pl.program_id(0), pl.program_id(1)

        @pl.when(i == 0)
        def _():
            dk_ref[...] = jnp.zeros_like(dk_ref)
            dv_ref[...] = jnp.zeros_like(dv_ref)

        lane, row = _lane_row((TB, 128))
        lo = lane < HD
        tri_lt = (row < lane).astype(BF16)
        q2, lane2, qpos2, tri_gt = _sb_stack(q_ref[...].astype(F32), i)
        do2 = _stack_heads(do_ref[...])
        ct = c_ref[...]
        trips_i = n_ref[p, i]
        first = jnp.maximum(i + 1 - SB_U * trips_i, 0)

        def step(t, carry):
            dq, g2 = carry
            jrs = [first + SB_U * t + u for u in range(SB_U)]
            j0s = [pl.multiple_of(jnp.minimum(jr, i) * TB, TB) for jr in jrs]
            ks = [k_ref[pl.ds(j0, TB), :] for j0 in j0s]
            vs = [v_ref[pl.ds(j0, TB), :] for j0 in j0s]
            zs = [_dot_nt(q2, kj) * SCALE for kj in ks]
            das = [_dot_nt(do2, vj) for vj in vs]
            valids, es, lbs, l1s = [], [], [], []
            for jr, z in zip(jrs, zs):
                kpos = jr * TB + lane2
                valid = (kpos >= PAD) & (kpos < qpos2)
                e = jnp.exp(-jnp.abs(z))
                lb = jnp.minimum(z, 0.0) - jnp.log(1.0 + e)
                valids.append(valid)
                es.append(e)
                lbs.append(lb)
                l1s.append(jnp.where(valid, lb - z, 0.0))
            sfxs = [_split_dot(l1, tri_gt) for l1 in l1s]
            a_s, gmats, gpre = [], [], []
            for jr, valid, lb, sfx, da in zip(jrs, valids, lbs, sfxs, das):
                later = jnp.concatenate(
                    [jnp.sum(jnp.where(lane == 2 * jr + hh, ct, 0.0), axis=1, keepdims=True) for hh in range(2)], axis=0)
                a = jnp.where(valid, jnp.exp(lb + sfx + later), 0.0)
                gmat = da * a
                a_s.append(a.astype(BF16))
                gmats.append(gmat)
                gpre.append(g2)
                g2 = g2 + jnp.sum(gmat, axis=1, keepdims=True)
            pres = [gp + _split_dot(gmat, tri_lt) for gp, gmat in zip(gpre, gmats)]
            for j0, kj, valid, z, e, gmat, pre, a in zip(j0s, ks, valids, zs, es, gmats, pres, a_s):
                r = 1.0 / (1.0 + e)
                big = z >= 0.0
                beta = jnp.where(big, r, e * r)
                omb = jnp.where(big, e * r, r)
                dz = (jnp.where(valid, gmat * omb - beta * pre, 0.0) * SCALE).astype(BF16)
                dq2 = _dot(dz, kj)
                dq = dq + jnp.where(lo, dq2[0:TB], dq2[TB:2 * TB])
                dk_ref[pl.ds(j0, TB), :] += _dot_tn(dz, q2)
                dv_ref[pl.ds(j0, TB), :] += _dot_tn(a, do2)
            return dq, g2

        dq, _ = lax.fori_loop(0, trips_i, step, (jnp.zeros((TB, 128), F32), jnp.zeros((2 * TB, 1), F32)))
        dq_ref[...] = dq

    slab = pl.BlockSpec((R, 128), lambda p, i: (0, p))
    blk = pl.BlockSpec((TB, 128), lambda p, i: (i, p))
    sd = jax.ShapeDtypeStruct((R, D), F32)
    return pl.pallas_call(
        body, grid=(D // 128, NB),
        in_specs=[pl.BlockSpec(memory_space=pltpu.SMEM), blk, slab, slab, blk, blk], out_specs=[blk, slab, slab],
        out_shape=[sd, sd, sd], name="sb_bwd",
        compiler_params=_cp(("arbitrary", "arbitrary")))(trips, q, k, v, car, do)


def _adamw(w, parts, m, v, name):
    rows, cols = w.shape
    tr = 256 if rows % 256 == 0 else rows
    nparts = len(parts)

    def body(*refs):
        w_ref = refs[0]
        p_refs = refs[1:1 + nparts]
        m_ref, v_ref, g_ref, d_ref, nm_ref, nv_ref = refs[1 + nparts:]
        g = p_refs[0][...]
        for p_ref in p_refs[1:]:
            g = g + p_ref[...]
        nm = ADAM_B1 * m_ref[...] + (1.0 - ADAM_B1) * g
        nv = ADAM_B2 * v_ref[...] + (1.0 - ADAM_B2) * (g * g)
        m_hat = nm / (1.0 - ADAM_B1 ** ADAM_STEP)
        v_hat = nv / (1.0 - ADAM_B2 ** ADAM_STEP)
        g_ref[...] = g
        d_ref[...] = -ADAM_LR * (m_hat / (jnp.sqrt(v_hat) + ADAM_EPS) + ADAM_WD * w_ref[...])
        nm_ref[...] = nm
        nv_ref[...] = nv

    blk = pl.BlockSpec((tr, cols), lambda i: (i, 0))
    sd = jax.ShapeDtypeStruct((rows, cols), F32)
    return pl.pallas_call(
        body, grid=(rows // tr,), in_specs=[blk] * (3 + nparts), out_specs=[blk] * 4, out_shape=[sd] * 4,
        name=name, compiler_params=_cp(("parallel",)))(w, *parts, m, v)


def _sum8(buf, name):
    _, rows, cols = buf.shape

    def body(b_ref, o_ref):
        acc = b_ref[0]
        for i in range(1, 8):
            acc = acc + b_ref[i]
        o_ref[...] = acc

    return pl.pallas_call(
        body, out_shape=jax.ShapeDtypeStruct((rows, cols), F32), name=name,
        compiler_params=pltpu.CompilerParams(vmem_limit_bytes=VMEM_LIMIT))(buf)


MESH = pl.DeviceIdType.MESH
ANY = pl.BlockSpec(memory_space=pl.ANY)


def _chip_peers():
    x, y = lax.axis_index("x"), lax.axis_index("y")
    return [(1 - x, y), (x, 1 - y), (1 - x, 1 - y)]


def _gather_chips(shards):
    n = len(shards)
    shards = [s.reshape((2, s.shape[0] // 2) + s.shape[1:]) for s in shards]

    def body(*refs):
        ins, outs = refs[:n], refs[n:2 * n]
        s1, r1, s2, r2 = refs[2 * n:]
        x, y, c = lax.axis_index("x"), lax.axis_index("y"), lax.axis_index("c")
        me = 2 * x + y
        peers = _chip_peers()

        def half(ref, a, cc):
            return ref.at[cc]

        first = []
        for j, (px, py) in enumerate(peers):
            for a in range(n):
                first.append(pltpu.make_async_remote_copy(
                    src_ref=half(ins[a], a, c), dst_ref=half(outs[a].at[me], a, c), send_sem=s1.at[j * n + a],
                    recv_sem=r1.at[j * n + a], device_id=(px, py, c), device_id_type=MESH))
        for cp in first:
            cp.start()
        passed = []
        for j, (px, py) in enumerate(peers):
            for a in range(n):
                got = half(outs[a].at[2 * px + py], a, c)
                pltpu.make_async_remote_copy(
                    src_ref=got, dst_ref=got, send_sem=s1.at[j * n + a], recv_sem=r1.at[j * n + a],
                    device_id=(px, py, c), device_id_type=MESH).wait_recv()
                cp = pltpu.make_async_remote_copy(
                    src_ref=got, dst_ref=got, send_sem=s2.at[j * n + a], recv_sem=r2.at[j * n + a],
                    device_id=(x, y, 1 - c), device_id_type=MESH)
                cp.start()
                passed.append(cp)
        for j, (px, py) in enumerate(peers):
            for a in range(n):
                theirs = half(outs[a].at[2 * px + py], a, 1 - c)
                pltpu.make_async_remote_copy(
                    src_ref=theirs, dst_ref=theirs, send_sem=s2.at[j * n + a], recv_sem=r2.at[j * n + a],
                    device_id=(x, y, 1 - c), device_id_type=MESH).wait_recv()
        for cp in first + passed:
            cp.wait_send()

    res = pl.pallas_call(
        body, in_specs=[ANY] * n, out_specs=[ANY] * n,
        out_shape=[jax.ShapeDtypeStruct((N_CHIPS,) + s.shape, s.dtype) for s in shards],
        scratch_shapes=[pltpu.SemaphoreType.DMA((3 * n,))] * 4, name="gather_chips")(*shards)
    me = 2 * lax.axis_index("x") + lax.axis_index("y")
    res = [lax.dynamic_update_index_in_dim(r, s, me, 0) for r, s in zip(res, shards)]
    return [r.reshape((N_CHIPS, 2 * r.shape[2]) + r.shape[3:]) for r in res]


def _pair_exchange(grads):
    n = len(grads)
    hs = [g.shape[1] // 2 for g in grads]
    grads = [g.reshape((N_CHIPS, 2, h) + g.shape[2:]) for g, h in zip(grads, hs)]

    def body(*refs):
        ins, got = refs[:n], refs[n:2 * n]
        ssem, rsem = refs[2 * n:]
        x, y, c = lax.axis_index("x"), lax.axis_index("y"), lax.axis_index("c")
        sends = [pltpu.make_async_remote_copy(
            src_ref=ins[a].at[:, 1 - c], dst_ref=got[a], send_sem=ssem.at[a],
            recv_sem=rsem.at[a], device_id=(x, y, 1 - c), device_id_type=MESH) for a in range(n)]
        for cp in sends:
            cp.start()
        for cp in sends:
            cp.wait()

    half_shapes = [jax.ShapeDtypeStruct((N_CHIPS, h) + g.shape[3:], g.dtype) for g, h in zip(grads, hs)]
    got = pl.pallas_call(
        body, in_specs=[ANY] * n, out_specs=[ANY] * n, out_shape=half_shapes,
        scratch_shapes=[pltpu.SemaphoreType.DMA((n,))] * 2, name="pair_exchange")(*grads)
    c = lax.axis_index("c")
    own = [lax.dynamic_index_in_dim(g, c, 1, keepdims=False) for g in grads]
    return own, got


def _sum_pair(own, got, send_dtype, name):
    _, rows, cols = own.shape
    tr = 256 if rows % 256 == 0 else rows

    def body(a_ref, b_ref, f_ref, s_ref):
        t = a_ref[...].astype(F32) + b_ref[...].astype(F32)
        f_ref[...] = t
        s_ref[...] = t.astype(send_dtype)

    blk = pl.BlockSpec((N_CHIPS, tr, cols), lambda i: (0, i, 0))
    return pl.pallas_call(
        body, grid=(rows // tr,), in_specs=[blk, blk], out_specs=[blk, blk],
        out_shape=[jax.ShapeDtypeStruct(own.shape, F32), jax.ShapeDtypeStruct(own.shape, send_dtype)],
        name=name, compiler_params=_cp(("parallel",)))(own, got)


def _scatter_chips(keep, send):
    n = len(send)

    def body(*refs):
        sin, land = refs[:n], refs[n:2 * n]
        ssem, rsem = refs[2 * n:]
        c = lax.axis_index("c")
        sends = []
        for j, (px, py) in enumerate(_chip_peers()):
            for a in range(n):
                sends.append(pltpu.make_async_remote_copy(
                    src_ref=sin[a].at[2 * px + py], dst_ref=land[a].at[j], send_sem=ssem.at[j * n + a],
                    recv_sem=rsem.at[j * n + a], device_id=(px, py, c), device_id_type=MESH))
        for cp in sends:
            cp.start()
        for cp in sends:
            cp.wait()

    land = pl.pallas_call(
        body, in_specs=[ANY] * n, out_specs=[ANY] * n,
        out_shape=[jax.ShapeDtypeStruct((3,) + s.shape[1:], s.dtype) for s in send],
        scratch_shapes=[pltpu.SemaphoreType.DMA((3 * n,))] * 2, name="scatter_chips")(*send)
    me = 2 * lax.axis_index("x") + lax.axis_index("y")
    mine = [lax.dynamic_index_in_dim(k, me, 0, keepdims=False) for k in keep]
    return mine, land


def _sum_shard(mine, land, name):
    rows, cols = mine.shape
    tr = 256 if rows % 256 == 0 else rows

    def body(m_ref, l_ref, o_ref):
        o_ref[...] = ((m_ref[...] + l_ref[0].astype(F32)) + l_ref[1].astype(F32)) + l_ref[2].astype(F32)

    return pl.pallas_call(
        body, grid=(rows // tr,),
        in_specs=[pl.BlockSpec((tr, cols), lambda i: (i, 0)), pl.BlockSpec((3, tr, cols), lambda i: (0, i, 0))],
        out_specs=pl.BlockSpec((tr, cols), lambda i: (i, 0)), out_shape=jax.ShapeDtypeStruct((rows, cols), F32),
        name=name, compiler_params=_cp(("parallel",)))(mine, land)


def _join_cores(halves):
    n = len(halves)

    def body(*refs):
        ins, outs = refs[:n], refs[n:2 * n]
        ssem, rsem = refs[2 * n:]
        x, y, c = lax.axis_index("x"), lax.axis_index("y"), lax.axis_index("c")
        sends = [pltpu.make_async_remote_copy(
            src_ref=ins[a], dst_ref=outs[a].at[c], send_sem=ssem.at[a], recv_sem=rsem.at[a],
            device_id=(x, y, 1 - c), device_id_type=MESH) for a in range(n)]
        for cp in sends:
            cp.start()
        for a in range(n):
            sends[a].wait_send()
            pltpu.make_async_remote_copy(
                src_ref=ins[a], dst_ref=outs[a].at[1 - c], send_sem=ssem.at[a], recv_sem=rsem.at[a],
                device_id=(x, y, 1 - c), device_id_type=MESH).wait_recv()

    res = pl.pallas_call(
        body, in_specs=[ANY] * n, out_specs=[ANY] * n,
        out_shape=[jax.ShapeDtypeStruct((2,) + h.shape, h.dtype) for h in halves],
        scratch_shapes=[pltpu.SemaphoreType.DMA((n,))] * 2, name="join_cores")(*halves)
    c = lax.axis_index("c")
    res = [lax.dynamic_update_index_in_dim(r, h, c, 0) for r, h in zip(res, halves)]
    return [r.reshape((2 * r.shape[1],) + r.shape[2:]) for r in res]


def _gather_all(vec):
    def body(v_ref, o_ref, lsem, ssem, rsem):
        x, y, c = lax.axis_index("x"), lax.axis_index("y"), lax.axis_index("c")
        me = 4 * x + 2 * y + c
        local = pltpu.make_async_copy(v_ref, o_ref.at[me], lsem)
        local.start()
        cps = []
        for k in range(1, 8):
            px, py, pc = x ^ (k >> 2), y ^ ((k >> 1) & 1), c ^ (k & 1)
            cps.append(pltpu.make_async_remote_copy(
                src_ref=v_ref, dst_ref=o_ref.at[me], send_sem=ssem.at[k - 1], recv_sem=rsem.at[k - 1],
                device_id=(px, py, pc), device_id_type=MESH))
        for cp in cps:
            cp.start()
        for k in range(1, 8):
            px, py, pc = x ^ (k >> 2), y ^ ((k >> 1) & 1), c ^ (k & 1)
            pltpu.make_async_remote_copy(
                src_ref=v_ref, dst_ref=o_ref.at[4 * px + 2 * py + pc], send_sem=ssem.at[k - 1],
                recv_sem=rsem.at[k - 1], device_id=(px, py, pc), device_id_type=MESH).wait_recv()
        for cp in cps:
            cp.wait_send()
        local.wait()

    return pl.pallas_call(
        body, in_specs=[ANY], out_specs=ANY, out_shape=jax.ShapeDtypeStruct((8,) + vec.shape, vec.dtype),
        scratch_shapes=[pltpu.SemaphoreType.DMA, pltpu.SemaphoreType.DMA((7,)), pltpu.SemaphoreType.DMA((7,))],
        name="gather_all")(vec)


def _rope_tables():
    pos = (jnp.arange(R, dtype=jnp.int32) - PAD).astype(F32)
    half = HD // 2
    inv = ROPE_THETA ** (-jnp.arange(half, dtype=F32) / half)
    ang = pos[:, None] * inv[None, :]
    cos, sin = jnp.cos(ang), jnp.sin(ang)
    cs = jnp.tile(cos, (1, 4))
    sn = jnp.tile(jnp.concatenate([-sin, sin], axis=1), (1, 2))
    return cs, sn


def _perm_cols(w):
    return jnp.concatenate([w[:, 0:512], w[:, 768:1280], w[:, 1280:2304], w[:, 2304:2816], w[:, 512:640],
                            w[:, 640:768]], axis=1)


def _unperm_cols(w):
    return jnp.concatenate([w[:, C_Q:C_Q + 512], w[:, C_K:C_K + 128], w[:, C_V:C_V + 128], w[:, C_GA:C_GA + 512],
                            w[:, C_GLU:C_GLU + 1024], w[:, C_GB:C_GB + 512]], axis=1)


def _local_step(x, target, p):
    w0 = _perm_cols(p["ab_w_in"])
    wo0, wpw, wo1 = p["ab_w_out"], p["ab_w_pw2"], p["sb_w_out"]
    w1 = p["sb_w_in"]
    conv_w = jnp.concatenate([p["ab_conv_w"], jnp.zeros((1, CC), F32)], axis=0)
    cs, sn = _rope_tables()

    h0 = jnp.concatenate([jnp.zeros((PAD, D), F32), p["meta_tokens"], x], axis=0)

    xn0 = _rms_fwd(h0, p["ab_pre_norm"], "rms_fwd0")
    z0 = _mm([(xn0, w0)], F32, "in_proj0", 544, 256)
    o0, lse0 = _swa_fwd(z0, cs, sn, p["ab_sinks"])
    a0 = _gate_fwd(o0, z0, C_GA, "gate_a_fwd")
    cv0, s0 = _conv_fwd(z0, conv_w, p["ab_conv_b"], p["ab_conv_ln_g"], p["ab_conv_ln_b"])
    t0 = _mm([(s0, wpw)], F32, "pw2", 544, 512)
    c0 = _gate_fwd(t0, z0, C_GB, "gate_b_fwd")
    mix0 = jnp.concatenate([a0, c0], axis=1)
    y0 = _mm([(mix0, wo0)], F32, "out_proj0", 544, 512)
    h1 = _post_fwd(h0, y0, p["ab_post_norm"], "post_fwd0")

    xn1 = _rms_fwd(h1, p["sb_pre_norm"], "rms_fwd1")
    q1 = _mm([(xn1, w1[0])], BF16, "in_proj1_q", 544, 512)
    k1 = _mm([(xn1, w1[1])], BF16, "in_proj1_k", 544, 512)
    v1 = _mm([(xn1, w1[2])], BF16, "in_proj1_v", 544, 512)
    g1 = _mm([(xn1, w1[3])], F32, "in_proj1_g", 544, 512)
    o1, car1, trips1 = _sb_fwd(q1, k1, v1)
    m1 = _gate_fwd(o1, g1, 0, "gate_sb_fwd")
    y1 = _mm([(m1, wo1)], F32, "out_proj1", 544, 512)
    h2 = _post_fwd(h1, y1, p["sb_post_norm"], "post_fwd1")

    dh2, loss_row = _loss_bwd(h2, target)

    dy1, d_sb_post = _rms_bwd(dh2, y1, p["sb_post_norm"], None, BF16, "post_bwd1")
    dm1 = _mm([(dy1, wo1.T)], F32, "out_proj1_dx", 544, 512)
    d_wo1 = _mm([(m1.T, dy1)], BF16, "out_proj1_dw", 512, 512)
    do1, dg1 = _gate_bwd(dm1, 0, o1, g1, 0, "gate_sb_bwd")
    dq1, dk1, dv1 = _sb_bwd(trips1, q1, k1, v1, car1, do1)
    dz1 = [dq1, dk1, dv1, dg1]
    dxn1 = _mm([(dz1[j], w1[j].T) for j in range(4)], F32, "in_proj1_dx", 544, 512)
    xn1t = xn1.T
    d_w1 = jnp.stack([_mm([(xn1t, dz1[j])], BF16, "in_proj1_dw%d" % j, 512, 512) for j in range(4)])
    dh1, d_sb_pre = _rms_bwd(dxn1, h1, p["sb_pre_norm"], dh2, F32, "rms_bwd1")

    dy0, d_ab_post = _rms_bwd(dh1, y0, p["ab_post_norm"], None, BF16, "post_bwd0")
    dmix0 = _mm([(dy0, wo0.T)], F32, "out_proj0_dx", 544, 512)
    d_wo0 = _mm([(mix0.T, dy0)], BF16, "out_proj0_dw", 512, 512)
    dt0, dgb0 = _gate_bwd(dmix0, 512, t0, z0, C_GB, "gate_b_bwd")
    ds0 = _mm([(dt0, wpw.T)], F32, "pw2_dx", 544, 512)
    d_wpw = _mm([(s0.T, dt0)], BF16, "pw2_dw", 512, 512)
    dglu0, d_convw, d_small = _conv_bwd(ds0, cv0, z0, conv_w, p["ab_conv_ln_g"], p["ab_conv_ln_b"])
    do0, dga0 = _gate_bwd(dmix0, 0, o0, z0, C_GA, "gate_a_bwd")
    dq0, dkv0, d_sinks = _swa_bwd(z0, cs, sn, p["ab_sinks"], o0, do0, lse0)
    dz0 = jnp.concatenate([dq0.astype(BF16), dga0, dglu0, dgb0, dkv0.astype(BF16)], axis=1)
    dxn0 = _mm([(dz0, w0.T)], F32, "in_proj0_dx", 544, 512)
    d_w0 = _unperm_cols(_mm([(xn0.T, dz0)], BF16, "in_proj0_dw", 512, 256))
    dh0, d_ab_pre = _rms_bwd(dxn0, h0, p["ab_pre_norm"], dh1, F32, "rms_bwd0")

    grads = {
        "meta_tokens": dh0[PAD:TB], "ab_pre_norm": d_ab_pre, "ab_w_in": d_w0, "ab_sinks": d_sinks[0:1, 0:8],
        "ab_conv_w": d_convw[0:CONV_W], "ab_conv_b": d_small[0:1], "ab_conv_ln_g": d_small[1:2],
        "ab_conv_ln_b": d_small[2:3], "ab_w_pw2": d_wpw, "ab_w_out": d_wo0, "ab_post_norm": d_ab_post,
        "sb_pre_norm": d_sb_pre, "sb_w_in": d_w1, "sb_w_out": d_wo1, "sb_post_norm": d_sb_post,
    }
    return loss_row, dh0[TB:], grads


SMALL_ROWS = 80
REP_ROWS = 32

WEIGHTS = ["meta_tokens", "ab_pre_norm", "ab_w_in", "ab_sinks", "ab_conv_w", "ab_conv_b", "ab_conv_ln_g",
           "ab_conv_ln_b", "ab_w_pw2", "ab_w_out", "ab_post_norm", "sb_pre_norm", "sb_w_in", "sb_w_out",
           "sb_post_norm"]
BIG = ["ab_w_in", "ab_w_out", "ab_w_pw2", "sb_w_in", "sb_w_out"]


def _pack_small(conv_w, meta, sb_pre, sb_post):
    rows = jnp.concatenate([conv_w, meta.reshape(32, 128), sb_pre.reshape(2, 128), sb_post.reshape(2, 128)], axis=0)
    return jnp.concatenate([rows, jnp.zeros((SMALL_ROWS - rows.shape[0], 128), F32)], axis=0)


def _unpack_small(s):
    return s[0:31], s[31:63].reshape(16, 256), s[63:65].reshape(1, 256), s[65:67].reshape(1, 256)


def _pack_rep(pre, post, conv_b, ln_g, ln_b, sinks):
    flat = jnp.concatenate([pre.reshape(-1), post.reshape(-1), conv_b.reshape(-1), ln_g.reshape(-1),
                            ln_b.reshape(-1), sinks.reshape(-1)])
    flat = jnp.concatenate([flat, jnp.zeros((REP_ROWS * 128 - flat.shape[0],), F32)])
    return flat.reshape(REP_ROWS, 128)


def _unpack_rep(r):
    f = r.reshape(-1)
    return (f[0:1024].reshape(1, 1024), f[1024:2048].reshape(1, 1024), f[2048:2560].reshape(1, 512),
            f[2560:3072].reshape(1, 512), f[3072:3584].reshape(1, 512), f[3584:3592].reshape(1, 8))


def _cols_to_chips(w, width):
    return w.reshape(w.shape[0], N_CHIPS, width).transpose(1, 0, 2)


def _chips_to_cols(w):
    return w.transpose(1, 0, 2).reshape(w.shape[1], -1)


def kernel(x, meta_tokens, ab_pre_norm, ab_w_in, ab_sinks, ab_conv_w, ab_conv_b, ab_conv_ln_g, ab_conv_ln_b, ab_w_pw2, ab_w_out, ab_post_norm, sb_pre_norm, sb_w_in, sb_w_out, sb_post_norm, loss_target, m_meta_tokens, m_ab_pre_norm, m_ab_w_in, m_ab_sinks, m_ab_conv_w, m_ab_conv_b, m_ab_conv_ln_g, m_ab_conv_ln_b, m_ab_w_pw2, m_ab_w_out, m_ab_post_norm, m_sb_pre_norm, m_sb_w_in, m_sb_w_out, m_sb_post_norm, v_meta_tokens, v_ab_pre_norm, v_ab_w_in, v_ab_sinks, v_ab_conv_w, v_ab_conv_b, v_ab_conv_ln_g, v_ab_conv_ln_b, v_ab_w_pw2, v_ab_w_out, v_ab_post_norm, v_sb_pre_norm, v_sb_w_in, v_sb_w_out, v_sb_post_norm):
    w = dict(meta_tokens=meta_tokens, ab_pre_norm=ab_pre_norm, ab_w_in=ab_w_in, ab_sinks=ab_sinks,
             ab_conv_w=ab_conv_w, ab_conv_b=ab_conv_b, ab_conv_ln_g=ab_conv_ln_g, ab_conv_ln_b=ab_conv_ln_b,
             ab_w_pw2=ab_w_pw2, ab_w_out=ab_w_out, ab_post_norm=ab_post_norm, sb_pre_norm=sb_pre_norm,
             sb_w_in=sb_w_in, sb_w_out=sb_w_out, sb_post_norm=sb_post_norm)
    m = dict(meta_tokens=m_meta_tokens, ab_pre_norm=m_ab_pre_norm, ab_w_in=m_ab_w_in, ab_sinks=m_ab_sinks,
             ab_conv_w=m_ab_conv_w, ab_conv_b=m_ab_conv_b, ab_conv_ln_g=m_ab_conv_ln_g,
             ab_conv_ln_b=m_ab_conv_ln_b, ab_w_pw2=m_ab_w_pw2, ab_w_out=m_ab_w_out, ab_post_norm=m_ab_post_norm,
             sb_pre_norm=m_sb_pre_norm, sb_w_in=m_sb_w_in, sb_w_out=m_sb_w_out, sb_post_norm=m_sb_post_norm)
    v = dict(meta_tokens=v_meta_tokens, ab_pre_norm=v_ab_pre_norm, ab_w_in=v_ab_w_in, ab_sinks=v_ab_sinks,
             ab_conv_w=v_ab_conv_w, ab_conv_b=v_ab_conv_b, ab_conv_ln_g=v_ab_conv_ln_g,
             ab_conv_ln_b=v_ab_conv_ln_b, ab_w_pw2=v_ab_w_pw2, ab_w_out=v_ab_w_out, ab_post_norm=v_ab_post_norm,
             sb_pre_norm=v_sb_pre_norm, sb_w_in=v_sb_w_in, sb_w_out=v_sb_w_out, sb_post_norm=v_sb_post_norm)

    def small_of(d):
        return _pack_small(d["ab_conv_w"][0], d["meta_tokens"], d["sb_pre_norm"], d["sb_post_norm"])

    def rep_of(d):
        return _pack_rep(d["ab_pre_norm"], d["ab_post_norm"], d["ab_conv_b"], d["ab_conv_ln_g"], d["ab_conv_ln_b"],
                         d["ab_sinks"])

    gathered = _gather_chips([w[k][0].astype(BF16) for k in BIG] + [small_of(w)])
    g_in0, g_out0, g_pw2, g_in1, g_out1, g_small = gathered
    conv_w_f = _chips_to_cols(g_small[:, 0:31])
    meta_f = _chips_to_cols(g_small[:, 31:63].reshape(N_CHIPS, 16, 256))
    sb_pre_f = g_small[:, 63:65].reshape(1, D)
    sb_post_f = g_small[:, 65:67].reshape(1, D)
    full = {
        "meta_tokens": meta_f, "ab_pre_norm": ab_pre_norm, "ab_w_in": _chips_to_cols(g_in0),
        "ab_sinks": ab_sinks, "ab_conv_w": conv_w_f, "ab_conv_b": ab_conv_b, "ab_conv_ln_g": ab_conv_ln_g,
        "ab_conv_ln_b": ab_conv_ln_b, "ab_w_pw2": g_pw2.reshape(CC, CC), "ab_w_out": g_out0.reshape(D, D),
        "ab_post_norm": ab_post_norm, "sb_pre_norm": sb_pre_f, "sb_w_in": g_in1, "sb_w_out": g_out1.reshape(D, D),
        "sb_post_norm": sb_post_f,
    }

    loss_row, grad_x, g = _local_step(x[0], loss_target[0], full)
    loss = lax.psum(loss_row[0, 0], ("x", "y", "c"))

    send = [_cols_to_chips(g["ab_w_in"], 704), g["ab_w_out"].reshape(N_CHIPS, 256, D),
            g["ab_w_pw2"].reshape(N_CHIPS, 128, CC), g["sb_w_in"], g["sb_w_out"].reshape(N_CHIPS, 256, D)]
    gs_conv = _cols_to_chips(g["ab_conv_w"], 128)
    gs_meta = _cols_to_chips(g["meta_tokens"], 256)
    gs_pre = g["sb_pre_norm"].reshape(N_CHIPS, 1, 256)
    gs_post = g["sb_post_norm"].reshape(N_CHIPS, 1, 256)
    send.append(jnp.stack([_pack_small(gs_conv[j], gs_meta[j], gs_pre[j], gs_post[j]) for j in range(N_CHIPS)]))
    names = BIG + ["small"]
    own, got = _pair_exchange(send)
    pair = [_sum_pair(o, t, o.dtype, "sum_pair_" + nm) for o, t, nm in zip(own, got, names)]
    mine, land = _scatter_chips([pr[0] for pr in pair], [pr[1] for pr in pair])
    halves = [_sum_shard(mi, la, "sum_shard_" + nm) for mi, la, nm in zip(mine, land, names)]
    total = _join_cores(halves)

    rep_g = _pack_rep(g["ab_pre_norm"], g["ab_post_norm"], g["ab_conv_b"], g["ab_conv_ln_g"], g["ab_conv_ln_b"],
                      g["ab_sinks"])
    rep_sum = _sum8(_gather_all(rep_g), "sum8_rep")

    out_g, out_d, out_m, out_v = {}, {}, {}, {}
    for i, k in enumerate(BIG):
        shp = w[k].shape
        res = _adamw(w[k][0], [total[i]], m[k][0], v[k][0], "adamw_" + k)
        out_g[k], out_d[k], out_m[k], out_v[k] = [r.reshape(shp) for r in res]
    res = _adamw(small_of(w), [total[5]], small_of(m), small_of(v), "adamw_small")
    for dst, r in zip((out_g, out_d, out_m, out_v), res):
        cw, mt, pre, post = _unpack_small(r)
        dst["ab_conv_w"], dst["meta_tokens"], dst["sb_pre_norm"], dst["sb_post_norm"] = cw[None], mt, pre, post
    res = _adamw(rep_of(w), [rep_sum], rep_of(m), rep_of(v), "adamw_rep")
    for dst, r in zip((out_g, out_d, out_m, out_v), res):
        (dst["ab_pre_norm"], dst["ab_post_norm"], dst["ab_conv_b"], dst["ab_conv_ln_g"], dst["ab_conv_ln_b"],
         dst["ab_sinks"]) = _unpack_rep(r)

    return (loss, grad_x[None], *[out_g[k] for k in WEIGHTS], *[out_d[k] for k in WEIGHTS],
            *[out_m[k] for k in WEIGHTS], *[out_v[k] for k in WEIGHTS])
```

```python
import functools

import jax
import jax.numpy as jnp
from jax import lax
from jax.experimental import pallas as pl
from jax.experimental.pallas import tpu as pltpu

F32 = jnp.float32
BF16 = jnp.bfloat16

D = 1024
SEQ = 2048
N_META = 16
TB = 128
PAD = TB - N_META
R = SEQ + TB
NB = R // TB
HD = 64
ROPE_THETA = 10000.0
NORM_EPS = 1e-6
LN_EPS = 1e-5
NEG = -1e30
SWA_HEADS = 8
CONV_W = 31
SCALE = HD ** -0.5
N_CHIPS = 4

C_Q, C_GA, C_GLU, C_GB, C_K, C_V = 0, 512, 1024, 2048, 2560, 2688
AB_IN = 2816

ADAM_LR, ADAM_B1, ADAM_B2, ADAM_EPS, ADAM_WD, ADAM_STEP = 0.001, 0.9, 0.999, 1e-08, 0.01, 10

VMEM_LIMIT = 56 * 1024 * 1024


def _cp(sem):
    return pltpu.CompilerParams(dimension_semantics=sem, vmem_limit_bytes=VMEM_LIMIT)


def _sig(x):
    return 1.0 / (1.0 + jnp.exp(-x))


def _dot(a, b):
    return lax.dot_general(a, b, (((1,), (0,)), ((), ())), preferred_element_type=F32)


def _dot_nt(a, b):
    return lax.dot_general(a, b, (((1,), (1,)), ((), ())), preferred_element_type=F32)


def _dot_tn(a, b):
    return lax.dot_general(a, b, (((0,), (0,)), ((), ())), preferred_element_type=F32)


def _mm(pairs, out_dtype, name, tm, tn, ta=False, tb=False):
    pairs = [(a, b if isinstance(b, tuple) else (b, None)) for a, b in pairs]
    a0, (b0, _) = pairs[0]
    m = a0.shape[1] if ta else a0.shape[0]
    n = b0.shape[-2] if tb else b0.shape[-1]
    npairs = len(pairs)
    dims = (((0 if ta else 1,), (1 if tb else 0,)), ((), ()))

    def body(*refs):
        o_ref = refs[2 * npairs]
        acc = None
        for i in range(npairs):
            t = lax.dot_general(refs[2 * i][...].astype(BF16), refs[2 * i + 1][...].astype(BF16), dims,
                                preferred_element_type=F32)
            acc = t if acc is None else acc + t
        o_ref[...] = acc.astype(out_dtype)

    in_specs, args = [], []
    for a, (b, sel) in pairs:
        k = a.shape[0] if ta else a.shape[1]
        in_specs.append(pl.BlockSpec((k, tm), lambda i, j: (0, i)) if ta else pl.BlockSpec((tm, k), lambda i, j: (i, 0)))
        bshape, bidx = ((tn, k), lambda i, j: (j, 0)) if tb else ((k, tn), lambda i, j: (0, j))
        if sel is None:
            in_specs.append(pl.BlockSpec(bshape, bidx))
        else:
            in_specs.append(pl.BlockSpec((None,) + bshape, functools.partial(lambda i, j, f, s: (s,) + f(i, j), f=bidx, s=sel)))
        args += [a, b]
    return pl.pallas_call(
        body, grid=(m // tm, n // tn), in_specs=in_specs,
        out_specs=pl.BlockSpec((tm, tn), lambda i, j: (i, j)),
        out_shape=jax.ShapeDtypeStruct((m, n), out_dtype), name=name,
        compiler_params=_cp(("parallel", "parallel")))(*args)


def _rms_fwd(h, g, name):
    def body(h_ref, g_ref, o_ref):
        x = h_ref[...]
        r = lax.rsqrt(jnp.mean(x * x, axis=1, keepdims=True) + NORM_EPS)
        o_ref[...] = (x * r * g_ref[...]).astype(BF16)

    return pl.pallas_call(
        body, grid=(NB,),
        in_specs=[pl.BlockSpec((TB, D), lambda n: (n, 0)), pl.BlockSpec((1, D), lambda n: (0, 0))],
        out_specs=pl.BlockSpec((TB, D), lambda n: (n, 0)),
        out_shape=jax.ShapeDtypeStruct((R, D), BF16), name=name, compiler_params=_cp(("parallel",)))(h, g)


def _post_fwd(h, y, g, name):
    def body(h_ref, y_ref, g_ref, o_ref):
        yv = y_ref[...]
        r = lax.rsqrt(jnp.mean(yv * yv, axis=1, keepdims=True) + NORM_EPS)
        o_ref[...] = h_ref[...] + yv * r * g_ref[...]

    blk = pl.BlockSpec((TB, D), lambda n: (n, 0))
    return pl.pallas_call(
        body, grid=(NB,), in_specs=[blk, blk, pl.BlockSpec((1, D), lambda n: (0, 0))], out_specs=blk,
        out_shape=jax.ShapeDtypeStruct((R, D), F32), name=name, compiler_params=_cp(("parallel",)))(h, y, g)


def _rms_bwd(dout, x, g, res, out_dtype, name, split=False):
    has_res = res is not None

    def body(*refs):
        if split:
            refs = list(refs)
            dx_rest_ref = refs.pop(-2)
        if has_res:
            d_ref, x_ref, g_ref, r_ref, dx_ref, dg_ref = refs
        else:
            d_ref, x_ref, g_ref, dx_ref, dg_ref = refs
        n = pl.program_id(0)
        xv = x_ref[...]
        dv = d_ref[...]
        r = lax.rsqrt(jnp.mean(xv * xv, axis=1, keepdims=True) + NORM_EPS)
        xh = xv * r
        dxh = dv * g_ref[...]
        dx = r * (dxh - xh * jnp.mean(dxh * xh, axis=1, keepdims=True))
        if has_res:
            dx = dx + r_ref[...]
        row = lax.broadcasted_iota(jnp.int32, (TB, D), 0) + n * TB
        dx = jnp.where(row >= PAD, dx, 0.0).astype(out_dtype)
        if split:
            @pl.when(n == 0)
            def _():
                dx_ref[...] = dx

            @pl.when(n > 0)
            def _():
                dx_rest_ref[...] = dx
        else:
            dx_ref[...] = dx

        @pl.when(n == 0)
        def _():
            dg_ref[...] = jnp.zeros_like(dg_ref)

        dg_ref[...] += jnp.sum(dv * xh, axis=0, keepdims=True)

    blk = pl.BlockSpec((TB, D), lambda n: (n, 0))
    vec = pl.BlockSpec((1, D), lambda n: (0, 0))
    ins = [dout, x, g] + ([res] if has_res else [])
    in_specs = [blk, blk, vec] + ([blk] if has_res else [])
    if split:
        out_specs = [pl.BlockSpec((TB, D), lambda n: (0, 0)), pl.BlockSpec((TB, D), lambda n: (jnp.maximum(n - 1, 0), 0)), vec]
        out_shape = [jax.ShapeDtypeStruct((TB, D), out_dtype), jax.ShapeDtypeStruct((SEQ, D), out_dtype),
                     jax.ShapeDtypeStruct((1, D), F32)]
    else:
        out_specs = [blk, vec]
        out_shape = [jax.ShapeDtypeStruct((R, D), out_dtype), jax.ShapeDtypeStruct((1, D), F32)]
    return pl.pallas_call(
        body, grid=(NB,), in_specs=in_specs, out_specs=out_specs, out_shape=out_shape,
        name=name, compiler_params=_cp(("arbitrary",)))(*ins)


GW = 512


def _gate_fwd(o, gsrc, goff, name):
    w = o.shape[1]

    def body(o_ref, g_ref, m_ref):
        gv = g_ref[...]
        m_ref[...] = (o_ref[...] * (gv * _sig(gv))).astype(BF16)

    gb = goff // GW
    return pl.pallas_call(
        body, grid=(NB, w // GW),
        in_specs=[pl.BlockSpec((TB, GW), lambda n, j: (n, j)), pl.BlockSpec((TB, GW), lambda n, j: (n, gb + j))],
        out_specs=pl.BlockSpec((TB, GW), lambda n, j: (n, j)),
        out_shape=jax.ShapeDtypeStruct((R, w), BF16), name=name,
        compiler_params=_cp(("parallel", "parallel")))(o, gsrc)


def _gate_bwd(dsrc, doff, o, gsrc, goff, name):
    w = o.shape[1]

    def body(d_ref, o_ref, g_ref, do_ref, dg_ref):
        gv = g_ref[...]
        dv = d_ref[...]
        s = _sig(gv)
        do_ref[...] = dv * (gv * s)
        dg_ref[...] = (dv * o_ref[...] * (s * (1.0 + gv * (1.0 - s)))).astype(BF16)

    db, gb = doff // GW, goff // GW
    blk = pl.BlockSpec((TB, GW), lambda n, j: (n, j))
    return pl.pallas_call(
        body, grid=(NB, w // GW),
        in_specs=[pl.BlockSpec((TB, GW), lambda n, j: (n, db + j)), blk,
                  pl.BlockSpec((TB, GW), lambda n, j: (n, gb + j))],
        out_specs=[blk, blk],
        out_shape=[jax.ShapeDtypeStruct((R, w), F32), jax.ShapeDtypeStruct((R, w), BF16)], name=name,
        compiler_params=_cp(("parallel", "parallel")))(dsrc, o, gsrc)


def _tail(h, y, g, target):
    def body(h_ref, y_ref, g_ref, t_ref, d_ref, dy_ref, dg_ref, l_ref):
        n = pl.program_id(0)

        @pl.when(n == 0)
        def _():
            d_ref[...] = jnp.zeros_like(d_ref)
            dy_ref[...] = jnp.zeros_like(dy_ref)
            dg_ref[...] = jnp.zeros_like(dg_ref)
            l_ref[...] = jnp.zeros_like(l_ref)

        @pl.when(n > 0)
        def _():
            yv = y_ref[...]
            r = lax.rsqrt(jnp.mean(yv * yv, axis=1, keepdims=True) + NORM_EPS)
            yh = yv * r
            err = (h_ref[...] + yh * g_ref[...]) - t_ref[...]
            dv = err * (1.0 / D)
            d_ref[...] = dv
            l_ref[...] += jnp.sum(err * err, axis=0, keepdims=True)
            dyh = dv * g_ref[...]
            dy_ref[...] = (r * (dyh - yh * jnp.mean(dyh * yh, axis=1, keepdims=True))).astype(BF16)
            dg_ref[...] += jnp.sum(dv * yh, axis=0, keepdims=True)

        @pl.when(n == NB - 1)
        def _():
            tot = jnp.sum(l_ref[...], axis=1, keepdims=True) * (0.5 / D)
            l_ref[...] = jnp.broadcast_to(tot, (1, D))

    blk = pl.BlockSpec((TB, D), lambda n: (n, 0))
    vec = pl.BlockSpec((1, D), lambda n: (0, 0))
    return pl.pallas_call(
        body, grid=(NB,),
        in_specs=[blk, blk, vec, pl.BlockSpec((TB, D), lambda n: (jnp.maximum(n - 1, 0), 0))],
        out_specs=[blk, blk, vec, vec],
        out_shape=[jax.ShapeDtypeStruct((R, D), F32), jax.ShapeDtypeStruct((R, D), BF16),
                   jax.ShapeDtypeStruct((1, D), F32), jax.ShapeDtypeStruct((1, D), F32)],
        name="tail", compiler_params=_cp(("arbitrary",)))(h, y, g, target)


def _lane_row(shape):
    return lax.broadcasted_iota(jnp.int32, shape, 1), lax.broadcasted_iota(jnp.int32, shape, 0)


def _rot_half(x, lane):
    return jnp.where(lane % HD < HD // 2, pltpu.roll(x, 128 - HD // 2, 1), pltpu.roll(x, HD // 2, 1))


def _swa_blocks(n):
    return (0, jnp.maximum(n - 1, 0), n)


SWA_STACKS = ((0, 0), (0, 1), (1, 0), (1, 1))


def _swa_masks(n, lane, row):
    qpos = n * TB + (row & (TB - 1))
    kp = (n - 1) * TB + lane
    kc = n * TB + lane
    m0 = (lane >= PAD) & (qpos - lane >= TB)
    mp = (kp >= PAD) & (qpos >= kp) & (qpos - kp < TB)
    mc = (kc >= PAD) & (qpos >= kc)
    return (m0, mp, mc)


def _stack_pair(xa, xb, par):
    lane = lax.broadcasted_iota(jnp.int32, (TB, 128), 1)
    keep = (lane < HD) if par == 0 else (lane >= HD)
    return jnp.concatenate([jnp.where(keep, xa, 0.0), jnp.where(keep, xb, 0.0)], axis=0)


def _per_head(a, b):
    row = lax.broadcasted_iota(jnp.int32, (2 * TB, 1), 0)
    return jnp.where(row < TB, a, b)


def _swa_load(n, zq_ref, zkv_ref, cs_ref, sn_ref, lane):
    r0 = pl.multiple_of(n * TB, TB)
    csq, snq = cs_ref[pl.ds(r0, TB), :], sn_ref[pl.ds(r0, TB), :]
    qc = []
    for c in range(4):
        x = zq_ref[:, c * 128:(c + 1) * 128]
        qc.append((x * csq + _rot_half(x, lane) * snq) * SCALE)
    qst = [_stack_pair(qc[2 * g], qc[2 * g + 1], par).astype(BF16) for g, par in SWA_STACKS]
    kvs = []
    for b in _swa_blocks(n):
        b0 = pl.multiple_of(b * TB, TB)
        csb, snb = cs_ref[pl.ds(b0, TB), :], sn_ref[pl.ds(b0, TB), :]
        kx = zkv_ref[pl.ds(b0, TB), 0:128]
        kr = kx * csb + _rot_half(kx, lane) * snb
        vx = zkv_ref[pl.ds(b0, TB), 128:256]
        kvs.append((kr.astype(BF16), pltpu.roll(kr, HD, 1).astype(BF16),
                    vx.astype(BF16), pltpu.roll(vx, HD, 1).astype(BF16), csb, snb, b0))
    return qst, (csq, snq), kvs


def _swa_fwd(z0, cs, sn, sinks):
    def body(zq_ref, zkv_ref, cs_ref, sn_ref, sk_ref, o_ref, lse_ref):
        n = pl.program_id(0)
        lane, row = _lane_row((TB, 128))
        lo = lane < HD
        masks = _swa_masks(n, *_lane_row((2 * TB, 128)))
        qst, _, kvs = _swa_load(n, zq_ref, zkv_ref, cs_ref, sn_ref, lane)
        ss = [[jnp.where(m, _dot_nt(qst[si], k if par == g else ka), NEG)
               for (k, ka, _, _, _, _, _), m in zip(kvs, masks)] for si, (g, par) in enumerate(SWA_STACKS)]
        o2, lse2 = [], []
        for si, (g, par) in enumerate(SWA_STACKS):
            sink = _per_head(sk_ref[0, 4 * g + par], sk_ref[0, 4 * g + 2 + par])
            s = ss[si]
            mx = jnp.maximum(jnp.maximum(jnp.max(s[0], axis=1, keepdims=True), jnp.max(s[1], axis=1, keepdims=True)),
                             jnp.max(s[2], axis=1, keepdims=True))
            mx = jnp.maximum(mx, sink)
            es = [jnp.exp(sb - mx) for sb in s]
            den = (jnp.sum(es[0], axis=1, keepdims=True) + jnp.sum(es[1], axis=1, keepdims=True)
                   + jnp.sum(es[2], axis=1, keepdims=True) + jnp.exp(sink - mx))
            inv = 1.0 / den
            t = jnp.zeros((2 * TB, 128), F32)
            for (_, _, v, va, _, _, _), e in zip(kvs, es):
                t = t + _dot((e * inv).astype(BF16), v if par == g else va)
            o2.append(t)
            lse2.append(mx + jnp.log(den))
        lse_t = jnp.zeros((TB, 128), F32)
        for g in range(2):
            for t in range(2):
                rows = slice(t * TB, (t + 1) * TB)
                c = 2 * g + t
                o_ref[:, c * 128:(c + 1) * 128] = jnp.where(lo, o2[2 * g][rows], o2[2 * g + 1][rows])
                for par in range(2):
                    lse_t = jnp.where(lane == 4 * g + 2 * t + par, lse2[2 * g + par][rows], lse_t)
        lse_ref[...] = lse_t

    full = pl.BlockSpec((R, 128), lambda n: (0, 0))
    return pl.pallas_call(
        body, grid=(NB,),
        in_specs=[pl.BlockSpec((TB, 512), lambda n: (n, C_Q // 512)),
                  pl.BlockSpec((R, 256), lambda n: (0, C_K // 256)), full, full,
                  pl.BlockSpec(memory_space=pltpu.SMEM)],
        out_specs=[pl.BlockSpec((TB, 512), lambda n: (n, 0)), pl.BlockSpec((TB, 128), lambda n: (n, 0))],
        out_shape=[jax.ShapeDtypeStruct((R, 512), F32), jax.ShapeDtypeStruct((R, 128), F32)],
        name="swa_fwd", compiler_params=_cp(("parallel",)))(z0, z0, cs, sn, sinks)


def _swa_bwd(z0, cs, sn, sinks, o, do, lse):
    def body(zq_ref, zkv_ref, cs_ref, sn_ref, sk_ref, o_ref, do_ref, lse_ref, dq_ref, dkv_ref, dsk_ref):
        n = pl.program_id(0)

        @pl.when(n == 0)
        def _():
            dkv_ref[...] = jnp.zeros_like(dkv_ref)
            dsk_ref[...] = jnp.zeros_like(dsk_ref)

        lane, row = _lane_row((TB, 128))
        lo = lane < HD
        masks = _swa_masks(n, *_lane_row((2 * TB, 128)))
        qst, (csq, snq), kvs = _swa_load(n, zq_ref, zkv_ref, cs_ref, sn_ref, lane)
        lse_t = lse_ref[...]
        ss = [[jnp.where(m, _dot_nt(qst[si], k if par == g else ka), NEG)
               for (k, ka, _, _, _, _, _), m in zip(kvs, masks)] for si, (g, par) in enumerate(SWA_STACKS)]
        dobs, deltas, lses, dps = [], [], [], []
        for g, par in SWA_STACKS:
            ca, cb = slice(2 * g * 128, (2 * g + 1) * 128), slice((2 * g + 1) * 128, (2 * g + 2) * 128)
            dom = _stack_pair(do_ref[:, ca], do_ref[:, cb], par)
            deltas.append(jnp.sum(dom * jnp.concatenate([o_ref[:, ca], o_ref[:, cb]], axis=0), axis=1, keepdims=True))
            dob = dom.astype(BF16)
            dobs.append(dob)
            lses.append(jnp.concatenate(
                [jnp.sum(jnp.where(lane == 4 * g + 2 * t + par, lse_t, 0.0), axis=1, keepdims=True) for t in range(2)],
                axis=0))
            dps.append([_dot_nt(dob, v if par == g else va) for (_, _, v, va, _, _, _) in kvs])
        dk_al = [jnp.zeros((TB, 128), F32) for _ in range(3)]
        dk_mis = [jnp.zeros((TB, 128), F32) for _ in range(3)]
        dv_al = [jnp.zeros((TB, 128), F32) for _ in range(3)]
        dv_mis = [jnp.zeros((TB, 128), F32) for _ in range(3)]
        dsk_t = jnp.zeros((TB, 128), F32)
        dq2 = []
        for si, (g, par) in enumerate(SWA_STACKS):
            dqt = jnp.zeros((2 * TB, 128), F32)
            for bi, (k, ka, _, _, _, _, _) in enumerate(kvs):
                p = jnp.exp(ss[si][bi] - lses[si])
                ds = (p * (dps[si][bi] - deltas[si])).astype(BF16)
                dqt = dqt + _dot(ds, k if par == g else ka)
                dkh = _dot_tn(ds, qst[si])
                dvh = _dot_tn(p.astype(BF16), dobs[si])
                if par == g:
                    dk_al[bi] = dk_al[bi] + dkh
                    dv_al[bi] = dv_al[bi] + dvh
                else:
                    dk_mis[bi] = dk_mis[bi] + dkh
                    dv_mis[bi] = dv_mis[bi] + dvh
            dq2.append(dqt)
            sink = _per_head(sk_ref[0, 4 * g + par], sk_ref[0, 4 * g + 2 + par])
            dsk = -jnp.exp(sink - lses[si]) * deltas[si]
            for t in range(2):
                dsk_t = jnp.where(lane == 4 * g + 2 * t + par, dsk[t * TB:(t + 1) * TB], dsk_t)
        for g in range(2):
            for t in range(2):
                rows = slice(t * TB, (t + 1) * TB)
                c = 2 * g + t
                dqc = jnp.where(lo, dq2[2 * g][rows], dq2[2 * g + 1][rows]) * SCALE
                dq_ref[:, c * 128:(c + 1) * 128] = dqc * csq + _rot_half(dqc * snq, lane)
        for bi, (_, _, _, _, csb, snb, b0) in enumerate(kvs):
            dk = dk_al[bi] + pltpu.roll(dk_mis[bi], HD, 1)
            dv = dv_al[bi] + pltpu.roll(dv_mis[bi], HD, 1)
            dkv_ref[pl.ds(b0, TB), 0:128] += dk * csb + _rot_half(dk * snb, lane)
            dkv_ref[pl.ds(b0, TB), 128:256] += dv
        dsk_ref[0:1, :] += jnp.sum(dsk_t, axis=0, keepdims=True)

    full = pl.BlockSpec((R, 128), lambda n: (0, 0))
    b512 = pl.BlockSpec((TB, 512), lambda n: (n, 0))
    return pl.pallas_call(
        body, grid=(NB,),
        in_specs=[pl.BlockSpec((TB, 512), lambda n: (n, C_Q // 512)),
                  pl.BlockSpec((R, 256), lambda n: (0, C_K // 256)), full, full,
                  pl.BlockSpec(memory_space=pltpu.SMEM), b512, b512, pl.BlockSpec((TB, 128), lambda n: (n, 0))],
        out_specs=[b512, pl.BlockSpec((R, 256), lambda n: (0, 0)), pl.BlockSpec((8, 128), lambda n: (0, 0))],
        out_shape=[jax.ShapeDtypeStruct((R, 512), F32), jax.ShapeDtypeStruct((R, 256), F32),
                   jax.ShapeDtypeStruct((8, 128), F32)],
        name="swa_bwd", compiler_params=_cp(("arbitrary",)))(z0, z0, cs, sn, sinks, o, do, lse)


CC = 512
HALO = CONV_W - 1


def _conv_fwd(z0, conv_w, conv_b, ln_g, ln_b):
    def body(g_ref, w_ref, cb_ref, lg_ref, lb_ref, cv_ref, s_ref, ubuf):
        n = pl.program_id(0)

        @pl.when(n == 0)
        def _():
            ubuf[0:TB, :] = jnp.zeros((TB, CC), F32)

        u = g_ref[:, 0:CC] * _sig(g_ref[:, CC:2 * CC])
        ubuf[TB:2 * TB, :] = u
        acc = jnp.zeros((TB, CC), F32)
        for w in range(CONV_W):
            acc = acc + ubuf[pl.ds(TB - HALO + w, TB), :] * w_ref[w:w + 1, :]
        cv = acc + cb_ref[...]
        cv_ref[...] = cv
        xc = cv - jnp.mean(cv, axis=1, keepdims=True)
        rs = lax.rsqrt(jnp.mean(xc * xc, axis=1, keepdims=True) + LN_EPS)
        ln = xc * rs * lg_ref[...] + lb_ref[...]
        s_ref[...] = (ln * _sig(ln)).astype(BF16)
        ubuf[0:TB, :] = u

    vec = pl.BlockSpec((1, CC), lambda n: (0, 0))
    blk = pl.BlockSpec((TB, CC), lambda n: (n, 0))
    return pl.pallas_call(
        body, grid=(NB,),
        in_specs=[pl.BlockSpec((TB, 2 * CC), lambda n: (n, C_GLU // (2 * CC))),
                  pl.BlockSpec((32, CC), lambda n: (0, 0)), vec, vec, vec],
        out_specs=[blk, blk],
        out_shape=[jax.ShapeDtypeStruct((R, CC), F32), jax.ShapeDtypeStruct((R, CC), BF16)],
        scratch_shapes=[pltpu.VMEM((2 * TB, CC), F32)],
        name="conv_fwd", compiler_params=_cp(("arbitrary",)))(z0, conv_w, conv_b, ln_g, ln_b)


def _conv_bwd(ds, cv, z0, conv_w, ln_g, ln_b):
    def body(ds_ref, cv_ref, g_ref, w_ref, lg_ref, lb_ref, dglu_ref, dw_ref, dsm_ref, dbuf):
        n = pl.program_id(0)

        @pl.when(n == 0)
        def _():
            dbuf[TB:2 * TB, :] = jnp.zeros((TB, CC), F32)
            dw_ref[...] = jnp.zeros_like(dw_ref)
            dsm_ref[...] = jnp.zeros_like(dsm_ref)

        cv = cv_ref[...]
        xc = cv - jnp.mean(cv, axis=1, keepdims=True)
        rs = lax.rsqrt(jnp.mean(xc * xc, axis=1, keepdims=True) + LN_EPS)
        xh = xc * rs
        ln = xh * lg_ref[...] + lb_ref[...]
        sg = _sig(ln)
        dln = ds_ref[...] * (sg * (1.0 + ln * (1.0 - sg)))
        dxh = dln * lg_ref[...]
        dcv = rs * (dxh - jnp.mean(dxh, axis=1, keepdims=True) - xh * jnp.mean(dxh * xh, axis=1, keepdims=True))
        dsm_ref[0:1, :] += jnp.sum(dcv, axis=0, keepdims=True)
        dsm_ref[1:2, :] += jnp.sum(dln * xh, axis=0, keepdims=True)
        dsm_ref[2:3, :] += jnp.sum(dln, axis=0, keepdims=True)
        dbuf[0:TB, :] = dcv
        a = g_ref[:, 0:CC]
        sb = _sig(g_ref[:, CC:2 * CC])
        u = a * sb
        du = jnp.zeros((TB, CC), F32)
        for w in range(CONV_W):
            sh = dbuf[pl.ds(HALO - w, TB), :]
            du = du + sh * w_ref[w:w + 1, :]
            dw_ref[w:w + 1, :] += jnp.sum(u * sh, axis=0, keepdims=True)
        dglu_ref[:, 0:CC] = (du * sb).astype(BF16)
        dglu_ref[:, CC:2 * CC] = (du * a * sb * (1.0 - sb)).astype(BF16)
        dbuf[TB:2 * TB, :] = dcv

    rev = lambda n: (NB - 1 - n, 0)
    vec = pl.BlockSpec((1, CC), lambda n: (0, 0))
    blk = pl.BlockSpec((TB, CC), rev)
    return pl.pallas_call(
        body, grid=(NB,),
        in_specs=[blk, blk, pl.BlockSpec((TB, 2 * CC), lambda n: (NB - 1 - n, C_GLU // (2 * CC))),
                  pl.BlockSpec((32, CC), lambda n: (0, 0)), vec, vec],
        out_specs=[pl.BlockSpec((TB, 2 * CC), rev), pl.BlockSpec((32, CC), lambda n: (0, 0)),
                   pl.BlockSpec((8, CC), lambda n: (0, 0))],
        out_shape=[jax.ShapeDtypeStruct((R, 2 * CC), BF16), jax.ShapeDtypeStruct((32, CC), F32),
                   jax.ShapeDtypeStruct((8, CC), F32)],
        scratch_shapes=[pltpu.VMEM((2 * TB, CC), F32)],
        name="conv_bwd", compiler_params=_cp(("arbitrary",)))(ds, cv, z0, conv_w, ln_g, ln_b)


def _split_dot(x, t):
    hi = x.astype(BF16)
    lo = (x - hi.astype(F32)).astype(BF16)
    return _dot(hi, t) + _dot(lo, t)


def _stack_heads(x):
    lane = lax.broadcasted_iota(jnp.int32, (TB, 128), 1)
    return jnp.concatenate([jnp.where(lane < HD, x, 0.0), jnp.where(lane < HD, 0.0, x)], axis=0).astype(BF16)


def _sb_stack(qv, i):
    lane2, row2 = _lane_row((2 * TB, 128))
    qpos2 = i * TB + (row2 & (TB - 1))
    lane, row = _lane_row((TB, 128))
    return _stack_heads(qv), lane2, qpos2, (row > lane).astype(BF16)


SB_U = 3
SB_DEAD = -104.0


def _sb_fwd(q, k, v, g):
    def body(q_ref, k_ref, v_ref, g_ref, o_ref, m_ref, c_ref, n_ref):
        p, i = pl.program_id(0), pl.program_id(1)
        lane, row = _lane_row((TB, 128))
        lo = lane < HD
        q2, lane2, qpos2, tri_gt = _sb_stack(q_ref[...].astype(F32) * SCALE, i)

        def cond(st):
            t, _, c2 = st
            return jnp.logical_and(i - SB_U * t >= 0, jnp.max(c2) > SB_DEAD)

        def step(st):
            t, acc, c2 = st
            jrs = [i - SB_U * t - u for u in range(SB_U)]
            j0s = [pl.multiple_of(jnp.maximum(jr, 0) * TB, TB) for jr in jrs]
            ks = [k_ref[pl.ds(j0, TB), :] for j0 in j0s]
            zs = [_dot_nt(q2, kj) for kj in ks]
            valids, lbs, l1s = [], [], []
            for jr, z in zip(jrs, zs):
                kpos = jr * TB + lane2
                valid = (kpos >= PAD) & (kpos < qpos2)
                lb = jnp.minimum(z, 0.0) - jnp.log(1.0 + jnp.exp(-jnp.abs(z)))
                valids.append(valid)
                lbs.append(lb)
                l1s.append(jnp.where(valid, lb - z, 0.0))
            sfxs = [_split_dot(l1, tri_gt) for l1 in l1s]
            carries = []
            for jr, l1 in zip(jrs, l1s):
                carries.append(c2)
                c_ref[...] = jnp.where(lane == 2 * jr, c2[0:TB], jnp.where(lane == 2 * jr + 1, c2[TB:2 * TB], c_ref[...]))
                c2 = c2 + jnp.sum(l1, axis=1, keepdims=True)
            for j0, valid, lb, sfx, cu in zip(j0s, valids, lbs, sfxs, carries):
                a = jnp.where(valid, jnp.exp(lb + sfx + cu), 0.0).astype(BF16)
                av = _dot(a, v_ref[pl.ds(j0, TB), :])
                acc = acc + jnp.where(lo, av[0:TB], av[TB:2 * TB])
            return t + 1, acc, c2

        c_ref[...] = jnp.zeros((TB, 128), F32)
        t, acc, _ = lax.while_loop(cond, step, (jnp.int32(0), jnp.zeros((TB, 128), F32), jnp.zeros((2 * TB, 1), F32)))
        o_ref[...] = acc
        gv = g_ref[...]
        m_ref[...] = (acc * (gv * _sig(gv))).astype(BF16)
        n_ref[p, i] = t

    slab = pl.BlockSpec((R, 128), lambda p, i: (0, p))
    blk = pl.BlockSpec((TB, 128), lambda p, i: (i, p))
    sd = jax.ShapeDtypeStruct((R, D), F32)
    return pl.pallas_call(
        body, grid=(D // 128, NB), in_specs=[blk, slab, slab, blk],
        out_specs=[blk, blk, blk, pl.BlockSpec(memory_space=pltpu.SMEM)],
        out_shape=[sd, jax.ShapeDtypeStruct((R, D), BF16), sd, jax.ShapeDtypeStruct((D // 128, NB), jnp.int32)],
        name="sb_fwd", compiler_params=_cp(("arbitrary", "arbitrary")))(q, k, v, g)


def _sb_bwd(trips, q, k, v, car, dm, g, o):
    def body(n_ref, q_ref, k_ref, v_ref, c_ref, dm_ref, g_ref, o_ref, dq_ref, dk_ref, dv_ref, dg_ref):
        p, i = pl.program_id(0), pl.program_id(1)

        @pl.when(i == 0)
        def _():
            dk_ref[...] = jnp.zeros_like(dk_ref)
            dv_ref[...] = jnp.zeros_like(dv_ref)

        lane, row = _lane_row((TB, 128))
        lo = lane < HD
        tri_lt = (row < lane).astype(BF16)
        q2, lane2, qpos2, tri_gt = _sb_stack(q_ref[...].astype(F32) * SCALE, i)
        gv, dmv = g_ref[...], dm_ref[...]
        sg = _sig(gv)
        dg_ref[...] = (dmv * o_ref[...] * (sg * (1.0 + gv * (1.0 - sg)))).astype(BF16)
        do2 = _stack_heads(dmv * (gv * sg))
        ct = c_ref[...]
        trips_i = n_ref[p, i]
        first = jnp.maximum(i + 1 - SB_U * trips_i, 0)

        def step(t, carry):
            dq, g2 = carry
            jrs = [first + SB_U * t + u for u in range(SB_U)]
            j0s = [pl.multiple_of(jnp.minimum(jr, i) * TB, TB) for jr in jrs]
            ks = [k_ref[pl.ds(j0, TB), :] for j0 in j0s]
            vs = [v_ref[pl.ds(j0, TB), :] for j0 in j0s]
            zs = [_dot_nt(q2, kj) for kj in ks]
            das = [_dot_nt(do2, vj) for vj in vs]
            valids, es, lbs, l1s = [], [], [], []
            for jr, z in zip(jrs, zs):
                kpos = jr * TB + lane2
                valid = (kpos >= PAD) & (kpos < qpos2)
                e = jnp.exp(-jnp.abs(z))
                lb = jnp.minimum(z, 0.0) - jnp.log(1.0 + e)
                valids.append(valid)
                es.append(e)
                lbs.append(lb)
                l1s.append(jnp.where(valid, lb - z, 0.0))
            sfxs = [_split_dot(l1, tri_gt) for l1 in l1s]
            a_s, gmats, gpre = [], [], []
            for jr, valid, lb, sfx, da in zip(jrs, valids, lbs, sfxs, das):
                later = jnp.concatenate(
                    [jnp.sum(jnp.where(lane == 2 * jr + hh, ct, 0.0), axis=1, keepdims=True) for hh in range(2)], axis=0)
                a = jnp.where(valid, jnp.exp(lb + sfx + later), 0.0)
                gmat = da * a
                a_s.append(a.astype(BF16))
                gmats.append(gmat)
                gpre.append(g2)
                g2 = g2 + jnp.sum(gmat, axis=1, keepdims=True)
            pres = [gp + _split_dot(gmat, tri_lt) for gp, gmat in zip(gpre, gmats)]
            for j0, kj, valid, z, e, gmat, pre, a in zip(j0s, ks, valids, zs, es, gmats, pres, a_s):
                r = 1.0 / (1.0 + e)
                big = z >= 0.0
                beta = jnp.where(big, r, e * r)
                omb = jnp.where(big, e * r, r)
                dz = jnp.where(valid, gmat * omb - beta * pre, 0.0).astype(BF16)
                dq2 = _dot(dz, kj)
                dq = dq + jnp.where(lo, dq2[0:TB], dq2[TB:2 * TB])
                dk_ref[pl.ds(j0, TB), :] += _dot_tn(dz, q2)
                dv_ref[pl.ds(j0, TB), :] += _dot_tn(a, do2)
            return dq, g2

        dq, _ = lax.fori_loop(0, trips_i, step, (jnp.zeros((TB, 128), F32), jnp.zeros((2 * TB, 1), F32)))
        dq_ref[...] = dq * SCALE

    slab = pl.BlockSpec((R, 128), lambda p, i: (0, p))
    blk = pl.BlockSpec((TB, 128), lambda p, i: (i, p))
    sd = jax.ShapeDtypeStruct((R, D), F32)
    return pl.pallas_call(
        body, grid=(D // 128, NB),
        in_specs=[pl.BlockSpec(memory_space=pltpu.SMEM), blk, slab, slab, blk, blk, blk, blk],
        out_specs=[blk, slab, slab, blk],
        out_shape=[sd, sd, sd, jax.ShapeDtypeStruct((R, D), BF16)], name="sb_bwd",
        compiler_params=_cp(("arbitrary", "arbitrary")))(trips, q, k, v, car, dm, g, o)


def _adamw(w, parts, m, v, name):
    rows, cols = w.shape
    tr = 256 if rows % 256 == 0 else rows
    nparts = len(parts)

    def body(*refs):
        w_ref = refs[0]
        p_refs = refs[1:1 + nparts]
        m_ref, v_ref, g_ref, d_ref, nm_ref, nv_ref = refs[1 + nparts:]
        g = p_refs[0][...]
        for p_ref in p_refs[1:]:
            g = g + p_ref[...]
        nm = ADAM_B1 * m_ref[...] + (1.0 - ADAM_B1) * g
        nv = ADAM_B2 * v_ref[...] + (1.0 - ADAM_B2) * (g * g)
        m_hat = nm / (1.0 - ADAM_B1 ** ADAM_STEP)
        v_hat = nv / (1.0 - ADAM_B2 ** ADAM_STEP)
        g_ref[...] = g
        d_ref[...] = -ADAM_LR * (m_hat / (jnp.sqrt(v_hat) + ADAM_EPS) + ADAM_WD * w_ref[...])
        nm_ref[...] = nm
        nv_ref[...] = nv

    blk = pl.BlockSpec((tr, cols), lambda i: (i, 0))
    sd = jax.ShapeDtypeStruct((rows, cols), F32)
    return pl.pallas_call(
        body, grid=(rows // tr,), in_specs=[blk] * (3 + nparts), out_specs=[blk] * 4, out_shape=[sd] * 4,
        name=name, compiler_params=_cp(("parallel",)))(w, *parts, m, v)


def _sum8(buf, name):
    _, rows, cols = buf.shape

    def body(b_ref, o_ref):
        acc = b_ref[0]
        for i in range(1, 8):
            acc = acc + b_ref[i]
        o_ref[...] = acc

    return pl.pallas_call(
        body, out_shape=jax.ShapeDtypeStruct((rows, cols), F32), name=name,
        compiler_params=pltpu.CompilerParams(vmem_limit_bytes=VMEM_LIMIT))(buf)


MESH = pl.DeviceIdType.MESH
ANY = pl.BlockSpec(memory_space=pl.ANY)


def _chip_peers():
    x, y = lax.axis_index("x"), lax.axis_index("y")
    return [(1 - x, y), (x, 1 - y), (1 - x, 1 - y)]


def _gather_chips(shards):
    n = len(shards)
    shards = [s.reshape((2, s.shape[0] // 2) + s.shape[1:]) for s in shards]

    def body(*refs):
        ins, outs = refs[:n], refs[n:2 * n]
        s1, r1, s2, r2 = refs[2 * n:]
        x, y, c = lax.axis_index("x"), lax.axis_index("y"), lax.axis_index("c")
        me = 2 * x + y
        peers = _chip_peers()

        def half(ref, a, cc):
            return ref.at[cc]

        first = []
        for j, (px, py) in enumerate(peers):
            for a in range(n):
                first.append(pltpu.make_async_remote_copy(
                    src_ref=half(ins[a], a, c), dst_ref=half(outs[a].at[me], a, c), send_sem=s1.at[j * n + a],
                    recv_sem=r1.at[j * n + a], device_id=(px, py, c), device_id_type=MESH))
        for cp in first:
            cp.start()
        passed = []
        for j, (px, py) in enumerate(peers):
            for a in range(n):
                got = half(outs[a].at[2 * px + py], a, c)
                pltpu.make_async_remote_copy(
                    src_ref=got, dst_ref=got, send_sem=s1.at[j * n + a], recv_sem=r1.at[j * n + a],
                    device_id=(px, py, c), device_id_type=MESH).wait_recv()
                cp = pltpu.make_async_remote_copy(
                    src_ref=got, dst_ref=got, send_sem=s2.at[j * n + a], recv_sem=r2.at[j * n + a],
                    device_id=(x, y, 1 - c), device_id_type=MESH)
                cp.start()
                passed.append(cp)
        for j, (px, py) in enumerate(peers):
            for a in range(n):
                theirs = half(outs[a].at[2 * px + py], a, 1 - c)
                pltpu.make_async_remote_copy(
                    src_ref=theirs, dst_ref=theirs, send_sem=s2.at[j * n + a], recv_sem=r2.at[j * n + a],
                    device_id=(x, y, 1 - c), device_id_type=MESH).wait_recv()
        for cp in first + passed:
            cp.wait_send()

    res = pl.pallas_call(
        body, in_specs=[ANY] * n, out_specs=[ANY] * n,
        out_shape=[jax.ShapeDtypeStruct((N_CHIPS,) + s.shape, s.dtype) for s in shards],
        scratch_shapes=[pltpu.SemaphoreType.DMA((3 * n,))] * 4, name="gather_chips")(*shards)
    me = 2 * lax.axis_index("x") + lax.axis_index("y")
    res = [lax.dynamic_update_index_in_dim(r, s, me, 0) for r, s in zip(res, shards)]
    return [r.reshape((N_CHIPS, 2 * r.shape[2]) + r.shape[3:]) for r in res]


def _pair_exchange(grads):
    n = len(grads)
    hs = [g.shape[1] // 2 for g in grads]
    grads = [g.reshape((N_CHIPS, 2, h) + g.shape[2:]) for g, h in zip(grads, hs)]

    def body(*refs):
        ins, got = refs[:n], refs[n:2 * n]
        ssem, rsem = refs[2 * n:]
        x, y, c = lax.axis_index("x"), lax.axis_index("y"), lax.axis_index("c")
        sends = [pltpu.make_async_remote_copy(
            src_ref=ins[a].at[:, 1 - c], dst_ref=got[a], send_sem=ssem.at[a],
            recv_sem=rsem.at[a], device_id=(x, y, 1 - c), device_id_type=MESH) for a in range(n)]
        for cp in sends:
            cp.start()
        for cp in sends:
            cp.wait()

    half_shapes = [jax.ShapeDtypeStruct((N_CHIPS, h) + g.shape[3:], g.dtype) for g, h in zip(grads, hs)]
    got = pl.pallas_call(
        body, in_specs=[ANY] * n, out_specs=[ANY] * n, out_shape=half_shapes,
        scratch_shapes=[pltpu.SemaphoreType.DMA((n,))] * 2, name="pair_exchange")(*grads)
    c = lax.axis_index("c")
    own = [lax.dynamic_index_in_dim(g, c, 1, keepdims=False) for g in grads]
    return own, got


def _sum_pair(own, got, send_dtype, name):
    _, rows, cols = own.shape
    tr = 256 if rows % 256 == 0 else rows

    def body(a_ref, b_ref, f_ref, s_ref):
        t = a_ref[...].astype(F32) + b_ref[...].astype(F32)
        f_ref[...] = t
        s_ref[...] = t.astype(send_dtype)

    blk = pl.BlockSpec((N_CHIPS, tr, cols), lambda i: (0, i, 0))
    return pl.pallas_call(
        body, grid=(rows // tr,), in_specs=[blk, blk], out_specs=[blk, blk],
        out_shape=[jax.ShapeDtypeStruct(own.shape, F32), jax.ShapeDtypeStruct(own.shape, send_dtype)],
        name=name, compiler_params=_cp(("parallel",)))(own, got)


def _scatter_chips(keep, send):
    n = len(send)

    def body(*refs):
        sin, land = refs[:n], refs[n:2 * n]
        ssem, rsem = refs[2 * n:]
        c = lax.axis_index("c")
        sends = []
        for j, (px, py) in enumerate(_chip_peers()):
            for a in range(n):
                sends.append(pltpu.make_async_remote_copy(
                    src_ref=sin[a].at[2 * px + py], dst_ref=land[a].at[j], send_sem=ssem.at[j * n + a],
                    recv_sem=rsem.at[j * n + a], device_id=(px, py, c), device_id_type=MESH))
        for cp in sends:
            cp.start()
        for cp in sends:
            cp.wait()

    land = pl.pallas_call(
        body, in_specs=[ANY] * n, out_specs=[ANY] * n,
        out_shape=[jax.ShapeDtypeStruct((3,) + s.shape[1:], s.dtype) for s in send],
        scratch_shapes=[pltpu.SemaphoreType.DMA((3 * n,))] * 2, name="scatter_chips")(*send)
    me = 2 * lax.axis_index("x") + lax.axis_index("y")
    mine = [lax.dynamic_index_in_dim(k, me, 0, keepdims=False) for k in keep]
    return mine, land


def _sum_shard(mine, land, name):
    rows, cols = mine.shape
    tr = 256 if rows % 256 == 0 else rows

    def body(m_ref, l_ref, o_ref):
        o_ref[...] = ((m_ref[...] + l_ref[0].astype(F32)) + l_ref[1].astype(F32)) + l_ref[2].astype(F32)

    return pl.pallas_call(
        body, grid=(rows // tr,),
        in_specs=[pl.BlockSpec((tr, cols), lambda i: (i, 0)), pl.BlockSpec((3, tr, cols), lambda i: (0, i, 0))],
        out_specs=pl.BlockSpec((tr, cols), lambda i: (i, 0)), out_shape=jax.ShapeDtypeStruct((rows, cols), F32),
        name=name, compiler_params=_cp(("parallel",)))(mine, land)


def _join_cores(halves):
    n = len(halves)

    def body(*refs):
        ins, outs = refs[:n], refs[n:2 * n]
        ssem, rsem = refs[2 * n:]
        x, y, c = lax.axis_index("x"), lax.axis_index("y"), lax.axis_index("c")
        sends = [pltpu.make_async_remote_copy(
            src_ref=ins[a], dst_ref=outs[a].at[c], send_sem=ssem.at[a], recv_sem=rsem.at[a],
            device_id=(x, y, 1 - c), device_id_type=MESH) for a in range(n)]
        for cp in sends:
            cp.start()
        for a in range(n):
            sends[a].wait_send()
            pltpu.make_async_remote_copy(
                src_ref=ins[a], dst_ref=outs[a].at[1 - c], send_sem=ssem.at[a], recv_sem=rsem.at[a],
                device_id=(x, y, 1 - c), device_id_type=MESH).wait_recv()

    res = pl.pallas_call(
        body, in_specs=[ANY] * n, out_specs=[ANY] * n,
        out_shape=[jax.ShapeDtypeStruct((2,) + h.shape, h.dtype) for h in halves],
        scratch_shapes=[pltpu.SemaphoreType.DMA((n,))] * 2, name="join_cores")(*halves)
    c = lax.axis_index("c")
    res = [lax.dynamic_update_index_in_dim(r, h, c, 0) for r, h in zip(res, halves)]
    return [r.reshape((2 * r.shape[1],) + r.shape[2:]) for r in res]


def _gather_all(vec):
    def body(v_ref, o_ref, lsem, ssem, rsem):
        x, y, c = lax.axis_index("x"), lax.axis_index("y"), lax.axis_index("c")
        me = 4 * x + 2 * y + c
        local = pltpu.make_async_copy(v_ref, o_ref.at[me], lsem)
        local.start()
        cps = []
        for k in range(1, 8):
            px, py, pc = x ^ (k >> 2), y ^ ((k >> 1) & 1), c ^ (k & 1)
            cps.append(pltpu.make_async_remote_copy(
                src_ref=v_ref, dst_ref=o_ref.at[me], send_sem=ssem.at[k - 1], recv_sem=rsem.at[k - 1],
                device_id=(px, py, pc), device_id_type=MESH))
        for cp in cps:
            cp.start()
        for k in range(1, 8):
            px, py, pc = x ^ (k >> 2), y ^ ((k >> 1) & 1), c ^ (k & 1)
            pltpu.make_async_remote_copy(
                src_ref=v_ref, dst_ref=o_ref.at[4 * px + 2 * py + pc], send_sem=ssem.at[k - 1],
                recv_sem=rsem.at[k - 1], device_id=(px, py, pc), device_id_type=MESH).wait_recv()
        for cp in cps:
            cp.wait_send()
        local.wait()

    return pl.pallas_call(
        body, in_specs=[ANY], out_specs=ANY, out_shape=jax.ShapeDtypeStruct((8,) + vec.shape, vec.dtype),
        scratch_shapes=[pltpu.SemaphoreType.DMA, pltpu.SemaphoreType.DMA((7,)), pltpu.SemaphoreType.DMA((7,))],
        name="gather_all")(vec)


def _rope_tables():
    pos = (jnp.arange(R, dtype=jnp.int32) - PAD).astype(F32)
    half = HD // 2
    inv = ROPE_THETA ** (-jnp.arange(half, dtype=F32) / half)
    ang = pos[:, None] * inv[None, :]
    cos, sin = jnp.cos(ang), jnp.sin(ang)
    cs = jnp.tile(cos, (1, 4))
    sn = jnp.tile(jnp.concatenate([-sin, sin], axis=1), (1, 2))
    return cs, sn


def _perm_cols(w):
    return jnp.concatenate([w[:, 0:512], w[:, 768:1280], w[:, 1280:2304], w[:, 2304:2816], w[:, 512:640],
                            w[:, 640:768]], axis=1)


def _unperm_cols(w):
    return jnp.concatenate([w[:, C_Q:C_Q + 512], w[:, C_K:C_K + 128], w[:, C_V:C_V + 128], w[:, C_GA:C_GA + 512],
                            w[:, C_GLU:C_GLU + 1024], w[:, C_GB:C_GB + 512]], axis=1)


def _local_step(x, target, p):
    w0 = _perm_cols(p["ab_w_in"])
    wo0, wpw, wo1 = p["ab_w_out"], p["ab_w_pw2"], p["sb_w_out"]
    w1 = p["sb_w_in"]
    conv_w = jnp.concatenate([p["ab_conv_w"], jnp.zeros((1, CC), F32)], axis=0)
    cs, sn = _rope_tables()

    h0 = jnp.concatenate([jnp.zeros((PAD, D), F32), p["meta_tokens"], x], axis=0)

    xn0 = _rms_fwd(h0, p["ab_pre_norm"], "rms_fwd0")
    z0 = _mm([(xn0, w0)], F32, "in_proj0", 544, 1408)
    o0, lse0 = _swa_fwd(z0, cs, sn, p["ab_sinks"])
    a0 = _gate_fwd(o0, z0, C_GA, "gate_a_fwd")
    cv0, s0 = _conv_fwd(z0, conv_w, p["ab_conv_b"], p["ab_conv_ln_g"], p["ab_conv_ln_b"])
    t0 = _mm([(s0, wpw)], F32, "pw2", 544, 512)
    c0 = _gate_fwd(t0, z0, C_GB, "gate_b_fwd")
    wo0h = wo0.reshape(2, CC, D)
    y0 = _mm([(a0, (wo0h, 0)), (c0, (wo0h, 1))], F32, "out_proj0", 544, 1024)
    h1 = _post_fwd(h0, y0, p["ab_post_norm"], "post_fwd0")

    xn1 = _rms_fwd(h1, p["sb_pre_norm"], "rms_fwd1")
    q1 = _mm([(xn1, (w1, 0))], BF16, "in_proj1_q", 544, 1024)
    k1 = _mm([(xn1, (w1, 1))], BF16, "in_proj1_k", 544, 1024)
    v1 = _mm([(xn1, (w1, 2))], BF16, "in_proj1_v", 544, 1024)
    g1 = _mm([(xn1, (w1, 3))], F32, "in_proj1_g", 544, 1024)
    o1, m1, car1, trips1 = _sb_fwd(q1, k1, v1, g1)
    y1 = _mm([(m1, wo1)], F32, "out_proj1", 544, 1024)

    dh2, dy1, d_sb_post, loss_row = _tail(h1, y1, p["sb_post_norm"], target)

    dm1 = _mm([(dy1, wo1)], F32, "out_proj1_dx", 544, 1024, tb=True)
    d_wo1 = _mm([(m1, dy1)], BF16, "out_proj1_dw", 512, 1024, ta=True)
    dq1, dk1, dv1, dg1 = _sb_bwd(trips1, q1, k1, v1, car1, dm1, g1, o1)
    dz1 = [dq1, dk1, dv1, dg1]
    dxn1 = _mm([(dz1[j], (w1, j)) for j in range(4)], F32, "in_proj1_dx", 544, 512, tb=True)
    d_w1 = jnp.stack([_mm([(xn1, dz1[j])], BF16, "in_proj1_dw%d" % j, 512, 1024, ta=True) for j in range(4)])
    dh1, d_sb_pre = _rms_bwd(dxn1, h1, p["sb_pre_norm"], dh2, F32, "rms_bwd1")

    dy0, d_ab_post = _rms_bwd(dh1, y0, p["ab_post_norm"], None, BF16, "post_bwd0")
    dmix0 = _mm([(dy0, wo0)], F32, "out_proj0_dx", 544, 1024, tb=True)
    d_wo0 = jnp.concatenate([_mm([(a0, dy0)], BF16, "out_proj0_dw_a", 512, 1024, ta=True),
                             _mm([(c0, dy0)], BF16, "out_proj0_dw_b", 512, 1024, ta=True)], axis=0)
    dt0, dgb0 = _gate_bwd(dmix0, 512, t0, z0, C_GB, "gate_b_bwd")
    ds0 = _mm([(dt0, wpw)], F32, "pw2_dx", 544, 512, tb=True)
    d_wpw = _mm([(s0, dt0)], BF16, "pw2_dw", 512, 512, ta=True)
    dglu0, d_convw, d_small = _conv_bwd(ds0, cv0, z0, conv_w, p["ab_conv_ln_g"], p["ab_conv_ln_b"])
    do0, dga0 = _gate_bwd(dmix0, 0, o0, z0, C_GA, "gate_a_bwd")
    dq0, dkv0, d_sinks = _swa_bwd(z0, cs, sn, p["ab_sinks"], o0, do0, lse0)
    dz0 = jnp.concatenate([dq0.astype(BF16), dga0, dglu0, dgb0, dkv0.astype(BF16)], axis=1)
    dxn0 = _mm([(dz0, w0)], F32, "in_proj0_dx", 544, 1024, tb=True)
    d_w0 = _unperm_cols(_mm([(xn0, dz0)], BF16, "in_proj0_dw", 512, 1408, ta=True))
    dh0_first, grad_x, d_ab_pre = _rms_bwd(dxn0, h0, p["ab_pre_norm"], dh1, F32, "rms_bwd0", split=True)

    grads = {
        "meta_tokens": dh0_first[PAD:TB], "ab_pre_norm": d_ab_pre, "ab_w_in": d_w0, "ab_sinks": d_sinks[0:1, 0:8],
        "ab_conv_w": d_convw[0:CONV_W], "ab_conv_b": d_small[0:1], "ab_conv_ln_g": d_small[1:2],
        "ab_conv_ln_b": d_small[2:3], "ab_w_pw2": d_wpw, "ab_w_out": d_wo0, "ab_post_norm": d_ab_post,
        "sb_pre_norm": d_sb_pre, "sb_w_in": d_w1, "sb_w_out": d_wo1, "sb_post_norm": d_sb_post,
    }
    return loss_row, grad_x, grads


SMALL_ROWS = 80
REP_ROWS = 32

WEIGHTS = ["meta_tokens", "ab_pre_norm", "ab_w_in", "ab_sinks", "ab_conv_w", "ab_conv_b", "ab_conv_ln_g",
           "ab_conv_ln_b", "ab_w_pw2", "ab_w_out", "ab_post_norm", "sb_pre_norm", "sb_w_in", "sb_w_out",
           "sb_post_norm"]
BIG = ["ab_w_in", "ab_w_out", "ab_w_pw2", "sb_w_in", "sb_w_out"]


def _pack_small(conv_w, meta, sb_pre, sb_post):
    rows = jnp.concatenate([conv_w, meta.reshape(32, 128), sb_pre.reshape(2, 128), sb_post.reshape(2, 128)], axis=0)
    return jnp.concatenate([rows, jnp.zeros((SMALL_ROWS - rows.shape[0], 128), F32)], axis=0)


def _unpack_small(s):
    return s[0:31], s[31:63].reshape(16, 256), s[63:65].reshape(1, 256), s[65:67].reshape(1, 256)


REP_LOSS = 3592


def _pack_rep(pre, post, conv_b, ln_g, ln_b, sinks, extra=None):
    flat = jnp.concatenate([pre.reshape(-1), post.reshape(-1), conv_b.reshape(-1), ln_g.reshape(-1),
                            ln_b.reshape(-1), sinks.reshape(-1)] + ([] if extra is None else [extra.reshape(-1)]))
    flat = jnp.concatenate([flat, jnp.zeros((REP_ROWS * 128 - flat.shape[0],), F32)])
    return flat.reshape(REP_ROWS, 128)


def _unpack_rep(r):
    f = r.reshape(-1)
    return (f[0:1024].reshape(1, 1024), f[1024:2048].reshape(1, 1024), f[2048:2560].reshape(1, 512),
            f[2560:3072].reshape(1, 512), f[3072:3584].reshape(1, 512), f[3584:3592].reshape(1, 8))


def _cols_to_chips(w, width):
    return w.reshape(w.shape[0], N_CHIPS, width).transpose(1, 0, 2)


def _chips_to_cols(w):
    return w.transpose(1, 0, 2).reshape(w.shape[1], -1)


def kernel(x, meta_tokens, ab_pre_norm, ab_w_in, ab_sinks, ab_conv_w, ab_conv_b, ab_conv_ln_g, ab_conv_ln_b, ab_w_pw2, ab_w_out, ab_post_norm, sb_pre_norm, sb_w_in, sb_w_out, sb_post_norm, loss_target, m_meta_tokens, m_ab_pre_norm, m_ab_w_in, m_ab_sinks, m_ab_conv_w, m_ab_conv_b, m_ab_conv_ln_g, m_ab_conv_ln_b, m_ab_w_pw2, m_ab_w_out, m_ab_post_norm, m_sb_pre_norm, m_sb_w_in, m_sb_w_out, m_sb_post_norm, v_meta_tokens, v_ab_pre_norm, v_ab_w_in, v_ab_sinks, v_ab_conv_w, v_ab_conv_b, v_ab_conv_ln_g, v_ab_conv_ln_b, v_ab_w_pw2, v_ab_w_out, v_ab_post_norm, v_sb_pre_norm, v_sb_w_in, v_sb_w_out, v_sb_post_norm):
    w = dict(meta_tokens=meta_tokens, ab_pre_norm=ab_pre_norm, ab_w_in=ab_w_in, ab_sinks=ab_sinks,
             ab_conv_w=ab_conv_w, ab_conv_b=ab_conv_b, ab_conv_ln_g=ab_conv_ln_g, ab_conv_ln_b=ab_conv_ln_b,
             ab_w_pw2=ab_w_pw2, ab_w_out=ab_w_out, ab_post_norm=ab_post_norm, sb_pre_norm=sb_pre_norm,
             sb_w_in=sb_w_in, sb_w_out=sb_w_out, sb_post_norm=sb_post_norm)
    m = dict(meta_tokens=m_meta_tokens, ab_pre_norm=m_ab_pre_norm, ab_w_in=m_ab_w_in, ab_sinks=m_ab_sinks,
             ab_conv_w=m_ab_conv_w, ab_conv_b=m_ab_conv_b, ab_conv_ln_g=m_ab_conv_ln_g,
             ab_conv_ln_b=m_ab_conv_ln_b, ab_w_pw2=m_ab_w_pw2, ab_w_out=m_ab_w_out, ab_post_norm=m_ab_post_norm,
             sb_pre_norm=m_sb_pre_norm, sb_w_in=m_sb_w_in, sb_w_out=m_sb_w_out, sb_post_norm=m_sb_post_norm)
    v = dict(meta_tokens=v_meta_tokens, ab_pre_norm=v_ab_pre_norm, ab_w_in=v_ab_w_in, ab_sinks=v_ab_sinks,
             ab_conv_w=v_ab_conv_w, ab_conv_b=v_ab_conv_b, ab_conv_ln_g=v_ab_conv_ln_g,
             ab_conv_ln_b=v_ab_conv_ln_b, ab_w_pw2=v_ab_w_pw2, ab_w_out=v_ab_w_out, ab_post_norm=v_ab_post_norm,
             sb_pre_norm=v_sb_pre_norm, sb_w_in=v_sb_w_in, sb_w_out=v_sb_w_out, sb_post_norm=v_sb_post_norm)

    def small_of(d):
        return _pack_small(d["ab_conv_w"][0], d["meta_tokens"], d["sb_pre_norm"], d["sb_post_norm"])

    def rep_of(d):
        return _pack_rep(d["ab_pre_norm"], d["ab_post_norm"], d["ab_conv_b"], d["ab_conv_ln_g"], d["ab_conv_ln_b"],
                         d["ab_sinks"])

    gathered = _gather_chips([w[k][0].astype(BF16) for k in BIG] + [small_of(w)])
    g_in0, g_out0, g_pw2, g_in1, g_out1, g_small = gathered
    conv_w_f = _chips_to_cols(g_small[:, 0:31])
    meta_f = _chips_to_cols(g_small[:, 31:63].reshape(N_CHIPS, 16, 256))
    sb_pre_f = g_small[:, 63:65].reshape(1, D)
    sb_post_f = g_small[:, 65:67].reshape(1, D)
    full = {
        "meta_tokens": meta_f, "ab_pre_norm": ab_pre_norm, "ab_w_in": _chips_to_cols(g_in0),
        "ab_sinks": ab_sinks, "ab_conv_w": conv_w_f, "ab_conv_b": ab_conv_b, "ab_conv_ln_g": ab_conv_ln_g,
        "ab_conv_ln_b": ab_conv_ln_b, "ab_w_pw2": g_pw2.reshape(CC, CC), "ab_w_out": g_out0.reshape(D, D),
        "ab_post_norm": ab_post_norm, "sb_pre_norm": sb_pre_f, "sb_w_in": g_in1, "sb_w_out": g_out1.reshape(D, D),
        "sb_post_norm": sb_post_f,
    }

    loss_row, grad_x, g = _local_step(x[0], loss_target[0], full)

    send = [_cols_to_chips(g["ab_w_in"], 704), g["ab_w_out"].reshape(N_CHIPS, 256, D),
            g["ab_w_pw2"].reshape(N_CHIPS, 128, CC), g["sb_w_in"], g["sb_w_out"].reshape(N_CHIPS, 256, D)]
    gs_conv = _cols_to_chips(g["ab_conv_w"], 128)
    gs_meta = _cols_to_chips(g["meta_tokens"], 256)
    gs_pre = g["sb_pre_norm"].reshape(N_CHIPS, 1, 256)
    gs_post = g["sb_post_norm"].reshape(N_CHIPS, 1, 256)
    send.append(jnp.stack([_pack_small(gs_conv[j], gs_meta[j], gs_pre[j], gs_post[j]) for j in range(N_CHIPS)]))
    names = BIG + ["small"]
    own, got = _pair_exchange(send)
    pair = [_sum_pair(o, t, o.dtype, "sum_pair_" + nm) for o, t, nm in zip(own, got, names)]
    mine, land = _scatter_chips([pr[0] for pr in pair], [pr[1] for pr in pair])
    halves = [_sum_shard(mi, la, "sum_shard_" + nm) for mi, la, nm in zip(mine, land, names)]
    total = _join_cores(halves)

    rep_g = _pack_rep(g["ab_pre_norm"], g["ab_post_norm"], g["ab_conv_b"], g["ab_conv_ln_g"], g["ab_conv_ln_b"],
                      g["ab_sinks"], loss_row[0:1, 0:1])
    rep_sum = _sum8(_gather_all(rep_g), "sum8_rep")
    loss = rep_sum.reshape(-1)[REP_LOSS]

    out_g, out_d, out_m, out_v = {}, {}, {}, {}
    for i, k in enumerate(BIG):
        shp = w[k].shape
        res = _adamw(w[k][0], [total[i]], m[k][0], v[k][0], "adamw_" + k)
        out_g[k], out_d[k], out_m[k], out_v[k] = [r.reshape(shp) for r in res]
    res = _adamw(small_of(w), [total[5]], small_of(m), small_of(v), "adamw_small")
    for dst, r in zip((out_g, out_d, out_m, out_v), res):
        cw, mt, pre, post = _unpack_small(r)
        dst["ab_conv_w"], dst["meta_tokens"], dst["sb_pre_norm"], dst["sb_post_norm"] = cw[None], mt, pre, post
    res = _adamw(rep_of(w), [rep_sum], rep_of(m), rep_of(v), "adamw_rep")
    for dst, r in zip((out_g, out_d, out_m, out_v), res):
        (dst["ab_pre_norm"], dst["ab_post_norm"], dst["ab_conv_b"], dst["ab_conv_ln_g"], dst["ab_conv_ln_b"],
         dst["ab_sinks"]) = _unpack_rep(r)

    return (loss, grad_x[None], *[out_g[k] for k in WEIGHTS], *[out_d[k] for k in WEIGHTS],
            *[out_m[k] for k in WEIGHTS], *[out_v[k] for k in WEIGHTS])
```

```python
import functools

import jax
import jax.numpy as jnp
from jax import lax
from jax.experimental import pallas as pl
from jax.experimental.pallas import tpu as pltpu

F32 = jnp.float32
BF16 = jnp.bfloat16

D = 1024
SEQ = 2048
N_META = 16
TB = 128
PAD = TB - N_META
R = SEQ + TB
NB = R // TB
HD = 64
ROPE_THETA = 10000.0
NORM_EPS = 1e-6
LN_EPS = 1e-5
NEG = -1e30
SWA_HEADS = 8
CONV_W = 31
SCALE = HD ** -0.5
N_CHIPS = 4

C_Q, C_GA, C_GLU, C_GB, C_K, C_V = 0, 512, 1024, 2048, 2560, 2688
AB_IN = 2816

ADAM_LR, ADAM_B1, ADAM_B2, ADAM_EPS, ADAM_WD, ADAM_STEP = 0.001, 0.9, 0.999, 1e-08, 0.01, 10

VMEM_LIMIT = 56 * 1024 * 1024


def _cp(sem):
    return pltpu.CompilerParams(dimension_semantics=sem, vmem_limit_bytes=VMEM_LIMIT)


def _sig(x):
    return 1.0 / (1.0 + jnp.exp(-x))


def _dot(a, b):
    return lax.dot_general(a, b, (((1,), (0,)), ((), ())), preferred_element_type=F32)


def _dot_nt(a, b):
    return lax.dot_general(a, b, (((1,), (1,)), ((), ())), preferred_element_type=F32)


def _dot_tn(a, b):
    return lax.dot_general(a, b, (((0,), (0,)), ((), ())), preferred_element_type=F32)


def _mm(pairs, out_dtype, name, tm, tn, ta=False, tb=False):
    pairs = [(a, b if isinstance(b, tuple) else (b, None)) for a, b in pairs]
    a0, (b0, _) = pairs[0]
    m = a0.shape[1] if ta else a0.shape[0]
    n = b0.shape[-2] if tb else b0.shape[-1]
    npairs = len(pairs)
    dims = (((0 if ta else 1,), (1 if tb else 0,)), ((), ()))

    def body(*refs):
        o_ref = refs[2 * npairs]
        acc = None
        for i in range(npairs):
            t = lax.dot_general(refs[2 * i][...].astype(BF16), refs[2 * i + 1][...].astype(BF16), dims,
                                preferred_element_type=F32)
            acc = t if acc is None else acc + t
        o_ref[...] = acc.astype(out_dtype)

    in_specs, args = [], []
    for a, (b, sel) in pairs:
        k = a.shape[0] if ta else a.shape[1]
        in_specs.append(pl.BlockSpec((k, tm), lambda i, j: (0, i)) if ta else pl.BlockSpec((tm, k), lambda i, j: (i, 0)))
        bshape, bidx = ((tn, k), lambda i, j: (j, 0)) if tb else ((k, tn), lambda i, j: (0, j))
        if sel is None:
            in_specs.append(pl.BlockSpec(bshape, bidx))
        else:
            in_specs.append(pl.BlockSpec((None,) + bshape, functools.partial(lambda i, j, f, s: (s,) + f(i, j), f=bidx, s=sel)))
        args += [a, b]
    return pl.pallas_call(
        body, grid=(m // tm, n // tn), in_specs=in_specs,
        out_specs=pl.BlockSpec((tm, tn), lambda i, j: (i, j)),
        out_shape=jax.ShapeDtypeStruct((m, n), out_dtype), name=name,
        compiler_params=_cp(("parallel", "parallel")))(*args)


def _rms_fwd(h, g, name):
    def body(h_ref, g_ref, o_ref):
        x = h_ref[...]
        r = lax.rsqrt(jnp.mean(x * x, axis=1, keepdims=True) + NORM_EPS)
        o_ref[...] = (x * r * g_ref[...]).astype(BF16)

    return pl.pallas_call(
        body, grid=(NB,),
        in_specs=[pl.BlockSpec((TB, D), lambda n: (n, 0)), pl.BlockSpec((1, D), lambda n: (0, 0))],
        out_specs=pl.BlockSpec((TB, D), lambda n: (n, 0)),
        out_shape=jax.ShapeDtypeStruct((R, D), BF16), name=name, compiler_params=_cp(("parallel",)))(h, g)


def _post_fwd(h, y, g, name):
    def body(h_ref, y_ref, g_ref, o_ref):
        yv = y_ref[...]
        r = lax.rsqrt(jnp.mean(yv * yv, axis=1, keepdims=True) + NORM_EPS)
        o_ref[...] = h_ref[...] + yv * r * g_ref[...]

    blk = pl.BlockSpec((TB, D), lambda n: (n, 0))
    return pl.pallas_call(
        body, grid=(NB,), in_specs=[blk, blk, pl.BlockSpec((1, D), lambda n: (0, 0))], out_specs=blk,
        out_shape=jax.ShapeDtypeStruct((R, D), F32), name=name, compiler_params=_cp(("parallel",)))(h, y, g)


def _rms_bwd(dout, x, g, res, out_dtype, name, split=False):
    has_res = res is not None

    def body(*refs):
        if split:
            refs = list(refs)
            dx_rest_ref = refs.pop(-2)
        if has_res:
            d_ref, x_ref, g_ref, r_ref, dx_ref, dg_ref = refs
        else:
            d_ref, x_ref, g_ref, dx_ref, dg_ref = refs
        n = pl.program_id(0)
        xv = x_ref[...]
        dv = d_ref[...]
        r = lax.rsqrt(jnp.mean(xv * xv, axis=1, keepdims=True) + NORM_EPS)
        xh = xv * r
        dxh = dv * g_ref[...]
        dx = r * (dxh - xh * jnp.mean(dxh * xh, axis=1, keepdims=True))
        if has_res:
            dx = dx + r_ref[...]
        row = lax.broadcasted_iota(jnp.int32, (TB, D), 0) + n * TB
        dx = jnp.where(row >= PAD, dx, 0.0).astype(out_dtype)
        if split:
            @pl.when(n == 0)
            def _():
                dx_ref[...] = dx

            @pl.when(n > 0)
            def _():
                dx_rest_ref[...] = dx
        else:
            dx_ref[...] = dx

        @pl.when(n == 0)
        def _():
            dg_ref[...] = jnp.zeros_like(dg_ref)

        dg_ref[...] += jnp.sum(dv * xh, axis=0, keepdims=True)

    blk = pl.BlockSpec((TB, D), lambda n: (n, 0))
    vec = pl.BlockSpec((1, D), lambda n: (0, 0))
    ins = [dout, x, g] + ([res] if has_res else [])
    in_specs = [blk, blk, vec] + ([blk] if has_res else [])
    if split:
        out_specs = [pl.BlockSpec((TB, D), lambda n: (0, 0)), pl.BlockSpec((TB, D), lambda n: (jnp.maximum(n - 1, 0), 0)), vec]
        out_shape = [jax.ShapeDtypeStruct((TB, D), out_dtype), jax.ShapeDtypeStruct((SEQ, D), out_dtype),
                     jax.ShapeDtypeStruct((1, D), F32)]
    else:
        out_specs = [blk, vec]
        out_shape = [jax.ShapeDtypeStruct((R, D), out_dtype), jax.ShapeDtypeStruct((1, D), F32)]
    return pl.pallas_call(
        body, grid=(NB,), in_specs=in_specs, out_specs=out_specs, out_shape=out_shape,
        name=name, compiler_params=_cp(("arbitrary",)))(*ins)


def _post_rms_fwd(h, y, g_post, g_next, name):
    def body(h_ref, y_ref, gp_ref, gn_ref, o_ref, x_ref):
        yv = y_ref[...]
        r = lax.rsqrt(jnp.mean(yv * yv, axis=1, keepdims=True) + NORM_EPS)
        hn = h_ref[...] + yv * r * gp_ref[...]
        o_ref[...] = hn
        r2 = lax.rsqrt(jnp.mean(hn * hn, axis=1, keepdims=True) + NORM_EPS)
        x_ref[...] = (hn * r2 * gn_ref[...]).astype(BF16)

    blk = pl.BlockSpec((TB, D), lambda n: (n, 0))
    vec = pl.BlockSpec((1, D), lambda n: (0, 0))
    return pl.pallas_call(
        body, grid=(NB,), in_specs=[blk, blk, vec, vec], out_specs=[blk, blk],
        out_shape=[jax.ShapeDtypeStruct((R, D), F32), jax.ShapeDtypeStruct((R, D), BF16)],
        name=name, compiler_params=_cp(("parallel",)))(h, y, g_post, g_next)


def _rms_post_bwd(dxn, h, g, res, y, g_post, name):
    def body(d_ref, h_ref, g_ref, r_ref, y_ref, gp_ref, dh_ref, dg_ref, dy_ref, dgp_ref):
        n = pl.program_id(0)

        @pl.when(n == 0)
        def _():
            dg_ref[...] = jnp.zeros_like(dg_ref)
            dgp_ref[...] = jnp.zeros_like(dgp_ref)

        hv, dv = h_ref[...], d_ref[...]
        r = lax.rsqrt(jnp.mean(hv * hv, axis=1, keepdims=True) + NORM_EPS)
        xh = hv * r
        dxh = dv * g_ref[...]
        dh = r * (dxh - xh * jnp.mean(dxh * xh, axis=1, keepdims=True)) + r_ref[...]
        row = lax.broadcasted_iota(jnp.int32, (TB, D), 0) + n * TB
        dh = jnp.where(row >= PAD, dh, 0.0)
        dh_ref[...] = dh
        dg_ref[...] += jnp.sum(dv * xh, axis=0, keepdims=True)
        yv = y_ref[...]
        ry = lax.rsqrt(jnp.mean(yv * yv, axis=1, keepdims=True) + NORM_EPS)
        yh = yv * ry
        dyh = dh * gp_ref[...]
        dy_ref[...] = (ry * (dyh - yh * jnp.mean(dyh * yh, axis=1, keepdims=True))).astype(BF16)
        dgp_ref[...] += jnp.sum(dh * yh, axis=0, keepdims=True)

    blk = pl.BlockSpec((TB, D), lambda n: (n, 0))
    vec = pl.BlockSpec((1, D), lambda n: (0, 0))
    return pl.pallas_call(
        body, grid=(NB,), in_specs=[blk, blk, vec, blk, blk, vec], out_specs=[blk, vec, blk, vec],
        out_shape=[jax.ShapeDtypeStruct((R, D), F32), jax.ShapeDtypeStruct((1, D), F32),
                   jax.ShapeDtypeStruct((R, D), BF16), jax.ShapeDtypeStruct((1, D), F32)],
        name=name, compiler_params=_cp(("arbitrary",)))(dxn, h, g, res, y, g_post)


GW = 512


def _gate_fwd(o, gsrc, goff, name):
    w = o.shape[1]

    def body(o_ref, g_ref, m_ref):
        gv = g_ref[...]
        m_ref[...] = (o_ref[...] * (gv * _sig(gv))).astype(BF16)

    gb = goff // GW
    return pl.pallas_call(
        body, grid=(NB, w // GW),
        in_specs=[pl.BlockSpec((TB, GW), lambda n, j: (n, j)), pl.BlockSpec((TB, GW), lambda n, j: (n, gb + j))],
        out_specs=pl.BlockSpec((TB, GW), lambda n, j: (n, j)),
        out_shape=jax.ShapeDtypeStruct((R, w), BF16), name=name,
        compiler_params=_cp(("parallel", "parallel")))(o, gsrc)


def _gate_bwd(dsrc, doff, o, gsrc, goff, name):
    w = o.shape[1]

    def body(d_ref, o_ref, g_ref, do_ref, dg_ref):
        gv = g_ref[...]
        dv = d_ref[...]
        s = _sig(gv)
        do_ref[...] = dv * (gv * s)
        dg_ref[...] = (dv * o_ref[...] * (s * (1.0 + gv * (1.0 - s)))).astype(BF16)

    db, gb = doff // GW, goff // GW
    blk = pl.BlockSpec((TB, GW), lambda n, j: (n, j))
    return pl.pallas_call(
        body, grid=(NB, w // GW),
        in_specs=[pl.BlockSpec((TB, GW), lambda n, j: (n, db + j)), blk,
                  pl.BlockSpec((TB, GW), lambda n, j: (n, gb + j))],
        out_specs=[blk, blk],
        out_shape=[jax.ShapeDtypeStruct((R, w), F32), jax.ShapeDtypeStruct((R, w), BF16)], name=name,
        compiler_params=_cp(("parallel", "parallel")))(dsrc, o, gsrc)


def _tail(h, y, g, target):
    def body(h_ref, y_ref, g_ref, t_ref, d_ref, dy_ref, dg_ref, l_ref):
        n = pl.program_id(0)

        @pl.when(n == 0)
        def _():
            d_ref[...] = jnp.zeros_like(d_ref)
            dy_ref[...] = jnp.zeros_like(dy_ref)
            dg_ref[...] = jnp.zeros_like(dg_ref)
            l_ref[...] = jnp.zeros_like(l_ref)

        @pl.when(n > 0)
        def _():
            yv = y_ref[...]
            r = lax.rsqrt(jnp.mean(yv * yv, axis=1, keepdims=True) + NORM_EPS)
            yh = yv * r
            err = (h_ref[...] + yh * g_ref[...]) - t_ref[...]
            dv = err * (1.0 / D)
            d_ref[...] = dv
            l_ref[...] += jnp.sum(err * err, axis=0, keepdims=True)
            dyh = dv * g_ref[...]
            dy_ref[...] = (r * (dyh - yh * jnp.mean(dyh * yh, axis=1, keepdims=True))).astype(BF16)
            dg_ref[...] += jnp.sum(dv * yh, axis=0, keepdims=True)

        @pl.when(n == NB - 1)
        def _():
            tot = jnp.sum(l_ref[...], axis=1, keepdims=True) * (0.5 / D)
            l_ref[...] = jnp.broadcast_to(tot, (1, D))

    blk = pl.BlockSpec((TB, D), lambda n: (n, 0))
    vec = pl.BlockSpec((1, D), lambda n: (0, 0))
    return pl.pallas_call(
        body, grid=(NB,),
        in_specs=[blk, blk, vec, pl.BlockSpec((TB, D), lambda n: (jnp.maximum(n - 1, 0), 0))],
        out_specs=[blk, blk, vec, vec],
        out_shape=[jax.ShapeDtypeStruct((R, D), F32), jax.ShapeDtypeStruct((R, D), BF16),
                   jax.ShapeDtypeStruct((1, D), F32), jax.ShapeDtypeStruct((1, D), F32)],
        name="tail", compiler_params=_cp(("arbitrary",)))(h, y, g, target)


def _lane_row(shape):
    return lax.broadcasted_iota(jnp.int32, shape, 1), lax.broadcasted_iota(jnp.int32, shape, 0)


def _rot_half(x, lane):
    return jnp.where(lane % HD < HD // 2, pltpu.roll(x, 128 - HD // 2, 1), pltpu.roll(x, HD // 2, 1))


def _swa_blocks(n):
    return (0, jnp.maximum(n - 1, 0), n)


SWA_STACKS = ((0, 0), (0, 1), (1, 0), (1, 1))


def _swa_masks(n, lane, row):
    qpos = n * TB + (row & (TB - 1))
    kp = (n - 1) * TB + lane
    kc = n * TB + lane
    m0 = (lane >= PAD) & (qpos - lane >= TB)
    mp = (kp >= PAD) & (qpos >= kp) & (qpos - kp < TB)
    mc = (kc >= PAD) & (qpos >= kc)
    return (m0, mp, mc)


def _stack_pair(xa, xb, par):
    lane = lax.broadcasted_iota(jnp.int32, (TB, 128), 1)
    keep = (lane < HD) if par == 0 else (lane >= HD)
    return jnp.concatenate([jnp.where(keep, xa, 0.0), jnp.where(keep, xb, 0.0)], axis=0)


def _per_head(a, b):
    row = lax.broadcasted_iota(jnp.int32, (2 * TB, 1), 0)
    return jnp.where(row < TB, a, b)


def _swa_load(n, zq_ref, zkv_ref, cs_ref, sn_ref, lane):
    r0 = pl.multiple_of(n * TB, TB)
    csq, snq = cs_ref[pl.ds(r0, TB), :], sn_ref[pl.ds(r0, TB), :]
    qc = []
    for c in range(4):
        x = zq_ref[:, c * 128:(c + 1) * 128]
        qc.append((x * csq + _rot_half(x, lane) * snq) * SCALE)
    qst = [_stack_pair(qc[2 * g], qc[2 * g + 1], par).astype(BF16) for g, par in SWA_STACKS]
    kvs = []
    for b in _swa_blocks(n):
        b0 = pl.multiple_of(b * TB, TB)
        csb, snb = cs_ref[pl.ds(b0, TB), :], sn_ref[pl.ds(b0, TB), :]
        kx = zkv_ref[pl.ds(b0, TB), 0:128]
        kr = kx * csb + _rot_half(kx, lane) * snb
        vx = zkv_ref[pl.ds(b0, TB), 128:256]
        kvs.append((kr.astype(BF16), pltpu.roll(kr, HD, 1).astype(BF16),
                    vx.astype(BF16), pltpu.roll(vx, HD, 1).astype(BF16), csb, snb, b0))
    return qst, (csq, snq), kvs


def _swa_fwd(z0, cs, sn, sinks):
    def body(zq_ref, zkv_ref, cs_ref, sn_ref, sk_ref, ga_ref, o_ref, a_ref, lse_ref):
        n = pl.program_id(0)
        lane, row = _lane_row((TB, 128))
        lo = lane < HD
        masks = _swa_masks(n, *_lane_row((2 * TB, 128)))
        qst, _, kvs = _swa_load(n, zq_ref, zkv_ref, cs_ref, sn_ref, lane)
        ss = [[jnp.where(m, _dot_nt(qst[si], k if par == g else ka), NEG)
               for (k, ka, _, _, _, _, _), m in zip(kvs, masks)] for si, (g, par) in enumerate(SWA_STACKS)]
        o2, lse2 = [], []
        for si, (g, par) in enumerate(SWA_STACKS):
            sink = _per_head(sk_ref[0, 4 * g + par], sk_ref[0, 4 * g + 2 + par])
            s = ss[si]
            mx = jnp.maximum(jnp.maximum(jnp.max(s[0], axis=1, keepdims=True), jnp.max(s[1], axis=1, keepdims=True)),
                             jnp.max(s[2], axis=1, keepdims=True))
            mx = jnp.maximum(mx, sink)
            es = [jnp.exp(sb - mx) for sb in s]
            den = (jnp.sum(es[0], axis=1, keepdims=True) + jnp.sum(es[1], axis=1, keepdims=True)
                   + jnp.sum(es[2], axis=1, keepdims=True) + jnp.exp(sink - mx))
            inv = 1.0 / den
            t = jnp.zeros((2 * TB, 128), F32)
            for (_, _, v, va, _, _, _), e in zip(kvs, es):
                t = t + _dot((e * inv).astype(BF16), v if par == g else va)
            o2.append(t)
            lse2.append(mx + jnp.log(den))
        lse_t = jnp.zeros((TB, 128), F32)
        for g in range(2):
            for t in range(2):
                rows = slice(t * TB, (t + 1) * TB)
                c = 2 * g + t
                oc = jnp.where(lo, o2[2 * g][rows], o2[2 * g + 1][rows])
                o_ref[:, c * 128:(c + 1) * 128] = oc
                gv = ga_ref[:, c * 128:(c + 1) * 128]
                a_ref[:, c * 128:(c + 1) * 128] = (oc * (gv * _sig(gv))).astype(BF16)
                for par in range(2):
                    lse_t = jnp.where(lane == 4 * g + 2 * t + par, lse2[2 * g + par][rows], lse_t)
        lse_ref[...] = lse_t

    full = pl.BlockSpec((R, 128), lambda n: (0, 0))
    return pl.pallas_call(
        body, grid=(NB,),
        in_specs=[pl.BlockSpec((TB, 512), lambda n: (n, C_Q // 512)),
                  pl.BlockSpec((R, 256), lambda n: (0, C_K // 256)), full, full,
                  pl.BlockSpec(memory_space=pltpu.SMEM), pl.BlockSpec((TB, 512), lambda n: (n, C_GA // 512))],
        out_specs=[pl.BlockSpec((TB, 512), lambda n: (n, 0)), pl.BlockSpec((TB, 512), lambda n: (n, 0)),
                   pl.BlockSpec((TB, 128), lambda n: (n, 0))],
        out_shape=[jax.ShapeDtypeStruct((R, 512), F32), jax.ShapeDtypeStruct((R, 512), BF16),
                   jax.ShapeDtypeStruct((R, 128), F32)],
        name="swa_fwd", compiler_params=_cp(("parallel",)))(z0, z0, cs, sn, sinks, z0)


def _swa_bwd(z0, cs, sn, sinks, o, dmix, lse):
    def body(zq_ref, zkv_ref, cs_ref, sn_ref, sk_ref, ga_ref, o_ref, dm_ref, lse_ref,
             dq_ref, dga_ref, dkv_ref, dsk_ref, do_ref, acc_ref):
        n = pl.program_id(0)

        @pl.when(n == 0)
        def _():
            acc_ref[...] = jnp.zeros_like(acc_ref)
            dsk_ref[...] = jnp.zeros_like(dsk_ref)

        gv, dmv = ga_ref[...], dm_ref[...]
        sg = _sig(gv)
        dga_ref[...] = (dmv * o_ref[...] * (sg * (1.0 + gv * (1.0 - sg)))).astype(BF16)
        do_ref[...] = dmv * (gv * sg)
        lane, row = _lane_row((TB, 128))
        lo = lane < HD
        masks = _swa_masks(n, *_lane_row((2 * TB, 128)))
        qst, (csq, snq), kvs = _swa_load(n, zq_ref, zkv_ref, cs_ref, sn_ref, lane)
        lse_t = lse_ref[...]
        ss = [[jnp.where(m, _dot_nt(qst[si], k if par == g else ka), NEG)
               for (k, ka, _, _, _, _, _), m in zip(kvs, masks)] for si, (g, par) in enumerate(SWA_STACKS)]
        dobs, deltas, lses, dps = [], [], [], []
        for g, par in SWA_STACKS:
            ca, cb = slice(2 * g * 128, (2 * g + 1) * 128), slice((2 * g + 1) * 128, (2 * g + 2) * 128)
            dom = _stack_pair(do_ref[:, ca], do_ref[:, cb], par)
            deltas.append(jnp.sum(dom * jnp.concatenate([o_ref[:, ca], o_ref[:, cb]], axis=0), axis=1, keepdims=True))
            dob = dom.astype(BF16)
            dobs.append(dob)
            lses.append(jnp.concatenate(
                [jnp.sum(jnp.where(lane == 4 * g + 2 * t + par, lse_t, 0.0), axis=1, keepdims=True) for t in range(2)],
                axis=0))
            dps.append([_dot_nt(dob, v if par == g else va) for (_, _, v, va, _, _, _) in kvs])
        dk_al = [jnp.zeros((TB, 128), F32) for _ in range(3)]
        dk_mis = [jnp.zeros((TB, 128), F32) for _ in range(3)]
        dv_al = [jnp.zeros((TB, 128), F32) for _ in range(3)]
        dv_mis = [jnp.zeros((TB, 128), F32) for _ in range(3)]
        dsk_t = jnp.zeros((TB, 128), F32)
        dq2 = []
        for si, (g, par) in enumerate(SWA_STACKS):
            dqt = jnp.zeros((2 * TB, 128), F32)
            for bi, (k, ka, _, _, _, _, _) in enumerate(kvs):
                p = jnp.exp(ss[si][bi] - lses[si])
                ds = (p * (dps[si][bi] - deltas[si])).astype(BF16)
                dqt = dqt + _dot(ds, k if par == g else ka)
                dkh = _dot_tn(ds, qst[si])
                dvh = _dot_tn(p.astype(BF16), dobs[si])
                if par == g:
                    dk_al[bi] = dk_al[bi] + dkh
                    dv_al[bi] = dv_al[bi] + dvh
                else:
                    dk_mis[bi] = dk_mis[bi] + dkh
                    dv_mis[bi] = dv_mis[bi] + dvh
            dq2.append(dqt)
            sink = _per_head(sk_ref[0, 4 * g + par], sk_ref[0, 4 * g + 2 + par])
            dsk = -jnp.exp(sink - lses[si]) * deltas[si]
            for t in range(2):
                dsk_t = jnp.where(lane == 4 * g + 2 * t + par, dsk[t * TB:(t + 1) * TB], dsk_t)
        for g in range(2):
            for t in range(2):
                rows = slice(t * TB, (t + 1) * TB)
                c = 2 * g + t
                dqc = jnp.where(lo, dq2[2 * g][rows], dq2[2 * g + 1][rows]) * SCALE
                dq_ref[:, c * 128:(c + 1) * 128] = (dqc * csq + _rot_half(dqc * snq, lane)).astype(BF16)
        for bi, (_, _, _, _, csb, snb, b0) in enumerate(kvs):
            dk = dk_al[bi] + pltpu.roll(dk_mis[bi], HD, 1)
            dv = dv_al[bi] + pltpu.roll(dv_mis[bi], HD, 1)
            acc_ref[pl.ds(b0, TB), 0:128] += dk * csb + _rot_half(dk * snb, lane)
            acc_ref[pl.ds(b0, TB), 128:256] += dv
        dsk_ref[0:1, :] += jnp.sum(dsk_t, axis=0, keepdims=True)

        @pl.when(n == NB - 1)
        def _():
            dkv_ref[...] = acc_ref[...].astype(BF16)

    full = pl.BlockSpec((R, 128), lambda n: (0, 0))
    b512 = pl.BlockSpec((TB, 512), lambda n: (n, 0))
    return pl.pallas_call(
        body, grid=(NB,),
        in_specs=[pl.BlockSpec((TB, 512), lambda n: (n, C_Q // 512)),
                  pl.BlockSpec((R, 256), lambda n: (0, C_K // 256)), full, full,
                  pl.BlockSpec(memory_space=pltpu.SMEM), pl.BlockSpec((TB, 512), lambda n: (n, C_GA // 512)),
                  b512, b512, pl.BlockSpec((TB, 128), lambda n: (n, 0))],
        out_specs=[b512, b512, pl.BlockSpec((R, 256), lambda n: (0, 0)), pl.BlockSpec((8, 128), lambda n: (0, 0))],
        out_shape=[jax.ShapeDtypeStruct((R, 512), BF16), jax.ShapeDtypeStruct((R, 512), BF16),
                   jax.ShapeDtypeStruct((R, 256), BF16), jax.ShapeDtypeStruct((8, 128), F32)],
        scratch_shapes=[pltpu.VMEM((TB, 512), F32), pltpu.VMEM((R, 256), F32)],
        name="swa_bwd", compiler_params=_cp(("arbitrary",)))(z0, z0, cs, sn, sinks, z0, o, dmix, lse)


CC = 512
HALO = CONV_W - 1


def _conv_fwd(z0, conv_w, conv_b, ln_g, ln_b):
    def body(g_ref, w_ref, cb_ref, lg_ref, lb_ref, cv_ref, s_ref, ubuf):
        n = pl.program_id(0)

        @pl.when(n == 0)
        def _():
            ubuf[0:TB, :] = jnp.zeros((TB, CC), F32)

        u = g_ref[:, 0:CC] * _sig(g_ref[:, CC:2 * CC])
        ubuf[TB:2 * TB, :] = u
        acc = jnp.zeros((TB, CC), F32)
        for w in range(CONV_W):
            acc = acc + ubuf[pl.ds(TB - HALO + w, TB), :] * w_ref[w:w + 1, :]
        cv = acc + cb_ref[...]
        cv_ref[...] = cv
        xc = cv - jnp.mean(cv, axis=1, keepdims=True)
        rs = lax.rsqrt(jnp.mean(xc * xc, axis=1, keepdims=True) + LN_EPS)
        ln = xc * rs * lg_ref[...] + lb_ref[...]
        s_ref[...] = (ln * _sig(ln)).astype(BF16)
        ubuf[0:TB, :] = u

    vec = pl.BlockSpec((1, CC), lambda n: (0, 0))
    blk = pl.BlockSpec((TB, CC), lambda n: (n, 0))
    return pl.pallas_call(
        body, grid=(NB,),
        in_specs=[pl.BlockSpec((TB, 2 * CC), lambda n: (n, C_GLU // (2 * CC))),
                  pl.BlockSpec((32, CC), lambda n: (0, 0)), vec, vec, vec],
        out_specs=[blk, blk],
        out_shape=[jax.ShapeDtypeStruct((R, CC), F32), jax.ShapeDtypeStruct((R, CC), BF16)],
        scratch_shapes=[pltpu.VMEM((2 * TB, CC), F32)],
        name="conv_fwd", compiler_params=_cp(("arbitrary",)))(z0, conv_w, conv_b, ln_g, ln_b)


def _conv_bwd(ds, cv, z0, conv_w, ln_g, ln_b):
    def body(ds_ref, cv_ref, g_ref, w_ref, lg_ref, lb_ref, dglu_ref, dw_ref, dsm_ref, dbuf):
        n = pl.program_id(0)

        @pl.when(n == 0)
        def _():
            dbuf[TB:2 * TB, :] = jnp.zeros((TB, CC), F32)
            dw_ref[...] = jnp.zeros_like(dw_ref)
            dsm_ref[...] = jnp.zeros_like(dsm_ref)

        cv = cv_ref[...]
        xc = cv - jnp.mean(cv, axis=1, keepdims=True)
        rs = lax.rsqrt(jnp.mean(xc * xc, axis=1, keepdims=True) + LN_EPS)
        xh = xc * rs
        ln = xh * lg_ref[...] + lb_ref[...]
        sg = _sig(ln)
        dln = ds_ref[...] * (sg * (1.0 + ln * (1.0 - sg)))
        dxh = dln * lg_ref[...]
        dcv = rs * (dxh - jnp.mean(dxh, axis=1, keepdims=True) - xh * jnp.mean(dxh * xh, axis=1, keepdims=True))
        dsm_ref[0:1, :] += jnp.sum(dcv, axis=0, keepdims=True)
        dsm_ref[1:2, :] += jnp.sum(dln * xh, axis=0, keepdims=True)
        dsm_ref[2:3, :] += jnp.sum(dln, axis=0, keepdims=True)
        dbuf[0:TB, :] = dcv
        a = g_ref[:, 0:CC]
        sb = _sig(g_ref[:, CC:2 * CC])
        u = a * sb
        du = jnp.zeros((TB, CC), F32)
        for w in range(CONV_W):
            sh = dbuf[pl.ds(HALO - w, TB), :]
            du = du + sh * w_ref[w:w + 1, :]
            dw_ref[w:w + 1, :] += jnp.sum(u * sh, axis=0, keepdims=True)
        dglu_ref[:, 0:CC] = (du * sb).astype(BF16)
        dglu_ref[:, CC:2 * CC] = (du * a * sb * (1.0 - sb)).astype(BF16)
        dbuf[TB:2 * TB, :] = dcv

    rev = lambda n: (NB - 1 - n, 0)
    vec = pl.BlockSpec((1, CC), lambda n: (0, 0))
    blk = pl.BlockSpec((TB, CC), rev)
    return pl.pallas_call(
        body, grid=(NB,),
        in_specs=[blk, blk, pl.BlockSpec((TB, 2 * CC), lambda n: (NB - 1 - n, C_GLU // (2 * CC))),
                  pl.BlockSpec((32, CC), lambda n: (0, 0)), vec, vec],
        out_specs=[pl.BlockSpec((TB, 2 * CC), rev), pl.BlockSpec((32, CC), lambda n: (0, 0)),
                   pl.BlockSpec((8, CC), lambda n: (0, 0))],
        out_shape=[jax.ShapeDtypeStruct((R, 2 * CC), BF16), jax.ShapeDtypeStruct((32, CC), F32),
                   jax.ShapeDtypeStruct((8, CC), F32)],
        scratch_shapes=[pltpu.VMEM((2 * TB, CC), F32)],
        name="conv_bwd", compiler_params=_cp(("arbitrary",)))(ds, cv, z0, conv_w, ln_g, ln_b)


def _split_dot(x, t):
    hi = x.astype(BF16)
    lo = (x - hi.astype(F32)).astype(BF16)
    return _dot(hi, t) + _dot(lo, t)


def _stack_heads(x):
    lane = lax.broadcasted_iota(jnp.int32, (TB, 128), 1)
    return jnp.concatenate([jnp.where(lane < HD, x, 0.0), jnp.where(lane < HD, 0.0, x)], axis=0).astype(BF16)


def _sb_stack(qv, i):
    lane2, row2 = _lane_row((2 * TB, 128))
    qpos2 = i * TB + (row2 & (TB - 1))
    lane, row = _lane_row((TB, 128))
    return _stack_heads(qv), lane2, qpos2, (row > lane).astype(BF16)


SB_U = 3
SB_DEAD = -104.0


def _sb_fwd(q, k, v, g):
    def body(q_ref, k_ref, v_ref, g_ref, o_ref, m_ref, c_ref, n_ref):
        p, i = pl.program_id(0), pl.program_id(1)
        lane, row = _lane_row((TB, 128))
        lo = lane < HD
        q2, lane2, qpos2, tri_gt = _sb_stack(q_ref[...].astype(F32) * SCALE, i)

        def cond(st):
            t, _, c2 = st
            return jnp.logical_and(i - SB_U * t >= 0, jnp.max(c2) > SB_DEAD)

        def step(st):
            t, acc, c2 = st
            jrs = [i - SB_U * t - u for u in range(SB_U)]
            j0s = [pl.multiple_of(jnp.maximum(jr, 0) * TB, TB) for jr in jrs]
            ks = [k_ref[pl.ds(j0, TB), :] for j0 in j0s]
            zs = [_dot_nt(q2, kj) for kj in ks]
            valids, lbs, l1s = [], [], []
            for jr, z in zip(jrs, zs):
                kpos = jr * TB + lane2
                valid = (kpos >= PAD) & (kpos < qpos2)
                lb = jnp.minimum(z, 0.0) - jnp.log(1.0 + jnp.exp(-jnp.abs(z)))
                valids.append(valid)
                lbs.append(lb)
                l1s.append(jnp.where(valid, lb - z, 0.0))
            sfxs = [_split_dot(l1, tri_gt) for l1 in l1s]
            carries = []
            for jr, l1 in zip(jrs, l1s):
                carries.append(c2)
                c_ref[...] = jnp.where(lane == 2 * jr, c2[0:TB], jnp.where(lane == 2 * jr + 1, c2[TB:2 * TB], c_ref[...]))
                c2 = c2 + jnp.sum(l1, axis=1, keepdims=True)
            for j0, valid, lb, sfx, cu in zip(j0s, valids, lbs, sfxs, carries):
                a = jnp.where(valid, jnp.exp(lb + sfx + cu), 0.0).astype(BF16)
                av = _dot(a, v_ref[pl.ds(j0, TB), :])
                acc = acc + jnp.where(lo, av[0:TB], av[TB:2 * TB])
            return t + 1, acc, c2

        c_ref[...] = jnp.zeros((TB, 128), F32)
        t, acc, _ = lax.while_loop(cond, step, (jnp.int32(0), jnp.zeros((TB, 128), F32), jnp.zeros((2 * TB, 1), F32)))
        o_ref[...] = acc
        gv = g_ref[...]
        m_ref[...] = (acc * (gv * _sig(gv))).astype(BF16)
        n_ref[p, i] = t

    slab = pl.BlockSpec((R, 128), lambda p, i: (0, p))
    blk = pl.BlockSpec((TB, 128), lambda p, i: (i, p))
    sd = jax.ShapeDtypeStruct((R, D), F32)
    return pl.pallas_call(
        body, grid=(D // 128, NB), in_specs=[blk, slab, slab, blk],
        out_specs=[blk, blk, blk, pl.BlockSpec(memory_space=pltpu.SMEM)],
        out_shape=[sd, jax.ShapeDtypeStruct((R, D), BF16), sd, jax.ShapeDtypeStruct((D // 128, NB), jnp.int32)],
        name="sb_fwd", compiler_params=_cp(("arbitrary", "arbitrary")))(q, k, v, g)


def _sb_bwd(trips, q, k, v, car, dm, g, o):
    def body(n_ref, q_ref, k_ref, v_ref, c_ref, dm_ref, g_ref, o_ref, dq_ref, dko_ref, dvo_ref, dg_ref,
             dk_ref, dv_ref):
        p, i = pl.program_id(0), pl.program_id(1)

        @pl.when(i == 0)
        def _():
            dk_ref[...] = jnp.zeros_like(dk_ref)
            dv_ref[...] = jnp.zeros_like(dv_ref)

        lane, row = _lane_row((TB, 128))
        lo = lane < HD
        tri_lt = (row < lane).astype(BF16)
        q2, lane2, qpos2, tri_gt = _sb_stack(q_ref[...].astype(F32) * SCALE, i)
        gv, dmv = g_ref[...], dm_ref[...]
        sg = _sig(gv)
        dg_ref[...] = (dmv * o_ref[...] * (sg * (1.0 + gv * (1.0 - sg)))).astype(BF16)
        do2 = _stack_heads(dmv * (gv * sg))
        ct = c_ref[...]
        trips_i = n_ref[p, i]
        first = jnp.maximum(i + 1 - SB_U * trips_i, 0)

        def step(t, carry):
            dq, g2 = carry
            jrs = [first + SB_U * t + u for u in range(SB_U)]
            j0s = [pl.multiple_of(jnp.minimum(jr, i) * TB, TB) for jr in jrs]
            ks = [k_ref[pl.ds(j0, TB), :] for j0 in j0s]
            vs = [v_ref[pl.ds(j0, TB), :] for j0 in j0s]
            zs = [_dot_nt(q2, kj) for kj in ks]
            das = [_dot_nt(do2, vj) for vj in vs]
            valids, es, lbs, l1s = [], [], [], []
            for jr, z in zip(jrs, zs):
                kpos = jr * TB + lane2
                valid = (kpos >= PAD) & (kpos < qpos2)
                e = jnp.exp(-jnp.abs(z))
                lb = jnp.minimum(z, 0.0) - jnp.log(1.0 + e)
                valids.append(valid)
                es.append(e)
                lbs.append(lb)
                l1s.append(jnp.where(valid, lb - z, 0.0))
            sfxs = [_split_dot(l1, tri_gt) for l1 in l1s]
            a_s, gmats, gpre = [], [], []
            for jr, valid, lb, sfx, da in zip(jrs, valids, lbs, sfxs, das):
                later = jnp.concatenate(
                    [jnp.sum(jnp.where(lane == 2 * jr + hh, ct, 0.0), axis=1, keepdims=True) for hh in range(2)], axis=0)
                a = jnp.where(valid, jnp.exp(lb + sfx + later), 0.0)
                gmat = da * a
                a_s.append(a.astype(BF16))
                gmats.append(gmat)
                gpre.append(g2)
                g2 = g2 + jnp.sum(gmat, axis=1, keepdims=True)
            pres = [gp + _split_dot(gmat, tri_lt) for gp, gmat in zip(gpre, gmats)]
            for j0, kj, valid, z, e, gmat, pre, a in zip(j0s, ks, valids, zs, es, gmats, pres, a_s):
                r = 1.0 / (1.0 + e)
                big = z >= 0.0
                beta = jnp.where(big, r, e * r)
                omb = jnp.where(big, e * r, r)
                dz = jnp.where(valid, gmat * omb - beta * pre, 0.0).astype(BF16)
                dq2 = _dot(dz, kj)
                dq = dq + jnp.where(lo, dq2[0:TB], dq2[TB:2 * TB])
                dk_ref[pl.ds(j0, TB), :] += _dot_tn(dz, q2)
                dv_ref[pl.ds(j0, TB), :] += _dot_tn(a, do2)
            return dq, g2

        dq, _ = lax.fori_loop(0, trips_i, step, (jnp.zeros((TB, 128), F32), jnp.zeros((2 * TB, 1), F32)))
        dq_ref[...] = (dq * SCALE).astype(BF16)

        @pl.when(i == NB - 1)
        def _():
            dko_ref[...] = dk_ref[...].astype(BF16)
            dvo_ref[...] = dv_ref[...].astype(BF16)

    slab = pl.BlockSpec((R, 128), lambda p, i: (0, p))
    blk = pl.BlockSpec((TB, 128), lambda p, i: (i, p))
    sd = jax.ShapeDtypeStruct((R, D), BF16)
    return pl.pallas_call(
        body, grid=(D // 128, NB),
        in_specs=[pl.BlockSpec(memory_space=pltpu.SMEM), blk, slab, slab, blk, blk, blk, blk],
        out_specs=[blk, slab, slab, blk], out_shape=[sd, sd, sd, sd],
        scratch_shapes=[pltpu.VMEM((R, 128), F32), pltpu.VMEM((R, 128), F32)], name="sb_bwd",
        compiler_params=_cp(("arbitrary", "arbitrary")))(trips, q, k, v, car, dm, g, o)


def _adamw(w, parts, m, v, name):
    rows, cols = w.shape
    tr = 256 if rows % 256 == 0 else rows
    nparts = len(parts)

    def body(*refs):
        w_ref = refs[0]
        p_refs = refs[1:1 + nparts]
        m_ref, v_ref, g_ref, d_ref, nm_ref, nv_ref = refs[1 + nparts:]
        g = p_refs[0][...]
        for p_ref in p_refs[1:]:
            g = g + p_ref[...]
        nm = ADAM_B1 * m_ref[...] + (1.0 - ADAM_B1) * g
        nv = ADAM_B2 * v_ref[...] + (1.0 - ADAM_B2) * (g * g)
        m_hat = nm / (1.0 - ADAM_B1 ** ADAM_STEP)
        v_hat = nv / (1.0 - ADAM_B2 ** ADAM_STEP)
        g_ref[...] = g
        d_ref[...] = -ADAM_LR * (m_hat / (jnp.sqrt(v_hat) + ADAM_EPS) + ADAM_WD * w_ref[...])
        nm_ref[...] = nm
        nv_ref[...] = nv

    blk = pl.BlockSpec((tr, cols), lambda i: (i, 0))
    sd = jax.ShapeDtypeStruct((rows, cols), F32)
    return pl.pallas_call(
        body, grid=(rows // tr,), in_specs=[blk] * (3 + nparts), out_specs=[blk] * 4, out_shape=[sd] * 4,
        name=name, compiler_params=_cp(("parallel",)))(w, *parts, m, v)


def _sum8(buf, name):
    _, rows, cols = buf.shape

    def body(b_ref, o_ref):
        acc = b_ref[0]
        for i in range(1, 8):
            acc = acc + b_ref[i]
        o_ref[...] = acc

    return pl.pallas_call(
        body, out_shape=jax.ShapeDtypeStruct((rows, cols), F32), name=name,
        compiler_params=pltpu.CompilerParams(vmem_limit_bytes=VMEM_LIMIT))(buf)


MESH = pl.DeviceIdType.MESH
ANY = pl.BlockSpec(memory_space=pl.ANY)


def _chip_peers():
    x, y = lax.axis_index("x"), lax.axis_index("y")
    return [(1 - x, y), (x, 1 - y), (1 - x, 1 - y)]


def _gather_chips(shards):
    n = len(shards)
    shards = [s.reshape((2, s.shape[0] // 2) + s.shape[1:]) for s in shards]

    def body(*refs):
        ins, outs = refs[:n], refs[n:2 * n]
        s1, r1, s2, r2 = refs[2 * n:]
        x, y, c = lax.axis_index("x"), lax.axis_index("y"), lax.axis_index("c")
        me = 2 * x + y
        peers = _chip_peers()

        def half(ref, a, cc):
            return ref.at[cc]

        first = []
        for j, (px, py) in enumerate(peers):
            for a in range(n):
                first.append(pltpu.make_async_remote_copy(
                    src_ref=half(ins[a], a, c), dst_ref=half(outs[a].at[me], a, c), send_sem=s1.at[j * n + a],
                    recv_sem=r1.at[j * n + a], device_id=(px, py, c), device_id_type=MESH))
        for cp in first:
            cp.start()
        passed = []
        for j, (px, py) in enumerate(peers):
            for a in range(n):
                got = half(outs[a].at[2 * px + py], a, c)
                pltpu.make_async_remote_copy(
                    src_ref=got, dst_ref=got, send_sem=s1.at[j * n + a], recv_sem=r1.at[j * n + a],
                    device_id=(px, py, c), device_id_type=MESH).wait_recv()
                cp = pltpu.make_async_remote_copy(
                    src_ref=got, dst_ref=got, send_sem=s2.at[j * n + a], recv_sem=r2.at[j * n + a],
                    device_id=(x, y, 1 - c), device_id_type=MESH)
                cp.start()
                passed.append(cp)
        for j, (px, py) in enumerate(peers):
            for a in range(n):
                theirs = half(outs[a].at[2 * px + py], a, 1 - c)
                pltpu.make_async_remote_copy(
                    src_ref=theirs, dst_ref=theirs, send_sem=s2.at[j * n + a], recv_sem=r2.at[j * n + a],
                    device_id=(x, y, 1 - c), device_id_type=MESH).wait_recv()
        for cp in first + passed:
            cp.wait_send()

    res = pl.pallas_call(
        body, in_specs=[ANY] * n, out_specs=[ANY] * n,
        out_shape=[jax.ShapeDtypeStruct((N_CHIPS,) + s.shape, s.dtype) for s in shards],
        scratch_shapes=[pltpu.SemaphoreType.DMA((3 * n,))] * 4, name="gather_chips")(*shards)
    me = 2 * lax.axis_index("x") + lax.axis_index("y")
    res = [lax.dynamic_update_index_in_dim(r, s, me, 0) for r, s in zip(res, shards)]
    return [r.reshape((N_CHIPS, 2 * r.shape[2]) + r.shape[3:]) for r in res]


def _pair_exchange(grads):
    n = len(grads)
    hs = [g.shape[1] // 2 for g in grads]
    grads = [g.reshape((N_CHIPS, 2, h) + g.shape[2:]) for g, h in zip(grads, hs)]

    def body(*refs):
        ins, got = refs[:n], refs[n:2 * n]
        ssem, rsem = refs[2 * n:]
        x, y, c = lax.axis_index("x"), lax.axis_index("y"), lax.axis_index("c")
        sends = [pltpu.make_async_remote_copy(
            src_ref=ins[a].at[:, 1 - c], dst_ref=got[a], send_sem=ssem.at[a],
            recv_sem=rsem.at[a], device_id=(x, y, 1 - c), device_id_type=MESH) for a in range(n)]
        for cp in sends:
            cp.start()
        for cp in sends:
            cp.wait()

    half_shapes = [jax.ShapeDtypeStruct((N_CHIPS, h) + g.shape[3:], g.dtype) for g, h in zip(grads, hs)]
    got = pl.pallas_call(
        body, in_specs=[ANY] * n, out_specs=[ANY] * n, out_shape=half_shapes,
        scratch_shapes=[pltpu.SemaphoreType.DMA((n,))] * 2, name="pair_exchange")(*grads)
    c = lax.axis_index("c")
    own = [lax.dynamic_index_in_dim(g, c, 1, keepdims=False) for g in grads]
    return own, got


def _sum_pair(own, got, send_dtype, name):
    _, rows, cols = own.shape
    tr = 256 if rows % 256 == 0 else rows

    def body(a_ref, b_ref, f_ref, s_ref):
        t = a_ref[...].astype(F32) + b_ref[...].astype(F32)
        f_ref[...] = t
        s_ref[...] = t.astype(send_dtype)

    blk = pl.BlockSpec((N_CHIPS, tr, cols), lambda i: (0, i, 0))
    return pl.pallas_call(
        body, grid=(rows // tr,), in_specs=[blk, blk], out_specs=[blk, blk],
        out_shape=[jax.ShapeDtypeStruct(own.shape, F32), jax.ShapeDtypeStruct(own.shape, send_dtype)],
        name=name, compiler_params=_cp(("parallel",)))(own, got)


def _scatter_chips(keep, send):
    n = len(send)

    def body(*refs):
        sin, land = refs[:n], refs[n:2 * n]
        ssem, rsem = refs[2 * n:]
        c = lax.axis_index("c")
        sends = []
        for j, (px, py) in enumerate(_chip_peers()):
            for a in range(n):
                sends.append(pltpu.make_async_remote_copy(
                    src_ref=sin[a].at[2 * px + py], dst_ref=land[a].at[j], send_sem=ssem.at[j * n + a],
                    recv_sem=rsem.at[j * n + a], device_id=(px, py, c), device_id_type=MESH))
        for cp in sends:
            cp.start()
        for cp in sends:
            cp.wait()

    land = pl.pallas_call(
        body, in_specs=[ANY] * n, out_specs=[ANY] * n,
        out_shape=[jax.ShapeDtypeStruct((3,) + s.shape[1:], s.dtype) for s in send],
        scratch_shapes=[pltpu.SemaphoreType.DMA((3 * n,))] * 2, name="scatter_chips")(*send)
    me = 2 * lax.axis_index("x") + lax.axis_index("y")
    mine = [lax.dynamic_index_in_dim(k, me, 0, keepdims=False) for k in keep]
    return mine, land


def _sum_shard(mine, land, name):
    rows, cols = mine.shape
    tr = 256 if rows % 256 == 0 else rows

    def body(m_ref, l_ref, o_ref):
        o_ref[...] = ((m_ref[...] + l_ref[0].astype(F32)) + l_ref[1].astype(F32)) + l_ref[2].astype(F32)

    return pl.pallas_call(
        body, grid=(rows // tr,),
        in_specs=[pl.BlockSpec((tr, cols), lambda i: (i, 0)), pl.BlockSpec((3, tr, cols), lambda i: (0, i, 0))],
        out_specs=pl.BlockSpec((tr, cols), lambda i: (i, 0)), out_shape=jax.ShapeDtypeStruct((rows, cols), F32),
        name=name, compiler_params=_cp(("parallel",)))(mine, land)


def _join_cores(halves):
    n = len(halves)

    def body(*refs):
        ins, outs = refs[:n], refs[n:2 * n]
        ssem, rsem = refs[2 * n:]
        x, y, c = lax.axis_index("x"), lax.axis_index("y"), lax.axis_index("c")
        sends = [pltpu.make_async_remote_copy(
            src_ref=ins[a], dst_ref=outs[a].at[c], send_sem=ssem.at[a], recv_sem=rsem.at[a],
            device_id=(x, y, 1 - c), device_id_type=MESH) for a in range(n)]
        for cp in sends:
            cp.start()
        for a in range(n):
            sends[a].wait_send()
            pltpu.make_async_remote_copy(
                src_ref=ins[a], dst_ref=outs[a].at[1 - c], send_sem=ssem.at[a], recv_sem=rsem.at[a],
                device_id=(x, y, 1 - c), device_id_type=MESH).wait_recv()

    res = pl.pallas_call(
        body, in_specs=[ANY] * n, out_specs=[ANY] * n,
        out_shape=[jax.ShapeDtypeStruct((2,) + h.shape, h.dtype) for h in halves],
        scratch_shapes=[pltpu.SemaphoreType.DMA((n,))] * 2, name="join_cores")(*halves)
    c = lax.axis_index("c")
    res = [lax.dynamic_update_index_in_dim(r, h, c, 0) for r, h in zip(res, halves)]
    return [r.reshape((2 * r.shape[1],) + r.shape[2:]) for r in res]


def _gather_all(vec):
    def body(v_ref, o_ref, lsem, ssem, rsem):
        x, y, c = lax.axis_index("x"), lax.axis_index("y"), lax.axis_index("c")
        me = 4 * x + 2 * y + c
        local = pltpu.make_async_copy(v_ref, o_ref.at[me], lsem)
        local.start()
        cps = []
        for k in range(1, 8):
            px, py, pc = x ^ (k >> 2), y ^ ((k >> 1) & 1), c ^ (k & 1)
            cps.append(pltpu.make_async_remote_copy(
                src_ref=v_ref, dst_ref=o_ref.at[me], send_sem=ssem.at[k - 1], recv_sem=rsem.at[k - 1],
                device_id=(px, py, pc), device_id_type=MESH))
        for cp in cps:
            cp.start()
        for k in range(1, 8):
            px, py, pc = x ^ (k >> 2), y ^ ((k >> 1) & 1), c ^ (k & 1)
            pltpu.make_async_remote_copy(
                src_ref=v_ref, dst_ref=o_ref.at[4 * px + 2 * py + pc], send_sem=ssem.at[k - 1],
                recv_sem=rsem.at[k - 1], device_id=(px, py, pc), device_id_type=MESH).wait_recv()
        for cp in cps:
            cp.wait_send()
        local.wait()

    return pl.pallas_call(
        body, in_specs=[ANY], out_specs=ANY, out_shape=jax.ShapeDtypeStruct((8,) + vec.shape, vec.dtype),
        scratch_shapes=[pltpu.SemaphoreType.DMA, pltpu.SemaphoreType.DMA((7,)), pltpu.SemaphoreType.DMA((7,))],
        name="gather_all")(vec)


def _rope_tables():
    pos = (jnp.arange(R, dtype=jnp.int32) - PAD).astype(F32)
    half = HD // 2
    inv = ROPE_THETA ** (-jnp.arange(half, dtype=F32) / half)
    ang = pos[:, None] * inv[None, :]
    cos, sin = jnp.cos(ang), jnp.sin(ang)
    cs = jnp.tile(cos, (1, 4))
    sn = jnp.tile(jnp.concatenate([-sin, sin], axis=1), (1, 2))
    return cs, sn


def _perm_cols(w):
    return jnp.concatenate([w[:, 0:512], w[:, 768:1280], w[:, 1280:2304], w[:, 2304:2816], w[:, 512:640],
                            w[:, 640:768]], axis=1)


def _unperm_cols(w):
    return jnp.concatenate([w[:, C_Q:C_Q + 512], w[:, C_K:C_K + 128], w[:, C_V:C_V + 128], w[:, C_GA:C_GA + 512],
                            w[:, C_GLU:C_GLU + 1024], w[:, C_GB:C_GB + 512]], axis=1)


def _local_step(x, target, p):
    w0 = _perm_cols(p["ab_w_in"])
    wo0, wpw, wo1 = p["ab_w_out"], p["ab_w_pw2"], p["sb_w_out"]
    w1 = p["sb_w_in"]
    conv_w = jnp.concatenate([p["ab_conv_w"], jnp.zeros((1, CC), F32)], axis=0)
    cs, sn = _rope_tables()

    h0 = jnp.concatenate([jnp.zeros((PAD, D), F32), p["meta_tokens"], x], axis=0)

    xn0 = _rms_fwd(h0, p["ab_pre_norm"], "rms_fwd0")
    z0 = _mm([(xn0, w0)], F32, "in_proj0", 544, 1408)
    o0, a0, lse0 = _swa_fwd(z0, cs, sn, p["ab_sinks"])
    cv0, s0 = _conv_fwd(z0, conv_w, p["ab_conv_b"], p["ab_conv_ln_g"], p["ab_conv_ln_b"])
    t0 = _mm([(s0, wpw)], F32, "pw2", 544, 512)
    c0 = _gate_fwd(t0, z0, C_GB, "gate_b_fwd")
    wo0h = wo0.reshape(2, CC, D)
    y0 = _mm([(a0, (wo0h, 0)), (c0, (wo0h, 1))], F32, "out_proj0", 544, 1024)

    h1, xn1 = _post_rms_fwd(h0, y0, p["ab_post_norm"], p["sb_pre_norm"], "post_rms_fwd")
    q1 =_mm([(xn1, (w1, 0))], BF16, "in_proj1_q", 544, 1024)
    k1 = _mm([(xn1, (w1, 1))], BF16, "in_proj1_k", 544, 1024)
    v1 = _mm([(xn1, (w1, 2))], BF16, "in_proj1_v", 544, 1024)
    g1 = _mm([(xn1, (w1, 3))], F32, "in_proj1_g", 544, 1024)
    o1, m1, car1, trips1 = _sb_fwd(q1, k1, v1, g1)
    y1 = _mm([(m1, wo1)], F32, "out_proj1", 544, 1024)

    dh2, dy1, d_sb_post, loss_row = _tail(h1, y1, p["sb_post_norm"], target)

    dm1 = _mm([(dy1, wo1)], F32, "out_proj1_dx", 544, 1024, tb=True)
    d_wo1 = _mm([(m1, dy1)], BF16, "out_proj1_dw", 512, 1024, ta=True)
    dq1, dk1, dv1, dg1 = _sb_bwd(trips1, q1, k1, v1, car1, dm1, g1, o1)
    dz1 = [dq1, dk1, dv1, dg1]
    dxn1 = _mm([(dz1[j], (w1, j)) for j in range(4)], F32, "in_proj1_dx", 544, 1024, tb=True)
    d_w1 = jnp.stack([_mm([(xn1, dz1[j])], BF16, "in_proj1_dw%d" % j, 512, 1024, ta=True) for j in range(4)])

    dh1, d_sb_pre, dy0, d_ab_post = _rms_post_bwd(dxn1, h1, p["sb_pre_norm"], dh2, y0, p["ab_post_norm"],
                                                  "rms_post_bwd")
    dmix0 = _mm([(dy0, wo0)], F32, "out_proj0_dx", 544, 1024, tb=True)
    d_wo0 = jnp.concatenate([_mm([(a0, dy0)], BF16, "out_proj0_dw_a", 512, 1024, ta=True),
                             _mm([(c0, dy0)], BF16, "out_proj0_dw_b", 512, 1024, ta=True)], axis=0)
    dt0, dgb0 = _gate_bwd(dmix0, 512, t0, z0, C_GB, "gate_b_bwd")
    ds0 = _mm([(dt0, wpw)], F32, "pw2_dx", 544, 512, tb=True)
    d_wpw = _mm([(s0, dt0)], BF16, "pw2_dw", 512, 512, ta=True)
    dglu0, d_convw, d_small = _conv_bwd(ds0, cv0, z0, conv_w, p["ab_conv_ln_g"], p["ab_conv_ln_b"])
    dq0, dga0, dkv0, d_sinks = _swa_bwd(z0, cs, sn, p["ab_sinks"], o0, dmix0, lse0)
    dz0 = jnp.concatenate([dq0, dga0, dglu0, dgb0, dkv0], axis=1)
    dxn0 = _mm([(dz0, w0)], F32, "in_proj0_dx", 544, 1024, tb=True)
    d_w0 = _unperm_cols(_mm([(xn0, dz0)], BF16, "in_proj0_dw", 512, 1408, ta=True))
    dh0_first, grad_x, d_ab_pre = _rms_bwd(dxn0, h0, p["ab_pre_norm"], dh1, F32, "rms_bwd0", split=True)

    grads = {
        "meta_tokens": dh0_first[PAD:TB], "ab_pre_norm": d_ab_pre, "ab_w_in": d_w0, "ab_sinks": d_sinks[0:1, 0:8],
        "ab_conv_w": d_convw[0:CONV_W], "ab_conv_b": d_small[0:1], "ab_conv_ln_g": d_small[1:2],
        "ab_conv_ln_b": d_small[2:3], "ab_w_pw2": d_wpw, "ab_w_out": d_wo0, "ab_post_norm": d_ab_post,
        "sb_pre_norm": d_sb_pre, "sb_w_in": d_w1, "sb_w_out": d_wo1, "sb_post_norm": d_sb_post,
    }
    return loss_row, grad_x, grads


SMALL_ROWS = 80
REP_ROWS = 32

WEIGHTS = ["meta_tokens", "ab_pre_norm", "ab_w_in", "ab_sinks", "ab_conv_w", "ab_conv_b", "ab_conv_ln_g",
           "ab_conv_ln_b", "ab_w_pw2", "ab_w_out", "ab_post_norm", "sb_pre_norm", "sb_w_in", "sb_w_out",
           "sb_post_norm"]
BIG = ["ab_w_in", "ab_w_out", "ab_w_pw2", "sb_w_in", "sb_w_out"]


def _pack_small(conv_w, meta, sb_pre, sb_post):
    rows = jnp.concatenate([conv_w, meta.reshape(32, 128), sb_pre.reshape(2, 128), sb_post.reshape(2, 128)], axis=0)
    return jnp.concatenate([rows, jnp.zeros((SMALL_ROWS - rows.shape[0], 128), F32)], axis=0)


def _unpack_small(s):
    return s[0:31], s[31:63].reshape(16, 256), s[63:65].reshape(1, 256), s[65:67].reshape(1, 256)


REP_LOSS = 3592


def _pack_rep(pre, post, conv_b, ln_g, ln_b, sinks, extra=None):
    flat = jnp.concatenate([pre.reshape(-1), post.reshape(-1), conv_b.reshape(-1), ln_g.reshape(-1),
                            ln_b.reshape(-1), sinks.reshape(-1)] + ([] if extra is None else [extra.reshape(-1)]))
    flat = jnp.concatenate([flat, jnp.zeros((REP_ROWS * 128 - flat.shape[0],), F32)])
    return flat.reshape(REP_ROWS, 128)


def _unpack_rep(r):
    f = r.reshape(-1)
    return (f[0:1024].reshape(1, 1024), f[1024:2048].reshape(1, 1024), f[2048:2560].reshape(1, 512),
            f[2560:3072].reshape(1, 512), f[3072:3584].reshape(1, 512), f[3584:3592].reshape(1, 8))


def _cols_to_chips(w, width):
    return w.reshape(w.shape[0], N_CHIPS, width).transpose(1, 0, 2)


def _chips_to_cols(w):
    return w.transpose(1, 0, 2).reshape(w.shape[1], -1)


def kernel(x, meta_tokens, ab_pre_norm, ab_w_in, ab_sinks, ab_conv_w, ab_conv_b, ab_conv_ln_g, ab_conv_ln_b, ab_w_pw2, ab_w_out, ab_post_norm, sb_pre_norm, sb_w_in, sb_w_out, sb_post_norm, loss_target, m_meta_tokens, m_ab_pre_norm, m_ab_w_in, m_ab_sinks, m_ab_conv_w, m_ab_conv_b, m_ab_conv_ln_g, m_ab_conv_ln_b, m_ab_w_pw2, m_ab_w_out, m_ab_post_norm, m_sb_pre_norm, m_sb_w_in, m_sb_w_out, m_sb_post_norm, v_meta_tokens, v_ab_pre_norm, v_ab_w_in, v_ab_sinks, v_ab_conv_w, v_ab_conv_b, v_ab_conv_ln_g, v_ab_conv_ln_b, v_ab_w_pw2, v_ab_w_out, v_ab_post_norm, v_sb_pre_norm, v_sb_w_in, v_sb_w_out, v_sb_post_norm):
    w = dict(meta_tokens=meta_tokens, ab_pre_norm=ab_pre_norm, ab_w_in=ab_w_in, ab_sinks=ab_sinks,
             ab_conv_w=ab_conv_w, ab_conv_b=ab_conv_b, ab_conv_ln_g=ab_conv_ln_g, ab_conv_ln_b=ab_conv_ln_b,
             ab_w_pw2=ab_w_pw2, ab_w_out=ab_w_out, ab_post_norm=ab_post_norm, sb_pre_norm=sb_pre_norm,
             sb_w_in=sb_w_in, sb_w_out=sb_w_out, sb_post_norm=sb_post_norm)
    m = dict(meta_tokens=m_meta_tokens, ab_pre_norm=m_ab_pre_norm, ab_w_in=m_ab_w_in, ab_sinks=m_ab_sinks,
             ab_conv_w=m_ab_conv_w, ab_conv_b=m_ab_conv_b, ab_conv_ln_g=m_ab_conv_ln_g,
             ab_conv_ln_b=m_ab_conv_ln_b, ab_w_pw2=m_ab_w_pw2, ab_w_out=m_ab_w_out, ab_post_norm=m_ab_post_norm,
             sb_pre_norm=m_sb_pre_norm, sb_w_in=m_sb_w_in, sb_w_out=m_sb_w_out, sb_post_norm=m_sb_post_norm)
    v = dict(meta_tokens=v_meta_tokens, ab_pre_norm=v_ab_pre_norm, ab_w_in=v_ab_w_in, ab_sinks=v_ab_sinks,
             ab_conv_w=v_ab_conv_w, ab_conv_b=v_ab_conv_b, ab_conv_ln_g=v_ab_conv_ln_g,
             ab_conv_ln_b=v_ab_conv_ln_b, ab_w_pw2=v_ab_w_pw2, ab_w_out=v_ab_w_out, ab_post_norm=v_ab_post_norm,
             sb_pre_norm=v_sb_pre_norm, sb_w_in=v_sb_w_in, sb_w_out=v_sb_w_out, sb_post_norm=v_sb_post_norm)

    def small_of(d):
        return _pack_small(d["ab_conv_w"][0], d["meta_tokens"], d["sb_pre_norm"], d["sb_post_norm"])

    def rep_of(d):
        return _pack_rep(d["ab_pre_norm"], d["ab_post_norm"], d["ab_conv_b"], d["ab_conv_ln_g"], d["ab_conv_ln_b"],
                         d["ab_sinks"])

    gathered = _gather_chips([w[k][0].astype(BF16) for k in BIG] + [small_of(w)])
    g_in0, g_out0, g_pw2, g_in1, g_out1, g_small = gathered
    conv_w_f = _chips_to_cols(g_small[:, 0:31])
    meta_f = _chips_to_cols(g_small[:, 31:63].reshape(N_CHIPS, 16, 256))
    sb_pre_f = g_small[:, 63:65].reshape(1, D)
    sb_post_f = g_small[:, 65:67].reshape(1, D)
    full = {
        "meta_tokens": meta_f, "ab_pre_norm": ab_pre_norm, "ab_w_in": _chips_to_cols(g_in0),
        "ab_sinks": ab_sinks, "ab_conv_w": conv_w_f, "ab_conv_b": ab_conv_b, "ab_conv_ln_g": ab_conv_ln_g,
        "ab_conv_ln_b": ab_conv_ln_b, "ab_w_pw2": g_pw2.reshape(CC, CC), "ab_w_out": g_out0.reshape(D, D),
        "ab_post_norm": ab_post_norm, "sb_pre_norm": sb_pre_f, "sb_w_in": g_in1, "sb_w_out": g_out1.reshape(D, D),
        "sb_post_norm": sb_post_f,
    }

    loss_row, grad_x, g = _local_step(x[0], loss_target[0], full)

    send = [_cols_to_chips(g["ab_w_in"], 704), g["ab_w_out"].reshape(N_CHIPS, 256, D),
            g["ab_w_pw2"].reshape(N_CHIPS, 128, CC), g["sb_w_in"], g["sb_w_out"].reshape(N_CHIPS, 256, D)]
    gs_conv = _cols_to_chips(g["ab_conv_w"], 128)
    gs_meta = _cols_to_chips(g["meta_tokens"], 256)
    gs_pre = g["sb_pre_norm"].reshape(N_CHIPS, 1, 256)
    gs_post = g["sb_post_norm"].reshape(N_CHIPS, 1, 256)
    send.append(jnp.stack([_pack_small(gs_conv[j], gs_meta[j], gs_pre[j], gs_post[j]) for j in range(N_CHIPS)]))
    names = BIG + ["small"]
    own, got = _pair_exchange(send)
    pair = [_sum_pair(o, t, o.dtype, "sum_pair_" + nm) for o, t, nm in zip(own, got, names)]
    mine, land = _scatter_chips([pr[0] for pr in pair], [pr[1] for pr in pair])
    halves = [_sum_shard(mi, la, "sum_shard_" + nm) for mi, la, nm in zip(mine, land, names)]
    total = _join_cores(halves)

    rep_g = _pack_rep(g["ab_pre_norm"], g["ab_post_norm"], g["ab_conv_b"], g["ab_conv_ln_g"], g["ab_conv_ln_b"],
                      g["ab_sinks"], loss_row[0:1, 0:1])
    rep_sum = _sum8(_gather_all(rep_g), "sum8_rep")
    loss = rep_sum.reshape(-1)[REP_LOSS]

    out_g, out_d, out_m, out_v = {}, {}, {}, {}
    for i, k in enumerate(BIG):
        shp = w[k].shape
        res = _adamw(w[k][0], [total[i]], m[k][0], v[k][0], "adamw_" + k)
        out_g[k], out_d[k], out_m[k], out_v[k] = [r.reshape(shp) for r in res]
    res = _adamw(small_of(w), [total[5]], small_of(m), small_of(v), "adamw_small")
    for dst, r in zip((out_g, out_d, out_m, out_v), res):
        cw, mt, pre, post = _unpack_small(r)
        dst["ab_conv_w"], dst["meta_tokens"], dst["sb_pre_norm"], dst["sb_post_norm"] = cw[None], mt, pre, post
    res = _adamw(rep_of(w), [rep_sum], rep_of(m), rep_of(v), "adamw_rep")
    for dst, r in zip((out_g, out_d, out_m, out_v), res):
        (dst["ab_pre_norm"], dst["ab_post_norm"], dst["ab_conv_b"], dst["ab_conv_ln_g"], dst["ab_conv_ln_b"],
         dst["ab_sinks"]) = _unpack_rep(r)

    return (loss, grad_x[None], *[out_g[k] for k in WEIGHTS], *[out_d[k] for k in WEIGHTS],
            *[out_m[k] for k in WEIGHTS], *[out_v[k] for k in WEIGHTS])
```

```python
import functools

import jax
import jax.numpy as jnp
from jax import lax
from jax.experimental import pallas as pl
from jax.experimental.pallas import tpu as pltpu

F32 = jnp.float32
BF16 = jnp.bfloat16

D = 1024
SEQ = 2048
N_META = 16
TB = 128
PAD = TB - N_META
R = SEQ + TB
NB = R // TB
HD = 64
ROPE_THETA = 10000.0
NORM_EPS = 1e-6
LN_EPS = 1e-5
NEG = -1e30
SWA_HEADS = 8
CONV_W = 31
SCALE = HD ** -0.5
N_CHIPS = 4

C_Q, C_GA, C_GLU, C_GB, C_K, C_V = 0, 512, 1024, 2048, 2560, 2688
AB_IN = 2816

ADAM_LR, ADAM_B1, ADAM_B2, ADAM_EPS, ADAM_WD, ADAM_STEP = 0.001, 0.9, 0.999, 1e-08, 0.01, 10

VMEM_LIMIT = 56 * 1024 * 1024


def _cp(sem):
    return pltpu.CompilerParams(dimension_semantics=sem, vmem_limit_bytes=VMEM_LIMIT)


def _sig(x):
    return 1.0 / (1.0 + jnp.exp(-x))


def _dot(a, b):
    return lax.dot_general(a, b, (((1,), (0,)), ((), ())), preferred_element_type=F32)


def _dot_nt(a, b):
    return lax.dot_general(a, b, (((1,), (1,)), ((), ())), preferred_element_type=F32)


def _dot_tn(a, b):
    return lax.dot_general(a, b, (((0,), (0,)), ((), ())), preferred_element_type=F32)


def _mm(pairs, out_dtype, name, tm, tn, ta=False, tb=False):
    pairs = [(a, b if isinstance(b, tuple) else (b, None)) for a, b in pairs]
    a0, (b0, _) = pairs[0]
    m = a0.shape[1] if ta else a0.shape[0]
    n = b0.shape[-2] if tb else b0.shape[-1]
    npairs = len(pairs)
    dims = (((0 if ta else 1,), (1 if tb else 0,)), ((), ()))

    def body(*refs):
        o_ref = refs[2 * npairs]
        acc = None
        for i in range(npairs):
            t = lax.dot_general(refs[2 * i][...].astype(BF16), refs[2 * i + 1][...].astype(BF16), dims,
                                preferred_element_type=F32)
            acc = t if acc is None else acc + t
        o_ref[...] = acc.astype(out_dtype)

    in_specs, args = [], []
    for a, (b, sel) in pairs:
        k = a.shape[0] if ta else a.shape[1]
        in_specs.append(pl.BlockSpec((k, tm), lambda i, j: (0, i)) if ta else pl.BlockSpec((tm, k), lambda i, j: (i, 0)))
        bshape, bidx = ((tn, k), lambda i, j: (j, 0)) if tb else ((k, tn), lambda i, j: (0, j))
        if sel is None:
            in_specs.append(pl.BlockSpec(bshape, bidx))
        else:
            in_specs.append(pl.BlockSpec((None,) + bshape, functools.partial(lambda i, j, f, s: (s,) + f(i, j), f=bidx, s=sel)))
        args += [a, b]
    return pl.pallas_call(
        body, grid=(m // tm, n // tn), in_specs=in_specs,
        out_specs=pl.BlockSpec((tm, tn), lambda i, j: (i, j)),
        out_shape=jax.ShapeDtypeStruct((m, n), out_dtype), name=name,
        compiler_params=_cp(("parallel", "parallel")))(*args)


def _rms_fwd(h, g, name):
    def body(h_ref, g_ref, o_ref):
        x = h_ref[...]
        r = lax.rsqrt(jnp.mean(x * x, axis=1, keepdims=True) + NORM_EPS)
        o_ref[...] = (x * r * g_ref[...]).astype(BF16)

    return pl.pallas_call(
        body, grid=(NB,),
        in_specs=[pl.BlockSpec((TB, D), lambda n: (n, 0)), pl.BlockSpec((1, D), lambda n: (0, 0))],
        out_specs=pl.BlockSpec((TB, D), lambda n: (n, 0)),
        out_shape=jax.ShapeDtypeStruct((R, D), BF16), name=name, compiler_params=_cp(("parallel",)))(h, g)


def _post_fwd(h, y, g, name):
    def body(h_ref, y_ref, g_ref, o_ref):
        yv = y_ref[...]
        r = lax.rsqrt(jnp.mean(yv * yv, axis=1, keepdims=True) + NORM_EPS)
        o_ref[...] = h_ref[...] + yv * r * g_ref[...]

    blk = pl.BlockSpec((TB, D), lambda n: (n, 0))
    return pl.pallas_call(
        body, grid=(NB,), in_specs=[blk, blk, pl.BlockSpec((1, D), lambda n: (0, 0))], out_specs=blk,
        out_shape=jax.ShapeDtypeStruct((R, D), F32), name=name, compiler_params=_cp(("parallel",)))(h, y, g)


def _rms_bwd(dout, x, g, res, out_dtype, name, split=False):
    has_res = res is not None

    def body(*refs):
        if split:
            refs = list(refs)
            dx_rest_ref = refs.pop(-2)
        if has_res:
            d_ref, x_ref, g_ref, r_ref, dx_ref, dg_ref = refs
        else:
            d_ref, x_ref, g_ref, dx_ref, dg_ref = refs
        n = pl.program_id(0)
        xv = x_ref[...]
        dv = d_ref[...]
        r = lax.rsqrt(jnp.mean(xv * xv, axis=1, keepdims=True) + NORM_EPS)
        xh = xv * r
        dxh = dv * g_ref[...]
        dx = r * (dxh - xh * jnp.mean(dxh * xh, axis=1, keepdims=True))
        if has_res:
            dx = dx + r_ref[...]
        row = lax.broadcasted_iota(jnp.int32, (TB, D), 0) + n * TB
        dx = jnp.where(row >= PAD, dx, 0.0).astype(out_dtype)
        if split:
            @pl.when(n == 0)
            def _():
                dx_ref[...] = dx

            @pl.when(n > 0)
            def _():
                dx_rest_ref[...] = dx
        else:
            dx_ref[...] = dx

        @pl.when(n == 0)
        def _():
            dg_ref[...] = jnp.zeros_like(dg_ref)

        dg_ref[...] += jnp.sum(dv * xh, axis=0, keepdims=True)

    blk = pl.BlockSpec((TB, D), lambda n: (n, 0))
    vec = pl.BlockSpec((1, D), lambda n: (0, 0))
    ins = [dout, x, g] + ([res] if has_res else [])
    in_specs = [blk, blk, vec] + ([blk] if has_res else [])
    if split:
        out_specs = [pl.BlockSpec((TB, D), lambda n: (0, 0)), pl.BlockSpec((TB, D), lambda n: (jnp.maximum(n - 1, 0), 0)), vec]
        out_shape = [jax.ShapeDtypeStruct((TB, D), out_dtype), jax.ShapeDtypeStruct((SEQ, D), out_dtype),
                     jax.ShapeDtypeStruct((1, D), F32)]
    else:
        out_specs = [blk, vec]
        out_shape = [jax.ShapeDtypeStruct((R, D), out_dtype), jax.ShapeDtypeStruct((1, D), F32)]
    return pl.pallas_call(
        body, grid=(NB,), in_specs=in_specs, out_specs=out_specs, out_shape=out_shape,
        name=name, compiler_params=_cp(("arbitrary",)))(*ins)


def _post_rms_fwd(h, y, g_post, g_next, name):
    def body(h_ref, y_ref, gp_ref, gn_ref, o_ref, x_ref):
        yv = y_ref[...]
        r = lax.rsqrt(jnp.mean(yv * yv, axis=1, keepdims=True) + NORM_EPS)
        hn = h_ref[...] + yv * r * gp_ref[...]
        o_ref[...] = hn
        r2 = lax.rsqrt(jnp.mean(hn * hn, axis=1, keepdims=True) + NORM_EPS)
        x_ref[...] = (hn * r2 * gn_ref[...]).astype(BF16)

    blk = pl.BlockSpec((TB, D), lambda n: (n, 0))
    vec = pl.BlockSpec((1, D), lambda n: (0, 0))
    return pl.pallas_call(
        body, grid=(NB,), in_specs=[blk, blk, vec, vec], out_specs=[blk, blk],
        out_shape=[jax.ShapeDtypeStruct((R, D), F32), jax.ShapeDtypeStruct((R, D), BF16)],
        name=name, compiler_params=_cp(("parallel",)))(h, y, g_post, g_next)


def _rms_post_bwd(dxn, h, g, res, y, g_post, name):
    def body(d_ref, h_ref, g_ref, r_ref, y_ref, gp_ref, dh_ref, dg_ref, dy_ref, dgp_ref):
        n = pl.program_id(0)

        @pl.when(n == 0)
        def _():
            dg_ref[...] = jnp.zeros_like(dg_ref)
            dgp_ref[...] = jnp.zeros_like(dgp_ref)

        hv, dv = h_ref[...], d_ref[...]
        r = lax.rsqrt(jnp.mean(hv * hv, axis=1, keepdims=True) + NORM_EPS)
        xh = hv * r
        dxh = dv * g_ref[...]
        dh = r * (dxh - xh * jnp.mean(dxh * xh, axis=1, keepdims=True)) + r_ref[...]
        row = lax.broadcasted_iota(jnp.int32, (TB, D), 0) + n * TB
        dh = jnp.where(row >= PAD, dh, 0.0)
        dh_ref[...] = dh
        dg_ref[...] += jnp.sum(dv * xh, axis=0, keepdims=True)
        yv = y_ref[...]
        ry = lax.rsqrt(jnp.mean(yv * yv, axis=1, keepdims=True) + NORM_EPS)
        yh = yv * ry
        dyh = dh * gp_ref[...]
        dy_ref[...] = (ry * (dyh - yh * jnp.mean(dyh * yh, axis=1, keepdims=True))).astype(BF16)
        dgp_ref[...] += jnp.sum(dh * yh, axis=0, keepdims=True)

    blk = pl.BlockSpec((TB, D), lambda n: (n, 0))
    vec = pl.BlockSpec((1, D), lambda n: (0, 0))
    return pl.pallas_call(
        body, grid=(NB,), in_specs=[blk, blk, vec, blk, blk, vec], out_specs=[blk, vec, blk, vec],
        out_shape=[jax.ShapeDtypeStruct((R, D), F32), jax.ShapeDtypeStruct((1, D), F32),
                   jax.ShapeDtypeStruct((R, D), BF16), jax.ShapeDtypeStruct((1, D), F32)],
        name=name, compiler_params=_cp(("arbitrary",)))(dxn, h, g, res, y, g_post)


GW = 512


def _gate_fwd(o, gsrc, goff, name):
    w = o.shape[1]

    def body(o_ref, g_ref, m_ref):
        gv = g_ref[...]
        m_ref[...] = (o_ref[...] * (gv * _sig(gv))).astype(BF16)

    gb = goff // GW
    return pl.pallas_call(
        body, grid=(NB, w // GW),
        in_specs=[pl.BlockSpec((TB, GW), lambda n, j: (n, j)), pl.BlockSpec((TB, GW), lambda n, j: (n, gb + j))],
        out_specs=pl.BlockSpec((TB, GW), lambda n, j: (n, j)),
        out_shape=jax.ShapeDtypeStruct((R, w), BF16), name=name,
        compiler_params=_cp(("parallel", "parallel")))(o, gsrc)


def _gate_bwd(dsrc, doff, o, gsrc, goff, name):
    w = o.shape[1]

    def body(d_ref, o_ref, g_ref, do_ref, dg_ref):
        gv = g_ref[...]
        dv = d_ref[...]
        s = _sig(gv)
        do_ref[...] = dv * (gv * s)
        dg_ref[...] = (dv * o_ref[...] * (s * (1.0 + gv * (1.0 - s)))).astype(BF16)

    db, gb = doff // GW, goff // GW
    blk = pl.BlockSpec((TB, GW), lambda n, j: (n, j))
    return pl.pallas_call(
        body, grid=(NB, w // GW),
        in_specs=[pl.BlockSpec((TB, GW), lambda n, j: (n, db + j)), blk,
                  pl.BlockSpec((TB, GW), lambda n, j: (n, gb + j))],
        out_specs=[blk, blk],
        out_shape=[jax.ShapeDtypeStruct((R, w), F32), jax.ShapeDtypeStruct((R, w), BF16)], name=name,
        compiler_params=_cp(("parallel", "parallel")))(dsrc, o, gsrc)


def _tail(h, y, g, target):
    def body(h_ref, y_ref, g_ref, t_ref, d_ref, dy_ref, dg_ref, l_ref):
        n = pl.program_id(0)

        @pl.when(n == 0)
        def _():
            d_ref[...] = jnp.zeros_like(d_ref)
            dy_ref[...] = jnp.zeros_like(dy_ref)
            dg_ref[...] = jnp.zeros_like(dg_ref)
            l_ref[...] = jnp.zeros_like(l_ref)

        @pl.when(n > 0)
        def _():
            yv = y_ref[...]
            r = lax.rsqrt(jnp.mean(yv * yv, axis=1, keepdims=True) + NORM_EPS)
            yh = yv * r
            err = (h_ref[...] + yh * g_ref[...]) - t_ref[...]
            dv = err * (1.0 / D)
            d_ref[...] = dv
            l_ref[...] += jnp.sum(err * err, axis=0, keepdims=True)
            dyh = dv * g_ref[...]
            dy_ref[...] = (r * (dyh - yh * jnp.mean(dyh * yh, axis=1, keepdims=True))).astype(BF16)
            dg_ref[...] += jnp.sum(dv * yh, axis=0, keepdims=True)

        @pl.when(n == NB - 1)
        def _():
            tot = jnp.sum(l_ref[...], axis=1, keepdims=True) * (0.5 / D)
            l_ref[...] = jnp.broadcast_to(tot, (1, D))

    blk = pl.BlockSpec((TB, D), lambda n: (n, 0))
    vec = pl.BlockSpec((1, D), lambda n: (0, 0))
    return pl.pallas_call(
        body, grid=(NB,),
        in_specs=[blk, blk, vec, pl.BlockSpec((TB, D), lambda n: (jnp.maximum(n - 1, 0), 0))],
        out_specs=[blk, blk, vec, vec],
        out_shape=[jax.ShapeDtypeStruct((R, D), F32), jax.ShapeDtypeStruct((R, D), BF16),
                   jax.ShapeDtypeStruct((1, D), F32), jax.ShapeDtypeStruct((1, D), F32)],
        name="tail", compiler_params=_cp(("arbitrary",)))(h, y, g, target)


def _lane_row(shape):
    return lax.broadcasted_iota(jnp.int32, shape, 1), lax.broadcasted_iota(jnp.int32, shape, 0)


def _rot_half(x, lane):
    return jnp.where(lane % HD < HD // 2, pltpu.roll(x, 128 - HD // 2, 1), pltpu.roll(x, HD // 2, 1))


def _swa_blocks(n):
    return (0, jnp.maximum(n - 1, 0), n)


SWA_STACKS = ((0, 0), (0, 1), (1, 0), (1, 1))


def _swa_masks(n, lane, row):
    qpos = n * TB + (row & (TB - 1))
    kp = (n - 1) * TB + lane
    kc = n * TB + lane
    m0 = (lane >= PAD) & (qpos - lane >= TB)
    mp = (kp >= PAD) & (qpos >= kp) & (qpos - kp < TB)
    mc = (kc >= PAD) & (qpos >= kc)
    return (m0, mp, mc)


def _stack_pair(xa, xb, par):
    lane = lax.broadcasted_iota(jnp.int32, (TB, 128), 1)
    keep = (lane < HD) if par == 0 else (lane >= HD)
    return jnp.concatenate([jnp.where(keep, xa, 0.0), jnp.where(keep, xb, 0.0)], axis=0)


def _per_head(a, b):
    row = lax.broadcasted_iota(jnp.int32, (2 * TB, 1), 0)
    return jnp.where(row < TB, a, b)


def _swa_load(n, zq_ref, zkv_ref, cs_ref, sn_ref, lane):
    r0 = pl.multiple_of(n * TB, TB)
    csq, snq = cs_ref[pl.ds(r0, TB), :], sn_ref[pl.ds(r0, TB), :]
    qc = []
    for c in range(4):
        x = zq_ref[:, c * 128:(c + 1) * 128]
        qc.append((x * csq + _rot_half(x, lane) * snq) * SCALE)
    qst = [_stack_pair(qc[2 * g], qc[2 * g + 1], par).astype(BF16) for g, par in SWA_STACKS]
    kvs = []
    for b in _swa_blocks(n):
        b0 = pl.multiple_of(b * TB, TB)
        csb, snb = cs_ref[pl.ds(b0, TB), :], sn_ref[pl.ds(b0, TB), :]
        kx = zkv_ref[pl.ds(b0, TB), 0:128]
        kr = kx * csb + _rot_half(kx, lane) * snb
        vx = zkv_ref[pl.ds(b0, TB), 128:256]
        kvs.append((kr.astype(BF16), pltpu.roll(kr, HD, 1).astype(BF16),
                    vx.astype(BF16), pltpu.roll(vx, HD, 1).astype(BF16), csb, snb, b0))
    return qst, (csq, snq), kvs


def _swa_fwd(z0, cs, sn, sinks):
    def body(zq_ref, zkv_ref, cs_ref, sn_ref, sk_ref, ga_ref, o_ref, a_ref, lse_ref):
        n = pl.program_id(0)
        lane, row = _lane_row((TB, 128))
        lo = lane < HD
        masks = _swa_masks(n, *_lane_row((2 * TB, 128)))
        qst, _, kvs = _swa_load(n, zq_ref, zkv_ref, cs_ref, sn_ref, lane)
        ss = [[jnp.where(m, _dot_nt(qst[si], k if par == g else ka), NEG)
               for (k, ka, _, _, _, _, _), m in zip(kvs, masks)] for si, (g, par) in enumerate(SWA_STACKS)]
        o2, lse2 = [], []
        for si, (g, par) in enumerate(SWA_STACKS):
            sink = _per_head(sk_ref[0, 4 * g + par], sk_ref[0, 4 * g + 2 + par])
            s = ss[si]
            mx = jnp.maximum(jnp.maximum(jnp.max(s[0], axis=1, keepdims=True), jnp.max(s[1], axis=1, keepdims=True)),
                             jnp.max(s[2], axis=1, keepdims=True))
            mx = jnp.maximum(mx, sink)
            es = [jnp.exp(sb - mx) for sb in s]
            den = (jnp.sum(es[0], axis=1, keepdims=True) + jnp.sum(es[1], axis=1, keepdims=True)
                   + jnp.sum(es[2], axis=1, keepdims=True) + jnp.exp(sink - mx))
            inv = 1.0 / den
            t = jnp.zeros((2 * TB, 128), F32)
            for (_, _, v, va, _, _, _), e in zip(kvs, es):
                t = t + _dot((e * inv).astype(BF16), v if par == g else va)
            o2.append(t)
            lse2.append(mx + jnp.log(den))
        lse_t = jnp.zeros((TB, 128), F32)
        for g in range(2):
            for t in range(2):
                rows = slice(t * TB, (t + 1) * TB)
                c = 2 * g + t
                oc = jnp.where(lo, o2[2 * g][rows], o2[2 * g + 1][rows])
                o_ref[:, c * 128:(c + 1) * 128] = oc
                gv = ga_ref[:, c * 128:(c + 1) * 128]
                a_ref[:, c * 128:(c + 1) * 128] = (oc * (gv * _sig(gv))).astype(BF16)
                for par in range(2):
                    lse_t = jnp.where(lane == 4 * g + 2 * t + par, lse2[2 * g + par][rows], lse_t)
        lse_ref[...] = lse_t

    full = pl.BlockSpec((R, 128), lambda n: (0, 0))
    return dict(
        body=body,
        in_specs=[pl.BlockSpec((TB, 512), lambda n: (n, C_Q // 512)),
                  pl.BlockSpec((R, 256), lambda n: (0, C_K // 256)), full, full,
                  pl.BlockSpec(memory_space=pltpu.SMEM), pl.BlockSpec((TB, 512), lambda n: (n, C_GA // 512))],
        args=[z0, z0, cs, sn, sinks, z0],
        out_specs=[pl.BlockSpec((TB, 512), lambda n: (n, 0)), pl.BlockSpec((TB, 512), lambda n: (n, 0)),
                   pl.BlockSpec((TB, 128), lambda n: (n, 0))],
        out_shape=[jax.ShapeDtypeStruct((R, 512), F32), jax.ShapeDtypeStruct((R, 512), BF16),
                   jax.ShapeDtypeStruct((R, 128), F32)],
        scratch=[])


def _swa_bwd(z0, cs, sn, sinks, o, dmix, lse):
    def body(zq_ref, zkv_ref, cs_ref, sn_ref, sk_ref, ga_ref, o_ref, dm_ref, lse_ref,
             dq_ref, dga_ref, dkv_ref, dsk_ref, do_ref, acc_ref):
        n = pl.program_id(0)

        @pl.when(n == 0)
        def _():
            acc_ref[...] = jnp.zeros_like(acc_ref)
            dsk_ref[...] = jnp.zeros_like(dsk_ref)

        gv, dmv = ga_ref[...], dm_ref[...]
        sg = _sig(gv)
        dga_ref[...] = (dmv * o_ref[...] * (sg * (1.0 + gv * (1.0 - sg)))).astype(BF16)
        do_ref[...] = dmv * (gv * sg)
        lane, row = _lane_row((TB, 128))
        lo = lane < HD
        masks = _swa_masks(n, *_lane_row((2 * TB, 128)))
        qst, (csq, snq), kvs = _swa_load(n, zq_ref, zkv_ref, cs_ref, sn_ref, lane)
        lse_t = lse_ref[...]
        ss = [[jnp.where(m, _dot_nt(qst[si], k if par == g else ka), NEG)
               for (k, ka, _, _, _, _, _), m in zip(kvs, masks)] for si, (g, par) in enumerate(SWA_STACKS)]
        dobs, deltas, lses, dps = [], [], [], []
        for g, par in SWA_STACKS:
            ca, cb = slice(2 * g * 128, (2 * g + 1) * 128), slice((2 * g + 1) * 128, (2 * g + 2) * 128)
            dom = _stack_pair(do_ref[:, ca], do_ref[:, cb], par)
            deltas.append(jnp.sum(dom * jnp.concatenate([o_ref[:, ca], o_ref[:, cb]], axis=0), axis=1, keepdims=True))
            dob = dom.astype(BF16)
            dobs.append(dob)
            lses.append(jnp.concatenate(
                [jnp.sum(jnp.where(lane == 4 * g + 2 * t + par, lse_t, 0.0), axis=1, keepdims=True) for t in range(2)],
                axis=0))
            dps.append([_dot_nt(dob, v if par == g else va) for (_, _, v, va, _, _, _) in kvs])
        dk_al = [jnp.zeros((TB, 128), F32) for _ in range(3)]
        dk_mis = [jnp.zeros((TB, 128), F32) for _ in range(3)]
        dv_al = [jnp.zeros((TB, 128), F32) for _ in range(3)]
        dv_mis = [jnp.zeros((TB, 128), F32) for _ in range(3)]
        dsk_t = jnp.zeros((TB, 128), F32)
        dq2 = []
        for si, (g, par) in enumerate(SWA_STACKS):
            dqt = jnp.zeros((2 * TB, 128), F32)
            for bi, (k, ka, _, _, _, _, _) in enumerate(kvs):
                p = jnp.exp(ss[si][bi] - lses[si])
                ds = (p * (dps[si][bi] - deltas[si])).astype(BF16)
                dqt = dqt + _dot(ds, k if par == g else ka)
                dkh = _dot_tn(ds, qst[si])
                dvh = _dot_tn(p.astype(BF16), dobs[si])
                if par == g:
                    dk_al[bi] = dk_al[bi] + dkh
                    dv_al[bi] = dv_al[bi] + dvh
                else:
                    dk_mis[bi] = dk_mis[bi] + dkh
                    dv_mis[bi] = dv_mis[bi] + dvh
            dq2.append(dqt)
            sink = _per_head(sk_ref[0, 4 * g + par], sk_ref[0, 4 * g + 2 + par])
            dsk = -jnp.exp(sink - lses[si]) * deltas[si]
            for t in range(2):
                dsk_t = jnp.where(lane == 4 * g + 2 * t + par, dsk[t * TB:(t + 1) * TB], dsk_t)
        for g in range(2):
            for t in range(2):
                rows = slice(t * TB, (t + 1) * TB)
                c = 2 * g + t
                dqc = jnp.where(lo, dq2[2 * g][rows], dq2[2 * g + 1][rows]) * SCALE
                dq_ref[:, c * 128:(c + 1) * 128] = (dqc * csq + _rot_half(dqc * snq, lane)).astype(BF16)
        for bi, (_, _, _, _, csb, snb, b0) in enumerate(kvs):
            dk = dk_al[bi] + pltpu.roll(dk_mis[bi], HD, 1)
            dv = dv_al[bi] + pltpu.roll(dv_mis[bi], HD, 1)
            acc_ref[pl.ds(b0, TB), 0:128] += dk * csb + _rot_half(dk * snb, lane)
            acc_ref[pl.ds(b0, TB), 128:256] += dv
        dsk_ref[0:1, :] += jnp.sum(dsk_t, axis=0, keepdims=True)

        @pl.when(n == NB - 1)
        def _():
            dkv_ref[...] = acc_ref[...].astype(BF16)

    full = pl.BlockSpec((R, 128), lambda n: (0, 0))
    b512 = pl.BlockSpec((TB, 512), lambda n: (n, 0))
    return dict(
        body=body,
        in_specs=[pl.BlockSpec((TB, 512), lambda n: (n, C_Q // 512)),
                  pl.BlockSpec((R, 256), lambda n: (0, C_K // 256)), full, full,
                  pl.BlockSpec(memory_space=pltpu.SMEM), pl.BlockSpec((TB, 512), lambda n: (n, C_GA // 512)),
                  b512, b512, pl.BlockSpec((TB, 128), lambda n: (n, 0))],
        args=[z0, z0, cs, sn, sinks, z0, o, dmix, lse],
        out_specs=[b512, b512, pl.BlockSpec((R, 256), lambda n: (0, 0)), pl.BlockSpec((8, 128), lambda n: (0, 0))],
        out_shape=[jax.ShapeDtypeStruct((R, 512), BF16), jax.ShapeDtypeStruct((R, 512), BF16),
                   jax.ShapeDtypeStruct((R, 256), BF16), jax.ShapeDtypeStruct((8, 128), F32)],
        scratch=[pltpu.VMEM((TB, 512), F32), pltpu.VMEM((R, 256), F32)])


CC = 512
HALO = CONV_W - 1


def _conv_fwd(z0, conv_w, conv_b, ln_g, ln_b):
    def body(g_ref, w_ref, cb_ref, lg_ref, lb_ref, cv_ref, s_ref, ubuf):
        n = pl.program_id(0)

        @pl.when(n == 0)
        def _():
            ubuf[0:TB, :] = jnp.zeros((TB, CC), F32)

        u = g_ref[:, 0:CC] * _sig(g_ref[:, CC:2 * CC])
        ubuf[TB:2 * TB, :] = u
        acc = jnp.zeros((TB, CC), F32)
        for w in range(CONV_W):
            acc = acc + ubuf[pl.ds(TB - HALO + w, TB), :] * w_ref[w:w + 1, :]
        cv = acc + cb_ref[...]
        cv_ref[...] = cv
        xc = cv - jnp.mean(cv, axis=1, keepdims=True)
        rs = lax.rsqrt(jnp.mean(xc * xc, axis=1, keepdims=True) + LN_EPS)
        ln = xc * rs * lg_ref[...] + lb_ref[...]
        s_ref[...] = (ln * _sig(ln)).astype(BF16)
        ubuf[0:TB, :] = u

    vec = pl.BlockSpec((1, CC), lambda n: (0, 0))
    blk = pl.BlockSpec((TB, CC), lambda n: (n, 0))
    return dict(
        body=body,
        in_specs=[pl.BlockSpec((TB, 2 * CC), lambda n: (n, C_GLU // (2 * CC))),
                  pl.BlockSpec((32, CC), lambda n: (0, 0)), vec, vec, vec],
        args=[z0, conv_w, conv_b, ln_g, ln_b],
        out_specs=[blk, blk],
        out_shape=[jax.ShapeDtypeStruct((R, CC), F32), jax.ShapeDtypeStruct((R, CC), BF16)],
        scratch=[pltpu.VMEM((2 * TB, CC), F32)])


def _conv_bwd(ds, cv, z0, conv_w, ln_g, ln_b):
    def body(ds_ref, cv_ref, g_ref, w_ref, lg_ref, lb_ref, dglu_ref, dw_ref, dsm_ref, dbuf):
        n = pl.program_id(0)

        @pl.when(n == 0)
        def _():
            dbuf[TB:2 * TB, :] = jnp.zeros((TB, CC), F32)
            dw_ref[...] = jnp.zeros_like(dw_ref)
            dsm_ref[...] = jnp.zeros_like(dsm_ref)

        cv = cv_ref[...]
        xc = cv - jnp.mean(cv, axis=1, keepdims=True)
        rs = lax.rsqrt(jnp.mean(xc * xc, axis=1, keepdims=True) + LN_EPS)
        xh = xc * rs
        ln = xh * lg_ref[...] + lb_ref[...]
        sg = _sig(ln)
        dln = ds_ref[...] * (sg * (1.0 + ln * (1.0 - sg)))
        dxh = dln * lg_ref[...]
        dcv = rs * (dxh - jnp.mean(dxh, axis=1, keepdims=True) - xh * jnp.mean(dxh * xh, axis=1, keepdims=True))
        dsm_ref[0:1, :] += jnp.sum(dcv, axis=0, keepdims=True)
        dsm_ref[1:2, :] += jnp.sum(dln * xh, axis=0, keepdims=True)
        dsm_ref[2:3, :] += jnp.sum(dln, axis=0, keepdims=True)
        dbuf[0:TB, :] = dcv
        a = g_ref[:, 0:CC]
        sb = _sig(g_ref[:, CC:2 * CC])
        u = a * sb
        du = jnp.zeros((TB, CC), F32)
        for w in range(CONV_W):
            sh = dbuf[pl.ds(HALO - w, TB), :]
            du = du + sh * w_ref[w:w + 1, :]
            dw_ref[w:w + 1, :] += jnp.sum(u * sh, axis=0, keepdims=True)
        dglu_ref[:, 0:CC] = (du * sb).astype(BF16)
        dglu_ref[:, CC:2 * CC] = (du * a * sb * (1.0 - sb)).astype(BF16)
        dbuf[TB:2 * TB, :] = dcv

    rev = lambda n: (NB - 1 - n, 0)
    vec = pl.BlockSpec((1, CC), lambda n: (0, 0))
    blk = pl.BlockSpec((TB, CC), rev)
    return dict(
        body=body,
        in_specs=[blk, blk, pl.BlockSpec((TB, 2 * CC), lambda n: (NB - 1 - n, C_GLU // (2 * CC))),
                  pl.BlockSpec((32, CC), lambda n: (0, 0)), vec, vec],
        args=[ds, cv, z0, conv_w, ln_g, ln_b],
        out_specs=[pl.BlockSpec((TB, 2 * CC), rev), pl.BlockSpec((32, CC), lambda n: (0, 0)),
                   pl.BlockSpec((8, CC), lambda n: (0, 0))],
        out_shape=[jax.ShapeDtypeStruct((R, 2 * CC), BF16), jax.ShapeDtypeStruct((32, CC), F32),
                   jax.ShapeDtypeStruct((8, CC), F32)],
        scratch=[pltpu.VMEM((2 * TB, CC), F32)])


def _split_dot(x, t):
    hi = x.astype(BF16)
    lo = (x - hi.astype(F32)).astype(BF16)
    return _dot(hi, t) + _dot(lo, t)


def _stack_heads(x):
    lane = lax.broadcasted_iota(jnp.int32, (TB, 128), 1)
    return jnp.concatenate([jnp.where(lane < HD, x, 0.0), jnp.where(lane < HD, 0.0, x)], axis=0).astype(BF16)


def _sb_stack(qv, i):
    lane2, row2 = _lane_row((2 * TB, 128))
    qpos2 = i * TB + (row2 & (TB - 1))
    lane, row = _lane_row((TB, 128))
    return _stack_heads(qv), lane2, qpos2, (row > lane).astype(BF16)


SB_U = 3
SB_DEAD = -104.0


def _sb_fwd(q, k, v, g):
    def body(q_ref, k_ref, v_ref, g_ref, o_ref, m_ref, c_ref, n_ref):
        p, i = pl.program_id(0), pl.program_id(1)
        lane, row = _lane_row((TB, 128))
        lo = lane < HD
        q2, lane2, qpos2, tri_gt = _sb_stack(q_ref[...].astype(F32) * SCALE, i)

        def cond(st):
            t, _, c2 = st
            return jnp.logical_and(i - SB_U * t >= 0, jnp.max(c2) > SB_DEAD)

        def step(st):
            t, acc, c2 = st
            jrs = [i - SB_U * t - u for u in range(SB_U)]
            j0s = [pl.multiple_of(jnp.maximum(jr, 0) * TB, TB) for jr in jrs]
            ks = [k_ref[pl.ds(j0, TB), :] for j0 in j0s]
            zs = [_dot_nt(q2, kj) for kj in ks]
            valids, lbs, l1s = [], [], []
            for jr, z in zip(jrs, zs):
                kpos = jr * TB + lane2
                valid = (kpos >= PAD) & (kpos < qpos2)
                lb = jnp.minimum(z, 0.0) - jnp.log(1.0 + jnp.exp(-jnp.abs(z)))
                valids.append(valid)
                lbs.append(lb)
                l1s.append(jnp.where(valid, lb - z, 0.0))
            sfxs = [_split_dot(l1, tri_gt) for l1 in l1s]
            carries = []
            for jr, l1 in zip(jrs, l1s):
                carries.append(c2)
                c_ref[...] = jnp.where(lane == 2 * jr, c2[0:TB], jnp.where(lane == 2 * jr + 1, c2[TB:2 * TB], c_ref[...]))
                c2 = c2 + jnp.sum(l1, axis=1, keepdims=True)
            for j0, valid, lb, sfx, cu in zip(j0s, valids, lbs, sfxs, carries):
                a = jnp.where(valid, jnp.exp(lb + sfx + cu), 0.0).astype(BF16)
                av = _dot(a, v_ref[pl.ds(j0, TB), :])
                acc = acc + jnp.where(lo, av[0:TB], av[TB:2 * TB])
            return t + 1, acc, c2

        c_ref[...] = jnp.zeros((TB, 128), F32)
        t, acc, _ = lax.while_loop(cond, step, (jnp.int32(0), jnp.zeros((TB, 128), F32), jnp.zeros((2 * TB, 1), F32)))
        o_ref[...] = acc
        gv = g_ref[...]
        m_ref[...] = (acc * (gv * _sig(gv))).astype(BF16)
        n_ref[p, i] = t

    slab = pl.BlockSpec((R, 128), lambda p, i: (0, p))
    blk = pl.BlockSpec((TB, 128), lambda p, i: (i, p))
    sd = jax.ShapeDtypeStruct((R, D), F32)
    return pl.pallas_call(
        body, grid=(D // 128, NB), in_specs=[blk, slab, slab, blk],
        out_specs=[blk, blk, blk, pl.BlockSpec(memory_space=pltpu.SMEM)],
        out_shape=[sd, jax.ShapeDtypeStruct((R, D), BF16), sd, jax.ShapeDtypeStruct((D // 128, NB), jnp.int32)],
        name="sb_fwd", compiler_params=_cp(("arbitrary", "arbitrary")))(q, k, v, g)


def _sb_bwd(trips, q, k, v, car, dm, g, o):
    def body(n_ref, q_ref, k_ref, v_ref, c_ref, dm_ref, g_ref, o_ref, dq_ref, dko_ref, dvo_ref, dg_ref,
             dk_ref, dv_ref):
        p, i = pl.program_id(0), pl.program_id(1)

        @pl.when(i == 0)
        def _():
            dk_ref[...] = jnp.zeros_like(dk_ref)
            dv_ref[...] = jnp.zeros_like(dv_ref)

        lane, row = _lane_row((TB, 128))
        lo = lane < HD
        tri_lt = (row < lane).astype(BF16)
        q2, lane2, qpos2, tri_gt = _sb_stack(q_ref[...].astype(F32) * SCALE, i)
        gv, dmv = g_ref[...], dm_ref[...]
        sg = _sig(gv)
        dg_ref[...] = (dmv * o_ref[...] * (sg * (1.0 + gv * (1.0 - sg)))).astype(BF16)
        do2 = _stack_heads(dmv * (gv * sg))
        ct = c_ref[...]
        trips_i = n_ref[p, i]
        first = jnp.maximum(i + 1 - SB_U * trips_i, 0)

        def step(t, carry):
            dq, g2 = carry
            jrs = [first + SB_U * t + u for u in range(SB_U)]
            j0s = [pl.multiple_of(jnp.minimum(jr, i) * TB, TB) for jr in jrs]
            ks = [k_ref[pl.ds(j0, TB), :] for j0 in j0s]
            vs = [v_ref[pl.ds(j0, TB), :] for j0 in j0s]
            zs = [_dot_nt(q2, kj) for kj in ks]
            das = [_dot_nt(do2, vj) for vj in vs]
            valids, es, lbs, l1s = [], [], [], []
            for jr, z in zip(jrs, zs):
                kpos = jr * TB + lane2
                valid = (kpos >= PAD) & (kpos < qpos2)
                e = jnp.exp(-jnp.abs(z))
                lb = jnp.minimum(z, 0.0) - jnp.log(1.0 + e)
                valids.append(valid)
                es.append(e)
                lbs.append(lb)
                l1s.append(jnp.where(valid, lb - z, 0.0))
            sfxs = [_split_dot(l1, tri_gt) for l1 in l1s]
            a_s, gmats, gpre = [], [], []
            for jr, valid, lb, sfx, da in zip(jrs, valids, lbs, sfxs, das):
                later = jnp.concatenate(
                    [jnp.sum(jnp.where(lane == 2 * jr + hh, ct, 0.0), axis=1, keepdims=True) for hh in range(2)], axis=0)
                a = jnp.where(valid, jnp.exp(lb + sfx + later), 0.0)
                gmat = da * a
                a_s.append(a.astype(BF16))
                gmats.append(gmat)
                gpre.append(g2)
                g2 = g2 + jnp.sum(gmat, axis=1, keepdims=True)
            pres = [gp + _split_dot(gmat, tri_lt) for gp, gmat in zip(gpre, gmats)]
            for j0, kj, valid, z, e, gmat, pre, a in zip(j0s, ks, valids, zs, es, gmats, pres, a_s):
                r = 1.0 / (1.0 + e)
                big = z >= 0.0
                beta = jnp.where(big, r, e * r)
                omb = jnp.where(big, e * r, r)
                dz = jnp.where(valid, gmat * omb - beta * pre, 0.0).astype(BF16)
                dq2 = _dot(dz, kj)
                dq = dq + jnp.where(lo, dq2[0:TB], dq2[TB:2 * TB])
                dk_ref[pl.ds(j0, TB), :] += _dot_tn(dz, q2)
                dv_ref[pl.ds(j0, TB), :] += _dot_tn(a, do2)
            return dq, g2

        dq, _ = lax.fori_loop(0, trips_i, step, (jnp.zeros((TB, 128), F32), jnp.zeros((2 * TB, 1), F32)))
        dq_ref[...] = (dq * SCALE).astype(BF16)

        @pl.when(i == NB - 1)
        def _():
            dko_ref[...] = dk_ref[...].astype(BF16)
            dvo_ref[...] = dv_ref[...].astype(BF16)

    slab = pl.BlockSpec((R, 128), lambda p, i: (0, p))
    blk = pl.BlockSpec((TB, 128), lambda p, i: (i, p))
    sd = jax.ShapeDtypeStruct((R, D), BF16)
    return pl.pallas_call(
        body, grid=(D // 128, NB),
        in_specs=[pl.BlockSpec(memory_space=pltpu.SMEM), blk, slab, slab, blk, blk, blk, blk],
        out_specs=[blk, slab, slab, blk], out_shape=[sd, sd, sd, sd],
        scratch_shapes=[pltpu.VMEM((R, 128), F32), pltpu.VMEM((R, 128), F32)], name="sb_bwd",
        compiler_params=_cp(("arbitrary", "arbitrary")))(trips, q, k, v, car, dm, g, o)


def _adamw(w, parts, m, v, name):
    rows, cols = w.shape
    tr = 256 if rows % 256 == 0 else rows
    nparts = len(parts)

    def body(*refs):
        w_ref = refs[0]
        p_refs = refs[1:1 + nparts]
        m_ref, v_ref, g_ref, d_ref, nm_ref, nv_ref = refs[1 + nparts:]
        g = p_refs[0][...]
        for p_ref in p_refs[1:]:
            g = g + p_ref[...]
        nm = ADAM_B1 * m_ref[...] + (1.0 - ADAM_B1) * g
        nv = ADAM_B2 * v_ref[...] + (1.0 - ADAM_B2) * (g * g)
        m_hat = nm / (1.0 - ADAM_B1 ** ADAM_STEP)
        v_hat = nv / (1.0 - ADAM_B2 ** ADAM_STEP)
        g_ref[...] = g
        d_ref[...] = -ADAM_LR * (m_hat / (jnp.sqrt(v_hat) + ADAM_EPS) + ADAM_WD * w_ref[...])
        nm_ref[...] = nm
        nv_ref[...] = nv

    blk = pl.BlockSpec((tr, cols), lambda i: (i, 0))
    sd = jax.ShapeDtypeStruct((rows, cols), F32)
    return pl.pallas_call(
        body, grid=(rows // tr,), in_specs=[blk] * (3 + nparts), out_specs=[blk] * 4, out_shape=[sd] * 4,
        name=name, compiler_params=_cp(("parallel",)))(w, *parts, m, v)


def _sum8(buf, name):
    _, rows, cols = buf.shape

    def body(b_ref, o_ref):
        acc = b_ref[0]
        for i in range(1, 8):
            acc = acc + b_ref[i]
        o_ref[...] = acc

    return pl.pallas_call(
        body, out_shape=jax.ShapeDtypeStruct((rows, cols), F32), name=name,
        compiler_params=pltpu.CompilerParams(vmem_limit_bytes=VMEM_LIMIT))(buf)


MESH = pl.DeviceIdType.MESH
ANY = pl.BlockSpec(memory_space=pl.ANY)


def _chip_peers():
    x, y = lax.axis_index("x"), lax.axis_index("y")
    return [(1 - x, y), (x, 1 - y), (1 - x, 1 - y)]


def _gather_chips(shards):
    plan = _gather_plan(shards)

    def body(*refs):
        n = len(shards)
        ins, outs, sems = refs[:n], refs[n:2 * n], refs[2 * n:]
        plan["start"](ins, outs, sems)
        plan["mid"](ins, outs, sems)
        plan["finish"](ins, outs, sems)

    n = len(shards)
    res = pl.pallas_call(
        body, in_specs=[ANY] * n, out_specs=[ANY] * n, out_shape=plan["out_shape"],
        scratch_shapes=plan["sems"], name="gather_chips")(*plan["args"])
    return plan["post"](res)


def _gather_plan(shards):
    n = len(shards)
    shards = [s.reshape((2, s.shape[0] // 2) + s.shape[1:]) for s in shards]

    def copies(kind, ins, outs, sems):
        s1, r1, s2, r2 = sems
        x, y, c = lax.axis_index("x"), lax.axis_index("y"), lax.axis_index("c")
        me = 2 * x + y
        out = []
        for j, (px, py) in enumerate(_chip_peers()):
            for a in range(n):
                k = j * n + a
                got = outs[a].at[2 * px + py].at[c]
                other = outs[a].at[2 * px + py].at[1 - c]
                src, dst, ss, rs, dev = {
                    "first": (ins[a].at[c], outs[a].at[me].at[c], s1, r1, (px, py, c)),
                    "landed": (got, got, s1, r1, (px, py, c)),
                    "passed": (got, got, s2, r2, (x, y, 1 - c)),
                    "theirs": (other, other, s2, r2, (x, y, 1 - c)),
                }[kind]
                out.append(pltpu.make_async_remote_copy(
                    src_ref=src, dst_ref=dst, send_sem=ss.at[k], recv_sem=rs.at[k], device_id=dev, device_id_type=MESH))
        return out

    def start(ins, outs, sems):
        for cp in copies("first", ins, outs, sems):
            cp.start()

    def mid(ins, outs, sems):
        for got, fwd in zip(copies("landed", ins, outs, sems), copies("passed", ins, outs, sems)):
            got.wait_recv()
            fwd.start()

    def finish(ins, outs, sems):
        for cp in copies("theirs", ins, outs, sems):
            cp.wait_recv()
        for cp in copies("first", ins, outs, sems) + copies("passed", ins, outs, sems):
            cp.wait_send()

    def post(res):
        me = 2 * lax.axis_index("x") + lax.axis_index("y")
        res = [lax.dynamic_update_index_in_dim(r, s, me, 0) for r, s in zip(res, shards)]
        return [r.reshape((N_CHIPS, 2 * r.shape[2]) + r.shape[3:]) for r in res]

    return dict(args=shards, out_shape=[jax.ShapeDtypeStruct((N_CHIPS,) + s.shape, s.dtype) for s in shards],
                sems=[pltpu.SemaphoreType.DMA((3 * n,))] * 4, start=start, mid=mid, finish=finish, post=post)


def _rows_call(name, parts, plan):
    n_in = [len(p["args"]) for p in parts]
    n_out = [len(p["out_shape"]) for p in parts]
    n_scr = [len(p["scratch"]) for p in parts]
    c_in, c_out = len(plan["args"]), len(plan["out_shape"])

    def split(refs, sizes):
        out, pos = [], 0
        for k in sizes:
            out.append(refs[pos:pos + k])
            pos += k
        return out

    def body(*refs):
        ins, outs, scr = split(refs, [sum(n_in) + c_in, sum(n_out) + c_out, sum(n_scr) + len(plan["sems"])])
        p_in, p_out, p_scr = split(ins, n_in + [c_in]), split(outs, n_out + [c_out]), split(scr, n_scr + [len(plan["sems"])])
        comm = (p_in[-1], p_out[-1], p_scr[-1])
        step = pl.program_id(0)

        @pl.when(step == 0)
        def _():
            plan["start"](*comm)

        for p, i, o, s in zip(parts, p_in, p_out, p_scr):
            p["body"](*i, *o, *s)

        @pl.when(step == NB - 2)
        def _():
            plan["mid"](*comm)

        @pl.when(step == NB - 1)
        def _():
            plan["finish"](*comm)

    flat = lambda key: [v for p in parts for v in p[key]]
    res = pl.pallas_call(
        body, grid=(NB,), in_specs=flat("in_specs") + [ANY] * c_in, out_specs=flat("out_specs") + [ANY] * c_out,
        out_shape=flat("out_shape") + plan["out_shape"], scratch_shapes=flat("scratch") + plan["sems"],
        name=name, compiler_params=_cp(("arbitrary",)))(*flat("args"), *plan["args"])
    outs = split(res, n_out + [c_out])
    return outs[:-1], outs[-1]


def _pair_exchange(grads, name):
    n = len(grads)
    hs = [g.shape[1] // 2 for g in grads]
    grads = [g.reshape((N_CHIPS, 2, h) + g.shape[2:]) for g, h in zip(grads, hs)]

    def body(*refs):
        ins, got = refs[:n], refs[n:2 * n]
        ssem, rsem = refs[2 * n:]
        x, y, c = lax.axis_index("x"), lax.axis_index("y"), lax.axis_index("c")
        sends = [pltpu.make_async_remote_copy(
            src_ref=ins[a].at[:, 1 - c], dst_ref=got[a], send_sem=ssem.at[a],
            recv_sem=rsem.at[a], device_id=(x, y, 1 - c), device_id_type=MESH) for a in range(n)]
        for cp in sends:
            cp.start()
        for cp in sends:
            cp.wait()

    half_shapes = [jax.ShapeDtypeStruct((N_CHIPS, h) + g.shape[3:], g.dtype) for g, h in zip(grads, hs)]
    got = pl.pallas_call(
        body, in_specs=[ANY] * n, out_specs=[ANY] * n, out_shape=half_shapes,
        scratch_shapes=[pltpu.SemaphoreType.DMA((n,))] * 2, name=name)(*grads)
    c = lax.axis_index("c")
    own = [lax.dynamic_index_in_dim(g, c, 1, keepdims=False) for g in grads]
    return own, got


def _sum_pair(own, got, send_dtype, name):
    _, rows, cols = own.shape
    tr = 256 if rows % 256 == 0 else rows

    def body(a_ref, b_ref, f_ref, s_ref):
        t = a_ref[...].astype(F32) + b_ref[...].astype(F32)
        f_ref[...] = t
        s_ref[...] = t.astype(send_dtype)

    blk = pl.BlockSpec((N_CHIPS, tr, cols), lambda i: (0, i, 0))
    return pl.pallas_call(
        body, grid=(rows // tr,), in_specs=[blk, blk], out_specs=[blk, blk],
        out_shape=[jax.ShapeDtypeStruct(own.shape, F32), jax.ShapeDtypeStruct(own.shape, send_dtype)],
        name=name, compiler_params=_cp(("parallel",)))(own, got)


def _scatter_chips(keep, send):
    n = len(send)
    plan = _scatter_plan(send)

    def body(*refs):
        sin, land, sems = refs[:n], refs[n:2 * n], refs[2 * n:]
        plan["start"](sin, land, sems)
        plan["finish"](sin, land, sems)

    land = pl.pallas_call(
        body, in_specs=[ANY] * n, out_specs=[ANY] * n, out_shape=plan["out_shape"],
        scratch_shapes=plan["sems"], name="scatter_chips")(*send)
    return _own_slab(keep), land


def _own_slab(keep):
    me = 2 * lax.axis_index("x") + lax.axis_index("y")
    return [lax.dynamic_index_in_dim(k, me, 0, keepdims=False) for k in keep]


def _scatter_plan(send):
    n = len(send)

    def copies(sin, land, sems):
        ssem, rsem = sems
        c = lax.axis_index("c")
        return [pltpu.make_async_remote_copy(
            src_ref=sin[a].at[2 * px + py], dst_ref=land[a].at[j], send_sem=ssem.at[j * n + a],
            recv_sem=rsem.at[j * n + a], device_id=(px, py, c), device_id_type=MESH)
            for j, (px, py) in enumerate(_chip_peers()) for a in range(n)]

    def start(sin, land, sems):
        for cp in copies(sin, land, sems):
            cp.start()

    def finish(sin, land, sems):
        for cp in copies(sin, land, sems):
            cp.wait()

    return dict(args=list(send), out_shape=[jax.ShapeDtypeStruct((3,) + s.shape[1:], s.dtype) for s in send],
                sems=[pltpu.SemaphoreType.DMA((3 * n,))] * 2, start=start, mid=lambda *a: None, finish=finish)


def _sum_shard(mine, land, name):
    rows, cols = mine.shape
    tr = 256 if rows % 256 == 0 else rows

    def body(m_ref, l_ref, o_ref):
        o_ref[...] = ((m_ref[...] + l_ref[0].astype(F32)) + l_ref[1].astype(F32)) + l_ref[2].astype(F32)

    return pl.pallas_call(
        body, grid=(rows // tr,),
        in_specs=[pl.BlockSpec((tr, cols), lambda i: (i, 0)), pl.BlockSpec((3, tr, cols), lambda i: (0, i, 0))],
        out_specs=pl.BlockSpec((tr, cols), lambda i: (i, 0)), out_shape=jax.ShapeDtypeStruct((rows, cols), F32),
        name=name, compiler_params=_cp(("parallel",)))(mine, land)


def _join_cores(halves):
    n = len(halves)

    def body(*refs):
        ins, outs = refs[:n], refs[n:2 * n]
        ssem, rsem = refs[2 * n:]
        x, y, c = lax.axis_index("x"), lax.axis_index("y"), lax.axis_index("c")
        sends = [pltpu.make_async_remote_copy(
            src_ref=ins[a], dst_ref=outs[a].at[c], send_sem=ssem.at[a], recv_sem=rsem.at[a],
            device_id=(x, y, 1 - c), device_id_type=MESH) for a in range(n)]
        for cp in sends:
            cp.start()
        for a in range(n):
            sends[a].wait_send()
            pltpu.make_async_remote_copy(
                src_ref=ins[a], dst_ref=outs[a].at[1 - c], send_sem=ssem.at[a], recv_sem=rsem.at[a],
                device_id=(x, y, 1 - c), device_id_type=MESH).wait_recv()

    res = pl.pallas_call(
        body, in_specs=[ANY] * n, out_specs=[ANY] * n,
        out_shape=[jax.ShapeDtypeStruct((2,) + h.shape, h.dtype) for h in halves],
        scratch_shapes=[pltpu.SemaphoreType.DMA((n,))] * 2, name="join_cores")(*halves)
    c = lax.axis_index("c")
    res = [lax.dynamic_update_index_in_dim(r, h, c, 0) for r, h in zip(res, halves)]
    return [r.reshape((2 * r.shape[1],) + r.shape[2:]) for r in res]


def _gather_all(vec):
    def body(v_ref, o_ref, lsem, ssem, rsem):
        x, y, c = lax.axis_index("x"), lax.axis_index("y"), lax.axis_index("c")
        me = 4 * x + 2 * y + c
        local = pltpu.make_async_copy(v_ref, o_ref.at[me], lsem)
        local.start()
        cps = []
        for k in range(1, 8):
            px, py, pc = x ^ (k >> 2), y ^ ((k >> 1) & 1), c ^ (k & 1)
            cps.append(pltpu.make_async_remote_copy(
                src_ref=v_ref, dst_ref=o_ref.at[me], send_sem=ssem.at[k - 1], recv_sem=rsem.at[k - 1],
                device_id=(px, py, pc), device_id_type=MESH))
        for cp in cps:
            cp.start()
        for k in range(1, 8):
            px, py, pc = x ^ (k >> 2), y ^ ((k >> 1) & 1), c ^ (k & 1)
            pltpu.make_async_remote_copy(
                src_ref=v_ref, dst_ref=o_ref.at[4 * px + 2 * py + pc], send_sem=ssem.at[k - 1],
                recv_sem=rsem.at[k - 1], device_id=(px, py, pc), device_id_type=MESH).wait_recv()
        for cp in cps:
            cp.wait_send()
        local.wait()

    return pl.pallas_call(
        body, in_specs=[ANY], out_specs=ANY, out_shape=jax.ShapeDtypeStruct((8,) + vec.shape, vec.dtype),
        scratch_shapes=[pltpu.SemaphoreType.DMA, pltpu.SemaphoreType.DMA((7,)), pltpu.SemaphoreType.DMA((7,))],
        name="gather_all")(vec)


def _rope_tables():
    pos = (jnp.arange(R, dtype=jnp.int32) - PAD).astype(F32)
    half = HD // 2
    inv = ROPE_THETA ** (-jnp.arange(half, dtype=F32) / half)
    ang = pos[:, None] * inv[None, :]
    cos, sin = jnp.cos(ang), jnp.sin(ang)
    cs = jnp.tile(cos, (1, 4))
    sn = jnp.tile(jnp.concatenate([-sin, sin], axis=1), (1, 2))
    return cs, sn


def _perm_cols(w):
    return jnp.concatenate([w[:, 0:512], w[:, 768:1280], w[:, 1280:2304], w[:, 2304:2816], w[:, 512:640],
                            w[:, 640:768]], axis=1)


def _unperm_cols(w):
    return jnp.concatenate([w[:, C_Q:C_Q + 512], w[:, C_K:C_K + 128], w[:, C_V:C_V + 128], w[:, C_GA:C_GA + 512],
                            w[:, C_GLU:C_GLU + 1024], w[:, C_GB:C_GB + 512]], axis=1)


def _local_step(x, target, p):
    w0 = _perm_cols(p["ab_w_in"])
    wo0, wpw = p["ab_w_out"], p["ab_w_pw2"]
    conv_w = jnp.concatenate([p["ab_conv_w"], jnp.zeros((1, CC), F32)], axis=0)
    cs, sn = _rope_tables()

    h0 = jnp.concatenate([jnp.zeros((PAD, D), F32), p["meta_tokens"], x], axis=0)

    xn0 = _rms_fwd(h0, p["ab_pre_norm"], "rms_fwd0")
    z0 = _mm([(xn0, w0)], F32, "in_proj0", 544, 1408)
    plan = _gather_plan([p["sb_w_in"], p["sb_w_out"]])
    ((o0, a0, lse0), (cv0, s0)), gathered = _rows_call(
        "fwd0", [_swa_fwd(z0, cs, sn, p["ab_sinks"]),
                 _conv_fwd(z0, conv_w, p["ab_conv_b"], p["ab_conv_ln_g"], p["ab_conv_ln_b"])], plan)
    w1, wo1 = plan["post"](gathered)
    wo1 = wo1.reshape(D, D)
    t0 = _mm([(s0, wpw)], F32, "pw2", 544, 512)
    c0 = _gate_fwd(t0, z0, C_GB, "gate_b_fwd")
    wo0h = wo0.reshape(2, CC, D)
    y0 = _mm([(a0, (wo0h, 0)), (c0, (wo0h, 1))], F32, "out_proj0", 544, 1024)

    h1, xn1 = _post_rms_fwd(h0, y0, p["ab_post_norm"], p["sb_pre_norm"], "post_rms_fwd")
    q1 =_mm([(xn1, (w1, 0))], BF16, "in_proj1_q", 544, 1024)
    k1 = _mm([(xn1, (w1, 1))], BF16, "in_proj1_k", 544, 1024)
    v1 = _mm([(xn1, (w1, 2))], BF16, "in_proj1_v", 544, 1024)
    g1 = _mm([(xn1, (w1, 3))], F32, "in_proj1_g", 544, 1024)
    o1, m1, car1, trips1 = _sb_fwd(q1, k1, v1, g1)
    y1 = _mm([(m1, wo1)], F32, "out_proj1", 544, 1024)

    dh2, dy1, d_sb_post, loss_row = _tail(h1, y1, p["sb_post_norm"], target)

    dm1 = _mm([(dy1, wo1)], F32, "out_proj1_dx", 544, 1024, tb=True)
    d_wo1 = _mm([(m1, dy1)], BF16, "out_proj1_dw", 512, 1024, ta=True)
    dq1, dk1, dv1, dg1 = _sb_bwd(trips1, q1, k1, v1, car1, dm1, g1, o1)
    dz1 = [dq1, dk1, dv1, dg1]
    dxn1 = _mm([(dz1[j], (w1, j)) for j in range(4)], F32, "in_proj1_dx", 544, 1024, tb=True)
    d_w1 = jnp.stack([_mm([(xn1, dz1[j])], BF16, "in_proj1_dw%d" % j, 512, 1024, ta=True) for j in range(4)])

    dh1, d_sb_pre, dy0, d_ab_post = _rms_post_bwd(dxn1, h1, p["sb_pre_norm"], dh2, y0, p["ab_post_norm"],
                                                  "rms_post_bwd")
    dmix0 = _mm([(dy0, wo0)], F32, "out_proj0_dx", 544, 1024, tb=True)
    d_wo0 = jnp.concatenate([_mm([(a0, dy0)], BF16, "out_proj0_dw_a", 512, 1024, ta=True),
                             _mm([(c0, dy0)], BF16, "out_proj0_dw_b", 512, 1024, ta=True)], axis=0)
    dt0, dgb0 = _gate_bwd(dmix0, 512, t0, z0, C_GB, "gate_b_bwd")
    ds0 = _mm([(dt0, wpw)], F32, "pw2_dx", 544, 512, tb=True)
    d_wpw = _mm([(s0, dt0)], BF16, "pw2_dw", 512, 512, ta=True)
    own1, got1 = _pair_exchange([d_w1, d_wo1.reshape(N_CHIPS, 256, D)], "pair_exchange1")
    pair1 = [_sum_pair(o, t, BF16, "sum_pair_" + nm) for o, t, nm in zip(own1, got1, ("sb_w_in", "sb_w_out"))]
    plan = _scatter_plan([pr[1] for pr in pair1])
    ((dglu0, d_convw, d_small), (dq0, dga0, dkv0, d_sinks)), land1 = _rows_call(
        "bwd0", [_conv_bwd(ds0, cv0, z0, conv_w, p["ab_conv_ln_g"], p["ab_conv_ln_b"]),
                 _swa_bwd(z0, cs, sn, p["ab_sinks"], o0, dmix0, lse0)], plan)
    halves1 = [_sum_shard(mi, la, "sum_shard_" + nm)
               for mi, la, nm in zip(_own_slab([pr[0] for pr in pair1]), land1, ("sb_w_in", "sb_w_out"))]
    dz0 = jnp.concatenate([dq0, dga0, dglu0, dgb0, dkv0], axis=1)
    dxn0 = _mm([(dz0, w0)], F32, "in_proj0_dx", 544, 1024, tb=True)
    d_w0 = _unperm_cols(_mm([(xn0, dz0)], BF16, "in_proj0_dw", 512, 1408, ta=True))
    dh0_first, grad_x, d_ab_pre = _rms_bwd(dxn0, h0, p["ab_pre_norm"], dh1, F32, "rms_bwd0", split=True)

    grads = {
        "meta_tokens": dh0_first[PAD:TB], "ab_pre_norm": d_ab_pre, "ab_w_in": d_w0, "ab_sinks": d_sinks[0:1, 0:8],
        "ab_conv_w": d_convw[0:CONV_W], "ab_conv_b": d_small[0:1], "ab_conv_ln_g": d_small[1:2],
        "ab_conv_ln_b": d_small[2:3], "ab_w_pw2": d_wpw, "ab_w_out": d_wo0, "ab_post_norm": d_ab_post,
        "sb_pre_norm": d_sb_pre, "sb_post_norm": d_sb_post,
    }
    return loss_row, grad_x, grads, halves1


SMALL_ROWS = 80
REP_ROWS = 32

WEIGHTS = ["meta_tokens", "ab_pre_norm", "ab_w_in", "ab_sinks", "ab_conv_w", "ab_conv_b", "ab_conv_ln_g",
           "ab_conv_ln_b", "ab_w_pw2", "ab_w_out", "ab_post_norm", "sb_pre_norm", "sb_w_in", "sb_w_out",
           "sb_post_norm"]
BIG = ["ab_w_in", "ab_w_out", "ab_w_pw2", "sb_w_in", "sb_w_out"]


def _pack_small(conv_w, meta, sb_pre, sb_post):
    rows = jnp.concatenate([conv_w, meta.reshape(32, 128), sb_pre.reshape(2, 128), sb_post.reshape(2, 128)], axis=0)
    return jnp.concatenate([rows, jnp.zeros((SMALL_ROWS - rows.shape[0], 128), F32)], axis=0)


def _unpack_small(s):
    return s[0:31], s[31:63].reshape(16, 256), s[63:65].reshape(1, 256), s[65:67].reshape(1, 256)


REP_LOSS = 3592


def _pack_rep(pre, post, conv_b, ln_g, ln_b, sinks, extra=None):
    flat = jnp.concatenate([pre.reshape(-1), post.reshape(-1), conv_b.reshape(-1), ln_g.reshape(-1),
                            ln_b.reshape(-1), sinks.reshape(-1)] + ([] if extra is None else [extra.reshape(-1)]))
    flat = jnp.concatenate([flat, jnp.zeros((REP_ROWS * 128 - flat.shape[0],), F32)])
    return flat.reshape(REP_ROWS, 128)


def _unpack_rep(r):
    f = r.reshape(-1)
    return (f[0:1024].reshape(1, 1024), f[1024:2048].reshape(1, 1024), f[2048:2560].reshape(1, 512),
            f[2560:3072].reshape(1, 512), f[3072:3584].reshape(1, 512), f[3584:3592].reshape(1, 8))


def _cols_to_chips(w, width):
    return w.reshape(w.shape[0], N_CHIPS, width).transpose(1, 0, 2)


def _chips_to_cols(w):
    return w.transpose(1, 0, 2).reshape(w.shape[1], -1)


def kernel(x, meta_tokens, ab_pre_norm, ab_w_in, ab_sinks, ab_conv_w, ab_conv_b, ab_conv_ln_g, ab_conv_ln_b, ab_w_pw2, ab_w_out, ab_post_norm, sb_pre_norm, sb_w_in, sb_w_out, sb_post_norm, loss_target, m_meta_tokens, m_ab_pre_norm, m_ab_w_in, m_ab_sinks, m_ab_conv_w, m_ab_conv_b, m_ab_conv_ln_g, m_ab_conv_ln_b, m_ab_w_pw2, m_ab_w_out, m_ab_post_norm, m_sb_pre_norm, m_sb_w_in, m_sb_w_out, m_sb_post_norm, v_meta_tokens, v_ab_pre_norm, v_ab_w_in, v_ab_sinks, v_ab_conv_w, v_ab_conv_b, v_ab_conv_ln_g, v_ab_conv_ln_b, v_ab_w_pw2, v_ab_w_out, v_ab_post_norm, v_sb_pre_norm, v_sb_w_in, v_sb_w_out, v_sb_post_norm):
    w = dict(meta_tokens=meta_tokens, ab_pre_norm=ab_pre_norm, ab_w_in=ab_w_in, ab_sinks=ab_sinks,
             ab_conv_w=ab_conv_w, ab_conv_b=ab_conv_b, ab_conv_ln_g=ab_conv_ln_g, ab_conv_ln_b=ab_conv_ln_b,
             ab_w_pw2=ab_w_pw2, ab_w_out=ab_w_out, ab_post_norm=ab_post_norm, sb_pre_norm=sb_pre_norm,
             sb_w_in=sb_w_in, sb_w_out=sb_w_out, sb_post_norm=sb_post_norm)
    m = dict(meta_tokens=m_meta_tokens, ab_pre_norm=m_ab_pre_norm, ab_w_in=m_ab_w_in, ab_sinks=m_ab_sinks,
             ab_conv_w=m_ab_conv_w, ab_conv_b=m_ab_conv_b, ab_conv_ln_g=m_ab_conv_ln_g,
             ab_conv_ln_b=m_ab_conv_ln_b, ab_w_pw2=m_ab_w_pw2, ab_w_out=m_ab_w_out, ab_post_norm=m_ab_post_norm,
             sb_pre_norm=m_sb_pre_norm, sb_w_in=m_sb_w_in, sb_w_out=m_sb_w_out, sb_post_norm=m_sb_post_norm)
    v = dict(meta_tokens=v_meta_tokens, ab_pre_norm=v_ab_pre_norm, ab_w_in=v_ab_w_in, ab_sinks=v_ab_sinks,
             ab_conv_w=v_ab_conv_w, ab_conv_b=v_ab_conv_b, ab_conv_ln_g=v_ab_conv_ln_g,
             ab_conv_ln_b=v_ab_conv_ln_b, ab_w_pw2=v_ab_w_pw2, ab_w_out=v_ab_w_out, ab_post_norm=v_ab_post_norm,
             sb_pre_norm=v_sb_pre_norm, sb_w_in=v_sb_w_in, sb_w_out=v_sb_w_out, sb_post_norm=v_sb_post_norm)

    def small_of(d):
        return _pack_small(d["ab_conv_w"][0], d["meta_tokens"], d["sb_pre_norm"], d["sb_post_norm"])

    def rep_of(d):
        return _pack_rep(d["ab_pre_norm"], d["ab_post_norm"], d["ab_conv_b"], d["ab_conv_ln_g"], d["ab_conv_ln_b"],
                         d["ab_sinks"])

    gathered = _gather_chips([w[k][0].astype(BF16) for k in BIG[:3]] + [small_of(w)])
    g_in0, g_out0, g_pw2, g_small = gathered
    conv_w_f = _chips_to_cols(g_small[:, 0:31])
    meta_f = _chips_to_cols(g_small[:, 31:63].reshape(N_CHIPS, 16, 256))
    sb_pre_f = g_small[:, 63:65].reshape(1, D)
    sb_post_f = g_small[:, 65:67].reshape(1, D)
    full = {
        "meta_tokens": meta_f, "ab_pre_norm": ab_pre_norm, "ab_w_in": _chips_to_cols(g_in0),
        "ab_sinks": ab_sinks, "ab_conv_w": conv_w_f, "ab_conv_b": ab_conv_b, "ab_conv_ln_g": ab_conv_ln_g,
        "ab_conv_ln_b": ab_conv_ln_b, "ab_w_pw2": g_pw2.reshape(CC, CC), "ab_w_out": g_out0.reshape(D, D),
        "ab_post_norm": ab_post_norm, "sb_pre_norm": sb_pre_f, "sb_w_in": sb_w_in[0].astype(BF16),
        "sb_w_out": sb_w_out[0].astype(BF16), "sb_post_norm": sb_post_f,
    }

    loss_row, grad_x, g, halves1 = _local_step(x[0], loss_target[0], full)

    send = [_cols_to_chips(g["ab_w_in"], 704), g["ab_w_out"].reshape(N_CHIPS, 256, D),
            g["ab_w_pw2"].reshape(N_CHIPS, 128, CC)]
    gs_conv = _cols_to_chips(g["ab_conv_w"], 128)
    gs_meta = _cols_to_chips(g["meta_tokens"], 256)
    gs_pre = g["sb_pre_norm"].reshape(N_CHIPS, 1, 256)
    gs_post = g["sb_post_norm"].reshape(N_CHIPS, 1, 256)
    send.append(jnp.stack([_pack_small(gs_conv[j], gs_meta[j], gs_pre[j], gs_post[j]) for j in range(N_CHIPS)]))
    names = BIG[:3] + ["small"]
    own, got = _pair_exchange(send, "pair_exchange0")
    pair = [_sum_pair(o, t, o.dtype, "sum_pair_" + nm) for o, t, nm in zip(own, got, names)]
    mine, land = _scatter_chips([pr[0] for pr in pair], [pr[1] for pr in pair])
    halves = [_sum_shard(mi, la, "sum_shard_" + nm) for mi, la, nm in zip(mine, land, names)]
    total = _join_cores(halves[:3] + halves1 + halves[3:])

    rep_g = _pack_rep(g["ab_pre_norm"], g["ab_post_norm"], g["ab_conv_b"], g["ab_conv_ln_g"], g["ab_conv_ln_b"],
                      g["ab_sinks"], loss_row[0:1, 0:1])
    rep_sum = _sum8(_gather_all(rep_g), "sum8_rep")
    loss = rep_sum.reshape(-1)[REP_LOSS]

    out_g, out_d, out_m, out_v = {}, {}, {}, {}
    for i, k in enumerate(BIG):
        shp = w[k].shape
        res = _adamw(w[k][0], [total[i]], m[k][0], v[k][0], "adamw_" + k)
        out_g[k], out_d[k], out_m[k], out_v[k] = [r.reshape(shp) for r in res]
    res = _adamw(small_of(w), [total[5]], small_of(m), small_of(v), "adamw_small")
    for dst, r in zip((out_g, out_d, out_m, out_v), res):
        cw, mt, pre, post = _unpack_small(r)
        dst["ab_conv_w"], dst["meta_tokens"], dst["sb_pre_norm"], dst["sb_post_norm"] = cw[None], mt, pre, post
    res = _adamw(rep_of(w), [rep_sum], rep_of(m), rep_of(v), "adamw_rep")
    for dst, r in zip((out_g, out_d, out_m, out_v), res):
        (dst["ab_pre_norm"], dst["ab_post_norm"], dst["ab_conv_b"], dst["ab_conv_ln_g"], dst["ab_conv_ln_b"],
         dst["ab_sinks"]) = _unpack_rep(r)

    return (loss, grad_x[None], *[out_g[k] for k in WEIGHTS], *[out_d[k] for k in WEIGHTS],
            *[out_m[k] for k in WEIGHTS], *[out_v[k] for k in WEIGHTS])
```

```python
import functools

import jax
import jax.numpy as jnp
from jax import lax
from jax.experimental import pallas as pl
from jax.experimental.pallas import tpu as pltpu

F32 = jnp.float32
BF16 = jnp.bfloat16

D = 1024
SEQ = 2048
N_META = 16
TB = 128
TR = 272
PAD = TB - N_META
R = SEQ + TB
NB = R // TB
HD = 64
ROPE_THETA = 10000.0
NORM_EPS = 1e-6
LN_EPS = 1e-5
NEG = -1e30
SWA_HEADS = 8
CONV_W = 31
SCALE = HD ** -0.5
N_CHIPS = 4

C_Q, C_GA, C_GLU, C_GB, C_K, C_V = 0, 512, 1024, 2048, 2560, 2688
AB_IN = 2816

ADAM_LR, ADAM_B1, ADAM_B2, ADAM_EPS, ADAM_WD, ADAM_STEP = 0.001, 0.9, 0.999, 1e-08, 0.01, 10

VMEM_LIMIT = 56 * 1024 * 1024


def _cp(sem):
    return pltpu.CompilerParams(dimension_semantics=sem, vmem_limit_bytes=VMEM_LIMIT)


def _sig(x):
    return 1.0 / (1.0 + jnp.exp(-x))


def _dot(a, b):
    return lax.dot_general(a, b, (((1,), (0,)), ((), ())), preferred_element_type=F32)


def _dot_nt(a, b):
    return lax.dot_general(a, b, (((1,), (1,)), ((), ())), preferred_element_type=F32)


def _dot_tn(a, b):
    return lax.dot_general(a, b, (((0,), (0,)), ((), ())), preferred_element_type=F32)


def _mm(pairs, out_dtype, name, tm, tn, ta=False, tb=False):
    pairs = [(a, b if isinstance(b, tuple) else (b, None)) for a, b in pairs]
    a0, (b0, _) = pairs[0]
    m = a0.shape[1] if ta else a0.shape[0]
    n = b0.shape[-2] if tb else b0.shape[-1]
    npairs = len(pairs)
    dims = (((0 if ta else 1,), (1 if tb else 0,)), ((), ()))

    def body(*refs):
        o_ref = refs[2 * npairs]
        acc = None
        for i in range(npairs):
            t = lax.dot_general(refs[2 * i][...].astype(BF16), refs[2 * i + 1][...].astype(BF16), dims,
                                preferred_element_type=F32)
            acc = t if acc is None else acc + t
        o_ref[...] = acc.astype(out_dtype)

    in_specs, args = [], []
    for a, (b, sel) in pairs:
        k = a.shape[0] if ta else a.shape[1]
        in_specs.append(pl.BlockSpec((k, tm), lambda i, j: (0, i)) if ta else pl.BlockSpec((tm, k), lambda i, j: (i, 0)))
        bshape, bidx = ((tn, k), lambda i, j: (j, 0)) if tb else ((k, tn), lambda i, j: (0, j))
        if sel is None:
            in_specs.append(pl.BlockSpec(bshape, bidx))
        else:
            in_specs.append(pl.BlockSpec((None,) + bshape, functools.partial(lambda i, j, f, s: (s,) + f(i, j), f=bidx, s=sel)))
        args += [a, b]
    return pl.pallas_call(
        body, grid=(m // tm, n // tn), in_specs=in_specs,
        out_specs=pl.BlockSpec((tm, tn), lambda i, j: (i, j)),
        out_shape=jax.ShapeDtypeStruct((m, n), out_dtype), name=name,
        compiler_params=_cp(("parallel", "parallel")))(*args)


def _rms_fwd(h, g, name):
    def body(h_ref, g_ref, o_ref):
        x = h_ref[...]
        r = lax.rsqrt(jnp.mean(x * x, axis=1, keepdims=True) + NORM_EPS)
        o_ref[...] = (x * r * g_ref[...]).astype(BF16)

    return pl.pallas_call(
        body, grid=(R // TR,),
        in_specs=[pl.BlockSpec((TR, D), lambda n: (n, 0)), pl.BlockSpec((1, D), lambda n: (0, 0))],
        out_specs=pl.BlockSpec((TR, D), lambda n: (n, 0)),
        out_shape=jax.ShapeDtypeStruct((R, D), BF16), name=name, compiler_params=_cp(("parallel",)))(h, g)


def _rms_bwd(dout, x, g, res, out_dtype, name, split=False):
    has_res = res is not None

    def body(*refs):
        if split:
            refs = list(refs)
            dx_rest_ref = refs.pop(-2)
        if has_res:
            d_ref, x_ref, g_ref, r_ref, dx_ref, dg_ref = refs
        else:
            d_ref, x_ref, g_ref, dx_ref, dg_ref = refs
        n = pl.program_id(0)
        xv = x_ref[...]
        dv = d_ref[...]
        r = lax.rsqrt(jnp.mean(xv * xv, axis=1, keepdims=True) + NORM_EPS)
        xh = xv * r
        dxh = dv * g_ref[...]
        dx = r * (dxh - xh * jnp.mean(dxh * xh, axis=1, keepdims=True))
        if has_res:
            dx = dx + r_ref[...]
        row = lax.broadcasted_iota(jnp.int32, (TB, D), 0) + n * TB
        dx = jnp.where(row >= PAD, dx, 0.0).astype(out_dtype)
        if split:
            @pl.when(n == 0)
            def _():
                dx_ref[...] = dx

            @pl.when(n > 0)
            def _():
                dx_rest_ref[...] = dx
        else:
            dx_ref[...] = dx

        @pl.when(n == 0)
        def _():
            dg_ref[...] = jnp.zeros_like(dg_ref)

        dg_ref[...] += jnp.sum(dv * xh, axis=0, keepdims=True)

    blk = pl.BlockSpec((TB, D), lambda n: (n, 0))
    vec = pl.BlockSpec((1, D), lambda n: (0, 0))
    ins = [dout, x, g] + ([res] if has_res else [])
    in_specs = [blk, blk, vec] + ([blk] if has_res else [])
    if split:
        out_specs = [pl.BlockSpec((TB, D), lambda n: (0, 0)), pl.BlockSpec((TB, D), lambda n: (jnp.maximum(n - 1, 0), 0)), vec]
        out_shape = [jax.ShapeDtypeStruct((TB, D), out_dtype), jax.ShapeDtypeStruct((SEQ, D), out_dtype),
                     jax.ShapeDtypeStruct((1, D), F32)]
    else:
        out_specs = [blk, vec]
        out_shape = [jax.ShapeDtypeStruct((R, D), out_dtype), jax.ShapeDtypeStruct((1, D), F32)]
    return pl.pallas_call(
        body, grid=(NB,), in_specs=in_specs, out_specs=out_specs, out_shape=out_shape,
        name=name, compiler_params=_cp(("arbitrary",)))(*ins)


def _post_rms_fwd(h, y, g_post, g_next, name):
    def body(h_ref, y_ref, gp_ref, gn_ref, o_ref, x_ref):
        yv = y_ref[...]
        r = lax.rsqrt(jnp.mean(yv * yv, axis=1, keepdims=True) + NORM_EPS)
        hn = h_ref[...] + yv * r * gp_ref[...]
        o_ref[...] = hn
        r2 = lax.rsqrt(jnp.mean(hn * hn, axis=1, keepdims=True) + NORM_EPS)
        x_ref[...] = (hn * r2 * gn_ref[...]).astype(BF16)

    blk = pl.BlockSpec((TR, D), lambda n: (n, 0))
    vec = pl.BlockSpec((1, D), lambda n: (0, 0))
    return pl.pallas_call(
        body, grid=(R // TR,), in_specs=[blk, blk, vec, vec], out_specs=[blk, blk],
        out_shape=[jax.ShapeDtypeStruct((R, D), F32), jax.ShapeDtypeStruct((R, D), BF16)],
        name=name, compiler_params=_cp(("parallel",)))(h, y, g_post, g_next)


def _rms_post_bwd(dxn, h, g, res, y, g_post, name):
    def body(d_ref, h_ref, g_ref, r_ref, y_ref, gp_ref, dh_ref, dg_ref, dy_ref, dgp_ref):
        n = pl.program_id(0)

        @pl.when(n == 0)
        def _():
            dg_ref[...] = jnp.zeros_like(dg_ref)
            dgp_ref[...] = jnp.zeros_like(dgp_ref)

        hv, dv = h_ref[...], d_ref[...]
        r = lax.rsqrt(jnp.mean(hv * hv, axis=1, keepdims=True) + NORM_EPS)
        xh = hv * r
        dxh = dv * g_ref[...]
        dh = r * (dxh - xh * jnp.mean(dxh * xh, axis=1, keepdims=True)) + r_ref[...]
        row = lax.broadcasted_iota(jnp.int32, (TR, D), 0) + n * TR
        dh = jnp.where(row >= PAD, dh, 0.0)
        dh_ref[...] = dh
        dg_ref[...] += jnp.sum(dv * xh, axis=0, keepdims=True)
        yv = y_ref[...]
        ry = lax.rsqrt(jnp.mean(yv * yv, axis=1, keepdims=True) + NORM_EPS)
        yh = yv * ry
        dyh = dh * gp_ref[...]
        dy_ref[...] = (ry * (dyh - yh * jnp.mean(dyh * yh, axis=1, keepdims=True))).astype(BF16)
        dgp_ref[...] += jnp.sum(dh * yh, axis=0, keepdims=True)

    blk = pl.BlockSpec((TR, D), lambda n: (n, 0))
    vec = pl.BlockSpec((1, D), lambda n: (0, 0))
    return pl.pallas_call(
        body, grid=(R // TR,), in_specs=[blk, blk, vec, blk, blk, vec], out_specs=[blk, vec, blk, vec],
        out_shape=[jax.ShapeDtypeStruct((R, D), F32), jax.ShapeDtypeStruct((1, D), F32),
                   jax.ShapeDtypeStruct((R, D), BF16), jax.ShapeDtypeStruct((1, D), F32)],
        name=name, compiler_params=_cp(("arbitrary",)))(dxn, h, g, res, y, g_post)


GW = 512


def _gate_fwd(o, gsrc, goff, name):
    w = o.shape[1]

    def body(o_ref, g_ref, m_ref):
        gv = g_ref[...]
        m_ref[...] = (o_ref[...] * (gv * _sig(gv))).astype(BF16)

    gb = goff // GW
    return pl.pallas_call(
        body, grid=(R // TR, w // GW),
        in_specs=[pl.BlockSpec((TR, GW), lambda n, j: (n, j)), pl.BlockSpec((TR, GW), lambda n, j: (n, gb + j))],
        out_specs=pl.BlockSpec((TR, GW), lambda n, j: (n, j)),
        out_shape=jax.ShapeDtypeStruct((R, w), BF16), name=name,
        compiler_params=_cp(("parallel", "parallel")))(o, gsrc)


def _gate_bwd(dsrc, doff, o, gsrc, goff, name):
    w = o.shape[1]

    def body(d_ref, o_ref, g_ref, do_ref, dg_ref):
        gv = g_ref[...]
        dv = d_ref[...]
        s = _sig(gv)
        do_ref[...] = dv * (gv * s)
        dg_ref[...] = (dv * o_ref[...] * (s * (1.0 + gv * (1.0 - s)))).astype(BF16)

    db, gb = doff // GW, goff // GW
    blk = pl.BlockSpec((TR, GW), lambda n, j: (n, j))
    return pl.pallas_call(
        body, grid=(R // TR, w // GW),
        in_specs=[pl.BlockSpec((TR, GW), lambda n, j: (n, db + j)), blk,
                  pl.BlockSpec((TR, GW), lambda n, j: (n, gb + j))],
        out_specs=[blk, blk],
        out_shape=[jax.ShapeDtypeStruct((R, w), F32), jax.ShapeDtypeStruct((R, w), BF16)], name=name,
        compiler_params=_cp(("parallel", "parallel")))(dsrc, o, gsrc)


def _tail(h, y, g, target):
    def body(h_ref, y_ref, g_ref, t_ref, d_ref, dy_ref, dg_ref, l_ref):
        n = pl.program_id(0)

        @pl.when(n == 0)
        def _():
            d_ref[...] = jnp.zeros_like(d_ref)
            dy_ref[...] = jnp.zeros_like(dy_ref)
            dg_ref[...] = jnp.zeros_like(dg_ref)
            l_ref[...] = jnp.zeros_like(l_ref)

        @pl.when(n > 0)
        def _():
            yv = y_ref[...]
            r = lax.rsqrt(jnp.mean(yv * yv, axis=1, keepdims=True) + NORM_EPS)
            yh = yv * r
            err = (h_ref[...] + yh * g_ref[...]) - t_ref[...]
            dv = err * (1.0 / D)
            d_ref[...] = dv
            l_ref[...] += jnp.sum(err * err, axis=0, keepdims=True)
            dyh = dv * g_ref[...]
            dy_ref[...] = (r * (dyh - yh * jnp.mean(dyh * yh, axis=1, keepdims=True))).astype(BF16)
            dg_ref[...] += jnp.sum(dv * yh, axis=0, keepdims=True)

        @pl.when(n == NB - 1)
        def _():
            tot = jnp.sum(l_ref[...], axis=1, keepdims=True) * (0.5 / D)
            l_ref[...] = jnp.broadcast_to(tot, (1, D))

    blk = pl.BlockSpec((TB, D), lambda n: (n, 0))
    vec = pl.BlockSpec((1, D), lambda n: (0, 0))
    return pl.pallas_call(
        body, grid=(NB,),
        in_specs=[blk, blk, vec, pl.BlockSpec((TB, D), lambda n: (jnp.maximum(n - 1, 0), 0))],
        out_specs=[blk, blk, vec, vec],
        out_shape=[jax.ShapeDtypeStruct((R, D), F32), jax.ShapeDtypeStruct((R, D), BF16),
                   jax.ShapeDtypeStruct((1, D), F32), jax.ShapeDtypeStruct((1, D), F32)],
        name="tail", compiler_params=_cp(("arbitrary",)))(h, y, g, target)


def _lane_row(shape):
    return lax.broadcasted_iota(jnp.int32, shape, 1), lax.broadcasted_iota(jnp.int32, shape, 0)


def _rot_half(x, lane):
    return jnp.where(lane % HD < HD // 2, pltpu.roll(x, 128 - HD // 2, 1), pltpu.roll(x, HD // 2, 1))


def _swa_blocks(n):
    return (0, jnp.maximum(n - 1, 0), n)


SWA_STACKS = ((0, 0), (0, 1), (1, 0), (1, 1))


def _swa_masks(n, lane, row):
    qpos = n * TB + (row & (TB - 1))
    kp = (n - 1) * TB + lane
    kc = n * TB + lane
    m0 = (lane >= PAD) & (qpos - lane >= TB)
    mp = (kp >= PAD) & (qpos >= kp) & (qpos - kp < TB)
    mc = (kc >= PAD) & (qpos >= kc)
    return (m0, mp, mc)


def _stack_pair(xa, xb, par):
    lane = lax.broadcasted_iota(jnp.int32, (TB, 128), 1)
    keep = (lane < HD) if par == 0 else (lane >= HD)
    return jnp.concatenate([jnp.where(keep, xa, 0.0), jnp.where(keep, xb, 0.0)], axis=0)


def _per_head(a, b):
    row = lax.broadcasted_iota(jnp.int32, (2 * TB, 1), 0)
    return jnp.where(row < TB, a, b)


def _swa_load(n, zq_ref, zkv_ref, cs_ref, sn_ref, lane):
    r0 = pl.multiple_of(n * TB, TB)
    csq, snq = cs_ref[pl.ds(r0, TB), :], sn_ref[pl.ds(r0, TB), :]
    qc = []
    for c in range(4):
        x = zq_ref[:, c * 128:(c + 1) * 128]
        qc.append((x * csq + _rot_half(x, lane) * snq) * SCALE)
    qst = [_stack_pair(qc[2 * g], qc[2 * g + 1], par).astype(BF16) for g, par in SWA_STACKS]
    kvs = []
    for b in _swa_blocks(n):
        b0 = pl.multiple_of(b * TB, TB)
        csb, snb = cs_ref[pl.ds(b0, TB), :], sn_ref[pl.ds(b0, TB), :]
        kx = zkv_ref[pl.ds(b0, TB), 0:128]
        kr = kx * csb + _rot_half(kx, lane) * snb
        vx = zkv_ref[pl.ds(b0, TB), 128:256]
        kvs.append((kr.astype(BF16), pltpu.roll(kr, HD, 1).astype(BF16),
                    vx.astype(BF16), pltpu.roll(vx, HD, 1).astype(BF16), csb, snb, b0))
    return qst, (csq, snq), kvs


def _swa_fwd(z0, cs, sn, sinks):
    def body(zq_ref, zkv_ref, cs_ref, sn_ref, sk_ref, ga_ref, o_ref, a_ref, lse_ref):
        n = pl.program_id(0)
        lane, row = _lane_row((TB, 128))
        lo = lane < HD
        masks = _swa_masks(n, *_lane_row((2 * TB, 128)))
        qst, _, kvs = _swa_load(n, zq_ref, zkv_ref, cs_ref, sn_ref, lane)
        ss = [[jnp.where(m, _dot_nt(qst[si], k if par == g else ka), NEG)
               for (k, ka, _, _, _, _, _), m in zip(kvs, masks)] for si, (g, par) in enumerate(SWA_STACKS)]
        o2, lse2 = [], []
        for si, (g, par) in enumerate(SWA_STACKS):
            sink = _per_head(sk_ref[0, 4 * g + par], sk_ref[0, 4 * g + 2 + par])
            s = ss[si]
            mx = jnp.maximum(jnp.maximum(jnp.max(s[0], axis=1, keepdims=True), jnp.max(s[1], axis=1, keepdims=True)),
                             jnp.max(s[2], axis=1, keepdims=True))
            mx = jnp.maximum(mx, sink)
            es = [jnp.exp(sb - mx) for sb in s]
            den = (jnp.sum(es[0], axis=1, keepdims=True) + jnp.sum(es[1], axis=1, keepdims=True)
                   + jnp.sum(es[2], axis=1, keepdims=True) + jnp.exp(sink - mx))
            inv = 1.0 / den
            t = jnp.zeros((2 * TB, 128), F32)
            for (_, _, v, va, _, _, _), e in zip(kvs, es):
                t = t + _dot((e * inv).astype(BF16), v if par == g else va)
            o2.append(t)
            lse2.append(mx + jnp.log(den))
        lse_t = jnp.zeros((TB, 128), F32)
        for g in range(2):
            for t in range(2):
                rows = slice(t * TB, (t + 1) * TB)
                c = 2 * g + t
                oc = jnp.where(lo, o2[2 * g][rows], o2[2 * g + 1][rows])
                o_ref[:, c * 128:(c + 1) * 128] = oc
                gv = ga_ref[:, c * 128:(c + 1) * 128]
                a_ref[:, c * 128:(c + 1) * 128] = (oc * (gv * _sig(gv))).astype(BF16)
                for par in range(2):
                    lse_t = jnp.where(lane == 4 * g + 2 * t + par, lse2[2 * g + par][rows], lse_t)
        lse_ref[...] = lse_t

    full = pl.BlockSpec((R, 128), lambda n: (0, 0))
    return dict(
        body=body,
        in_specs=[pl.BlockSpec((TB, 512), lambda n: (n, C_Q // 512)),
                  pl.BlockSpec((R, 256), lambda n: (0, C_K // 256)), full, full,
                  pl.BlockSpec(memory_space=pltpu.SMEM), pl.BlockSpec((TB, 512), lambda n: (n, C_GA // 512))],
        args=[z0, z0, cs, sn, sinks, z0],
        out_specs=[pl.BlockSpec((TB, 512), lambda n: (n, 0)), pl.BlockSpec((TB, 512), lambda n: (n, 0)),
                   pl.BlockSpec((TB, 128), lambda n: (n, 0))],
        out_shape=[jax.ShapeDtypeStruct((R, 512), F32), jax.ShapeDtypeStruct((R, 512), BF16),
                   jax.ShapeDtypeStruct((R, 128), F32)],
        scratch=[])


def _swa_bwd(z0, cs, sn, sinks, o, dmix, lse):
    def body(zq_ref, zkv_ref, cs_ref, sn_ref, sk_ref, ga_ref, o_ref, dm_ref, lse_ref,
             dq_ref, dga_ref, dkv_ref, dsk_ref, do_ref, acc_ref):
        n = pl.program_id(0)

        @pl.when(n == 0)
        def _():
            acc_ref[...] = jnp.zeros_like(acc_ref)
            dsk_ref[...] = jnp.zeros_like(dsk_ref)

        gv, dmv = ga_ref[...], dm_ref[...]
        sg = _sig(gv)
        dga_ref[...] = (dmv * o_ref[...] * (sg * (1.0 + gv * (1.0 - sg)))).astype(BF16)
        do_ref[...] = dmv * (gv * sg)
        lane, row = _lane_row((TB, 128))
        lo = lane < HD
        masks = _swa_masks(n, *_lane_row((2 * TB, 128)))
        qst, (csq, snq), kvs = _swa_load(n, zq_ref, zkv_ref, cs_ref, sn_ref, lane)
        lse_t = lse_ref[...]
        ss = [[jnp.where(m, _dot_nt(qst[si], k if par == g else ka), NEG)
               for (k, ka, _, _, _, _, _), m in zip(kvs, masks)] for si, (g, par) in enumerate(SWA_STACKS)]
        dobs, deltas, lses, dps = [], [], [], []
        for g, par in SWA_STACKS:
            ca, cb = slice(2 * g * 128, (2 * g + 1) * 128), slice((2 * g + 1) * 128, (2 * g + 2) * 128)
            dom = _stack_pair(do_ref[:, ca], do_ref[:, cb], par)
            deltas.append(jnp.sum(dom * jnp.concatenate([o_ref[:, ca], o_ref[:, cb]], axis=0), axis=1, keepdims=True))
            dob = dom.astype(BF16)
            dobs.append(dob)
            lses.append(jnp.concatenate(
                [jnp.sum(jnp.where(lane == 4 * g + 2 * t + par, lse_t, 0.0), axis=1, keepdims=True) for t in range(2)],
                axis=0))
            dps.append([_dot_nt(dob, v if par == g else va) for (_, _, v, va, _, _, _) in kvs])
        dk_al = [jnp.zeros((TB, 128), F32) for _ in range(3)]
        dk_mis = [jnp.zeros((TB, 128), F32) for _ in range(3)]
        dv_al = [jnp.zeros((TB, 128), F32) for _ in range(3)]
        dv_mis = [jnp.zeros((TB, 128), F32) for _ in range(3)]
        dsk_t = jnp.zeros((TB, 128), F32)
        dq2 = []
        for si, (g, par) in enumerate(SWA_STACKS):
            dqt = jnp.zeros((2 * TB, 128), F32)
            for bi, (k, ka, _, _, _, _, _) in enumerate(kvs):
                p = jnp.exp(ss[si][bi] - lses[si])
                ds = (p * (dps[si][bi] - deltas[si])).astype(BF16)
                dqt = dqt + _dot(ds, k if par == g else ka)
                dkh = _dot_tn(ds, qst[si])
                dvh = _dot_tn(p.astype(BF16), dobs[si])
                if par == g:
                    dk_al[bi] = dk_al[bi] + dkh
                    dv_al[bi] = dv_al[bi] + dvh
                else:
                    dk_mis[bi] = dk_mis[bi] + dkh
                    dv_mis[bi] = dv_mis[bi] + dvh
            dq2.append(dqt)
            sink = _per_head(sk_ref[0, 4 * g + par], sk_ref[0, 4 * g + 2 + par])
            dsk = -jnp.exp(sink - lses[si]) * deltas[si]
            for t in range(2):
                dsk_t = jnp.where(lane == 4 * g + 2 * t + par, dsk[t * TB:(t + 1) * TB], dsk_t)
        for g in range(2):
            for t in range(2):
                rows = slice(t * TB, (t + 1) * TB)
                c = 2 * g + t
                dqc = jnp.where(lo, dq2[2 * g][rows], dq2[2 * g + 1][rows]) * SCALE
                dq_ref[:, c * 128:(c + 1) * 128] = (dqc * csq + _rot_half(dqc * snq, lane)).astype(BF16)
        for bi, (_, _, _, _, csb, snb, b0) in enumerate(kvs):
            dk = dk_al[bi] + pltpu.roll(dk_mis[bi], HD, 1)
            dv = dv_al[bi] + pltpu.roll(dv_mis[bi], HD, 1)
            acc_ref[pl.ds(b0, TB), 0:128] += dk * csb + _rot_half(dk * snb, lane)
            acc_ref[pl.ds(b0, TB), 128:256] += dv
        dsk_ref[0:1, :] += jnp.sum(dsk_t, axis=0, keepdims=True)

        @pl.when(n == NB - 1)
        def _():
            dkv_ref[...] = acc_ref[...].astype(BF16)

    full = pl.BlockSpec((R, 128), lambda n: (0, 0))
    b512 = pl.BlockSpec((TB, 512), lambda n: (n, 0))
    return dict(
        body=body,
        in_specs=[pl.BlockSpec((TB, 512), lambda n: (n, C_Q // 512)),
                  pl.BlockSpec((R, 256), lambda n: (0, C_K // 256)), full, full,
                  pl.BlockSpec(memory_space=pltpu.SMEM), pl.BlockSpec((TB, 512), lambda n: (n, C_GA // 512)),
                  b512, b512, pl.BlockSpec((TB, 128), lambda n: (n, 0))],
        args=[z0, z0, cs, sn, sinks, z0, o, dmix, lse],
        out_specs=[b512, b512, pl.BlockSpec((R, 256), lambda n: (0, 0)), pl.BlockSpec((8, 128), lambda n: (0, 0))],
        out_shape=[jax.ShapeDtypeStruct((R, 512), BF16), jax.ShapeDtypeStruct((R, 512), BF16),
                   jax.ShapeDtypeStruct((R, 256), BF16), jax.ShapeDtypeStruct((8, 128), F32)],
        scratch=[pltpu.VMEM((TB, 512), F32), pltpu.VMEM((R, 256), F32)])


CC = 512
HALO = CONV_W - 1


def _conv_fwd(z0, conv_w, conv_b, ln_g, ln_b):
    def body(g_ref, w_ref, cb_ref, lg_ref, lb_ref, cv_ref, s_ref, ubuf):
        n = pl.program_id(0)

        @pl.when(n == 0)
        def _():
            ubuf[...] = jnp.zeros_like(ubuf)

        u = g_ref[:, 0:CC] * _sig(g_ref[:, CC:2 * CC])
        for k in range(8):
            ubuf[k, 0:TB + 8, :] = ubuf[k, TB:2 * TB + 8, :]
            ubuf[k, pl.ds(TB + 8 - k, TB), :] = u
        acc = jnp.zeros((TB, CC), F32)
        for w in range(CONV_W):
            off = TB - HALO + w
            acc = acc + ubuf[off % 8, pl.ds(off + 8 - off % 8, TB), :] * w_ref[w:w + 1, :]
        cv = acc + cb_ref[...]
        cv_ref[...] = cv
        xc = cv - jnp.mean(cv, axis=1, keepdims=True)
        rs = lax.rsqrt(jnp.mean(xc * xc, axis=1, keepdims=True) + LN_EPS)
        ln = xc * rs * lg_ref[...] + lb_ref[...]
        s_ref[...] = (ln * _sig(ln)).astype(BF16)

    vec = pl.BlockSpec((1, CC), lambda n: (0, 0))
    blk = pl.BlockSpec((TB, CC), lambda n: (n, 0))
    return dict(
        body=body,
        in_specs=[pl.BlockSpec((TB, 2 * CC), lambda n: (n, C_GLU // (2 * CC))),
                  pl.BlockSpec((32, CC), lambda n: (0, 0)), vec, vec, vec],
        args=[z0, conv_w, conv_b, ln_g, ln_b],
        out_specs=[blk, blk],
        out_shape=[jax.ShapeDtypeStruct((R, CC), F32), jax.ShapeDtypeStruct((R, CC), BF16)],
        scratch=[pltpu.VMEM((8, 2 * TB + 8, CC), F32)])


def _conv_bwd(ds, cv, z0, conv_w, ln_g, ln_b):
    def body(ds_ref, cv_ref, g_ref, w_ref, lg_ref, lb_ref, dglu_ref, dw_ref, dsm_ref, dbuf):
        n = pl.program_id(0)

        @pl.when(n == 0)
        def _():
            dbuf[...] = jnp.zeros_like(dbuf)
            dw_ref[...] = jnp.zeros_like(dw_ref)
            dsm_ref[...] = jnp.zeros_like(dsm_ref)

        cv = cv_ref[...]
        xc = cv - jnp.mean(cv, axis=1, keepdims=True)
        rs = lax.rsqrt(jnp.mean(xc * xc, axis=1, keepdims=True) + LN_EPS)
        xh = xc * rs
        ln = xh * lg_ref[...] + lb_ref[...]
        sg = _sig(ln)
        dln = ds_ref[...] * (sg * (1.0 + ln * (1.0 - sg)))
        dxh = dln * lg_ref[...]
        dcv = rs * (dxh - jnp.mean(dxh, axis=1, keepdims=True) - xh * jnp.mean(dxh * xh, axis=1, keepdims=True))
        dsm_ref[0:1, :] += jnp.sum(dcv, axis=0, keepdims=True)
        dsm_ref[1:2, :] += jnp.sum(dln * xh, axis=0, keepdims=True)
        dsm_ref[2:3, :] += jnp.sum(dln, axis=0, keepdims=True)
        for k in range(8):
            dbuf[k, TB:2 * TB + 8, :] = dbuf[k, 0:TB + 8, :]
            dbuf[k, pl.ds(8 - k, TB), :] = dcv
        a = g_ref[:, 0:CC]
        sb = _sig(g_ref[:, CC:2 * CC])
        u = a * sb
        du = jnp.zeros((TB, CC), F32)
        for w in range(CONV_W):
            off = HALO - w
            sh = dbuf[off % 8, pl.ds(off + 8 - off % 8, TB), :]
            du = du + sh * w_ref[w:w + 1, :]
            dw_ref[w:w + 1, :] += jnp.sum(u * sh, axis=0, keepdims=True)
        dglu_ref[:, 0:CC] = (du * sb).astype(BF16)
        dglu_ref[:, CC:2 * CC] = (du * a * sb * (1.0 - sb)).astype(BF16)

    rev = lambda n: (NB - 1 - n, 0)
    vec = pl.BlockSpec((1, CC), lambda n: (0, 0))
    blk = pl.BlockSpec((TB, CC), rev)
    return dict(
        body=body,
        in_specs=[blk, blk, pl.BlockSpec((TB, 2 * CC), lambda n: (NB - 1 - n, C_GLU // (2 * CC))),
                  pl.BlockSpec((32, CC), lambda n: (0, 0)), vec, vec],
        args=[ds, cv, z0, conv_w, ln_g, ln_b],
        out_specs=[pl.BlockSpec((TB, 2 * CC), rev), pl.BlockSpec((32, CC), lambda n: (0, 0)),
                   pl.BlockSpec((8, CC), lambda n: (0, 0))],
        out_shape=[jax.ShapeDtypeStruct((R, 2 * CC), BF16), jax.ShapeDtypeStruct((32, CC), F32),
                   jax.ShapeDtypeStruct((8, CC), F32)],
        scratch=[pltpu.VMEM((8, 2 * TB + 8, CC), F32)])


def _split_dot(x, t):
    hi = x.astype(BF16)
    lo = (x - hi.astype(F32)).astype(BF16)
    return _dot(hi, t) + _dot(lo, t)


def _stack_heads(x):
    lane = lax.broadcasted_iota(jnp.int32, (TB, 128), 1)
    return jnp.concatenate([jnp.where(lane < HD, x, 0.0), jnp.where(lane < HD, 0.0, x)], axis=0).astype(BF16)


def _sb_stack(qv, i):
    lane2, row2 = _lane_row((2 * TB, 128))
    qpos2 = i * TB + (row2 & (TB - 1))
    lane, row = _lane_row((TB, 128))
    return _stack_heads(qv), lane2, qpos2, (row > lane).astype(BF16)


SB_U = 3
SB_DEAD = -104.0


def _sb_fwd(q, k, v, g):
    def body(q_ref, k_ref, v_ref, g_ref, o_ref, m_ref, c_ref, n_ref):
        p, i = pl.program_id(0), pl.program_id(1)
        lane, row = _lane_row((TB, 128))
        lo = lane < HD
        q2, lane2, qpos2, tri_gt = _sb_stack(q_ref[...].astype(F32) * SCALE, i)

        def cond(st):
            t, _, c2 = st
            return jnp.logical_and(i - SB_U * t >= 0, jnp.max(c2) > SB_DEAD)

        def step(st):
            t, acc, c2 = st
            jrs = [i - SB_U * t - u for u in range(SB_U)]
            j0s = [pl.multiple_of(jnp.maximum(jr, 0) * TB, TB) for jr in jrs]
            ks = [k_ref[pl.ds(j0, TB), :] for j0 in j0s]
            zs = [_dot_nt(q2, kj) for kj in ks]
            valids, lbs, l1s = [], [], []
            for jr, z in zip(jrs, zs):
                kpos = jr * TB + lane2
                valid = (kpos >= PAD) & (kpos < qpos2)
                lb = jnp.minimum(z, 0.0) - jnp.log(1.0 + jnp.exp(-jnp.abs(z)))
                valids.append(valid)
                lbs.append(lb)
                l1s.append(jnp.where(valid, lb - z, 0.0))
            sfxs = [_split_dot(l1, tri_gt) for l1 in l1s]
            carries = []
            for jr, l1 in zip(jrs, l1s):
                carries.append(c2)
                c_ref[...] = jnp.where(lane == 2 * jr, c2[0:TB], jnp.where(lane == 2 * jr + 1, c2[TB:2 * TB], c_ref[...]))
                c2 = c2 + jnp.sum(l1, axis=1, keepdims=True)
            for j0, valid, lb, sfx, cu in zip(j0s, valids, lbs, sfxs, carries):
                a = jnp.where(valid, jnp.exp(lb + sfx + cu), 0.0).astype(BF16)
                av = _dot(a, v_ref[pl.ds(j0, TB), :])
                acc = acc + jnp.where(lo, av[0:TB], av[TB:2 * TB])
            return t + 1, acc, c2

        c_ref[...] = jnp.zeros((TB, 128), F32)
        t, acc, _ = lax.while_loop(cond, step, (jnp.int32(0), jnp.zeros((TB, 128), F32), jnp.zeros((2 * TB, 1), F32)))
        o_ref[...] = acc
        gv = g_ref[...]
        m_ref[...] = (acc * (gv * _sig(gv))).astype(BF16)
        n_ref[p, i] = t

    slab = pl.BlockSpec((R, 128), lambda p, i: (0, p))
    blk = pl.BlockSpec((TB, 128), lambda p, i: (i, p))
    sd = jax.ShapeDtypeStruct((R, D), F32)
    return pl.pallas_call(
        body, grid=(D // 128, NB), in_specs=[blk, slab, slab, blk],
        out_specs=[blk, blk, blk, pl.BlockSpec(memory_space=pltpu.SMEM)],
        out_shape=[sd, jax.ShapeDtypeStruct((R, D), BF16), sd, jax.ShapeDtypeStruct((D // 128, NB), jnp.int32)],
        name="sb_fwd", compiler_params=_cp(("arbitrary", "arbitrary")))(q, k, v, g)


def _sb_bwd(trips, q, k, v, car, dm, g, o):
    def body(n_ref, q_ref, k_ref, v_ref, c_ref, dm_ref, g_ref, o_ref, dq_ref, dko_ref, dvo_ref, dg_ref,
             dk_ref, dv_ref):
        p, i = pl.program_id(0), pl.program_id(1)

        @pl.when(i == 0)
        def _():
            dk_ref[...] = jnp.zeros_like(dk_ref)
            dv_ref[...] = jnp.zeros_like(dv_ref)

        lane, row = _lane_row((TB, 128))
        lo = lane < HD
        tri_lt = (row < lane).astype(BF16)
        q2, lane2, qpos2, tri_gt = _sb_stack(q_ref[...].astype(F32) * SCALE, i)
        gv, dmv = g_ref[...], dm_ref[...]
        sg = _sig(gv)
        dg_ref[...] = (dmv * o_ref[...] * (sg * (1.0 + gv * (1.0 - sg)))).astype(BF16)
        do2 = _stack_heads(dmv * (gv * sg))
        ct = c_ref[...]
        trips_i = n_ref[p, i]
        first = jnp.maximum(i + 1 - SB_U * trips_i, 0)

        def step(t, carry):
            dq, g2 = carry
            jrs = [first + SB_U * t + u for u in range(SB_U)]
            j0s = [pl.multiple_of(jnp.minimum(jr, i) * TB, TB) for jr in jrs]
            ks = [k_ref[pl.ds(j0, TB), :] for j0 in j0s]
            vs = [v_ref[pl.ds(j0, TB), :] for j0 in j0s]
            zs = [_dot_nt(q2, kj) for kj in ks]
            das = [_dot_nt(do2, vj) for vj in vs]
            valids, es, lbs, l1s = [], [], [], []
            for jr, z in zip(jrs, zs):
                kpos = jr * TB + lane2
                valid = (kpos >= PAD) & (kpos < qpos2)
                e = jnp.exp(-jnp.abs(z))
                lb = jnp.minimum(z, 0.0) - jnp.log(1.0 + e)
                valids.append(valid)
                es.append(e)
                lbs.append(lb)
                l1s.append(jnp.where(valid, lb - z, 0.0))
            sfxs = [_split_dot(l1, tri_gt) for l1 in l1s]
            a_s, gmats, gpre = [], [], []
            for jr, valid, lb, sfx, da in zip(jrs, valids, lbs, sfxs, das):
                later = jnp.concatenate(
                    [jnp.sum(jnp.where(lane == 2 * jr + hh, ct, 0.0), axis=1, keepdims=True) for hh in range(2)], axis=0)
                a = jnp.where(valid, jnp.exp(lb + sfx + later), 0.0)
                gmat = da * a
                a_s.append(a.astype(BF16))
                gmats.append(gmat)
                gpre.append(g2)
                g2 = g2 + jnp.sum(gmat, axis=1, keepdims=True)
            pres = [gp + _split_dot(gmat, tri_lt) for gp, gmat in zip(gpre, gmats)]
            for j0, kj, valid, z, e, gmat, pre, a in zip(j0s, ks, valids, zs, es, gmats, pres, a_s):
                r = 1.0 / (1.0 + e)
                big = z >= 0.0
                beta = jnp.where(big, r, e * r)
                omb = jnp.where(big, e * r, r)
                dz = jnp.where(valid, gmat * omb - beta * pre, 0.0).astype(BF16)
                dq2 = _dot(dz, kj)
                dq = dq + jnp.where(lo, dq2[0:TB], dq2[TB:2 * TB])
                dk_ref[pl.ds(j0, TB), :] += _dot_tn(dz, q2)
                dv_ref[pl.ds(j0, TB), :] += _dot_tn(a, do2)
            return dq, g2

        dq, _ = lax.fori_loop(0, trips_i, step, (jnp.zeros((TB, 128), F32), jnp.zeros((2 * TB, 1), F32)))
        dq_ref[...] = (dq * SCALE).astype(BF16)

        @pl.when(i == NB - 1)
        def _():
            dko_ref[...] = dk_ref[...].astype(BF16)
            dvo_ref[...] = dv_ref[...].astype(BF16)

    slab = pl.BlockSpec((R, 128), lambda p, i: (0, p))
    blk = pl.BlockSpec((TB, 128), lambda p, i: (i, p))
    sd = jax.ShapeDtypeStruct((R, D), BF16)
    return pl.pallas_call(
        body, grid=(D // 128, NB),
        in_specs=[pl.BlockSpec(memory_space=pltpu.SMEM), blk, slab, slab, blk, blk, blk, blk],
        out_specs=[blk, slab, slab, blk], out_shape=[sd, sd, sd, sd],
        scratch_shapes=[pltpu.VMEM((R, 128), F32), pltpu.VMEM((R, 128), F32)], name="sb_bwd",
        compiler_params=_cp(("arbitrary", "arbitrary")))(trips, q, k, v, car, dm, g, o)


def _adamw(w, parts, m, v, name):
    rows, cols = w.shape
    tr = 256 if rows % 256 == 0 else rows
    nparts = len(parts)

    def body(*refs):
        w_ref = refs[0]
        p_refs = refs[1:1 + nparts]
        m_ref, v_ref, g_ref, d_ref, nm_ref, nv_ref = refs[1 + nparts:]
        g = p_refs[0][...]
        for p_ref in p_refs[1:]:
            g = g + p_ref[...]
        nm = ADAM_B1 * m_ref[...] + (1.0 - ADAM_B1) * g
        nv = ADAM_B2 * v_ref[...] + (1.0 - ADAM_B2) * (g * g)
        m_hat = nm / (1.0 - ADAM_B1 ** ADAM_STEP)
        v_hat = nv / (1.0 - ADAM_B2 ** ADAM_STEP)
        g_ref[...] = g
        d_ref[...] = -ADAM_LR * (m_hat / (jnp.sqrt(v_hat) + ADAM_EPS) + ADAM_WD * w_ref[...])
        nm_ref[...] = nm
        nv_ref[...] = nv

    blk = pl.BlockSpec((tr, cols), lambda i: (i, 0))
    sd = jax.ShapeDtypeStruct((rows, cols), F32)
    return pl.pallas_call(
        body, grid=(rows // tr,), in_specs=[blk] * (3 + nparts), out_specs=[blk] * 4, out_shape=[sd] * 4,
        name=name, compiler_params=_cp(("parallel",)))(w, *parts, m, v)


def _sum8(buf, name):
    _, rows, cols = buf.shape

    def body(b_ref, o_ref):
        acc = b_ref[0]
        for i in range(1, 8):
            acc = acc + b_ref[i]
        o_ref[...] = acc

    return pl.pallas_call(
        body, out_shape=jax.ShapeDtypeStruct((rows, cols), F32), name=name,
        compiler_params=pltpu.CompilerParams(vmem_limit_bytes=VMEM_LIMIT))(buf)


MESH = pl.DeviceIdType.MESH
ANY = pl.BlockSpec(memory_space=pl.ANY)


def _chip_peers():
    x, y = lax.axis_index("x"), lax.axis_index("y")
    return [(1 - x, y), (x, 1 - y), (1 - x, 1 - y)]


def _gather_chips(shards):
    plan = _gather_plan(shards)

    def body(*refs):
        n = len(shards)
        ins, outs, sems = refs[:n], refs[n:2 * n], refs[2 * n:]
        plan["start"](ins, outs, sems)
        plan["mid"](ins, outs, sems)
        plan["finish"](ins, outs, sems)

    n = len(shards)
    res = pl.pallas_call(
        body, in_specs=[ANY] * n, out_specs=[ANY] * n, out_shape=plan["out_shape"],
        scratch_shapes=plan["sems"], name="gather_chips")(*plan["args"])
    return plan["post"](res)


def _gather_plan(shards):
    n = len(shards)
    shards = [s.reshape((2, s.shape[0] // 2) + s.shape[1:]) for s in shards]

    def copies(kind, ins, outs, sems):
        s1, r1, s2, r2 = sems
        x, y, c = lax.axis_index("x"), lax.axis_index("y"), lax.axis_index("c")
        me = 2 * x + y
        out = []
        for j, (px, py) in enumerate(_chip_peers()):
            for a in range(n):
                k = j * n + a
                got = outs[a].at[2 * px + py].at[c]
                other = outs[a].at[2 * px + py].at[1 - c]
                src, dst, ss, rs, dev = {
                    "first": (ins[a].at[c], outs[a].at[me].at[c], s1, r1, (px, py, c)),
                    "landed": (got, got, s1, r1, (px, py, c)),
                    "passed": (got, got, s2, r2, (x, y, 1 - c)),
                    "theirs": (other, other, s2, r2, (x, y, 1 - c)),
                }[kind]
                out.append(pltpu.make_async_remote_copy(
                    src_ref=src, dst_ref=dst, send_sem=ss.at[k], recv_sem=rs.at[k], device_id=dev, device_id_type=MESH))
        return out

    def start(ins, outs, sems):
        for cp in copies("first", ins, outs, sems):
            cp.start()

    def mid(ins, outs, sems):
        for got, fwd in zip(copies("landed", ins, outs, sems), copies("passed", ins, outs, sems)):
            got.wait_recv()
            fwd.start()

    def finish(ins, outs, sems):
        for cp in copies("theirs", ins, outs, sems):
            cp.wait_recv()
        for cp in copies("first", ins, outs, sems) + copies("passed", ins, outs, sems):
            cp.wait_send()

    def post(res):
        me = 2 * lax.axis_index("x") + lax.axis_index("y")
        res = [lax.dynamic_update_index_in_dim(r, s, me, 0) for r, s in zip(res, shards)]
        return [r.reshape((N_CHIPS, 2 * r.shape[2]) + r.shape[3:]) for r in res]

    return dict(args=shards, out_shape=[jax.ShapeDtypeStruct((N_CHIPS,) + s.shape, s.dtype) for s in shards],
                sems=[pltpu.SemaphoreType.DMA((3 * n,))] * 4, start=start, mid=mid, finish=finish, post=post)


def _rows_call(name, parts, plan):
    n_in = [len(p["args"]) for p in parts]
    n_out = [len(p["out_shape"]) for p in parts]
    n_scr = [len(p["scratch"]) for p in parts]
    c_in, c_out = len(plan["args"]), len(plan["out_shape"])

    def split(refs, sizes):
        out, pos = [], 0
        for k in sizes:
            out.append(refs[pos:pos + k])
            pos += k
        return out

    def body(*refs):
        ins, outs, scr = split(refs, [sum(n_in) + c_in, sum(n_out) + c_out, sum(n_scr) + len(plan["sems"])])
        p_in, p_out, p_scr = split(ins, n_in + [c_in]), split(outs, n_out + [c_out]), split(scr, n_scr + [len(plan["sems"])])
        comm = (p_in[-1], p_out[-1], p_scr[-1])
        step = pl.program_id(0)

        @pl.when(step == 0)
        def _():
            plan["start"](*comm)

        for p, i, o, s in zip(parts, p_in, p_out, p_scr):
            p["body"](*i, *o, *s)

        @pl.when(step == NB - 2)
        def _():
            plan["mid"](*comm)

        @pl.when(step == NB - 1)
        def _():
            plan["finish"](*comm)

    flat = lambda key: [v for p in parts for v in p[key]]
    res = pl.pallas_call(
        body, grid=(NB,), in_specs=flat("in_specs") + [ANY] * c_in, out_specs=flat("out_specs") + [ANY] * c_out,
        out_shape=flat("out_shape") + plan["out_shape"], scratch_shapes=flat("scratch") + plan["sems"],
        name=name, compiler_params=_cp(("arbitrary",)))(*flat("args"), *plan["args"])
    outs = split(res, n_out + [c_out])
    return outs[:-1], outs[-1]


def _pair_exchange(grads, name):
    n = len(grads)
    hs = [g.shape[1] // 2 for g in grads]
    grads = [g.reshape((N_CHIPS, 2, h) + g.shape[2:]) for g, h in zip(grads, hs)]

    def body(*refs):
        ins, got = refs[:n], refs[n:2 * n]
        ssem, rsem = refs[2 * n:]
        x, y, c = lax.axis_index("x"), lax.axis_index("y"), lax.axis_index("c")
        sends = [pltpu.make_async_remote_copy(
            src_ref=ins[a].at[:, 1 - c], dst_ref=got[a], send_sem=ssem.at[a],
            recv_sem=rsem.at[a], device_id=(x, y, 1 - c), device_id_type=MESH) for a in range(n)]
        for cp in sends:
            cp.start()
        for cp in sends:
            cp.wait()

    half_shapes = [jax.ShapeDtypeStruct((N_CHIPS, h) + g.shape[3:], g.dtype) for g, h in zip(grads, hs)]
    got = pl.pallas_call(
        body, in_specs=[ANY] * n, out_specs=[ANY] * n, out_shape=half_shapes,
        scratch_shapes=[pltpu.SemaphoreType.DMA((n,))] * 2, name=name)(*grads)
    c = lax.axis_index("c")
    own = [lax.dynamic_index_in_dim(g, c, 1, keepdims=False) for g in grads]
    return own, got


def _sum_pair(own, got, send_dtype, name):
    _, rows, cols = own.shape
    tr = 256 if rows % 256 == 0 else rows

    def body(a_ref, b_ref, f_ref, s_ref):
        t = a_ref[...].astype(F32) + b_ref[...].astype(F32)
        f_ref[...] = t
        s_ref[...] = t.astype(send_dtype)

    blk = pl.BlockSpec((N_CHIPS, tr, cols), lambda i: (0, i, 0))
    return pl.pallas_call(
        body, grid=(rows // tr,), in_specs=[blk, blk], out_specs=[blk, blk],
        out_shape=[jax.ShapeDtypeStruct(own.shape, F32), jax.ShapeDtypeStruct(own.shape, send_dtype)],
        name=name, compiler_params=_cp(("parallel",)))(own, got)


def _scatter_chips(keep, send):
    n = len(send)
    plan = _scatter_plan(send)

    def body(*refs):
        sin, land, sems = refs[:n], refs[n:2 * n], refs[2 * n:]
        plan["start"](sin, land, sems)
        plan["finish"](sin, land, sems)

    land = pl.pallas_call(
        body, in_specs=[ANY] * n, out_specs=[ANY] * n, out_shape=plan["out_shape"],
        scratch_shapes=plan["sems"], name="scatter_chips")(*send)
    return _own_slab(keep), land


def _own_slab(keep):
    me = 2 * lax.axis_index("x") + lax.axis_index("y")
    return [lax.dynamic_index_in_dim(k, me, 0, keepdims=False) for k in keep]


def _scatter_plan(send):
    n = len(send)

    def copies(sin, land, sems):
        ssem, rsem = sems
        c = lax.axis_index("c")
        return [pltpu.make_async_remote_copy(
            src_ref=sin[a].at[2 * px + py], dst_ref=land[a].at[j], send_sem=ssem.at[j * n + a],
            recv_sem=rsem.at[j * n + a], device_id=(px, py, c), device_id_type=MESH)
            for j, (px, py) in enumerate(_chip_peers()) for a in range(n)]

    def start(sin, land, sems):
        for cp in copies(sin, land, sems):
            cp.start()

    def finish(sin, land, sems):
        for cp in copies(sin, land, sems):
            cp.wait()

    return dict(args=list(send), out_shape=[jax.ShapeDtypeStruct((3,) + s.shape[1:], s.dtype) for s in send],
                sems=[pltpu.SemaphoreType.DMA((3 * n,))] * 2, start=start, mid=lambda *a: None, finish=finish)


def _sum_shard(mine, land, name):
    rows, cols = mine.shape
    tr = 256 if rows % 256 == 0 else rows

    def body(m_ref, l_ref, o_ref):
        o_ref[...] = ((m_ref[...] + l_ref[0].astype(F32)) + l_ref[1].astype(F32)) + l_ref[2].astype(F32)

    return pl.pallas_call(
        body, grid=(rows // tr,),
        in_specs=[pl.BlockSpec((tr, cols), lambda i: (i, 0)), pl.BlockSpec((3, tr, cols), lambda i: (0, i, 0))],
        out_specs=pl.BlockSpec((tr, cols), lambda i: (i, 0)), out_shape=jax.ShapeDtypeStruct((rows, cols), F32),
        name=name, compiler_params=_cp(("parallel",)))(mine, land)


def _join_cores(halves):
    n = len(halves)

    def body(*refs):
        ins, outs = refs[:n], refs[n:2 * n]
        ssem, rsem = refs[2 * n:]
        x, y, c = lax.axis_index("x"), lax.axis_index("y"), lax.axis_index("c")
        sends = [pltpu.make_async_remote_copy(
            src_ref=ins[a], dst_ref=outs[a].at[c], send_sem=ssem.at[a], recv_sem=rsem.at[a],
            device_id=(x, y, 1 - c), device_id_type=MESH) for a in range(n)]
        for cp in sends:
            cp.start()
        for a in range(n):
            sends[a].wait_send()
            pltpu.make_async_remote_copy(
                src_ref=ins[a], dst_ref=outs[a].at[1 - c], send_sem=ssem.at[a], recv_sem=rsem.at[a],
                device_id=(x, y, 1 - c), device_id_type=MESH).wait_recv()

    res = pl.pallas_call(
        body, in_specs=[ANY] * n, out_specs=[ANY] * n,
        out_shape=[jax.ShapeDtypeStruct((2,) + h.shape, h.dtype) for h in halves],
        scratch_shapes=[pltpu.SemaphoreType.DMA((n,))] * 2, name="join_cores")(*halves)
    c = lax.axis_index("c")
    res = [lax.dynamic_update_index_in_dim(r, h, c, 0) for r, h in zip(res, halves)]
    return [r.reshape((2 * r.shape[1],) + r.shape[2:]) for r in res]


def _gather_all(vec):
    def body(v_ref, o_ref, lsem, ssem, rsem):
        x, y, c = lax.axis_index("x"), lax.axis_index("y"), lax.axis_index("c")
        me = 4 * x + 2 * y + c
        local = pltpu.make_async_copy(v_ref, o_ref.at[me], lsem)
        local.start()
        cps = []
        for k in range(1, 8):
            px, py, pc = x ^ (k >> 2), y ^ ((k >> 1) & 1), c ^ (k & 1)
            cps.append(pltpu.make_async_remote_copy(
                src_ref=v_ref, dst_ref=o_ref.at[me], send_sem=ssem.at[k - 1], recv_sem=rsem.at[k - 1],
                device_id=(px, py, pc), device_id_type=MESH))
        for cp in cps:
            cp.start()
        for k in range(1, 8):
            px, py, pc = x ^ (k >> 2), y ^ ((k >> 1) & 1), c ^ (k & 1)
            pltpu.make_async_remote_copy(
                src_ref=v_ref, dst_ref=o_ref.at[4 * px + 2 * py + pc], send_sem=ssem.at[k - 1],
                recv_sem=rsem.at[k - 1], device_id=(px, py, pc), device_id_type=MESH).wait_recv()
        for cp in cps:
            cp.wait_send()
        local.wait()

    return pl.pallas_call(
        body, in_specs=[ANY], out_specs=ANY, out_shape=jax.ShapeDtypeStruct((8,) + vec.shape, vec.dtype),
        scratch_shapes=[pltpu.SemaphoreType.DMA, pltpu.SemaphoreType.DMA((7,)), pltpu.SemaphoreType.DMA((7,))],
        name="gather_all")(vec)


def _rope_tables():
    pos = (jnp.arange(R, dtype=jnp.int32) - PAD).astype(F32)
    half = HD // 2
    inv = ROPE_THETA ** (-jnp.arange(half, dtype=F32) / half)
    ang = pos[:, None] * inv[None, :]
    cos, sin = jnp.cos(ang), jnp.sin(ang)
    cs = jnp.tile(cos, (1, 4))
    sn = jnp.tile(jnp.concatenate([-sin, sin], axis=1), (1, 2))
    return cs, sn


def _perm_cols(w):
    return jnp.concatenate([w[:, 0:512], w[:, 768:1280], w[:, 1280:2304], w[:, 2304:2816], w[:, 512:640],
                            w[:, 640:768]], axis=1)


def _unperm_cols(w):
    return jnp.concatenate([w[:, C_Q:C_Q + 512], w[:, C_K:C_K + 128], w[:, C_V:C_V + 128], w[:, C_GA:C_GA + 512],
                            w[:, C_GLU:C_GLU + 1024], w[:, C_GB:C_GB + 512]], axis=1)


def _local_step(x, target, p):
    w0 = _perm_cols(p["ab_w_in"])
    conv_w = jnp.concatenate([p["ab_conv_w"], jnp.zeros((1, CC), F32)], axis=0)
    cs, sn = _rope_tables()

    h0 = jnp.concatenate([jnp.zeros((PAD, D), F32), p["meta_tokens"], x], axis=0)

    xn0 = _rms_fwd(h0, p["ab_pre_norm"], "rms_fwd0")
    z0 = _mm([(xn0, w0)], F32, "in_proj0", 544, 1408)
    plan = _gather_plan([p["sb_w_in"], p["sb_w_out"], p["ab_w_out"], p["ab_w_pw2"]])
    ((o0, a0, lse0), (cv0, s0)), gathered = _rows_call(
        "fwd0", [_swa_fwd(z0, cs, sn, p["ab_sinks"]),
                 _conv_fwd(z0, conv_w, p["ab_conv_b"], p["ab_conv_ln_g"], p["ab_conv_ln_b"])], plan)
    w1, wo1, wo0, wpw = plan["post"](gathered)
    wo1, wo0, wpw = wo1.reshape(D, D), wo0.reshape(D, D), wpw.reshape(CC, CC)
    t0 = _mm([(s0, wpw)], F32, "pw2", 544, 512)
    c0 = _gate_fwd(t0, z0, C_GB, "gate_b_fwd")
    wo0h = wo0.reshape(2, CC, D)
    y0 = _mm([(a0, (wo0h, 0)), (c0, (wo0h, 1))], F32, "out_proj0", 544, 1024)

    h1, xn1 = _post_rms_fwd(h0, y0, p["ab_post_norm"], p["sb_pre_norm"], "post_rms_fwd")
    q1 =_mm([(xn1, (w1, 0))], BF16, "in_proj1_q", 544, 1024)
    k1 = _mm([(xn1, (w1, 1))], BF16, "in_proj1_k", 544, 1024)
    v1 = _mm([(xn1, (w1, 2))], BF16, "in_proj1_v", 544, 1024)
    g1 = _mm([(xn1, (w1, 3))], F32, "in_proj1_g", 544, 1024)
    o1, m1, car1, trips1 = _sb_fwd(q1, k1, v1, g1)
    y1 = _mm([(m1, wo1)], F32, "out_proj1", 544, 1024)

    dh2, dy1, d_sb_post, loss_row = _tail(h1, y1, p["sb_post_norm"], target)

    dm1 = _mm([(dy1, wo1)], F32, "out_proj1_dx", 544, 1024, tb=True)
    d_wo1 = _mm([(m1, dy1)], BF16, "out_proj1_dw", 512, 1024, ta=True)
    dq1, dk1, dv1, dg1 = _sb_bwd(trips1, q1, k1, v1, car1, dm1, g1, o1)
    dz1 = [dq1, dk1, dv1, dg1]
    dxn1 = _mm([(dz1[j], (w1, j)) for j in range(4)], F32, "in_proj1_dx", 544, 1024, tb=True)
    d_w1 = jnp.stack([_mm([(xn1, dz1[j])], BF16, "in_proj1_dw%d" % j, 512, 1024, ta=True) for j in range(4)])

    dh1, d_sb_pre, dy0, d_ab_post = _rms_post_bwd(dxn1, h1, p["sb_pre_norm"], dh2, y0, p["ab_post_norm"],
                                                  "rms_post_bwd")
    dmix0 = _mm([(dy0, wo0)], F32, "out_proj0_dx", 544, 1024, tb=True)
    d_wo0 = jnp.concatenate([_mm([(a0, dy0)], BF16, "out_proj0_dw_a", 512, 1024, ta=True),
                             _mm([(c0, dy0)], BF16, "out_proj0_dw_b", 512, 1024, ta=True)], axis=0)
    dt0, dgb0 = _gate_bwd(dmix0, 512, t0, z0, C_GB, "gate_b_bwd")
    ds0 = _mm([(dt0, wpw)], F32, "pw2_dx", 544, 512, tb=True)
    d_wpw = _mm([(s0, dt0)], BF16, "pw2_dw", 512, 512, ta=True)
    early = ("sb_w_in", "sb_w_out", "ab_w_out", "ab_w_pw2")
    own1, got1 = _pair_exchange([d_w1, d_wo1.reshape(N_CHIPS, 256, D), d_wo0.reshape(N_CHIPS, 256, D),
                                 d_wpw.reshape(N_CHIPS, 128, CC)], "pair_exchange1")
    pair1 = [_sum_pair(o, t, BF16, "sum_pair_" + nm) for o, t, nm in zip(own1, got1, early)]
    plan = _scatter_plan([pr[1] for pr in pair1])
    ((dglu0, d_convw, d_small), (dq0, dga0, dkv0, d_sinks)), land1 = _rows_call(
        "bwd0", [_conv_bwd(ds0, cv0, z0, conv_w, p["ab_conv_ln_g"], p["ab_conv_ln_b"]),
                 _swa_bwd(z0, cs, sn, p["ab_sinks"], o0, dmix0, lse0)], plan)
    halves1 = [_sum_shard(mi, la, "sum_shard_" + nm)
               for mi, la, nm in zip(_own_slab([pr[0] for pr in pair1]), land1, early)]
    dz0 = jnp.concatenate([dq0, dga0, dglu0, dgb0, dkv0], axis=1)
    dxn0 = _mm([(dz0, w0)], F32, "in_proj0_dx", 544, 1024, tb=True)
    d_w0 = _unperm_cols(_mm([(xn0, dz0)], BF16, "in_proj0_dw", 512, 1408, ta=True))
    dh0_first, grad_x, d_ab_pre = _rms_bwd(dxn0, h0, p["ab_pre_norm"], dh1, F32, "rms_bwd0", split=True)

    grads = {
        "meta_tokens": dh0_first[PAD:TB], "ab_pre_norm": d_ab_pre, "ab_w_in": d_w0, "ab_sinks": d_sinks[0:1, 0:8],
        "ab_conv_w": d_convw[0:CONV_W], "ab_conv_b": d_small[0:1], "ab_conv_ln_g": d_small[1:2],
        "ab_conv_ln_b": d_small[2:3], "ab_w_pw2": d_wpw, "ab_w_out": d_wo0, "ab_post_norm": d_ab_post,
        "sb_pre_norm": d_sb_pre, "sb_post_norm": d_sb_post,
    }
    return loss_row, grad_x, grads, halves1


SMALL_ROWS = 80
REP_ROWS = 32

WEIGHTS = ["meta_tokens", "ab_pre_norm", "ab_w_in", "ab_sinks", "ab_conv_w", "ab_conv_b", "ab_conv_ln_g",
           "ab_conv_ln_b", "ab_w_pw2", "ab_w_out", "ab_post_norm", "sb_pre_norm", "sb_w_in", "sb_w_out",
           "sb_post_norm"]
BIG = ["ab_w_in", "ab_w_out", "ab_w_pw2", "sb_w_in", "sb_w_out"]


def _pack_small(conv_w, meta, sb_pre, sb_post):
    rows = jnp.concatenate([conv_w, meta.reshape(32, 128), sb_pre.reshape(2, 128), sb_post.reshape(2, 128)], axis=0)
    return jnp.concatenate([rows, jnp.zeros((SMALL_ROWS - rows.shape[0], 128), F32)], axis=0)


def _unpack_small(s):
    return s[0:31], s[31:63].reshape(16, 256), s[63:65].reshape(1, 256), s[65:67].reshape(1, 256)


REP_LOSS = 3592


def _pack_rep(pre, post, conv_b, ln_g, ln_b, sinks, extra=None):
    flat = jnp.concatenate([pre.reshape(-1), post.reshape(-1), conv_b.reshape(-1), ln_g.reshape(-1),
                            ln_b.reshape(-1), sinks.reshape(-1)] + ([] if extra is None else [extra.reshape(-1)]))
    flat = jnp.concatenate([flat, jnp.zeros((REP_ROWS * 128 - flat.shape[0],), F32)])
    return flat.reshape(REP_ROWS, 128)


def _unpack_rep(r):
    f = r.reshape(-1)
    return (f[0:1024].reshape(1, 1024), f[1024:2048].reshape(1, 1024), f[2048:2560].reshape(1, 512),
            f[2560:3072].reshape(1, 512), f[3072:3584].reshape(1, 512), f[3584:3592].reshape(1, 8))


def _cols_to_chips(w, width):
    return w.reshape(w.shape[0], N_CHIPS, width).transpose(1, 0, 2)


def _chips_to_cols(w):
    return w.transpose(1, 0, 2).reshape(w.shape[1], -1)


def kernel(x, meta_tokens, ab_pre_norm, ab_w_in, ab_sinks, ab_conv_w, ab_conv_b, ab_conv_ln_g, ab_conv_ln_b, ab_w_pw2, ab_w_out, ab_post_norm, sb_pre_norm, sb_w_in, sb_w_out, sb_post_norm, loss_target, m_meta_tokens, m_ab_pre_norm, m_ab_w_in, m_ab_sinks, m_ab_conv_w, m_ab_conv_b, m_ab_conv_ln_g, m_ab_conv_ln_b, m_ab_w_pw2, m_ab_w_out, m_ab_post_norm, m_sb_pre_norm, m_sb_w_in, m_sb_w_out, m_sb_post_norm, v_meta_tokens, v_ab_pre_norm, v_ab_w_in, v_ab_sinks, v_ab_conv_w, v_ab_conv_b, v_ab_conv_ln_g, v_ab_conv_ln_b, v_ab_w_pw2, v_ab_w_out, v_ab_post_norm, v_sb_pre_norm, v_sb_w_in, v_sb_w_out, v_sb_post_norm):
    w = dict(meta_tokens=meta_tokens, ab_pre_norm=ab_pre_norm, ab_w_in=ab_w_in, ab_sinks=ab_sinks,
             ab_conv_w=ab_conv_w, ab_conv_b=ab_conv_b, ab_conv_ln_g=ab_conv_ln_g, ab_conv_ln_b=ab_conv_ln_b,
             ab_w_pw2=ab_w_pw2, ab_w_out=ab_w_out, ab_post_norm=ab_post_norm, sb_pre_norm=sb_pre_norm,
             sb_w_in=sb_w_in, sb_w_out=sb_w_out, sb_post_norm=sb_post_norm)
    m = dict(meta_tokens=m_meta_tokens, ab_pre_norm=m_ab_pre_norm, ab_w_in=m_ab_w_in, ab_sinks=m_ab_sinks,
             ab_conv_w=m_ab_conv_w, ab_conv_b=m_ab_conv_b, ab_conv_ln_g=m_ab_conv_ln_g,
             ab_conv_ln_b=m_ab_conv_ln_b, ab_w_pw2=m_ab_w_pw2, ab_w_out=m_ab_w_out, ab_post_norm=m_ab_post_norm,
             sb_pre_norm=m_sb_pre_norm, sb_w_in=m_sb_w_in, sb_w_out=m_sb_w_out, sb_post_norm=m_sb_post_norm)
    v = dict(meta_tokens=v_meta_tokens, ab_pre_norm=v_ab_pre_norm, ab_w_in=v_ab_w_in, ab_sinks=v_ab_sinks,
             ab_conv_w=v_ab_conv_w, ab_conv_b=v_ab_conv_b, ab_conv_ln_g=v_ab_conv_ln_g,
             ab_conv_ln_b=v_ab_conv_ln_b, ab_w_pw2=v_ab_w_pw2, ab_w_out=v_ab_w_out, ab_post_norm=v_ab_post_norm,
             sb_pre_norm=v_sb_pre_norm, sb_w_in=v_sb_w_in, sb_w_out=v_sb_w_out, sb_post_norm=v_sb_post_norm)

    def small_of(d):
        return _pack_small(d["ab_conv_w"][0], d["meta_tokens"], d["sb_pre_norm"], d["sb_post_norm"])

    def rep_of(d):
        return _pack_rep(d["ab_pre_norm"], d["ab_post_norm"], d["ab_conv_b"], d["ab_conv_ln_g"], d["ab_conv_ln_b"],
                         d["ab_sinks"])

    g_in0, g_small = _gather_chips([ab_w_in[0].astype(BF16), small_of(w)])
    conv_w_f = _chips_to_cols(g_small[:, 0:31])
    meta_f = _chips_to_cols(g_small[:, 31:63].reshape(N_CHIPS, 16, 256))
    sb_pre_f = g_small[:, 63:65].reshape(1, D)
    sb_post_f = g_small[:, 65:67].reshape(1, D)
    full = {
        "meta_tokens": meta_f, "ab_pre_norm": ab_pre_norm, "ab_w_in": _chips_to_cols(g_in0),
        "ab_sinks": ab_sinks, "ab_conv_w": conv_w_f, "ab_conv_b": ab_conv_b, "ab_conv_ln_g": ab_conv_ln_g,
        "ab_conv_ln_b": ab_conv_ln_b, "ab_w_pw2": ab_w_pw2[0].astype(BF16), "ab_w_out": ab_w_out[0].astype(BF16),
        "ab_post_norm": ab_post_norm, "sb_pre_norm": sb_pre_f, "sb_w_in": sb_w_in[0].astype(BF16),
        "sb_w_out": sb_w_out[0].astype(BF16), "sb_post_norm": sb_post_f,
    }

    loss_row, grad_x, g, halves1 = _local_step(x[0], loss_target[0], full)

    send = [_cols_to_chips(g["ab_w_in"], 704)]
    gs_conv = _cols_to_chips(g["ab_conv_w"], 128)
    gs_meta = _cols_to_chips(g["meta_tokens"], 256)
    gs_pre = g["sb_pre_norm"].reshape(N_CHIPS, 1, 256)
    gs_post = g["sb_post_norm"].reshape(N_CHIPS, 1, 256)
    send.append(jnp.stack([_pack_small(gs_conv[j], gs_meta[j], gs_pre[j], gs_post[j]) for j in range(N_CHIPS)]))
    names = ["ab_w_in", "small"]
    own, got = _pair_exchange(send, "pair_exchange0")
    pair = [_sum_pair(o, t, o.dtype, "sum_pair_" + nm) for o, t, nm in zip(own, got, names)]
    mine, land = _scatter_chips([pr[0] for pr in pair], [pr[1] for pr in pair])
    halves = [_sum_shard(mi, la, "sum_shard_" + nm) for mi, la, nm in zip(mine, land, names)]
    h_sb_in, h_sb_out, h_ab_out, h_pw2 = halves1
    total = _join_cores([halves[0], h_ab_out, h_pw2, h_sb_in, h_sb_out, halves[1]])

    rep_g = _pack_rep(g["ab_pre_norm"], g["ab_post_norm"], g["ab_conv_b"], g["ab_conv_ln_g"], g["ab_conv_ln_b"],
                      g["ab_sinks"], loss_row[0:1, 0:1])
    rep_sum = _sum8(_gather_all(rep_g), "sum8_rep")
    loss = rep_sum.reshape(-1)[REP_LOSS]

    out_g, out_d, out_m, out_v = {}, {}, {}, {}
    for i, k in enumerate(BIG):
        shp = w[k].shape
        res = _adamw(w[k][0], [total[i]], m[k][0], v[k][0], "adamw_" + k)
        out_g[k], out_d[k], out_m[k], out_v[k] = [r.reshape(shp) for r in res]
    res = _adamw(small_of(w), [total[5]], small_of(m), small_of(v), "adamw_small")
    for dst, r in zip((out_g, out_d, out_m, out_v), res):
        cw, mt, pre, post = _unpack_small(r)
        dst["ab_conv_w"], dst["meta_tokens"], dst["sb_pre_norm"], dst["sb_post_norm"] = cw[None], mt, pre, post
    res = _adamw(rep_of(w), [rep_sum], rep_of(m), rep_of(v), "adamw_rep")
    for dst, r in zip((out_g, out_d, out_m, out_v), res):
        (dst["ab_pre_norm"], dst["ab_post_norm"], dst["ab_conv_b"], dst["ab_conv_ln_g"], dst["ab_conv_ln_b"],
         dst["ab_sinks"]) = _unpack_rep(r)

    return (loss, grad_x[None], *[out_g[k] for k in WEIGHTS], *[out_d[k] for k in WEIGHTS],
            *[out_m[k] for k in WEIGHTS], *[out_v[k] for k in WEIGHTS])
```

```python
import functools

import jax
import jax.numpy as jnp
from jax import lax
from jax.experimental import pallas as pl
from jax.experimental.pallas import tpu as pltpu

F32 = jnp.float32
BF16 = jnp.bfloat16

D = 1024
SEQ = 2048
N_META = 16
TB = 128
TR = 272
PAD = TB - N_META
R = SEQ + TB
NB = R // TB
HD = 64
ROPE_THETA = 10000.0
NORM_EPS = 1e-6
LN_EPS = 1e-5
NEG = -1e30
SWA_HEADS = 8
CONV_W = 31
SCALE = HD ** -0.5
N_CHIPS = 4

C_Q, C_GA, C_GLU, C_GB, C_K, C_V = 0, 512, 1024, 2048, 2560, 2688
AB_IN = 2816

ADAM_LR, ADAM_B1, ADAM_B2, ADAM_EPS, ADAM_WD, ADAM_STEP = 0.001, 0.9, 0.999, 1e-08, 0.01, 10

VMEM_LIMIT = 56 * 1024 * 1024


def _cp(sem):
    return pltpu.CompilerParams(dimension_semantics=sem, vmem_limit_bytes=VMEM_LIMIT)


def _sig(x):
    return 1.0 / (1.0 + jnp.exp(-x))


def _dot(a, b):
    return lax.dot_general(a, b, (((1,), (0,)), ((), ())), preferred_element_type=F32)


def _dot_nt(a, b):
    return lax.dot_general(a, b, (((1,), (1,)), ((), ())), preferred_element_type=F32)


def _dot_tn(a, b):
    return lax.dot_general(a, b, (((0,), (0,)), ((), ())), preferred_element_type=F32)


def _mm(pairs, out_dtype, name, tm, tn, ta=False, tb=False):
    pairs = [(a, b if isinstance(b, tuple) else (b, None)) for a, b in pairs]
    a0, (b0, _) = pairs[0]
    m = a0.shape[1] if ta else a0.shape[0]
    n = b0.shape[-2] if tb else b0.shape[-1]
    npairs = len(pairs)
    dims = (((0 if ta else 1,), (1 if tb else 0,)), ((), ()))

    def body(*refs):
        o_ref = refs[2 * npairs]
        acc = None
        for i in range(npairs):
            t = lax.dot_general(refs[2 * i][...].astype(BF16), refs[2 * i + 1][...].astype(BF16), dims,
                                preferred_element_type=F32)
            acc = t if acc is None else acc + t
        o_ref[...] = acc.astype(out_dtype)

    in_specs, args = [], []
    for a, (b, sel) in pairs:
        k = a.shape[0] if ta else a.shape[1]
        in_specs.append(pl.BlockSpec((k, tm), lambda i, j: (0, i)) if ta else pl.BlockSpec((tm, k), lambda i, j: (i, 0)))
        bshape, bidx = ((tn, k), lambda i, j: (j, 0)) if tb else ((k, tn), lambda i, j: (0, j))
        if sel is None:
            in_specs.append(pl.BlockSpec(bshape, bidx))
        else:
            in_specs.append(pl.BlockSpec((None,) + bshape, functools.partial(lambda i, j, f, s: (s,) + f(i, j), f=bidx, s=sel)))
        args += [a, b]
    return pl.pallas_call(
        body, grid=(m // tm, n // tn), in_specs=in_specs,
        out_specs=pl.BlockSpec((tm, tn), lambda i, j: (i, j)),
        out_shape=jax.ShapeDtypeStruct((m, n), out_dtype), name=name,
        compiler_params=_cp(("parallel", "parallel")))(*args)


def _rms_fwd(h, g, name):
    def body(h_ref, g_ref, o_ref):
        x = h_ref[...]
        r = lax.rsqrt(jnp.mean(x * x, axis=1, keepdims=True) + NORM_EPS)
        o_ref[...] = (x * r * g_ref[...]).astype(BF16)

    return pl.pallas_call(
        body, grid=(R // TR,),
        in_specs=[pl.BlockSpec((TR, D), lambda n: (n, 0)), pl.BlockSpec((1, D), lambda n: (0, 0))],
        out_specs=pl.BlockSpec((TR, D), lambda n: (n, 0)),
        out_shape=jax.ShapeDtypeStruct((R, D), BF16), name=name, compiler_params=_cp(("parallel",)))(h, g)


def _rms_bwd(dout, x, g, res, out_dtype, name, split=False):
    has_res = res is not None

    def body(*refs):
        if split:
            refs = list(refs)
            dx_rest_ref = refs.pop(-2)
        if has_res:
            d_ref, x_ref, g_ref, r_ref, dx_ref, dg_ref = refs
        else:
            d_ref, x_ref, g_ref, dx_ref, dg_ref = refs
        n = pl.program_id(0)
        xv = x_ref[...]
        dv = d_ref[...]
        r = lax.rsqrt(jnp.mean(xv * xv, axis=1, keepdims=True) + NORM_EPS)
        xh = xv * r
        dxh = dv * g_ref[...]
        dx = r * (dxh - xh * jnp.mean(dxh * xh, axis=1, keepdims=True))
        if has_res:
            dx = dx + r_ref[...]
        row = lax.broadcasted_iota(jnp.int32, (TB, D), 0) + n * TB
        dx = jnp.where(row >= PAD, dx, 0.0).astype(out_dtype)
        if split:
            @pl.when(n == 0)
            def _():
                dx_ref[...] = dx

            @pl.when(n > 0)
            def _():
                dx_rest_ref[...] = dx
        else:
            dx_ref[...] = dx

        @pl.when(n == 0)
        def _():
            dg_ref[...] = jnp.zeros_like(dg_ref)

        dg_ref[...] += jnp.sum(dv * xh, axis=0, keepdims=True)

    blk = pl.BlockSpec((TB, D), lambda n: (n, 0))
    vec = pl.BlockSpec((1, D), lambda n: (0, 0))
    ins = [dout, x, g] + ([res] if has_res else [])
    in_specs = [blk, blk, vec] + ([blk] if has_res else [])
    if split:
        out_specs = [pl.BlockSpec((TB, D), lambda n: (0, 0)), pl.BlockSpec((TB, D), lambda n: (jnp.maximum(n - 1, 0), 0)), vec]
        out_shape = [jax.ShapeDtypeStruct((TB, D), out_dtype), jax.ShapeDtypeStruct((SEQ, D), out_dtype),
                     jax.ShapeDtypeStruct((1, D), F32)]
    else:
        out_specs = [blk, vec]
        out_shape = [jax.ShapeDtypeStruct((R, D), out_dtype), jax.ShapeDtypeStruct((1, D), F32)]
    return pl.pallas_call(
        body, grid=(NB,), in_specs=in_specs, out_specs=out_specs, out_shape=out_shape,
        name=name, compiler_params=_cp(("arbitrary",)))(*ins)


def _post_rms_fwd(h, y, g_post, g_next, name):
    def body(h_ref, y_ref, gp_ref, gn_ref, o_ref, x_ref):
        yv = y_ref[...]
        r = lax.rsqrt(jnp.mean(yv * yv, axis=1, keepdims=True) + NORM_EPS)
        hn = h_ref[...] + yv * r * gp_ref[...]
        o_ref[...] = hn
        r2 = lax.rsqrt(jnp.mean(hn * hn, axis=1, keepdims=True) + NORM_EPS)
        x_ref[...] = (hn * r2 * gn_ref[...]).astype(BF16)

    blk = pl.BlockSpec((TR, D), lambda n: (n, 0))
    vec = pl.BlockSpec((1, D), lambda n: (0, 0))
    return pl.pallas_call(
        body, grid=(R // TR,), in_specs=[blk, blk, vec, vec], out_specs=[blk, blk],
        out_shape=[jax.ShapeDtypeStruct((R, D), F32), jax.ShapeDtypeStruct((R, D), BF16)],
        name=name, compiler_params=_cp(("parallel",)))(h, y, g_post, g_next)


def _rms_post_bwd(dxn, h, g, res, y, g_post, name):
    def body(d_ref, h_ref, g_ref, r_ref, y_ref, gp_ref, dh_ref, dg_ref, dy_ref, dgp_ref):
        n = pl.program_id(0)

        @pl.when(n == 0)
        def _():
            dg_ref[...] = jnp.zeros_like(dg_ref)
            dgp_ref[...] = jnp.zeros_like(dgp_ref)

        hv, dv = h_ref[...], d_ref[...]
        r = lax.rsqrt(jnp.mean(hv * hv, axis=1, keepdims=True) + NORM_EPS)
        xh = hv * r
        dxh = dv * g_ref[...]
        dh = r * (dxh - xh * jnp.mean(dxh * xh, axis=1, keepdims=True)) + r_ref[...]
        row = lax.broadcasted_iota(jnp.int32, (TR, D), 0) + n * TR
        dh = jnp.where(row >= PAD, dh, 0.0)
        dh_ref[...] = dh
        dg_ref[...] += jnp.sum(dv * xh, axis=0, keepdims=True)
        yv = y_ref[...]
        ry = lax.rsqrt(jnp.mean(yv * yv, axis=1, keepdims=True) + NORM_EPS)
        yh = yv * ry
        dyh = dh * gp_ref[...]
        dy_ref[...] = (ry * (dyh - yh * jnp.mean(dyh * yh, axis=1, keepdims=True))).astype(BF16)
        dgp_ref[...] += jnp.sum(dh * yh, axis=0, keepdims=True)

    blk = pl.BlockSpec((TR, D), lambda n: (n, 0))
    vec = pl.BlockSpec((1, D), lambda n: (0, 0))
    return pl.pallas_call(
        body, grid=(R // TR,), in_specs=[blk, blk, vec, blk, blk, vec], out_specs=[blk, vec, blk, vec],
        out_shape=[jax.ShapeDtypeStruct((R, D), F32), jax.ShapeDtypeStruct((1, D), F32),
                   jax.ShapeDtypeStruct((R, D), BF16), jax.ShapeDtypeStruct((1, D), F32)],
        name=name, compiler_params=_cp(("arbitrary",)))(dxn, h, g, res, y, g_post)


GW = 512


def _gate_fwd(o, gsrc, goff, name):
    w = o.shape[1]

    def body(o_ref, g_ref, m_ref):
        gv = g_ref[...]
        m_ref[...] = (o_ref[...] * (gv * _sig(gv))).astype(BF16)

    gb = goff // GW
    return pl.pallas_call(
        body, grid=(R // TR, w // GW),
        in_specs=[pl.BlockSpec((TR, GW), lambda n, j: (n, j)), pl.BlockSpec((TR, GW), lambda n, j: (n, gb + j))],
        out_specs=pl.BlockSpec((TR, GW), lambda n, j: (n, j)),
        out_shape=jax.ShapeDtypeStruct((R, w), BF16), name=name,
        compiler_params=_cp(("parallel", "parallel")))(o, gsrc)


def _gate_bwd(dsrc, doff, o, gsrc, goff, name):
    w = o.shape[1]

    def body(d_ref, o_ref, g_ref, do_ref, dg_ref):
        gv = g_ref[...]
        dv = d_ref[...]
        s = _sig(gv)
        do_ref[...] = dv * (gv * s)
        dg_ref[...] = (dv * o_ref[...] * (s * (1.0 + gv * (1.0 - s)))).astype(BF16)

    db, gb = doff // GW, goff // GW
    blk = pl.BlockSpec((TR, GW), lambda n, j: (n, j))
    return pl.pallas_call(
        body, grid=(R // TR, w // GW),
        in_specs=[pl.BlockSpec((TR, GW), lambda n, j: (n, db + j)), blk,
                  pl.BlockSpec((TR, GW), lambda n, j: (n, gb + j))],
        out_specs=[blk, blk],
        out_shape=[jax.ShapeDtypeStruct((R, w), F32), jax.ShapeDtypeStruct((R, w), BF16)], name=name,
        compiler_params=_cp(("parallel", "parallel")))(dsrc, o, gsrc)


def _tail(h, y, g, target):
    def body(h_ref, y_ref, g_ref, t_ref, d_ref, dy_ref, dg_ref, l_ref):
        n = pl.program_id(0)

        @pl.when(n == 0)
        def _():
            d_ref[...] = jnp.zeros_like(d_ref)
            dy_ref[...] = jnp.zeros_like(dy_ref)
            dg_ref[...] = jnp.zeros_like(dg_ref)
            l_ref[...] = jnp.zeros_like(l_ref)

        @pl.when(n > 0)
        def _():
            yv = y_ref[...]
            r = lax.rsqrt(jnp.mean(yv * yv, axis=1, keepdims=True) + NORM_EPS)
            yh = yv * r
            err = (h_ref[...] + yh * g_ref[...]) - t_ref[...]
            dv = err * (1.0 / D)
            d_ref[...] = dv
            l_ref[...] += jnp.sum(err * err, axis=0, keepdims=True)
            dyh = dv * g_ref[...]
            dy_ref[...] = (r * (dyh - yh * jnp.mean(dyh * yh, axis=1, keepdims=True))).astype(BF16)
            dg_ref[...] += jnp.sum(dv * yh, axis=0, keepdims=True)

        @pl.when(n == NB - 1)
        def _():
            tot = jnp.sum(l_ref[...], axis=1, keepdims=True) * (0.5 / D)
            l_ref[...] = jnp.broadcast_to(tot, (1, D))

    blk = pl.BlockSpec((TB, D), lambda n: (n, 0))
    vec = pl.BlockSpec((1, D), lambda n: (0, 0))
    return pl.pallas_call(
        body, grid=(NB,),
        in_specs=[blk, blk, vec, pl.BlockSpec((TB, D), lambda n: (jnp.maximum(n - 1, 0), 0))],
        out_specs=[blk, blk, vec, vec],
        out_shape=[jax.ShapeDtypeStruct((R, D), F32), jax.ShapeDtypeStruct((R, D), BF16),
                   jax.ShapeDtypeStruct((1, D), F32), jax.ShapeDtypeStruct((1, D), F32)],
        name="tail", compiler_params=_cp(("arbitrary",)))(h, y, g, target)


def _lane_row(shape):
    return lax.broadcasted_iota(jnp.int32, shape, 1), lax.broadcasted_iota(jnp.int32, shape, 0)


def _rot_half(x, lane):
    return jnp.where(lane % HD < HD // 2, pltpu.roll(x, 128 - HD // 2, 1), pltpu.roll(x, HD // 2, 1))


def _swa_blocks(n):
    return (0, jnp.maximum(n - 1, 0), n)


SWA_STACKS = ((0, 0), (0, 1), (1, 0), (1, 1))


def _swa_masks(n, lane, row):
    qpos = n * TB + (row & (TB - 1))
    kp = (n - 1) * TB + lane
    kc = n * TB + lane
    m0 = (lane >= PAD) & (qpos - lane >= TB)
    mp = (kp >= PAD) & (qpos >= kp) & (qpos - kp < TB)
    mc = (kc >= PAD) & (qpos >= kc)
    return (m0, mp, mc)


def _stack_pair(xa, xb, par):
    lane = lax.broadcasted_iota(jnp.int32, (TB, 128), 1)
    keep = (lane < HD) if par == 0 else (lane >= HD)
    return jnp.concatenate([jnp.where(keep, xa, 0.0), jnp.where(keep, xb, 0.0)], axis=0)


def _per_head(a, b):
    row = lax.broadcasted_iota(jnp.int32, (2 * TB, 1), 0)
    return jnp.where(row < TB, a, b)


def _swa_load(n, zq_ref, zkv_ref, cs_ref, sn_ref, lane):
    r0 = pl.multiple_of(n * TB, TB)
    csq, snq = cs_ref[pl.ds(r0, TB), :], sn_ref[pl.ds(r0, TB), :]
    qc = []
    for c in range(4):
        x = zq_ref[:, c * 128:(c + 1) * 128]
        qc.append((x * csq + _rot_half(x, lane) * snq) * SCALE)
    qst = [_stack_pair(qc[2 * g], qc[2 * g + 1], par).astype(BF16) for g, par in SWA_STACKS]
    kvs = []
    for b in _swa_blocks(n):
        b0 = pl.multiple_of(b * TB, TB)
        csb, snb = cs_ref[pl.ds(b0, TB), :], sn_ref[pl.ds(b0, TB), :]
        kx = zkv_ref[pl.ds(b0, TB), 0:128]
        kr = kx * csb + _rot_half(kx, lane) * snb
        vx = zkv_ref[pl.ds(b0, TB), 128:256]
        kvs.append((kr.astype(BF16), pltpu.roll(kr, HD, 1).astype(BF16),
                    vx.astype(BF16), pltpu.roll(vx, HD, 1).astype(BF16), csb, snb, b0))
    return qst, (csq, snq), kvs


def _swa_fwd(z0, cs, sn, sinks):
    def body(zq_ref, zkv_ref, cs_ref, sn_ref, sk_ref, ga_ref, o_ref, a_ref, lse_ref):
        n = pl.program_id(0)
        lane, row = _lane_row((TB, 128))
        lo = lane < HD
        masks = _swa_masks(n, *_lane_row((2 * TB, 128)))
        qst, _, kvs = _swa_load(n, zq_ref, zkv_ref, cs_ref, sn_ref, lane)
        ss = [[jnp.where(m, _dot_nt(qst[si], k if par == g else ka), NEG)
               for (k, ka, _, _, _, _, _), m in zip(kvs, masks)] for si, (g, par) in enumerate(SWA_STACKS)]
        o2, lse2 = [], []
        for si, (g, par) in enumerate(SWA_STACKS):
            sink = _per_head(sk_ref[0, 4 * g + par], sk_ref[0, 4 * g + 2 + par])
            s = ss[si]
            mx = jnp.maximum(jnp.maximum(jnp.max(s[0], axis=1, keepdims=True), jnp.max(s[1], axis=1, keepdims=True)),
                             jnp.max(s[2], axis=1, keepdims=True))
            mx = jnp.maximum(mx, sink)
            es = [jnp.exp(sb - mx) for sb in s]
            den = (jnp.sum(es[0], axis=1, keepdims=True) + jnp.sum(es[1], axis=1, keepdims=True)
                   + jnp.sum(es[2], axis=1, keepdims=True) + jnp.exp(sink - mx))
            inv = 1.0 / den
            t = jnp.zeros((2 * TB, 128), F32)
            for (_, _, v, va, _, _, _), e in zip(kvs, es):
                t = t + _dot((e * inv).astype(BF16), v if par == g else va)
            o2.append(t)
            lse2.append(mx + jnp.log(den))
        lse_t = jnp.zeros((TB, 128), F32)
        for g in range(2):
            for t in range(2):
                rows = slice(t * TB, (t + 1) * TB)
                c = 2 * g + t
                oc = jnp.where(lo, o2[2 * g][rows], o2[2 * g + 1][rows])
                o_ref[:, c * 128:(c + 1) * 128] = oc
                gv = ga_ref[:, c * 128:(c + 1) * 128]
                a_ref[:, c * 128:(c + 1) * 128] = (oc * (gv * _sig(gv))).astype(BF16)
                for par in range(2):
                    lse_t = jnp.where(lane == 4 * g + 2 * t + par, lse2[2 * g + par][rows], lse_t)
        lse_ref[...] = lse_t

    full = pl.BlockSpec((R, 128), lambda n: (0, 0))
    return dict(
        body=body,
        in_specs=[pl.BlockSpec((TB, 512), lambda n: (n, C_Q // 512)),
                  pl.BlockSpec((R, 256), lambda n: (0, C_K // 256)), full, full,
                  pl.BlockSpec(memory_space=pltpu.SMEM), pl.BlockSpec((TB, 512), lambda n: (n, C_GA // 512))],
        args=[z0, z0, cs, sn, sinks, z0],
        out_specs=[pl.BlockSpec((TB, 512), lambda n: (n, 0)), pl.BlockSpec((TB, 512), lambda n: (n, 0)),
                   pl.BlockSpec((TB, 128), lambda n: (n, 0))],
        out_shape=[jax.ShapeDtypeStruct((R, 512), F32), jax.ShapeDtypeStruct((R, 512), BF16),
                   jax.ShapeDtypeStruct((R, 128), F32)],
        scratch=[])


def _swa_bwd(z0, cs, sn, sinks, o, dmix, lse):
    def body(zq_ref, zkv_ref, cs_ref, sn_ref, sk_ref, ga_ref, o_ref, dm_ref, lse_ref,
             dq_ref, dga_ref, dkv_ref, dsk_ref, do_ref, acc_ref):
        n = pl.program_id(0)

        @pl.when(n == 0)
        def _():
            acc_ref[...] = jnp.zeros_like(acc_ref)
            dsk_ref[...] = jnp.zeros_like(dsk_ref)

        gv, dmv = ga_ref[...], dm_ref[...]
        sg = _sig(gv)
        dga_ref[...] = (dmv * o_ref[...] * (sg * (1.0 + gv * (1.0 - sg)))).astype(BF16)
        do_ref[...] = dmv * (gv * sg)
        lane, row = _lane_row((TB, 128))
        lo = lane < HD
        masks = _swa_masks(n, *_lane_row((2 * TB, 128)))
        qst, (csq, snq), kvs = _swa_load(n, zq_ref, zkv_ref, cs_ref, sn_ref, lane)
        lse_t = lse_ref[...]
        ss = [[jnp.where(m, _dot_nt(qst[si], k if par == g else ka), NEG)
               for (k, ka, _, _, _, _, _), m in zip(kvs, masks)] for si, (g, par) in enumerate(SWA_STACKS)]
        dobs, deltas, lses, dps = [], [], [], []
        for g, par in SWA_STACKS:
            ca, cb = slice(2 * g * 128, (2 * g + 1) * 128), slice((2 * g + 1) * 128, (2 * g + 2) * 128)
            dom = _stack_pair(do_ref[:, ca], do_ref[:, cb], par)
            deltas.append(jnp.sum(dom * jnp.concatenate([o_ref[:, ca], o_ref[:, cb]], axis=0), axis=1, keepdims=True))
            dob = dom.astype(BF16)
            dobs.append(dob)
            lses.append(jnp.concatenate(
                [jnp.sum(jnp.where(lane == 4 * g + 2 * t + par, lse_t, 0.0), axis=1, keepdims=True) for t in range(2)],
                axis=0))
            dps.append([_dot_nt(dob, v if par == g else va) for (_, _, v, va, _, _, _) in kvs])
        dk_al = [jnp.zeros((TB, 128), F32) for _ in range(3)]
        dk_mis = [jnp.zeros((TB, 128), F32) for _ in range(3)]
        dv_al = [jnp.zeros((TB, 128), F32) for _ in range(3)]
        dv_mis = [jnp.zeros((TB, 128), F32) for _ in range(3)]
        dsk_t = jnp.zeros((TB, 128), F32)
        dq2 = []
        for si, (g, par) in enumerate(SWA_STACKS):
            dqt = jnp.zeros((2 * TB, 128), F32)
            for bi, (k, ka, _, _, _, _, _) in enumerate(kvs):
                p = jnp.exp(ss[si][bi] - lses[si])
                ds = (p * (dps[si][bi] - deltas[si])).astype(BF16)
                dqt = dqt + _dot(ds, k if par == g else ka)
                dkh = _dot_tn(ds, qst[si])
                dvh = _dot_tn(p.astype(BF16), dobs[si])
                if par == g:
                    dk_al[bi] = dk_al[bi] + dkh
                    dv_al[bi] = dv_al[bi] + dvh
                else:
                    dk_mis[bi] = dk_mis[bi] + dkh
                    dv_mis[bi] = dv_mis[bi] + dvh
            dq2.append(dqt)
            sink = _per_head(sk_ref[0, 4 * g + par], sk_ref[0, 4 * g + 2 + par])
            dsk = -jnp.exp(sink - lses[si]) * deltas[si]
            for t in range(2):
                dsk_t = jnp.where(lane == 4 * g + 2 * t + par, dsk[t * TB:(t + 1) * TB], dsk_t)
        for g in range(2):
            for t in range(2):
                rows = slice(t * TB, (t + 1) * TB)
                c = 2 * g + t
                dqc = jnp.where(lo, dq2[2 * g][rows], dq2[2 * g + 1][rows]) * SCALE
                dq_ref[:, c * 128:(c + 1) * 128] = (dqc * csq + _rot_half(dqc * snq, lane)).astype(BF16)
        for bi, (_, _, _, _, csb, snb, b0) in enumerate(kvs):
            dk = dk_al[bi] + pltpu.roll(dk_mis[bi], HD, 1)
            dv = dv_al[bi] + pltpu.roll(dv_mis[bi], HD, 1)
            acc_ref[pl.ds(b0, TB), 0:128] += dk * csb + _rot_half(dk * snb, lane)
            acc_ref[pl.ds(b0, TB), 128:256] += dv
        dsk_ref[0:1, :] += jnp.sum(dsk_t, axis=0, keepdims=True)

        @pl.when(n == NB - 1)
        def _():
            dkv_ref[...] = acc_ref[...].astype(BF16)

    full = pl.BlockSpec((R, 128), lambda n: (0, 0))
    b512 = pl.BlockSpec((TB, 512), lambda n: (n, 0))
    return dict(
        body=body,
        in_specs=[pl.BlockSpec((TB, 512), lambda n: (n, C_Q // 512)),
                  pl.BlockSpec((R, 256), lambda n: (0, C_K // 256)), full, full,
                  pl.BlockSpec(memory_space=pltpu.SMEM), pl.BlockSpec((TB, 512), lambda n: (n, C_GA // 512)),
                  b512, b512, pl.BlockSpec((TB, 128), lambda n: (n, 0))],
        args=[z0, z0, cs, sn, sinks, z0, o, dmix, lse],
        out_specs=[b512, b512, pl.BlockSpec((R, 256), lambda n: (0, 0)), pl.BlockSpec((8, 128), lambda n: (0, 0))],
        out_shape=[jax.ShapeDtypeStruct((R, 512), BF16), jax.ShapeDtypeStruct((R, 512), BF16),
                   jax.ShapeDtypeStruct((R, 256), BF16), jax.ShapeDtypeStruct((8, 128), F32)],
        scratch=[pltpu.VMEM((TB, 512), F32), pltpu.VMEM((R, 256), F32)])


CC = 512
HALO = CONV_W - 1


def _conv_fwd(z0, conv_w, conv_b, ln_g, ln_b):
    def body(g_ref, w_ref, cb_ref, lg_ref, lb_ref, cv_ref, s_ref, ubuf):
        n = pl.program_id(0)

        @pl.when(n == 0)
        def _():
            ubuf[...] = jnp.zeros_like(ubuf)

        u = g_ref[:, 0:CC] * _sig(g_ref[:, CC:2 * CC])
        for k in range(8):
            ubuf[k, 0:TB + 8, :] = ubuf[k, TB:2 * TB + 8, :]
            ubuf[k, pl.ds(TB + 8 - k, TB), :] = u
        acc = jnp.zeros((TB, CC), F32)
        for w in range(CONV_W):
            off = TB - HALO + w
            acc = acc + ubuf[off % 8, pl.ds(off + 8 - off % 8, TB), :] * w_ref[w:w + 1, :]
        cv = acc + cb_ref[...]
        cv_ref[...] = cv
        xc = cv - jnp.mean(cv, axis=1, keepdims=True)
        rs = lax.rsqrt(jnp.mean(xc * xc, axis=1, keepdims=True) + LN_EPS)
        ln = xc * rs * lg_ref[...] + lb_ref[...]
        s_ref[...] = (ln * _sig(ln)).astype(BF16)

    vec = pl.BlockSpec((1, CC), lambda n: (0, 0))
    blk = pl.BlockSpec((TB, CC), lambda n: (n, 0))
    return dict(
        body=body,
        in_specs=[pl.BlockSpec((TB, 2 * CC), lambda n: (n, C_GLU // (2 * CC))),
                  pl.BlockSpec((32, CC), lambda n: (0, 0)), vec, vec, vec],
        args=[z0, conv_w, conv_b, ln_g, ln_b],
        out_specs=[blk, blk],
        out_shape=[jax.ShapeDtypeStruct((R, CC), F32), jax.ShapeDtypeStruct((R, CC), BF16)],
        scratch=[pltpu.VMEM((8, 2 * TB + 8, CC), F32)])


def _conv_bwd(ds, cv, z0, conv_w, ln_g, ln_b):
    def body(ds_ref, cv_ref, g_ref, w_ref, lg_ref, lb_ref, dglu_ref, dw_ref, dsm_ref, dbuf):
        n = pl.program_id(0)

        @pl.when(n == 0)
        def _():
            dbuf[...] = jnp.zeros_like(dbuf)
            dw_ref[...] = jnp.zeros_like(dw_ref)
            dsm_ref[...] = jnp.zeros_like(dsm_ref)

        cv = cv_ref[...]
        xc = cv - jnp.mean(cv, axis=1, keepdims=True)
        rs = lax.rsqrt(jnp.mean(xc * xc, axis=1, keepdims=True) + LN_EPS)
        xh = xc * rs
        ln = xh * lg_ref[...] + lb_ref[...]
        sg = _sig(ln)
        dln = ds_ref[...] * (sg * (1.0 + ln * (1.0 - sg)))
        dxh = dln * lg_ref[...]
        dcv = rs * (dxh - jnp.mean(dxh, axis=1, keepdims=True) - xh * jnp.mean(dxh * xh, axis=1, keepdims=True))
        dsm_ref[0:1, :] += jnp.sum(dcv, axis=0, keepdims=True)
        dsm_ref[1:2, :] += jnp.sum(dln * xh, axis=0, keepdims=True)
        dsm_ref[2:3, :] += jnp.sum(dln, axis=0, keepdims=True)
        for k in range(8):
            dbuf[k, TB:2 * TB + 8, :] = dbuf[k, 0:TB + 8, :]
            dbuf[k, pl.ds(8 - k, TB), :] = dcv
        a = g_ref[:, 0:CC]
        sb = _sig(g_ref[:, CC:2 * CC])
        u = a * sb
        du = jnp.zeros((TB, CC), F32)
        for w in range(CONV_W):
            off = HALO - w
            sh = dbuf[off % 8, pl.ds(off + 8 - off % 8, TB), :]
            du = du + sh * w_ref[w:w + 1, :]
            dw_ref[w:w + 1, :] += jnp.sum(u * sh, axis=0, keepdims=True)
        dglu_ref[:, 0:CC] = (du * sb).astype(BF16)
        dglu_ref[:, CC:2 * CC] = (du * a * sb * (1.0 - sb)).astype(BF16)

    rev = lambda n: (NB - 1 - n, 0)
    vec = pl.BlockSpec((1, CC), lambda n: (0, 0))
    blk = pl.BlockSpec((TB, CC), rev)
    return dict(
        body=body,
        in_specs=[blk, blk, pl.BlockSpec((TB, 2 * CC), lambda n: (NB - 1 - n, C_GLU // (2 * CC))),
                  pl.BlockSpec((32, CC), lambda n: (0, 0)), vec, vec],
        args=[ds, cv, z0, conv_w, ln_g, ln_b],
        out_specs=[pl.BlockSpec((TB, 2 * CC), rev), pl.BlockSpec((32, CC), lambda n: (0, 0)),
                   pl.BlockSpec((8, CC), lambda n: (0, 0))],
        out_shape=[jax.ShapeDtypeStruct((R, 2 * CC), BF16), jax.ShapeDtypeStruct((32, CC), F32),
                   jax.ShapeDtypeStruct((8, CC), F32)],
        scratch=[pltpu.VMEM((8, 2 * TB + 8, CC), F32)])


def _split_dot(x, t):
    hi = x.astype(BF16)
    lo = (x - hi.astype(F32)).astype(BF16)
    return _dot(hi, t) + _dot(lo, t)


def _stack_heads(x):
    lane = lax.broadcasted_iota(jnp.int32, (TB, 128), 1)
    return jnp.concatenate([jnp.where(lane < HD, x, 0.0), jnp.where(lane < HD, 0.0, x)], axis=0).astype(BF16)


def _sb_stack(qv, i):
    lane2, row2 = _lane_row((2 * TB, 128))
    qpos2 = i * TB + (row2 & (TB - 1))
    lane, row = _lane_row((TB, 128))
    return _stack_heads(qv), lane2, qpos2, (row > lane).astype(BF16)


SB_U = 3
SB_DEAD = -104.0
SB_P = 4


def _sb_fwd(q, k, v, g):
    def body(q_ref, k_ref, v_ref, g_ref, o_ref, m_ref, c_ref, n_ref):
        p, i = pl.program_id(0), pl.program_id(1)
        lane, row = _lane_row((TB, 128))
        lo = lane < HD
        slabs = [slice(s * 128, (s + 1) * 128) for s in range(SB_P)]
        q2s = []
        for sl in slabs:
            q2, lane2, qpos2, tri_gt = _sb_stack(q_ref[:, sl].astype(F32) * SCALE, i)
            q2s.append(q2)

        def cond(st):
            t, _, c2s = st
            alive = jnp.max(c2s[0])
            for c2 in c2s[1:]:
                alive = jnp.maximum(alive, jnp.max(c2))
            return jnp.logical_and(i - SB_U * t >= 0, alive > SB_DEAD)

        def step(st):
            t, accs, c2s = st
            accs, c2s = list(accs), list(c2s)
            jrs = [i - SB_U * t - u for u in range(SB_U)]
            j0s = [pl.multiple_of(jnp.maximum(jr, 0) * TB, TB) for jr in jrs]
            valids = []
            for jr in jrs:
                kpos = jr * TB + lane2
                valids.append((kpos >= PAD) & (kpos < qpos2))
            zs = [[_dot_nt(q2s[s], k_ref[pl.ds(j0, TB), slabs[s]]) for j0 in j0s] for s in range(SB_P)]
            lbs, l1s = [], []
            for s in range(SB_P):
                lbs.append([jnp.minimum(z, 0.0) - jnp.log(1.0 + jnp.exp(-jnp.abs(z))) for z in zs[s]])
                l1s.append([jnp.where(valid, lb - z, 0.0) for valid, lb, z in zip(valids, lbs[s], zs[s])])
            sfxs = [[_split_dot(l1, tri_gt) for l1 in l1s[s]] for s in range(SB_P)]
            carries = []
            for s in range(SB_P):
                cs, c2 = [], c2s[s]
                for jr, l1 in zip(jrs, l1s[s]):
                    cs.append(c2)
                    c_ref[:, slabs[s]] = jnp.where(lane == 2 * jr, c2[0:TB],
                                                   jnp.where(lane == 2 * jr + 1, c2[TB:2 * TB], c_ref[:, slabs[s]]))
                    c2 = c2 + jnp.sum(l1, axis=1, keepdims=True)
                carries.append(cs)
                c2s[s] = c2
            for s in range(SB_P):
                for j0, valid, lb, sfx, cu in zip(j0s, valids, lbs[s], sfxs[s], carries[s]):
                    a = jnp.where(valid, jnp.exp(lb + sfx + cu), 0.0).astype(BF16)
                    av = _dot(a, v_ref[pl.ds(j0, TB), slabs[s]])
                    accs[s] = accs[s] + jnp.where(lo, av[0:TB], av[TB:2 * TB])
            return t + 1, tuple(accs), tuple(c2s)

        c_ref[...] = jnp.zeros_like(c_ref)
        init = (jnp.int32(0), tuple(jnp.zeros((TB, 128), F32) for _ in slabs),
                tuple(jnp.zeros((2 * TB, 1), F32) for _ in slabs))
        t, accs, _ = lax.while_loop(cond, step, init)
        for sl, acc in zip(slabs, accs):
            o_ref[:, sl] = acc
            gv = g_ref[:, sl]
            m_ref[:, sl] = (acc * (gv * _sig(gv))).astype(BF16)
        n_ref[p, i] = t

    wide = SB_P * 128
    slab = pl.BlockSpec((R, wide), lambda p, i: (0, p))
    blk = pl.BlockSpec((TB, wide), lambda p, i: (i, p))
    sd = jax.ShapeDtypeStruct((R, D), F32)
    return pl.pallas_call(
        body, grid=(D // wide, NB), in_specs=[blk, slab, slab, blk],
        out_specs=[blk, blk, blk, pl.BlockSpec(memory_space=pltpu.SMEM)],
        out_shape=[sd, jax.ShapeDtypeStruct((R, D), BF16), sd, jax.ShapeDtypeStruct((D // wide, NB), jnp.int32)],
        name="sb_fwd", compiler_params=_cp(("arbitrary", "arbitrary")))(q, k, v, g)


def _sb_bwd(trips, q, k, v, car, dm, g, o):
    def body(n_ref, q_ref, k_ref, v_ref, c_ref, dm_ref, g_ref, o_ref, dq_ref, dko_ref, dvo_ref, dg_ref,
             dk_ref, dv_ref):
        p, i = pl.program_id(0), pl.program_id(1)

        @pl.when(i == 0)
        def _():
            dk_ref[...] = jnp.zeros_like(dk_ref)
            dv_ref[...] = jnp.zeros_like(dv_ref)

        lane, row = _lane_row((TB, 128))
        lo = lane < HD
        tri_lt = (row < lane).astype(BF16)
        slabs = [slice(s * 128, (s + 1) * 128) for s in range(SB_P)]
        q2s, do2s, cts = [], [], []
        for sl in slabs:
            q2, lane2, qpos2, tri_gt = _sb_stack(q_ref[:, sl].astype(F32) * SCALE, i)
            q2s.append(q2)
            gv, dmv = g_ref[:, sl], dm_ref[:, sl]
            sg = _sig(gv)
            dg_ref[:, sl] = (dmv * o_ref[:, sl] * (sg * (1.0 + gv * (1.0 - sg)))).astype(BF16)
            do2s.append(_stack_heads(dmv * (gv * sg)))
            cts.append(c_ref[:, sl])
        trips_i = n_ref[p, i]
        first = jnp.maximum(i + 1 - SB_U * trips_i, 0)

        def step(t, carry):
            dqs, g2s = carry
            dqs, g2s = list(dqs), list(g2s)
            jrs = [first + SB_U * t + u for u in range(SB_U)]
            j0s = [pl.multiple_of(jnp.minimum(jr, i) * TB, TB) for jr in jrs]
            valids = []
            for jr in jrs:
                kpos = jr * TB + lane2
                valids.append((kpos >= PAD) & (kpos < qpos2))
            ks = [[k_ref[pl.ds(j0, TB), sl] for j0 in j0s] for sl in slabs]
            zs = [[_dot_nt(q2s[s], kj) for kj in ks[s]] for s in range(SB_P)]
            das = [[_dot_nt(do2s[s], v_ref[pl.ds(j0, TB), slabs[s]]) for j0 in j0s] for s in range(SB_P)]
            es = [[jnp.exp(-jnp.abs(z)) for z in zs[s]] for s in range(SB_P)]
            lbs = [[jnp.minimum(z, 0.0) - jnp.log(1.0 + e) for z, e in zip(zs[s], es[s])] for s in range(SB_P)]
            l1s = [[jnp.where(valid, lb - z, 0.0) for valid, lb, z in zip(valids, lbs[s], zs[s])] for s in range(SB_P)]
            sfxs = [[_split_dot(l1, tri_gt) for l1 in l1s[s]] for s in range(SB_P)]
            a_s, gmats, gpres = [], [], []
            for s in range(SB_P):
                a_l, gm_l, gp_l, g2 = [], [], [], g2s[s]
                for jr, valid, lb, sfx, da in zip(jrs, valids, lbs[s], sfxs[s], das[s]):
                    later = jnp.concatenate(
                        [jnp.sum(jnp.where(lane == 2 * jr + hh, cts[s], 0.0), axis=1, keepdims=True) for hh in range(2)],
                        axis=0)
                    a = jnp.where(valid, jnp.exp(lb + sfx + later), 0.0)
                    gmat = da * a
                    a_l.append(a.astype(BF16))
                    gm_l.append(gmat)
                    gp_l.append(g2)
                    g2 = g2 + jnp.sum(gmat, axis=1, keepdims=True)
                a_s.append(a_l)
                gmats.append(gm_l)
                gpres.append(gp_l)
                g2s[s] = g2
            pres = [[gp + _split_dot(gmat, tri_lt) for gp, gmat in zip(gpres[s], gmats[s])] for s in range(SB_P)]
            for s in range(SB_P):
                for j0, kj, valid, z, e, gmat, pre, a in zip(j0s, ks[s], valids, zs[s], es[s], gmats[s], pres[s], a_s[s]):
                    r = 1.0 / (1.0 + e)
                    big = z >= 0.0
                    beta = jnp.where(big, r, e * r)
                    omb = jnp.where(big, e * r, r)
                    dz = jnp.where(valid, gmat * omb - beta * pre, 0.0).astype(BF16)
                    dq2 = _dot(dz, kj)
                    dqs[s] = dqs[s] + jnp.where(lo, dq2[0:TB], dq2[TB:2 * TB])
                    dk_ref[pl.ds(j0, TB), slabs[s]] += _dot_tn(dz, q2s[s])
                    dv_ref[pl.ds(j0, TB), slabs[s]] += _dot_tn(a, do2s[s])
            return tuple(dqs), tuple(g2s)

        init = (tuple(jnp.zeros((TB, 128), F32) for _ in slabs), tuple(jnp.zeros((2 * TB, 1), F32) for _ in slabs))
        dqs, _ = lax.fori_loop(0, trips_i, step, init)
        for sl, dq in zip(slabs, dqs):
            dq_ref[:, sl] = (dq * SCALE).astype(BF16)

        @pl.when(i == NB - 1)
        def _():
            dko_ref[...] = dk_ref[...].astype(BF16)
            dvo_ref[...] = dv_ref[...].astype(BF16)

    wide = SB_P * 128
    slab = pl.BlockSpec((R, wide), lambda p, i: (0, p))
    blk = pl.BlockSpec((TB, wide), lambda p, i: (i, p))
    sd = jax.ShapeDtypeStruct((R, D), BF16)
    return pl.pallas_call(
        body, grid=(D // wide, NB),
        in_specs=[pl.BlockSpec(memory_space=pltpu.SMEM), blk, slab, slab, blk, blk, blk, blk],
        out_specs=[blk, slab, slab, blk], out_shape=[sd, sd, sd, sd],
        scratch_shapes=[pltpu.VMEM((R, wide), F32), pltpu.VMEM((R, wide), F32)], name="sb_bwd",
        compiler_params=_cp(("arbitrary", "arbitrary")))(trips, q, k, v, car, dm, g, o)


def _adamw(w, parts, m, v, name):
    rows, cols = w.shape
    tr = 256 if rows % 256 == 0 else rows
    nparts = len(parts)

    def body(*refs):
        w_ref = refs[0]
        p_refs = refs[1:1 + nparts]
        m_ref, v_ref, g_ref, d_ref, nm_ref, nv_ref = refs[1 + nparts:]
        g = p_refs[0][...]
        for p_ref in p_refs[1:]:
            g = g + p_ref[...]
        nm = ADAM_B1 * m_ref[...] + (1.0 - ADAM_B1) * g
        nv = ADAM_B2 * v_ref[...] + (1.0 - ADAM_B2) * (g * g)
        m_hat = nm / (1.0 - ADAM_B1 ** ADAM_STEP)
        v_hat = nv / (1.0 - ADAM_B2 ** ADAM_STEP)
        g_ref[...] = g
        d_ref[...] = -ADAM_LR * (m_hat / (jnp.sqrt(v_hat) + ADAM_EPS) + ADAM_WD * w_ref[...])
        nm_ref[...] = nm
        nv_ref[...] = nv

    blk = pl.BlockSpec((tr, cols), lambda i: (i, 0))
    sd = jax.ShapeDtypeStruct((rows, cols), F32)
    return pl.pallas_call(
        body, grid=(rows // tr,), in_specs=[blk] * (3 + nparts), out_specs=[blk] * 4, out_shape=[sd] * 4,
        name=name, compiler_params=_cp(("parallel",)))(w, *parts, m, v)


def _sum8(buf, name):
    _, rows, cols = buf.shape

    def body(b_ref, o_ref):
        acc = b_ref[0]
        for i in range(1, 8):
            acc = acc + b_ref[i]
        o_ref[...] = acc

    return pl.pallas_call(
        body, out_shape=jax.ShapeDtypeStruct((rows, cols), F32), name=name,
        compiler_params=pltpu.CompilerParams(vmem_limit_bytes=VMEM_LIMIT))(buf)


MESH = pl.DeviceIdType.MESH
ANY = pl.BlockSpec(memory_space=pl.ANY)


def _chip_peers():
    x, y = lax.axis_index("x"), lax.axis_index("y")
    return [(1 - x, y), (x, 1 - y), (1 - x, 1 - y)]


def _gather_chips(shards):
    plan = _gather_plan(shards)

    def body(*refs):
        n = len(shards)
        ins, outs, sems = refs[:n], refs[n:2 * n], refs[2 * n:]
        plan["start"](ins, outs, sems)
        plan["mid"](ins, outs, sems)
        plan["finish"](ins, outs, sems)

    n = len(shards)
    res = pl.pallas_call(
        body, in_specs=[ANY] * n, out_specs=[ANY] * n, out_shape=plan["out_shape"],
        scratch_shapes=plan["sems"], name="gather_chips")(*plan["args"])
    return plan["post"](res)


def _gather_plan(shards):
    n = len(shards)
    shards = [s.reshape((2, s.shape[0] // 2) + s.shape[1:]) for s in shards]

    def copies(kind, ins, outs, sems):
        s1, r1, s2, r2 = sems
        x, y, c = lax.axis_index("x"), lax.axis_index("y"), lax.axis_index("c")
        me = 2 * x + y
        out = []
        for j, (px, py) in enumerate(_chip_peers()):
            for a in range(n):
                k = j * n + a
                got = outs[a].at[2 * px + py].at[c]
                other = outs[a].at[2 * px + py].at[1 - c]
                src, dst, ss, rs, dev = {
                    "first": (ins[a].at[c], outs[a].at[me].at[c], s1, r1, (px, py, c)),
                    "landed": (got, got, s1, r1, (px, py, c)),
                    "passed": (got, got, s2, r2, (x, y, 1 - c)),
                    "theirs": (other, other, s2, r2, (x, y, 1 - c)),
                }[kind]
                out.append(pltpu.make_async_remote_copy(
                    src_ref=src, dst_ref=dst, send_sem=ss.at[k], recv_sem=rs.at[k], device_id=dev, device_id_type=MESH))
        return out

    def start(ins, outs, sems):
        for cp in copies("first", ins, outs, sems):
            cp.start()

    def mid(ins, outs, sems):
        for got, fwd in zip(copies("landed", ins, outs, sems), copies("passed", ins, outs, sems)):
            got.wait_recv()
            fwd.start()

    def finish(ins, outs, sems):
        for cp in copies("theirs", ins, outs, sems):
            cp.wait_recv()
        for cp in copies("first", ins, outs, sems) + copies("passed", ins, outs, sems):
            cp.wait_send()

    def post(res):
        me = 2 * lax.axis_index("x") + lax.axis_index("y")
        res = [lax.dynamic_update_index_in_dim(r, s, me, 0) for r, s in zip(res, shards)]
        return [r.reshape((N_CHIPS, 2 * r.shape[2]) + r.shape[3:]) for r in res]

    return dict(args=shards, out_shape=[jax.ShapeDtypeStruct((N_CHIPS,) + s.shape, s.dtype) for s in shards],
                sems=[pltpu.SemaphoreType.DMA((3 * n,))] * 4, start=start, mid=mid, finish=finish, post=post)


def _rows_call(name, parts, plan):
    n_in = [len(p["args"]) for p in parts]
    n_out = [len(p["out_shape"]) for p in parts]
    n_scr = [len(p["scratch"]) for p in parts]
    c_in, c_out = len(plan["args"]), len(plan["out_shape"])

    def split(refs, sizes):
        out, pos = [], 0
        for k in sizes:
            out.append(refs[pos:pos + k])
            pos += k
        return out

    def body(*refs):
        ins, outs, scr = split(refs, [sum(n_in) + c_in, sum(n_out) + c_out, sum(n_scr) + len(plan["sems"])])
        p_in, p_out, p_scr = split(ins, n_in + [c_in]), split(outs, n_out + [c_out]), split(scr, n_scr + [len(plan["sems"])])
        comm = (p_in[-1], p_out[-1], p_scr[-1])
        step = pl.program_id(0)

        @pl.when(step == 0)
        def _():
            plan["start"](*comm)

        for p, i, o, s in zip(parts, p_in, p_out, p_scr):
            p["body"](*i, *o, *s)

        @pl.when(step == NB - 2)
        def _():
            plan["mid"](*comm)

        @pl.when(step == NB - 1)
        def _():
            plan["finish"](*comm)

    flat = lambda key: [v for p in parts for v in p[key]]
    res = pl.pallas_call(
        body, grid=(NB,), in_specs=flat("in_specs") + [ANY] * c_in, out_specs=flat("out_specs") + [ANY] * c_out,
        out_shape=flat("out_shape") + plan["out_shape"], scratch_shapes=flat("scratch") + plan["sems"],
        name=name, compiler_params=_cp(("arbitrary",)))(*flat("args"), *plan["args"])
    outs = split(res, n_out + [c_out])
    return outs[:-1], outs[-1]


def _pair_exchange(grads, name):
    n = len(grads)
    hs = [g.shape[1] // 2 for g in grads]
    grads = [g.reshape((N_CHIPS, 2, h) + g.shape[2:]) for g, h in zip(grads, hs)]

    def body(*refs):
        ins, got = refs[:n], refs[n:2 * n]
        ssem, rsem = refs[2 * n:]
        x, y, c = lax.axis_index("x"), lax.axis_index("y"), lax.axis_index("c")
        sends = [pltpu.make_async_remote_copy(
            src_ref=ins[a].at[:, 1 - c], dst_ref=got[a], send_sem=ssem.at[a],
            recv_sem=rsem.at[a], device_id=(x, y, 1 - c), device_id_type=MESH) for a in range(n)]
        for cp in sends:
            cp.start()
        for cp in sends:
            cp.wait()

    half_shapes = [jax.ShapeDtypeStruct((N_CHIPS, h) + g.shape[3:], g.dtype) for g, h in zip(grads, hs)]
    got = pl.pallas_call(
        body, in_specs=[ANY] * n, out_specs=[ANY] * n, out_shape=half_shapes,
        scratch_shapes=[pltpu.SemaphoreType.DMA((n,))] * 2, name=name)(*grads)
    c = lax.axis_index("c")
    own = [lax.dynamic_index_in_dim(g, c, 1, keepdims=False) for g in grads]
    return own, got


def _sum_pair(own, got, send_dtype, name):
    _, rows, cols = own.shape
    tr = 256 if rows % 256 == 0 else rows

    def body(a_ref, b_ref, f_ref, s_ref):
        t = a_ref[...].astype(F32) + b_ref[...].astype(F32)
        f_ref[...] = t
        s_ref[...] = t.astype(send_dtype)

    blk = pl.BlockSpec((N_CHIPS, tr, cols), lambda i: (0, i, 0))
    return pl.pallas_call(
        body, grid=(rows // tr,), in_specs=[blk, blk], out_specs=[blk, blk],
        out_shape=[jax.ShapeDtypeStruct(own.shape, F32), jax.ShapeDtypeStruct(own.shape, send_dtype)],
        name=name, compiler_params=_cp(("parallel",)))(own, got)


def _scatter_chips(keep, send):
    n = len(send)
    plan = _scatter_plan(send)

    def body(*refs):
        sin, land, sems = refs[:n], refs[n:2 * n], refs[2 * n:]
        plan["start"](sin, land, sems)
        plan["finish"](sin, land, sems)

    land = pl.pallas_call(
        body, in_specs=[ANY] * n, out_specs=[ANY] * n, out_shape=plan["out_shape"],
        scratch_shapes=plan["sems"], name="scatter_chips")(*send)
    return _own_slab(keep), land


def _own_slab(keep):
    me = 2 * lax.axis_index("x") + lax.axis_index("y")
    return [lax.dynamic_index_in_dim(k, me, 0, keepdims=False) for k in keep]


def _scatter_plan(send):
    n = len(send)

    def copies(sin, land, sems):
        ssem, rsem = sems
        c = lax.axis_index("c")
        return [pltpu.make_async_remote_copy(
            src_ref=sin[a].at[2 * px + py], dst_ref=land[a].at[j], send_sem=ssem.at[j * n + a],
            recv_sem=rsem.at[j * n + a], device_id=(px, py, c), device_id_type=MESH)
            for j, (px, py) in enumerate(_chip_peers()) for a in range(n)]

    def start(sin, land, sems):
        for cp in copies(sin, land, sems):
            cp.start()

    def finish(sin, land, sems):
        for cp in copies(sin, land, sems):
            cp.wait()

    return dict(args=list(send), out_shape=[jax.ShapeDtypeStruct((3,) + s.shape[1:], s.dtype) for s in send],
                sems=[pltpu.SemaphoreType.DMA((3 * n,))] * 2, start=start, mid=lambda *a: None, finish=finish)


def _sum_shard(mine, land, name):
    rows, cols = mine.shape
    tr = 256 if rows % 256 == 0 else rows

    def body(m_ref, l_ref, o_ref):
        o_ref[...] = ((m_ref[...] + l_ref[0].astype(F32)) + l_ref[1].astype(F32)) + l_ref[2].astype(F32)

    return pl.pallas_call(
        body, grid=(rows // tr,),
        in_specs=[pl.BlockSpec((tr, cols), lambda i: (i, 0)), pl.BlockSpec((3, tr, cols), lambda i: (0, i, 0))],
        out_specs=pl.BlockSpec((tr, cols), lambda i: (i, 0)), out_shape=jax.ShapeDtypeStruct((rows, cols), F32),
        name=name, compiler_params=_cp(("parallel",)))(mine, land)


def _join_cores(halves):
    n = len(halves)

    def body(*refs):
        ins, outs = refs[:n], refs[n:2 * n]
        ssem, rsem = refs[2 * n:]
        x, y, c = lax.axis_index("x"), lax.axis_index("y"), lax.axis_index("c")
        sends = [pltpu.make_async_remote_copy(
            src_ref=ins[a], dst_ref=outs[a].at[c], send_sem=ssem.at[a], recv_sem=rsem.at[a],
            device_id=(x, y, 1 - c), device_id_type=MESH) for a in range(n)]
        for cp in sends:
            cp.start()
        for a in range(n):
            sends[a].wait_send()
            pltpu.make_async_remote_copy(
                src_ref=ins[a], dst_ref=outs[a].at[1 - c], send_sem=ssem.at[a], recv_sem=rsem.at[a],
                device_id=(x, y, 1 - c), device_id_type=MESH).wait_recv()

    res = pl.pallas_call(
        body, in_specs=[ANY] * n, out_specs=[ANY] * n,
        out_shape=[jax.ShapeDtypeStruct((2,) + h.shape, h.dtype) for h in halves],
        scratch_shapes=[pltpu.SemaphoreType.DMA((n,))] * 2, name="join_cores")(*halves)
    c = lax.axis_index("c")
    res = [lax.dynamic_update_index_in_dim(r, h, c, 0) for r, h in zip(res, halves)]
    return [r.reshape((2 * r.shape[1],) + r.shape[2:]) for r in res]


def _gather_all(vec):
    def body(v_ref, o_ref, lsem, ssem, rsem):
        x, y, c = lax.axis_index("x"), lax.axis_index("y"), lax.axis_index("c")
        me = 4 * x + 2 * y + c
        local = pltpu.make_async_copy(v_ref, o_ref.at[me], lsem)
        local.start()
        cps = []
        for k in range(1, 8):
            px, py, pc = x ^ (k >> 2), y ^ ((k >> 1) & 1), c ^ (k & 1)
            cps.append(pltpu.make_async_remote_copy(
                src_ref=v_ref, dst_ref=o_ref.at[me], send_sem=ssem.at[k - 1], recv_sem=rsem.at[k - 1],
                device_id=(px, py, pc), device_id_type=MESH))
        for cp in cps:
            cp.start()
        for k in range(1, 8):
            px, py, pc = x ^ (k >> 2), y ^ ((k >> 1) & 1), c ^ (k & 1)
            pltpu.make_async_remote_copy(
                src_ref=v_ref, dst_ref=o_ref.at[4 * px + 2 * py + pc], send_sem=ssem.at[k - 1],
                recv_sem=rsem.at[k - 1], device_id=(px, py, pc), device_id_type=MESH).wait_recv()
        for cp in cps:
            cp.wait_send()
        local.wait()

    return pl.pallas_call(
        body, in_specs=[ANY], out_specs=ANY, out_shape=jax.ShapeDtypeStruct((8,) + vec.shape, vec.dtype),
        scratch_shapes=[pltpu.SemaphoreType.DMA, pltpu.SemaphoreType.DMA((7,)), pltpu.SemaphoreType.DMA((7,))],
        name="gather_all")(vec)


def _rope_tables():
    pos = (jnp.arange(R, dtype=jnp.int32) - PAD).astype(F32)
    half = HD // 2
    inv = ROPE_THETA ** (-jnp.arange(half, dtype=F32) / half)
    ang = pos[:, None] * inv[None, :]
    cos, sin = jnp.cos(ang), jnp.sin(ang)
    cs = jnp.tile(cos, (1, 4))
    sn = jnp.tile(jnp.concatenate([-sin, sin], axis=1), (1, 2))
    return cs, sn


def _perm_cols(w):
    return jnp.concatenate([w[:, 0:512], w[:, 768:1280], w[:, 1280:2304], w[:, 2304:2816], w[:, 512:640],
                            w[:, 640:768]], axis=1)


def _unperm_cols(w):
    return jnp.concatenate([w[:, C_Q:C_Q + 512], w[:, C_K:C_K + 128], w[:, C_V:C_V + 128], w[:, C_GA:C_GA + 512],
                            w[:, C_GLU:C_GLU + 1024], w[:, C_GB:C_GB + 512]], axis=1)


def _local_step(x, target, p):
    w0 = _perm_cols(p["ab_w_in"])
    conv_w = jnp.concatenate([p["ab_conv_w"], jnp.zeros((1, CC), F32)], axis=0)
    cs, sn = _rope_tables()

    h0 = jnp.concatenate([jnp.zeros((PAD, D), F32), p["meta_tokens"], x], axis=0)

    xn0 = _rms_fwd(h0, p["ab_pre_norm"], "rms_fwd0")
    z0 = _mm([(xn0, w0)], F32, "in_proj0", 544, 1408)
    plan = _gather_plan([p["sb_w_in"], p["sb_w_out"], p["ab_w_out"], p["ab_w_pw2"]])
    ((o0, a0, lse0), (cv0, s0)), gathered = _rows_call(
        "fwd0", [_swa_fwd(z0, cs, sn, p["ab_sinks"]),
                 _conv_fwd(z0, conv_w, p["ab_conv_b"], p["ab_conv_ln_g"], p["ab_conv_ln_b"])], plan)
    w1, wo1, wo0, wpw = plan["post"](gathered)
    wo1, wo0, wpw = wo1.reshape(D, D), wo0.reshape(D, D), wpw.reshape(CC, CC)
    t0 = _mm([(s0, wpw)], F32, "pw2", 544, 512)
    c0 = _gate_fwd(t0, z0, C_GB, "gate_b_fwd")
    wo0h = wo0.reshape(2, CC, D)
    y0 = _mm([(a0, (wo0h, 0)), (c0, (wo0h, 1))], F32, "out_proj0", 544, 1024)

    h1, xn1 = _post_rms_fwd(h0, y0, p["ab_post_norm"], p["sb_pre_norm"], "post_rms_fwd")
    q1 =_mm([(xn1, (w1, 0))], BF16, "in_proj1_q", 544, 1024)
    k1 = _mm([(xn1, (w1, 1))], BF16, "in_proj1_k", 544, 1024)
    v1 = _mm([(xn1, (w1, 2))], BF16, "in_proj1_v", 544, 1024)
    g1 = _mm([(xn1, (w1, 3))], F32, "in_proj1_g", 544, 1024)
    o1, m1, car1, trips1 = _sb_fwd(q1, k1, v1, g1)
    y1 = _mm([(m1, wo1)], F32, "out_proj1", 544, 1024)

    dh2, dy1, d_sb_post, loss_row = _tail(h1, y1, p["sb_post_norm"], target)

    dm1 = _mm([(dy1, wo1)], F32, "out_proj1_dx", 544, 1024, tb=True)
    d_wo1 = _mm([(m1, dy1)], BF16, "out_proj1_dw", 512, 1024, ta=True)
    dq1, dk1, dv1, dg1 = _sb_bwd(trips1, q1, k1, v1, car1, dm1, g1, o1)
    dz1 = [dq1, dk1, dv1, dg1]
    dxn1 = _mm([(dz1[j], (w1, j)) for j in range(4)], F32, "in_proj1_dx", 544, 1024, tb=True)
    d_w1 = jnp.stack([_mm([(xn1, dz1[j])], BF16, "in_proj1_dw%d" % j, 512, 1024, ta=True) for j in range(4)])

    dh1, d_sb_pre, dy0, d_ab_post = _rms_post_bwd(dxn1, h1, p["sb_pre_norm"], dh2, y0, p["ab_post_norm"],
                                                  "rms_post_bwd")
    dmix0 = _mm([(dy0, wo0)], F32, "out_proj0_dx", 544, 1024, tb=True)
    d_wo0 = jnp.concatenate([_mm([(a0, dy0)], BF16, "out_proj0_dw_a", 512, 1024, ta=True),
                             _mm([(c0, dy0)], BF16, "out_proj0_dw_b", 512, 1024, ta=True)], axis=0)
    dt0, dgb0 = _gate_bwd(dmix0, 512, t0, z0, C_GB, "gate_b_bwd")
    ds0 = _mm([(dt0, wpw)], F32, "pw2_dx", 544, 512, tb=True)
    d_wpw = _mm([(s0, dt0)], BF16, "pw2_dw", 512, 512, ta=True)
    early = ("sb_w_in", "sb_w_out", "ab_w_out", "ab_w_pw2")
    own1, got1 = _pair_exchange([d_w1, d_wo1.reshape(N_CHIPS, 256, D), d_wo0.reshape(N_CHIPS, 256, D),
                                 d_wpw.reshape(N_CHIPS, 128, CC)], "pair_exchange1")
    pair1 = [_sum_pair(o, t, BF16, "sum_pair_" + nm) for o, t, nm in zip(own1, got1, early)]
    plan = _scatter_plan([pr[1] for pr in pair1])
    ((dglu0, d_convw, d_small), (dq0, dga0, dkv0, d_sinks)), land1 = _rows_call(
        "bwd0", [_conv_bwd(ds0, cv0, z0, conv_w, p["ab_conv_ln_g"], p["ab_conv_ln_b"]),
                 _swa_bwd(z0, cs, sn, p["ab_sinks"], o0, dmix0, lse0)], plan)
    halves1 = [_sum_shard(mi, la, "sum_shard_" + nm)
               for mi, la, nm in zip(_own_slab([pr[0] for pr in pair1]), land1, early)]
    dz0 = jnp.concatenate([dq0, dga0, dglu0, dgb0, dkv0], axis=1)
    dxn0 = _mm([(dz0, w0)], F32, "in_proj0_dx", 544, 1024, tb=True)
    d_w0 = _unperm_cols(_mm([(xn0, dz0)], BF16, "in_proj0_dw", 512, 1408, ta=True))
    dh0_first, grad_x, d_ab_pre = _rms_bwd(dxn0, h0, p["ab_pre_norm"], dh1, F32, "rms_bwd0", split=True)

    grads = {
        "meta_tokens": dh0_first[PAD:TB], "ab_pre_norm": d_ab_pre, "ab_w_in": d_w0, "ab_sinks": d_sinks[0:1, 0:8],
        "ab_conv_w": d_convw[0:CONV_W], "ab_conv_b": d_small[0:1], "ab_conv_ln_g": d_small[1:2],
        "ab_conv_ln_b": d_small[2:3], "ab_w_pw2": d_wpw, "ab_w_out": d_wo0, "ab_post_norm": d_ab_post,
        "sb_pre_norm": d_sb_pre, "sb_post_norm": d_sb_post,
    }
    return loss_row, grad_x, grads, halves1


SMALL_ROWS = 80
REP_ROWS = 32

WEIGHTS = ["meta_tokens", "ab_pre_norm", "ab_w_in", "ab_sinks", "ab_conv_w", "ab_conv_b", "ab_conv_ln_g",
           "ab_conv_ln_b", "ab_w_pw2", "ab_w_out", "ab_post_norm", "sb_pre_norm", "sb_w_in", "sb_w_out",
           "sb_post_norm"]
BIG = ["ab_w_in", "ab_w_out", "ab_w_pw2", "sb_w_in", "sb_w_out"]


def _pack_small(conv_w, meta, sb_pre, sb_post):
    rows = jnp.concatenate([conv_w, meta.reshape(32, 128), sb_pre.reshape(2, 128), sb_post.reshape(2, 128)], axis=0)
    return jnp.concatenate([rows, jnp.zeros((SMALL_ROWS - rows.shape[0], 128), F32)], axis=0)


def _unpack_small(s):
    return s[0:31], s[31:63].reshape(16, 256), s[63:65].reshape(1, 256), s[65:67].reshape(1, 256)


REP_LOSS = 3592


def _pack_rep(pre, post, conv_b, ln_g, ln_b, sinks, extra=None):
    flat = jnp.concatenate([pre.reshape(-1), post.reshape(-1), conv_b.reshape(-1), ln_g.reshape(-1),
                            ln_b.reshape(-1), sinks.reshape(-1)] + ([] if extra is None else [extra.reshape(-1)]))
    flat = jnp.concatenate([flat, jnp.zeros((REP_ROWS * 128 - flat.shape[0],), F32)])
    return flat.reshape(REP_ROWS, 128)


def _unpack_rep(r):
    f = r.reshape(-1)
    return (f[0:1024].reshape(1, 1024), f[1024:2048].reshape(1, 1024), f[2048:2560].reshape(1, 512),
            f[2560:3072].reshape(1, 512), f[3072:3584].reshape(1, 512), f[3584:3592].reshape(1, 8))


def _cols_to_chips(w, width):
    return w.reshape(w.shape[0], N_CHIPS, width).transpose(1, 0, 2)


def _chips_to_cols(w):
    return w.transpose(1, 0, 2).reshape(w.shape[1], -1)


def kernel(x, meta_tokens, ab_pre_norm, ab_w_in, ab_sinks, ab_conv_w, ab_conv_b, ab_conv_ln_g, ab_conv_ln_b, ab_w_pw2, ab_w_out, ab_post_norm, sb_pre_norm, sb_w_in, sb_w_out, sb_post_norm, loss_target, m_meta_tokens, m_ab_pre_norm, m_ab_w_in, m_ab_sinks, m_ab_conv_w, m_ab_conv_b, m_ab_conv_ln_g, m_ab_conv_ln_b, m_ab_w_pw2, m_ab_w_out, m_ab_post_norm, m_sb_pre_norm, m_sb_w_in, m_sb_w_out, m_sb_post_norm, v_meta_tokens, v_ab_pre_norm, v_ab_w_in, v_ab_sinks, v_ab_conv_w, v_ab_conv_b, v_ab_conv_ln_g, v_ab_conv_ln_b, v_ab_w_pw2, v_ab_w_out, v_ab_post_norm, v_sb_pre_norm, v_sb_w_in, v_sb_w_out, v_sb_post_norm):
    w = dict(meta_tokens=meta_tokens, ab_pre_norm=ab_pre_norm, ab_w_in=ab_w_in, ab_sinks=ab_sinks,
             ab_conv_w=ab_conv_w, ab_conv_b=ab_conv_b, ab_conv_ln_g=ab_conv_ln_g, ab_conv_ln_b=ab_conv_ln_b,
             ab_w_pw2=ab_w_pw2, ab_w_out=ab_w_out, ab_post_norm=ab_post_norm, sb_pre_norm=sb_pre_norm,
             sb_w_in=sb_w_in, sb_w_out=sb_w_out, sb_post_norm=sb_post_norm)
    m = dict(meta_tokens=m_meta_tokens, ab_pre_norm=m_ab_pre_norm, ab_w_in=m_ab_w_in, ab_sinks=m_ab_sinks,
             ab_conv_w=m_ab_conv_w, ab_conv_b=m_ab_conv_b, ab_conv_ln_g=m_ab_conv_ln_g,
             ab_conv_ln_b=m_ab_conv_ln_b, ab_w_pw2=m_ab_w_pw2, ab_w_out=m_ab_w_out, ab_post_norm=m_ab_post_norm,
             sb_pre_norm=m_sb_pre_norm, sb_w_in=m_sb_w_in, sb_w_out=m_sb_w_out, sb_post_norm=m_sb_post_norm)
    v = dict(meta_tokens=v_meta_tokens, ab_pre_norm=v_ab_pre_norm, ab_w_in=v_ab_w_in, ab_sinks=v_ab_sinks,
             ab_conv_w=v_ab_conv_w, ab_conv_b=v_ab_conv_b, ab_conv_ln_g=v_ab_conv_ln_g,
             ab_conv_ln_b=v_ab_conv_ln_b, ab_w_pw2=v_ab_w_pw2, ab_w_out=v_ab_w_out, ab_post_norm=v_ab_post_norm,
             sb_pre_norm=v_sb_pre_norm, sb_w_in=v_sb_w_in, sb_w_out=v_sb_w_out, sb_post_norm=v_sb_post_norm)

    def small_of(d):
        return _pack_small(d["ab_conv_w"][0], d["meta_tokens"], d["sb_pre_norm"], d["sb_post_norm"])

    def rep_of(d):
        return _pack_rep(d["ab_pre_norm"], d["ab_post_norm"], d["ab_conv_b"], d["ab_conv_ln_g"], d["ab_conv_ln_b"],
                         d["ab_sinks"])

    g_in0, g_small = _gather_chips([ab_w_in[0].astype(BF16), small_of(w)])
    conv_w_f = _chips_to_cols(g_small[:, 0:31])
    meta_f = _chips_to_cols(g_small[:, 31:63].reshape(N_CHIPS, 16, 256))
    sb_pre_f = g_small[:, 63:65].reshape(1, D)
    sb_post_f = g_small[:, 65:67].reshape(1, D)
    full = {
        "meta_tokens": meta_f, "ab_pre_norm": ab_pre_norm, "ab_w_in": _chips_to_cols(g_in0),
        "ab_sinks": ab_sinks, "ab_conv_w": conv_w_f, "ab_conv_b": ab_conv_b, "ab_conv_ln_g": ab_conv_ln_g,
        "ab_conv_ln_b": ab_conv_ln_b, "ab_w_pw2": ab_w_pw2[0].astype(BF16), "ab_w_out": ab_w_out[0].astype(BF16),
        "ab_post_norm": ab_post_norm, "sb_pre_norm": sb_pre_f, "sb_w_in": sb_w_in[0].astype(BF16),
        "sb_w_out": sb_w_out[0].astype(BF16), "sb_post_norm": sb_post_f,
    }

    loss_row, grad_x, g, halves1 = _local_step(x[0], loss_target[0], full)

    send = [_cols_to_chips(g["ab_w_in"], 704)]
    gs_conv = _cols_to_chips(g["ab_conv_w"], 128)
    gs_meta = _cols_to_chips(g["meta_tokens"], 256)
    gs_pre = g["sb_pre_norm"].reshape(N_CHIPS, 1, 256)
    gs_post = g["sb_post_norm"].reshape(N_CHIPS, 1, 256)
    send.append(jnp.stack([_pack_small(gs_conv[j], gs_meta[j], gs_pre[j], gs_post[j]) for j in range(N_CHIPS)]))
    names = ["ab_w_in", "small"]
    own, got = _pair_exchange(send, "pair_exchange0")
    pair = [_sum_pair(o, t, o.dtype, "sum_pair_" + nm) for o, t, nm in zip(own, got, names)]
    mine, land = _scatter_chips([pr[0] for pr in pair], [pr[1] for pr in pair])
    halves = [_sum_shard(mi, la, "sum_shard_" + nm) for mi, la, nm in zip(mine, land, names)]
    h_sb_in, h_sb_out, h_ab_out, h_pw2 = halves1
    total = _join_cores([halves[0], h_ab_out, h_pw2, h_sb_in, h_sb_out, halves[1]])

    rep_g = _pack_rep(g["ab_pre_norm"], g["ab_post_norm"], g["ab_conv_b"], g["ab_conv_ln_g"], g["ab_conv_ln_b"],
                      g["ab_sinks"], loss_row[0:1, 0:1])
    rep_sum = _sum8(_gather_all(rep_g), "sum8_rep")
    loss = rep_sum.reshape(-1)[REP_LOSS]

    out_g, out_d, out_m, out_v = {}, {}, {}, {}
    for i, k in enumerate(BIG):
        shp = w[k].shape
        res = _adamw(w[k][0], [total[i]], m[k][0], v[k][0], "adamw_" + k)
        out_g[k], out_d[k], out_m[k], out_v[k] = [r.reshape(shp) for r in res]
    res = _adamw(small_of(w), [total[5]], small_of(m), small_of(v), "adamw_small")
    for dst, r in zip((out_g, out_d, out_m, out_v), res):
        cw, mt, pre, post = _unpack_small(r)
        dst["ab_conv_w"], dst["meta_tokens"], dst["sb_pre_norm"], dst["sb_post_norm"] = cw[None], mt, pre, post
    res = _adamw(rep_of(w), [rep_sum], rep_of(m), rep_of(v), "adamw_rep")
    for dst, r in zip((out_g, out_d, out_m, out_v), res):
        (dst["ab_pre_norm"], dst["ab_post_norm"], dst["ab_conv_b"], dst["ab_conv_ln_g"], dst["ab_conv_ln_b"],
         dst["ab_sinks"]) = _unpack_rep(r)

    return (loss, grad_x[None], *[out_g[k] for k in WEIGHTS], *[out_d[k] for k in WEIGHTS],
            *[out_m[k] for k in WEIGHTS], *[out_v[k] for k in WEIGHTS])
```

```python
import functools

import jax
import jax.numpy as jnp
from jax import lax
from jax.experimental import pallas as pl
from jax.experimental.pallas import tpu as pltpu

F32 = jnp.float32
BF16 = jnp.bfloat16

D = 1024
SEQ = 2048
N_META = 16
TB = 128
TR = 272
PAD = TB - N_META
R = SEQ + TB
NB = R // TB
HD = 64
ROPE_THETA = 10000.0
NORM_EPS = 1e-6
LN_EPS = 1e-5
NEG = -1e30
SWA_HEADS = 8
CONV_W = 31
SCALE = HD ** -0.5
N_CHIPS = 4

C_Q, C_GA, C_GLU, C_GB, C_K, C_V = 0, 512, 1024, 2048, 2560, 2688
AB_IN = 2816

ADAM_LR, ADAM_B1, ADAM_B2, ADAM_EPS, ADAM_WD, ADAM_STEP = 0.001, 0.9, 0.999, 1e-08, 0.01, 10

VMEM_LIMIT = 56 * 1024 * 1024


def _cp(sem):
    return pltpu.CompilerParams(dimension_semantics=sem, vmem_limit_bytes=VMEM_LIMIT)


def _sig(x):
    return 1.0 / (1.0 + jnp.exp(-x))


def _dot(a, b):
    return lax.dot_general(a, b, (((1,), (0,)), ((), ())), preferred_element_type=F32)


def _dot_nt(a, b):
    return lax.dot_general(a, b, (((1,), (1,)), ((), ())), preferred_element_type=F32)


def _dot_tn(a, b):
    return lax.dot_general(a, b, (((0,), (0,)), ((), ())), preferred_element_type=F32)


def _mm(pairs, out_dtype, name, tm, tn, ta=False, tb=False, plan=None):
    pairs = [(a, b if isinstance(b, tuple) else (b, None)) for a, b in pairs]
    a0, (b0, _) = pairs[0]
    m = a0.shape[1] if ta else a0.shape[0]
    n = b0.shape[-2] if tb else b0.shape[-1]
    npairs = len(pairs)
    dims = (((0 if ta else 1,), (1 if tb else 0,)), ((), ()))
    c_in = len(plan["args"]) if plan else 0
    c_out = len(plan["out_shape"]) if plan else 0
    steps = (m // tm) * (n // tn)

    def body(*refs):
        o_ref = refs[2 * npairs + c_in]
        if plan:
            comm = (refs[2 * npairs:2 * npairs + c_in], refs[2 * npairs + c_in + 1:2 * npairs + c_in + 1 + c_out],
                    refs[2 * npairs + c_in + 1 + c_out:])
            step = pl.program_id(0) * (n // tn) + pl.program_id(1)

            @pl.when(step == 0)
            def _():
                plan["start"](*comm)

        acc = None
        for i in range(npairs):
            t = lax.dot_general(refs[2 * i][...].astype(BF16), refs[2 * i + 1][...].astype(BF16), dims,
                                preferred_element_type=F32)
            acc = t if acc is None else acc + t
        o_ref[...] = acc.astype(out_dtype)
        if plan:
            @pl.when(step == steps - 2)
            def _():
                plan["mid"](*comm)

            @pl.when(step == steps - 1)
            def _():
                plan["finish"](*comm)

    in_specs, args = [], []
    for a, (b, sel) in pairs:
        k = a.shape[0] if ta else a.shape[1]
        in_specs.append(pl.BlockSpec((k, tm), lambda i, j: (0, i)) if ta else pl.BlockSpec((tm, k), lambda i, j: (i, 0)))
        bshape, bidx = ((tn, k), lambda i, j: (j, 0)) if tb else ((k, tn), lambda i, j: (0, j))
        if sel is None:
            in_specs.append(pl.BlockSpec(bshape, bidx))
        else:
            in_specs.append(pl.BlockSpec((None,) + bshape, functools.partial(lambda i, j, f, s: (s,) + f(i, j), f=bidx, s=sel)))
        args += [a, b]
    out_spec = pl.BlockSpec((tm, tn), lambda i, j: (i, j))
    out_shape = jax.ShapeDtypeStruct((m, n), out_dtype)
    if not plan:
        return pl.pallas_call(
            body, grid=(m // tm, n // tn), in_specs=in_specs, out_specs=out_spec, out_shape=out_shape, name=name,
            compiler_params=_cp(("parallel", "parallel")))(*args)
    assert steps >= 2
    res = pl.pallas_call(
        body, grid=(m // tm, n // tn), in_specs=in_specs + [ANY] * c_in, out_specs=[out_spec] + [ANY] * c_out,
        out_shape=[out_shape] + plan["out_shape"], scratch_shapes=plan["sems"], name=name,
        compiler_params=_cp(("arbitrary", "arbitrary")))(*args, *plan["args"])
    return res[0], res[1:]


def _rms_fwd(h, g, name):
    def body(h_ref, g_ref, o_ref):
        x = h_ref[...]
        r = lax.rsqrt(jnp.mean(x * x, axis=1, keepdims=True) + NORM_EPS)
        o_ref[...] = (x * r * g_ref[...]).astype(BF16)

    return pl.pallas_call(
        body, grid=(R // TR,),
        in_specs=[pl.BlockSpec((TR, D), lambda n: (n, 0)), pl.BlockSpec((1, D), lambda n: (0, 0))],
        out_specs=pl.BlockSpec((TR, D), lambda n: (n, 0)),
        out_shape=jax.ShapeDtypeStruct((R, D), BF16), name=name, compiler_params=_cp(("parallel",)))(h, g)


def _rms_bwd(dout, x, g, res, out_dtype, name, split=False):
    has_res = res is not None

    def body(*refs):
        if split:
            refs = list(refs)
            dx_rest_ref = refs.pop(-2)
        if has_res:
            d_ref, x_ref, g_ref, r_ref, dx_ref, dg_ref = refs
        else:
            d_ref, x_ref, g_ref, dx_ref, dg_ref = refs
        n = pl.program_id(0)
        xv = x_ref[...]
        dv = d_ref[...]
        r = lax.rsqrt(jnp.mean(xv * xv, axis=1, keepdims=True) + NORM_EPS)
        xh = xv * r
        dxh = dv * g_ref[...]
        dx = r * (dxh - xh * jnp.mean(dxh * xh, axis=1, keepdims=True))
        if has_res:
            dx = dx + r_ref[...]
        row = lax.broadcasted_iota(jnp.int32, (TB, D), 0) + n * TB
        dx = jnp.where(row >= PAD, dx, 0.0).astype(out_dtype)
        if split:
            @pl.when(n == 0)
            def _():
                dx_ref[...] = dx

            @pl.when(n > 0)
            def _():
                dx_rest_ref[...] = dx
        else:
            dx_ref[...] = dx

        @pl.when(n == 0)
        def _():
            dg_ref[...] = jnp.zeros_like(dg_ref)

        dg_ref[...] += jnp.sum(dv * xh, axis=0, keepdims=True)

    blk = pl.BlockSpec((TB, D), lambda n: (n, 0))
    vec = pl.BlockSpec((1, D), lambda n: (0, 0))
    ins = [dout, x, g] + ([res] if has_res else [])
    in_specs = [blk, blk, vec] + ([blk] if has_res else [])
    if split:
        out_specs = [pl.BlockSpec((TB, D), lambda n: (0, 0)), pl.BlockSpec((TB, D), lambda n: (jnp.maximum(n - 1, 0), 0)), vec]
        out_shape = [jax.ShapeDtypeStruct((TB, D), out_dtype), jax.ShapeDtypeStruct((SEQ, D), out_dtype),
                     jax.ShapeDtypeStruct((1, D), F32)]
    else:
        out_specs = [blk, vec]
        out_shape = [jax.ShapeDtypeStruct((R, D), out_dtype), jax.ShapeDtypeStruct((1, D), F32)]
    return pl.pallas_call(
        body, grid=(NB,), in_specs=in_specs, out_specs=out_specs, out_shape=out_shape,
        name=name, compiler_params=_cp(("arbitrary",)))(*ins)


def _post_rms_fwd(h, y, g_post, g_next, name):
    def body(h_ref, y_ref, gp_ref, gn_ref, o_ref, x_ref):
        yv = y_ref[...]
        r = lax.rsqrt(jnp.mean(yv * yv, axis=1, keepdims=True) + NORM_EPS)
        hn = h_ref[...] + yv * r * gp_ref[...]
        o_ref[...] = hn
        r2 = lax.rsqrt(jnp.mean(hn * hn, axis=1, keepdims=True) + NORM_EPS)
        x_ref[...] = (hn * r2 * gn_ref[...]).astype(BF16)

    blk = pl.BlockSpec((TR, D), lambda n: (n, 0))
    vec = pl.BlockSpec((1, D), lambda n: (0, 0))
    return pl.pallas_call(
        body, grid=(R // TR,), in_specs=[blk, blk, vec, vec], out_specs=[blk, blk],
        out_shape=[jax.ShapeDtypeStruct((R, D), F32), jax.ShapeDtypeStruct((R, D), BF16)],
        name=name, compiler_params=_cp(("parallel",)))(h, y, g_post, g_next)


def _rms_post_bwd(dxn, h, g, res, y, g_post, name):
    def body(d_ref, h_ref, g_ref, r_ref, y_ref, gp_ref, dh_ref, dg_ref, dy_ref, dgp_ref):
        n = pl.program_id(0)

        @pl.when(n == 0)
        def _():
            dg_ref[...] = jnp.zeros_like(dg_ref)
            dgp_ref[...] = jnp.zeros_like(dgp_ref)

        hv, dv = h_ref[...], d_ref[...]
        r = lax.rsqrt(jnp.mean(hv * hv, axis=1, keepdims=True) + NORM_EPS)
        xh = hv * r
        dxh = dv * g_ref[...]
        dh = r * (dxh - xh * jnp.mean(dxh * xh, axis=1, keepdims=True)) + r_ref[...]
        row = lax.broadcasted_iota(jnp.int32, (TR, D), 0) + n * TR
        dh = jnp.where(row >= PAD, dh, 0.0)
        dh_ref[...] = dh
        dg_ref[...] += jnp.sum(dv * xh, axis=0, keepdims=True)
        yv = y_ref[...]
        ry = lax.rsqrt(jnp.mean(yv * yv, axis=1, keepdims=True) + NORM_EPS)
        yh = yv * ry
        dyh = dh * gp_ref[...]
        dy_ref[...] = (ry * (dyh - yh * jnp.mean(dyh * yh, axis=1, keepdims=True))).astype(BF16)
        dgp_ref[...] += jnp.sum(dh * yh, axis=0, keepdims=True)

    blk = pl.BlockSpec((TR, D), lambda n: (n, 0))
    vec = pl.BlockSpec((1, D), lambda n: (0, 0))
    return pl.pallas_call(
        body, grid=(R // TR,), in_specs=[blk, blk, vec, blk, blk, vec], out_specs=[blk, vec, blk, vec],
        out_shape=[jax.ShapeDtypeStruct((R, D), F32), jax.ShapeDtypeStruct((1, D), F32),
                   jax.ShapeDtypeStruct((R, D), BF16), jax.ShapeDtypeStruct((1, D), F32)],
        name=name, compiler_params=_cp(("arbitrary",)))(dxn, h, g, res, y, g_post)


GW = 512


def _gate_fwd(o, gsrc, goff, name):
    w = o.shape[1]

    def body(o_ref, g_ref, m_ref):
        gv = g_ref[...]
        m_ref[...] = (o_ref[...] * (gv * _sig(gv))).astype(BF16)

    gb = goff // GW
    return pl.pallas_call(
        body, grid=(R // TR, w // GW),
        in_specs=[pl.BlockSpec((TR, GW), lambda n, j: (n, j)), pl.BlockSpec((TR, GW), lambda n, j: (n, gb + j))],
        out_specs=pl.BlockSpec((TR, GW), lambda n, j: (n, j)),
        out_shape=jax.ShapeDtypeStruct((R, w), BF16), name=name,
        compiler_params=_cp(("parallel", "parallel")))(o, gsrc)


def _gate_bwd(dsrc, doff, o, gsrc, goff, name):
    w = o.shape[1]

    def body(d_ref, o_ref, g_ref, do_ref, dg_ref):
        gv = g_ref[...]
        dv = d_ref[...]
        s = _sig(gv)
        do_ref[...] = dv * (gv * s)
        dg_ref[...] = (dv * o_ref[...] * (s * (1.0 + gv * (1.0 - s)))).astype(BF16)

    db, gb = doff // GW, goff // GW
    blk = pl.BlockSpec((TR, GW), lambda n, j: (n, j))
    return pl.pallas_call(
        body, grid=(R // TR, w // GW),
        in_specs=[pl.BlockSpec((TR, GW), lambda n, j: (n, db + j)), blk,
                  pl.BlockSpec((TR, GW), lambda n, j: (n, gb + j))],
        out_specs=[blk, blk],
        out_shape=[jax.ShapeDtypeStruct((R, w), F32), jax.ShapeDtypeStruct((R, w), BF16)], name=name,
        compiler_params=_cp(("parallel", "parallel")))(dsrc, o, gsrc)


def _tail(h, y, g, target):
    def body(h_ref, y_ref, g_ref, t_ref, d_ref, dy_ref, dg_ref, l_ref):
        n = pl.program_id(0)

        @pl.when(n == 0)
        def _():
            d_ref[...] = jnp.zeros_like(d_ref)
            dy_ref[...] = jnp.zeros_like(dy_ref)
            dg_ref[...] = jnp.zeros_like(dg_ref)
            l_ref[...] = jnp.zeros_like(l_ref)

        @pl.when(n > 0)
        def _():
            yv = y_ref[...]
            r = lax.rsqrt(jnp.mean(yv * yv, axis=1, keepdims=True) + NORM_EPS)
            yh = yv * r
            err = (h_ref[...] + yh * g_ref[...]) - t_ref[...]
            dv = err * (1.0 / D)
            d_ref[...] = dv
            l_ref[...] += jnp.sum(err * err, axis=0, keepdims=True)
            dyh = dv * g_ref[...]
            dy_ref[...] = (r * (dyh - yh * jnp.mean(dyh * yh, axis=1, keepdims=True))).astype(BF16)
            dg_ref[...] += jnp.sum(dv * yh, axis=0, keepdims=True)

        @pl.when(n == NB - 1)
        def _():
            tot = jnp.sum(l_ref[...], axis=1, keepdims=True) * (0.5 / D)
            l_ref[...] = jnp.broadcast_to(tot, (1, D))

    blk = pl.BlockSpec((TB, D), lambda n: (n, 0))
    vec = pl.BlockSpec((1, D), lambda n: (0, 0))
    return pl.pallas_call(
        body, grid=(NB,),
        in_specs=[blk, blk, vec, pl.BlockSpec((TB, D), lambda n: (jnp.maximum(n - 1, 0), 0))],
        out_specs=[blk, blk, vec, vec],
        out_shape=[jax.ShapeDtypeStruct((R, D), F32), jax.ShapeDtypeStruct((R, D), BF16),
                   jax.ShapeDtypeStruct((1, D), F32), jax.ShapeDtypeStruct((1, D), F32)],
        name="tail", compiler_params=_cp(("arbitrary",)))(h, y, g, target)


def _lane_row(shape):
    return lax.broadcasted_iota(jnp.int32, shape, 1), lax.broadcasted_iota(jnp.int32, shape, 0)


def _rot_half(x, lane):
    return jnp.where(lane % HD < HD // 2, pltpu.roll(x, 128 - HD // 2, 1), pltpu.roll(x, HD // 2, 1))


def _swa_blocks(n):
    return (0, jnp.maximum(n - 1, 0), n)


SWA_STACKS = ((0, 0), (0, 1), (1, 0), (1, 1))


def _swa_masks(n, lane, row):
    qpos = n * TB + (row & (TB - 1))
    kp = (n - 1) * TB + lane
    kc = n * TB + lane
    m0 = (lane >= PAD) & (qpos - lane >= TB)
    mp = (kp >= PAD) & (qpos >= kp) & (qpos - kp < TB)
    mc = (kc >= PAD) & (qpos >= kc)
    return (m0, mp, mc)


def _stack_pair(xa, xb, par):
    lane = lax.broadcasted_iota(jnp.int32, (TB, 128), 1)
    keep = (lane < HD) if par == 0 else (lane >= HD)
    return jnp.concatenate([jnp.where(keep, xa, 0.0), jnp.where(keep, xb, 0.0)], axis=0)


def _per_head(a, b):
    row = lax.broadcasted_iota(jnp.int32, (2 * TB, 1), 0)
    return jnp.where(row < TB, a, b)


def _swa_load(n, zq_ref, zkv_ref, cs_ref, sn_ref, lane):
    r0 = pl.multiple_of(n * TB, TB)
    csq, snq = cs_ref[pl.ds(r0, TB), :], sn_ref[pl.ds(r0, TB), :]
    qc = []
    for c in range(4):
        x = zq_ref[:, c * 128:(c + 1) * 128]
        qc.append((x * csq + _rot_half(x, lane) * snq) * SCALE)
    qst = [_stack_pair(qc[2 * g], qc[2 * g + 1], par).astype(BF16) for g, par in SWA_STACKS]
    kvs = []
    for b in _swa_blocks(n):
        b0 = pl.multiple_of(b * TB, TB)
        csb, snb = cs_ref[pl.ds(b0, TB), :], sn_ref[pl.ds(b0, TB), :]
        kx = zkv_ref[pl.ds(b0, TB), 0:128]
        kr = kx * csb + _rot_half(kx, lane) * snb
        vx = zkv_ref[pl.ds(b0, TB), 128:256]
        kvs.append((kr.astype(BF16), pltpu.roll(kr, HD, 1).astype(BF16),
                    vx.astype(BF16), pltpu.roll(vx, HD, 1).astype(BF16), csb, snb, b0))
    return qst, (csq, snq), kvs


def _swa_fwd(z0, cs, sn, sinks):
    def body(zq_ref, zkv_ref, cs_ref, sn_ref, sk_ref, ga_ref, o_ref, a_ref, lse_ref):
        n = pl.program_id(0)
        lane, row = _lane_row((TB, 128))
        lo = lane < HD
        masks = _swa_masks(n, *_lane_row((2 * TB, 128)))
        qst, _, kvs = _swa_load(n, zq_ref, zkv_ref, cs_ref, sn_ref, lane)
        ss = [[jnp.where(m, _dot_nt(qst[si], k if par == g else ka), NEG)
               for (k, ka, _, _, _, _, _), m in zip(kvs, masks)] for si, (g, par) in enumerate(SWA_STACKS)]
        o2, lse2 = [], []
        for si, (g, par) in enumerate(SWA_STACKS):
            sink = _per_head(sk_ref[0, 4 * g + par], sk_ref[0, 4 * g + 2 + par])
            s = ss[si]
            mx = jnp.maximum(jnp.maximum(jnp.max(s[0], axis=1, keepdims=True), jnp.max(s[1], axis=1, keepdims=True)),
                             jnp.max(s[2], axis=1, keepdims=True))
            mx = jnp.maximum(mx, sink)
            es = [jnp.exp(sb - mx) for sb in s]
            den = (jnp.sum(es[0], axis=1, keepdims=True) + jnp.sum(es[1], axis=1, keepdims=True)
                   + jnp.sum(es[2], axis=1, keepdims=True) + jnp.exp(sink - mx))
            inv = 1.0 / den
            t = jnp.zeros((2 * TB, 128), F32)
            for (_, _, v, va, _, _, _), e in zip(kvs, es):
                t = t + _dot((e * inv).astype(BF16), v if par == g else va)
            o2.append(t)
            lse2.append(mx + jnp.log(den))
        lse_t = jnp.zeros((TB, 128), F32)
        for g in range(2):
            for t in range(2):
                rows = slice(t * TB, (t + 1) * TB)
                c = 2 * g + t
                oc = jnp.where(lo, o2[2 * g][rows], o2[2 * g + 1][rows])
                o_ref[:, c * 128:(c + 1) * 128] = oc
                gv = ga_ref[:, c * 128:(c + 1) * 128]
                a_ref[:, c * 128:(c + 1) * 128] = (oc * (gv * _sig(gv))).astype(BF16)
                for par in range(2):
                    lse_t = jnp.where(lane == 4 * g + 2 * t + par, lse2[2 * g + par][rows], lse_t)
        lse_ref[...] = lse_t

    full = pl.BlockSpec((R, 128), lambda n: (0, 0))
    return dict(
        body=body,
        in_specs=[pl.BlockSpec((TB, 512), lambda n: (n, C_Q // 512)),
                  pl.BlockSpec((R, 256), lambda n: (0, C_K // 256)), full, full,
                  pl.BlockSpec(memory_space=pltpu.SMEM), pl.BlockSpec((TB, 512), lambda n: (n, C_GA // 512))],
        args=[z0, z0, cs, sn, sinks, z0],
        out_specs=[pl.BlockSpec((TB, 512), lambda n: (n, 0)), pl.BlockSpec((TB, 512), lambda n: (n, 0)),
                   pl.BlockSpec((TB, 128), lambda n: (n, 0))],
        out_shape=[jax.ShapeDtypeStruct((R, 512), F32), jax.ShapeDtypeStruct((R, 512), BF16),
                   jax.ShapeDtypeStruct((R, 128), F32)],
        scratch=[])


def _swa_bwd(z0, cs, sn, sinks, o, dmix, lse):
    def body(zq_ref, zkv_ref, cs_ref, sn_ref, sk_ref, ga_ref, o_ref, dm_ref, lse_ref,
             dq_ref, dga_ref, dkv_ref, dsk_ref, do_ref, acc_ref):
        n = pl.program_id(0)

        @pl.when(n == 0)
        def _():
            acc_ref[...] = jnp.zeros_like(acc_ref)
            dsk_ref[...] = jnp.zeros_like(dsk_ref)

        gv, dmv = ga_ref[...], dm_ref[...]
        sg = _sig(gv)
        dga_ref[...] = (dmv * o_ref[...] * (sg * (1.0 + gv * (1.0 - sg)))).astype(BF16)
        do_ref[...] = dmv * (gv * sg)
        lane, row = _lane_row((TB, 128))
        lo = lane < HD
        masks = _swa_masks(n, *_lane_row((2 * TB, 128)))
        qst, (csq, snq), kvs = _swa_load(n, zq_ref, zkv_ref, cs_ref, sn_ref, lane)
        lse_t = lse_ref[...]
        ss = [[jnp.where(m, _dot_nt(qst[si], k if par == g else ka), NEG)
               for (k, ka, _, _, _, _, _), m in zip(kvs, masks)] for si, (g, par) in enumerate(SWA_STACKS)]
        dobs, deltas, lses, dps = [], [], [], []
        for g, par in SWA_STACKS:
            ca, cb = slice(2 * g * 128, (2 * g + 1) * 128), slice((2 * g + 1) * 128, (2 * g + 2) * 128)
            dom = _stack_pair(do_ref[:, ca], do_ref[:, cb], par)
            deltas.append(jnp.sum(dom * jnp.concatenate([o_ref[:, ca], o_ref[:, cb]], axis=0), axis=1, keepdims=True))
            dob = dom.astype(BF16)
            dobs.append(dob)
            lses.append(jnp.concatenate(
                [jnp.sum(jnp.where(lane == 4 * g + 2 * t + par, lse_t, 0.0), axis=1, keepdims=True) for t in range(2)],
                axis=0))
            dps.append([_dot_nt(dob, v if par == g else va) for (_, _, v, va, _, _, _) in kvs])
        dk_al = [jnp.zeros((TB, 128), F32) for _ in range(3)]
        dk_mis = [jnp.zeros((TB, 128), F32) for _ in range(3)]
        dv_al = [jnp.zeros((TB, 128), F32) for _ in range(3)]
        dv_mis = [jnp.zeros((TB, 128), F32) for _ in range(3)]
        dsk_t = jnp.zeros((TB, 128), F32)
        dq2 = []
        for si, (g, par) in enumerate(SWA_STACKS):
            dqt = jnp.zeros((2 * TB, 128), F32)
            for bi, (k, ka, _, _, _, _, _) in enumerate(kvs):
                p = jnp.exp(ss[si][bi] - lses[si])
                ds = (p * (dps[si][bi] - deltas[si])).astype(BF16)
                dqt = dqt + _dot(ds, k if par == g else ka)
                dkh = _dot_tn(ds, qst[si])
                dvh = _dot_tn(p.astype(BF16), dobs[si])
                if par == g:
                    dk_al[bi] = dk_al[bi] + dkh
                    dv_al[bi] = dv_al[bi] + dvh
                else:
                    dk_mis[bi] = dk_mis[bi] + dkh
                    dv_mis[bi] = dv_mis[bi] + dvh
            dq2.append(dqt)
            sink = _per_head(sk_ref[0, 4 * g + par], sk_ref[0, 4 * g + 2 + par])
            dsk = -jnp.exp(sink - lses[si]) * deltas[si]
            for t in range(2):
                dsk_t = jnp.where(lane == 4 * g + 2 * t + par, dsk[t * TB:(t + 1) * TB], dsk_t)
        for g in range(2):
            for t in range(2):
                rows = slice(t * TB, (t + 1) * TB)
                c = 2 * g + t
                dqc = jnp.where(lo, dq2[2 * g][rows], dq2[2 * g + 1][rows]) * SCALE
                dq_ref[:, c * 128:(c + 1) * 128] = (dqc * csq + _rot_half(dqc * snq, lane)).astype(BF16)
        for bi, (_, _, _, _, csb, snb, b0) in enumerate(kvs):
            dk = dk_al[bi] + pltpu.roll(dk_mis[bi], HD, 1)
            dv = dv_al[bi] + pltpu.roll(dv_mis[bi], HD, 1)
            acc_ref[pl.ds(b0, TB), 0:128] += dk * csb + _rot_half(dk * snb, lane)
            acc_ref[pl.ds(b0, TB), 128:256] += dv
        dsk_ref[0:1, :] += jnp.sum(dsk_t, axis=0, keepdims=True)

        @pl.when(n == NB - 1)
        def _():
            dkv_ref[...] = acc_ref[...].astype(BF16)

    full = pl.BlockSpec((R, 128), lambda n: (0, 0))
    b512 = pl.BlockSpec((TB, 512), lambda n: (n, 0))
    return dict(
        body=body,
        in_specs=[pl.BlockSpec((TB, 512), lambda n: (n, C_Q // 512)),
                  pl.BlockSpec((R, 256), lambda n: (0, C_K // 256)), full, full,
                  pl.BlockSpec(memory_space=pltpu.SMEM), pl.BlockSpec((TB, 512), lambda n: (n, C_GA // 512)),
                  b512, b512, pl.BlockSpec((TB, 128), lambda n: (n, 0))],
        args=[z0, z0, cs, sn, sinks, z0, o, dmix, lse],
        out_specs=[b512, b512, pl.BlockSpec((R, 256), lambda n: (0, 0)), pl.BlockSpec((8, 128), lambda n: (0, 0))],
        out_shape=[jax.ShapeDtypeStruct((R, 512), BF16), jax.ShapeDtypeStruct((R, 512), BF16),
                   jax.ShapeDtypeStruct((R, 256), BF16), jax.ShapeDtypeStruct((8, 128), F32)],
        scratch=[pltpu.VMEM((TB, 512), F32), pltpu.VMEM((R, 256), F32)])


CC = 512
HALO = CONV_W - 1


def _conv_fwd(z0, conv_w, conv_b, ln_g, ln_b):
    def body(g_ref, w_ref, cb_ref, lg_ref, lb_ref, cv_ref, s_ref, ubuf):
        n = pl.program_id(0)

        @pl.when(n == 0)
        def _():
            ubuf[...] = jnp.zeros_like(ubuf)

        u = g_ref[:, 0:CC] * _sig(g_ref[:, CC:2 * CC])
        for k in range(8):
            ubuf[k, 0:TB + 8, :] = ubuf[k, TB:2 * TB + 8, :]
            ubuf[k, pl.ds(TB + 8 - k, TB), :] = u
        acc = jnp.zeros((TB, CC), F32)
        for w in range(CONV_W):
            off = TB - HALO + w
            acc = acc + ubuf[off % 8, pl.ds(off + 8 - off % 8, TB), :] * w_ref[w:w + 1, :]
        cv = acc + cb_ref[...]
        cv_ref[...] = cv
        xc = cv - jnp.mean(cv, axis=1, keepdims=True)
        rs = lax.rsqrt(jnp.mean(xc * xc, axis=1, keepdims=True) + LN_EPS)
        ln = xc * rs * lg_ref[...] + lb_ref[...]
        s_ref[...] = (ln * _sig(ln)).astype(BF16)

    vec = pl.BlockSpec((1, CC), lambda n: (0, 0))
    blk = pl.BlockSpec((TB, CC), lambda n: (n, 0))
    return dict(
        body=body,
        in_specs=[pl.BlockSpec((TB, 2 * CC), lambda n: (n, C_GLU // (2 * CC))),
                  pl.BlockSpec((32, CC), lambda n: (0, 0)), vec, vec, vec],
        args=[z0, conv_w, conv_b, ln_g, ln_b],
        out_specs=[blk, blk],
        out_shape=[jax.ShapeDtypeStruct((R, CC), F32), jax.ShapeDtypeStruct((R, CC), BF16)],
        scratch=[pltpu.VMEM((8, 2 * TB + 8, CC), F32)])


def _conv_bwd(ds, cv, z0, conv_w, ln_g, ln_b):
    def body(ds_ref, cv_ref, g_ref, w_ref, lg_ref, lb_ref, dglu_ref, dw_ref, dsm_ref, dbuf):
        n = pl.program_id(0)

        @pl.when(n == 0)
        def _():
            dbuf[...] = jnp.zeros_like(dbuf)
            dw_ref[...] = jnp.zeros_like(dw_ref)
            dsm_ref[...] = jnp.zeros_like(dsm_ref)

        cv = cv_ref[...]
        xc = cv - jnp.mean(cv, axis=1, keepdims=True)
        rs = lax.rsqrt(jnp.mean(xc * xc, axis=1, keepdims=True) + LN_EPS)
        xh = xc * rs
        ln = xh * lg_ref[...] + lb_ref[...]
        sg = _sig(ln)
        dln = ds_ref[...] * (sg * (1.0 + ln * (1.0 - sg)))
        dxh = dln * lg_ref[...]
        dcv = rs * (dxh - jnp.mean(dxh, axis=1, keepdims=True) - xh * jnp.mean(dxh * xh, axis=1, keepdims=True))
        dsm_ref[0:1, :] += jnp.sum(dcv, axis=0, keepdims=True)
        dsm_ref[1:2, :] += jnp.sum(dln * xh, axis=0, keepdims=True)
        dsm_ref[2:3, :] += jnp.sum(dln, axis=0, keepdims=True)
        for k in range(8):
            dbuf[k, TB:2 * TB + 8, :] = dbuf[k, 0:TB + 8, :]
            dbuf[k, pl.ds(8 - k, TB), :] = dcv
        a = g_ref[:, 0:CC]
        sb = _sig(g_ref[:, CC:2 * CC])
        u = a * sb
        du = jnp.zeros((TB, CC), F32)
        for w in range(CONV_W):
            off = HALO - w
            sh = dbuf[off % 8, pl.ds(off + 8 - off % 8, TB), :]
            du = du + sh * w_ref[w:w + 1, :]
            dw_ref[w:w + 1, :] += jnp.sum(u * sh, axis=0, keepdims=True)
        dglu_ref[:, 0:CC] = (du * sb).astype(BF16)
        dglu_ref[:, CC:2 * CC] = (du * a * sb * (1.0 - sb)).astype(BF16)

    rev = lambda n: (NB - 1 - n, 0)
    vec = pl.BlockSpec((1, CC), lambda n: (0, 0))
    blk = pl.BlockSpec((TB, CC), rev)
    return dict(
        body=body,
        in_specs=[blk, blk, pl.BlockSpec((TB, 2 * CC), lambda n: (NB - 1 - n, C_GLU // (2 * CC))),
                  pl.BlockSpec((32, CC), lambda n: (0, 0)), vec, vec],
        args=[ds, cv, z0, conv_w, ln_g, ln_b],
        out_specs=[pl.BlockSpec((TB, 2 * CC), rev), pl.BlockSpec((32, CC), lambda n: (0, 0)),
                   pl.BlockSpec((8, CC), lambda n: (0, 0))],
        out_shape=[jax.ShapeDtypeStruct((R, 2 * CC), BF16), jax.ShapeDtypeStruct((32, CC), F32),
                   jax.ShapeDtypeStruct((8, CC), F32)],
        scratch=[pltpu.VMEM((8, 2 * TB + 8, CC), F32)])


def _split_dot(x, t):
    hi = x.astype(BF16)
    lo = (x - hi.astype(F32)).astype(BF16)
    return _dot(hi, t) + _dot(lo, t)


def _stack_heads(x):
    lane = lax.broadcasted_iota(jnp.int32, (TB, 128), 1)
    return jnp.concatenate([jnp.where(lane < HD, x, 0.0), jnp.where(lane < HD, 0.0, x)], axis=0).astype(BF16)


def _sb_stack(qv, i):
    lane2, row2 = _lane_row((2 * TB, 128))
    qpos2 = i * TB + (row2 & (TB - 1))
    lane, row = _lane_row((TB, 128))
    return _stack_heads(qv), lane2, qpos2, (row > lane).astype(BF16)


SB_U = 3
SB_DEAD = -104.0
SB_P = 4


def _sb_fwd(q, k, v, g):
    def body(q_ref, k_ref, v_ref, g_ref, o_ref, m_ref, c_ref, n_ref):
        p, i = pl.program_id(0), pl.program_id(1)
        lane, row = _lane_row((TB, 128))
        lo = lane < HD
        slabs = [slice(s * 128, (s + 1) * 128) for s in range(SB_P)]
        q2s = []
        for sl in slabs:
            q2, lane2, qpos2, tri_gt = _sb_stack(q_ref[:, sl].astype(F32) * SCALE, i)
            q2s.append(q2)

        def cond(st):
            t, _, c2s = st
            alive = jnp.max(c2s[0])
            for c2 in c2s[1:]:
                alive = jnp.maximum(alive, jnp.max(c2))
            return jnp.logical_and(i - SB_U * t >= 0, alive > SB_DEAD)

        def step(st):
            t, accs, c2s = st
            accs, c2s = list(accs), list(c2s)
            jrs = [i - SB_U * t - u for u in range(SB_U)]
            j0s = [pl.multiple_of(jnp.maximum(jr, 0) * TB, TB) for jr in jrs]
            valids = []
            for jr in jrs:
                kpos = jr * TB + lane2
                valids.append((kpos >= PAD) & (kpos < qpos2))
            zs = [[jnp.where(valid, _dot_nt(q2s[s], k_ref[pl.ds(j0, TB), slabs[s]]), NEG)
                   for j0, valid in zip(j0s, valids)] for s in range(SB_P)]
            lbs, l1s = [], []
            for s in range(SB_P):
                lbs.append([jnp.minimum(z, 0.0) - jnp.log(1.0 + jnp.exp(-jnp.abs(z))) for z in zs[s]])
                l1s.append([lb - z for lb, z in zip(lbs[s], zs[s])])
            sfxs = [[_split_dot(l1, tri_gt) for l1 in l1s[s]] for s in range(SB_P)]
            carries = []
            for s in range(SB_P):
                cs, c2 = [], c2s[s]
                for jr, l1 in zip(jrs, l1s[s]):
                    cs.append(c2)
                    c_ref[:, slabs[s]] = jnp.where(lane == 2 * jr, c2[0:TB],
                                                   jnp.where(lane == 2 * jr + 1, c2[TB:2 * TB], c_ref[:, slabs[s]]))
                    c2 = c2 + jnp.sum(l1, axis=1, keepdims=True)
                carries.append(cs)
                c2s[s] = c2
            for s in range(SB_P):
                for j0, valid, lb, sfx, cu in zip(j0s, valids, lbs[s], sfxs[s], carries[s]):
                    a = jnp.exp(lb + sfx + cu).astype(BF16)
                    av = _dot(a, v_ref[pl.ds(j0, TB), slabs[s]])
                    accs[s] = accs[s] + jnp.where(lo, av[0:TB], av[TB:2 * TB])
            return t + 1, tuple(accs), tuple(c2s)

        c_ref[...] = jnp.zeros_like(c_ref)
        init = (jnp.int32(0), tuple(jnp.zeros((TB, 128), F32) for _ in slabs),
                tuple(jnp.zeros((2 * TB, 1), F32) for _ in slabs))
        t, accs, _ = lax.while_loop(cond, step, init)
        for sl, acc in zip(slabs, accs):
            o_ref[:, sl] = acc
            gv = g_ref[:, sl]
            m_ref[:, sl] = (acc * (gv * _sig(gv))).astype(BF16)
        n_ref[p, i] = t

    wide = SB_P * 128
    slab = pl.BlockSpec((R, wide), lambda p, i: (0, p))
    blk = pl.BlockSpec((TB, wide), lambda p, i: (i, p))
    sd = jax.ShapeDtypeStruct((R, D), F32)
    return pl.pallas_call(
        body, grid=(D // wide, NB), in_specs=[blk, slab, slab, blk],
        out_specs=[blk, blk, blk, pl.BlockSpec(memory_space=pltpu.SMEM)],
        out_shape=[sd, jax.ShapeDtypeStruct((R, D), BF16), sd, jax.ShapeDtypeStruct((D // wide, NB), jnp.int32)],
        name="sb_fwd", compiler_params=_cp(("arbitrary", "arbitrary")))(q, k, v, g)


def _sb_bwd(trips, q, k, v, car, dm, g, o):
    def body(n_ref, q_ref, k_ref, v_ref, c_ref, dm_ref, g_ref, o_ref, dq_ref, dko_ref, dvo_ref, dg_ref,
             dk_ref, dv_ref):
        p, i = pl.program_id(0), pl.program_id(1)

        @pl.when(i == 0)
        def _():
            dk_ref[...] = jnp.zeros_like(dk_ref)
            dv_ref[...] = jnp.zeros_like(dv_ref)

        lane, row = _lane_row((TB, 128))
        lo = lane < HD
        tri_lt = (row < lane).astype(BF16)
        slabs = [slice(s * 128, (s + 1) * 128) for s in range(SB_P)]
        q2s, do2s, cts = [], [], []
        for sl in slabs:
            q2, lane2, qpos2, tri_gt = _sb_stack(q_ref[:, sl].astype(F32) * SCALE, i)
            q2s.append(q2)
            gv, dmv = g_ref[:, sl], dm_ref[:, sl]
            sg = _sig(gv)
            dg_ref[:, sl] = (dmv * o_ref[:, sl] * (sg * (1.0 + gv * (1.0 - sg)))).astype(BF16)
            do2s.append(_stack_heads(dmv * (gv * sg)))
            cts.append(c_ref[:, sl])
        trips_i = n_ref[p, i]
        first = jnp.maximum(i + 1 - SB_U * trips_i, 0)

        def step(t, carry):
            dqs, g2s = carry
            dqs, g2s = list(dqs), list(g2s)
            jrs = [first + SB_U * t + u for u in range(SB_U)]
            j0s = [pl.multiple_of(jnp.minimum(jr, i) * TB, TB) for jr in jrs]
            valids = []
            for jr in jrs:
                kpos = jr * TB + lane2
                valids.append((kpos >= PAD) & (kpos < qpos2))
            ks = [[k_ref[pl.ds(j0, TB), sl] for j0 in j0s] for sl in slabs]
            zs = [[jnp.where(valid, _dot_nt(q2s[s], kj), NEG) for kj, valid in zip(ks[s], valids)] for s in range(SB_P)]
            das = [[_dot_nt(do2s[s], v_ref[pl.ds(j0, TB), slabs[s]]) for j0 in j0s] for s in range(SB_P)]
            es = [[jnp.exp(-jnp.abs(z)) for z in zs[s]] for s in range(SB_P)]
            lbs = [[jnp.minimum(z, 0.0) - jnp.log(1.0 + e) for z, e in zip(zs[s], es[s])] for s in range(SB_P)]
            l1s = [[lb - z for lb, z in zip(lbs[s], zs[s])] for s in range(SB_P)]
            sfxs = [[_split_dot(l1, tri_gt) for l1 in l1s[s]] for s in range(SB_P)]
            a_s, gmats, gpres = [], [], []
            for s in range(SB_P):
                a_l, gm_l, gp_l, g2 = [], [], [], g2s[s]
                for jr, valid, lb, sfx, da in zip(jrs, valids, lbs[s], sfxs[s], das[s]):
                    later = jnp.concatenate(
                        [jnp.sum(jnp.where(lane == 2 * jr + hh, cts[s], 0.0), axis=1, keepdims=True) for hh in range(2)],
                        axis=0)
                    a = jnp.exp(lb + sfx + later)
                    gmat = da * a
                    a_l.append(a.astype(BF16))
                    gm_l.append(gmat)
                    gp_l.append(g2)
                    g2 = g2 + jnp.sum(gmat, axis=1, keepdims=True)
                a_s.append(a_l)
                gmats.append(gm_l)
                gpres.append(gp_l)
                g2s[s] = g2
            pres = [[gp + _split_dot(gmat, tri_lt) for gp, gmat in zip(gpres[s], gmats[s])] for s in range(SB_P)]
            for s in range(SB_P):
                for j0, kj, valid, z, e, gmat, pre, a in zip(j0s, ks[s], valids, zs[s], es[s], gmats[s], pres[s], a_s[s]):
                    r = 1.0 / (1.0 + e)
                    big = z >= 0.0
                    beta = jnp.where(big, r, e * r)
                    omb = jnp.where(big, e * r, r)
                    dz = (gmat * omb - beta * pre).astype(BF16)
                    dq2 = _dot(dz, kj)
                    dqs[s] = dqs[s] + jnp.where(lo, dq2[0:TB], dq2[TB:2 * TB])
                    dk_ref[pl.ds(j0, TB), slabs[s]] += _dot_tn(dz, q2s[s])
                    dv_ref[pl.ds(j0, TB), slabs[s]] += _dot_tn(a, do2s[s])
            return tuple(dqs), tuple(g2s)

        init = (tuple(jnp.zeros((TB, 128), F32) for _ in slabs), tuple(jnp.zeros((2 * TB, 1), F32) for _ in slabs))
        dqs, _ = lax.fori_loop(0, trips_i, step, init)
        for sl, dq in zip(slabs, dqs):
            dq_ref[:, sl] = (dq * SCALE).astype(BF16)

        @pl.when(i == NB - 1)
        def _():
            dko_ref[...] = dk_ref[...].astype(BF16)
            dvo_ref[...] = dv_ref[...].astype(BF16)

    wide = SB_P * 128
    slab = pl.BlockSpec((R, wide), lambda p, i: (0, p))
    blk = pl.BlockSpec((TB, wide), lambda p, i: (i, p))
    sd = jax.ShapeDtypeStruct((R, D), BF16)
    return pl.pallas_call(
        body, grid=(D // wide, NB),
        in_specs=[pl.BlockSpec(memory_space=pltpu.SMEM), blk, slab, slab, blk, blk, blk, blk],
        out_specs=[blk, slab, slab, blk], out_shape=[sd, sd, sd, sd],
        scratch_shapes=[pltpu.VMEM((R, wide), F32), pltpu.VMEM((R, wide), F32)], name="sb_bwd",
        compiler_params=_cp(("arbitrary", "arbitrary")))(trips, q, k, v, car, dm, g, o)


def _adamw(w, parts, m, v, name):
    rows, cols = w.shape
    tr = 256 if rows % 256 == 0 else rows
    nparts = len(parts)

    def body(*refs):
        w_ref = refs[0]
        p_refs = refs[1:1 + nparts]
        m_ref, v_ref, g_ref, d_ref, nm_ref, nv_ref = refs[1 + nparts:]
        g = p_refs[0][...]
        for p_ref in p_refs[1:]:
            g = g + p_ref[...]
        nm = ADAM_B1 * m_ref[...] + (1.0 - ADAM_B1) * g
        nv = ADAM_B2 * v_ref[...] + (1.0 - ADAM_B2) * (g * g)
        m_hat = nm / (1.0 - ADAM_B1 ** ADAM_STEP)
        v_hat = nv / (1.0 - ADAM_B2 ** ADAM_STEP)
        g_ref[...] = g
        d_ref[...] = -ADAM_LR * (m_hat / (jnp.sqrt(v_hat) + ADAM_EPS) + ADAM_WD * w_ref[...])
        nm_ref[...] = nm
        nv_ref[...] = nv

    blk = pl.BlockSpec((tr, cols), lambda i: (i, 0))
    sd = jax.ShapeDtypeStruct((rows, cols), F32)
    return pl.pallas_call(
        body, grid=(rows // tr,), in_specs=[blk] * (3 + nparts), out_specs=[blk] * 4, out_shape=[sd] * 4,
        name=name, compiler_params=_cp(("parallel",)))(w, *parts, m, v)


def _sum8(buf, name):
    _, rows, cols = buf.shape

    def body(b_ref, o_ref):
        acc = b_ref[0]
        for i in range(1, 8):
            acc = acc + b_ref[i]
        o_ref[...] = acc

    return pl.pallas_call(
        body, out_shape=jax.ShapeDtypeStruct((rows, cols), F32), name=name,
        compiler_params=pltpu.CompilerParams(vmem_limit_bytes=VMEM_LIMIT))(buf)


MESH = pl.DeviceIdType.MESH
ANY = pl.BlockSpec(memory_space=pl.ANY)


def _chip_peers():
    x, y = lax.axis_index("x"), lax.axis_index("y")
    return [(1 - x, y), (x, 1 - y), (1 - x, 1 - y)]


def _gather_chips(shards):
    plan = _gather_plan(shards)

    def body(*refs):
        n = len(shards)
        ins, outs, sems = refs[:n], refs[n:2 * n], refs[2 * n:]
        plan["start"](ins, outs, sems)
        plan["mid"](ins, outs, sems)
        plan["finish"](ins, outs, sems)

    n = len(shards)
    res = pl.pallas_call(
        body, in_specs=[ANY] * n, out_specs=[ANY] * n, out_shape=plan["out_shape"],
        scratch_shapes=plan["sems"], name="gather_chips")(*plan["args"])
    return plan["post"](res)


def _gather_plan(shards):
    n = len(shards)
    shards = [s.reshape((2, s.shape[0] // 2) + s.shape[1:]) for s in shards]

    def copies(kind, ins, outs, sems):
        s1, r1, s2, r2 = sems
        x, y, c = lax.axis_index("x"), lax.axis_index("y"), lax.axis_index("c")
        me = 2 * x + y
        out = []
        for j, (px, py) in enumerate(_chip_peers()):
            for a in range(n):
                k = j * n + a
                got = outs[a].at[2 * px + py].at[c]
                other = outs[a].at[2 * px + py].at[1 - c]
                src, dst, ss, rs, dev = {
                    "first": (ins[a].at[c], outs[a].at[me].at[c], s1, r1, (px, py, c)),
                    "landed": (got, got, s1, r1, (px, py, c)),
                    "passed": (got, got, s2, r2, (x, y, 1 - c)),
                    "theirs": (other, other, s2, r2, (x, y, 1 - c)),
                }[kind]
                out.append(pltpu.make_async_remote_copy(
                    src_ref=src, dst_ref=dst, send_sem=ss.at[k], recv_sem=rs.at[k], device_id=dev, device_id_type=MESH))
        return out

    def start(ins, outs, sems):
        for cp in copies("first", ins, outs, sems):
            cp.start()

    def mid(ins, outs, sems):
        for got, fwd in zip(copies("landed", ins, outs, sems), copies("passed", ins, outs, sems)):
            got.wait_recv()
            fwd.start()

    def finish(ins, outs, sems):
        for cp in copies("theirs", ins, outs, sems):
            cp.wait_recv()
        for cp in copies("first", ins, outs, sems) + copies("passed", ins, outs, sems):
            cp.wait_send()

    def post(res):
        me = 2 * lax.axis_index("x") + lax.axis_index("y")
        res = [lax.dynamic_update_index_in_dim(r, s, me, 0) for r, s in zip(res, shards)]
        return [r.reshape((N_CHIPS, 2 * r.shape[2]) + r.shape[3:]) for r in res]

    return dict(args=shards, out_shape=[jax.ShapeDtypeStruct((N_CHIPS,) + s.shape, s.dtype) for s in shards],
                sems=[pltpu.SemaphoreType.DMA((3 * n,))] * 4, start=start, mid=mid, finish=finish, post=post)


def _rows_call(name, parts, plan):
    n_in = [len(p["args"]) for p in parts]
    n_out = [len(p["out_shape"]) for p in parts]
    n_scr = [len(p["scratch"]) for p in parts]
    c_in, c_out = len(plan["args"]), len(plan["out_shape"])

    def split(refs, sizes):
        out, pos = [], 0
        for k in sizes:
            out.append(refs[pos:pos + k])
            pos += k
        return out

    def body(*refs):
        ins, outs, scr = split(refs, [sum(n_in) + c_in, sum(n_out) + c_out, sum(n_scr) + len(plan["sems"])])
        p_in, p_out, p_scr = split(ins, n_in + [c_in]), split(outs, n_out + [c_out]), split(scr, n_scr + [len(plan["sems"])])
        comm = (p_in[-1], p_out[-1], p_scr[-1])
        step = pl.program_id(0)

        @pl.when(step == 0)
        def _():
            plan["start"](*comm)

        for p, i, o, s in zip(parts, p_in, p_out, p_scr):
            p["body"](*i, *o, *s)

        @pl.when(step == NB - 2)
        def _():
            plan["mid"](*comm)

        @pl.when(step == NB - 1)
        def _():
            plan["finish"](*comm)

    flat = lambda key: [v for p in parts for v in p[key]]
    res = pl.pallas_call(
        body, grid=(NB,), in_specs=flat("in_specs") + [ANY] * c_in, out_specs=flat("out_specs") + [ANY] * c_out,
        out_shape=flat("out_shape") + plan["out_shape"], scratch_shapes=flat("scratch") + plan["sems"],
        name=name, compiler_params=_cp(("arbitrary",)))(*flat("args"), *plan["args"])
    outs = split(res, n_out + [c_out])
    return outs[:-1], outs[-1]


def _pair_exchange(grads, name):
    n = len(grads)
    hs = [g.shape[1] // 2 for g in grads]
    grads = [g.reshape((N_CHIPS, 2, h) + g.shape[2:]) for g, h in zip(grads, hs)]

    def body(*refs):
        ins, got = refs[:n], refs[n:2 * n]
        ssem, rsem = refs[2 * n:]
        x, y, c = lax.axis_index("x"), lax.axis_index("y"), lax.axis_index("c")
        sends = [pltpu.make_async_remote_copy(
            src_ref=ins[a].at[:, 1 - c], dst_ref=got[a], send_sem=ssem.at[a],
            recv_sem=rsem.at[a], device_id=(x, y, 1 - c), device_id_type=MESH) for a in range(n)]
        for cp in sends:
            cp.start()
        for cp in sends:
            cp.wait()

    half_shapes = [jax.ShapeDtypeStruct((N_CHIPS, h) + g.shape[3:], g.dtype) for g, h in zip(grads, hs)]
    got = pl.pallas_call(
        body, in_specs=[ANY] * n, out_specs=[ANY] * n, out_shape=half_shapes,
        scratch_shapes=[pltpu.SemaphoreType.DMA((n,))] * 2, name=name)(*grads)
    c = lax.axis_index("c")
    own = [lax.dynamic_index_in_dim(g, c, 1, keepdims=False) for g in grads]
    return own, got


def _sum_pair(own, got, send_dtype, name):
    _, rows, cols = own.shape
    tr = 256 if rows % 256 == 0 else rows

    def body(a_ref, b_ref, f_ref, s_ref):
        t = a_ref[...].astype(F32) + b_ref[...].astype(F32)
        f_ref[...] = t
        s_ref[...] = t.astype(send_dtype)

    blk = pl.BlockSpec((N_CHIPS, tr, cols), lambda i: (0, i, 0))
    return pl.pallas_call(
        body, grid=(rows // tr,), in_specs=[blk, blk], out_specs=[blk, blk],
        out_shape=[jax.ShapeDtypeStruct(own.shape, F32), jax.ShapeDtypeStruct(own.shape, send_dtype)],
        name=name, compiler_params=_cp(("parallel",)))(own, got)


def _scatter_chips(keep, send):
    n = len(send)
    plan = _scatter_plan(send)

    def body(*refs):
        sin, land, sems = refs[:n], refs[n:2 * n], refs[2 * n:]
        plan["start"](sin, land, sems)
        plan["finish"](sin, land, sems)

    land = pl.pallas_call(
        body, in_specs=[ANY] * n, out_specs=[ANY] * n, out_shape=plan["out_shape"],
        scratch_shapes=plan["sems"], name="scatter_chips")(*send)
    return _own_slab(keep), land


def _own_slab(keep):
    me = 2 * lax.axis_index("x") + lax.axis_index("y")
    return [lax.dynamic_index_in_dim(k, me, 0, keepdims=False) for k in keep]


def _scatter_plan(send):
    n = len(send)

    def copies(sin, land, sems):
        ssem, rsem = sems
        c = lax.axis_index("c")
        return [pltpu.make_async_remote_copy(
            src_ref=sin[a].at[2 * px + py], dst_ref=land[a].at[j], send_sem=ssem.at[j * n + a],
            recv_sem=rsem.at[j * n + a], device_id=(px, py, c), device_id_type=MESH)
            for j, (px, py) in enumerate(_chip_peers()) for a in range(n)]

    def start(sin, land, sems):
        for cp in copies(sin, land, sems):
            cp.start()

    def finish(sin, land, sems):
        for cp in copies(sin, land, sems):
            cp.wait()

    return dict(args=list(send), out_shape=[jax.ShapeDtypeStruct((3,) + s.shape[1:], s.dtype) for s in send],
                sems=[pltpu.SemaphoreType.DMA((3 * n,))] * 2, start=start, mid=lambda *a: None, finish=finish)


def _sum_shard(mine, land, name):
    rows, cols = mine.shape
    tr = 256 if rows % 256 == 0 else rows

    def body(m_ref, l_ref, o_ref):
        o_ref[...] = ((m_ref[...] + l_ref[0].astype(F32)) + l_ref[1].astype(F32)) + l_ref[2].astype(F32)

    return pl.pallas_call(
        body, grid=(rows // tr,),
        in_specs=[pl.BlockSpec((tr, cols), lambda i: (i, 0)), pl.BlockSpec((3, tr, cols), lambda i: (0, i, 0))],
        out_specs=pl.BlockSpec((tr, cols), lambda i: (i, 0)), out_shape=jax.ShapeDtypeStruct((rows, cols), F32),
        name=name, compiler_params=_cp(("parallel",)))(mine, land)


def _join_cores(halves):
    n = len(halves)

    def body(*refs):
        ins, outs = refs[:n], refs[n:2 * n]
        ssem, rsem = refs[2 * n:]
        x, y, c = lax.axis_index("x"), lax.axis_index("y"), lax.axis_index("c")
        sends = [pltpu.make_async_remote_copy(
            src_ref=ins[a], dst_ref=outs[a].at[c], send_sem=ssem.at[a], recv_sem=rsem.at[a],
            device_id=(x, y, 1 - c), device_id_type=MESH) for a in range(n)]
        for cp in sends:
            cp.start()
        for a in range(n):
            sends[a].wait_send()
            pltpu.make_async_remote_copy(
                src_ref=ins[a], dst_ref=outs[a].at[1 - c], send_sem=ssem.at[a], recv_sem=rsem.at[a],
                device_id=(x, y, 1 - c), device_id_type=MESH).wait_recv()

    res = pl.pallas_call(
        body, in_specs=[ANY] * n, out_specs=[ANY] * n,
        out_shape=[jax.ShapeDtypeStruct((2,) + h.shape, h.dtype) for h in halves],
        scratch_shapes=[pltpu.SemaphoreType.DMA((n,))] * 2, name="join_cores")(*halves)
    c = lax.axis_index("c")
    res = [lax.dynamic_update_index_in_dim(r, h, c, 0) for r, h in zip(res, halves)]
    return [r.reshape((2 * r.shape[1],) + r.shape[2:]) for r in res]


def _gather_all(vec):
    def body(v_ref, o_ref, lsem, ssem, rsem):
        x, y, c = lax.axis_index("x"), lax.axis_index("y"), lax.axis_index("c")
        me = 4 * x + 2 * y + c
        local = pltpu.make_async_copy(v_ref, o_ref.at[me], lsem)
        local.start()
        cps = []
        for k in range(1, 8):
            px, py, pc = x ^ (k >> 2), y ^ ((k >> 1) & 1), c ^ (k & 1)
            cps.append(pltpu.make_async_remote_copy(
                src_ref=v_ref, dst_ref=o_ref.at[me], send_sem=ssem.at[k - 1], recv_sem=rsem.at[k - 1],
                device_id=(px, py, pc), device_id_type=MESH))
        for cp in cps:
            cp.start()
        for k in range(1, 8):
            px, py, pc = x ^ (k >> 2), y ^ ((k >> 1) & 1), c ^ (k & 1)
            pltpu.make_async_remote_copy(
                src_ref=v_ref, dst_ref=o_ref.at[4 * px + 2 * py + pc], send_sem=ssem.at[k - 1],
                recv_sem=rsem.at[k - 1], device_id=(px, py, pc), device_id_type=MESH).wait_recv()
        for cp in cps:
            cp.wait_send()
        local.wait()

    return pl.pallas_call(
        body, in_specs=[ANY], out_specs=ANY, out_shape=jax.ShapeDtypeStruct((8,) + vec.shape, vec.dtype),
        scratch_shapes=[pltpu.SemaphoreType.DMA, pltpu.SemaphoreType.DMA((7,)), pltpu.SemaphoreType.DMA((7,))],
        name="gather_all")(vec)


def _rope_tables():
    pos = (jnp.arange(R, dtype=jnp.int32) - PAD).astype(F32)
    half = HD // 2
    inv = ROPE_THETA ** (-jnp.arange(half, dtype=F32) / half)
    ang = pos[:, None] * inv[None, :]
    cos, sin = jnp.cos(ang), jnp.sin(ang)
    cs = jnp.tile(cos, (1, 4))
    sn = jnp.tile(jnp.concatenate([-sin, sin], axis=1), (1, 2))
    return cs, sn


def _perm_cols(w):
    return jnp.concatenate([w[:, 0:512], w[:, 768:1280], w[:, 1280:2304], w[:, 2304:2816], w[:, 512:640],
                            w[:, 640:768]], axis=1)


def _unperm_cols(w):
    return jnp.concatenate([w[:, C_Q:C_Q + 512], w[:, C_K:C_K + 128], w[:, C_V:C_V + 128], w[:, C_GA:C_GA + 512],
                            w[:, C_GLU:C_GLU + 1024], w[:, C_GB:C_GB + 512]], axis=1)


def _local_step(x, target, p):
    w0 = _perm_cols(p["ab_w_in"])
    conv_w = jnp.concatenate([p["ab_conv_w"], jnp.zeros((1, CC), F32)], axis=0)
    cs, sn = _rope_tables()

    h0 = jnp.concatenate([jnp.zeros((PAD, D), F32), p["meta_tokens"], x], axis=0)

    xn0 = _rms_fwd(h0, p["ab_pre_norm"], "rms_fwd0")
    plan = _gather_plan([p["sb_w_out"], p["ab_w_out"], p["ab_w_pw2"]])
    z0, gathered = _mm([(xn0, w0)], F32, "in_proj0", 544, 1408, plan=plan)
    wo1, wo0, wpw = plan["post"](gathered)
    wo1, wo0, wpw = wo1.reshape(D, D), wo0.reshape(D, D), wpw.reshape(CC, CC)
    plan = _gather_plan([p["sb_w_in"]])
    ((o0, a0, lse0), (cv0, s0)), gathered = _rows_call(
        "fwd0", [_swa_fwd(z0, cs, sn, p["ab_sinks"]),
                 _conv_fwd(z0, conv_w, p["ab_conv_b"], p["ab_conv_ln_g"], p["ab_conv_ln_b"])], plan)
    (w1,) = plan["post"](gathered)
    t0 = _mm([(s0, wpw)], F32, "pw2", 544, 512)
    c0 = _gate_fwd(t0, z0, C_GB, "gate_b_fwd")
    wo0h = wo0.reshape(2, CC, D)
    y0 = _mm([(a0, (wo0h, 0)), (c0, (wo0h, 1))], F32, "out_proj0", 544, 1024)

    h1, xn1 = _post_rms_fwd(h0, y0, p["ab_post_norm"], p["sb_pre_norm"], "post_rms_fwd")
    q1 =_mm([(xn1, (w1, 0))], BF16, "in_proj1_q", 544, 1024)
    k1 = _mm([(xn1, (w1, 1))], BF16, "in_proj1_k", 544, 1024)
    v1 = _mm([(xn1, (w1, 2))], BF16, "in_proj1_v", 544, 1024)
    g1 = _mm([(xn1, (w1, 3))], F32, "in_proj1_g", 544, 1024)
    o1, m1, car1, trips1 = _sb_fwd(q1, k1, v1, g1)
    y1 = _mm([(m1, wo1)], F32, "out_proj1", 544, 1024)

    dh2, dy1, d_sb_post, loss_row = _tail(h1, y1, p["sb_post_norm"], target)

    dm1 = _mm([(dy1, wo1)], F32, "out_proj1_dx", 544, 1024, tb=True)
    d_wo1 = _mm([(m1, dy1)], BF16, "out_proj1_dw", 512, 1024, ta=True)
    dq1, dk1, dv1, dg1 = _sb_bwd(trips1, q1, k1, v1, car1, dm1, g1, o1)
    dz1 = [dq1, dk1, dv1, dg1]
    dxn1 = _mm([(dz1[j], (w1, j)) for j in range(4)], F32, "in_proj1_dx", 544, 1024, tb=True)
    d_w1 = jnp.stack([_mm([(xn1, dz1[j])], BF16, "in_proj1_dw%d" % j, 512, 1024, ta=True) for j in range(4)])

    dh1, d_sb_pre, dy0, d_ab_post = _rms_post_bwd(dxn1, h1, p["sb_pre_norm"], dh2, y0, p["ab_post_norm"],
                                                  "rms_post_bwd")
    dmix0 = _mm([(dy0, wo0)], F32, "out_proj0_dx", 544, 1024, tb=True)
    d_wo0 = jnp.concatenate([_mm([(a0, dy0)], BF16, "out_proj0_dw_a", 512, 1024, ta=True),
                             _mm([(c0, dy0)], BF16, "out_proj0_dw_b", 512, 1024, ta=True)], axis=0)
    dt0, dgb0 = _gate_bwd(dmix0, 512, t0, z0, C_GB, "gate_b_bwd")
    ds0 = _mm([(dt0, wpw)], F32, "pw2_dx", 544, 512, tb=True)
    d_wpw = _mm([(s0, dt0)], BF16, "pw2_dw", 512, 512, ta=True)
    early = ("sb_w_in", "sb_w_out", "ab_w_out", "ab_w_pw2")
    own1, got1 = _pair_exchange([d_w1, d_wo1.reshape(N_CHIPS, 256, D), d_wo0.reshape(N_CHIPS, 256, D),
                                 d_wpw.reshape(N_CHIPS, 128, CC)], "pair_exchange1")
    pair1 = [_sum_pair(o, t, BF16, "sum_pair_" + nm) for o, t, nm in zip(own1, got1, early)]
    plan = _scatter_plan([pr[1] for pr in pair1])
    ((dglu0, d_convw, d_small), (dq0, dga0, dkv0, d_sinks)), land1 = _rows_call(
        "bwd0", [_conv_bwd(ds0, cv0, z0, conv_w, p["ab_conv_ln_g"], p["ab_conv_ln_b"]),
                 _swa_bwd(z0, cs, sn, p["ab_sinks"], o0, dmix0, lse0)], plan)
    halves1 = [_sum_shard(mi, la, "sum_shard_" + nm)
               for mi, la, nm in zip(_own_slab([pr[0] for pr in pair1]), land1, early)]
    dz0 = jnp.concatenate([dq0, dga0, dglu0, dgb0, dkv0], axis=1)
    d_w0 = _unperm_cols(_mm([(xn0, dz0)], BF16, "in_proj0_dw", 512, 1408, ta=True))
    own0, got0 = _pair_exchange([_cols_to_chips(d_w0, 704)], "pair_exchange0")
    keep0, send0 = _sum_pair(own0[0], got0[0], BF16, "sum_pair_ab_w_in")
    plan = _scatter_plan([send0])
    dxn0, land0 = _mm([(dz0, w0)], F32, "in_proj0_dx", 544, 1024, tb=True, plan=plan)
    half0 = _sum_shard(_own_slab([keep0])[0], land0[0], "sum_shard_ab_w_in")
    dh0_first, grad_x, d_ab_pre = _rms_bwd(dxn0, h0, p["ab_pre_norm"], dh1, F32, "rms_bwd0", split=True)

    grads = {
        "meta_tokens": dh0_first[PAD:TB], "ab_pre_norm": d_ab_pre, "ab_sinks": d_sinks[0:1, 0:8],
        "ab_conv_w": d_convw[0:CONV_W], "ab_conv_b": d_small[0:1], "ab_conv_ln_g": d_small[1:2],
        "ab_conv_ln_b": d_small[2:3], "ab_post_norm": d_ab_post, "sb_pre_norm": d_sb_pre, "sb_post_norm": d_sb_post,
    }
    h_sb_in, h_sb_out, h_ab_out, h_pw2 = halves1
    return loss_row, grad_x, grads, [half0, h_ab_out, h_pw2, h_sb_in, h_sb_out]


SMALL_ROWS = 80
REP_ROWS = 32

WEIGHTS = ["meta_tokens", "ab_pre_norm", "ab_w_in", "ab_sinks", "ab_conv_w", "ab_conv_b", "ab_conv_ln_g",
           "ab_conv_ln_b", "ab_w_pw2", "ab_w_out", "ab_post_norm", "sb_pre_norm", "sb_w_in", "sb_w_out",
           "sb_post_norm"]
BIG = ["ab_w_in", "ab_w_out", "ab_w_pw2", "sb_w_in", "sb_w_out"]


def _pack_small(conv_w, meta, sb_pre, sb_post):
    rows = jnp.concatenate([conv_w, meta.reshape(32, 128), sb_pre.reshape(2, 128), sb_post.reshape(2, 128)], axis=0)
    return jnp.concatenate([rows, jnp.zeros((SMALL_ROWS - rows.shape[0], 128), F32)], axis=0)


def _unpack_small(s):
    return s[0:31], s[31:63].reshape(16, 256), s[63:65].reshape(1, 256), s[65:67].reshape(1, 256)


REP_LOSS = 3592


def _pack_rep(pre, post, conv_b, ln_g, ln_b, sinks, extra=None):
    flat = jnp.concatenate([pre.reshape(-1), post.reshape(-1), conv_b.reshape(-1), ln_g.reshape(-1),
                            ln_b.reshape(-1), sinks.reshape(-1)] + ([] if extra is None else [extra.reshape(-1)]))
    flat = jnp.concatenate([flat, jnp.zeros((REP_ROWS * 128 - flat.shape[0],), F32)])
    return flat.reshape(REP_ROWS, 128)


def _unpack_rep(r):
    f = r.reshape(-1)
    return (f[0:1024].reshape(1, 1024), f[1024:2048].reshape(1, 1024), f[2048:2560].reshape(1, 512),
            f[2560:3072].reshape(1, 512), f[3072:3584].reshape(1, 512), f[3584:3592].reshape(1, 8))


def _cols_to_chips(w, width):
    return w.reshape(w.shape[0], N_CHIPS, width).transpose(1, 0, 2)


def _chips_to_cols(w):
    return w.transpose(1, 0, 2).reshape(w.shape[1], -1)


def kernel(x, meta_tokens, ab_pre_norm, ab_w_in, ab_sinks, ab_conv_w, ab_conv_b, ab_conv_ln_g, ab_conv_ln_b, ab_w_pw2, ab_w_out, ab_post_norm, sb_pre_norm, sb_w_in, sb_w_out, sb_post_norm, loss_target, m_meta_tokens, m_ab_pre_norm, m_ab_w_in, m_ab_sinks, m_ab_conv_w, m_ab_conv_b, m_ab_conv_ln_g, m_ab_conv_ln_b, m_ab_w_pw2, m_ab_w_out, m_ab_post_norm, m_sb_pre_norm, m_sb_w_in, m_sb_w_out, m_sb_post_norm, v_meta_tokens, v_ab_pre_norm, v_ab_w_in, v_ab_sinks, v_ab_conv_w, v_ab_conv_b, v_ab_conv_ln_g, v_ab_conv_ln_b, v_ab_w_pw2, v_ab_w_out, v_ab_post_norm, v_sb_pre_norm, v_sb_w_in, v_sb_w_out, v_sb_post_norm):
    w = dict(meta_tokens=meta_tokens, ab_pre_norm=ab_pre_norm, ab_w_in=ab_w_in, ab_sinks=ab_sinks,
             ab_conv_w=ab_conv_w, ab_conv_b=ab_conv_b, ab_conv_ln_g=ab_conv_ln_g, ab_conv_ln_b=ab_conv_ln_b,
             ab_w_pw2=ab_w_pw2, ab_w_out=ab_w_out, ab_post_norm=ab_post_norm, sb_pre_norm=sb_pre_norm,
             sb_w_in=sb_w_in, sb_w_out=sb_w_out, sb_post_norm=sb_post_norm)
    m = dict(meta_tokens=m_meta_tokens, ab_pre_norm=m_ab_pre_norm, ab_w_in=m_ab_w_in, ab_sinks=m_ab_sinks,
             ab_conv_w=m_ab_conv_w, ab_conv_b=m_ab_conv_b, ab_conv_ln_g=m_ab_conv_ln_g,
             ab_conv_ln_b=m_ab_conv_ln_b, ab_w_pw2=m_ab_w_pw2, ab_w_out=m_ab_w_out, ab_post_norm=m_ab_post_norm,
             sb_pre_norm=m_sb_pre_norm, sb_w_in=m_sb_w_in, sb_w_out=m_sb_w_out, sb_post_norm=m_sb_post_norm)
    v = dict(meta_tokens=v_meta_tokens, ab_pre_norm=v_ab_pre_norm, ab_w_in=v_ab_w_in, ab_sinks=v_ab_sinks,
             ab_conv_w=v_ab_conv_w, ab_conv_b=v_ab_conv_b, ab_conv_ln_g=v_ab_conv_ln_g,
             ab_conv_ln_b=v_ab_conv_ln_b, ab_w_pw2=v_ab_w_pw2, ab_w_out=v_ab_w_out, ab_post_norm=v_ab_post_norm,
             sb_pre_norm=v_sb_pre_norm, sb_w_in=v_sb_w_in, sb_w_out=v_sb_w_out, sb_post_norm=v_sb_post_norm)

    def small_of(d):
        return _pack_small(d["ab_conv_w"][0], d["meta_tokens"], d["sb_pre_norm"], d["sb_post_norm"])

    def rep_of(d):
        return _pack_rep(d["ab_pre_norm"], d["ab_post_norm"], d["ab_conv_b"], d["ab_conv_ln_g"], d["ab_conv_ln_b"],
                         d["ab_sinks"])

    g_in0, g_small = _gather_chips([ab_w_in[0].astype(BF16), small_of(w)])
    conv_w_f = _chips_to_cols(g_small[:, 0:31])
    meta_f = _chips_to_cols(g_small[:, 31:63].reshape(N_CHIPS, 16, 256))
    sb_pre_f = g_small[:, 63:65].reshape(1, D)
    sb_post_f = g_small[:, 65:67].reshape(1, D)
    full = {
        "meta_tokens": meta_f, "ab_pre_norm": ab_pre_norm, "ab_w_in": _chips_to_cols(g_in0),
        "ab_sinks": ab_sinks, "ab_conv_w": conv_w_f, "ab_conv_b": ab_conv_b, "ab_conv_ln_g": ab_conv_ln_g,
        "ab_conv_ln_b": ab_conv_ln_b, "ab_w_pw2": ab_w_pw2[0].astype(BF16), "ab_w_out": ab_w_out[0].astype(BF16),
        "ab_post_norm": ab_post_norm, "sb_pre_norm": sb_pre_f, "sb_w_in": sb_w_in[0].astype(BF16),
        "sb_w_out": sb_w_out[0].astype(BF16), "sb_post_norm": sb_post_f,
    }

    loss_row, grad_x, g, halves = _local_step(x[0], loss_target[0], full)

    total = _join_cores(halves)

    rep_g = _pack_rep(g["ab_pre_norm"], g["ab_post_norm"], g["ab_conv_b"], g["ab_conv_ln_g"], g["ab_conv_ln_b"],
                      g["ab_sinks"], loss_row[0:1, 0:1])
    vec = jnp.concatenate([rep_g, g["ab_conv_w"].reshape(124, 128), g["meta_tokens"].reshape(128, 128),
                           g["sb_pre_norm"].reshape(8, 128), g["sb_post_norm"].reshape(8, 128),
                           jnp.zeros((4, 128), F32)], axis=0)
    vec_sum = _sum8(_gather_all(vec), "sum8_small")
    rep_sum = vec_sum[0:REP_ROWS]
    loss = rep_sum.reshape(-1)[REP_LOSS]
    me = 2 * lax.axis_index("x") + lax.axis_index("y")
    small_sum = _pack_small(
        lax.dynamic_slice_in_dim(vec_sum[32:156].reshape(CONV_W, CC), me * 128, 128, axis=1),
        lax.dynamic_slice_in_dim(vec_sum[156:284].reshape(N_META, D), me * 256, 256, axis=1),
        lax.dynamic_slice_in_dim(vec_sum[284:292].reshape(1, D), me * 256, 256, axis=1),
        lax.dynamic_slice_in_dim(vec_sum[292:300].reshape(1, D), me * 256, 256, axis=1))

    out_g, out_d, out_m, out_v = {}, {}, {}, {}
    for i, k in enumerate(BIG):
        shp = w[k].shape
        res = _adamw(w[k][0], [total[i]], m[k][0], v[k][0], "adamw_" + k)
        out_g[k], out_d[k], out_m[k], out_v[k] = [r.reshape(shp) for r in res]
    res = _adamw(small_of(w), [small_sum], small_of(m), small_of(v), "adamw_small")
    for dst, r in zip((out_g, out_d, out_m, out_v), res):
        cw, mt, pre, post = _unpack_small(r)
        dst["ab_conv_w"], dst["meta_tokens"], dst["sb_pre_norm"], dst["sb_post_norm"] = cw[None], mt, pre, post
    res = _adamw(rep_of(w), [rep_sum], rep_of(m), rep_of(v), "adamw_rep")
    for dst, r in zip((out_g, out_d, out_m, out_v), res):
        (dst["ab_pre_norm"], dst["ab_post_norm"], dst["ab_conv_b"], dst["ab_conv_ln_g"], dst["ab_conv_ln_b"],
         dst["ab_sinks"]) = _unpack_rep(r)

    return (loss, grad_x[None], *[out_g[k] for k in WEIGHTS], *[out_d[k] for k in WEIGHTS],
            *[out_m[k] for k in WEIGHTS], *[out_v[k] for k in WEIGHTS])
```

```python
import functools

import jax
import jax.numpy as jnp
from jax import lax
from jax.experimental import pallas as pl
from jax.experimental.pallas import tpu as pltpu

F32 = jnp.float32
BF16 = jnp.bfloat16

D = 1024
SEQ = 2048
N_META = 16
TB = 128
TR = 272
PAD = TB - N_META
R = SEQ + TB
NB = R // TB
HD = 64
ROPE_THETA = 10000.0
NORM_EPS = 1e-6
LN_EPS = 1e-5
NEG = -1e30
SWA_HEADS = 8
CONV_W = 31
SCALE = HD ** -0.5
N_CHIPS = 4

C_Q, C_GA, C_GLU, C_GB, C_K, C_V = 0, 512, 1024, 2048, 2560, 2688
AB_IN = 2816

ADAM_LR, ADAM_B1, ADAM_B2, ADAM_EPS, ADAM_WD, ADAM_STEP = 0.001, 0.9, 0.999, 1e-08, 0.01, 10

VMEM_LIMIT = 56 * 1024 * 1024


def _cp(sem):
    return pltpu.CompilerParams(dimension_semantics=sem, vmem_limit_bytes=VMEM_LIMIT)


def _sig(x):
    return 1.0 / (1.0 + jnp.exp(-x))


def _dot(a, b):
    return lax.dot_general(a, b, (((1,), (0,)), ((), ())), preferred_element_type=F32)


def _dot_nt(a, b):
    return lax.dot_general(a, b, (((1,), (1,)), ((), ())), preferred_element_type=F32)


def _dot_tn(a, b):
    return lax.dot_general(a, b, (((0,), (0,)), ((), ())), preferred_element_type=F32)


def _mm(pairs, out_dtype, name, tm, tn, ta=False, tb=False, plan=None):
    pairs = [(a, b if isinstance(b, tuple) else (b, None)) for a, b in pairs]
    a0, (b0, _) = pairs[0]
    m = a0.shape[1] if ta else a0.shape[0]
    n = b0.shape[-2] if tb else b0.shape[-1]
    npairs = len(pairs)
    dims = (((0 if ta else 1,), (1 if tb else 0,)), ((), ()))
    c_in = len(plan["args"]) if plan else 0
    c_out = len(plan["out_shape"]) if plan else 0
    steps = (m // tm) * (n // tn)

    def body(*refs):
        o_ref = refs[2 * npairs + c_in]
        if plan:
            comm = (refs[2 * npairs:2 * npairs + c_in], refs[2 * npairs + c_in + 1:2 * npairs + c_in + 1 + c_out],
                    refs[2 * npairs + c_in + 1 + c_out:])
            step = pl.program_id(0) * (n // tn) + pl.program_id(1)

            @pl.when(step == 0)
            def _():
                plan["start"](*comm)

        acc = None
        for i in range(npairs):
            t = lax.dot_general(refs[2 * i][...].astype(BF16), refs[2 * i + 1][...].astype(BF16), dims,
                                preferred_element_type=F32)
            acc = t if acc is None else acc + t
        o_ref[...] = acc.astype(out_dtype)
        if plan:
            @pl.when(step == steps - 2)
            def _():
                plan["mid"](*comm)

            @pl.when(step == steps - 1)
            def _():
                plan["finish"](*comm)

    in_specs, args = [], []
    for a, (b, sel) in pairs:
        k = a.shape[0] if ta else a.shape[1]
        in_specs.append(pl.BlockSpec((k, tm), lambda i, j: (0, i)) if ta else pl.BlockSpec((tm, k), lambda i, j: (i, 0)))
        bshape, bidx = ((tn, k), lambda i, j: (j, 0)) if tb else ((k, tn), lambda i, j: (0, j))
        if sel is None:
            in_specs.append(pl.BlockSpec(bshape, bidx))
        else:
            in_specs.append(pl.BlockSpec((None,) + bshape, functools.partial(lambda i, j, f, s: (s,) + f(i, j), f=bidx, s=sel)))
        args += [a, b]
    out_spec = pl.BlockSpec((tm, tn), lambda i, j: (i, j))
    out_shape = jax.ShapeDtypeStruct((m, n), out_dtype)
    if not plan:
        return pl.pallas_call(
            body, grid=(m // tm, n // tn), in_specs=in_specs, out_specs=out_spec, out_shape=out_shape, name=name,
            compiler_params=_cp(("parallel", "parallel")))(*args)
    assert steps >= 2
    res = pl.pallas_call(
        body, grid=(m // tm, n // tn), in_specs=in_specs + [ANY] * c_in, out_specs=[out_spec] + [ANY] * c_out,
        out_shape=[out_shape] + plan["out_shape"], scratch_shapes=plan["sems"], name=name,
        compiler_params=_cp(("arbitrary", "arbitrary")))(*args, *plan["args"])
    return res[0], res[1:]


def _rms_fwd(h, g, name):
    def body(h_ref, g_ref, o_ref):
        x = h_ref[...]
        r = lax.rsqrt(jnp.mean(x * x, axis=1, keepdims=True) + NORM_EPS)
        o_ref[...] = (x * r * g_ref[...]).astype(BF16)

    return pl.pallas_call(
        body, grid=(R // TR,),
        in_specs=[pl.BlockSpec((TR, D), lambda n: (n, 0)), pl.BlockSpec((1, D), lambda n: (0, 0))],
        out_specs=pl.BlockSpec((TR, D), lambda n: (n, 0)),
        out_shape=jax.ShapeDtypeStruct((R, D), BF16), name=name, compiler_params=_cp(("parallel",)))(h, g)


def _rms_bwd(dout, x, g, res, out_dtype, name, split=False):
    has_res = res is not None

    def body(*refs):
        if split:
            refs = list(refs)
            dx_rest_ref = refs.pop(-2)
        if has_res:
            d_ref, x_ref, g_ref, r_ref, dx_ref, dg_ref = refs
        else:
            d_ref, x_ref, g_ref, dx_ref, dg_ref = refs
        n = pl.program_id(0)
        xv = x_ref[...]
        dv = d_ref[...]
        r = lax.rsqrt(jnp.mean(xv * xv, axis=1, keepdims=True) + NORM_EPS)
        xh = xv * r
        dxh = dv * g_ref[...]
        dx = r * (dxh - xh * jnp.mean(dxh * xh, axis=1, keepdims=True))
        if has_res:
            dx = dx + r_ref[...]
        row = lax.broadcasted_iota(jnp.int32, (TB, D), 0) + n * TB
        dx = jnp.where(row >= PAD, dx, 0.0).astype(out_dtype)
        if split:
            @pl.when(n == 0)
            def _():
                dx_ref[...] = dx

            @pl.when(n > 0)
            def _():
                dx_rest_ref[...] = dx
        else:
            dx_ref[...] = dx

        @pl.when(n == 0)
        def _():
            dg_ref[...] = jnp.zeros_like(dg_ref)

        dg_ref[...] += jnp.sum(dv * xh, axis=0, keepdims=True)

    blk = pl.BlockSpec((TB, D), lambda n: (n, 0))
    vec = pl.BlockSpec((1, D), lambda n: (0, 0))
    ins = [dout, x, g] + ([res] if has_res else [])
    in_specs = [blk, blk, vec] + ([blk] if has_res else [])
    if split:
        out_specs = [pl.BlockSpec((TB, D), lambda n: (0, 0)), pl.BlockSpec((TB, D), lambda n: (jnp.maximum(n - 1, 0), 0)), vec]
        out_shape = [jax.ShapeDtypeStruct((TB, D), out_dtype), jax.ShapeDtypeStruct((SEQ, D), out_dtype),
                     jax.ShapeDtypeStruct((1, D), F32)]
    else:
        out_specs = [blk, vec]
        out_shape = [jax.ShapeDtypeStruct((R, D), out_dtype), jax.ShapeDtypeStruct((1, D), F32)]
    return pl.pallas_call(
        body, grid=(NB,), in_specs=in_specs, out_specs=out_specs, out_shape=out_shape,
        name=name, compiler_params=_cp(("arbitrary",)))(*ins)


def _post_rms_fwd(h, y, g_post, g_next, name):
    def body(h_ref, y_ref, gp_ref, gn_ref, o_ref, x_ref):
        yv = y_ref[...]
        r = lax.rsqrt(jnp.mean(yv * yv, axis=1, keepdims=True) + NORM_EPS)
        hn = h_ref[...] + yv * r * gp_ref[...]
        o_ref[...] = hn
        r2 = lax.rsqrt(jnp.mean(hn * hn, axis=1, keepdims=True) + NORM_EPS)
        x_ref[...] = (hn * r2 * gn_ref[...]).astype(BF16)

    blk = pl.BlockSpec((TR, D), lambda n: (n, 0))
    vec = pl.BlockSpec((1, D), lambda n: (0, 0))
    return pl.pallas_call(
        body, grid=(R // TR,), in_specs=[blk, blk, vec, vec], out_specs=[blk, blk],
        out_shape=[jax.ShapeDtypeStruct((R, D), F32), jax.ShapeDtypeStruct((R, D), BF16)],
        name=name, compiler_params=_cp(("parallel",)))(h, y, g_post, g_next)


def _rms_post_bwd(dxn, h, g, res, y, g_post, name):
    def body(d_ref, h_ref, g_ref, r_ref, y_ref, gp_ref, dh_ref, dg_ref, dy_ref, dgp_ref):
        n = pl.program_id(0)

        @pl.when(n == 0)
        def _():
            dg_ref[...] = jnp.zeros_like(dg_ref)
            dgp_ref[...] = jnp.zeros_like(dgp_ref)

        hv, dv = h_ref[...], d_ref[...]
        r = lax.rsqrt(jnp.mean(hv * hv, axis=1, keepdims=True) + NORM_EPS)
        xh = hv * r
        dxh = dv * g_ref[...]
        dh = r * (dxh - xh * jnp.mean(dxh * xh, axis=1, keepdims=True)) + r_ref[...]
        row = lax.broadcasted_iota(jnp.int32, (TR, D), 0) + n * TR
        dh = jnp.where(row >= PAD, dh, 0.0)
        dh_ref[...] = dh
        dg_ref[...] += jnp.sum(dv * xh, axis=0, keepdims=True)
        yv = y_ref[...]
        ry = lax.rsqrt(jnp.mean(yv * yv, axis=1, keepdims=True) + NORM_EPS)
        yh = yv * ry
        dyh = dh * gp_ref[...]
        dy_ref[...] = (ry * (dyh - yh * jnp.mean(dyh * yh, axis=1, keepdims=True))).astype(BF16)
        dgp_ref[...] += jnp.sum(dh * yh, axis=0, keepdims=True)

    blk = pl.BlockSpec((TR, D), lambda n: (n, 0))
    vec = pl.BlockSpec((1, D), lambda n: (0, 0))
    return pl.pallas_call(
        body, grid=(R // TR,), in_specs=[blk, blk, vec, blk, blk, vec], out_specs=[blk, vec, blk, vec],
        out_shape=[jax.ShapeDtypeStruct((R, D), F32), jax.ShapeDtypeStruct((1, D), F32),
                   jax.ShapeDtypeStruct((R, D), BF16), jax.ShapeDtypeStruct((1, D), F32)],
        name=name, compiler_params=_cp(("arbitrary",)))(dxn, h, g, res, y, g_post)


GW = 512


def _gate_fwd(o, gsrc, goff, name):
    w = o.shape[1]

    def body(o_ref, g_ref, m_ref):
        gv = g_ref[...]
        m_ref[...] = (o_ref[...] * (gv * _sig(gv))).astype(BF16)

    gb = goff // GW
    return pl.pallas_call(
        body, grid=(R // TR, w // GW),
        in_specs=[pl.BlockSpec((TR, GW), lambda n, j: (n, j)), pl.BlockSpec((TR, GW), lambda n, j: (n, gb + j))],
        out_specs=pl.BlockSpec((TR, GW), lambda n, j: (n, j)),
        out_shape=jax.ShapeDtypeStruct((R, w), BF16), name=name,
        compiler_params=_cp(("parallel", "parallel")))(o, gsrc)


def _gate_bwd(dsrc, doff, o, gsrc, goff, name):
    w = o.shape[1]

    def body(d_ref, o_ref, g_ref, do_ref, dg_ref):
        gv = g_ref[...]
        dv = d_ref[...]
        s = _sig(gv)
        do_ref[...] = dv * (gv * s)
        dg_ref[...] = (dv * o_ref[...] * (s * (1.0 + gv * (1.0 - s)))).astype(BF16)

    db, gb = doff // GW, goff // GW
    blk = pl.BlockSpec((TR, GW), lambda n, j: (n, j))
    return pl.pallas_call(
        body, grid=(R // TR, w // GW),
        in_specs=[pl.BlockSpec((TR, GW), lambda n, j: (n, db + j)), blk,
                  pl.BlockSpec((TR, GW), lambda n, j: (n, gb + j))],
        out_specs=[blk, blk],
        out_shape=[jax.ShapeDtypeStruct((R, w), F32), jax.ShapeDtypeStruct((R, w), BF16)], name=name,
        compiler_params=_cp(("parallel", "parallel")))(dsrc, o, gsrc)


def _tail(h, y, g, target):
    def body(h_ref, y_ref, g_ref, t_ref, d_ref, dy_ref, dg_ref, l_ref):
        n = pl.program_id(0)

        @pl.when(n == 0)
        def _():
            d_ref[...] = jnp.zeros_like(d_ref)
            dy_ref[...] = jnp.zeros_like(dy_ref)
            dg_ref[...] = jnp.zeros_like(dg_ref)
            l_ref[...] = jnp.zeros_like(l_ref)

        @pl.when(n > 0)
        def _():
            yv = y_ref[...]
            r = lax.rsqrt(jnp.mean(yv * yv, axis=1, keepdims=True) + NORM_EPS)
            yh = yv * r
            err = (h_ref[...] + yh * g_ref[...]) - t_ref[...]
            dv = err * (1.0 / D)
            d_ref[...] = dv
            l_ref[...] += jnp.sum(err * err, axis=0, keepdims=True)
            dyh = dv * g_ref[...]
            dy_ref[...] = (r * (dyh - yh * jnp.mean(dyh * yh, axis=1, keepdims=True))).astype(BF16)
            dg_ref[...] += jnp.sum(dv * yh, axis=0, keepdims=True)

        @pl.when(n == NB - 1)
        def _():
            tot = jnp.sum(l_ref[...], axis=1, keepdims=True) * (0.5 / D)
            l_ref[...] = jnp.broadcast_to(tot, (1, D))

    blk = pl.BlockSpec((TB, D), lambda n: (n, 0))
    vec = pl.BlockSpec((1, D), lambda n: (0, 0))
    return pl.pallas_call(
        body, grid=(NB,),
        in_specs=[blk, blk, vec, pl.BlockSpec((TB, D), lambda n: (jnp.maximum(n - 1, 0), 0))],
        out_specs=[blk, blk, vec, vec],
        out_shape=[jax.ShapeDtypeStruct((R, D), F32), jax.ShapeDtypeStruct((R, D), BF16),
                   jax.ShapeDtypeStruct((1, D), F32), jax.ShapeDtypeStruct((1, D), F32)],
        name="tail", compiler_params=_cp(("arbitrary",)))(h, y, g, target)


def _lane_row(shape):
    return lax.broadcasted_iota(jnp.int32, shape, 1), lax.broadcasted_iota(jnp.int32, shape, 0)


def _rot_half(x, lane):
    return jnp.where(lane % HD < HD // 2, pltpu.roll(x, 128 - HD // 2, 1), pltpu.roll(x, HD // 2, 1))


def _swa_blocks(n):
    return (0, jnp.maximum(n - 1, 0), n)


SWA_STACKS = ((0, 0), (0, 1), (1, 0), (1, 1))


def _swa_masks(n, lane, row):
    qpos = n * TB + (row & (TB - 1))
    kp = (n - 1) * TB + lane
    kc = n * TB + lane
    m0 = (lane >= PAD) & (qpos - lane >= TB)
    mp = (kp >= PAD) & (qpos >= kp) & (qpos - kp < TB)
    mc = (kc >= PAD) & (qpos >= kc)
    return (m0, mp, mc)


def _stack_pair(xa, xb, par):
    lane = lax.broadcasted_iota(jnp.int32, (TB, 128), 1)
    keep = (lane < HD) if par == 0 else (lane >= HD)
    return jnp.concatenate([jnp.where(keep, xa, 0.0), jnp.where(keep, xb, 0.0)], axis=0)


def _per_head(a, b):
    row = lax.broadcasted_iota(jnp.int32, (2 * TB, 1), 0)
    return jnp.where(row < TB, a, b)


def _swa_load(n, zq_ref, zkv_ref, cs_ref, sn_ref, lane):
    r0 = pl.multiple_of(n * TB, TB)
    csq, snq = cs_ref[pl.ds(r0, TB), :], sn_ref[pl.ds(r0, TB), :]
    qc = []
    for c in range(4):
        x = zq_ref[:, c * 128:(c + 1) * 128]
        qc.append((x * csq + _rot_half(x, lane) * snq) * SCALE)
    qst = [_stack_pair(qc[2 * g], qc[2 * g + 1], par).astype(BF16) for g, par in SWA_STACKS]
    kvs = []
    for b in _swa_blocks(n):
        b0 = pl.multiple_of(b * TB, TB)
        csb, snb = cs_ref[pl.ds(b0, TB), :], sn_ref[pl.ds(b0, TB), :]
        kx = zkv_ref[pl.ds(b0, TB), 0:128]
        kr = kx * csb + _rot_half(kx, lane) * snb
        vx = zkv_ref[pl.ds(b0, TB), 128:256]
        kvs.append((kr.astype(BF16), pltpu.roll(kr, HD, 1).astype(BF16),
                    vx.astype(BF16), pltpu.roll(vx, HD, 1).astype(BF16), csb, snb, b0))
    return qst, (csq, snq), kvs


def _swa_fwd(z0, cs, sn, sinks):
    def body(zq_ref, zkv_ref, cs_ref, sn_ref, sk_ref, ga_ref, o_ref, a_ref, lse_ref):
        n = pl.program_id(0)
        lane, row = _lane_row((TB, 128))
        lo = lane < HD
        masks = _swa_masks(n, *_lane_row((2 * TB, 128)))
        qst, _, kvs = _swa_load(n, zq_ref, zkv_ref, cs_ref, sn_ref, lane)
        ss = [[jnp.where(m, _dot_nt(qst[si], k if par == g else ka), NEG)
               for (k, ka, _, _, _, _, _), m in zip(kvs, masks)] for si, (g, par) in enumerate(SWA_STACKS)]
        o2, lse2 = [], []
        for si, (g, par) in enumerate(SWA_STACKS):
            sink = _per_head(sk_ref[0, 4 * g + par], sk_ref[0, 4 * g + 2 + par])
            s = ss[si]
            mx = jnp.maximum(jnp.maximum(jnp.max(s[0], axis=1, keepdims=True), jnp.max(s[1], axis=1, keepdims=True)),
                             jnp.max(s[2], axis=1, keepdims=True))
            mx = jnp.maximum(mx, sink)
            es = [jnp.exp(sb - mx) for sb in s]
            den = (jnp.sum(es[0], axis=1, keepdims=True) + jnp.sum(es[1], axis=1, keepdims=True)
                   + jnp.sum(es[2], axis=1, keepdims=True) + jnp.exp(sink - mx))
            inv = 1.0 / den
            t = jnp.zeros((2 * TB, 128), F32)
            for (_, _, v, va, _, _, _), e in zip(kvs, es):
                t = t + _dot((e * inv).astype(BF16), v if par == g else va)
            o2.append(t)
            lse2.append(mx + jnp.log(den))
        lse_t = jnp.zeros((TB, 128), F32)
        for g in range(2):
            for t in range(2):
                rows = slice(t * TB, (t + 1) * TB)
                c = 2 * g + t
                oc = jnp.where(lo, o2[2 * g][rows], o2[2 * g + 1][rows])
                o_ref[:, c * 128:(c + 1) * 128] = oc
                gv = ga_ref[:, c * 128:(c + 1) * 128]
                a_ref[:, c * 128:(c + 1) * 128] = (oc * (gv * _sig(gv))).astype(BF16)
                for par in range(2):
                    lse_t = jnp.where(lane == 4 * g + 2 * t + par, lse2[2 * g + par][rows], lse_t)
        lse_ref[...] = lse_t

    full = pl.BlockSpec((R, 128), lambda n: (0, 0))
    return dict(
        body=body,
        in_specs=[pl.BlockSpec((TB, 512), lambda n: (n, C_Q // 512)),
                  pl.BlockSpec((R, 256), lambda n: (0, C_K // 256)), full, full,
                  pl.BlockSpec(memory_space=pltpu.SMEM), pl.BlockSpec((TB, 512), lambda n: (n, C_GA // 512))],
        args=[z0, z0, cs, sn, sinks, z0],
        out_specs=[pl.BlockSpec((TB, 512), lambda n: (n, 0)), pl.BlockSpec((TB, 512), lambda n: (n, 0)),
                   pl.BlockSpec((TB, 128), lambda n: (n, 0))],
        out_shape=[jax.ShapeDtypeStruct((R, 512), F32), jax.ShapeDtypeStruct((R, 512), BF16),
                   jax.ShapeDtypeStruct((R, 128), F32)],
        scratch=[])


def _swa_bwd(z0, cs, sn, sinks, o, dmix, lse):
    def body(zq_ref, zkv_ref, cs_ref, sn_ref, sk_ref, ga_ref, o_ref, dm_ref, lse_ref,
             dq_ref, dga_ref, dkv_ref, dsk_ref, do_ref, acc_ref):
        n = pl.program_id(0)

        @pl.when(n == 0)
        def _():
            acc_ref[...] = jnp.zeros_like(acc_ref)
            dsk_ref[...] = jnp.zeros_like(dsk_ref)

        gv, dmv = ga_ref[...], dm_ref[...]
        sg = _sig(gv)
        dga_ref[...] = (dmv * o_ref[...] * (sg * (1.0 + gv * (1.0 - sg)))).astype(BF16)
        do_ref[...] = dmv * (gv * sg)
        lane, row = _lane_row((TB, 128))
        lo = lane < HD
        masks = _swa_masks(n, *_lane_row((2 * TB, 128)))
        qst, (csq, snq), kvs = _swa_load(n, zq_ref, zkv_ref, cs_ref, sn_ref, lane)
        lse_t = lse_ref[...]
        ss = [[jnp.where(m, _dot_nt(qst[si], k if par == g else ka), NEG)
               for (k, ka, _, _, _, _, _), m in zip(kvs, masks)] for si, (g, par) in enumerate(SWA_STACKS)]
        dobs, deltas, lses, dps = [], [], [], []
        for g, par in SWA_STACKS:
            ca, cb = slice(2 * g * 128, (2 * g + 1) * 128), slice((2 * g + 1) * 128, (2 * g + 2) * 128)
            dom = _stack_pair(do_ref[:, ca], do_ref[:, cb], par)
            deltas.append(jnp.sum(dom * jnp.concatenate([o_ref[:, ca], o_ref[:, cb]], axis=0), axis=1, keepdims=True))
            dob = dom.astype(BF16)
            dobs.append(dob)
            lses.append(jnp.concatenate(
                [jnp.sum(jnp.where(lane == 4 * g + 2 * t + par, lse_t, 0.0), axis=1, keepdims=True) for t in range(2)],
                axis=0))
            dps.append([_dot_nt(dob, v if par == g else va) for (_, _, v, va, _, _, _) in kvs])
        dk_al = [jnp.zeros((TB, 128), F32) for _ in range(3)]
        dk_mis = [jnp.zeros((TB, 128), F32) for _ in range(3)]
        dv_al = [jnp.zeros((TB, 128), F32) for _ in range(3)]
        dv_mis = [jnp.zeros((TB, 128), F32) for _ in range(3)]
        dsk_t = jnp.zeros((TB, 128), F32)
        dq2 = []
        for si, (g, par) in enumerate(SWA_STACKS):
            dqt = jnp.zeros((2 * TB, 128), F32)
            for bi, (k, ka, _, _, _, _, _) in enumerate(kvs):
                p = jnp.exp(ss[si][bi] - lses[si])
                ds = (p * (dps[si][bi] - deltas[si])).astype(BF16)
                dqt = dqt + _dot(ds, k if par == g else ka)
                dkh = _dot_tn(ds, qst[si])
                dvh = _dot_tn(p.astype(BF16), dobs[si])
                if par == g:
                    dk_al[bi] = dk_al[bi] + dkh
                    dv_al[bi] = dv_al[bi] + dvh
                else:
                    dk_mis[bi] = dk_mis[bi] + dkh
                    dv_mis[bi] = dv_mis[bi] + dvh
            dq2.append(dqt)
            sink = _per_head(sk_ref[0, 4 * g + par], sk_ref[0, 4 * g + 2 + par])
            dsk = -jnp.exp(sink - lses[si]) * deltas[si]
            for t in range(2):
                dsk_t = jnp.where(lane == 4 * g + 2 * t + par, dsk[t * TB:(t + 1) * TB], dsk_t)
        for g in range(2):
            for t in range(2):
                rows = slice(t * TB, (t + 1) * TB)
                c = 2 * g + t
                dqc = jnp.where(lo, dq2[2 * g][rows], dq2[2 * g + 1][rows]) * SCALE
                dq_ref[:, c * 128:(c + 1) * 128] = (dqc * csq + _rot_half(dqc * snq, lane)).astype(BF16)
        for bi, (_, _, _, _, csb, snb, b0) in enumerate(kvs):
            dk = dk_al[bi] + pltpu.roll(dk_mis[bi], HD, 1)
            dv = dv_al[bi] + pltpu.roll(dv_mis[bi], HD, 1)
            acc_ref[pl.ds(b0, TB), 0:128] += dk * csb + _rot_half(dk * snb, lane)
            acc_ref[pl.ds(b0, TB), 128:256] += dv
        dsk_ref[0:1, :] += jnp.sum(dsk_t, axis=0, keepdims=True)

        @pl.when(n == NB - 1)
        def _():
            dkv_ref[...] = acc_ref[...].astype(BF16)

    full = pl.BlockSpec((R, 128), lambda n: (0, 0))
    b512 = pl.BlockSpec((TB, 512), lambda n: (n, 0))
    return dict(
        body=body,
        in_specs=[pl.BlockSpec((TB, 512), lambda n: (n, C_Q // 512)),
                  pl.BlockSpec((R, 256), lambda n: (0, C_K // 256)), full, full,
                  pl.BlockSpec(memory_space=pltpu.SMEM), pl.BlockSpec((TB, 512), lambda n: (n, C_GA // 512)),
                  b512, b512, pl.BlockSpec((TB, 128), lambda n: (n, 0))],
        args=[z0, z0, cs, sn, sinks, z0, o, dmix, lse],
        out_specs=[b512, b512, pl.BlockSpec((R, 256), lambda n: (0, 0)), pl.BlockSpec((8, 128), lambda n: (0, 0))],
        out_shape=[jax.ShapeDtypeStruct((R, 512), BF16), jax.ShapeDtypeStruct((R, 512), BF16),
                   jax.ShapeDtypeStruct((R, 256), BF16), jax.ShapeDtypeStruct((8, 128), F32)],
        scratch=[pltpu.VMEM((TB, 512), F32), pltpu.VMEM((R, 256), F32)])


CC = 512
HALO = CONV_W - 1


def _conv_fwd(z0, conv_w, conv_b, ln_g, ln_b):
    def body(g_ref, w_ref, cb_ref, lg_ref, lb_ref, cv_ref, s_ref, ubuf):
        n = pl.program_id(0)

        @pl.when(n == 0)
        def _():
            ubuf[...] = jnp.zeros_like(ubuf)

        u = g_ref[:, 0:CC] * _sig(g_ref[:, CC:2 * CC])
        for k in range(8):
            ubuf[k, 0:TB + 8, :] = ubuf[k, TB:2 * TB + 8, :]
            ubuf[k, pl.ds(TB + 8 - k, TB), :] = u
        acc = jnp.zeros((TB, CC), F32)
        for w in range(CONV_W):
            off = TB - HALO + w
            acc = acc + ubuf[off % 8, pl.ds(off + 8 - off % 8, TB), :] * w_ref[w:w + 1, :]
        cv = acc + cb_ref[...]
        cv_ref[...] = cv
        xc = cv - jnp.mean(cv, axis=1, keepdims=True)
        rs = lax.rsqrt(jnp.mean(xc * xc, axis=1, keepdims=True) + LN_EPS)
        ln = xc * rs * lg_ref[...] + lb_ref[...]
        s_ref[...] = (ln * _sig(ln)).astype(BF16)

    vec = pl.BlockSpec((1, CC), lambda n: (0, 0))
    blk = pl.BlockSpec((TB, CC), lambda n: (n, 0))
    return dict(
        body=body,
        in_specs=[pl.BlockSpec((TB, 2 * CC), lambda n: (n, C_GLU // (2 * CC))),
                  pl.BlockSpec((32, CC), lambda n: (0, 0)), vec, vec, vec],
        args=[z0, conv_w, conv_b, ln_g, ln_b],
        out_specs=[blk, blk],
        out_shape=[jax.ShapeDtypeStruct((R, CC), F32), jax.ShapeDtypeStruct((R, CC), BF16)],
        scratch=[pltpu.VMEM((8, 2 * TB + 8, CC), F32)])


def _conv_bwd(ds, cv, z0, conv_w, ln_g, ln_b):
    def body(ds_ref, cv_ref, g_ref, w_ref, lg_ref, lb_ref, dglu_ref, dw_ref, dsm_ref, dbuf):
        n = pl.program_id(0)

        @pl.when(n == 0)
        def _():
            dbuf[...] = jnp.zeros_like(dbuf)
            dw_ref[...] = jnp.zeros_like(dw_ref)
            dsm_ref[...] = jnp.zeros_like(dsm_ref)

        cv = cv_ref[...]
        xc = cv - jnp.mean(cv, axis=1, keepdims=True)
        rs = lax.rsqrt(jnp.mean(xc * xc, axis=1, keepdims=True) + LN_EPS)
        xh = xc * rs
        ln = xh * lg_ref[...] + lb_ref[...]
        sg = _sig(ln)
        dln = ds_ref[...] * (sg * (1.0 + ln * (1.0 - sg)))
        dxh = dln * lg_ref[...]
        dcv = rs * (dxh - jnp.mean(dxh, axis=1, keepdims=True) - xh * jnp.mean(dxh * xh, axis=1, keepdims=True))
        dsm_ref[0:1, :] += jnp.sum(dcv, axis=0, keepdims=True)
        dsm_ref[1:2, :] += jnp.sum(dln * xh, axis=0, keepdims=True)
        dsm_ref[2:3, :] += jnp.sum(dln, axis=0, keepdims=True)
        for k in range(8):
            dbuf[k, TB:2 * TB + 8, :] = dbuf[k, 0:TB + 8, :]
            dbuf[k, pl.ds(8 - k, TB), :] = dcv
        a = g_ref[:, 0:CC]
        sb = _sig(g_ref[:, CC:2 * CC])
        u = a * sb
        du = jnp.zeros((TB, CC), F32)
        for w in range(CONV_W):
            off = HALO - w
            sh = dbuf[off % 8, pl.ds(off + 8 - off % 8, TB), :]
            du = du + sh * w_ref[w:w + 1, :]
            dw_ref[w:w + 1, :] += jnp.sum(u * sh, axis=0, keepdims=True)
        dglu_ref[:, 0:CC] = (du * sb).astype(BF16)
        dglu_ref[:, CC:2 * CC] = (du * a * sb * (1.0 - sb)).astype(BF16)

    rev = lambda n: (NB - 1 - n, 0)
    vec = pl.BlockSpec((1, CC), lambda n: (0, 0))
    blk = pl.BlockSpec((TB, CC), rev)
    return dict(
        body=body,
        in_specs=[blk, blk, pl.BlockSpec((TB, 2 * CC), lambda n: (NB - 1 - n, C_GLU // (2 * CC))),
                  pl.BlockSpec((32, CC), lambda n: (0, 0)), vec, vec],
        args=[ds, cv, z0, conv_w, ln_g, ln_b],
        out_specs=[pl.BlockSpec((TB, 2 * CC), rev), pl.BlockSpec((32, CC), lambda n: (0, 0)),
                   pl.BlockSpec((8, CC), lambda n: (0, 0))],
        out_shape=[jax.ShapeDtypeStruct((R, 2 * CC), BF16), jax.ShapeDtypeStruct((32, CC), F32),
                   jax.ShapeDtypeStruct((8, CC), F32)],
        scratch=[pltpu.VMEM((8, 2 * TB + 8, CC), F32)])


def _split_dot(x, t):
    hi = x.astype(BF16)
    lo = (x - hi.astype(F32)).astype(BF16)
    return _dot(hi, t) + _dot(lo, t)


def _stack_heads(x):
    lane = lax.broadcasted_iota(jnp.int32, (TB, 128), 1)
    return jnp.concatenate([jnp.where(lane < HD, x, 0.0), jnp.where(lane < HD, 0.0, x)], axis=0).astype(BF16)


def _sb_stack(qv, i):
    lane2, row2 = _lane_row((2 * TB, 128))
    qpos2 = i * TB + (row2 & (TB - 1))
    lane, row = _lane_row((TB, 128))
    return _stack_heads(qv), lane2, qpos2, (row > lane).astype(BF16)


SB_U = 3
SB_DEAD = -104.0
SB_P = 4


def _sb_fwd(q, k, v, g):
    def body(q_ref, k_ref, v_ref, g_ref, o_ref, m_ref, c_ref, n_ref):
        p, i = pl.program_id(0), pl.program_id(1)
        lane, row = _lane_row((TB, 128))
        lo = lane < HD
        slabs = [slice(s * 128, (s + 1) * 128) for s in range(SB_P)]
        q2s = []
        for sl in slabs:
            q2, lane2, qpos2, tri_gt = _sb_stack(q_ref[:, sl].astype(F32) * SCALE, i)
            q2s.append(q2)

        def cond(st):
            t, _, c2s = st
            alive = jnp.max(c2s[0])
            for c2 in c2s[1:]:
                alive = jnp.maximum(alive, jnp.max(c2))
            return jnp.logical_and(i - SB_U * t >= 0, alive > SB_DEAD)

        def step(st):
            t, accs, c2s = st
            accs, c2s = list(accs), list(c2s)
            jrs = [i - SB_U * t - u for u in range(SB_U)]
            j0s = [pl.multiple_of(jnp.maximum(jr, 0) * TB, TB) for jr in jrs]
            valids = []
            for jr in jrs:
                kpos = jr * TB + lane2
                valids.append((kpos >= PAD) & (kpos < qpos2))
            zs = [[jnp.where(valid, _dot_nt(q2s[s], k_ref[pl.ds(j0, TB), slabs[s]]), NEG)
                   for j0, valid in zip(j0s, valids)] for s in range(SB_P)]
            lbs, l1s = [], []
            for s in range(SB_P):
                lbs.append([jnp.minimum(z, 0.0) - jnp.log(1.0 + jnp.exp(-jnp.abs(z))) for z in zs[s]])
                l1s.append([lb - z for lb, z in zip(lbs[s], zs[s])])
            sfxs = [[_split_dot(l1, tri_gt) for l1 in l1s[s]] for s in range(SB_P)]
            carries = []
            for s in range(SB_P):
                cs, c2 = [], c2s[s]
                for jr, l1 in zip(jrs, l1s[s]):
                    cs.append(c2)
                    c_ref[:, slabs[s]] = jnp.where(lane == 2 * jr, c2[0:TB],
                                                   jnp.where(lane == 2 * jr + 1, c2[TB:2 * TB], c_ref[:, slabs[s]]))
                    c2 = c2 + jnp.sum(l1, axis=1, keepdims=True)
                carries.append(cs)
                c2s[s] = c2
            for s in range(SB_P):
                for j0, valid, lb, sfx, cu in zip(j0s, valids, lbs[s], sfxs[s], carries[s]):
                    a = jnp.exp(lb + sfx + cu).astype(BF16)
                    av = _dot(a, v_ref[pl.ds(j0, TB), slabs[s]])
                    accs[s] = accs[s] + jnp.where(lo, av[0:TB], av[TB:2 * TB])
            return t + 1, tuple(accs), tuple(c2s)

        c_ref[...] = jnp.zeros_like(c_ref)
        init = (jnp.int32(0), tuple(jnp.zeros((TB, 128), F32) for _ in slabs),
                tuple(jnp.zeros((2 * TB, 1), F32) for _ in slabs))
        t, accs, _ = lax.while_loop(cond, step, init)
        for sl, acc in zip(slabs, accs):
            o_ref[:, sl] = acc
            gv = g_ref[:, sl]
            m_ref[:, sl] = (acc * (gv * _sig(gv))).astype(BF16)
        n_ref[p, i] = t

    wide = SB_P * 128
    slab = pl.BlockSpec((R, wide), lambda p, i: (0, p))
    blk = pl.BlockSpec((TB, wide), lambda p, i: (i, p))
    sd = jax.ShapeDtypeStruct((R, D), F32)
    return pl.pallas_call(
        body, grid=(D // wide, NB), in_specs=[blk, slab, slab, blk],
        out_specs=[blk, blk, blk, pl.BlockSpec(memory_space=pltpu.SMEM)],
        out_shape=[sd, jax.ShapeDtypeStruct((R, D), BF16), sd, jax.ShapeDtypeStruct((D // wide, NB), jnp.int32)],
        name="sb_fwd", compiler_params=_cp(("arbitrary", "arbitrary")))(q, k, v, g)


def _sb_bwd(trips, q, k, v, car, dm, g, o):
    def body(n_ref, q_ref, k_ref, v_ref, c_ref, dm_ref, g_ref, o_ref, dq_ref, dko_ref, dvo_ref, dg_ref,
             dk_ref, dv_ref):
        p, i = pl.program_id(0), pl.program_id(1)

        @pl.when(i == 0)
        def _():
            dk_ref[...] = jnp.zeros_like(dk_ref)
            dv_ref[...] = jnp.zeros_like(dv_ref)

        lane, row = _lane_row((TB, 128))
        lo = lane < HD
        tri_lt = (row < lane).astype(BF16)
        slabs = [slice(s * 128, (s + 1) * 128) for s in range(SB_P)]
        q2s, do2s, cts = [], [], []
        for sl in slabs:
            q2, lane2, qpos2, tri_gt = _sb_stack(q_ref[:, sl].astype(F32) * SCALE, i)
            q2s.append(q2)
            gv, dmv = g_ref[:, sl], dm_ref[:, sl]
            sg = _sig(gv)
            dg_ref[:, sl] = (dmv * o_ref[:, sl] * (sg * (1.0 + gv * (1.0 - sg)))).astype(BF16)
            do2s.append(_stack_heads(dmv * (gv * sg)))
            cts.append(c_ref[:, sl])
        trips_i = n_ref[p, i]
        first = jnp.maximum(i + 1 - SB_U * trips_i, 0)

        def step(t, carry):
            dqs, g2s = carry
            dqs, g2s = list(dqs), list(g2s)
            jrs = [first + SB_U * t + u for u in range(SB_U)]
            j0s = [pl.multiple_of(jnp.minimum(jr, i) * TB, TB) for jr in jrs]
            valids = []
            for jr in jrs:
                kpos = jr * TB + lane2
                valids.append((kpos >= PAD) & (kpos < qpos2))
            ks = [[k_ref[pl.ds(j0, TB), sl] for j0 in j0s] for sl in slabs]
            zs = [[jnp.where(valid, _dot_nt(q2s[s], kj), NEG) for kj, valid in zip(ks[s], valids)] for s in range(SB_P)]
            das = [[_dot_nt(do2s[s], v_ref[pl.ds(j0, TB), slabs[s]]) for j0 in j0s] for s in range(SB_P)]
            es = [[jnp.exp(-jnp.abs(z)) for z in zs[s]] for s in range(SB_P)]
            lbs = [[jnp.minimum(z, 0.0) - jnp.log(1.0 + e) for z, e in zip(zs[s], es[s])] for s in range(SB_P)]
            l1s = [[lb - z for lb, z in zip(lbs[s], zs[s])] for s in range(SB_P)]
            sfxs = [[_split_dot(l1, tri_gt) for l1 in l1s[s]] for s in range(SB_P)]
            a_s, gmats, gpres = [], [], []
            for s in range(SB_P):
                a_l, gm_l, gp_l, g2 = [], [], [], g2s[s]
                for jr, valid, lb, sfx, da in zip(jrs, valids, lbs[s], sfxs[s], das[s]):
                    later = jnp.concatenate(
                        [jnp.sum(jnp.where(lane == 2 * jr + hh, cts[s], 0.0), axis=1, keepdims=True) for hh in range(2)],
                        axis=0)
                    a = jnp.exp(lb + sfx + later)
                    gmat = da * a
                    a_l.append(a.astype(BF16))
                    gm_l.append(gmat)
                    gp_l.append(g2)
                    g2 = g2 + jnp.sum(gmat, axis=1, keepdims=True)
                a_s.append(a_l)
                gmats.append(gm_l)
                gpres.append(gp_l)
                g2s[s] = g2
            pres = [[gp + _split_dot(gmat, tri_lt) for gp, gmat in zip(gpres[s], gmats[s])] for s in range(SB_P)]
            for s in range(SB_P):
                for j0, kj, valid, z, e, gmat, pre, a in zip(j0s, ks[s], valids, zs[s], es[s], gmats[s], pres[s], a_s[s]):
                    r = 1.0 / (1.0 + e)
                    big = z >= 0.0
                    beta = jnp.where(big, r, e * r)
                    omb = jnp.where(big, e * r, r)
                    dz = (gmat * omb - beta * pre).astype(BF16)
                    dq2 = _dot(dz, kj)
                    dqs[s] = dqs[s] + jnp.where(lo, dq2[0:TB], dq2[TB:2 * TB])
                    dk_ref[pl.ds(j0, TB), slabs[s]] += _dot_tn(dz, q2s[s])
                    dv_ref[pl.ds(j0, TB), slabs[s]] += _dot_tn(a, do2s[s])
            return tuple(dqs), tuple(g2s)

        init = (tuple(jnp.zeros((TB, 128), F32) for _ in slabs), tuple(jnp.zeros((2 * TB, 1), F32) for _ in slabs))
        dqs, _ = lax.fori_loop(0, trips_i, step, init)
        for sl, dq in zip(slabs, dqs):
            dq_ref[:, sl] = (dq * SCALE).astype(BF16)

        @pl.when(i == NB - 1)
        def _():
            dko_ref[...] = dk_ref[...].astype(BF16)
            dvo_ref[...] = dv_ref[...].astype(BF16)

    wide = SB_P * 128
    slab = pl.BlockSpec((R, wide), lambda p, i: (0, p))
    blk = pl.BlockSpec((TB, wide), lambda p, i: (i, p))
    sd = jax.ShapeDtypeStruct((R, D), BF16)
    return pl.pallas_call(
        body, grid=(D // wide, NB),
        in_specs=[pl.BlockSpec(memory_space=pltpu.SMEM), blk, slab, slab, blk, blk, blk, blk],
        out_specs=[blk, slab, slab, blk], out_shape=[sd, sd, sd, sd],
        scratch_shapes=[pltpu.VMEM((R, wide), F32), pltpu.VMEM((R, wide), F32)], name="sb_bwd",
        compiler_params=_cp(("arbitrary", "arbitrary")))(trips, q, k, v, car, dm, g, o)


def _adamw(w, parts, m, v, name):
    rows, cols = w.shape
    tr = next((t for t in (256, 176) if rows % t == 0), rows)
    nparts = len(parts)

    def body(*refs):
        w_ref = refs[0]
        p_refs = refs[1:1 + nparts]
        m_ref, v_ref, g_ref, d_ref, nm_ref, nv_ref = refs[1 + nparts:]
        g = p_refs[0][...]
        for p_ref in p_refs[1:]:
            g = g + p_ref[...]
        nm = ADAM_B1 * m_ref[...] + (1.0 - ADAM_B1) * g
        nv = ADAM_B2 * v_ref[...] + (1.0 - ADAM_B2) * (g * g)
        m_hat = nm / (1.0 - ADAM_B1 ** ADAM_STEP)
        v_hat = nv / (1.0 - ADAM_B2 ** ADAM_STEP)
        g_ref[...] = g
        d_ref[...] = -ADAM_LR * (m_hat / (jnp.sqrt(v_hat) + ADAM_EPS) + ADAM_WD * w_ref[...])
        nm_ref[...] = nm
        nv_ref[...] = nv

    blk = pl.BlockSpec((tr, cols), lambda i: (i, 0))
    sd = jax.ShapeDtypeStruct((rows, cols), F32)
    return pl.pallas_call(
        body, grid=(rows // tr,), in_specs=[blk] * (3 + nparts), out_specs=[blk] * 4, out_shape=[sd] * 4,
        name=name, compiler_params=_cp(("parallel",)))(w, *parts, m, v)


def _sum8(buf, name):
    _, rows, cols = buf.shape

    def body(b_ref, o_ref):
        acc = b_ref[0]
        for i in range(1, 8):
            acc = acc + b_ref[i]
        o_ref[...] = acc

    return pl.pallas_call(
        body, out_shape=jax.ShapeDtypeStruct((rows, cols), F32), name=name,
        compiler_params=pltpu.CompilerParams(vmem_limit_bytes=VMEM_LIMIT))(buf)


MESH = pl.DeviceIdType.MESH
ANY = pl.BlockSpec(memory_space=pl.ANY)


def _chip_peers():
    x, y = lax.axis_index("x"), lax.axis_index("y")
    return [(1 - x, y), (x, 1 - y), (1 - x, 1 - y)]


def _gather_chips(shards):
    plan = _gather_plan(shards)

    def body(*refs):
        n = len(shards)
        ins, outs, sems = refs[:n], refs[n:2 * n], refs[2 * n:]
        plan["start"](ins, outs, sems)
        plan["mid"](ins, outs, sems)
        plan["finish"](ins, outs, sems)

    n = len(shards)
    res = pl.pallas_call(
        body, in_specs=[ANY] * n, out_specs=[ANY] * n, out_shape=plan["out_shape"],
        scratch_shapes=plan["sems"], name="gather_chips")(*plan["args"])
    return plan["post"](res)


def _gather_plan(shards):
    n = len(shards)
    shards = [s.reshape((2, s.shape[0] // 2) + s.shape[1:]) for s in shards]

    def copies(kind, ins, outs, sems):
        s1, r1, s2, r2 = sems
        x, y, c = lax.axis_index("x"), lax.axis_index("y"), lax.axis_index("c")
        me = 2 * x + y
        out = []
        for j, (px, py) in enumerate(_chip_peers()):
            for a in range(n):
                k = j * n + a
                got = outs[a].at[2 * px + py].at[c]
                other = outs[a].at[2 * px + py].at[1 - c]
                src, dst, ss, rs, dev = {
                    "first": (ins[a].at[c], outs[a].at[me].at[c], s1, r1, (px, py, c)),
                    "landed": (got, got, s1, r1, (px, py, c)),
                    "passed": (got, got, s2, r2, (x, y, 1 - c)),
                    "theirs": (other, other, s2, r2, (x, y, 1 - c)),
                }[kind]
                out.append(pltpu.make_async_remote_copy(
                    src_ref=src, dst_ref=dst, send_sem=ss.at[k], recv_sem=rs.at[k], device_id=dev, device_id_type=MESH))
        return out

    def start(ins, outs, sems):
        for cp in copies("first", ins, outs, sems):
            cp.start()

    def mid(ins, outs, sems):
        for got, fwd in zip(copies("landed", ins, outs, sems), copies("passed", ins, outs, sems)):
            got.wait_recv()
            fwd.start()

    def finish(ins, outs, sems):
        for cp in copies("theirs", ins, outs, sems):
            cp.wait_recv()
        for cp in copies("first", ins, outs, sems) + copies("passed", ins, outs, sems):
            cp.wait_send()

    def post(res):
        me = 2 * lax.axis_index("x") + lax.axis_index("y")
        res = [lax.dynamic_update_index_in_dim(r, s, me, 0) for r, s in zip(res, shards)]
        return [r.reshape((N_CHIPS, 2 * r.shape[2]) + r.shape[3:]) for r in res]

    return dict(args=shards, out_shape=[jax.ShapeDtypeStruct((N_CHIPS,) + s.shape, s.dtype) for s in shards],
                sems=[pltpu.SemaphoreType.DMA((3 * n,))] * 4, start=start, mid=mid, finish=finish, post=post)


def _rows_call(name, parts, plan):
    n_in = [len(p["args"]) for p in parts]
    n_out = [len(p["out_shape"]) for p in parts]
    n_scr = [len(p["scratch"]) for p in parts]
    c_in, c_out = len(plan["args"]), len(plan["out_shape"])

    def split(refs, sizes):
        out, pos = [], 0
        for k in sizes:
            out.append(refs[pos:pos + k])
            pos += k
        return out

    def body(*refs):
        ins, outs, scr = split(refs, [sum(n_in) + c_in, sum(n_out) + c_out, sum(n_scr) + len(plan["sems"])])
        p_in, p_out, p_scr = split(ins, n_in + [c_in]), split(outs, n_out + [c_out]), split(scr, n_scr + [len(plan["sems"])])
        comm = (p_in[-1], p_out[-1], p_scr[-1])
        step = pl.program_id(0)

        @pl.when(step == 0)
        def _():
            plan["start"](*comm)

        for p, i, o, s in zip(parts, p_in, p_out, p_scr):
            p["body"](*i, *o, *s)

        @pl.when(step == NB - 2)
        def _():
            plan["mid"](*comm)

        @pl.when(step == NB - 1)
        def _():
            plan["finish"](*comm)

    flat = lambda key: [v for p in parts for v in p[key]]
    res = pl.pallas_call(
        body, grid=(NB,), in_specs=flat("in_specs") + [ANY] * c_in, out_specs=flat("out_specs") + [ANY] * c_out,
        out_shape=flat("out_shape") + plan["out_shape"], scratch_shapes=flat("scratch") + plan["sems"],
        name=name, compiler_params=_cp(("arbitrary",)))(*flat("args"), *plan["args"])
    outs = split(res, n_out + [c_out])
    return outs[:-1], outs[-1]


def _pair_exchange(grads, name):
    n = len(grads)
    hs = [g.shape[1] // 2 for g in grads]
    grads = [g.reshape((N_CHIPS, 2, h) + g.shape[2:]) for g, h in zip(grads, hs)]

    def body(*refs):
        ins, got = refs[:n], refs[n:2 * n]
        ssem, rsem = refs[2 * n:]
        x, y, c = lax.axis_index("x"), lax.axis_index("y"), lax.axis_index("c")
        sends = [pltpu.make_async_remote_copy(
            src_ref=ins[a].at[:, 1 - c], dst_ref=got[a], send_sem=ssem.at[a],
            recv_sem=rsem.at[a], device_id=(x, y, 1 - c), device_id_type=MESH) for a in range(n)]
        for cp in sends:
            cp.start()
        for cp in sends:
            cp.wait()

    half_shapes = [jax.ShapeDtypeStruct((N_CHIPS, h) + g.shape[3:], g.dtype) for g, h in zip(grads, hs)]
    got = pl.pallas_call(
        body, in_specs=[ANY] * n, out_specs=[ANY] * n, out_shape=half_shapes,
        scratch_shapes=[pltpu.SemaphoreType.DMA((n,))] * 2, name=name)(*grads)
    c = lax.axis_index("c")
    own = [lax.dynamic_index_in_dim(g, c, 1, keepdims=False) for g in grads]
    return own, got


def _sum_pair(own, got, send_dtype, name):
    _, rows, cols = own.shape
    tr = 256 if rows % 256 == 0 else rows

    def body(a_ref, b_ref, f_ref, s_ref):
        t = a_ref[...].astype(F32) + b_ref[...].astype(F32)
        f_ref[...] = t
        s_ref[...] = t.astype(send_dtype)

    blk = pl.BlockSpec((N_CHIPS, tr, cols), lambda i: (0, i, 0))
    return pl.pallas_call(
        body, grid=(rows // tr,), in_specs=[blk, blk], out_specs=[blk, blk],
        out_shape=[jax.ShapeDtypeStruct(own.shape, F32), jax.ShapeDtypeStruct(own.shape, send_dtype)],
        name=name, compiler_params=_cp(("parallel",)))(own, got)


def _scatter_chips(keep, send):
    n = len(send)
    plan = _scatter_plan(send)

    def body(*refs):
        sin, land, sems = refs[:n], refs[n:2 * n], refs[2 * n:]
        plan["start"](sin, land, sems)
        plan["finish"](sin, land, sems)

    land = pl.pallas_call(
        body, in_specs=[ANY] * n, out_specs=[ANY] * n, out_shape=plan["out_shape"],
        scratch_shapes=plan["sems"], name="scatter_chips")(*send)
    return _own_slab(keep), land


def _own_slab(keep):
    me = 2 * lax.axis_index("x") + lax.axis_index("y")
    return [lax.dynamic_index_in_dim(k, me, 0, keepdims=False) for k in keep]


def _scatter_plan(send):
    n = len(send)

    def copies(sin, land, sems):
        ssem, rsem = sems
        c = lax.axis_index("c")
        return [pltpu.make_async_remote_copy(
            src_ref=sin[a].at[2 * px + py], dst_ref=land[a].at[j], send_sem=ssem.at[j * n + a],
            recv_sem=rsem.at[j * n + a], device_id=(px, py, c), device_id_type=MESH)
            for j, (px, py) in enumerate(_chip_peers()) for a in range(n)]

    def start(sin, land, sems):
        for cp in copies(sin, land, sems):
            cp.start()

    def finish(sin, land, sems):
        for cp in copies(sin, land, sems):
            cp.wait()

    return dict(args=list(send), out_shape=[jax.ShapeDtypeStruct((3,) + s.shape[1:], s.dtype) for s in send],
                sems=[pltpu.SemaphoreType.DMA((3 * n,))] * 2, start=start, mid=lambda *a: None, finish=finish)


def _sum_shard(mine, land, name):
    rows, cols = mine.shape
    tr = 256 if rows % 256 == 0 else rows

    def body(m_ref, l_ref, o_ref):
        o_ref[...] = ((m_ref[...] + l_ref[0].astype(F32)) + l_ref[1].astype(F32)) + l_ref[2].astype(F32)

    return pl.pallas_call(
        body, grid=(rows // tr,),
        in_specs=[pl.BlockSpec((tr, cols), lambda i: (i, 0)), pl.BlockSpec((3, tr, cols), lambda i: (0, i, 0))],
        out_specs=pl.BlockSpec((tr, cols), lambda i: (i, 0)), out_shape=jax.ShapeDtypeStruct((rows, cols), F32),
        name=name, compiler_params=_cp(("parallel",)))(mine, land)


def _join_cores(halves):
    n = len(halves)

    def body(*refs):
        ins, outs = refs[:n], refs[n:2 * n]
        ssem, rsem = refs[2 * n:]
        x, y, c = lax.axis_index("x"), lax.axis_index("y"), lax.axis_index("c")
        sends = [pltpu.make_async_remote_copy(
            src_ref=ins[a], dst_ref=outs[a].at[c], send_sem=ssem.at[a], recv_sem=rsem.at[a],
            device_id=(x, y, 1 - c), device_id_type=MESH) for a in range(n)]
        for cp in sends:
            cp.start()
        for a in range(n):
            sends[a].wait_send()
            pltpu.make_async_remote_copy(
                src_ref=ins[a], dst_ref=outs[a].at[1 - c], send_sem=ssem.at[a], recv_sem=rsem.at[a],
                device_id=(x, y, 1 - c), device_id_type=MESH).wait_recv()

    res = pl.pallas_call(
        body, in_specs=[ANY] * n, out_specs=[ANY] * n,
        out_shape=[jax.ShapeDtypeStruct((2,) + h.shape, h.dtype) for h in halves],
        scratch_shapes=[pltpu.SemaphoreType.DMA((n,))] * 2, name="join_cores")(*halves)
    c = lax.axis_index("c")
    res = [lax.dynamic_update_index_in_dim(r, h, c, 0) for r, h in zip(res, halves)]
    return [r.reshape((2 * r.shape[1],) + r.shape[2:]) for r in res]


def _gather_all(vec):
    def body(v_ref, o_ref, lsem, ssem, rsem):
        x, y, c = lax.axis_index("x"), lax.axis_index("y"), lax.axis_index("c")
        me = 4 * x + 2 * y + c
        local = pltpu.make_async_copy(v_ref, o_ref.at[me], lsem)
        local.start()
        cps = []
        for k in range(1, 8):
            px, py, pc = x ^ (k >> 2), y ^ ((k >> 1) & 1), c ^ (k & 1)
            cps.append(pltpu.make_async_remote_copy(
                src_ref=v_ref, dst_ref=o_ref.at[me], send_sem=ssem.at[k - 1], recv_sem=rsem.at[k - 1],
                device_id=(px, py, pc), device_id_type=MESH))
        for cp in cps:
            cp.start()
        for k in range(1, 8):
            px, py, pc = x ^ (k >> 2), y ^ ((k >> 1) & 1), c ^ (k & 1)
            pltpu.make_async_remote_copy(
                src_ref=v_ref, dst_ref=o_ref.at[4 * px + 2 * py + pc], send_sem=ssem.at[k - 1],
                recv_sem=rsem.at[k - 1], device_id=(px, py, pc), device_id_type=MESH).wait_recv()
        for cp in cps:
            cp.wait_send()
        local.wait()

    return pl.pallas_call(
        body, in_specs=[ANY], out_specs=ANY, out_shape=jax.ShapeDtypeStruct((8,) + vec.shape, vec.dtype),
        scratch_shapes=[pltpu.SemaphoreType.DMA, pltpu.SemaphoreType.DMA((7,)), pltpu.SemaphoreType.DMA((7,))],
        name="gather_all")(vec)


def _rope_tables():
    pos = (jnp.arange(R, dtype=jnp.int32) - PAD).astype(F32)
    half = HD // 2
    inv = ROPE_THETA ** (-jnp.arange(half, dtype=F32) / half)
    ang = pos[:, None] * inv[None, :]
    cos, sin = jnp.cos(ang), jnp.sin(ang)
    cs = jnp.tile(cos, (1, 4))
    sn = jnp.tile(jnp.concatenate([-sin, sin], axis=1), (1, 2))
    return cs, sn


def _perm_rows(w):
    return jnp.concatenate([w[0:512], w[768:1280], w[1280:2304], w[2304:2816], w[512:640], w[640:768]], axis=0)


def _unperm_rows(w):
    return jnp.concatenate([w[C_Q:C_Q + 512], w[C_K:C_K + 128], w[C_V:C_V + 128], w[C_GA:C_GA + 512],
                            w[C_GLU:C_GLU + 1024], w[C_GB:C_GB + 512]], axis=0)


def _local_step(x, target, p):
    w0t = _perm_rows(p["ab_w_in"])
    conv_w = jnp.concatenate([p["ab_conv_w"], jnp.zeros((1, CC), F32)], axis=0)
    cs, sn = _rope_tables()

    h0 = jnp.concatenate([jnp.zeros((PAD, D), F32), p["meta_tokens"], x], axis=0)

    xn0 = _rms_fwd(h0, p["ab_pre_norm"], "rms_fwd0")
    plan = _gather_plan([p["sb_w_out"], p["ab_w_out"], p["ab_w_pw2"]])
    z0, gathered = _mm([(xn0, w0t)], F32, "in_proj0", 544, 1408, tb=True, plan=plan)
    wo1, wo0, wpw = plan["post"](gathered)
    wo1, wo0, wpw = wo1.reshape(D, D), wo0.reshape(D, D), wpw.reshape(CC, CC)
    plan = _gather_plan([p["sb_w_in"]])
    ((o0, a0, lse0), (cv0, s0)), gathered = _rows_call(
        "fwd0", [_swa_fwd(z0, cs, sn, p["ab_sinks"]),
                 _conv_fwd(z0, conv_w, p["ab_conv_b"], p["ab_conv_ln_g"], p["ab_conv_ln_b"])], plan)
    (w1,) = plan["post"](gathered)
    t0 = _mm([(s0, wpw)], F32, "pw2", 544, 512)
    c0 = _gate_fwd(t0, z0, C_GB, "gate_b_fwd")
    wo0h = wo0.reshape(2, CC, D)
    y0 = _mm([(a0, (wo0h, 0)), (c0, (wo0h, 1))], F32, "out_proj0", 544, 1024)

    h1, xn1 = _post_rms_fwd(h0, y0, p["ab_post_norm"], p["sb_pre_norm"], "post_rms_fwd")
    q1 =_mm([(xn1, (w1, 0))], BF16, "in_proj1_q", 544, 1024)
    k1 = _mm([(xn1, (w1, 1))], BF16, "in_proj1_k", 544, 1024)
    v1 = _mm([(xn1, (w1, 2))], BF16, "in_proj1_v", 544, 1024)
    g1 = _mm([(xn1, (w1, 3))], F32, "in_proj1_g", 544, 1024)
    o1, m1, car1, trips1 = _sb_fwd(q1, k1, v1, g1)
    y1 = _mm([(m1, wo1)], F32, "out_proj1", 544, 1024)

    dh2, dy1, d_sb_post, loss_row = _tail(h1, y1, p["sb_post_norm"], target)

    dm1 = _mm([(dy1, wo1)], F32, "out_proj1_dx", 544, 1024, tb=True)
    d_wo1 = _mm([(m1, dy1)], BF16, "out_proj1_dw", 512, 1024, ta=True)
    dq1, dk1, dv1, dg1 = _sb_bwd(trips1, q1, k1, v1, car1, dm1, g1, o1)
    dz1 = [dq1, dk1, dv1, dg1]
    dxn1 = _mm([(dz1[j], (w1, j)) for j in range(4)], F32, "in_proj1_dx", 544, 1024, tb=True)
    d_w1 = jnp.stack([_mm([(xn1, dz1[j])], BF16, "in_proj1_dw%d" % j, 512, 1024, ta=True) for j in range(4)])

    dh1, d_sb_pre, dy0, d_ab_post = _rms_post_bwd(dxn1, h1, p["sb_pre_norm"], dh2, y0, p["ab_post_norm"],
                                                  "rms_post_bwd")
    dmix0 = _mm([(dy0, wo0)], F32, "out_proj0_dx", 544, 1024, tb=True)
    d_wo0 = jnp.concatenate([_mm([(a0, dy0)], BF16, "out_proj0_dw_a", 512, 1024, ta=True),
                             _mm([(c0, dy0)], BF16, "out_proj0_dw_b", 512, 1024, ta=True)], axis=0)
    dt0, dgb0 = _gate_bwd(dmix0, 512, t0, z0, C_GB, "gate_b_bwd")
    ds0 = _mm([(dt0, wpw)], F32, "pw2_dx", 544, 512, tb=True)
    d_wpw = _mm([(s0, dt0)], BF16, "pw2_dw", 512, 512, ta=True)
    early = ("sb_w_in", "sb_w_out", "ab_w_out", "ab_w_pw2")
    own1, got1 = _pair_exchange([d_w1, d_wo1.reshape(N_CHIPS, 256, D), d_wo0.reshape(N_CHIPS, 256, D),
                                 d_wpw.reshape(N_CHIPS, 128, CC)], "pair_exchange1")
    pair1 = [_sum_pair(o, t, BF16, "sum_pair_" + nm) for o, t, nm in zip(own1, got1, early)]
    plan = _scatter_plan([pr[1] for pr in pair1])
    ((dglu0, d_convw, d_small), (dq0, dga0, dkv0, d_sinks)), land1 = _rows_call(
        "bwd0", [_conv_bwd(ds0, cv0, z0, conv_w, p["ab_conv_ln_g"], p["ab_conv_ln_b"]),
                 _swa_bwd(z0, cs, sn, p["ab_sinks"], o0, dmix0, lse0)], plan)
    halves1 = [_sum_shard(mi, la, "sum_shard_" + nm)
               for mi, la, nm in zip(_own_slab([pr[0] for pr in pair1]), land1, early)]
    dz0 = jnp.concatenate([dq0, dga0, dglu0, dgb0, dkv0], axis=1)
    d_w0t = _unperm_rows(_mm([(dz0, xn0)], BF16, "in_proj0_dw", 1408, 512, ta=True))
    own0, got0 = _pair_exchange([d_w0t.reshape(N_CHIPS, 704, D)], "pair_exchange0")
    keep0, send0 = _sum_pair(own0[0], got0[0], BF16, "sum_pair_ab_w_in")
    plan = _scatter_plan([send0])
    dxn0, land0 = _mm([(dz0, w0t)], F32, "in_proj0_dx", 544, 1024, plan=plan)
    half0 = _sum_shard(_own_slab([keep0])[0], land0[0], "sum_shard_ab_w_in")
    dh0_first, grad_x, d_ab_pre = _rms_bwd(dxn0, h0, p["ab_pre_norm"], dh1, F32, "rms_bwd0", split=True)

    grads = {
        "meta_tokens": dh0_first[PAD:TB], "ab_pre_norm": d_ab_pre, "ab_sinks": d_sinks[0:1, 0:8],
        "ab_conv_w": d_convw[0:CONV_W], "ab_conv_b": d_small[0:1], "ab_conv_ln_g": d_small[1:2],
        "ab_conv_ln_b": d_small[2:3], "ab_post_norm": d_ab_post, "sb_pre_norm": d_sb_pre, "sb_post_norm": d_sb_post,
    }
    h_sb_in, h_sb_out, h_ab_out, h_pw2 = halves1
    return loss_row, grad_x, grads, [half0, h_ab_out, h_pw2, h_sb_in, h_sb_out]


SMALL_ROWS = 80
REP_ROWS = 32

WEIGHTS = ["meta_tokens", "ab_pre_norm", "ab_w_in", "ab_sinks", "ab_conv_w", "ab_conv_b", "ab_conv_ln_g",
           "ab_conv_ln_b", "ab_w_pw2", "ab_w_out", "ab_post_norm", "sb_pre_norm", "sb_w_in", "sb_w_out",
           "sb_post_norm"]
BIG = ["ab_w_in", "ab_w_out", "ab_w_pw2", "sb_w_in", "sb_w_out"]


def _pack_small(conv_w, meta, sb_pre, sb_post):
    pad = lambda a, rows: jnp.pad(a, ((0, rows - a.shape[0]), (0, 0)))
    return jnp.concatenate([pad(conv_w, 32), meta.reshape(32, 128), pad(sb_pre.reshape(2, 128), 8),
                            pad(sb_post.reshape(2, 128), 8)], axis=0)


def _unpack_small(s):
    return s[0:31], s[32:64].reshape(16, 256), s[64:66].reshape(1, 256), s[72:74].reshape(1, 256)


REP_LOSS = 3592


def _pack_rep(pre, post, conv_b, ln_g, ln_b, sinks, extra=None):
    flat = jnp.concatenate([pre.reshape(-1), post.reshape(-1), conv_b.reshape(-1), ln_g.reshape(-1),
                            ln_b.reshape(-1), sinks.reshape(-1)] + ([] if extra is None else [extra.reshape(-1)]))
    flat = jnp.concatenate([flat, jnp.zeros((REP_ROWS * 128 - flat.shape[0],), F32)])
    return flat.reshape(REP_ROWS, 128)


def _unpack_rep(r):
    f = r.reshape(-1)
    return (f[0:1024].reshape(1, 1024), f[1024:2048].reshape(1, 1024), f[2048:2560].reshape(1, 512),
            f[2560:3072].reshape(1, 512), f[3072:3584].reshape(1, 512), f[3584:3592].reshape(1, 8))


def _cols_to_chips(w, width):
    return w.reshape(w.shape[0], N_CHIPS, width).transpose(1, 0, 2)


def _chips_to_cols(w):
    return w.transpose(1, 0, 2).reshape(w.shape[1], -1)


def kernel(x, meta_tokens, ab_pre_norm, ab_w_in, ab_sinks, ab_conv_w, ab_conv_b, ab_conv_ln_g, ab_conv_ln_b, ab_w_pw2, ab_w_out, ab_post_norm, sb_pre_norm, sb_w_in, sb_w_out, sb_post_norm, loss_target, m_meta_tokens, m_ab_pre_norm, m_ab_w_in, m_ab_sinks, m_ab_conv_w, m_ab_conv_b, m_ab_conv_ln_g, m_ab_conv_ln_b, m_ab_w_pw2, m_ab_w_out, m_ab_post_norm, m_sb_pre_norm, m_sb_w_in, m_sb_w_out, m_sb_post_norm, v_meta_tokens, v_ab_pre_norm, v_ab_w_in, v_ab_sinks, v_ab_conv_w, v_ab_conv_b, v_ab_conv_ln_g, v_ab_conv_ln_b, v_ab_w_pw2, v_ab_w_out, v_ab_post_norm, v_sb_pre_norm, v_sb_w_in, v_sb_w_out, v_sb_post_norm):
    w = dict(meta_tokens=meta_tokens, ab_pre_norm=ab_pre_norm, ab_w_in=ab_w_in, ab_sinks=ab_sinks,
             ab_conv_w=ab_conv_w, ab_conv_b=ab_conv_b, ab_conv_ln_g=ab_conv_ln_g, ab_conv_ln_b=ab_conv_ln_b,
             ab_w_pw2=ab_w_pw2, ab_w_out=ab_w_out, ab_post_norm=ab_post_norm, sb_pre_norm=sb_pre_norm,
             sb_w_in=sb_w_in, sb_w_out=sb_w_out, sb_post_norm=sb_post_norm)
    m = dict(meta_tokens=m_meta_tokens, ab_pre_norm=m_ab_pre_norm, ab_w_in=m_ab_w_in, ab_sinks=m_ab_sinks,
             ab_conv_w=m_ab_conv_w, ab_conv_b=m_ab_conv_b, ab_conv_ln_g=m_ab_conv_ln_g,
             ab_conv_ln_b=m_ab_conv_ln_b, ab_w_pw2=m_ab_w_pw2, ab_w_out=m_ab_w_out, ab_post_norm=m_ab_post_norm,
             sb_pre_norm=m_sb_pre_norm, sb_w_in=m_sb_w_in, sb_w_out=m_sb_w_out, sb_post_norm=m_sb_post_norm)
    v = dict(meta_tokens=v_meta_tokens, ab_pre_norm=v_ab_pre_norm, ab_w_in=v_ab_w_in, ab_sinks=v_ab_sinks,
             ab_conv_w=v_ab_conv_w, ab_conv_b=v_ab_conv_b, ab_conv_ln_g=v_ab_conv_ln_g,
             ab_conv_ln_b=v_ab_conv_ln_b, ab_w_pw2=v_ab_w_pw2, ab_w_out=v_ab_w_out, ab_post_norm=v_ab_post_norm,
             sb_pre_norm=v_sb_pre_norm, sb_w_in=v_sb_w_in, sb_w_out=v_sb_w_out, sb_post_norm=v_sb_post_norm)

    def small_of(d):
        return _pack_small(d["ab_conv_w"][0], d["meta_tokens"], d["sb_pre_norm"], d["sb_post_norm"])

    def rep_of(d):
        return _pack_rep(d["ab_pre_norm"], d["ab_post_norm"], d["ab_conv_b"], d["ab_conv_ln_g"], d["ab_conv_ln_b"],
                         d["ab_sinks"])

    g_in0, g_small = _gather_chips([ab_w_in[0].T.astype(BF16), small_of(w)])
    conv_w_f = _chips_to_cols(g_small[:, 0:31])
    meta_f = _chips_to_cols(g_small[:, 32:64].reshape(N_CHIPS, 16, 256))
    sb_pre_f = g_small[:, 64:66].reshape(1, D)
    sb_post_f = g_small[:, 72:74].reshape(1, D)
    full = {
        "meta_tokens": meta_f, "ab_pre_norm": ab_pre_norm, "ab_w_in": g_in0.reshape(AB_IN, D),
        "ab_sinks": ab_sinks, "ab_conv_w": conv_w_f, "ab_conv_b": ab_conv_b, "ab_conv_ln_g": ab_conv_ln_g,
        "ab_conv_ln_b": ab_conv_ln_b, "ab_w_pw2": ab_w_pw2[0].astype(BF16), "ab_w_out": ab_w_out[0].astype(BF16),
        "ab_post_norm": ab_post_norm, "sb_pre_norm": sb_pre_f, "sb_w_in": sb_w_in[0].astype(BF16),
        "sb_w_out": sb_w_out[0].astype(BF16), "sb_post_norm": sb_post_f,
    }

    loss_row, grad_x, g, halves = _local_step(x[0], loss_target[0], full)

    total = _join_cores(halves)

    rep_g = _pack_rep(g["ab_pre_norm"], g["ab_post_norm"], g["ab_conv_b"], g["ab_conv_ln_g"], g["ab_conv_ln_b"],
                      g["ab_sinks"], loss_row[0:1, 0:1])
    vec = jnp.concatenate([rep_g, jnp.pad(g["ab_conv_w"].reshape(124, 128), ((0, 4), (0, 0))),
                           g["meta_tokens"].reshape(128, 128), g["sb_pre_norm"].reshape(8, 128),
                           g["sb_post_norm"].reshape(8, 128)], axis=0)
    vec_sum = _sum8(_gather_all(vec), "sum8_small")
    rep_sum = vec_sum[0:REP_ROWS]
    loss = rep_sum.reshape(-1)[REP_LOSS]
    me = 2 * lax.axis_index("x") + lax.axis_index("y")
    small_sum = _pack_small(
        lax.dynamic_slice_in_dim(vec_sum[32:156].reshape(CONV_W, CC), me * 128, 128, axis=1),
        lax.dynamic_slice_in_dim(vec_sum[160:288].reshape(N_META, D), me * 256, 256, axis=1),
        lax.dynamic_slice_in_dim(vec_sum[288:296].reshape(1, D), me * 256, 256, axis=1),
        lax.dynamic_slice_in_dim(vec_sum[296:304].reshape(1, D), me * 256, 256, axis=1))

    out_g, out_d, out_m, out_v = {}, {}, {}, {}
    for i, k in enumerate(BIG):
        shp = w[k].shape
        if k == "ab_w_in":
            res = _adamw(w[k][0].T, [total[i]], m[k][0].T, v[k][0].T, "adamw_" + k)
            out_g[k], out_d[k], out_m[k], out_v[k] = [r.T.reshape(shp) for r in res]
            continue
        res = _adamw(w[k][0], [total[i]], m[k][0], v[k][0], "adamw_" + k)
        out_g[k], out_d[k], out_m[k], out_v[k] = [r.reshape(shp) for r in res]
    res = _adamw(small_of(w), [small_sum], small_of(m), small_of(v), "adamw_small")
    for dst, r in zip((out_g, out_d, out_m, out_v), res):
        cw, mt, pre, post = _unpack_small(r)
        dst["ab_conv_w"], dst["meta_tokens"], dst["sb_pre_norm"], dst["sb_post_norm"] = cw[None], mt, pre, post
    res = _adamw(rep_of(w), [rep_sum], rep_of(m), rep_of(v), "adamw_rep")
    for dst, r in zip((out_g, out_d, out_m, out_v), res):
        (dst["ab_pre_norm"], dst["ab_post_norm"], dst["ab_conv_b"], dst["ab_conv_ln_g"], dst["ab_conv_ln_b"],
         dst["ab_sinks"]) = _unpack_rep(r)

    return (loss, grad_x[None], *[out_g[k] for k in WEIGHTS], *[out_d[k] for k in WEIGHTS],
            *[out_m[k] for k in WEIGHTS], *[out_v[k] for k in WEIGHTS])
```

```python
import functools

import jax
import jax.numpy as jnp
from jax import lax
from jax.experimental import pallas as pl
from jax.experimental.pallas import tpu as pltpu

F32 = jnp.float32
BF16 = jnp.bfloat16

D = 1024
SEQ = 2048
N_META = 16
TB = 128
TR = 272
PAD = TB - N_META
R = SEQ + TB
NB = R // TB
HD = 64
ROPE_THETA = 10000.0
NORM_EPS = 1e-6
LN_EPS = 1e-5
NEG = -1e30
CONV_W = 31
SCALE = HD ** -0.5
N_CHIPS = 4

C_Q, C_K, C_V, C_GA, C_GLU, C_GB = 0, 512, 640, 768, 1280, 2304
AB_IN = 2816

ADAM_LR, ADAM_B1, ADAM_B2, ADAM_EPS, ADAM_WD, ADAM_STEP = 0.001, 0.9, 0.999, 1e-08, 0.01, 10

VMEM_LIMIT = 56 * 1024 * 1024


def _cp(sem):
    return pltpu.CompilerParams(dimension_semantics=sem, vmem_limit_bytes=VMEM_LIMIT)


def _sig(x):
    return 1.0 / (1.0 + jnp.exp(-x))


def _dot(a, b):
    return lax.dot_general(a, b, (((1,), (0,)), ((), ())), preferred_element_type=F32)


def _dot_nt(a, b):
    return lax.dot_general(a, b, (((1,), (1,)), ((), ())), preferred_element_type=F32)


def _dot_tn(a, b):
    return lax.dot_general(a, b, (((0,), (0,)), ((), ())), preferred_element_type=F32)


def _mm(pairs, out_dtype, name, tm, tn, ta=False, tb=False, plan=None):
    pairs = [(a, b if isinstance(b, tuple) else (b, None)) for a, b in pairs]
    a0, (b0, _) = pairs[0]
    m = a0.shape[1] if ta else a0.shape[0]
    n = b0.shape[-2] if tb else b0.shape[-1]
    npairs = len(pairs)
    dims = (((0 if ta else 1,), (1 if tb else 0,)), ((), ()))
    c_in = len(plan["args"]) if plan else 0
    c_out = len(plan["out_shape"]) if plan else 0
    steps = (m // tm) * (n // tn)

    def body(*refs):
        o_ref = refs[2 * npairs + c_in]
        if plan:
            comm = (refs[2 * npairs:2 * npairs + c_in], refs[2 * npairs + c_in + 1:2 * npairs + c_in + 1 + c_out],
                    refs[2 * npairs + c_in + 1 + c_out:])
            step = pl.program_id(0) * (n // tn) + pl.program_id(1)

            @pl.when(step == 0)
            def _():
                plan["start"](*comm)

        acc = None
        for i in range(npairs):
            t = lax.dot_general(refs[2 * i][...].astype(BF16), refs[2 * i + 1][...].astype(BF16), dims,
                                preferred_element_type=F32)
            acc = t if acc is None else acc + t
        o_ref[...] = acc.astype(out_dtype)
        if plan:
            @pl.when(step == steps - 2)
            def _():
                plan["mid"](*comm)

            @pl.when(step == steps - 1)
            def _():
                plan["finish"](*comm)

    in_specs, args = [], []
    for a, (b, sel) in pairs:
        k = a.shape[0] if ta else a.shape[1]
        in_specs.append(pl.BlockSpec((k, tm), lambda i, j: (0, i)) if ta else pl.BlockSpec((tm, k), lambda i, j: (i, 0)))
        bshape, bidx = ((tn, k), lambda i, j: (j, 0)) if tb else ((k, tn), lambda i, j: (0, j))
        if sel is None:
            in_specs.append(pl.BlockSpec(bshape, bidx))
        else:
            in_specs.append(pl.BlockSpec((None,) + bshape, functools.partial(lambda i, j, f, s: (s,) + f(i, j), f=bidx, s=sel)))
        args += [a, b]
    out_spec = pl.BlockSpec((tm, tn), lambda i, j: (i, j))
    out_shape = jax.ShapeDtypeStruct((m, n), out_dtype)
    if not plan:
        return pl.pallas_call(
            body, grid=(m // tm, n // tn), in_specs=in_specs, out_specs=out_spec, out_shape=out_shape, name=name,
            compiler_params=_cp(("parallel", "parallel")))(*args)
    assert steps >= 2
    res = pl.pallas_call(
        body, grid=(m // tm, n // tn), in_specs=in_specs + [ANY] * c_in, out_specs=[out_spec] + [ANY] * c_out,
        out_shape=[out_shape] + plan["out_shape"], scratch_shapes=plan["sems"], name=name,
        compiler_params=_cp(("arbitrary", "arbitrary")))(*args, *plan["args"])
    return res[0], res[1:]


def _rms_fwd(h, g, name):
    def body(h_ref, g_ref, o_ref):
        x = h_ref[...]
        r = lax.rsqrt(jnp.mean(x * x, axis=1, keepdims=True) + NORM_EPS)
        o_ref[...] = (x * r * g_ref[...]).astype(BF16)

    return pl.pallas_call(
        body, grid=(R // TR,),
        in_specs=[pl.BlockSpec((TR, D), lambda n: (n, 0)), pl.BlockSpec((1, D), lambda n: (0, 0))],
        out_specs=pl.BlockSpec((TR, D), lambda n: (n, 0)),
        out_shape=jax.ShapeDtypeStruct((R, D), BF16), name=name, compiler_params=_cp(("parallel",)))(h, g)


def _rms_bwd(dout, x, g, res, out_dtype, name, split=False):
    has_res = res is not None

    def body(*refs):
        if split:
            refs = list(refs)
            dx_rest_ref = refs.pop(-2)
        if has_res:
            d_ref, x_ref, g_ref, r_ref, dx_ref, dg_ref = refs
        else:
            d_ref, x_ref, g_ref, dx_ref, dg_ref = refs
        n = pl.program_id(0)
        xv = x_ref[...]
        dv = d_ref[...]
        r = lax.rsqrt(jnp.mean(xv * xv, axis=1, keepdims=True) + NORM_EPS)
        xh = xv * r
        dxh = dv * g_ref[...]
        dx = r * (dxh - xh * jnp.mean(dxh * xh, axis=1, keepdims=True))
        if has_res:
            dx = dx + r_ref[...]
        row = lax.broadcasted_iota(jnp.int32, (TB, D), 0) + n * TB
        dx = jnp.where(row >= PAD, dx, 0.0).astype(out_dtype)
        if split:
            @pl.when(n == 0)
            def _():
                dx_ref[...] = dx

            @pl.when(n > 0)
            def _():
                dx_rest_ref[...] = dx
        else:
            dx_ref[...] = dx

        @pl.when(n == 0)
        def _():
            dg_ref[...] = jnp.zeros_like(dg_ref)

        dg_ref[...] += jnp.sum(dv * xh, axis=0, keepdims=True)

    blk = pl.BlockSpec((TB, D), lambda n: (n, 0))
    vec = pl.BlockSpec((1, D), lambda n: (0, 0))
    ins = [dout, x, g] + ([res] if has_res else [])
    in_specs = [blk, blk, vec] + ([blk] if has_res else [])
    if split:
        out_specs = [pl.BlockSpec((TB, D), lambda n: (0, 0)), pl.BlockSpec((TB, D), lambda n: (jnp.maximum(n - 1, 0), 0)), vec]
        out_shape = [jax.ShapeDtypeStruct((TB, D), out_dtype), jax.ShapeDtypeStruct((SEQ, D), out_dtype),
                     jax.ShapeDtypeStruct((1, D), F32)]
    else:
        out_specs = [blk, vec]
        out_shape = [jax.ShapeDtypeStruct((R, D), out_dtype), jax.ShapeDtypeStruct((1, D), F32)]
    return pl.pallas_call(
        body, grid=(NB,), in_specs=in_specs, out_specs=out_specs, out_shape=out_shape,
        name=name, compiler_params=_cp(("arbitrary",)))(*ins)


def _post_rms_fwd(h, y, g_post, g_next, name):
    def body(h_ref, y_ref, gp_ref, gn_ref, o_ref, x_ref):
        yv = y_ref[...]
        r = lax.rsqrt(jnp.mean(yv * yv, axis=1, keepdims=True) + NORM_EPS)
        hn = h_ref[...] + yv * r * gp_ref[...]
        o_ref[...] = hn
        r2 = lax.rsqrt(jnp.mean(hn * hn, axis=1, keepdims=True) + NORM_EPS)
        x_ref[...] = (hn * r2 * gn_ref[...]).astype(BF16)

    blk = pl.BlockSpec((TR, D), lambda n: (n, 0))
    vec = pl.BlockSpec((1, D), lambda n: (0, 0))
    return pl.pallas_call(
        body, grid=(R // TR,), in_specs=[blk, blk, vec, vec], out_specs=[blk, blk],
        out_shape=[jax.ShapeDtypeStruct((R, D), F32), jax.ShapeDtypeStruct((R, D), BF16)],
        name=name, compiler_params=_cp(("parallel",)))(h, y, g_post, g_next)


def _rms_post_bwd(dxn, h, g, res, y, g_post, name):
    def body(d_ref, h_ref, g_ref, r_ref, y_ref, gp_ref, dh_ref, dg_ref, dy_ref, dgp_ref):
        n = pl.program_id(0)

        @pl.when(n == 0)
        def _():
            dg_ref[...] = jnp.zeros_like(dg_ref)
            dgp_ref[...] = jnp.zeros_like(dgp_ref)

        hv, dv = h_ref[...], d_ref[...]
        r = lax.rsqrt(jnp.mean(hv * hv, axis=1, keepdims=True) + NORM_EPS)
        xh = hv * r
        dxh = dv * g_ref[...]
        dh = r * (dxh - xh * jnp.mean(dxh * xh, axis=1, keepdims=True)) + r_ref[...]
        row = lax.broadcasted_iota(jnp.int32, (TR, D), 0) + n * TR
        dh = jnp.where(row >= PAD, dh, 0.0)
        dh_ref[...] = dh
        dg_ref[...] += jnp.sum(dv * xh, axis=0, keepdims=True)
        yv = y_ref[...]
        ry = lax.rsqrt(jnp.mean(yv * yv, axis=1, keepdims=True) + NORM_EPS)
        yh = yv * ry
        dyh = dh * gp_ref[...]
        dy_ref[...] = (ry * (dyh - yh * jnp.mean(dyh * yh, axis=1, keepdims=True))).astype(BF16)
        dgp_ref[...] += jnp.sum(dh * yh, axis=0, keepdims=True)

    blk = pl.BlockSpec((TR, D), lambda n: (n, 0))
    vec = pl.BlockSpec((1, D), lambda n: (0, 0))
    return pl.pallas_call(
        body, grid=(R // TR,), in_specs=[blk, blk, vec, blk, blk, vec], out_specs=[blk, vec, blk, vec],
        out_shape=[jax.ShapeDtypeStruct((R, D), F32), jax.ShapeDtypeStruct((1, D), F32),
                   jax.ShapeDtypeStruct((R, D), BF16), jax.ShapeDtypeStruct((1, D), F32)],
        name=name, compiler_params=_cp(("arbitrary",)))(dxn, h, g, res, y, g_post)


GW = 512


def _cols_spec(rows, width, where):
    def index(*g):
        r, c = where(*g)
        return r * rows, (c if isinstance(c, int) else pl.multiple_of(c, 128))
    return pl.BlockSpec((pl.Element(rows), pl.Element(width)), index)


def _gate_fwd(o, gsrc, goff, name):
    w = o.shape[1]

    def body(o_ref, g_ref, m_ref):
        gv = g_ref[...]
        m_ref[...] = (o_ref[...] * (gv * _sig(gv))).astype(BF16)

    return pl.pallas_call(
        body, grid=(R // TR, w // GW),
        in_specs=[pl.BlockSpec((TR, GW), lambda n, j: (n, j)), _cols_spec(TR, GW, lambda n, j: (n, goff + j * GW))],
        out_specs=pl.BlockSpec((TR, GW), lambda n, j: (n, j)),
        out_shape=jax.ShapeDtypeStruct((R, w), BF16), name=name,
        compiler_params=_cp(("parallel", "parallel")))(o, gsrc)


def _gate_bwd(dsrc, doff, o, gsrc, goff, name):
    w = o.shape[1]

    def body(d_ref, o_ref, g_ref, do_ref, dg_ref):
        gv = g_ref[...]
        dv = d_ref[...]
        s = _sig(gv)
        do_ref[...] = dv * (gv * s)
        dg_ref[...] = (dv * o_ref[...] * (s * (1.0 + gv * (1.0 - s)))).astype(BF16)

    blk = pl.BlockSpec((TR, GW), lambda n, j: (n, j))
    return pl.pallas_call(
        body, grid=(R // TR, w // GW),
        in_specs=[_cols_spec(TR, GW, lambda n, j: (n, doff + j * GW)), blk,
                  _cols_spec(TR, GW, lambda n, j: (n, goff + j * GW))],
        out_specs=[blk, blk],
        out_shape=[jax.ShapeDtypeStruct((R, w), F32), jax.ShapeDtypeStruct((R, w), BF16)], name=name,
        compiler_params=_cp(("parallel", "parallel")))(dsrc, o, gsrc)


def _tail(h, y, g, target):
    def body(h_ref, y_ref, g_ref, t_ref, d_ref, dy_ref, dg_ref, l_ref):
        n = pl.program_id(0)

        @pl.when(n == 0)
        def _():
            d_ref[...] = jnp.zeros_like(d_ref)
            dy_ref[...] = jnp.zeros_like(dy_ref)
            dg_ref[...] = jnp.zeros_like(dg_ref)
            l_ref[...] = jnp.zeros_like(l_ref)

        @pl.when(n > 0)
        def _():
            yv = y_ref[...]
            r = lax.rsqrt(jnp.mean(yv * yv, axis=1, keepdims=True) + NORM_EPS)
            yh = yv * r
            err = (h_ref[...] + yh * g_ref[...]) - t_ref[...]
            dv = err * (1.0 / D)
            d_ref[...] = dv
            l_ref[...] += jnp.sum(err * err, axis=0, keepdims=True)
            dyh = dv * g_ref[...]
            dy_ref[...] = (r * (dyh - yh * jnp.mean(dyh * yh, axis=1, keepdims=True))).astype(BF16)
            dg_ref[...] += jnp.sum(dv * yh, axis=0, keepdims=True)

        @pl.when(n == NB - 1)
        def _():
            tot = jnp.sum(l_ref[...], axis=1, keepdims=True) * (0.5 / D)
            l_ref[...] = jnp.broadcast_to(tot, (1, D))

    blk = pl.BlockSpec((TB, D), lambda n: (n, 0))
    vec = pl.BlockSpec((1, D), lambda n: (0, 0))
    return pl.pallas_call(
        body, grid=(NB,),
        in_specs=[blk, blk, vec, pl.BlockSpec((TB, D), lambda n: (jnp.maximum(n - 1, 0), 0))],
        out_specs=[blk, blk, vec, vec],
        out_shape=[jax.ShapeDtypeStruct((R, D), F32), jax.ShapeDtypeStruct((R, D), BF16),
                   jax.ShapeDtypeStruct((1, D), F32), jax.ShapeDtypeStruct((1, D), F32)],
        name="tail", compiler_params=_cp(("arbitrary",)))(h, y, g, target)


def _lane_row(shape):
    return lax.broadcasted_iota(jnp.int32, shape, 1), lax.broadcasted_iota(jnp.int32, shape, 0)


def _rot_half(x, lane):
    return jnp.where(lane % HD < HD // 2, pltpu.roll(x, 128 - HD // 2, 1), pltpu.roll(x, HD // 2, 1))


def _swa_blocks(n):
    return (0, jnp.maximum(n - 1, 0), n)


SWA_STACKS = ((0, 0), (0, 1), (1, 0), (1, 1))


def _swa_masks(n, lane, row):
    qpos = n * TB + (row & (TB - 1))
    kp = (n - 1) * TB + lane
    kc = n * TB + lane
    m0 = (lane >= PAD) & (qpos - lane >= TB)
    mp = (kp >= PAD) & (qpos >= kp) & (qpos - kp < TB)
    mc = (kc >= PAD) & (qpos >= kc)
    return (m0, mp, mc)


def _stack_pair(xa, xb, par):
    lane = lax.broadcasted_iota(jnp.int32, (TB, 128), 1)
    keep = (lane < HD) if par == 0 else (lane >= HD)
    return jnp.concatenate([jnp.where(keep, xa, 0.0), jnp.where(keep, xb, 0.0)], axis=0)


def _per_head(a, b):
    row = lax.broadcasted_iota(jnp.int32, (2 * TB, 1), 0)
    return jnp.where(row < TB, a, b)


def _swa_load(n, zq_ref, zkv_ref, cs_ref, sn_ref, lane):
    r0 = pl.multiple_of(n * TB, TB)
    csq, snq = cs_ref[pl.ds(r0, TB), :], sn_ref[pl.ds(r0, TB), :]
    qc = []
    for c in range(4):
        x = zq_ref[:, c * 128:(c + 1) * 128]
        qc.append((x * csq + _rot_half(x, lane) * snq) * SCALE)
    qst = [_stack_pair(qc[2 * g], qc[2 * g + 1], par).astype(BF16) for g, par in SWA_STACKS]
    kvs = []
    for b in _swa_blocks(n):
        b0 = pl.multiple_of(b * TB, TB)
        csb, snb = cs_ref[pl.ds(b0, TB), :], sn_ref[pl.ds(b0, TB), :]
        kx = zkv_ref[pl.ds(b0, TB), 0:128]
        kr = kx * csb + _rot_half(kx, lane) * snb
        vx = zkv_ref[pl.ds(b0, TB), 128:256]
        kvs.append((kr.astype(BF16), pltpu.roll(kr, HD, 1).astype(BF16),
                    vx.astype(BF16), pltpu.roll(vx, HD, 1).astype(BF16), csb, snb, b0))
    return qst, (csq, snq), kvs


def _swa_fwd(z0, cs, sn, sinks):
    def body(zq_ref, zkv_ref, cs_ref, sn_ref, sk_ref, ga_ref, o_ref, a_ref, lse_ref):
        n = pl.program_id(0)
        lane, row = _lane_row((TB, 128))
        lo = lane < HD
        masks = _swa_masks(n, *_lane_row((2 * TB, 128)))
        qst, _, kvs = _swa_load(n, zq_ref, zkv_ref, cs_ref, sn_ref, lane)
        ss = [[jnp.where(m, _dot_nt(qst[si], k if par == g else ka), NEG)
               for (k, ka, _, _, _, _, _), m in zip(kvs, masks)] for si, (g, par) in enumerate(SWA_STACKS)]
        o2, lse2 = [], []
        for si, (g, par) in enumerate(SWA_STACKS):
            sink = _per_head(sk_ref[0, 4 * g + par], sk_ref[0, 4 * g + 2 + par])
            s = ss[si]
            mx = jnp.maximum(jnp.maximum(jnp.max(s[0], axis=1, keepdims=True), jnp.max(s[1], axis=1, keepdims=True)),
                             jnp.max(s[2], axis=1, keepdims=True))
            mx = jnp.maximum(mx, sink)
            es = [jnp.exp(sb - mx) for sb in s]
            den = (jnp.sum(es[0], axis=1, keepdims=True) + jnp.sum(es[1], axis=1, keepdims=True)
                   + jnp.sum(es[2], axis=1, keepdims=True) + jnp.exp(sink - mx))
            inv = 1.0 / den
            t = jnp.zeros((2 * TB, 128), F32)
            for (_, _, v, va, _, _, _), e in zip(kvs, es):
                t = t + _dot((e * inv).astype(BF16), v if par == g else va)
            o2.append(t)
            lse2.append(mx + jnp.log(den))
        lse_t = jnp.zeros((TB, 128), F32)
        for g in range(2):
            for t in range(2):
                rows = slice(t * TB, (t + 1) * TB)
                c = 2 * g + t
                oc = jnp.where(lo, o2[2 * g][rows], o2[2 * g + 1][rows])
                o_ref[:, c * 128:(c + 1) * 128] = oc
                gv = ga_ref[:, c * 128:(c + 1) * 128]
                a_ref[:, c * 128:(c + 1) * 128] = (oc * (gv * _sig(gv))).astype(BF16)
                for par in range(2):
                    lse_t = jnp.where(lane == 4 * g + 2 * t + par, lse2[2 * g + par][rows], lse_t)
        lse_ref[...] = lse_t

    full = pl.BlockSpec((R, 128), lambda n: (0, 0))
    return dict(
        body=body,
        in_specs=[pl.BlockSpec((TB, 512), lambda n: (n, C_Q // 512)),
                  pl.BlockSpec((R, 256), lambda n: (0, C_K // 256)), full, full,
                  pl.BlockSpec(memory_space=pltpu.SMEM), _cols_spec(TB, 512, lambda n: (n, C_GA))],
        args=[z0, z0, cs, sn, sinks, z0],
        out_specs=[pl.BlockSpec((TB, 512), lambda n: (n, 0)), pl.BlockSpec((TB, 512), lambda n: (n, 0)),
                   pl.BlockSpec((TB, 128), lambda n: (n, 0))],
        out_shape=[jax.ShapeDtypeStruct((R, 512), F32), jax.ShapeDtypeStruct((R, 512), BF16),
                   jax.ShapeDtypeStruct((R, 128), F32)],
        scratch=[])


def _swa_bwd(z0, cs, sn, sinks, o, dmix, lse):
    def body(zq_ref, zkv_ref, cs_ref, sn_ref, sk_ref, ga_ref, o_ref, dm_ref, lse_ref,
             dq_ref, dga_ref, dkv_ref, dsk_ref, do_ref, acc_ref):
        n = pl.program_id(0)

        @pl.when(n == 0)
        def _():
            acc_ref[...] = jnp.zeros_like(acc_ref)
            dsk_ref[...] = jnp.zeros_like(dsk_ref)

        gv, dmv = ga_ref[...], dm_ref[...]
        sg = _sig(gv)
        dga_ref[...] = (dmv * o_ref[...] * (sg * (1.0 + gv * (1.0 - sg)))).astype(BF16)
        do_ref[...] = dmv * (gv * sg)
        lane, row = _lane_row((TB, 128))
        lo = lane < HD
        masks = _swa_masks(n, *_lane_row((2 * TB, 128)))
        qst, (csq, snq), kvs = _swa_load(n, zq_ref, zkv_ref, cs_ref, sn_ref, lane)
        lse_t = lse_ref[...]
        ss = [[jnp.where(m, _dot_nt(qst[si], k if par == g else ka), NEG)
               for (k, ka, _, _, _, _, _), m in zip(kvs, masks)] for si, (g, par) in enumerate(SWA_STACKS)]
        dobs, deltas, lses, dps = [], [], [], []
        for g, par in SWA_STACKS:
            ca, cb = slice(2 * g * 128, (2 * g + 1) * 128), slice((2 * g + 1) * 128, (2 * g + 2) * 128)
            dom = _stack_pair(do_ref[:, ca], do_ref[:, cb], par)
            deltas.append(jnp.sum(dom * jnp.concatenate([o_ref[:, ca], o_ref[:, cb]], axis=0), axis=1, keepdims=True))
            dob = dom.astype(BF16)
            dobs.append(dob)
            lses.append(jnp.concatenate(
                [jnp.sum(jnp.where(lane == 4 * g + 2 * t + par, lse_t, 0.0), axis=1, keepdims=True) for t in range(2)],
                axis=0))
            dps.append([_dot_nt(dob, v if par == g else va) for (_, _, v, va, _, _, _) in kvs])
        dk_al = [jnp.zeros((TB, 128), F32) for _ in range(3)]
        dk_mis = [jnp.zeros((TB, 128), F32) for _ in range(3)]
        dv_al = [jnp.zeros((TB, 128), F32) for _ in range(3)]
        dv_mis = [jnp.zeros((TB, 128), F32) for _ in range(3)]
        dsk_t = jnp.zeros((TB, 128), F32)
        dq2 = []
        for si, (g, par) in enumerate(SWA_STACKS):
            dqt = jnp.zeros((2 * TB, 128), F32)
            for bi, (k, ka, _, _, _, _, _) in enumerate(kvs):
                p = jnp.exp(ss[si][bi] - lses[si])
                ds = (p * (dps[si][bi] - deltas[si])).astype(BF16)
                dqt = dqt + _dot(ds, k if par == g else ka)
                dkh = _dot_tn(ds, qst[si])
                dvh = _dot_tn(p.astype(BF16), dobs[si])
                if par == g:
                    dk_al[bi] = dk_al[bi] + dkh
                    dv_al[bi] = dv_al[bi] + dvh
                else:
                    dk_mis[bi] = dk_mis[bi] + dkh
                    dv_mis[bi] = dv_mis[bi] + dvh
            dq2.append(dqt)
            sink = _per_head(sk_ref[0, 4 * g + par], sk_ref[0, 4 * g + 2 + par])
            dsk = -jnp.exp(sink - lses[si]) * deltas[si]
            for t in range(2):
                dsk_t = jnp.where(lane == 4 * g + 2 * t + par, dsk[t * TB:(t + 1) * TB], dsk_t)
        for g in range(2):
            for t in range(2):
                rows = slice(t * TB, (t + 1) * TB)
                c = 2 * g + t
                dqc = jnp.where(lo, dq2[2 * g][rows], dq2[2 * g + 1][rows]) * SCALE
                dq_ref[:, c * 128:(c + 1) * 128] = (dqc * csq + _rot_half(dqc * snq, lane)).astype(BF16)
        for bi, (_, _, _, _, csb, snb, b0) in enumerate(kvs):
            dk = dk_al[bi] + pltpu.roll(dk_mis[bi], HD, 1)
            dv = dv_al[bi] + pltpu.roll(dv_mis[bi], HD, 1)
            acc_ref[pl.ds(b0, TB), 0:128] += dk * csb + _rot_half(dk * snb, lane)
            acc_ref[pl.ds(b0, TB), 128:256] += dv
        dsk_ref[0:1, :] += jnp.sum(dsk_t, axis=0, keepdims=True)

        @pl.when(n == NB - 1)
        def _():
            dkv_ref[...] = acc_ref[...].astype(BF16)

    full = pl.BlockSpec((R, 128), lambda n: (0, 0))
    b512 = pl.BlockSpec((TB, 512), lambda n: (n, 0))
    return dict(
        body=body,
        in_specs=[pl.BlockSpec((TB, 512), lambda n: (n, C_Q // 512)),
                  pl.BlockSpec((R, 256), lambda n: (0, C_K // 256)), full, full,
                  pl.BlockSpec(memory_space=pltpu.SMEM), _cols_spec(TB, 512, lambda n: (n, C_GA)),
                  b512, b512, pl.BlockSpec((TB, 128), lambda n: (n, 0))],
        args=[z0, z0, cs, sn, sinks, z0, o, dmix, lse],
        out_specs=[b512, b512, pl.BlockSpec((R, 256), lambda n: (0, 0)), pl.BlockSpec((8, 128), lambda n: (0, 0))],
        out_shape=[jax.ShapeDtypeStruct((R, 512), BF16), jax.ShapeDtypeStruct((R, 512), BF16),
                   jax.ShapeDtypeStruct((R, 256), BF16), jax.ShapeDtypeStruct((8, 128), F32)],
        scratch=[pltpu.VMEM((TB, 512), F32), pltpu.VMEM((R, 256), F32)])


CC = 512
HALO = CONV_W - 1


def _conv_fwd(z0, conv_w, conv_b, ln_g, ln_b):
    def body(g_ref, w_ref, cb_ref, lg_ref, lb_ref, cv_ref, s_ref, ubuf):
        n = pl.program_id(0)

        @pl.when(n == 0)
        def _():
            ubuf[...] = jnp.zeros_like(ubuf)

        u = g_ref[:, 0:CC] * _sig(g_ref[:, CC:2 * CC])
        for k in range(8):
            ubuf[k, 0:TB + 8, :] = ubuf[k, TB:2 * TB + 8, :]
            ubuf[k, pl.ds(TB + 8 - k, TB), :] = u
        acc = jnp.zeros((TB, CC), F32)
        for w in range(CONV_W):
            off = TB - HALO + w
            acc = acc + ubuf[off % 8, pl.ds(off + 8 - off % 8, TB), :] * w_ref[w:w + 1, :]
        cv = acc + cb_ref[...]
        cv_ref[...] = cv
        xc = cv - jnp.mean(cv, axis=1, keepdims=True)
        rs = lax.rsqrt(jnp.mean(xc * xc, axis=1, keepdims=True) + LN_EPS)
        ln = xc * rs * lg_ref[...] + lb_ref[...]
        s_ref[...] = (ln * _sig(ln)).astype(BF16)

    vec = pl.BlockSpec((1, CC), lambda n: (0, 0))
    blk = pl.BlockSpec((TB, CC), lambda n: (n, 0))
    return dict(
        body=body,
        in_specs=[_cols_spec(TB, 2 * CC, lambda n: (n, C_GLU)),
                  pl.BlockSpec((32, CC), lambda n: (0, 0)), vec, vec, vec],
        args=[z0, conv_w, conv_b, ln_g, ln_b],
        out_specs=[blk, blk],
        out_shape=[jax.ShapeDtypeStruct((R, CC), F32), jax.ShapeDtypeStruct((R, CC), BF16)],
        scratch=[pltpu.VMEM((8, 2 * TB + 8, CC), F32)])


def _conv_bwd(ds, cv, z0, conv_w, ln_g, ln_b):
    def body(ds_ref, cv_ref, g_ref, w_ref, lg_ref, lb_ref, dglu_ref, dw_ref, dsm_ref, dbuf):
        n = pl.program_id(0)

        @pl.when(n == 0)
        def _():
            dbuf[...] = jnp.zeros_like(dbuf)
            dw_ref[...] = jnp.zeros_like(dw_ref)
            dsm_ref[...] = jnp.zeros_like(dsm_ref)

        cv = cv_ref[...]
        xc = cv - jnp.mean(cv, axis=1, keepdims=True)
        rs = lax.rsqrt(jnp.mean(xc * xc, axis=1, keepdims=True) + LN_EPS)
        xh = xc * rs
        ln = xh * lg_ref[...] + lb_ref[...]
        sg = _sig(ln)
        dln = ds_ref[...] * (sg * (1.0 + ln * (1.0 - sg)))
        dxh = dln * lg_ref[...]
        dcv = rs * (dxh - jnp.mean(dxh, axis=1, keepdims=True) - xh * jnp.mean(dxh * xh, axis=1, keepdims=True))
        dsm_ref[0:1, :] += jnp.sum(dcv, axis=0, keepdims=True)
        dsm_ref[1:2, :] += jnp.sum(dln * xh, axis=0, keepdims=True)
        dsm_ref[2:3, :] += jnp.sum(dln, axis=0, keepdims=True)
        for k in range(8):
            dbuf[k, TB:2 * TB + 8, :] = dbuf[k, 0:TB + 8, :]
            dbuf[k, pl.ds(8 - k, TB), :] = dcv
        a = g_ref[:, 0:CC]
        sb = _sig(g_ref[:, CC:2 * CC])
        u = a * sb
        du = jnp.zeros((TB, CC), F32)
        for w in range(CONV_W):
            off = HALO - w
            sh = dbuf[off % 8, pl.ds(off + 8 - off % 8, TB), :]
            du = du + sh * w_ref[w:w + 1, :]
            dw_ref[w:w + 1, :] += jnp.sum(u * sh, axis=0, keepdims=True)
        dglu_ref[:, 0:CC] = (du * sb).astype(BF16)
        dglu_ref[:, CC:2 * CC] = (du * a * sb * (1.0 - sb)).astype(BF16)

    rev = lambda n: (NB - 1 - n, 0)
    vec = pl.BlockSpec((1, CC), lambda n: (0, 0))
    blk = pl.BlockSpec((TB, CC), rev)
    return dict(
        body=body,
        in_specs=[blk, blk, _cols_spec(TB, 2 * CC, lambda n: (NB - 1 - n, C_GLU)),
                  pl.BlockSpec((32, CC), lambda n: (0, 0)), vec, vec],
        args=[ds, cv, z0, conv_w, ln_g, ln_b],
        out_specs=[pl.BlockSpec((TB, 2 * CC), rev), pl.BlockSpec((32, CC), lambda n: (0, 0)),
                   pl.BlockSpec((8, CC), lambda n: (0, 0))],
        out_shape=[jax.ShapeDtypeStruct((R, 2 * CC), BF16), jax.ShapeDtypeStruct((32, CC), F32),
                   jax.ShapeDtypeStruct((8, CC), F32)],
        scratch=[pltpu.VMEM((8, 2 * TB + 8, CC), F32)])


def _split_dot(x, t):
    hi = x.astype(BF16)
    lo = (x - hi.astype(F32)).astype(BF16)
    return _dot(hi, t) + _dot(lo, t)


def _stack_heads(x):
    lane = lax.broadcasted_iota(jnp.int32, (TB, 128), 1)
    return jnp.concatenate([jnp.where(lane < HD, x, 0.0), jnp.where(lane < HD, 0.0, x)], axis=0).astype(BF16)


def _sb_stack(qv, i):
    lane2, row2 = _lane_row((2 * TB, 128))
    qpos2 = i * TB + (row2 & (TB - 1))
    lane, row = _lane_row((TB, 128))
    return _stack_heads(qv), lane2, qpos2, (row > lane).astype(BF16)


SB_U = 3
SB_DEAD = -104.0
SB_P = 4


def _sb_fwd(q, k, v, g):
    def body(q_ref, k_ref, v_ref, g_ref, o_ref, m_ref, c_ref, n_ref):
        p, i = pl.program_id(0), pl.program_id(1)
        lane, row = _lane_row((TB, 128))
        lo = lane < HD
        slabs = [slice(s * 128, (s + 1) * 128) for s in range(SB_P)]
        q2s = []
        for sl in slabs:
            q2, lane2, qpos2, tri_gt = _sb_stack(q_ref[:, sl].astype(F32) * SCALE, i)
            q2s.append(q2)

        def cond(st):
            t, _, c2s = st
            alive = jnp.max(c2s[0])
            for c2 in c2s[1:]:
                alive = jnp.maximum(alive, jnp.max(c2))
            return jnp.logical_and(i - SB_U * t >= 0, alive > SB_DEAD)

        def step(st):
            t, accs, c2s = st
            accs, c2s = list(accs), list(c2s)
            jrs = [i - SB_U * t - u for u in range(SB_U)]
            j0s = [pl.multiple_of(jnp.maximum(jr, 0) * TB, TB) for jr in jrs]
            valids = []
            for jr in jrs:
                kpos = jr * TB + lane2
                valids.append((kpos >= PAD) & (kpos < qpos2))
            zs = [[jnp.where(valid, _dot_nt(q2s[s], k_ref[pl.ds(j0, TB), slabs[s]]), NEG)
                   for j0, valid in zip(j0s, valids)] for s in range(SB_P)]
            lbs, l1s = [], []
            for s in range(SB_P):
                lbs.append([jnp.minimum(z, 0.0) - jnp.log(1.0 + jnp.exp(-jnp.abs(z))) for z in zs[s]])
                l1s.append([lb - z for lb, z in zip(lbs[s], zs[s])])
            sfxs = [[_split_dot(l1, tri_gt) for l1 in l1s[s]] for s in range(SB_P)]
            carries = []
            for s in range(SB_P):
                cs, c2 = [], c2s[s]
                for jr, l1 in zip(jrs, l1s[s]):
                    cs.append(c2)
                    c_ref[:, slabs[s]] = jnp.where(lane == 2 * jr, c2[0:TB],
                                                   jnp.where(lane == 2 * jr + 1, c2[TB:2 * TB], c_ref[:, slabs[s]]))
                    c2 = c2 + jnp.sum(l1, axis=1, keepdims=True)
                carries.append(cs)
                c2s[s] = c2
            for s in range(SB_P):
                for j0, valid, lb, sfx, cu in zip(j0s, valids, lbs[s], sfxs[s], carries[s]):
                    a = jnp.exp(lb + sfx + cu).astype(BF16)
                    av = _dot(a, v_ref[pl.ds(j0, TB), slabs[s]])
                    accs[s] = accs[s] + jnp.where(lo, av[0:TB], av[TB:2 * TB])
            return t + 1, tuple(accs), tuple(c2s)

        c_ref[...] = jnp.zeros_like(c_ref)
        init = (jnp.int32(0), tuple(jnp.zeros((TB, 128), F32) for _ in slabs),
                tuple(jnp.zeros((2 * TB, 1), F32) for _ in slabs))
        t, accs, _ = lax.while_loop(cond, step, init)
        for sl, acc in zip(slabs, accs):
            o_ref[:, sl] = acc
            gv = g_ref[:, sl]
            m_ref[:, sl] = (acc * (gv * _sig(gv))).astype(BF16)
        n_ref[p, i] = t

    wide = SB_P * 128
    slab = pl.BlockSpec((R, wide), lambda p, i: (0, p))
    blk = pl.BlockSpec((TB, wide), lambda p, i: (i, p))
    sd = jax.ShapeDtypeStruct((R, D), F32)
    return pl.pallas_call(
        body, grid=(D // wide, NB), in_specs=[blk, slab, slab, blk],
        out_specs=[blk, blk, blk, pl.BlockSpec(memory_space=pltpu.SMEM)],
        out_shape=[sd, jax.ShapeDtypeStruct((R, D), BF16), sd, jax.ShapeDtypeStruct((D // wide, NB), jnp.int32)],
        name="sb_fwd", compiler_params=_cp(("arbitrary", "arbitrary")))(q, k, v, g)


def _sb_bwd(trips, q, k, v, car, dm, g, o):
    def body(n_ref, q_ref, k_ref, v_ref, c_ref, dm_ref, g_ref, o_ref, dq_ref, dko_ref, dvo_ref, dg_ref,
             dk_ref, dv_ref):
        p, i = pl.program_id(0), pl.program_id(1)

        @pl.when(i == 0)
        def _():
            dk_ref[...] = jnp.zeros_like(dk_ref)
            dv_ref[...] = jnp.zeros_like(dv_ref)

        lane, row = _lane_row((TB, 128))
        lo = lane < HD
        tri_lt = (row < lane).astype(BF16)
        slabs = [slice(s * 128, (s + 1) * 128) for s in range(SB_P)]
        q2s, do2s, cts = [], [], []
        for sl in slabs:
            q2, lane2, qpos2, tri_gt = _sb_stack(q_ref[:, sl].astype(F32) * SCALE, i)
            q2s.append(q2)
            gv, dmv = g_ref[:, sl], dm_ref[:, sl]
            sg = _sig(gv)
            dg_ref[:, sl] = (dmv * o_ref[:, sl] * (sg * (1.0 + gv * (1.0 - sg)))).astype(BF16)
            do2s.append(_stack_heads(dmv * (gv * sg)))
            cts.append(c_ref[:, sl])
        trips_i = n_ref[p, i]
        first = jnp.maximum(i + 1 - SB_U * trips_i, 0)

        def step(t, carry):
            dqs, g2s = carry
            dqs, g2s = list(dqs), list(g2s)
            jrs = [first + SB_U * t + u for u in range(SB_U)]
            j0s = [pl.multiple_of(jnp.minimum(jr, i) * TB, TB) for jr in jrs]
            valids = []
            for jr in jrs:
                kpos = jr * TB + lane2
                valids.append((kpos >= PAD) & (kpos < qpos2))
            ks = [[k_ref[pl.ds(j0, TB), sl] for j0 in j0s] for sl in slabs]
            zs = [[jnp.where(valid, _dot_nt(q2s[s], kj), NEG) for kj, valid in zip(ks[s], valids)] for s in range(SB_P)]
            das = [[_dot_nt(do2s[s], v_ref[pl.ds(j0, TB), slabs[s]]) for j0 in j0s] for s in range(SB_P)]
            es = [[jnp.exp(-jnp.abs(z)) for z in zs[s]] for s in range(SB_P)]
            lbs = [[jnp.minimum(z, 0.0) - jnp.log(1.0 + e) for z, e in zip(zs[s], es[s])] for s in range(SB_P)]
            l1s = [[lb - z for lb, z in zip(lbs[s], zs[s])] for s in range(SB_P)]
            sfxs = [[_split_dot(l1, tri_gt) for l1 in l1s[s]] for s in range(SB_P)]
            a_s, gmats, gpres = [], [], []
            for s in range(SB_P):
                a_l, gm_l, gp_l, g2 = [], [], [], g2s[s]
                for jr, valid, lb, sfx, da in zip(jrs, valids, lbs[s], sfxs[s], das[s]):
                    later = jnp.concatenate(
                        [jnp.sum(jnp.where(lane == 2 * jr + hh, cts[s], 0.0), axis=1, keepdims=True) for hh in range(2)],
                        axis=0)
                    a = jnp.exp(lb + sfx + later)
                    gmat = da * a
                    a_l.append(a.astype(BF16))
                    gm_l.append(gmat)
                    gp_l.append(g2)
                    g2 = g2 + jnp.sum(gmat, axis=1, keepdims=True)
                a_s.append(a_l)
                gmats.append(gm_l)
                gpres.append(gp_l)
                g2s[s] = g2
            pres = [[gp + _split_dot(gmat, tri_lt) for gp, gmat in zip(gpres[s], gmats[s])] for s in range(SB_P)]
            for s in range(SB_P):
                for j0, kj, valid, z, e, gmat, pre, a in zip(j0s, ks[s], valids, zs[s], es[s], gmats[s], pres[s], a_s[s]):
                    r = 1.0 / (1.0 + e)
                    big = z >= 0.0
                    beta = jnp.where(big, r, e * r)
                    omb = jnp.where(big, e * r, r)
                    dz = (gmat * omb - beta * pre).astype(BF16)
                    dq2 = _dot(dz, kj)
                    dqs[s] = dqs[s] + jnp.where(lo, dq2[0:TB], dq2[TB:2 * TB])
                    dk_ref[pl.ds(j0, TB), slabs[s]] += _dot_tn(dz, q2s[s])
                    dv_ref[pl.ds(j0, TB), slabs[s]] += _dot_tn(a, do2s[s])
            return tuple(dqs), tuple(g2s)

        init = (tuple(jnp.zeros((TB, 128), F32) for _ in slabs), tuple(jnp.zeros((2 * TB, 1), F32) for _ in slabs))
        dqs, _ = lax.fori_loop(0, trips_i, step, init)
        for sl, dq in zip(slabs, dqs):
            dq_ref[:, sl] = (dq * SCALE).astype(BF16)

        @pl.when(i == NB - 1)
        def _():
            dko_ref[...] = dk_ref[...].astype(BF16)
            dvo_ref[...] = dv_ref[...].astype(BF16)

    wide = SB_P * 128
    slab = pl.BlockSpec((R, wide), lambda p, i: (0, p))
    blk = pl.BlockSpec((TB, wide), lambda p, i: (i, p))
    sd = jax.ShapeDtypeStruct((R, D), BF16)
    return pl.pallas_call(
        body, grid=(D // wide, NB),
        in_specs=[pl.BlockSpec(memory_space=pltpu.SMEM), blk, slab, slab, blk, blk, blk, blk],
        out_specs=[blk, slab, slab, blk], out_shape=[sd, sd, sd, sd],
        scratch_shapes=[pltpu.VMEM((R, wide), F32), pltpu.VMEM((R, wide), F32)], name="sb_bwd",
        compiler_params=_cp(("arbitrary", "arbitrary")))(trips, q, k, v, car, dm, g, o)


def _adamw(w, parts, m, v, name):
    rows, cols = w.shape
    tr = next((t for t in (256, 176) if rows % t == 0), rows)
    nparts = len(parts)

    def body(*refs):
        w_ref = refs[0]
        p_refs = refs[1:1 + nparts]
        m_ref, v_ref, g_ref, d_ref, nm_ref, nv_ref = refs[1 + nparts:]
        g = p_refs[0][...]
        for p_ref in p_refs[1:]:
            g = g + p_ref[...]
        nm = ADAM_B1 * m_ref[...] + (1.0 - ADAM_B1) * g
        nv = ADAM_B2 * v_ref[...] + (1.0 - ADAM_B2) * (g * g)
        m_hat = nm / (1.0 - ADAM_B1 ** ADAM_STEP)
        v_hat = nv / (1.0 - ADAM_B2 ** ADAM_STEP)
        g_ref[...] = g
        d_ref[...] = -ADAM_LR * (m_hat / (jnp.sqrt(v_hat) + ADAM_EPS) + ADAM_WD * w_ref[...])
        nm_ref[...] = nm
        nv_ref[...] = nv

    blk = pl.BlockSpec((tr, cols), lambda i: (i, 0))
    sd = jax.ShapeDtypeStruct((rows, cols), F32)
    return pl.pallas_call(
        body, grid=(rows // tr,), in_specs=[blk] * (3 + nparts), out_specs=[blk] * 4, out_shape=[sd] * 4,
        name=name, compiler_params=_cp(("parallel",)))(w, *parts, m, v)


def _sum8(buf, name):
    _, rows, cols = buf.shape

    def body(b_ref, o_ref):
        acc = b_ref[0]
        for i in range(1, 8):
            acc = acc + b_ref[i]
        o_ref[...] = acc

    return pl.pallas_call(
        body, out_shape=jax.ShapeDtypeStruct((rows, cols), F32), name=name,
        compiler_params=pltpu.CompilerParams(vmem_limit_bytes=VMEM_LIMIT))(buf)


MESH = pl.DeviceIdType.MESH
ANY = pl.BlockSpec(memory_space=pl.ANY)


def _chip_peers():
    x, y = lax.axis_index("x"), lax.axis_index("y")
    return [(1 - x, y), (x, 1 - y), (1 - x, 1 - y)]


def _gather_chips(shards):
    plan = _gather_plan(shards)

    def body(*refs):
        n = len(shards)
        ins, outs, sems = refs[:n], refs[n:2 * n], refs[2 * n:]
        plan["start"](ins, outs, sems)
        plan["mid"](ins, outs, sems)
        plan["finish"](ins, outs, sems)

    n = len(shards)
    res = pl.pallas_call(
        body, in_specs=[ANY] * n, out_specs=[ANY] * n, out_shape=plan["out_shape"],
        scratch_shapes=plan["sems"], name="gather_chips")(*plan["args"])
    return plan["post"](res)


def _gather_plan(shards):
    n = len(shards)
    shards = [s.reshape((2, s.shape[0] // 2) + s.shape[1:]) for s in shards]

    def copies(kind, ins, outs, sems):
        s1, r1, s2, r2 = sems
        x, y, c = lax.axis_index("x"), lax.axis_index("y"), lax.axis_index("c")
        me = 2 * x + y
        out = []
        for j, (px, py) in enumerate(_chip_peers()):
            for a in range(n):
                k = j * n + a
                got = outs[a].at[2 * px + py].at[c]
                other = outs[a].at[2 * px + py].at[1 - c]
                src, dst, ss, rs, dev = {
                    "first": (ins[a].at[c], outs[a].at[me].at[c], s1, r1, (px, py, c)),
                    "landed": (got, got, s1, r1, (px, py, c)),
                    "passed": (got, got, s2, r2, (x, y, 1 - c)),
                    "theirs": (other, other, s2, r2, (x, y, 1 - c)),
                }[kind]
                out.append(pltpu.make_async_remote_copy(
                    src_ref=src, dst_ref=dst, send_sem=ss.at[k], recv_sem=rs.at[k], device_id=dev, device_id_type=MESH))
        return out

    def start(ins, outs, sems):
        for cp in copies("first", ins, outs, sems):
            cp.start()

    def mid(ins, outs, sems):
        for got, fwd in zip(copies("landed", ins, outs, sems), copies("passed", ins, outs, sems)):
            got.wait_recv()
            fwd.start()

    def finish(ins, outs, sems):
        for cp in copies("theirs", ins, outs, sems):
            cp.wait_recv()
        for cp in copies("first", ins, outs, sems) + copies("passed", ins, outs, sems):
            cp.wait_send()

    def post(res):
        me = 2 * lax.axis_index("x") + lax.axis_index("y")
        res = [lax.dynamic_update_index_in_dim(r, s, me, 0) for r, s in zip(res, shards)]
        return [r.reshape((N_CHIPS, 2 * r.shape[2]) + r.shape[3:]) for r in res]

    return dict(args=shards, out_shape=[jax.ShapeDtypeStruct((N_CHIPS,) + s.shape, s.dtype) for s in shards],
                sems=[pltpu.SemaphoreType.DMA((3 * n,))] * 4, start=start, mid=mid, finish=finish, post=post)


def _rows_call(name, parts, plan):
    n_in = [len(p["args"]) for p in parts]
    n_out = [len(p["out_shape"]) for p in parts]
    n_scr = [len(p["scratch"]) for p in parts]
    c_in, c_out = len(plan["args"]), len(plan["out_shape"])

    def split(refs, sizes):
        out, pos = [], 0
        for k in sizes:
            out.append(refs[pos:pos + k])
            pos += k
        return out

    def body(*refs):
        ins, outs, scr = split(refs, [sum(n_in) + c_in, sum(n_out) + c_out, sum(n_scr) + len(plan["sems"])])
        p_in, p_out, p_scr = split(ins, n_in + [c_in]), split(outs, n_out + [c_out]), split(scr, n_scr + [len(plan["sems"])])
        comm = (p_in[-1], p_out[-1], p_scr[-1])
        step = pl.program_id(0)

        @pl.when(step == 0)
        def _():
            plan["start"](*comm)

        for p, i, o, s in zip(parts, p_in, p_out, p_scr):
            p["body"](*i, *o, *s)

        @pl.when(step == NB - 2)
        def _():
            plan["mid"](*comm)

        @pl.when(step == NB - 1)
        def _():
            plan["finish"](*comm)

    flat = lambda key: [v for p in parts for v in p[key]]
    res = pl.pallas_call(
        body, grid=(NB,), in_specs=flat("in_specs") + [ANY] * c_in, out_specs=flat("out_specs") + [ANY] * c_out,
        out_shape=flat("out_shape") + plan["out_shape"], scratch_shapes=flat("scratch") + plan["sems"],
        name=name, compiler_params=_cp(("arbitrary",)))(*flat("args"), *plan["args"])
    outs = split(res, n_out + [c_out])
    return outs[:-1], outs[-1]


def _pair_exchange(grads, name):
    n = len(grads)
    hs = [g.shape[1] // 2 for g in grads]
    grads = [g.reshape((N_CHIPS, 2, h) + g.shape[2:]) for g, h in zip(grads, hs)]

    def body(*refs):
        ins, got = refs[:n], refs[n:2 * n]
        ssem, rsem = refs[2 * n:]
        x, y, c = lax.axis_index("x"), lax.axis_index("y"), lax.axis_index("c")
        sends = [pltpu.make_async_remote_copy(
            src_ref=ins[a].at[:, 1 - c], dst_ref=got[a], send_sem=ssem.at[a],
            recv_sem=rsem.at[a], device_id=(x, y, 1 - c), device_id_type=MESH) for a in range(n)]
        for cp in sends:
            cp.start()
        for cp in sends:
            cp.wait()

    half_shapes = [jax.ShapeDtypeStruct((N_CHIPS, h) + g.shape[3:], g.dtype) for g, h in zip(grads, hs)]
    got = pl.pallas_call(
        body, in_specs=[ANY] * n, out_specs=[ANY] * n, out_shape=half_shapes,
        scratch_shapes=[pltpu.SemaphoreType.DMA((n,))] * 2, name=name)(*grads)
    c = lax.axis_index("c")
    own = [lax.dynamic_index_in_dim(g, c, 1, keepdims=False) for g in grads]
    return own, got


def _sum_pair(own, got, send_dtype, name):
    _, rows, cols = own.shape
    tr = 256 if rows % 256 == 0 else rows

    def body(a_ref, b_ref, f_ref, s_ref):
        t = a_ref[...].astype(F32) + b_ref[...].astype(F32)
        f_ref[...] = t
        s_ref[...] = t.astype(send_dtype)

    blk = pl.BlockSpec((N_CHIPS, tr, cols), lambda i: (0, i, 0))
    return pl.pallas_call(
        body, grid=(rows // tr,), in_specs=[blk, blk], out_specs=[blk, blk],
        out_shape=[jax.ShapeDtypeStruct(own.shape, F32), jax.ShapeDtypeStruct(own.shape, send_dtype)],
        name=name, compiler_params=_cp(("parallel",)))(own, got)


def _own_slab(keep):
    me = 2 * lax.axis_index("x") + lax.axis_index("y")
    return [lax.dynamic_index_in_dim(k, me, 0, keepdims=False) for k in keep]


def _scatter_plan(send):
    n = len(send)

    def copies(sin, land, sems):
        ssem, rsem = sems
        c = lax.axis_index("c")
        return [pltpu.make_async_remote_copy(
            src_ref=sin[a].at[2 * px + py], dst_ref=land[a].at[j], send_sem=ssem.at[j * n + a],
            recv_sem=rsem.at[j * n + a], device_id=(px, py, c), device_id_type=MESH)
            for j, (px, py) in enumerate(_chip_peers()) for a in range(n)]

    def start(sin, land, sems):
        for cp in copies(sin, land, sems):
            cp.start()

    def finish(sin, land, sems):
        for cp in copies(sin, land, sems):
            cp.wait()

    return dict(args=list(send), out_shape=[jax.ShapeDtypeStruct((3,) + s.shape[1:], s.dtype) for s in send],
                sems=[pltpu.SemaphoreType.DMA((3 * n,))] * 2, start=start, mid=lambda *a: None, finish=finish)


def _sum_shard(mine, land, name):
    rows, cols = mine.shape
    tr = 256 if rows % 256 == 0 else rows

    def body(m_ref, l_ref, o_ref):
        o_ref[...] = ((m_ref[...] + l_ref[0].astype(F32)) + l_ref[1].astype(F32)) + l_ref[2].astype(F32)

    return pl.pallas_call(
        body, grid=(rows // tr,),
        in_specs=[pl.BlockSpec((tr, cols), lambda i: (i, 0)), pl.BlockSpec((3, tr, cols), lambda i: (0, i, 0))],
        out_specs=pl.BlockSpec((tr, cols), lambda i: (i, 0)), out_shape=jax.ShapeDtypeStruct((rows, cols), F32),
        name=name, compiler_params=_cp(("parallel",)))(mine, land)


def _join_cores(halves):
    n = len(halves)

    def body(*refs):
        ins, outs = refs[:n], refs[n:2 * n]
        ssem, rsem = refs[2 * n:]
        x, y, c = lax.axis_index("x"), lax.axis_index("y"), lax.axis_index("c")
        sends = [pltpu.make_async_remote_copy(
            src_ref=ins[a], dst_ref=outs[a].at[c], send_sem=ssem.at[a], recv_sem=rsem.at[a],
            device_id=(x, y, 1 - c), device_id_type=MESH) for a in range(n)]
        for cp in sends:
            cp.start()
        for a in range(n):
            sends[a].wait_send()
            pltpu.make_async_remote_copy(
                src_ref=ins[a], dst_ref=outs[a].at[1 - c], send_sem=ssem.at[a], recv_sem=rsem.at[a],
                device_id=(x, y, 1 - c), device_id_type=MESH).wait_recv()

    res = pl.pallas_call(
        body, in_specs=[ANY] * n, out_specs=[ANY] * n,
        out_shape=[jax.ShapeDtypeStruct((2,) + h.shape, h.dtype) for h in halves],
        scratch_shapes=[pltpu.SemaphoreType.DMA((n,))] * 2, name="join_cores")(*halves)
    c = lax.axis_index("c")
    res = [lax.dynamic_update_index_in_dim(r, h, c, 0) for r, h in zip(res, halves)]
    return [r.reshape((2 * r.shape[1],) + r.shape[2:]) for r in res]


def _gather_all(vec):
    def body(v_ref, o_ref, lsem, ssem, rsem):
        x, y, c = lax.axis_index("x"), lax.axis_index("y"), lax.axis_index("c")
        me = 4 * x + 2 * y + c
        local = pltpu.make_async_copy(v_ref, o_ref.at[me], lsem)
        local.start()
        cps = []
        for k in range(1, 8):
            px, py, pc = x ^ (k >> 2), y ^ ((k >> 1) & 1), c ^ (k & 1)
            cps.append(pltpu.make_async_remote_copy(
                src_ref=v_ref, dst_ref=o_ref.at[me], send_sem=ssem.at[k - 1], recv_sem=rsem.at[k - 1],
                device_id=(px, py, pc), device_id_type=MESH))
        for cp in cps:
            cp.start()
        for k in range(1, 8):
            px, py, pc = x ^ (k >> 2), y ^ ((k >> 1) & 1), c ^ (k & 1)
            pltpu.make_async_remote_copy(
                src_ref=v_ref, dst_ref=o_ref.at[4 * px + 2 * py + pc], send_sem=ssem.at[k - 1],
                recv_sem=rsem.at[k - 1], device_id=(px, py, pc), device_id_type=MESH).wait_recv()
        for cp in cps:
            cp.wait_send()
        local.wait()

    return pl.pallas_call(
        body, in_specs=[ANY], out_specs=ANY, out_shape=jax.ShapeDtypeStruct((8,) + vec.shape, vec.dtype),
        scratch_shapes=[pltpu.SemaphoreType.DMA, pltpu.SemaphoreType.DMA((7,)), pltpu.SemaphoreType.DMA((7,))],
        name="gather_all")(vec)


def _rope_tables():
    pos = (jnp.arange(R, dtype=jnp.int32) - PAD).astype(F32)
    half = HD // 2
    inv = ROPE_THETA ** (-jnp.arange(half, dtype=F32) / half)
    ang = pos[:, None] * inv[None, :]
    cos, sin = jnp.cos(ang), jnp.sin(ang)
    cs = jnp.tile(cos, (1, 4))
    sn = jnp.tile(jnp.concatenate([-sin, sin], axis=1), (1, 2))
    return cs, sn


def _local_step(x, target, p):
    w0t = p["ab_w_in"]
    conv_w = jnp.concatenate([p["ab_conv_w"], jnp.zeros((1, CC), F32)], axis=0)
    cs, sn = _rope_tables()

    h0 = jnp.concatenate([jnp.zeros((PAD, D), F32), p["meta_tokens"], x], axis=0)

    xn0 = _rms_fwd(h0, p["ab_pre_norm"], "rms_fwd0")
    plan = _gather_plan([p["sb_w_out"], p["ab_w_out"], p["ab_w_pw2"]])
    z0, gathered = _mm([(xn0, w0t)], F32, "in_proj0", 544, 1408, tb=True, plan=plan)
    wo1, wo0, wpw = plan["post"](gathered)
    wo1, wo0, wpw = wo1.reshape(D, D), wo0.reshape(D, D), wpw.reshape(CC, CC)
    plan = _gather_plan([p["sb_w_in"]])
    ((o0, a0, lse0), (cv0, s0)), gathered = _rows_call(
        "fwd0", [_swa_fwd(z0, cs, sn, p["ab_sinks"]),
                 _conv_fwd(z0, conv_w, p["ab_conv_b"], p["ab_conv_ln_g"], p["ab_conv_ln_b"])], plan)
    (w1,) = plan["post"](gathered)
    t0 = _mm([(s0, wpw)], F32, "pw2", 544, 512)
    c0 = _gate_fwd(t0, z0, C_GB, "gate_b_fwd")
    wo0h = wo0.reshape(2, CC, D)
    y0 = _mm([(a0, (wo0h, 0)), (c0, (wo0h, 1))], F32, "out_proj0", 544, 1024)

    h1, xn1 = _post_rms_fwd(h0, y0, p["ab_post_norm"], p["sb_pre_norm"], "post_rms_fwd")
    q1 =_mm([(xn1, (w1, 0))], BF16, "in_proj1_q", 544, 1024)
    k1 = _mm([(xn1, (w1, 1))], BF16, "in_proj1_k", 544, 1024)
    v1 = _mm([(xn1, (w1, 2))], BF16, "in_proj1_v", 544, 1024)
    g1 = _mm([(xn1, (w1, 3))], F32, "in_proj1_g", 544, 1024)
    o1, m1, car1, trips1 = _sb_fwd(q1, k1, v1, g1)
    y1 = _mm([(m1, wo1)], F32, "out_proj1", 544, 1024)

    dh2, dy1, d_sb_post, loss_row = _tail(h1, y1, p["sb_post_norm"], target)

    dm1 = _mm([(dy1, wo1)], F32, "out_proj1_dx", 544, 1024, tb=True)
    d_wo1 = _mm([(m1, dy1)], BF16, "out_proj1_dw", 512, 1024, ta=True)
    dq1, dk1, dv1, dg1 = _sb_bwd(trips1, q1, k1, v1, car1, dm1, g1, o1)
    dz1 = [dq1, dk1, dv1, dg1]
    dxn1 = _mm([(dz1[j], (w1, j)) for j in range(4)], F32, "in_proj1_dx", 544, 1024, tb=True)
    d_w1 = jnp.stack([_mm([(xn1, dz1[j])], BF16, "in_proj1_dw%d" % j, 512, 1024, ta=True) for j in range(4)])

    dh1, d_sb_pre, dy0, d_ab_post = _rms_post_bwd(dxn1, h1, p["sb_pre_norm"], dh2, y0, p["ab_post_norm"],
                                                  "rms_post_bwd")
    dmix0 = _mm([(dy0, wo0)], F32, "out_proj0_dx", 544, 1024, tb=True)
    d_wo0 = jnp.concatenate([_mm([(a0, dy0)], BF16, "out_proj0_dw_a", 512, 1024, ta=True),
                             _mm([(c0, dy0)], BF16, "out_proj0_dw_b", 512, 1024, ta=True)], axis=0)
    dt0, dgb0 = _gate_bwd(dmix0, 512, t0, z0, C_GB, "gate_b_bwd")
    ds0 = _mm([(dt0, wpw)], F32, "pw2_dx", 544, 512, tb=True)
    d_wpw = _mm([(s0, dt0)], BF16, "pw2_dw", 512, 512, ta=True)
    early = ("sb_w_in", "sb_w_out", "ab_w_out", "ab_w_pw2")
    own1, got1 = _pair_exchange([d_w1, d_wo1.reshape(N_CHIPS, 256, D), d_wo0.reshape(N_CHIPS, 256, D),
                                 d_wpw.reshape(N_CHIPS, 128, CC)], "pair_exchange1")
    pair1 = [_sum_pair(o, t, BF16, "sum_pair_" + nm) for o, t, nm in zip(own1, got1, early)]
    plan = _scatter_plan([pr[1] for pr in pair1])
    ((dglu0, d_convw, d_small), (dq0, dga0, dkv0, d_sinks)), land1 = _rows_call(
        "bwd0", [_conv_bwd(ds0, cv0, z0, conv_w, p["ab_conv_ln_g"], p["ab_conv_ln_b"]),
                 _swa_bwd(z0, cs, sn, p["ab_sinks"], o0, dmix0, lse0)], plan)
    halves1 = [_sum_shard(mi, la, "sum_shard_" + nm)
               for mi, la, nm in zip(_own_slab([pr[0] for pr in pair1]), land1, early)]
    dz0 = jnp.concatenate([dq0, dkv0, dga0, dglu0, dgb0], axis=1)
    d_w0t = _mm([(dz0, xn0)], BF16, "in_proj0_dw", 1408, 512, ta=True)
    own0, got0 = _pair_exchange([d_w0t.reshape(N_CHIPS, 704, D)], "pair_exchange0")
    keep0, send0 = _sum_pair(own0[0], got0[0], BF16, "sum_pair_ab_w_in")
    plan = _scatter_plan([send0])
    dxn0, land0 = _mm([(dz0, w0t)], F32, "in_proj0_dx", 544, 1024, plan=plan)
    half0 = _sum_shard(_own_slab([keep0])[0], land0[0], "sum_shard_ab_w_in")
    dh0_first, grad_x, d_ab_pre = _rms_bwd(dxn0, h0, p["ab_pre_norm"], dh1, F32, "rms_bwd0", split=True)

    grads = {
        "meta_tokens": dh0_first[PAD:TB], "ab_pre_norm": d_ab_pre, "ab_sinks": d_sinks[0:1, 0:8],
        "ab_conv_w": d_convw[0:CONV_W], "ab_conv_b": d_small[0:1], "ab_conv_ln_g": d_small[1:2],
        "ab_conv_ln_b": d_small[2:3], "ab_post_norm": d_ab_post, "sb_pre_norm": d_sb_pre, "sb_post_norm": d_sb_post,
    }
    h_sb_in, h_sb_out, h_ab_out, h_pw2 = halves1
    return loss_row, grad_x, grads, [half0, h_ab_out, h_pw2, h_sb_in, h_sb_out]


SMALL_ROWS = 80
REP_ROWS = 32

WEIGHTS = ["meta_tokens", "ab_pre_norm", "ab_w_in", "ab_sinks", "ab_conv_w", "ab_conv_b", "ab_conv_ln_g",
           "ab_conv_ln_b", "ab_w_pw2", "ab_w_out", "ab_post_norm", "sb_pre_norm", "sb_w_in", "sb_w_out",
           "sb_post_norm"]
BIG = ["ab_w_in", "ab_w_out", "ab_w_pw2", "sb_w_in", "sb_w_out"]


def _pack_small(conv_w, meta, sb_pre, sb_post):
    pad = lambda a, rows: jnp.pad(a, ((0, rows - a.shape[0]), (0, 0)))
    return jnp.concatenate([pad(conv_w, 32), meta.reshape(32, 128), pad(sb_pre.reshape(2, 128), 8),
                            pad(sb_post.reshape(2, 128), 8)], axis=0)


def _unpack_small(s):
    return s[0:31], s[32:64].reshape(16, 256), s[64:66].reshape(1, 256), s[72:74].reshape(1, 256)


REP_LOSS = 3592


def _pack_rep(pre, post, conv_b, ln_g, ln_b, sinks, extra=None):
    flat = jnp.concatenate([pre.reshape(-1), post.reshape(-1), conv_b.reshape(-1), ln_g.reshape(-1),
                            ln_b.reshape(-1), sinks.reshape(-1)] + ([] if extra is None else [extra.reshape(-1)]))
    flat = jnp.concatenate([flat, jnp.zeros((REP_ROWS * 128 - flat.shape[0],), F32)])
    return flat.reshape(REP_ROWS, 128)


def _unpack_rep(r):
    f = r.reshape(-1)
    return (f[0:1024].reshape(1, 1024), f[1024:2048].reshape(1, 1024), f[2048:2560].reshape(1, 512),
            f[2560:3072].reshape(1, 512), f[3072:3584].reshape(1, 512), f[3584:3592].reshape(1, 8))


def _chips_to_cols(w):
    return w.transpose(1, 0, 2).reshape(w.shape[1], -1)


def kernel(x, meta_tokens, ab_pre_norm, ab_w_in, ab_sinks, ab_conv_w, ab_conv_b, ab_conv_ln_g, ab_conv_ln_b, ab_w_pw2, ab_w_out, ab_post_norm, sb_pre_norm, sb_w_in, sb_w_out, sb_post_norm, loss_target, m_meta_tokens, m_ab_pre_norm, m_ab_w_in, m_ab_sinks, m_ab_conv_w, m_ab_conv_b, m_ab_conv_ln_g, m_ab_conv_ln_b, m_ab_w_pw2, m_ab_w_out, m_ab_post_norm, m_sb_pre_norm, m_sb_w_in, m_sb_w_out, m_sb_post_norm, v_meta_tokens, v_ab_pre_norm, v_ab_w_in, v_ab_sinks, v_ab_conv_w, v_ab_conv_b, v_ab_conv_ln_g, v_ab_conv_ln_b, v_ab_w_pw2, v_ab_w_out, v_ab_post_norm, v_sb_pre_norm, v_sb_w_in, v_sb_w_out, v_sb_post_norm):
    w = dict(meta_tokens=meta_tokens, ab_pre_norm=ab_pre_norm, ab_w_in=ab_w_in, ab_sinks=ab_sinks,
             ab_conv_w=ab_conv_w, ab_conv_b=ab_conv_b, ab_conv_ln_g=ab_conv_ln_g, ab_conv_ln_b=ab_conv_ln_b,
             ab_w_pw2=ab_w_pw2, ab_w_out=ab_w_out, ab_post_norm=ab_post_norm, sb_pre_norm=sb_pre_norm,
             sb_w_in=sb_w_in, sb_w_out=sb_w_out, sb_post_norm=sb_post_norm)
    m = dict(meta_tokens=m_meta_tokens, ab_pre_norm=m_ab_pre_norm, ab_w_in=m_ab_w_in, ab_sinks=m_ab_sinks,
             ab_conv_w=m_ab_conv_w, ab_conv_b=m_ab_conv_b, ab_conv_ln_g=m_ab_conv_ln_g,
             ab_conv_ln_b=m_ab_conv_ln_b, ab_w_pw2=m_ab_w_pw2, ab_w_out=m_ab_w_out, ab_post_norm=m_ab_post_norm,
             sb_pre_norm=m_sb_pre_norm, sb_w_in=m_sb_w_in, sb_w_out=m_sb_w_out, sb_post_norm=m_sb_post_norm)
    v = dict(meta_tokens=v_meta_tokens, ab_pre_norm=v_ab_pre_norm, ab_w_in=v_ab_w_in, ab_sinks=v_ab_sinks,
             ab_conv_w=v_ab_conv_w, ab_conv_b=v_ab_conv_b, ab_conv_ln_g=v_ab_conv_ln_g,
             ab_conv_ln_b=v_ab_conv_ln_b, ab_w_pw2=v_ab_w_pw2, ab_w_out=v_ab_w_out, ab_post_norm=v_ab_post_norm,
             sb_pre_norm=v_sb_pre_norm, sb_w_in=v_sb_w_in, sb_w_out=v_sb_w_out, sb_post_norm=v_sb_post_norm)

    def small_of(d):
        return _pack_small(d["ab_conv_w"][0], d["meta_tokens"], d["sb_pre_norm"], d["sb_post_norm"])

    def rep_of(d):
        return _pack_rep(d["ab_pre_norm"], d["ab_post_norm"], d["ab_conv_b"], d["ab_conv_ln_g"], d["ab_conv_ln_b"],
                         d["ab_sinks"])

    g_in0, g_small = _gather_chips([ab_w_in[0].T.astype(BF16), small_of(w)])
    conv_w_f = _chips_to_cols(g_small[:, 0:31])
    meta_f = _chips_to_cols(g_small[:, 32:64].reshape(N_CHIPS, 16, 256))
    sb_pre_f = g_small[:, 64:66].reshape(1, D)
    sb_post_f = g_small[:, 72:74].reshape(1, D)
    full = {
        "meta_tokens": meta_f, "ab_pre_norm": ab_pre_norm, "ab_w_in": g_in0.reshape(AB_IN, D),
        "ab_sinks": ab_sinks, "ab_conv_w": conv_w_f, "ab_conv_b": ab_conv_b, "ab_conv_ln_g": ab_conv_ln_g,
        "ab_conv_ln_b": ab_conv_ln_b, "ab_w_pw2": ab_w_pw2[0].astype(BF16), "ab_w_out": ab_w_out[0].astype(BF16),
        "ab_post_norm": ab_post_norm, "sb_pre_norm": sb_pre_f, "sb_w_in": sb_w_in[0].astype(BF16),
        "sb_w_out": sb_w_out[0].astype(BF16), "sb_post_norm": sb_post_f,
    }

    loss_row, grad_x, g, halves = _local_step(x[0], loss_target[0], full)

    total = _join_cores(halves)

    rep_g = _pack_rep(g["ab_pre_norm"], g["ab_post_norm"], g["ab_conv_b"], g["ab_conv_ln_g"], g["ab_conv_ln_b"],
                      g["ab_sinks"], loss_row[0:1, 0:1])
    vec = jnp.concatenate([rep_g, jnp.pad(g["ab_conv_w"].reshape(124, 128), ((0, 4), (0, 0))),
                           g["meta_tokens"].reshape(128, 128), g["sb_pre_norm"].reshape(8, 128),
                           g["sb_post_norm"].reshape(8, 128)], axis=0)
    vec_sum = _sum8(_gather_all(vec), "sum8_small")
    rep_sum = vec_sum[0:REP_ROWS]
    loss = rep_sum.reshape(-1)[REP_LOSS]
    me = 2 * lax.axis_index("x") + lax.axis_index("y")
    small_sum = _pack_small(
        lax.dynamic_slice_in_dim(vec_sum[32:156].reshape(CONV_W, CC), me * 128, 128, axis=1),
        lax.dynamic_slice_in_dim(vec_sum[160:288].reshape(N_META, D), me * 256, 256, axis=1),
        lax.dynamic_slice_in_dim(vec_sum[288:296].reshape(1, D), me * 256, 256, axis=1),
        lax.dynamic_slice_in_dim(vec_sum[296:304].reshape(1, D), me * 256, 256, axis=1))

    out_g, out_d, out_m, out_v = {}, {}, {}, {}
    for i, k in enumerate(BIG):
        shp = w[k].shape
        if k == "ab_w_in":
            res = _adamw(w[k][0].T, [total[i]], m[k][0].T, v[k][0].T, "adamw_" + k)
            out_g[k], out_d[k], out_m[k], out_v[k] = [r.T.reshape(shp) for r in res]
            continue
        res = _adamw(w[k][0], [total[i]], m[k][0], v[k][0], "adamw_" + k)
        out_g[k], out_d[k], out_m[k], out_v[k] = [r.reshape(shp) for r in res]
    res = _adamw(small_of(w), [small_sum], small_of(m), small_of(v), "adamw_small")
    for dst, r in zip((out_g, out_d, out_m, out_v), res):
        cw, mt, pre, post = _unpack_small(r)
        dst["ab_conv_w"], dst["meta_tokens"], dst["sb_pre_norm"], dst["sb_post_norm"] = cw[None], mt, pre, post
    res = _adamw(rep_of(w), [rep_sum], rep_of(m), rep_of(v), "adamw_rep")
    for dst, r in zip((out_g, out_d, out_m, out_v), res):
        (dst["ab_pre_norm"], dst["ab_post_norm"], dst["ab_conv_b"], dst["ab_conv_ln_g"], dst["ab_conv_ln_b"],
         dst["ab_sinks"]) = _unpack_rep(r)

    return (loss, grad_x[None], *[out_g[k] for k in WEIGHTS], *[out_d[k] for k in WEIGHTS],
            *[out_m[k] for k in WEIGHTS], *[out_v[k] for k in WEIGHTS])
```

```python
import functools

import jax
import jax.numpy as jnp
from jax import lax
from jax.experimental import pallas as pl
from jax.experimental.pallas import tpu as pltpu

F32 = jnp.float32
BF16 = jnp.bfloat16

D = 1024
SEQ = 2048
N_META = 16
TB = 128
TR = 272
PAD = TB - N_META
R = SEQ + TB
NB = R // TB
HD = 64
ROPE_THETA = 10000.0
NORM_EPS = 1e-6
LN_EPS = 1e-5
NEG = -1e30
CONV_W = 31
SCALE = HD ** -0.5
N_CHIPS = 4

C_Q, C_K, C_V, C_GA, C_GLU, C_GB = 0, 512, 640, 768, 1280, 2304
AB_IN = 2816

ADAM_LR, ADAM_B1, ADAM_B2, ADAM_EPS, ADAM_WD, ADAM_STEP = 0.001, 0.9, 0.999, 1e-08, 0.01, 10

VMEM_LIMIT = 56 * 1024 * 1024


def _cp(sem):
    return pltpu.CompilerParams(dimension_semantics=sem, vmem_limit_bytes=VMEM_LIMIT)


def _sig(x):
    return 1.0 / (1.0 + jnp.exp(-x))


def _dot(a, b):
    return lax.dot_general(a, b, (((1,), (0,)), ((), ())), preferred_element_type=F32)


def _dot_nt(a, b):
    return lax.dot_general(a, b, (((1,), (1,)), ((), ())), preferred_element_type=F32)


def _dot_tn(a, b):
    return lax.dot_general(a, b, (((0,), (0,)), ((), ())), preferred_element_type=F32)


def _mm(pairs, out_dtype, name, tm, tn, ta=False, tb=False, plan=None):
    pairs = [(a, b if isinstance(b, tuple) else (b, None)) for a, b in pairs]
    a0, (b0, _) = pairs[0]
    m = a0.shape[1] if ta else a0.shape[0]
    n = b0.shape[-2] if tb else b0.shape[-1]
    npairs = len(pairs)
    dims = (((0 if ta else 1,), (1 if tb else 0,)), ((), ()))
    c_in = len(plan["args"]) if plan else 0
    c_out = len(plan["out_shape"]) if plan else 0
    steps = (m // tm) * (n // tn)

    def body(*refs):
        o_ref = refs[2 * npairs + c_in]
        if plan:
            comm = (refs[2 * npairs:2 * npairs + c_in], refs[2 * npairs + c_in + 1:2 * npairs + c_in + 1 + c_out],
                    refs[2 * npairs + c_in + 1 + c_out:])
            step = pl.program_id(0) * (n // tn) + pl.program_id(1)

            @pl.when(step == 0)
            def _():
                plan["start"](*comm)

        acc = None
        for i in range(npairs):
            t = lax.dot_general(refs[2 * i][...].astype(BF16), refs[2 * i + 1][...].astype(BF16), dims,
                                preferred_element_type=F32)
            acc = t if acc is None else acc + t
        o_ref[...] = acc.astype(out_dtype)
        if plan:
            @pl.when(step == steps - 2)
            def _():
                plan["mid"](*comm)

            @pl.when(step == steps - 1)
            def _():
                plan["finish"](*comm)

    in_specs, args = [], []
    for a, (b, sel) in pairs:
        k = a.shape[0] if ta else a.shape[1]
        in_specs.append(pl.BlockSpec((k, tm), lambda i, j: (0, i)) if ta else pl.BlockSpec((tm, k), lambda i, j: (i, 0)))
        bshape, bidx = ((tn, k), lambda i, j: (j, 0)) if tb else ((k, tn), lambda i, j: (0, j))
        if sel is None:
            in_specs.append(pl.BlockSpec(bshape, bidx))
        else:
            in_specs.append(pl.BlockSpec((None,) + bshape, functools.partial(lambda i, j, f, s: (s,) + f(i, j), f=bidx, s=sel)))
        args += [a, b]
    out_spec = pl.BlockSpec((tm, tn), lambda i, j: (i, j))
    out_shape = jax.ShapeDtypeStruct((m, n), out_dtype)
    if not plan:
        return pl.pallas_call(
            body, grid=(m // tm, n // tn), in_specs=in_specs, out_specs=out_spec, out_shape=out_shape, name=name,
            compiler_params=_cp(("parallel", "parallel")))(*args)
    assert steps >= 2
    res = pl.pallas_call(
        body, grid=(m // tm, n // tn), in_specs=in_specs + [ANY] * c_in, out_specs=[out_spec] + [ANY] * c_out,
        out_shape=[out_shape] + plan["out_shape"], scratch_shapes=plan["sems"], name=name,
        compiler_params=_cp(("arbitrary", "arbitrary")))(*args, *plan["args"])
    return res[0], res[1:]


def _rms_fwd(h, g, name):
    def body(h_ref, g_ref, o_ref):
        x = h_ref[...]
        r = lax.rsqrt(jnp.mean(x * x, axis=1, keepdims=True) + NORM_EPS)
        o_ref[...] = (x * r * g_ref[...]).astype(BF16)

    return pl.pallas_call(
        body, grid=(R // TR,),
        in_specs=[pl.BlockSpec((TR, D), lambda n: (n, 0)), pl.BlockSpec((1, D), lambda n: (0, 0))],
        out_specs=pl.BlockSpec((TR, D), lambda n: (n, 0)),
        out_shape=jax.ShapeDtypeStruct((R, D), BF16), name=name, compiler_params=_cp(("parallel",)))(h, g)


def _rms_bwd(dout, x, g, res, out_dtype, name, split=False):
    has_res = res is not None

    def body(*refs):
        if split:
            refs = list(refs)
            dx_rest_ref = refs.pop(-2)
        if has_res:
            d_ref, x_ref, g_ref, r_ref, dx_ref, dg_ref = refs
        else:
            d_ref, x_ref, g_ref, dx_ref, dg_ref = refs
        n = pl.program_id(0)
        xv = x_ref[...]
        dv = d_ref[...]
        r = lax.rsqrt(jnp.mean(xv * xv, axis=1, keepdims=True) + NORM_EPS)
        xh = xv * r
        dxh = dv * g_ref[...]
        dx = r * (dxh - xh * jnp.mean(dxh * xh, axis=1, keepdims=True))
        if has_res:
            dx = dx + r_ref[...]
        row = lax.broadcasted_iota(jnp.int32, (TB, D), 0) + n * TB
        dx = jnp.where(row >= PAD, dx, 0.0).astype(out_dtype)
        if split:
            @pl.when(n == 0)
            def _():
                dx_ref[...] = dx

            @pl.when(n > 0)
            def _():
                dx_rest_ref[...] = dx
        else:
            dx_ref[...] = dx

        @pl.when(n == 0)
        def _():
            dg_ref[...] = jnp.zeros_like(dg_ref)

        dg_ref[...] += jnp.sum(dv * xh, axis=0, keepdims=True)

    blk = pl.BlockSpec((TB, D), lambda n: (n, 0))
    vec = pl.BlockSpec((1, D), lambda n: (0, 0))
    ins = [dout, x, g] + ([res] if has_res else [])
    in_specs = [blk, blk, vec] + ([blk] if has_res else [])
    if split:
        out_specs = [pl.BlockSpec((TB, D), lambda n: (0, 0)), pl.BlockSpec((TB, D), lambda n: (jnp.maximum(n - 1, 0), 0)), vec]
        out_shape = [jax.ShapeDtypeStruct((TB, D), out_dtype), jax.ShapeDtypeStruct((SEQ, D), out_dtype),
                     jax.ShapeDtypeStruct((1, D), F32)]
    else:
        out_specs = [blk, vec]
        out_shape = [jax.ShapeDtypeStruct((R, D), out_dtype), jax.ShapeDtypeStruct((1, D), F32)]
    return pl.pallas_call(
        body, grid=(NB,), in_specs=in_specs, out_specs=out_specs, out_shape=out_shape,
        name=name, compiler_params=_cp(("arbitrary",)))(*ins)


def _post_rms_fwd(h, y, g_post, g_next, name):
    def body(h_ref, y_ref, gp_ref, gn_ref, o_ref, x_ref):
        yv = y_ref[...]
        r = lax.rsqrt(jnp.mean(yv * yv, axis=1, keepdims=True) + NORM_EPS)
        hn = h_ref[...] + yv * r * gp_ref[...]
        o_ref[...] = hn
        r2 = lax.rsqrt(jnp.mean(hn * hn, axis=1, keepdims=True) + NORM_EPS)
        x_ref[...] = (hn * r2 * gn_ref[...]).astype(BF16)

    blk = pl.BlockSpec((TR, D), lambda n: (n, 0))
    vec = pl.BlockSpec((1, D), lambda n: (0, 0))
    return pl.pallas_call(
        body, grid=(R // TR,), in_specs=[blk, blk, vec, vec], out_specs=[blk, blk],
        out_shape=[jax.ShapeDtypeStruct((R, D), F32), jax.ShapeDtypeStruct((R, D), BF16)],
        name=name, compiler_params=_cp(("parallel",)))(h, y, g_post, g_next)


def _rms_post_bwd(dxn, h, g, res, y, g_post, name):
    def body(d_ref, h_ref, g_ref, r_ref, y_ref, gp_ref, dh_ref, dg_ref, dy_ref, dgp_ref):
        n = pl.program_id(0)

        @pl.when(n == 0)
        def _():
            dg_ref[...] = jnp.zeros_like(dg_ref)
            dgp_ref[...] = jnp.zeros_like(dgp_ref)

        hv, dv = h_ref[...], d_ref[...]
        r = lax.rsqrt(jnp.mean(hv * hv, axis=1, keepdims=True) + NORM_EPS)
        xh = hv * r
        dxh = dv * g_ref[...]
        dh = r * (dxh - xh * jnp.mean(dxh * xh, axis=1, keepdims=True)) + r_ref[...]
        row = lax.broadcasted_iota(jnp.int32, (TR, D), 0) + n * TR
        dh = jnp.where(row >= PAD, dh, 0.0)
        dh_ref[...] = dh
        dg_ref[...] += jnp.sum(dv * xh, axis=0, keepdims=True)
        yv = y_ref[...]
        ry = lax.rsqrt(jnp.mean(yv * yv, axis=1, keepdims=True) + NORM_EPS)
        yh = yv * ry
        dyh = dh * gp_ref[...]
        dy_ref[...] = (ry * (dyh - yh * jnp.mean(dyh * yh, axis=1, keepdims=True))).astype(BF16)
        dgp_ref[...] += jnp.sum(dh * yh, axis=0, keepdims=True)

    blk = pl.BlockSpec((TR, D), lambda n: (n, 0))
    vec = pl.BlockSpec((1, D), lambda n: (0, 0))
    return pl.pallas_call(
        body, grid=(R // TR,), in_specs=[blk, blk, vec, blk, blk, vec], out_specs=[blk, vec, blk, vec],
        out_shape=[jax.ShapeDtypeStruct((R, D), F32), jax.ShapeDtypeStruct((1, D), F32),
                   jax.ShapeDtypeStruct((R, D), BF16), jax.ShapeDtypeStruct((1, D), F32)],
        name=name, compiler_params=_cp(("arbitrary",)))(dxn, h, g, res, y, g_post)


GW = 512


def _cols_spec(rows, width, where):
    def index(*g):
        r, c = where(*g)
        return r * rows, (c if isinstance(c, int) else pl.multiple_of(c, 128))
    return pl.BlockSpec((pl.Element(rows), pl.Element(width)), index)


def _gate_fwd(o, gsrc, goff, name):
    w = o.shape[1]

    def body(o_ref, g_ref, m_ref):
        gv = g_ref[...]
        m_ref[...] = (o_ref[...] * (gv * _sig(gv))).astype(BF16)

    return pl.pallas_call(
        body, grid=(R // TR, w // GW),
        in_specs=[pl.BlockSpec((TR, GW), lambda n, j: (n, j)), _cols_spec(TR, GW, lambda n, j: (n, goff + j * GW))],
        out_specs=pl.BlockSpec((TR, GW), lambda n, j: (n, j)),
        out_shape=jax.ShapeDtypeStruct((R, w), BF16), name=name,
        compiler_params=_cp(("parallel", "parallel")))(o, gsrc)


def _gate_bwd(dsrc, doff, o, gsrc, goff, name):
    w = o.shape[1]

    def body(d_ref, o_ref, g_ref, do_ref, dg_ref):
        gv = g_ref[...]
        dv = d_ref[...]
        s = _sig(gv)
        do_ref[...] = dv * (gv * s)
        dg_ref[...] = (dv * o_ref[...] * (s * (1.0 + gv * (1.0 - s)))).astype(BF16)

    blk = pl.BlockSpec((TR, GW), lambda n, j: (n, j))
    return pl.pallas_call(
        body, grid=(R // TR, w // GW),
        in_specs=[_cols_spec(TR, GW, lambda n, j: (n, doff + j * GW)), blk,
                  _cols_spec(TR, GW, lambda n, j: (n, goff + j * GW))],
        out_specs=[blk, blk],
        out_shape=[jax.ShapeDtypeStruct((R, w), F32), jax.ShapeDtypeStruct((R, w), BF16)], name=name,
        compiler_params=_cp(("parallel", "parallel")))(dsrc, o, gsrc)


def _tail(h, y, g, target):
    def body(h_ref, y_ref, g_ref, t_ref, d_ref, dy_ref, dg_ref, l_ref):
        n = pl.program_id(0)

        @pl.when(n == 0)
        def _():
            d_ref[...] = jnp.zeros_like(d_ref)
            dy_ref[...] = jnp.zeros_like(dy_ref)
            dg_ref[...] = jnp.zeros_like(dg_ref)
            l_ref[...] = jnp.zeros_like(l_ref)

        @pl.when(n > 0)
        def _():
            yv = y_ref[...]
            r = lax.rsqrt(jnp.mean(yv * yv, axis=1, keepdims=True) + NORM_EPS)
            yh = yv * r
            err = (h_ref[...] + yh * g_ref[...]) - t_ref[...]
            dv = err * (1.0 / D)
            d_ref[...] = dv
            l_ref[...] += jnp.sum(err * err, axis=0, keepdims=True)
            dyh = dv * g_ref[...]
            dy_ref[...] = (r * (dyh - yh * jnp.mean(dyh * yh, axis=1, keepdims=True))).astype(BF16)
            dg_ref[...] += jnp.sum(dv * yh, axis=0, keepdims=True)

        @pl.when(n == NB - 1)
        def _():
            tot = jnp.sum(l_ref[...], axis=1, keepdims=True) * (0.5 / D)
            l_ref[...] = jnp.broadcast_to(tot, (1, D))

    blk = pl.BlockSpec((TB, D), lambda n: (n, 0))
    vec = pl.BlockSpec((1, D), lambda n: (0, 0))
    return pl.pallas_call(
        body, grid=(NB,),
        in_specs=[blk, blk, vec, pl.BlockSpec((TB, D), lambda n: (jnp.maximum(n - 1, 0), 0))],
        out_specs=[blk, blk, vec, vec],
        out_shape=[jax.ShapeDtypeStruct((R, D), F32), jax.ShapeDtypeStruct((R, D), BF16),
                   jax.ShapeDtypeStruct((1, D), F32), jax.ShapeDtypeStruct((1, D), F32)],
        name="tail", compiler_params=_cp(("arbitrary",)))(h, y, g, target)


def _lane_row(shape):
    return lax.broadcasted_iota(jnp.int32, shape, 1), lax.broadcasted_iota(jnp.int32, shape, 0)


def _rot_half(x, lane):
    return jnp.where(lane % HD < HD // 2, pltpu.roll(x, 128 - HD // 2, 1), pltpu.roll(x, HD // 2, 1))


def _swa_blocks(n):
    return (0, jnp.maximum(n - 1, 0), n)


SWA_STACKS = ((0, 0), (0, 1), (1, 0), (1, 1))


def _swa_masks(n, lane, row):
    qpos = n * TB + (row & (TB - 1))
    kp = (n - 1) * TB + lane
    kc = n * TB + lane
    m0 = (lane >= PAD) & (qpos - lane >= TB)
    mp = (kp >= PAD) & (qpos >= kp) & (qpos - kp < TB)
    mc = (kc >= PAD) & (qpos >= kc)
    return (m0, mp, mc)


def _stack_pair(xa, xb, par):
    lane = lax.broadcasted_iota(jnp.int32, (TB, 128), 1)
    keep = (lane < HD) if par == 0 else (lane >= HD)
    return jnp.concatenate([jnp.where(keep, xa, 0.0), jnp.where(keep, xb, 0.0)], axis=0)


def _per_head(a, b):
    row = lax.broadcasted_iota(jnp.int32, (2 * TB, 1), 0)
    return jnp.where(row < TB, a, b)


def _swa_load(n, zq_ref, zkv_ref, cs_ref, sn_ref, lane):
    r0 = pl.multiple_of(n * TB, TB)
    csq, snq = cs_ref[pl.ds(r0, TB), :], sn_ref[pl.ds(r0, TB), :]
    qc = []
    for c in range(4):
        x = zq_ref[:, c * 128:(c + 1) * 128]
        qc.append((x * csq + _rot_half(x, lane) * snq) * SCALE)
    qst = [_stack_pair(qc[2 * g], qc[2 * g + 1], par).astype(BF16) for g, par in SWA_STACKS]
    kvs = []
    for b in _swa_blocks(n):
        b0 = pl.multiple_of(b * TB, TB)
        csb, snb = cs_ref[pl.ds(b0, TB), :], sn_ref[pl.ds(b0, TB), :]
        kx = zkv_ref[pl.ds(b0, TB), 0:128]
        kr = kx * csb + _rot_half(kx, lane) * snb
        vx = zkv_ref[pl.ds(b0, TB), 128:256]
        kvs.append((kr.astype(BF16), pltpu.roll(kr, HD, 1).astype(BF16),
                    vx.astype(BF16), pltpu.roll(vx, HD, 1).astype(BF16), csb, snb, b0))
    return qst, (csq, snq), kvs


def _swa_fwd(z0, cs, sn, sinks):
    def body(zq_ref, zkv_ref, cs_ref, sn_ref, sk_ref, ga_ref, o_ref, a_ref, lse_ref):
        n = pl.program_id(0)
        lane, row = _lane_row((TB, 128))
        lo = lane < HD
        masks = _swa_masks(n, *_lane_row((2 * TB, 128)))
        qst, _, kvs = _swa_load(n, zq_ref, zkv_ref, cs_ref, sn_ref, lane)
        ss = [[jnp.where(m, _dot_nt(qst[si], k if par == g else ka), NEG)
               for (k, ka, _, _, _, _, _), m in zip(kvs, masks)] for si, (g, par) in enumerate(SWA_STACKS)]
        o2, lse2 = [], []
        for si, (g, par) in enumerate(SWA_STACKS):
            sink = _per_head(sk_ref[0, 4 * g + par], sk_ref[0, 4 * g + 2 + par])
            s = ss[si]
            mx = jnp.maximum(jnp.maximum(jnp.max(s[0], axis=1, keepdims=True), jnp.max(s[1], axis=1, keepdims=True)),
                             jnp.max(s[2], axis=1, keepdims=True))
            mx = jnp.maximum(mx, sink)
            es = [jnp.exp(sb - mx) for sb in s]
            den = (jnp.sum(es[0], axis=1, keepdims=True) + jnp.sum(es[1], axis=1, keepdims=True)
                   + jnp.sum(es[2], axis=1, keepdims=True) + jnp.exp(sink - mx))
            inv = 1.0 / den
            t = jnp.zeros((2 * TB, 128), F32)
            for (_, _, v, va, _, _, _), e in zip(kvs, es):
                t = t + _dot((e * inv).astype(BF16), v if par == g else va)
            o2.append(t)
            lse2.append(mx + jnp.log(den))
        lse_t = jnp.zeros((TB, 128), F32)
        for g in range(2):
            for t in range(2):
                rows = slice(t * TB, (t + 1) * TB)
                c = 2 * g + t
                oc = jnp.where(lo, o2[2 * g][rows], o2[2 * g + 1][rows])
                o_ref[:, c * 128:(c + 1) * 128] = oc
                gv = ga_ref[:, c * 128:(c + 1) * 128]
                a_ref[:, c * 128:(c + 1) * 128] = (oc * (gv * _sig(gv))).astype(BF16)
                for par in range(2):
                    lse_t = jnp.where(lane == 4 * g + 2 * t + par, lse2[2 * g + par][rows], lse_t)
        lse_ref[...] = lse_t

    full = pl.BlockSpec((R, 128), lambda n: (0, 0))
    return dict(
        body=body,
        in_specs=[pl.BlockSpec((TB, 512), lambda n: (n, C_Q // 512)),
                  pl.BlockSpec((R, 256), lambda n: (0, C_K // 256)), full, full,
                  pl.BlockSpec(memory_space=pltpu.SMEM), _cols_spec(TB, 512, lambda n: (n, C_GA))],
        args=[z0, z0, cs, sn, sinks, z0],
        out_specs=[pl.BlockSpec((TB, 512), lambda n: (n, 0)), pl.BlockSpec((TB, 512), lambda n: (n, 0)),
                   pl.BlockSpec((TB, 128), lambda n: (n, 0))],
        out_shape=[jax.ShapeDtypeStruct((R, 512), F32), jax.ShapeDtypeStruct((R, 512), BF16),
                   jax.ShapeDtypeStruct((R, 128), F32)],
        scratch=[])


def _swa_bwd(z0, cs, sn, sinks, o, dmix, lse):
    def body(zq_ref, zkv_ref, cs_ref, sn_ref, sk_ref, ga_ref, o_ref, dm_ref, lse_ref,
             dq_ref, dga_ref, dkv_ref, dsk_ref, do_ref, acc_ref):
        n = pl.program_id(0)

        @pl.when(n == 0)
        def _():
            acc_ref[...] = jnp.zeros_like(acc_ref)
            dsk_ref[...] = jnp.zeros_like(dsk_ref)

        gv, dmv = ga_ref[...], dm_ref[...]
        sg = _sig(gv)
        dga_ref[...] = (dmv * o_ref[...] * (sg * (1.0 + gv * (1.0 - sg)))).astype(BF16)
        do_ref[...] = dmv * (gv * sg)
        lane, row = _lane_row((TB, 128))
        lo = lane < HD
        masks = _swa_masks(n, *_lane_row((2 * TB, 128)))
        qst, (csq, snq), kvs = _swa_load(n, zq_ref, zkv_ref, cs_ref, sn_ref, lane)
        lse_t = lse_ref[...]
        ss = [[jnp.where(m, _dot_nt(qst[si], k if par == g else ka), NEG)
               for (k, ka, _, _, _, _, _), m in zip(kvs, masks)] for si, (g, par) in enumerate(SWA_STACKS)]
        dobs, deltas, lses, dps = [], [], [], []
        for g, par in SWA_STACKS:
            ca, cb = slice(2 * g * 128, (2 * g + 1) * 128), slice((2 * g + 1) * 128, (2 * g + 2) * 128)
            dom = _stack_pair(do_ref[:, ca], do_ref[:, cb], par)
            deltas.append(jnp.sum(dom * jnp.concatenate([o_ref[:, ca], o_ref[:, cb]], axis=0), axis=1, keepdims=True))
            dob = dom.astype(BF16)
            dobs.append(dob)
            lses.append(jnp.concatenate(
                [jnp.sum(jnp.where(lane == 4 * g + 2 * t + par, lse_t, 0.0), axis=1, keepdims=True) for t in range(2)],
                axis=0))
            dps.append([_dot_nt(dob, v if par == g else va) for (_, _, v, va, _, _, _) in kvs])
        dk_al = [jnp.zeros((TB, 128), F32) for _ in range(3)]
        dk_mis = [jnp.zeros((TB, 128), F32) for _ in range(3)]
        dv_al = [jnp.zeros((TB, 128), F32) for _ in range(3)]
        dv_mis = [jnp.zeros((TB, 128), F32) for _ in range(3)]
        dsk_t = jnp.zeros((TB, 128), F32)
        dq2 = []
        for si, (g, par) in enumerate(SWA_STACKS):
            dqt = jnp.zeros((2 * TB, 128), F32)
            for bi, (k, ka, _, _, _, _, _) in enumerate(kvs):
                p = jnp.exp(ss[si][bi] - lses[si])
                ds = (p * (dps[si][bi] - deltas[si])).astype(BF16)
                dqt = dqt + _dot(ds, k if par == g else ka)
                dkh = _dot_tn(ds, qst[si])
                dvh = _dot_tn(p.astype(BF16), dobs[si])
                if par == g:
                    dk_al[bi] = dk_al[bi] + dkh
                    dv_al[bi] = dv_al[bi] + dvh
                else:
                    dk_mis[bi] = dk_mis[bi] + dkh
                    dv_mis[bi] = dv_mis[bi] + dvh
            dq2.append(dqt)
            sink = _per_head(sk_ref[0, 4 * g + par], sk_ref[0, 4 * g + 2 + par])
            dsk = -jnp.exp(sink - lses[si]) * deltas[si]
            for t in range(2):
                dsk_t = jnp.where(lane == 4 * g + 2 * t + par, dsk[t * TB:(t + 1) * TB], dsk_t)
        for g in range(2):
            for t in range(2):
                rows = slice(t * TB, (t + 1) * TB)
                c = 2 * g + t
                dqc = jnp.where(lo, dq2[2 * g][rows], dq2[2 * g + 1][rows]) * SCALE
                dq_ref[:, c * 128:(c + 1) * 128] = (dqc * csq + _rot_half(dqc * snq, lane)).astype(BF16)
        for bi, (_, _, _, _, csb, snb, b0) in enumerate(kvs):
            dk = dk_al[bi] + pltpu.roll(dk_mis[bi], HD, 1)
            dv = dv_al[bi] + pltpu.roll(dv_mis[bi], HD, 1)
            acc_ref[pl.ds(b0, TB), 0:128] += dk * csb + _rot_half(dk * snb, lane)
            acc_ref[pl.ds(b0, TB), 128:256] += dv
        dsk_ref[0:1, :] += jnp.sum(dsk_t, axis=0, keepdims=True)

        @pl.when(n == NB - 1)
        def _():
            dkv_ref[...] = acc_ref[...].astype(BF16)

    full = pl.BlockSpec((R, 128), lambda n: (0, 0))
    b512 = pl.BlockSpec((TB, 512), lambda n: (n, 0))
    return dict(
        body=body,
        in_specs=[pl.BlockSpec((TB, 512), lambda n: (n, C_Q // 512)),
                  pl.BlockSpec((R, 256), lambda n: (0, C_K // 256)), full, full,
                  pl.BlockSpec(memory_space=pltpu.SMEM), _cols_spec(TB, 512, lambda n: (n, C_GA)),
                  b512, b512, pl.BlockSpec((TB, 128), lambda n: (n, 0))],
        args=[z0, z0, cs, sn, sinks, z0, o, dmix, lse],
        out_specs=[b512, b512, pl.BlockSpec((R, 256), lambda n: (0, 0)), pl.BlockSpec((8, 128), lambda n: (0, 0))],
        out_shape=[jax.ShapeDtypeStruct((R, 512), BF16), jax.ShapeDtypeStruct((R, 512), BF16),
                   jax.ShapeDtypeStruct((R, 256), BF16), jax.ShapeDtypeStruct((8, 128), F32)],
        scratch=[pltpu.VMEM((TB, 512), F32), pltpu.VMEM((R, 256), F32)])


CC = 512
HALO = CONV_W - 1


def _conv_fwd(z0, conv_w, conv_b, ln_g, ln_b):
    def body(g_ref, w_ref, cb_ref, lg_ref, lb_ref, cv_ref, s_ref, ubuf):
        n = pl.program_id(0)

        @pl.when(n == 0)
        def _():
            ubuf[...] = jnp.zeros_like(ubuf)

        u = g_ref[:, 0:CC] * _sig(g_ref[:, CC:2 * CC])
        for k in range(8):
            ubuf[k, 0:TB + 8, :] = ubuf[k, TB:2 * TB + 8, :]
            ubuf[k, pl.ds(TB + 8 - k, TB), :] = u
        acc = jnp.zeros((TB, CC), F32)
        for w in range(CONV_W):
            off = TB - HALO + w
            acc = acc + ubuf[off % 8, pl.ds(off + 8 - off % 8, TB), :] * w_ref[w:w + 1, :]
        cv = acc + cb_ref[...]
        cv_ref[...] = cv
        xc = cv - jnp.mean(cv, axis=1, keepdims=True)
        rs = lax.rsqrt(jnp.mean(xc * xc, axis=1, keepdims=True) + LN_EPS)
        ln = xc * rs * lg_ref[...] + lb_ref[...]
        s_ref[...] = (ln * _sig(ln)).astype(BF16)

    vec = pl.BlockSpec((1, CC), lambda n: (0, 0))
    blk = pl.BlockSpec((TB, CC), lambda n: (n, 0))
    return dict(
        body=body,
        in_specs=[_cols_spec(TB, 2 * CC, lambda n: (n, C_GLU)),
                  pl.BlockSpec((32, CC), lambda n: (0, 0)), vec, vec, vec],
        args=[z0, conv_w, conv_b, ln_g, ln_b],
        out_specs=[blk, blk],
        out_shape=[jax.ShapeDtypeStruct((R, CC), F32), jax.ShapeDtypeStruct((R, CC), BF16)],
        scratch=[pltpu.VMEM((8, 2 * TB + 8, CC), F32)])


def _conv_bwd(ds, cv, z0, conv_w, ln_g, ln_b):
    def body(ds_ref, cv_ref, g_ref, w_ref, lg_ref, lb_ref, dglu_ref, dw_ref, dsm_ref, dbuf):
        n = pl.program_id(0)

        @pl.when(n == 0)
        def _():
            dbuf[...] = jnp.zeros_like(dbuf)
            dw_ref[...] = jnp.zeros_like(dw_ref)
            dsm_ref[...] = jnp.zeros_like(dsm_ref)

        cv = cv_ref[...]
        xc = cv - jnp.mean(cv, axis=1, keepdims=True)
        rs = lax.rsqrt(jnp.mean(xc * xc, axis=1, keepdims=True) + LN_EPS)
        xh = xc * rs
        ln = xh * lg_ref[...] + lb_ref[...]
        sg = _sig(ln)
        dln = ds_ref[...] * (sg * (1.0 + ln * (1.0 - sg)))
        dxh = dln * lg_ref[...]
        dcv = rs * (dxh - jnp.mean(dxh, axis=1, keepdims=True) - xh * jnp.mean(dxh * xh, axis=1, keepdims=True))
        dsm_ref[0:1, :] += jnp.sum(dcv, axis=0, keepdims=True)
        dsm_ref[1:2, :] += jnp.sum(dln * xh, axis=0, keepdims=True)
        dsm_ref[2:3, :] += jnp.sum(dln, axis=0, keepdims=True)
        for k in range(8):
            dbuf[k, TB:2 * TB + 8, :] = dbuf[k, 0:TB + 8, :]
            dbuf[k, pl.ds(8 - k, TB), :] = dcv
        a = g_ref[:, 0:CC]
        sb = _sig(g_ref[:, CC:2 * CC])
        u = a * sb
        du = jnp.zeros((TB, CC), F32)
        for w in range(CONV_W):
            off = HALO - w
            sh = dbuf[off % 8, pl.ds(off + 8 - off % 8, TB), :]
            du = du + sh * w_ref[w:w + 1, :]
            dw_ref[w:w + 1, :] += jnp.sum(u * sh, axis=0, keepdims=True)
        dglu_ref[:, 0:CC] = (du * sb).astype(BF16)
        dglu_ref[:, CC:2 * CC] = (du * a * sb * (1.0 - sb)).astype(BF16)

    rev = lambda n: (NB - 1 - n, 0)
    vec = pl.BlockSpec((1, CC), lambda n: (0, 0))
    blk = pl.BlockSpec((TB, CC), rev)
    return dict(
        body=body,
        in_specs=[blk, blk, _cols_spec(TB, 2 * CC, lambda n: (NB - 1 - n, C_GLU)),
                  pl.BlockSpec((32, CC), lambda n: (0, 0)), vec, vec],
        args=[ds, cv, z0, conv_w, ln_g, ln_b],
        out_specs=[pl.BlockSpec((TB, 2 * CC), rev), pl.BlockSpec((32, CC), lambda n: (0, 0)),
                   pl.BlockSpec((8, CC), lambda n: (0, 0))],
        out_shape=[jax.ShapeDtypeStruct((R, 2 * CC), BF16), jax.ShapeDtypeStruct((32, CC), F32),
                   jax.ShapeDtypeStruct((8, CC), F32)],
        scratch=[pltpu.VMEM((8, 2 * TB + 8, CC), F32)])


def _split_dot(x, t):
    hi = x.astype(BF16)
    lo = (x - hi.astype(F32)).astype(BF16)
    return _dot(hi, t) + _dot(lo, t)


def _stack_heads(x):
    lane = lax.broadcasted_iota(jnp.int32, (TB, 128), 1)
    return jnp.concatenate([jnp.where(lane < HD, x, 0.0), jnp.where(lane < HD, 0.0, x)], axis=0).astype(BF16)


def _sb_stack(qv, i):
    lane2, row2 = _lane_row((2 * TB, 128))
    qpos2 = i * TB + (row2 & (TB - 1))
    lane, row = _lane_row((TB, 128))
    return _stack_heads(qv), lane2, qpos2, (row > lane).astype(BF16)


SB_U = 3
SB_DEAD = -104.0
SB_P = 4


def _sb_fwd(q, k, v, g):
    def body(q_ref, k_ref, v_ref, g_ref, o_ref, m_ref, c_ref, n_ref):
        p, i = pl.program_id(0), pl.program_id(1)
        lane, row = _lane_row((TB, 128))
        lo = lane < HD
        slabs = [slice(s * 128, (s + 1) * 128) for s in range(SB_P)]
        q2s = []
        for sl in slabs:
            q2, lane2, qpos2, tri_gt = _sb_stack(q_ref[:, sl].astype(F32) * SCALE, i)
            q2s.append(q2)

        def cond(st):
            t, _, c2s = st
            alive = jnp.max(c2s[0])
            for c2 in c2s[1:]:
                alive = jnp.maximum(alive, jnp.max(c2))
            return jnp.logical_and(i - SB_U * t >= 0, alive > SB_DEAD)

        def step(st):
            t, accs, c2s = st
            accs, c2s = list(accs), list(c2s)
            jrs = [i - SB_U * t - u for u in range(SB_U)]
            j0s = [pl.multiple_of(jnp.maximum(jr, 0) * TB, TB) for jr in jrs]
            valids = []
            for jr in jrs:
                kpos = jr * TB + lane2
                valids.append((kpos >= PAD) & (kpos < qpos2))
            zs = [[jnp.where(valid, _dot_nt(q2s[s], k_ref[pl.ds(j0, TB), slabs[s]]), NEG)
                   for j0, valid in zip(j0s, valids)] for s in range(SB_P)]
            lbs, l1s = [], []
            for s in range(SB_P):
                lbs.append([jnp.minimum(z, 0.0) - jnp.log(1.0 + jnp.exp(-jnp.abs(z))) for z in zs[s]])
                l1s.append([lb - z for lb, z in zip(lbs[s], zs[s])])
            sfxs = [[_split_dot(l1, tri_gt) for l1 in l1s[s]] for s in range(SB_P)]
            carries = []
            for s in range(SB_P):
                cs, c2 = [], c2s[s]
                for jr, l1 in zip(jrs, l1s[s]):
                    cs.append(c2)
                    c_ref[:, slabs[s]] = jnp.where(lane == 2 * jr, c2[0:TB],
                                                   jnp.where(lane == 2 * jr + 1, c2[TB:2 * TB], c_ref[:, slabs[s]]))
                    c2 = c2 + jnp.sum(l1, axis=1, keepdims=True)
                carries.append(cs)
                c2s[s] = c2
            for s in range(SB_P):
                for j0, valid, lb, sfx, cu in zip(j0s, valids, lbs[s], sfxs[s], carries[s]):
                    a = jnp.exp(lb + sfx + cu).astype(BF16)
                    av = _dot(a, v_ref[pl.ds(j0, TB), slabs[s]])
                    accs[s] = accs[s] + jnp.where(lo, av[0:TB], av[TB:2 * TB])
            return t + 1, tuple(accs), tuple(c2s)

        c_ref[...] = jnp.zeros_like(c_ref)
        init = (jnp.int32(0), tuple(jnp.zeros((TB, 128), F32) for _ in slabs),
                tuple(jnp.zeros((2 * TB, 1), F32) for _ in slabs))
        t, accs, _ = lax.while_loop(cond, step, init)
        for sl, acc in zip(slabs, accs):
            o_ref[:, sl] = acc
            gv = g_ref[:, sl]
            m_ref[:, sl] = (acc * (gv * _sig(gv))).astype(BF16)
        n_ref[p, i] = t

    wide = SB_P * 128
    slab = pl.BlockSpec((R, wide), lambda p, i: (0, p))
    blk = pl.BlockSpec((TB, wide), lambda p, i: (i, p))
    sd = jax.ShapeDtypeStruct((R, D), F32)
    return pl.pallas_call(
        body, grid=(D // wide, NB), in_specs=[blk, slab, slab, blk],
        out_specs=[blk, blk, blk, pl.BlockSpec(memory_space=pltpu.SMEM)],
        out_shape=[sd, jax.ShapeDtypeStruct((R, D), BF16), sd, jax.ShapeDtypeStruct((D // wide, NB), jnp.int32)],
        name="sb_fwd", compiler_params=_cp(("arbitrary", "arbitrary")))(q, k, v, g)


def _sb_bwd(trips, q, k, v, car, dm, g, o):
    def body(n_ref, q_ref, k_ref, v_ref, c_ref, dm_ref, g_ref, o_ref, dq_ref, dko_ref, dvo_ref, dg_ref,
             dk_ref, dv_ref):
        p, i = pl.program_id(0), pl.program_id(1)

        @pl.when(i == 0)
        def _():
            dk_ref[...] = jnp.zeros_like(dk_ref)
            dv_ref[...] = jnp.zeros_like(dv_ref)

        lane, row = _lane_row((TB, 128))
        lo = lane < HD
        tri_lt = (row < lane).astype(BF16)
        slabs = [slice(s * 128, (s + 1) * 128) for s in range(SB_P)]
        q2s, do2s, cts = [], [], []
        for sl in slabs:
            q2, lane2, qpos2, tri_gt = _sb_stack(q_ref[:, sl].astype(F32) * SCALE, i)
            q2s.append(q2)
            gv, dmv = g_ref[:, sl], dm_ref[:, sl]
            sg = _sig(gv)
            dg_ref[:, sl] = (dmv * o_ref[:, sl] * (sg * (1.0 + gv * (1.0 - sg)))).astype(BF16)
            do2s.append(_stack_heads(dmv * (gv * sg)))
            cts.append(c_ref[:, sl])
        trips_i = n_ref[p, i]
        first = jnp.maximum(i + 1 - SB_U * trips_i, 0)

        def step(t, carry):
            dqs, g2s = carry
            dqs, g2s = list(dqs), list(g2s)
            jrs = [first + SB_U * t + u for u in range(SB_U)]
            j0s = [pl.multiple_of(jnp.minimum(jr, i) * TB, TB) for jr in jrs]
            valids = []
            for jr in jrs:
                kpos = jr * TB + lane2
                valids.append((kpos >= PAD) & (kpos < qpos2))
            ks = [[k_ref[pl.ds(j0, TB), sl] for j0 in j0s] for sl in slabs]
            zs = [[jnp.where(valid, _dot_nt(q2s[s], kj), NEG) for kj, valid in zip(ks[s], valids)] for s in range(SB_P)]
            das = [[_dot_nt(do2s[s], v_ref[pl.ds(j0, TB), slabs[s]]) for j0 in j0s] for s in range(SB_P)]
            es = [[jnp.exp(-jnp.abs(z)) for z in zs[s]] for s in range(SB_P)]
            lbs = [[jnp.minimum(z, 0.0) - jnp.log(1.0 + e) for z, e in zip(zs[s], es[s])] for s in range(SB_P)]
            l1s = [[lb - z for lb, z in zip(lbs[s], zs[s])] for s in range(SB_P)]
            sfxs = [[_split_dot(l1, tri_gt) for l1 in l1s[s]] for s in range(SB_P)]
            a_s, gmats, gpres = [], [], []
            for s in range(SB_P):
                a_l, gm_l, gp_l, g2 = [], [], [], g2s[s]
                for jr, valid, lb, sfx, da in zip(jrs, valids, lbs[s], sfxs[s], das[s]):
                    later = jnp.concatenate(
                        [jnp.sum(jnp.where(lane == 2 * jr + hh, cts[s], 0.0), axis=1, keepdims=True) for hh in range(2)],
                        axis=0)
                    a = jnp.exp(lb + sfx + later)
                    gmat = da * a
                    a_l.append(a.astype(BF16))
                    gm_l.append(gmat)
                    gp_l.append(g2)
                    g2 = g2 + jnp.sum(gmat, axis=1, keepdims=True)
                a_s.append(a_l)
                gmats.append(gm_l)
                gpres.append(gp_l)
                g2s[s] = g2
            pres = [[gp + _split_dot(gmat, tri_lt) for gp, gmat in zip(gpres[s], gmats[s])] for s in range(SB_P)]
            for s in range(SB_P):
                for j0, kj, valid, z, e, gmat, pre, a in zip(j0s, ks[s], valids, zs[s], es[s], gmats[s], pres[s], a_s[s]):
                    r = 1.0 / (1.0 + e)
                    big = z >= 0.0
                    beta = jnp.where(big, r, e * r)
                    omb = jnp.where(big, e * r, r)
                    dz = (gmat * omb - beta * pre).astype(BF16)
                    dq2 = _dot(dz, kj)
                    dqs[s] = dqs[s] + jnp.where(lo, dq2[0:TB], dq2[TB:2 * TB])
                    dk_ref[pl.ds(j0, TB), slabs[s]] += _dot_tn(dz, q2s[s])
                    dv_ref[pl.ds(j0, TB), slabs[s]] += _dot_tn(a, do2s[s])
            return tuple(dqs), tuple(g2s)

        init = (tuple(jnp.zeros((TB, 128), F32) for _ in slabs), tuple(jnp.zeros((2 * TB, 1), F32) for _ in slabs))
        dqs, _ = lax.fori_loop(0, trips_i, step, init)
        for sl, dq in zip(slabs, dqs):
            dq_ref[:, sl] = (dq * SCALE).astype(BF16)

        @pl.when(i == NB - 1)
        def _():
            dko_ref[...] = dk_ref[...].astype(BF16)
            dvo_ref[...] = dv_ref[...].astype(BF16)

    wide = SB_P * 128
    slab = pl.BlockSpec((R, wide), lambda p, i: (0, p))
    blk = pl.BlockSpec((TB, wide), lambda p, i: (i, p))
    sd = jax.ShapeDtypeStruct((R, D), BF16)
    return pl.pallas_call(
        body, grid=(D // wide, NB),
        in_specs=[pl.BlockSpec(memory_space=pltpu.SMEM), blk, slab, slab, blk, blk, blk, blk],
        out_specs=[blk, slab, slab, blk], out_shape=[sd, sd, sd, sd],
        scratch_shapes=[pltpu.VMEM((R, wide), F32), pltpu.VMEM((R, wide), F32)], name="sb_bwd",
        compiler_params=_cp(("arbitrary", "arbitrary")))(trips, q, k, v, car, dm, g, o)


def _adamw(w, parts, m, v, name):
    rows, cols = w.shape
    tr = next((t for t in (256, 176) if rows % t == 0), rows)
    nparts = len(parts)

    def body(*refs):
        w_ref = refs[0]
        p_refs = refs[1:1 + nparts]
        m_ref, v_ref, g_ref, d_ref, nm_ref, nv_ref = refs[1 + nparts:]
        g = p_refs[0][...]
        for p_ref in p_refs[1:]:
            g = g + p_ref[...]
        nm = ADAM_B1 * m_ref[...] + (1.0 - ADAM_B1) * g
        nv = ADAM_B2 * v_ref[...] + (1.0 - ADAM_B2) * (g * g)
        m_hat = nm / (1.0 - ADAM_B1 ** ADAM_STEP)
        v_hat = nv / (1.0 - ADAM_B2 ** ADAM_STEP)
        g_ref[...] = g
        d_ref[...] = -ADAM_LR * (m_hat / (jnp.sqrt(v_hat) + ADAM_EPS) + ADAM_WD * w_ref[...])
        nm_ref[...] = nm
        nv_ref[...] = nv

    blk = pl.BlockSpec((tr, cols), lambda i: (i, 0))
    sd = jax.ShapeDtypeStruct((rows, cols), F32)
    return pl.pallas_call(
        body, grid=(rows // tr,), in_specs=[blk] * (3 + nparts), out_specs=[blk] * 4, out_shape=[sd] * 4,
        name=name, compiler_params=_cp(("parallel",)))(w, *parts, m, v)


def _sum8(buf, name):
    _, rows, cols = buf.shape

    def body(b_ref, o_ref):
        acc = b_ref[0]
        for i in range(1, 8):
            acc = acc + b_ref[i]
        o_ref[...] = acc

    return pl.pallas_call(
        body, out_shape=jax.ShapeDtypeStruct((rows, cols), F32), name=name,
        compiler_params=pltpu.CompilerParams(vmem_limit_bytes=VMEM_LIMIT))(buf)


MESH = pl.DeviceIdType.MESH
ANY = pl.BlockSpec(memory_space=pl.ANY)


def _chip_peers():
    x, y = lax.axis_index("x"), lax.axis_index("y")
    return [(1 - x, y), (x, 1 - y), (1 - x, 1 - y)]


def _gather_chips(shards):
    plan = _gather_plan(shards)

    def body(*refs):
        n = len(shards)
        ins, outs, sems = refs[:n], refs[n:2 * n], refs[2 * n:]
        plan["start"](ins, outs, sems)
        plan["mid"](ins, outs, sems)
        plan["finish"](ins, outs, sems)

    n = len(shards)
    res = pl.pallas_call(
        body, in_specs=[ANY] * n, out_specs=[ANY] * n, out_shape=plan["out_shape"],
        scratch_shapes=plan["sems"], name="gather_chips")(*plan["args"])
    return plan["post"](res)


def _gather_plan(shards):
    n = len(shards)
    shards = [s.reshape((2, s.shape[0] // 2) + s.shape[1:]) for s in shards]

    def copies(kind, ins, outs, sems):
        s1, r1, s2, r2 = sems
        x, y, c = lax.axis_index("x"), lax.axis_index("y"), lax.axis_index("c")
        me = 2 * x + y
        out = []
        for j, (px, py) in enumerate(_chip_peers()):
            for a in range(n):
                k = j * n + a
                got = outs[a].at[2 * px + py].at[c]
                other = outs[a].at[2 * px + py].at[1 - c]
                src, dst, ss, rs, dev = {
                    "first": (ins[a].at[c], outs[a].at[me].at[c], s1, r1, (px, py, c)),
                    "landed": (got, got, s1, r1, (px, py, c)),
                    "passed": (got, got, s2, r2, (x, y, 1 - c)),
                    "theirs": (other, other, s2, r2, (x, y, 1 - c)),
                }[kind]
                out.append(pltpu.make_async_remote_copy(
                    src_ref=src, dst_ref=dst, send_sem=ss.at[k], recv_sem=rs.at[k], device_id=dev, device_id_type=MESH))
        return out

    def start(ins, outs, sems):
        for cp in copies("first", ins, outs, sems):
            cp.start()

    def mid(ins, outs, sems):
        for got, fwd in zip(copies("landed", ins, outs, sems), copies("passed", ins, outs, sems)):
            got.wait_recv()
            fwd.start()

    def finish(ins, outs, sems):
        for cp in copies("theirs", ins, outs, sems):
            cp.wait_recv()
        for cp in copies("first", ins, outs, sems) + copies("passed", ins, outs, sems):
            cp.wait_send()

    def post(res):
        me = 2 * lax.axis_index("x") + lax.axis_index("y")
        res = [lax.dynamic_update_index_in_dim(r, s, me, 0) for r, s in zip(res, shards)]
        return [r.reshape((N_CHIPS, 2 * r.shape[2]) + r.shape[3:]) for r in res]

    return dict(args=shards, out_shape=[jax.ShapeDtypeStruct((N_CHIPS,) + s.shape, s.dtype) for s in shards],
                sems=[pltpu.SemaphoreType.DMA((3 * n,))] * 4, start=start, mid=mid, finish=finish, post=post)


def _rows_call(name, parts, plan):
    n_in = [len(p["args"]) for p in parts]
    n_out = [len(p["out_shape"]) for p in parts]
    n_scr = [len(p["scratch"]) for p in parts]
    c_in, c_out = len(plan["args"]), len(plan["out_shape"])

    def split(refs, sizes):
        out, pos = [], 0
        for k in sizes:
            out.append(refs[pos:pos + k])
            pos += k
        return out

    def body(*refs):
        ins, outs, scr = split(refs, [sum(n_in) + c_in, sum(n_out) + c_out, sum(n_scr) + len(plan["sems"])])
        p_in, p_out, p_scr = split(ins, n_in + [c_in]), split(outs, n_out + [c_out]), split(scr, n_scr + [len(plan["sems"])])
        comm = (p_in[-1], p_out[-1], p_scr[-1])
        step = pl.program_id(0)

        @pl.when(step == 0)
        def _():
            plan["start"](*comm)

        for p, i, o, s in zip(parts, p_in, p_out, p_scr):
            p["body"](*i, *o, *s)

        @pl.when(step == NB - 2)
        def _():
            plan["mid"](*comm)

        @pl.when(step == NB - 1)
        def _():
            plan["finish"](*comm)

    flat = lambda key: [v for p in parts for v in p[key]]
    res = pl.pallas_call(
        body, grid=(NB,), in_specs=flat("in_specs") + [ANY] * c_in, out_specs=flat("out_specs") + [ANY] * c_out,
        out_shape=flat("out_shape") + plan["out_shape"], scratch_shapes=flat("scratch") + plan["sems"],
        name=name, compiler_params=_cp(("arbitrary",)))(*flat("args"), *plan["args"])
    outs = split(res, n_out + [c_out])
    return outs[:-1], outs[-1]


def _pair_exchange(grads, name):
    n = len(grads)
    hs = [g.shape[1] // 2 for g in grads]
    grads = [g.reshape((N_CHIPS, 2, h) + g.shape[2:]) for g, h in zip(grads, hs)]

    def body(*refs):
        ins, got = refs[:n], refs[n:2 * n]
        ssem, rsem = refs[2 * n:]
        x, y, c = lax.axis_index("x"), lax.axis_index("y"), lax.axis_index("c")
        sends = [pltpu.make_async_remote_copy(
            src_ref=ins[a].at[:, 1 - c], dst_ref=got[a], send_sem=ssem.at[a],
            recv_sem=rsem.at[a], device_id=(x, y, 1 - c), device_id_type=MESH) for a in range(n)]
        for cp in sends:
            cp.start()
        for cp in sends:
            cp.wait()

    half_shapes = [jax.ShapeDtypeStruct((N_CHIPS, h) + g.shape[3:], g.dtype) for g, h in zip(grads, hs)]
    got = pl.pallas_call(
        body, in_specs=[ANY] * n, out_specs=[ANY] * n, out_shape=half_shapes,
        scratch_shapes=[pltpu.SemaphoreType.DMA((n,))] * 2, name=name)(*grads)
    return grads, got


def _sum_pair(both, got, send_dtype, name):
    _, _, rows, cols = both.shape
    tr = 256 if rows % 256 == 0 else rows

    def body(c_ref, a_ref, b_ref, f_ref, s_ref):
        t = a_ref[...].astype(F32) + b_ref[...].astype(F32)
        f_ref[...] = t
        s_ref[...] = t.astype(send_dtype)

    blk = pl.BlockSpec((N_CHIPS, tr, cols), lambda i, c: (0, i, 0))
    mine = pl.BlockSpec((N_CHIPS, None, tr, cols), lambda i, c: (0, c[0], i, 0))
    return pl.pallas_call(
        body, grid_spec=pltpu.PrefetchScalarGridSpec(
            num_scalar_prefetch=1, grid=(rows // tr,), in_specs=[mine, blk], out_specs=[blk, blk]),
        out_shape=[jax.ShapeDtypeStruct(got.shape, F32), jax.ShapeDtypeStruct(got.shape, send_dtype)],
        name=name, compiler_params=_cp(("parallel",)))(lax.axis_index("c").reshape(1), both, got)


def _scatter_plan(send):
    n = len(send)

    def copies(sin, land, sems):
        ssem, rsem = sems
        c = lax.axis_index("c")
        return [pltpu.make_async_remote_copy(
            src_ref=sin[a].at[2 * px + py], dst_ref=land[a].at[j], send_sem=ssem.at[j * n + a],
            recv_sem=rsem.at[j * n + a], device_id=(px, py, c), device_id_type=MESH)
            for j, (px, py) in enumerate(_chip_peers()) for a in range(n)]

    def start(sin, land, sems):
        for cp in copies(sin, land, sems):
            cp.start()

    def finish(sin, land, sems):
        for cp in copies(sin, land, sems):
            cp.wait()

    return dict(args=list(send), out_shape=[jax.ShapeDtypeStruct((3,) + s.shape[1:], s.dtype) for s in send],
                sems=[pltpu.SemaphoreType.DMA((3 * n,))] * 2, start=start, mid=lambda *a: None, finish=finish)


def _sum_shard(keep, land, name):
    _, rows, cols = keep.shape
    tr = 256 if rows % 256 == 0 else rows

    def body(me_ref, m_ref, l_ref, o_ref):
        o_ref[...] = ((m_ref[...] + l_ref[0].astype(F32)) + l_ref[1].astype(F32)) + l_ref[2].astype(F32)

    own = pl.BlockSpec((None, tr, cols), lambda i, me: (me[0], i, 0))
    me = (2 * lax.axis_index("x") + lax.axis_index("y")).reshape(1)
    return pl.pallas_call(
        body, grid_spec=pltpu.PrefetchScalarGridSpec(
            num_scalar_prefetch=1, grid=(rows // tr,),
            in_specs=[own, pl.BlockSpec((3, tr, cols), lambda i, me: (0, i, 0))],
            out_specs=pl.BlockSpec((tr, cols), lambda i, me: (i, 0))),
        out_shape=jax.ShapeDtypeStruct((rows, cols), F32),
        name=name, compiler_params=_cp(("parallel",)))(me, keep, land)


def _join_cores(halves):
    n = len(halves)

    def body(*refs):
        ins, outs = refs[:n], refs[n:2 * n]
        ssem, rsem = refs[2 * n:]
        x, y, c = lax.axis_index("x"), lax.axis_index("y"), lax.axis_index("c")
        sends = [pltpu.make_async_remote_copy(
            src_ref=ins[a], dst_ref=outs[a].at[c], send_sem=ssem.at[a], recv_sem=rsem.at[a],
            device_id=(x, y, 1 - c), device_id_type=MESH) for a in range(n)]
        for cp in sends:
            cp.start()
        for a in range(n):
            sends[a].wait_send()
            pltpu.make_async_remote_copy(
                src_ref=ins[a], dst_ref=outs[a].at[1 - c], send_sem=ssem.at[a], recv_sem=rsem.at[a],
                device_id=(x, y, 1 - c), device_id_type=MESH).wait_recv()

    res = pl.pallas_call(
        body, in_specs=[ANY] * n, out_specs=[ANY] * n,
        out_shape=[jax.ShapeDtypeStruct((2,) + h.shape, h.dtype) for h in halves],
        scratch_shapes=[pltpu.SemaphoreType.DMA((n,))] * 2, name="join_cores")(*halves)
    c = lax.axis_index("c")
    res = [lax.dynamic_update_index_in_dim(r, h, c, 0) for r, h in zip(res, halves)]
    return [r.reshape((2 * r.shape[1],) + r.shape[2:]) for r in res]


def _gather_all(vec):
    def body(v_ref, o_ref, lsem, ssem, rsem):
        x, y, c = lax.axis_index("x"), lax.axis_index("y"), lax.axis_index("c")
        me = 4 * x + 2 * y + c
        local = pltpu.make_async_copy(v_ref, o_ref.at[me], lsem)
        local.start()
        cps = []
        for k in range(1, 8):
            px, py, pc = x ^ (k >> 2), y ^ ((k >> 1) & 1), c ^ (k & 1)
            cps.append(pltpu.make_async_remote_copy(
                src_ref=v_ref, dst_ref=o_ref.at[me], send_sem=ssem.at[k - 1], recv_sem=rsem.at[k - 1],
                device_id=(px, py, pc), device_id_type=MESH))
        for cp in cps:
            cp.start()
        for k in range(1, 8):
            px, py, pc = x ^ (k >> 2), y ^ ((k >> 1) & 1), c ^ (k & 1)
            pltpu.make_async_remote_copy(
                src_ref=v_ref, dst_ref=o_ref.at[4 * px + 2 * py + pc], send_sem=ssem.at[k - 1],
                recv_sem=rsem.at[k - 1], device_id=(px, py, pc), device_id_type=MESH).wait_recv()
        for cp in cps:
            cp.wait_send()
        local.wait()

    return pl.pallas_call(
        body, in_specs=[ANY], out_specs=ANY, out_shape=jax.ShapeDtypeStruct((8,) + vec.shape, vec.dtype),
        scratch_shapes=[pltpu.SemaphoreType.DMA, pltpu.SemaphoreType.DMA((7,)), pltpu.SemaphoreType.DMA((7,))],
        name="gather_all")(vec)


def _rope_tables():
    pos = (jnp.arange(R, dtype=jnp.int32) - PAD).astype(F32)
    half = HD // 2
    inv = ROPE_THETA ** (-jnp.arange(half, dtype=F32) / half)
    ang = pos[:, None] * inv[None, :]
    cos, sin = jnp.cos(ang), jnp.sin(ang)
    cs = jnp.tile(cos, (1, 4))
    sn = jnp.tile(jnp.concatenate([-sin, sin], axis=1), (1, 2))
    return cs, sn


def _local_step(x, target, p):
    w0t = p["ab_w_in"]
    conv_w = jnp.concatenate([p["ab_conv_w"], jnp.zeros((1, CC), F32)], axis=0)
    cs, sn = _rope_tables()

    h0 = jnp.concatenate([jnp.zeros((PAD, D), F32), p["meta_tokens"], x], axis=0)

    xn0 = _rms_fwd(h0, p["ab_pre_norm"], "rms_fwd0")
    plan = _gather_plan([p["sb_w_out"], p["ab_w_out"], p["ab_w_pw2"]])
    z0, gathered = _mm([(xn0, w0t)], F32, "in_proj0", 544, 1408, tb=True, plan=plan)
    wo1, wo0, wpw = plan["post"](gathered)
    wo1, wo0, wpw = wo1.reshape(D, D), wo0.reshape(D, D), wpw.reshape(CC, CC)
    plan = _gather_plan([p["sb_w_in"]])
    ((o0, a0, lse0), (cv0, s0)), gathered = _rows_call(
        "fwd0", [_swa_fwd(z0, cs, sn, p["ab_sinks"]),
                 _conv_fwd(z0, conv_w, p["ab_conv_b"], p["ab_conv_ln_g"], p["ab_conv_ln_b"])], plan)
    (w1,) = plan["post"](gathered)
    t0 = _mm([(s0, wpw)], F32, "pw2", 544, 512)
    c0 = _gate_fwd(t0, z0, C_GB, "gate_b_fwd")
    wo0h = wo0.reshape(2, CC, D)
    y0 = _mm([(a0, (wo0h, 0)), (c0, (wo0h, 1))], F32, "out_proj0", 544, 1024)

    h1, xn1 = _post_rms_fwd(h0, y0, p["ab_post_norm"], p["sb_pre_norm"], "post_rms_fwd")
    q1 =_mm([(xn1, (w1, 0))], BF16, "in_proj1_q", 544, 1024)
    k1 = _mm([(xn1, (w1, 1))], BF16, "in_proj1_k", 544, 1024)
    v1 = _mm([(xn1, (w1, 2))], BF16, "in_proj1_v", 544, 1024)
    g1 = _mm([(xn1, (w1, 3))], F32, "in_proj1_g", 544, 1024)
    o1, m1, car1, trips1 = _sb_fwd(q1, k1, v1, g1)
    y1 = _mm([(m1, wo1)], F32, "out_proj1", 544, 1024)

    dh2, dy1, d_sb_post, loss_row = _tail(h1, y1, p["sb_post_norm"], target)

    dm1 = _mm([(dy1, wo1)], F32, "out_proj1_dx", 544, 1024, tb=True)
    d_wo1 = _mm([(m1, dy1)], BF16, "out_proj1_dw", 512, 1024, ta=True)
    dq1, dk1, dv1, dg1 = _sb_bwd(trips1, q1, k1, v1, car1, dm1, g1, o1)
    dz1 = [dq1, dk1, dv1, dg1]
    dxn1 = _mm([(dz1[j], (w1, j)) for j in range(4)], F32, "in_proj1_dx", 544, 1024, tb=True)
    d_w1 = jnp.stack([_mm([(xn1, dz1[j])], BF16, "in_proj1_dw%d" % j, 512, 1024, ta=True) for j in range(4)])

    dh1, d_sb_pre, dy0, d_ab_post = _rms_post_bwd(dxn1, h1, p["sb_pre_norm"], dh2, y0, p["ab_post_norm"],
                                                  "rms_post_bwd")
    dmix0 = _mm([(dy0, wo0)], F32, "out_proj0_dx", 544, 1024, tb=True)
    d_wo0 = jnp.concatenate([_mm([(a0, dy0)], BF16, "out_proj0_dw_a", 512, 1024, ta=True),
                             _mm([(c0, dy0)], BF16, "out_proj0_dw_b", 512, 1024, ta=True)], axis=0)
    dt0, dgb0 = _gate_bwd(dmix0, 512, t0, z0, C_GB, "gate_b_bwd")
    ds0 = _mm([(dt0, wpw)], F32, "pw2_dx", 544, 512, tb=True)
    d_wpw = _mm([(s0, dt0)], BF16, "pw2_dw", 512, 512, ta=True)
    early = ("sb_w_in", "sb_w_out", "ab_w_out", "ab_w_pw2")
    own1, got1 = _pair_exchange([d_w1, d_wo1.reshape(N_CHIPS, 256, D), d_wo0.reshape(N_CHIPS, 256, D),
                                 d_wpw.reshape(N_CHIPS, 128, CC)], "pair_exchange1")
    pair1 = [_sum_pair(o, t, BF16, "sum_pair_" + nm) for o, t, nm in zip(own1, got1, early)]
    plan = _scatter_plan([pr[1] for pr in pair1])
    ((dglu0, d_convw, d_small), (dq0, dga0, dkv0, d_sinks)), land1 = _rows_call(
        "bwd0", [_conv_bwd(ds0, cv0, z0, conv_w, p["ab_conv_ln_g"], p["ab_conv_ln_b"]),
                 _swa_bwd(z0, cs, sn, p["ab_sinks"], o0, dmix0, lse0)], plan)
    halves1 = [_sum_shard(pr[0], la, "sum_shard_" + nm) for pr, la, nm in zip(pair1, land1, early)]
    dz0 = jnp.concatenate([dq0, dkv0, dga0, dglu0, dgb0], axis=1)
    d_w0t = _mm([(dz0, xn0)], BF16, "in_proj0_dw", 1408, 512, ta=True)
    own0, got0 = _pair_exchange([d_w0t.reshape(N_CHIPS, 704, D)], "pair_exchange0")
    keep0, send0 = _sum_pair(own0[0], got0[0], BF16, "sum_pair_ab_w_in")
    plan = _scatter_plan([send0])
    dxn0, land0 = _mm([(dz0, w0t)], F32, "in_proj0_dx", 544, 1024, plan=plan)
    half0 = _sum_shard(keep0, land0[0], "sum_shard_ab_w_in")
    dh0_first, grad_x, d_ab_pre = _rms_bwd(dxn0, h0, p["ab_pre_norm"], dh1, F32, "rms_bwd0", split=True)

    grads = {
        "meta_tokens": dh0_first[PAD:TB], "ab_pre_norm": d_ab_pre, "ab_sinks": d_sinks[0:1, 0:8],
        "ab_conv_w": d_convw[0:CONV_W], "ab_conv_b": d_small[0:1], "ab_conv_ln_g": d_small[1:2],
        "ab_conv_ln_b": d_small[2:3], "ab_post_norm": d_ab_post, "sb_pre_norm": d_sb_pre, "sb_post_norm": d_sb_post,
    }
    h_sb_in, h_sb_out, h_ab_out, h_pw2 = halves1
    return loss_row, grad_x, grads, [half0, h_ab_out, h_pw2, h_sb_in, h_sb_out]


SMALL_ROWS = 80
REP_ROWS = 32

WEIGHTS = ["meta_tokens", "ab_pre_norm", "ab_w_in", "ab_sinks", "ab_conv_w", "ab_conv_b", "ab_conv_ln_g",
           "ab_conv_ln_b", "ab_w_pw2", "ab_w_out", "ab_post_norm", "sb_pre_norm", "sb_w_in", "sb_w_out",
           "sb_post_norm"]
BIG = ["ab_w_in", "ab_w_out", "ab_w_pw2", "sb_w_in", "sb_w_out"]


def _pack_small(conv_w, meta, sb_pre, sb_post):
    pad = lambda a, rows: jnp.pad(a, ((0, rows - a.shape[0]), (0, 0)))
    return jnp.concatenate([pad(conv_w, 32), meta.reshape(32, 128), pad(sb_pre.reshape(2, 128), 8),
                            pad(sb_post.reshape(2, 128), 8)], axis=0)


def _unpack_small(s):
    return s[0:31], s[32:64].reshape(16, 256), s[64:66].reshape(1, 256), s[72:74].reshape(1, 256)


REP_LOSS = 3592


def _pack_rep(pre, post, conv_b, ln_g, ln_b, sinks, extra=None):
    flat = jnp.concatenate([pre.reshape(-1), post.reshape(-1), conv_b.reshape(-1), ln_g.reshape(-1),
                            ln_b.reshape(-1), sinks.reshape(-1)] + ([] if extra is None else [extra.reshape(-1)]))
    flat = jnp.concatenate([flat, jnp.zeros((REP_ROWS * 128 - flat.shape[0],), F32)])
    return flat.reshape(REP_ROWS, 128)


def _unpack_rep(r):
    f = r.reshape(-1)
    return (f[0:1024].reshape(1, 1024), f[1024:2048].reshape(1, 1024), f[2048:2560].reshape(1, 512),
            f[2560:3072].reshape(1, 512), f[3072:3584].reshape(1, 512), f[3584:3592].reshape(1, 8))


def _chips_to_cols(w):
    return w.transpose(1, 0, 2).reshape(w.shape[1], -1)


def kernel(x, meta_tokens, ab_pre_norm, ab_w_in, ab_sinks, ab_conv_w, ab_conv_b, ab_conv_ln_g, ab_conv_ln_b, ab_w_pw2, ab_w_out, ab_post_norm, sb_pre_norm, sb_w_in, sb_w_out, sb_post_norm, loss_target, m_meta_tokens, m_ab_pre_norm, m_ab_w_in, m_ab_sinks, m_ab_conv_w, m_ab_conv_b, m_ab_conv_ln_g, m_ab_conv_ln_b, m_ab_w_pw2, m_ab_w_out, m_ab_post_norm, m_sb_pre_norm, m_sb_w_in, m_sb_w_out, m_sb_post_norm, v_meta_tokens, v_ab_pre_norm, v_ab_w_in, v_ab_sinks, v_ab_conv_w, v_ab_conv_b, v_ab_conv_ln_g, v_ab_conv_ln_b, v_ab_w_pw2, v_ab_w_out, v_ab_post_norm, v_sb_pre_norm, v_sb_w_in, v_sb_w_out, v_sb_post_norm):
    w = dict(meta_tokens=meta_tokens, ab_pre_norm=ab_pre_norm, ab_w_in=ab_w_in, ab_sinks=ab_sinks,
             ab_conv_w=ab_conv_w, ab_conv_b=ab_conv_b, ab_conv_ln_g=ab_conv_ln_g, ab_conv_ln_b=ab_conv_ln_b,
             ab_w_pw2=ab_w_pw2, ab_w_out=ab_w_out, ab_post_norm=ab_post_norm, sb_pre_norm=sb_pre_norm,
             sb_w_in=sb_w_in, sb_w_out=sb_w_out, sb_post_norm=sb_post_norm)
    m = dict(meta_tokens=m_meta_tokens, ab_pre_norm=m_ab_pre_norm, ab_w_in=m_ab_w_in, ab_sinks=m_ab_sinks,
             ab_conv_w=m_ab_conv_w, ab_conv_b=m_ab_conv_b, ab_conv_ln_g=m_ab_conv_ln_g,
             ab_conv_ln_b=m_ab_conv_ln_b, ab_w_pw2=m_ab_w_pw2, ab_w_out=m_ab_w_out, ab_post_norm=m_ab_post_norm,
             sb_pre_norm=m_sb_pre_norm, sb_w_in=m_sb_w_in, sb_w_out=m_sb_w_out, sb_post_norm=m_sb_post_norm)
    v = dict(meta_tokens=v_meta_tokens, ab_pre_norm=v_ab_pre_norm, ab_w_in=v_ab_w_in, ab_sinks=v_ab_sinks,
             ab_conv_w=v_ab_conv_w, ab_conv_b=v_ab_conv_b, ab_conv_ln_g=v_ab_conv_ln_g,
             ab_conv_ln_b=v_ab_conv_ln_b, ab_w_pw2=v_ab_w_pw2, ab_w_out=v_ab_w_out, ab_post_norm=v_ab_post_norm,
             sb_pre_norm=v_sb_pre_norm, sb_w_in=v_sb_w_in, sb_w_out=v_sb_w_out, sb_post_norm=v_sb_post_norm)

    def small_of(d):
        return _pack_small(d["ab_conv_w"][0], d["meta_tokens"], d["sb_pre_norm"], d["sb_post_norm"])

    def rep_of(d):
        return _pack_rep(d["ab_pre_norm"], d["ab_post_norm"], d["ab_conv_b"], d["ab_conv_ln_g"], d["ab_conv_ln_b"],
                         d["ab_sinks"])

    g_in0, g_small = _gather_chips([ab_w_in[0].T.astype(BF16), small_of(w)])
    conv_w_f = _chips_to_cols(g_small[:, 0:31])
    meta_f = _chips_to_cols(g_small[:, 32:64].reshape(N_CHIPS, 16, 256))
    sb_pre_f = g_small[:, 64:66].reshape(1, D)
    sb_post_f = g_small[:, 72:74].reshape(1, D)
    full = {
        "meta_tokens": meta_f, "ab_pre_norm": ab_pre_norm, "ab_w_in": g_in0.reshape(AB_IN, D),
        "ab_sinks": ab_sinks, "ab_conv_w": conv_w_f, "ab_conv_b": ab_conv_b, "ab_conv_ln_g": ab_conv_ln_g,
        "ab_conv_ln_b": ab_conv_ln_b, "ab_w_pw2": ab_w_pw2[0].astype(BF16), "ab_w_out": ab_w_out[0].astype(BF16),
        "ab_post_norm": ab_post_norm, "sb_pre_norm": sb_pre_f, "sb_w_in": sb_w_in[0].astype(BF16),
        "sb_w_out": sb_w_out[0].astype(BF16), "sb_post_norm": sb_post_f,
    }

    loss_row, grad_x, g, halves = _local_step(x[0], loss_target[0], full)

    total = _join_cores(halves)

    rep_g = _pack_rep(g["ab_pre_norm"], g["ab_post_norm"], g["ab_conv_b"], g["ab_conv_ln_g"], g["ab_conv_ln_b"],
                      g["ab_sinks"], loss_row[0:1, 0:1])
    vec = jnp.concatenate([rep_g, jnp.pad(g["ab_conv_w"].reshape(124, 128), ((0, 4), (0, 0))),
                           g["meta_tokens"].reshape(128, 128), g["sb_pre_norm"].reshape(8, 128),
                           g["sb_post_norm"].reshape(8, 128)], axis=0)
    vec_sum = _sum8(_gather_all(vec), "sum8_small")
    rep_sum = vec_sum[0:REP_ROWS]
    loss = rep_sum.reshape(-1)[REP_LOSS]
    me = 2 * lax.axis_index("x") + lax.axis_index("y")
    small_sum = _pack_small(
        lax.dynamic_slice_in_dim(vec_sum[32:156].reshape(CONV_W, CC), me * 128, 128, axis=1),
        lax.dynamic_slice_in_dim(vec_sum[160:288].reshape(N_META, D), me * 256, 256, axis=1),
        lax.dynamic_slice_in_dim(vec_sum[288:296].reshape(1, D), me * 256, 256, axis=1),
        lax.dynamic_slice_in_dim(vec_sum[296:304].reshape(1, D), me * 256, 256, axis=1))

    out_g, out_d, out_m, out_v = {}, {}, {}, {}
    for i, k in enumerate(BIG):
        shp = w[k].shape
        if k == "ab_w_in":
            res = _adamw(w[k][0].T, [total[i]], m[k][0].T, v[k][0].T, "adamw_" + k)
            out_g[k], out_d[k], out_m[k], out_v[k] = [r.T.reshape(shp) for r in res]
            continue
        res = _adamw(w[k][0], [total[i]], m[k][0], v[k][0], "adamw_" + k)
        out_g[k], out_d[k], out_m[k], out_v[k] = [r.reshape(shp) for r in res]
    res = _adamw(small_of(w), [small_sum], small_of(m), small_of(v), "adamw_small")
    for dst, r in zip((out_g, out_d, out_m, out_v), res):
        cw, mt, pre, post = _unpack_small(r)
        dst["ab_conv_w"], dst["meta_tokens"], dst["sb_pre_norm"], dst["sb_post_norm"] = cw[None], mt, pre, post
    res = _adamw(rep_of(w), [rep_sum], rep_of(m), rep_of(v), "adamw_rep")
    for dst, r in zip((out_g, out_d, out_m, out_v), res):
        (dst["ab_pre_norm"], dst["ab_post_norm"], dst["ab_conv_b"], dst["ab_conv_ln_g"], dst["ab_conv_ln_b"],
         dst["ab_sinks"]) = _unpack_rep(r)

    return (loss, grad_x[None], *[out_g[k] for k in WEIGHTS], *[out_d[k] for k in WEIGHTS],
            *[out_m[k] for k in WEIGHTS], *[out_v[k] for k in WEIGHTS])
```

```python
import functools

import jax
import jax.numpy as jnp
from jax import lax
from jax.experimental import pallas as pl
from jax.experimental.pallas import tpu as pltpu

F32 = jnp.float32
BF16 = jnp.bfloat16

D = 1024
SEQ = 2048
N_META = 16
TB = 128
TR = 272
PAD = TB - N_META
R = SEQ + TB
NB = R // TB
HD = 64
ROPE_THETA = 10000.0
NORM_EPS = 1e-6
LN_EPS = 1e-5
NEG = -1e30
CONV_W = 31
SCALE = HD ** -0.5
N_CHIPS = 4

C_Q, C_K, C_V, C_GA, C_GLU, C_GB = 0, 512, 640, 768, 1280, 2304
AB_IN = 2816

ADAM_LR, ADAM_B1, ADAM_B2, ADAM_EPS, ADAM_WD, ADAM_STEP = 0.001, 0.9, 0.999, 1e-08, 0.01, 10

VMEM_LIMIT = 56 * 1024 * 1024


def _cp(sem):
    return pltpu.CompilerParams(dimension_semantics=sem, vmem_limit_bytes=VMEM_LIMIT)


def _sig(x):
    return 1.0 / (1.0 + jnp.exp(-x))


def _dot(a, b):
    return lax.dot_general(a, b, (((1,), (0,)), ((), ())), preferred_element_type=F32)


def _dot_nt(a, b):
    return lax.dot_general(a, b, (((1,), (1,)), ((), ())), preferred_element_type=F32)


def _dot_tn(a, b):
    return lax.dot_general(a, b, (((0,), (0,)), ((), ())), preferred_element_type=F32)


def _mm(pairs, out_dtype, name, tm, tn, ta=False, tb=False, plan=None):
    pairs = [(a, b if isinstance(b, tuple) else (b, None)) for a, b in pairs]
    a0, (b0, _) = pairs[0]
    m = a0.shape[1] if ta else a0.shape[0]
    n = b0.shape[-2] if tb else b0.shape[-1]
    npairs = len(pairs)
    dims = (((0 if ta else 1,), (1 if tb else 0,)), ((), ()))
    c_in = len(plan["args"]) if plan else 0
    c_out = len(plan["out_shape"]) if plan else 0
    steps = (m // tm) * (n // tn)

    def body(*refs):
        o_ref = refs[2 * npairs + c_in]
        if plan:
            comm = (refs[2 * npairs:2 * npairs + c_in], refs[2 * npairs + c_in + 1:2 * npairs + c_in + 1 + c_out],
                    refs[2 * npairs + c_in + 1 + c_out:])
            step = pl.program_id(0) * (n // tn) + pl.program_id(1)

            @pl.when(step == 0)
            def _():
                plan["start"](*comm)

        acc = None
        for i in range(npairs):
            t = lax.dot_general(refs[2 * i][...].astype(BF16), refs[2 * i + 1][...].astype(BF16), dims,
                                preferred_element_type=F32)
            acc = t if acc is None else acc + t
        o_ref[...] = acc.astype(out_dtype)
        if plan:
            @pl.when(step == steps - 2)
            def _():
                plan["mid"](*comm)

            @pl.when(step == steps - 1)
            def _():
                plan["finish"](*comm)

    in_specs, args = [], []
    for a, (b, sel) in pairs:
        k = a.shape[0] if ta else a.shape[1]
        in_specs.append(pl.BlockSpec((k, tm), lambda i, j: (0, i)) if ta else pl.BlockSpec((tm, k), lambda i, j: (i, 0)))
        bshape, bidx = ((tn, k), lambda i, j: (j, 0)) if tb else ((k, tn), lambda i, j: (0, j))
        if sel is None:
            in_specs.append(pl.BlockSpec(bshape, bidx))
        else:
            in_specs.append(pl.BlockSpec((None,) + bshape, functools.partial(lambda i, j, f, s: (s,) + f(i, j), f=bidx, s=sel)))
        args += [a, b]
    out_spec = pl.BlockSpec((tm, tn), lambda i, j: (i, j))
    out_shape = jax.ShapeDtypeStruct((m, n), out_dtype)
    if not plan:
        return pl.pallas_call(
            body, grid=(m // tm, n // tn), in_specs=in_specs, out_specs=out_spec, out_shape=out_shape, name=name,
            compiler_params=_cp(("parallel", "parallel")))(*args)
    assert steps >= 2
    res = pl.pallas_call(
        body, grid=(m // tm, n // tn), in_specs=in_specs + [ANY] * c_in, out_specs=[out_spec] + [ANY] * c_out,
        out_shape=[out_shape] + plan["out_shape"], scratch_shapes=plan["sems"], name=name,
        compiler_params=_cp(("arbitrary", "arbitrary")))(*args, *plan["args"])
    return res[0], res[1:]


PW_TM = 544


def _pw2_fwd(s, w, z0):
    def body(s_ref, w_ref, g_ref, t_ref, c_ref):
        t = _dot(s_ref[...], w_ref[...])
        gv = g_ref[...]
        t_ref[...] = t
        c_ref[...] = (t * (gv * _sig(gv))).astype(BF16)

    blk = pl.BlockSpec((PW_TM, CC), lambda i: (i, 0))
    return pl.pallas_call(
        body, grid=(R // PW_TM,),
        in_specs=[blk, pl.BlockSpec((CC, CC), lambda i: (0, 0)), _cols_spec(PW_TM, CC, lambda i: (i, C_GB))],
        out_specs=[blk, blk],
        out_shape=[jax.ShapeDtypeStruct((R, CC), F32), jax.ShapeDtypeStruct((R, CC), BF16)],
        name="pw2_fwd", compiler_params=_cp(("parallel",)))(s, w, z0)


def _pw2_bwd(dmix, t, z0, w, s):
    def body(d_ref, t_ref, g_ref, w_ref, s_ref, dg_ref, ds_ref, dw_ref):
        @pl.when(pl.program_id(0) == 0)
        def _():
            dw_ref[...] = jnp.zeros_like(dw_ref)

        gv, dv = g_ref[...], d_ref[...]
        sg = _sig(gv)
        dg_ref[...] = (dv * t_ref[...] * (sg * (1.0 + gv * (1.0 - sg)))).astype(BF16)
        dt = (dv * (gv * sg)).astype(BF16)
        ds_ref[...] = _dot_nt(dt, w_ref[...])
        dw_ref[...] += _dot_tn(s_ref[...], dt)

    blk = pl.BlockSpec((PW_TM, CC), lambda i: (i, 0))
    full = pl.BlockSpec((CC, CC), lambda i: (0, 0))
    return pl.pallas_call(
        body, grid=(R // PW_TM,),
        in_specs=[_cols_spec(PW_TM, CC, lambda i: (i, CC)), blk, _cols_spec(PW_TM, CC, lambda i: (i, C_GB)), full, blk],
        out_specs=[blk, blk, full],
        out_shape=[jax.ShapeDtypeStruct((R, CC), BF16), jax.ShapeDtypeStruct((R, CC), F32),
                   jax.ShapeDtypeStruct((CC, CC), F32)],
        name="pw2_bwd", compiler_params=_cp(("arbitrary",)))(dmix, t, z0, w, s)


def _rms_fwd(h, g, name):
    def body(h_ref, g_ref, o_ref):
        x = h_ref[...]
        r = lax.rsqrt(jnp.mean(x * x, axis=1, keepdims=True) + NORM_EPS)
        o_ref[...] = (x * r * g_ref[...]).astype(BF16)

    return pl.pallas_call(
        body, grid=(R // TR,),
        in_specs=[pl.BlockSpec((TR, D), lambda n: (n, 0)), pl.BlockSpec((1, D), lambda n: (0, 0))],
        out_specs=pl.BlockSpec((TR, D), lambda n: (n, 0)),
        out_shape=jax.ShapeDtypeStruct((R, D), BF16), name=name, compiler_params=_cp(("parallel",)))(h, g)


def _rms_bwd(dout, x, g, res, out_dtype, name, split=False):
    has_res = res is not None

    def body(*refs):
        if split:
            refs = list(refs)
            dx_rest_ref = refs.pop(-2)
        if has_res:
            d_ref, x_ref, g_ref, r_ref, dx_ref, dg_ref = refs
        else:
            d_ref, x_ref, g_ref, dx_ref, dg_ref = refs
        n = pl.program_id(0)
        xv = x_ref[...]
        dv = d_ref[...]
        r = lax.rsqrt(jnp.mean(xv * xv, axis=1, keepdims=True) + NORM_EPS)
        xh = xv * r
        dxh = dv * g_ref[...]
        dx = r * (dxh - xh * jnp.mean(dxh * xh, axis=1, keepdims=True))
        if has_res:
            dx = dx + r_ref[...]
        row = lax.broadcasted_iota(jnp.int32, (TB, D), 0) + n * TB
        dx = jnp.where(row >= PAD, dx, 0.0).astype(out_dtype)
        if split:
            @pl.when(n == 0)
            def _():
                dx_ref[...] = dx

            @pl.when(n > 0)
            def _():
                dx_rest_ref[...] = dx
        else:
            dx_ref[...] = dx

        @pl.when(n == 0)
        def _():
            dg_ref[...] = jnp.zeros_like(dg_ref)

        dg_ref[...] += jnp.sum(dv * xh, axis=0, keepdims=True)

    blk = pl.BlockSpec((TB, D), lambda n: (n, 0))
    vec = pl.BlockSpec((1, D), lambda n: (0, 0))
    ins = [dout, x, g] + ([res] if has_res else [])
    in_specs = [blk, blk, vec] + ([blk] if has_res else [])
    if split:
        out_specs = [pl.BlockSpec((TB, D), lambda n: (0, 0)), pl.BlockSpec((TB, D), lambda n: (jnp.maximum(n - 1, 0), 0)), vec]
        out_shape = [jax.ShapeDtypeStruct((TB, D), out_dtype), jax.ShapeDtypeStruct((SEQ, D), out_dtype),
                     jax.ShapeDtypeStruct((1, D), F32)]
    else:
        out_specs = [blk, vec]
        out_shape = [jax.ShapeDtypeStruct((R, D), out_dtype), jax.ShapeDtypeStruct((1, D), F32)]
    return pl.pallas_call(
        body, grid=(NB,), in_specs=in_specs, out_specs=out_specs, out_shape=out_shape,
        name=name, compiler_params=_cp(("arbitrary",)))(*ins)


def _post_rms_fwd(h, y, g_post, g_next, name):
    def body(h_ref, y_ref, gp_ref, gn_ref, o_ref, x_ref):
        yv = y_ref[...]
        r = lax.rsqrt(jnp.mean(yv * yv, axis=1, keepdims=True) + NORM_EPS)
        hn = h_ref[...] + yv * r * gp_ref[...]
        o_ref[...] = hn
        r2 = lax.rsqrt(jnp.mean(hn * hn, axis=1, keepdims=True) + NORM_EPS)
        x_ref[...] = (hn * r2 * gn_ref[...]).astype(BF16)

    blk = pl.BlockSpec((TR, D), lambda n: (n, 0))
    vec = pl.BlockSpec((1, D), lambda n: (0, 0))
    return pl.pallas_call(
        body, grid=(R // TR,), in_specs=[blk, blk, vec, vec], out_specs=[blk, blk],
        out_shape=[jax.ShapeDtypeStruct((R, D), F32), jax.ShapeDtypeStruct((R, D), BF16)],
        name=name, compiler_params=_cp(("parallel",)))(h, y, g_post, g_next)


def _rms_post_bwd(dxn, h, g, res, y, g_post, name):
    def body(d_ref, h_ref, g_ref, r_ref, y_ref, gp_ref, dh_ref, dg_ref, dy_ref, dgp_ref):
        n = pl.program_id(0)

        @pl.when(n == 0)
        def _():
            dg_ref[...] = jnp.zeros_like(dg_ref)
            dgp_ref[...] = jnp.zeros_like(dgp_ref)

        hv, dv = h_ref[...], d_ref[...]
        r = lax.rsqrt(jnp.mean(hv * hv, axis=1, keepdims=True) + NORM_EPS)
        xh = hv * r
        dxh = dv * g_ref[...]
        dh = r * (dxh - xh * jnp.mean(dxh * xh, axis=1, keepdims=True)) + r_ref[...]
        row = lax.broadcasted_iota(jnp.int32, (TR, D), 0) + n * TR
        dh = jnp.where(row >= PAD, dh, 0.0)
        dh_ref[...] = dh
        dg_ref[...] += jnp.sum(dv * xh, axis=0, keepdims=True)
        yv = y_ref[...]
        ry = lax.rsqrt(jnp.mean(yv * yv, axis=1, keepdims=True) + NORM_EPS)
        yh = yv * ry
        dyh = dh * gp_ref[...]
        dy_ref[...] = (ry * (dyh - yh * jnp.mean(dyh * yh, axis=1, keepdims=True))).astype(BF16)
        dgp_ref[...] += jnp.sum(dh * yh, axis=0, keepdims=True)

    blk = pl.BlockSpec((TR, D), lambda n: (n, 0))
    vec = pl.BlockSpec((1, D), lambda n: (0, 0))
    return pl.pallas_call(
        body, grid=(R // TR,), in_specs=[blk, blk, vec, blk, blk, vec], out_specs=[blk, vec, blk, vec],
        out_shape=[jax.ShapeDtypeStruct((R, D), F32), jax.ShapeDtypeStruct((1, D), F32),
                   jax.ShapeDtypeStruct((R, D), BF16), jax.ShapeDtypeStruct((1, D), F32)],
        name=name, compiler_params=_cp(("arbitrary",)))(dxn, h, g, res, y, g_post)


def _cols_spec(rows, width, where):
    def index(*g):
        r, c = where(*g)
        return r * rows, (c if isinstance(c, int) else pl.multiple_of(c, 128))
    return pl.BlockSpec((pl.Element(rows), pl.Element(width)), index)


def _tail(h, y, g, target):
    def body(h_ref, y_ref, g_ref, t_ref, d_ref, dy_ref, dg_ref, l_ref):
        n = pl.program_id(0)

        @pl.when(n == 0)
        def _():
            d_ref[...] = jnp.zeros_like(d_ref)
            dy_ref[...] = jnp.zeros_like(dy_ref)
            dg_ref[...] = jnp.zeros_like(dg_ref)
            l_ref[...] = jnp.zeros_like(l_ref)

        @pl.when(n > 0)
        def _():
            yv = y_ref[...]
            r = lax.rsqrt(jnp.mean(yv * yv, axis=1, keepdims=True) + NORM_EPS)
            yh = yv * r
            err = (h_ref[...] + yh * g_ref[...]) - t_ref[...]
            dv = err * (1.0 / D)
            d_ref[...] = dv
            l_ref[...] += jnp.sum(err * err, axis=0, keepdims=True)
            dyh = dv * g_ref[...]
            dy_ref[...] = (r * (dyh - yh * jnp.mean(dyh * yh, axis=1, keepdims=True))).astype(BF16)
            dg_ref[...] += jnp.sum(dv * yh, axis=0, keepdims=True)

        @pl.when(n == NB - 1)
        def _():
            tot = jnp.sum(l_ref[...], axis=1, keepdims=True) * (0.5 / D)
            l_ref[...] = jnp.broadcast_to(tot, (1, D))

    blk = pl.BlockSpec((TB, D), lambda n: (n, 0))
    vec = pl.BlockSpec((1, D), lambda n: (0, 0))
    return pl.pallas_call(
        body, grid=(NB,),
        in_specs=[blk, blk, vec, pl.BlockSpec((TB, D), lambda n: (jnp.maximum(n - 1, 0), 0))],
        out_specs=[blk, blk, vec, vec],
        out_shape=[jax.ShapeDtypeStruct((R, D), F32), jax.ShapeDtypeStruct((R, D), BF16),
                   jax.ShapeDtypeStruct((1, D), F32), jax.ShapeDtypeStruct((1, D), F32)],
        name="tail", compiler_params=_cp(("arbitrary",)))(h, y, g, target)


def _lane_row(shape):
    return lax.broadcasted_iota(jnp.int32, shape, 1), lax.broadcasted_iota(jnp.int32, shape, 0)


def _rot_half(x, lane):
    return jnp.where(lane % HD < HD // 2, pltpu.roll(x, 128 - HD // 2, 1), pltpu.roll(x, HD // 2, 1))


def _swa_blocks(n):
    return (0, jnp.maximum(n - 1, 0), n)


SWA_STACKS = ((0, 0), (0, 1), (1, 0), (1, 1))


def _swa_masks(n, lane, row):
    qpos = n * TB + (row & (TB - 1))
    kp = (n - 1) * TB + lane
    kc = n * TB + lane
    m0 = (lane >= PAD) & (qpos - lane >= TB)
    mp = (kp >= PAD) & (qpos >= kp) & (qpos - kp < TB)
    mc = (kc >= PAD) & (qpos >= kc)
    return (m0, mp, mc)


def _stack_pair(xa, xb, par):
    lane = lax.broadcasted_iota(jnp.int32, (TB, 128), 1)
    keep = (lane < HD) if par == 0 else (lane >= HD)
    return jnp.concatenate([jnp.where(keep, xa, 0.0), jnp.where(keep, xb, 0.0)], axis=0)


def _per_head(a, b):
    row = lax.broadcasted_iota(jnp.int32, (2 * TB, 1), 0)
    return jnp.where(row < TB, a, b)


def _swa_load(n, zq_ref, zkv_ref, cs_ref, sn_ref, lane):
    r0 = pl.multiple_of(n * TB, TB)
    csq, snq = cs_ref[pl.ds(r0, TB), :], sn_ref[pl.ds(r0, TB), :]
    qc = []
    for c in range(4):
        x = zq_ref[:, c * 128:(c + 1) * 128]
        qc.append((x * csq + _rot_half(x, lane) * snq) * SCALE)
    qst = [_stack_pair(qc[2 * g], qc[2 * g + 1], par).astype(BF16) for g, par in SWA_STACKS]
    kvs = []
    for b in _swa_blocks(n):
        b0 = pl.multiple_of(b * TB, TB)
        csb, snb = cs_ref[pl.ds(b0, TB), :], sn_ref[pl.ds(b0, TB), :]
        kx = zkv_ref[pl.ds(b0, TB), 0:128]
        kr = kx * csb + _rot_half(kx, lane) * snb
        vx = zkv_ref[pl.ds(b0, TB), 128:256]
        kvs.append((kr.astype(BF16), pltpu.roll(kr, HD, 1).astype(BF16),
                    vx.astype(BF16), pltpu.roll(vx, HD, 1).astype(BF16), csb, snb, b0))
    return qst, (csq, snq), kvs


def _swa_fwd(z0, cs, sn, sinks):
    def body(zq_ref, zkv_ref, cs_ref, sn_ref, sk_ref, ga_ref, o_ref, a_ref, lse_ref):
        n = pl.program_id(0)
        lane, row = _lane_row((TB, 128))
        lo = lane < HD
        masks = _swa_masks(n, *_lane_row((2 * TB, 128)))
        qst, _, kvs = _swa_load(n, zq_ref, zkv_ref, cs_ref, sn_ref, lane)
        ss = [[jnp.where(m, _dot_nt(qst[si], k if par == g else ka), NEG)
               for (k, ka, _, _, _, _, _), m in zip(kvs, masks)] for si, (g, par) in enumerate(SWA_STACKS)]
        o2, lse2 = [], []
        for si, (g, par) in enumerate(SWA_STACKS):
            sink = _per_head(sk_ref[0, 4 * g + par], sk_ref[0, 4 * g + 2 + par])
            s = ss[si]
            mx = jnp.maximum(jnp.maximum(jnp.max(s[0], axis=1, keepdims=True), jnp.max(s[1], axis=1, keepdims=True)),
                             jnp.max(s[2], axis=1, keepdims=True))
            mx = jnp.maximum(mx, sink)
            es = [jnp.exp(sb - mx) for sb in s]
            den = (jnp.sum(es[0], axis=1, keepdims=True) + jnp.sum(es[1], axis=1, keepdims=True)
                   + jnp.sum(es[2], axis=1, keepdims=True) + jnp.exp(sink - mx))
            inv = 1.0 / den
            t = jnp.zeros((2 * TB, 128), F32)
            for (_, _, v, va, _, _, _), e in zip(kvs, es):
                t = t + _dot((e * inv).astype(BF16), v if par == g else va)
            o2.append(t)
            lse2.append(mx + jnp.log(den))
        lse_t = jnp.zeros((TB, 128), F32)
        for g in range(2):
            for t in range(2):
                rows = slice(t * TB, (t + 1) * TB)
                c = 2 * g + t
                oc = jnp.where(lo, o2[2 * g][rows], o2[2 * g + 1][rows])
                o_ref[:, c * 128:(c + 1) * 128] = oc
                gv = ga_ref[:, c * 128:(c + 1) * 128]
                a_ref[:, c * 128:(c + 1) * 128] = (oc * (gv * _sig(gv))).astype(BF16)
                for par in range(2):
                    lse_t = jnp.where(lane == 4 * g + 2 * t + par, lse2[2 * g + par][rows], lse_t)
        lse_ref[...] = lse_t

    full = pl.BlockSpec((R, 128), lambda n: (0, 0))
    return dict(
        body=body,
        in_specs=[pl.BlockSpec((TB, 512), lambda n: (n, C_Q // 512)),
                  pl.BlockSpec((R, 256), lambda n: (0, C_K // 256)), full, full,
                  pl.BlockSpec(memory_space=pltpu.SMEM), _cols_spec(TB, 512, lambda n: (n, C_GA))],
        args=[z0, z0, cs, sn, sinks, z0],
        out_specs=[pl.BlockSpec((TB, 512), lambda n: (n, 0)), pl.BlockSpec((TB, 512), lambda n: (n, 0)),
                   pl.BlockSpec((TB, 128), lambda n: (n, 0))],
        out_shape=[jax.ShapeDtypeStruct((R, 512), F32), jax.ShapeDtypeStruct((R, 512), BF16),
                   jax.ShapeDtypeStruct((R, 128), F32)],
        scratch=[])


def _swa_bwd(z0, cs, sn, sinks, o, dmix, lse):
    def body(zq_ref, zkv_ref, cs_ref, sn_ref, sk_ref, ga_ref, o_ref, dm_ref, lse_ref,
             dq_ref, dga_ref, dkv_ref, dsk_ref, do_ref, acc_ref):
        n = pl.program_id(0)

        @pl.when(n == 0)
        def _():
            acc_ref[...] = jnp.zeros_like(acc_ref)
            dsk_ref[...] = jnp.zeros_like(dsk_ref)

        gv, dmv = ga_ref[...], dm_ref[...]
        sg = _sig(gv)
        dga_ref[...] = (dmv * o_ref[...] * (sg * (1.0 + gv * (1.0 - sg)))).astype(BF16)
        do_ref[...] = dmv * (gv * sg)
        lane, row = _lane_row((TB, 128))
        lo = lane < HD
        masks = _swa_masks(n, *_lane_row((2 * TB, 128)))
        qst, (csq, snq), kvs = _swa_load(n, zq_ref, zkv_ref, cs_ref, sn_ref, lane)
        lse_t = lse_ref[...]
        ss = [[jnp.where(m, _dot_nt(qst[si], k if par == g else ka), NEG)
               for (k, ka, _, _, _, _, _), m in zip(kvs, masks)] for si, (g, par) in enumerate(SWA_STACKS)]
        dobs, deltas, lses, dps = [], [], [], []
        for g, par in SWA_STACKS:
            ca, cb = slice(2 * g * 128, (2 * g + 1) * 128), slice((2 * g + 1) * 128, (2 * g + 2) * 128)
            dom = _stack_pair(do_ref[:, ca], do_ref[:, cb], par)
            deltas.append(jnp.sum(dom * jnp.concatenate([o_ref[:, ca], o_ref[:, cb]], axis=0), axis=1, keepdims=True))
            dob = dom.astype(BF16)
            dobs.append(dob)
            lses.append(jnp.concatenate(
                [jnp.sum(jnp.where(lane == 4 * g + 2 * t + par, lse_t, 0.0), axis=1, keepdims=True) for t in range(2)],
                axis=0))
            dps.append([_dot_nt(dob, v if par == g else va) for (_, _, v, va, _, _, _) in kvs])
        dk_al = [jnp.zeros((TB, 128), F32) for _ in range(3)]
        dk_mis = [jnp.zeros((TB, 128), F32) for _ in range(3)]
        dv_al = [jnp.zeros((TB, 128), F32) for _ in range(3)]
        dv_mis = [jnp.zeros((TB, 128), F32) for _ in range(3)]
        dsk_t = jnp.zeros((TB, 128), F32)
        dq2 = []
        for si, (g, par) in enumerate(SWA_STACKS):
            dqt = jnp.zeros((2 * TB, 128), F32)
            for bi, (k, ka, _, _, _, _, _) in enumerate(kvs):
                p = jnp.exp(ss[si][bi] - lses[si])
                ds = (p * (dps[si][bi] - deltas[si])).astype(BF16)
                dqt = dqt + _dot(ds, k if par == g else ka)
                dkh = _dot_tn(ds, qst[si])
                dvh = _dot_tn(p.astype(BF16), dobs[si])
                if par == g:
                    dk_al[bi] = dk_al[bi] + dkh
                    dv_al[bi] = dv_al[bi] + dvh
                else:
                    dk_mis[bi] = dk_mis[bi] + dkh
                    dv_mis[bi] = dv_mis[bi] + dvh
            dq2.append(dqt)
            sink = _per_head(sk_ref[0, 4 * g + par], sk_ref[0, 4 * g + 2 + par])
            dsk = -jnp.exp(sink - lses[si]) * deltas[si]
            for t in range(2):
                dsk_t = jnp.where(lane == 4 * g + 2 * t + par, dsk[t * TB:(t + 1) * TB], dsk_t)
        for g in range(2):
            for t in range(2):
                rows = slice(t * TB, (t + 1) * TB)
                c = 2 * g + t
                dqc = jnp.where(lo, dq2[2 * g][rows], dq2[2 * g + 1][rows]) * SCALE
                dq_ref[:, c * 128:(c + 1) * 128] = (dqc * csq + _rot_half(dqc * snq, lane)).astype(BF16)
        for bi, (_, _, _, _, csb, snb, b0) in enumerate(kvs):
            dk = dk_al[bi] + pltpu.roll(dk_mis[bi], HD, 1)
            dv = dv_al[bi] + pltpu.roll(dv_mis[bi], HD, 1)
            acc_ref[pl.ds(b0, TB), 0:128] += dk * csb + _rot_half(dk * snb, lane)
            acc_ref[pl.ds(b0, TB), 128:256] += dv
        dsk_ref[0:1, :] += jnp.sum(dsk_t, axis=0, keepdims=True)

        @pl.when(n == NB - 1)
        def _():
            dkv_ref[...] = acc_ref[...].astype(BF16)

    full = pl.BlockSpec((R, 128), lambda n: (0, 0))
    b512 = pl.BlockSpec((TB, 512), lambda n: (n, 0))
    return dict(
        body=body,
        in_specs=[pl.BlockSpec((TB, 512), lambda n: (n, C_Q // 512)),
                  pl.BlockSpec((R, 256), lambda n: (0, C_K // 256)), full, full,
                  pl.BlockSpec(memory_space=pltpu.SMEM), _cols_spec(TB, 512, lambda n: (n, C_GA)),
                  b512, b512, pl.BlockSpec((TB, 128), lambda n: (n, 0))],
        args=[z0, z0, cs, sn, sinks, z0, o, dmix, lse],
        out_specs=[b512, b512, pl.BlockSpec((R, 256), lambda n: (0, 0)), pl.BlockSpec((8, 128), lambda n: (0, 0))],
        out_shape=[jax.ShapeDtypeStruct((R, 512), BF16), jax.ShapeDtypeStruct((R, 512), BF16),
                   jax.ShapeDtypeStruct((R, 256), BF16), jax.ShapeDtypeStruct((8, 128), F32)],
        scratch=[pltpu.VMEM((TB, 512), F32), pltpu.VMEM((R, 256), F32)])


CC = 512
HALO = CONV_W - 1


def _conv_fwd(z0, conv_w, conv_b, ln_g, ln_b):
    def body(g_ref, w_ref, cb_ref, lg_ref, lb_ref, cv_ref, s_ref, ubuf):
        n = pl.program_id(0)

        @pl.when(n == 0)
        def _():
            ubuf[...] = jnp.zeros_like(ubuf)

        u = g_ref[:, 0:CC] * _sig(g_ref[:, CC:2 * CC])
        for k in range(8):
            ubuf[k, 0:TB + 8, :] = ubuf[k, TB:2 * TB + 8, :]
            ubuf[k, pl.ds(TB + 8 - k, TB), :] = u
        acc = jnp.zeros((TB, CC), F32)
        for w in range(CONV_W):
            off = TB - HALO + w
            acc = acc + ubuf[off % 8, pl.ds(off + 8 - off % 8, TB), :] * w_ref[w:w + 1, :]
        cv = acc + cb_ref[...]
        cv_ref[...] = cv
        xc = cv - jnp.mean(cv, axis=1, keepdims=True)
        rs = lax.rsqrt(jnp.mean(xc * xc, axis=1, keepdims=True) + LN_EPS)
        ln = xc * rs * lg_ref[...] + lb_ref[...]
        s_ref[...] = (ln * _sig(ln)).astype(BF16)

    vec = pl.BlockSpec((1, CC), lambda n: (0, 0))
    blk = pl.BlockSpec((TB, CC), lambda n: (n, 0))
    return dict(
        body=body,
        in_specs=[_cols_spec(TB, 2 * CC, lambda n: (n, C_GLU)),
                  pl.BlockSpec((32, CC), lambda n: (0, 0)), vec, vec, vec],
        args=[z0, conv_w, conv_b, ln_g, ln_b],
        out_specs=[blk, blk],
        out_shape=[jax.ShapeDtypeStruct((R, CC), F32), jax.ShapeDtypeStruct((R, CC), BF16)],
        scratch=[pltpu.VMEM((8, 2 * TB + 8, CC), F32)])


def _conv_bwd(ds, cv, z0, conv_w, ln_g, ln_b):
    def body(ds_ref, cv_ref, g_ref, w_ref, lg_ref, lb_ref, dglu_ref, dw_ref, dsm_ref, dbuf):
        n = pl.program_id(0)

        @pl.when(n == 0)
        def _():
            dbuf[...] = jnp.zeros_like(dbuf)
            dw_ref[...] = jnp.zeros_like(dw_ref)
            dsm_ref[...] = jnp.zeros_like(dsm_ref)

        cv = cv_ref[...]
        xc = cv - jnp.mean(cv, axis=1, keepdims=True)
        rs = lax.rsqrt(jnp.mean(xc * xc, axis=1, keepdims=True) + LN_EPS)
        xh = xc * rs
        ln = xh * lg_ref[...] + lb_ref[...]
        sg = _sig(ln)
        dln = ds_ref[...] * (sg * (1.0 + ln * (1.0 - sg)))
        dxh = dln * lg_ref[...]
        dcv = rs * (dxh - jnp.mean(dxh, axis=1, keepdims=True) - xh * jnp.mean(dxh * xh, axis=1, keepdims=True))
        dsm_ref[0:1, :] += jnp.sum(dcv, axis=0, keepdims=True)
        dsm_ref[1:2, :] += jnp.sum(dln * xh, axis=0, keepdims=True)
        dsm_ref[2:3, :] += jnp.sum(dln, axis=0, keepdims=True)
        for k in range(8):
            dbuf[k, TB:2 * TB + 8, :] = dbuf[k, 0:TB + 8, :]
            dbuf[k, pl.ds(8 - k, TB), :] = dcv
        a = g_ref[:, 0:CC]
        sb = _sig(g_ref[:, CC:2 * CC])
        u = a * sb
        du = jnp.zeros((TB, CC), F32)
        for w in range(CONV_W):
            off = HALO - w
            sh = dbuf[off % 8, pl.ds(off + 8 - off % 8, TB), :]
            du = du + sh * w_ref[w:w + 1, :]
            dw_ref[w:w + 1, :] += jnp.sum(u * sh, axis=0, keepdims=True)
        dglu_ref[:, 0:CC] = (du * sb).astype(BF16)
        dglu_ref[:, CC:2 * CC] = (du * a * sb * (1.0 - sb)).astype(BF16)

    rev = lambda n: (NB - 1 - n, 0)
    vec = pl.BlockSpec((1, CC), lambda n: (0, 0))
    blk = pl.BlockSpec((TB, CC), rev)
    return dict(
        body=body,
        in_specs=[blk, blk, _cols_spec(TB, 2 * CC, lambda n: (NB - 1 - n, C_GLU)),
                  pl.BlockSpec((32, CC), lambda n: (0, 0)), vec, vec],
        args=[ds, cv, z0, conv_w, ln_g, ln_b],
        out_specs=[pl.BlockSpec((TB, 2 * CC), rev), pl.BlockSpec((32, CC), lambda n: (0, 0)),
                   pl.BlockSpec((8, CC), lambda n: (0, 0))],
        out_shape=[jax.ShapeDtypeStruct((R, 2 * CC), BF16), jax.ShapeDtypeStruct((32, CC), F32),
                   jax.ShapeDtypeStruct((8, CC), F32)],
        scratch=[pltpu.VMEM((8, 2 * TB + 8, CC), F32)])


def _split_dot(x, t):
    hi = x.astype(BF16)
    lo = (x - hi.astype(F32)).astype(BF16)
    return _dot(hi, t) + _dot(lo, t)


def _stack_heads(x):
    lane = lax.broadcasted_iota(jnp.int32, (TB, 128), 1)
    return jnp.concatenate([jnp.where(lane < HD, x, 0.0), jnp.where(lane < HD, 0.0, x)], axis=0).astype(BF16)


def _sb_stack(qv, i):
    lane2, row2 = _lane_row((2 * TB, 128))
    qpos2 = i * TB + (row2 & (TB - 1))
    lane, row = _lane_row((TB, 128))
    return _stack_heads(qv), lane2, qpos2, (row > lane).astype(BF16)


SB_U = 3
SB_DEAD = -104.0
SB_P = 4


def _sb_fwd(q, k, v, g):
    def body(q_ref, k_ref, v_ref, g_ref, o_ref, m_ref, c_ref, n_ref):
        p, i = pl.program_id(0), pl.program_id(1)
        lane, row = _lane_row((TB, 128))
        lo = lane < HD
        slabs = [slice(s * 128, (s + 1) * 128) for s in range(SB_P)]
        q2s = []
        for sl in slabs:
            q2, lane2, qpos2, tri_gt = _sb_stack(q_ref[:, sl].astype(F32) * SCALE, i)
            q2s.append(q2)

        def cond(st):
            t, _, c2s = st
            alive = jnp.max(c2s[0])
            for c2 in c2s[1:]:
                alive = jnp.maximum(alive, jnp.max(c2))
            return jnp.logical_and(i - SB_U * t >= 0, alive > SB_DEAD)

        def step(st):
            t, accs, c2s = st
            accs, c2s = list(accs), list(c2s)
            jrs = [i - SB_U * t - u for u in range(SB_U)]
            j0s = [pl.multiple_of(jnp.maximum(jr, 0) * TB, TB) for jr in jrs]
            valids = []
            for jr in jrs:
                kpos = jr * TB + lane2
                valids.append((kpos >= PAD) & (kpos < qpos2))
            zs = [[jnp.where(valid, _dot_nt(q2s[s], k_ref[pl.ds(j0, TB), slabs[s]]), NEG)
                   for j0, valid in zip(j0s, valids)] for s in range(SB_P)]
            lbs, l1s = [], []
            for s in range(SB_P):
                lbs.append([jnp.minimum(z, 0.0) - jnp.log(1.0 + jnp.exp(-jnp.abs(z))) for z in zs[s]])
                l1s.append([lb - z for lb, z in zip(lbs[s], zs[s])])
            sfxs = [[_split_dot(l1, tri_gt) for l1 in l1s[s]] for s in range(SB_P)]
            carries = []
            for s in range(SB_P):
                cs, c2 = [], c2s[s]
                for jr, l1 in zip(jrs, l1s[s]):
                    cs.append(c2)
                    c_ref[:, slabs[s]] = jnp.where(lane == 2 * jr, c2[0:TB],
                                                   jnp.where(lane == 2 * jr + 1, c2[TB:2 * TB], c_ref[:, slabs[s]]))
                    c2 = c2 + jnp.sum(l1, axis=1, keepdims=True)
                carries.append(cs)
                c2s[s] = c2
            for s in range(SB_P):
                for j0, valid, lb, sfx, cu in zip(j0s, valids, lbs[s], sfxs[s], carries[s]):
                    a = jnp.exp(lb + sfx + cu).astype(BF16)
                    av = _dot(a, v_ref[pl.ds(j0, TB), slabs[s]])
                    accs[s] = accs[s] + jnp.where(lo, av[0:TB], av[TB:2 * TB])
            return t + 1, tuple(accs), tuple(c2s)

        c_ref[...] = jnp.zeros_like(c_ref)
        init = (jnp.int32(0), tuple(jnp.zeros((TB, 128), F32) for _ in slabs),
                tuple(jnp.zeros((2 * TB, 1), F32) for _ in slabs))
        t, accs, _ = lax.while_loop(cond, step, init)
        for sl, acc in zip(slabs, accs):
            o_ref[:, sl] = acc
            gv = g_ref[:, sl]
            m_ref[:, sl] = (acc * (gv * _sig(gv))).astype(BF16)
        n_ref[p, i] = t

    wide = SB_P * 128
    slab = pl.BlockSpec((R, wide), lambda p, i: (0, p))
    blk = pl.BlockSpec((TB, wide), lambda p, i: (i, p))
    sd = jax.ShapeDtypeStruct((R, D), F32)
    return pl.pallas_call(
        body, grid=(D // wide, NB), in_specs=[blk, slab, slab, blk],
        out_specs=[blk, blk, blk, pl.BlockSpec(memory_space=pltpu.SMEM)],
        out_shape=[sd, jax.ShapeDtypeStruct((R, D), BF16), sd, jax.ShapeDtypeStruct((D // wide, NB), jnp.int32)],
        name="sb_fwd", compiler_params=_cp(("arbitrary", "arbitrary")))(q, k, v, g)


def _sb_bwd(trips, q, k, v, car, dm, g, o):
    def body(n_ref, q_ref, k_ref, v_ref, c_ref, dm_ref, g_ref, o_ref, dq_ref, dko_ref, dvo_ref, dg_ref,
             dk_ref, dv_ref):
        p, i = pl.program_id(0), pl.program_id(1)

        @pl.when(i == 0)
        def _():
            dk_ref[...] = jnp.zeros_like(dk_ref)
            dv_ref[...] = jnp.zeros_like(dv_ref)

        lane, row = _lane_row((TB, 128))
        lo = lane < HD
        tri_lt = (row < lane).astype(BF16)
        slabs = [slice(s * 128, (s + 1) * 128) for s in range(SB_P)]
        q2s, do2s, cts = [], [], []
        for sl in slabs:
            q2, lane2, qpos2, tri_gt = _sb_stack(q_ref[:, sl].astype(F32) * SCALE, i)
            q2s.append(q2)
            gv, dmv = g_ref[:, sl], dm_ref[:, sl]
            sg = _sig(gv)
            dg_ref[:, sl] = (dmv * o_ref[:, sl] * (sg * (1.0 + gv * (1.0 - sg)))).astype(BF16)
            do2s.append(_stack_heads(dmv * (gv * sg)))
            cts.append(c_ref[:, sl])
        trips_i = n_ref[p, i]
        first = jnp.maximum(i + 1 - SB_U * trips_i, 0)

        def step(t, carry):
            dqs, g2s = carry
            dqs, g2s = list(dqs), list(g2s)
            jrs = [first + SB_U * t + u for u in range(SB_U)]
            j0s = [pl.multiple_of(jnp.minimum(jr, i) * TB, TB) for jr in jrs]
            valids = []
            for jr in jrs:
                kpos = jr * TB + lane2
                valids.append((kpos >= PAD) & (kpos < qpos2))
            ks = [[k_ref[pl.ds(j0, TB), sl] for j0 in j0s] for sl in slabs]
            zs = [[jnp.where(valid, _dot_nt(q2s[s], kj), NEG) for kj, valid in zip(ks[s], valids)] for s in range(SB_P)]
            das = [[_dot_nt(do2s[s], v_ref[pl.ds(j0, TB), slabs[s]]) for j0 in j0s] for s in range(SB_P)]
            es = [[jnp.exp(-jnp.abs(z)) for z in zs[s]] for s in range(SB_P)]
            lbs = [[jnp.minimum(z, 0.0) - jnp.log(1.0 + e) for z, e in zip(zs[s], es[s])] for s in range(SB_P)]
            l1s = [[lb - z for lb, z in zip(lbs[s], zs[s])] for s in range(SB_P)]
            sfxs = [[_split_dot(l1, tri_gt) for l1 in l1s[s]] for s in range(SB_P)]
            a_s, gmats, gpres = [], [], []
            for s in range(SB_P):
                a_l, gm_l, gp_l, g2 = [], [], [], g2s[s]
                for jr, valid, lb, sfx, da in zip(jrs, valids, lbs[s], sfxs[s], das[s]):
                    later = jnp.concatenate(
                        [jnp.sum(jnp.where(lane == 2 * jr + hh, cts[s], 0.0), axis=1, keepdims=True) for hh in range(2)],
                        axis=0)
                    a = jnp.exp(lb + sfx + later)
                    gmat = da * a
                    a_l.append(a.astype(BF16))
                    gm_l.append(gmat)
                    gp_l.append(g2)
                    g2 = g2 + jnp.sum(gmat, axis=1, keepdims=True)
                a_s.append(a_l)
                gmats.append(gm_l)
                gpres.append(gp_l)
                g2s[s] = g2
            pres = [[gp + _split_dot(gmat, tri_lt) for gp, gmat in zip(gpres[s], gmats[s])] for s in range(SB_P)]
            for s in range(SB_P):
                for j0, kj, valid, z, e, gmat, pre, a in zip(j0s, ks[s], valids, zs[s], es[s], gmats[s], pres[s], a_s[s]):
                    r = 1.0 / (1.0 + e)
                    big = z >= 0.0
                    beta = jnp.where(big, r, e * r)
                    omb = jnp.where(big, e * r, r)
                    dz = (gmat * omb - beta * pre).astype(BF16)
                    dq2 = _dot(dz, kj)
                    dqs[s] = dqs[s] + jnp.where(lo, dq2[0:TB], dq2[TB:2 * TB])
                    dk_ref[pl.ds(j0, TB), slabs[s]] += _dot_tn(dz, q2s[s])
                    dv_ref[pl.ds(j0, TB), slabs[s]] += _dot_tn(a, do2s[s])
            return tuple(dqs), tuple(g2s)

        init = (tuple(jnp.zeros((TB, 128), F32) for _ in slabs), tuple(jnp.zeros((2 * TB, 1), F32) for _ in slabs))
        dqs, _ = lax.fori_loop(0, trips_i, step, init)
        for sl, dq in zip(slabs, dqs):
            dq_ref[:, sl] = (dq * SCALE).astype(BF16)

        @pl.when(i == NB - 1)
        def _():
            dko_ref[...] = dk_ref[...].astype(BF16)
            dvo_ref[...] = dv_ref[...].astype(BF16)

    wide = SB_P * 128
    slab = pl.BlockSpec((R, wide), lambda p, i: (0, p))
    blk = pl.BlockSpec((TB, wide), lambda p, i: (i, p))
    sd = jax.ShapeDtypeStruct((R, D), BF16)
    return pl.pallas_call(
        body, grid=(D // wide, NB),
        in_specs=[pl.BlockSpec(memory_space=pltpu.SMEM), blk, slab, slab, blk, blk, blk, blk],
        out_specs=[blk, slab, slab, blk], out_shape=[sd, sd, sd, sd],
        scratch_shapes=[pltpu.VMEM((R, wide), F32), pltpu.VMEM((R, wide), F32)], name="sb_bwd",
        compiler_params=_cp(("arbitrary", "arbitrary")))(trips, q, k, v, car, dm, g, o)


def _adamw(w, parts, m, v, name):
    rows, cols = w.shape
    tr = next((t for t in (256, 176) if rows % t == 0), rows)
    nparts = len(parts)

    def body(*refs):
        w_ref = refs[0]
        p_refs = refs[1:1 + nparts]
        m_ref, v_ref, g_ref, d_ref, nm_ref, nv_ref = refs[1 + nparts:]
        g = p_refs[0][...]
        for p_ref in p_refs[1:]:
            g = g + p_ref[...]
        nm = ADAM_B1 * m_ref[...] + (1.0 - ADAM_B1) * g
        nv = ADAM_B2 * v_ref[...] + (1.0 - ADAM_B2) * (g * g)
        m_hat = nm / (1.0 - ADAM_B1 ** ADAM_STEP)
        v_hat = nv / (1.0 - ADAM_B2 ** ADAM_STEP)
        g_ref[...] = g
        d_ref[...] = -ADAM_LR * (m_hat / (jnp.sqrt(v_hat) + ADAM_EPS) + ADAM_WD * w_ref[...])
        nm_ref[...] = nm
        nv_ref[...] = nv

    blk = pl.BlockSpec((tr, cols), lambda i: (i, 0))
    sd = jax.ShapeDtypeStruct((rows, cols), F32)
    return pl.pallas_call(
        body, grid=(rows // tr,), in_specs=[blk] * (3 + nparts), out_specs=[blk] * 4, out_shape=[sd] * 4,
        name=name, compiler_params=_cp(("parallel",)))(w, *parts, m, v)


def _sum8(buf, name):
    _, rows, cols = buf.shape

    def body(b_ref, o_ref):
        acc = b_ref[0]
        for i in range(1, 8):
            acc = acc + b_ref[i]
        o_ref[...] = acc

    return pl.pallas_call(
        body, out_shape=jax.ShapeDtypeStruct((rows, cols), F32), name=name,
        compiler_params=pltpu.CompilerParams(vmem_limit_bytes=VMEM_LIMIT))(buf)


MESH = pl.DeviceIdType.MESH
ANY = pl.BlockSpec(memory_space=pl.ANY)


def _chip_peers():
    x, y = lax.axis_index("x"), lax.axis_index("y")
    return [(1 - x, y), (x, 1 - y), (1 - x, 1 - y)]


def _gather_chips(shards):
    plan = _gather_plan(shards)

    def body(*refs):
        n = len(shards)
        ins, outs, sems = refs[:n], refs[n:2 * n], refs[2 * n:]
        plan["start"](ins, outs, sems)
        plan["mid"](ins, outs, sems)
        plan["finish"](ins, outs, sems)

    n = len(shards)
    res = pl.pallas_call(
        body, in_specs=[ANY] * n, out_specs=[ANY] * n, out_shape=plan["out_shape"],
        scratch_shapes=plan["sems"], name="gather_chips")(*plan["args"])
    return plan["post"](res)


def _gather_plan(shards):
    n = len(shards)
    shards = [s.reshape((2, s.shape[0] // 2) + s.shape[1:]) for s in shards]

    def copies(kind, ins, outs, sems):
        s1, r1, s2, r2 = sems
        x, y, c = lax.axis_index("x"), lax.axis_index("y"), lax.axis_index("c")
        me = 2 * x + y
        out = []
        for j, (px, py) in enumerate(_chip_peers()):
            for a in range(n):
                k = j * n + a
                got = outs[a].at[2 * px + py].at[c]
                other = outs[a].at[2 * px + py].at[1 - c]
                src, dst, ss, rs, dev = {
                    "first": (ins[a].at[c], outs[a].at[me].at[c], s1, r1, (px, py, c)),
                    "landed": (got, got, s1, r1, (px, py, c)),
                    "passed": (got, got, s2, r2, (x, y, 1 - c)),
                    "theirs": (other, other, s2, r2, (x, y, 1 - c)),
                }[kind]
                out.append(pltpu.make_async_remote_copy(
                    src_ref=src, dst_ref=dst, send_sem=ss.at[k], recv_sem=rs.at[k], device_id=dev, device_id_type=MESH))
        return out

    def start(ins, outs, sems):
        for cp in copies("first", ins, outs, sems):
            cp.start()

    def mid(ins, outs, sems):
        for got, fwd in zip(copies("landed", ins, outs, sems), copies("passed", ins, outs, sems)):
            got.wait_recv()
            fwd.start()

    def finish(ins, outs, sems):
        for cp in copies("theirs", ins, outs, sems):
            cp.wait_recv()
        for cp in copies("first", ins, outs, sems) + copies("passed", ins, outs, sems):
            cp.wait_send()

    def post(res):
        me = 2 * lax.axis_index("x") + lax.axis_index("y")
        res = [lax.dynamic_update_index_in_dim(r, s, me, 0) for r, s in zip(res, shards)]
        return [r.reshape((N_CHIPS, 2 * r.shape[2]) + r.shape[3:]) for r in res]

    return dict(args=shards, out_shape=[jax.ShapeDtypeStruct((N_CHIPS,) + s.shape, s.dtype) for s in shards],
                sems=[pltpu.SemaphoreType.DMA((3 * n,))] * 4, start=start, mid=mid, finish=finish, post=post)


def _rows_call(name, parts, plan):
    n_in = [len(p["args"]) for p in parts]
    n_out = [len(p["out_shape"]) for p in parts]
    n_scr = [len(p["scratch"]) for p in parts]
    c_in, c_out = len(plan["args"]), len(plan["out_shape"])

    def split(refs, sizes):
        out, pos = [], 0
        for k in sizes:
            out.append(refs[pos:pos + k])
            pos += k
        return out

    def body(*refs):
        ins, outs, scr = split(refs, [sum(n_in) + c_in, sum(n_out) + c_out, sum(n_scr) + len(plan["sems"])])
        p_in, p_out, p_scr = split(ins, n_in + [c_in]), split(outs, n_out + [c_out]), split(scr, n_scr + [len(plan["sems"])])
        comm = (p_in[-1], p_out[-1], p_scr[-1])
        step = pl.program_id(0)

        @pl.when(step == 0)
        def _():
            plan["start"](*comm)

        for p, i, o, s in zip(parts, p_in, p_out, p_scr):
            p["body"](*i, *o, *s)

        @pl.when(step == NB - 2)
        def _():
            plan["mid"](*comm)

        @pl.when(step == NB - 1)
        def _():
            plan["finish"](*comm)

    flat = lambda key: [v for p in parts for v in p[key]]
    res = pl.pallas_call(
        body, grid=(NB,), in_specs=flat("in_specs") + [ANY] * c_in, out_specs=flat("out_specs") + [ANY] * c_out,
        out_shape=flat("out_shape") + plan["out_shape"], scratch_shapes=flat("scratch") + plan["sems"],
        name=name, compiler_params=_cp(("arbitrary",)))(*flat("args"), *plan["args"])
    outs = split(res, n_out + [c_out])
    return outs[:-1], outs[-1]


def _pair_exchange(grads, name):
    n = len(grads)
    hs = [g.shape[1] // 2 for g in grads]
    grads = [g.reshape((N_CHIPS, 2, h) + g.shape[2:]) for g, h in zip(grads, hs)]

    def body(*refs):
        ins, got = refs[:n], refs[n:2 * n]
        ssem, rsem = refs[2 * n:]
        x, y, c = lax.axis_index("x"), lax.axis_index("y"), lax.axis_index("c")
        sends = [pltpu.make_async_remote_copy(
            src_ref=ins[a].at[:, 1 - c], dst_ref=got[a], send_sem=ssem.at[a],
            recv_sem=rsem.at[a], device_id=(x, y, 1 - c), device_id_type=MESH) for a in range(n)]
        for cp in sends:
            cp.start()
        for cp in sends:
            cp.wait()

    half_shapes = [jax.ShapeDtypeStruct((N_CHIPS, h) + g.shape[3:], g.dtype) for g, h in zip(grads, hs)]
    got = pl.pallas_call(
        body, in_specs=[ANY] * n, out_specs=[ANY] * n, out_shape=half_shapes,
        scratch_shapes=[pltpu.SemaphoreType.DMA((n,))] * 2, name=name)(*grads)
    return grads, got


def _sum_pair(both, got, send_dtype, name):
    _, _, rows, cols = both.shape
    tr = 256 if rows % 256 == 0 else rows

    def body(c_ref, a_ref, b_ref, f_ref, s_ref):
        t = a_ref[...].astype(F32) + b_ref[...].astype(F32)
        f_ref[...] = t
        s_ref[...] = t.astype(send_dtype)

    blk = pl.BlockSpec((N_CHIPS, tr, cols), lambda i, c: (0, i, 0))
    mine = pl.BlockSpec((N_CHIPS, None, tr, cols), lambda i, c: (0, c[0], i, 0))
    return pl.pallas_call(
        body, grid_spec=pltpu.PrefetchScalarGridSpec(
            num_scalar_prefetch=1, grid=(rows // tr,), in_specs=[mine, blk], out_specs=[blk, blk]),
        out_shape=[jax.ShapeDtypeStruct(got.shape, F32), jax.ShapeDtypeStruct(got.shape, send_dtype)],
        name=name, compiler_params=_cp(("parallel",)))(lax.axis_index("c").reshape(1), both, got)


def _scatter_plan(send):
    n = len(send)

    def copies(sin, land, sems):
        ssem, rsem = sems
        c = lax.axis_index("c")
        return [pltpu.make_async_remote_copy(
            src_ref=sin[a].at[2 * px + py], dst_ref=land[a].at[j], send_sem=ssem.at[j * n + a],
            recv_sem=rsem.at[j * n + a], device_id=(px, py, c), device_id_type=MESH)
            for j, (px, py) in enumerate(_chip_peers()) for a in range(n)]

    def start(sin, land, sems):
        for cp in copies(sin, land, sems):
            cp.start()

    def finish(sin, land, sems):
        for cp in copies(sin, land, sems):
            cp.wait()

    return dict(args=list(send), out_shape=[jax.ShapeDtypeStruct((3,) + s.shape[1:], s.dtype) for s in send],
                sems=[pltpu.SemaphoreType.DMA((3 * n,))] * 2, start=start, mid=lambda *a: None, finish=finish)


def _sum_shard(keep, land, name):
    _, rows, cols = keep.shape
    tr = 256 if rows % 256 == 0 else rows

    def body(me_ref, m_ref, l_ref, o_ref):
        o_ref[...] = ((m_ref[...] + l_ref[0].astype(F32)) + l_ref[1].astype(F32)) + l_ref[2].astype(F32)

    own = pl.BlockSpec((None, tr, cols), lambda i, me: (me[0], i, 0))
    me = (2 * lax.axis_index("x") + lax.axis_index("y")).reshape(1)
    return pl.pallas_call(
        body, grid_spec=pltpu.PrefetchScalarGridSpec(
            num_scalar_prefetch=1, grid=(rows // tr,),
            in_specs=[own, pl.BlockSpec((3, tr, cols), lambda i, me: (0, i, 0))],
            out_specs=pl.BlockSpec((tr, cols), lambda i, me: (i, 0))),
        out_shape=jax.ShapeDtypeStruct((rows, cols), F32),
        name=name, compiler_params=_cp(("parallel",)))(me, keep, land)


def _join_cores(halves):
    n = len(halves)

    def body(*refs):
        ins, outs = refs[:n], refs[n:2 * n]
        ssem, rsem = refs[2 * n:]
        x, y, c = lax.axis_index("x"), lax.axis_index("y"), lax.axis_index("c")
        sends = [pltpu.make_async_remote_copy(
            src_ref=ins[a], dst_ref=outs[a].at[c], send_sem=ssem.at[a], recv_sem=rsem.at[a],
            device_id=(x, y, 1 - c), device_id_type=MESH) for a in range(n)]
        for cp in sends:
            cp.start()
        for a in range(n):
            sends[a].wait_send()
            pltpu.make_async_remote_copy(
                src_ref=ins[a], dst_ref=outs[a].at[1 - c], send_sem=ssem.at[a], recv_sem=rsem.at[a],
                device_id=(x, y, 1 - c), device_id_type=MESH).wait_recv()

    res = pl.pallas_call(
        body, in_specs=[ANY] * n, out_specs=[ANY] * n,
        out_shape=[jax.ShapeDtypeStruct((2,) + h.shape, h.dtype) for h in halves],
        scratch_shapes=[pltpu.SemaphoreType.DMA((n,))] * 2, name="join_cores")(*halves)
    c = lax.axis_index("c")
    res = [lax.dynamic_update_index_in_dim(r, h, c, 0) for r, h in zip(res, halves)]
    return [r.reshape((2 * r.shape[1],) + r.shape[2:]) for r in res]


def _gather_all(vec):
    def body(v_ref, o_ref, lsem, ssem, rsem):
        x, y, c = lax.axis_index("x"), lax.axis_index("y"), lax.axis_index("c")
        me = 4 * x + 2 * y + c
        local = pltpu.make_async_copy(v_ref, o_ref.at[me], lsem)
        local.start()
        cps = []
        for k in range(1, 8):
            px, py, pc = x ^ (k >> 2), y ^ ((k >> 1) & 1), c ^ (k & 1)
            cps.append(pltpu.make_async_remote_copy(
                src_ref=v_ref, dst_ref=o_ref.at[me], send_sem=ssem.at[k - 1], recv_sem=rsem.at[k - 1],
                device_id=(px, py, pc), device_id_type=MESH))
        for cp in cps:
            cp.start()
        for k in range(1, 8):
            px, py, pc = x ^ (k >> 2), y ^ ((k >> 1) & 1), c ^ (k & 1)
            pltpu.make_async_remote_copy(
                src_ref=v_ref, dst_ref=o_ref.at[4 * px + 2 * py + pc], send_sem=ssem.at[k - 1],
                recv_sem=rsem.at[k - 1], device_id=(px, py, pc), device_id_type=MESH).wait_recv()
        for cp in cps:
            cp.wait_send()
        local.wait()

    return pl.pallas_call(
        body, in_specs=[ANY], out_specs=ANY, out_shape=jax.ShapeDtypeStruct((8,) + vec.shape, vec.dtype),
        scratch_shapes=[pltpu.SemaphoreType.DMA, pltpu.SemaphoreType.DMA((7,)), pltpu.SemaphoreType.DMA((7,))],
        name="gather_all")(vec)


def _rope_tables():
    pos = (jnp.arange(R, dtype=jnp.int32) - PAD).astype(F32)
    half = HD // 2
    inv = ROPE_THETA ** (-jnp.arange(half, dtype=F32) / half)
    ang = pos[:, None] * inv[None, :]
    cos, sin = jnp.cos(ang), jnp.sin(ang)
    cs = jnp.tile(cos, (1, 4))
    sn = jnp.tile(jnp.concatenate([-sin, sin], axis=1), (1, 2))
    return cs, sn


def _local_step(x, target, p):
    w0t = p["ab_w_in"]
    conv_w = jnp.concatenate([p["ab_conv_w"], jnp.zeros((1, CC), F32)], axis=0)
    cs, sn = _rope_tables()

    h0 = jnp.concatenate([jnp.zeros((PAD, D), F32), p["meta_tokens"], x], axis=0)

    xn0 = _rms_fwd(h0, p["ab_pre_norm"], "rms_fwd0")
    plan = _gather_plan([p["sb_w_out"], p["ab_w_out"], p["ab_w_pw2"]])
    z0, gathered = _mm([(xn0, w0t)], F32, "in_proj0", 544, 1408, tb=True, plan=plan)
    wo1, wo0, wpw = plan["post"](gathered)
    wo1, wo0, wpw = wo1.reshape(D, D), wo0.reshape(D, D), wpw.reshape(CC, CC)
    plan = _gather_plan([p["sb_w_in"]])
    ((o0, a0, lse0), (cv0, s0)), gathered = _rows_call(
        "fwd0", [_swa_fwd(z0, cs, sn, p["ab_sinks"]),
                 _conv_fwd(z0, conv_w, p["ab_conv_b"], p["ab_conv_ln_g"], p["ab_conv_ln_b"])], plan)
    (w1,) = plan["post"](gathered)
    t0, c0 = _pw2_fwd(s0, wpw, z0)
    wo0h = wo0.reshape(2, CC, D)
    y0 = _mm([(a0, (wo0h, 0)), (c0, (wo0h, 1))], F32, "out_proj0", 544, 1024)

    h1, xn1 = _post_rms_fwd(h0, y0, p["ab_post_norm"], p["sb_pre_norm"], "post_rms_fwd")
    q1 =_mm([(xn1, (w1, 0))], BF16, "in_proj1_q", 544, 1024)
    k1 = _mm([(xn1, (w1, 1))], BF16, "in_proj1_k", 544, 1024)
    v1 = _mm([(xn1, (w1, 2))], BF16, "in_proj1_v", 544, 1024)
    g1 = _mm([(xn1, (w1, 3))], F32, "in_proj1_g", 544, 1024)
    o1, m1, car1, trips1 = _sb_fwd(q1, k1, v1, g1)
    y1 = _mm([(m1, wo1)], F32, "out_proj1", 544, 1024)

    dh2, dy1, d_sb_post, loss_row = _tail(h1, y1, p["sb_post_norm"], target)

    dm1 = _mm([(dy1, wo1)], F32, "out_proj1_dx", 544, 1024, tb=True)
    d_wo1 = _mm([(m1, dy1)], BF16, "out_proj1_dw", 512, 1024, ta=True)
    dq1, dk1, dv1, dg1 = _sb_bwd(trips1, q1, k1, v1, car1, dm1, g1, o1)
    dz1 = [dq1, dk1, dv1, dg1]
    dxn1 = _mm([(dz1[j], (w1, j)) for j in range(4)], F32, "in_proj1_dx", 544, 1024, tb=True)
    d_w1 = jnp.stack([_mm([(xn1, dz1[j])], BF16, "in_proj1_dw%d" % j, 512, 1024, ta=True) for j in range(4)])

    dh1, d_sb_pre, dy0, d_ab_post = _rms_post_bwd(dxn1, h1, p["sb_pre_norm"], dh2, y0, p["ab_post_norm"],
                                                  "rms_post_bwd")
    dmix0 = _mm([(dy0, wo0)], F32, "out_proj0_dx", 544, 1024, tb=True)
    d_wo0 = jnp.concatenate([_mm([(a0, dy0)], BF16, "out_proj0_dw_a", 512, 1024, ta=True),
                             _mm([(c0, dy0)], BF16, "out_proj0_dw_b", 512, 1024, ta=True)], axis=0)
    dgb0, ds0, d_wpw = _pw2_bwd(dmix0, t0, z0, wpw, s0)
    d_wpw = d_wpw.astype(BF16)
    early = ("sb_w_in", "sb_w_out", "ab_w_out", "ab_w_pw2")
    own1, got1 = _pair_exchange([d_w1, d_wo1.reshape(N_CHIPS, 256, D), d_wo0.reshape(N_CHIPS, 256, D),
                                 d_wpw.reshape(N_CHIPS, 128, CC)], "pair_exchange1")
    pair1 = [_sum_pair(o, t, BF16, "sum_pair_" + nm) for o, t, nm in zip(own1, got1, early)]
    plan = _scatter_plan([pr[1] for pr in pair1])
    ((dglu0, d_convw, d_small), (dq0, dga0, dkv0, d_sinks)), land1 = _rows_call(
        "bwd0", [_conv_bwd(ds0, cv0, z0, conv_w, p["ab_conv_ln_g"], p["ab_conv_ln_b"]),
                 _swa_bwd(z0, cs, sn, p["ab_sinks"], o0, dmix0, lse0)], plan)
    halves1 = [_sum_shard(pr[0], la, "sum_shard_" + nm) for pr, la, nm in zip(pair1, land1, early)]
    dz0 = jnp.concatenate([dq0, dkv0, dga0, dglu0, dgb0], axis=1)
    d_w0t = _mm([(dz0, xn0)], BF16, "in_proj0_dw", 1408, 512, ta=True)
    own0, got0 = _pair_exchange([d_w0t.reshape(N_CHIPS, 704, D)], "pair_exchange0")
    keep0, send0 = _sum_pair(own0[0], got0[0], BF16, "sum_pair_ab_w_in")
    plan = _scatter_plan([send0])
    dxn0, land0 = _mm([(dz0, w0t)], F32, "in_proj0_dx", 544, 1024, plan=plan)
    half0 = _sum_shard(keep0, land0[0], "sum_shard_ab_w_in")
    dh0_first, grad_x, d_ab_pre = _rms_bwd(dxn0, h0, p["ab_pre_norm"], dh1, F32, "rms_bwd0", split=True)

    grads = {
        "meta_tokens": dh0_first[PAD:TB], "ab_pre_norm": d_ab_pre, "ab_sinks": d_sinks[0:1, 0:8],
        "ab_conv_w": d_convw[0:CONV_W], "ab_conv_b": d_small[0:1], "ab_conv_ln_g": d_small[1:2],
        "ab_conv_ln_b": d_small[2:3], "ab_post_norm": d_ab_post, "sb_pre_norm": d_sb_pre, "sb_post_norm": d_sb_post,
    }
    h_sb_in, h_sb_out, h_ab_out, h_pw2 = halves1
    return loss_row, grad_x, grads, [half0, h_ab_out, h_pw2, h_sb_in, h_sb_out]


SMALL_ROWS = 80
REP_ROWS = 32

WEIGHTS = ["meta_tokens", "ab_pre_norm", "ab_w_in", "ab_sinks", "ab_conv_w", "ab_conv_b", "ab_conv_ln_g",
           "ab_conv_ln_b", "ab_w_pw2", "ab_w_out", "ab_post_norm", "sb_pre_norm", "sb_w_in", "sb_w_out",
           "sb_post_norm"]
BIG = ["ab_w_in", "ab_w_out", "ab_w_pw2", "sb_w_in", "sb_w_out"]


def _pack_small(conv_w, meta, sb_pre, sb_post):
    pad = lambda a, rows: jnp.pad(a, ((0, rows - a.shape[0]), (0, 0)))
    return jnp.concatenate([pad(conv_w, 32), meta.reshape(32, 128), pad(sb_pre.reshape(2, 128), 8),
                            pad(sb_post.reshape(2, 128), 8)], axis=0)


def _unpack_small(s):
    return s[0:31], s[32:64].reshape(16, 256), s[64:66].reshape(1, 256), s[72:74].reshape(1, 256)


REP_LOSS = 3592


def _pack_rep(pre, post, conv_b, ln_g, ln_b, sinks, extra=None):
    flat = jnp.concatenate([pre.reshape(-1), post.reshape(-1), conv_b.reshape(-1), ln_g.reshape(-1),
                            ln_b.reshape(-1), sinks.reshape(-1)] + ([] if extra is None else [extra.reshape(-1)]))
    flat = jnp.concatenate([flat, jnp.zeros((REP_ROWS * 128 - flat.shape[0],), F32)])
    return flat.reshape(REP_ROWS, 128)


def _unpack_rep(r):
    f = r.reshape(-1)
    return (f[0:1024].reshape(1, 1024), f[1024:2048].reshape(1, 1024), f[2048:2560].reshape(1, 512),
            f[2560:3072].reshape(1, 512), f[3072:3584].reshape(1, 512), f[3584:3592].reshape(1, 8))


def _chips_to_cols(w):
    return w.transpose(1, 0, 2).reshape(w.shape[1], -1)


def kernel(x, meta_tokens, ab_pre_norm, ab_w_in, ab_sinks, ab_conv_w, ab_conv_b, ab_conv_ln_g, ab_conv_ln_b, ab_w_pw2, ab_w_out, ab_post_norm, sb_pre_norm, sb_w_in, sb_w_out, sb_post_norm, loss_target, m_meta_tokens, m_ab_pre_norm, m_ab_w_in, m_ab_sinks, m_ab_conv_w, m_ab_conv_b, m_ab_conv_ln_g, m_ab_conv_ln_b, m_ab_w_pw2, m_ab_w_out, m_ab_post_norm, m_sb_pre_norm, m_sb_w_in, m_sb_w_out, m_sb_post_norm, v_meta_tokens, v_ab_pre_norm, v_ab_w_in, v_ab_sinks, v_ab_conv_w, v_ab_conv_b, v_ab_conv_ln_g, v_ab_conv_ln_b, v_ab_w_pw2, v_ab_w_out, v_ab_post_norm, v_sb_pre_norm, v_sb_w_in, v_sb_w_out, v_sb_post_norm):
    w = dict(meta_tokens=meta_tokens, ab_pre_norm=ab_pre_norm, ab_w_in=ab_w_in, ab_sinks=ab_sinks,
             ab_conv_w=ab_conv_w, ab_conv_b=ab_conv_b, ab_conv_ln_g=ab_conv_ln_g, ab_conv_ln_b=ab_conv_ln_b,
             ab_w_pw2=ab_w_pw2, ab_w_out=ab_w_out, ab_post_norm=ab_post_norm, sb_pre_norm=sb_pre_norm,
             sb_w_in=sb_w_in, sb_w_out=sb_w_out, sb_post_norm=sb_post_norm)
    m = dict(meta_tokens=m_meta_tokens, ab_pre_norm=m_ab_pre_norm, ab_w_in=m_ab_w_in, ab_sinks=m_ab_sinks,
             ab_conv_w=m_ab_conv_w, ab_conv_b=m_ab_conv_b, ab_conv_ln_g=m_ab_conv_ln_g,
             ab_conv_ln_b=m_ab_conv_ln_b, ab_w_pw2=m_ab_w_pw2, ab_w_out=m_ab_w_out, ab_post_norm=m_ab_post_norm,
             sb_pre_norm=m_sb_pre_norm, sb_w_in=m_sb_w_in, sb_w_out=m_sb_w_out, sb_post_norm=m_sb_post_norm)
    v = dict(meta_tokens=v_meta_tokens, ab_pre_norm=v_ab_pre_norm, ab_w_in=v_ab_w_in, ab_sinks=v_ab_sinks,
             ab_conv_w=v_ab_conv_w, ab_conv_b=v_ab_conv_b, ab_conv_ln_g=v_ab_conv_ln_g,
             ab_conv_ln_b=v_ab_conv_ln_b, ab_w_pw2=v_ab_w_pw2, ab_w_out=v_ab_w_out, ab_post_norm=v_ab_post_norm,
             sb_pre_norm=v_sb_pre_norm, sb_w_in=v_sb_w_in, sb_w_out=v_sb_w_out, sb_post_norm=v_sb_post_norm)

    def small_of(d):
        return _pack_small(d["ab_conv_w"][0], d["meta_tokens"], d["sb_pre_norm"], d["sb_post_norm"])

    def rep_of(d):
        return _pack_rep(d["ab_pre_norm"], d["ab_post_norm"], d["ab_conv_b"], d["ab_conv_ln_g"], d["ab_conv_ln_b"],
                         d["ab_sinks"])

    g_in0, g_small = _gather_chips([ab_w_in[0].T.astype(BF16), small_of(w)])
    conv_w_f = _chips_to_cols(g_small[:, 0:31])
    meta_f = _chips_to_cols(g_small[:, 32:64].reshape(N_CHIPS, 16, 256))
    sb_pre_f = g_small[:, 64:66].reshape(1, D)
    sb_post_f = g_small[:, 72:74].reshape(1, D)
    full = {
        "meta_tokens": meta_f, "ab_pre_norm": ab_pre_norm, "ab_w_in": g_in0.reshape(AB_IN, D),
        "ab_sinks": ab_sinks, "ab_conv_w": conv_w_f, "ab_conv_b": ab_conv_b, "ab_conv_ln_g": ab_conv_ln_g,
        "ab_conv_ln_b": ab_conv_ln_b, "ab_w_pw2": ab_w_pw2[0].astype(BF16), "ab_w_out": ab_w_out[0].astype(BF16),
        "ab_post_norm": ab_post_norm, "sb_pre_norm": sb_pre_f, "sb_w_in": sb_w_in[0].astype(BF16),
        "sb_w_out": sb_w_out[0].astype(BF16), "sb_post_norm": sb_post_f,
    }

    loss_row, grad_x, g, halves = _local_step(x[0], loss_target[0], full)

    total = _join_cores(halves)

    rep_g = _pack_rep(g["ab_pre_norm"], g["ab_post_norm"], g["ab_conv_b"], g["ab_conv_ln_g"], g["ab_conv_ln_b"],
                      g["ab_sinks"], loss_row[0:1, 0:1])
    vec = jnp.concatenate([rep_g, jnp.pad(g["ab_conv_w"].reshape(124, 128), ((0, 4), (0, 0))),
                           g["meta_tokens"].reshape(128, 128), g["sb_pre_norm"].reshape(8, 128),
                           g["sb_post_norm"].reshape(8, 128)], axis=0)
    vec_sum = _sum8(_gather_all(vec), "sum8_small")
    rep_sum = vec_sum[0:REP_ROWS]
    loss = rep_sum.reshape(-1)[REP_LOSS]
    me = 2 * lax.axis_index("x") + lax.axis_index("y")
    small_sum = _pack_small(
        lax.dynamic_slice_in_dim(vec_sum[32:156].reshape(CONV_W, CC), me * 128, 128, axis=1),
        lax.dynamic_slice_in_dim(vec_sum[160:288].reshape(N_META, D), me * 256, 256, axis=1),
        lax.dynamic_slice_in_dim(vec_sum[288:296].reshape(1, D), me * 256, 256, axis=1),
        lax.dynamic_slice_in_dim(vec_sum[296:304].reshape(1, D), me * 256, 256, axis=1))

    out_g, out_d, out_m, out_v = {}, {}, {}, {}
    for i, k in enumerate(BIG):
        shp = w[k].shape
        if k == "ab_w_in":
            res = _adamw(w[k][0].T, [total[i]], m[k][0].T, v[k][0].T, "adamw_" + k)
            out_g[k], out_d[k], out_m[k], out_v[k] = [r.T.reshape(shp) for r in res]
            continue
        res = _adamw(w[k][0], [total[i]], m[k][0], v[k][0], "adamw_" + k)
        out_g[k], out_d[k], out_m[k], out_v[k] = [r.reshape(shp) for r in res]
    res = _adamw(small_of(w), [small_sum], small_of(m), small_of(v), "adamw_small")
    for dst, r in zip((out_g, out_d, out_m, out_v), res):
        cw, mt, pre, post = _unpack_small(r)
        dst["ab_conv_w"], dst["meta_tokens"], dst["sb_pre_norm"], dst["sb_post_norm"] = cw[None], mt, pre, post
    res = _adamw(rep_of(w), [rep_sum], rep_of(m), rep_of(v), "adamw_rep")
    for dst, r in zip((out_g, out_d, out_m, out_v), res):
        (dst["ab_pre_norm"], dst["ab_post_norm"], dst["ab_conv_b"], dst["ab_conv_ln_g"], dst["ab_conv_ln_b"],
         dst["ab_sinks"]) = _unpack_rep(r)

    return (loss, grad_x[None], *[out_g[k] for k in WEIGHTS], *[out_d[k] for k in WEIGHTS],
            *[out_m[k] for k in WEIGHTS], *[out_v[k] for k in WEIGHTS])
```

```python
import functools

import jax
import jax.numpy as jnp
from jax import lax
from jax.experimental import pallas as pl
from jax.experimental.pallas import tpu as pltpu

F32 = jnp.float32
BF16 = jnp.bfloat16

D = 1024
SEQ = 2048
N_META = 16
TB = 128
TR = 272
PAD = TB - N_META
R = SEQ + TB
NB = R // TB
HD = 64
ROPE_THETA = 10000.0
NORM_EPS = 1e-6
LN_EPS = 1e-5
NEG = -1e30
CONV_W = 31
SCALE = HD ** -0.5
N_CHIPS = 4

C_Q, C_K, C_V, C_GA, C_GLU, C_GB = 0, 512, 640, 768, 1280, 2304
AB_IN = 2816

ADAM_LR, ADAM_B1, ADAM_B2, ADAM_EPS, ADAM_WD, ADAM_STEP = 0.001, 0.9, 0.999, 1e-08, 0.01, 10

VMEM_LIMIT = 56 * 1024 * 1024


def _cp(sem):
    return pltpu.CompilerParams(dimension_semantics=sem, vmem_limit_bytes=VMEM_LIMIT)


def _sig(x):
    return 1.0 / (1.0 + jnp.exp(-x))


def _dot(a, b):
    return lax.dot_general(a, b, (((1,), (0,)), ((), ())), preferred_element_type=F32)


def _dot_nt(a, b):
    return lax.dot_general(a, b, (((1,), (1,)), ((), ())), preferred_element_type=F32)


def _dot_tn(a, b):
    return lax.dot_general(a, b, (((0,), (0,)), ((), ())), preferred_element_type=F32)


def _mm(pairs, out_dtype, name, tm, tn, ta=False, tb=False, plan=None):
    pairs = [(a, b if isinstance(b, tuple) else (b, None)) for a, b in pairs]
    a0, (b0, _) = pairs[0]
    m = a0.shape[1] if ta else a0.shape[0]
    n = b0.shape[-2] if tb else b0.shape[-1]
    npairs = len(pairs)
    dims = (((0 if ta else 1,), (1 if tb else 0,)), ((), ()))
    c_in = len(plan["args"]) if plan else 0
    c_out = len(plan["out_shape"]) if plan else 0
    steps = (m // tm) * (n // tn)

    def body(*refs):
        o_ref = refs[2 * npairs + c_in]
        if plan:
            comm = (refs[2 * npairs:2 * npairs + c_in], refs[2 * npairs + c_in + 1:2 * npairs + c_in + 1 + c_out],
                    refs[2 * npairs + c_in + 1 + c_out:])
            step = pl.program_id(0) * (n // tn) + pl.program_id(1)

            @pl.when(step == 0)
            def _():
                plan["start"](*comm)

        acc = None
        for i in range(npairs):
            t = lax.dot_general(refs[2 * i][...].astype(BF16), refs[2 * i + 1][...].astype(BF16), dims,
                                preferred_element_type=F32)
            acc = t if acc is None else acc + t
        o_ref[...] = acc.astype(out_dtype)
        if plan:
            @pl.when(step == steps - 2)
            def _():
                plan["mid"](*comm)

            @pl.when(step == steps - 1)
            def _():
                plan["finish"](*comm)

    in_specs, args = [], []
    for a, (b, sel) in pairs:
        k = a.shape[0] if ta else a.shape[1]
        in_specs.append(pl.BlockSpec((k, tm), lambda i, j: (0, i)) if ta else pl.BlockSpec((tm, k), lambda i, j: (i, 0)))
        bshape, bidx = ((tn, k), lambda i, j: (j, 0)) if tb else ((k, tn), lambda i, j: (0, j))
        if sel is None:
            in_specs.append(pl.BlockSpec(bshape, bidx))
        else:
            in_specs.append(pl.BlockSpec((None,) + bshape, functools.partial(lambda i, j, f, s: (s,) + f(i, j), f=bidx, s=sel)))
        args += [a, b]
    out_spec = pl.BlockSpec((tm, tn), lambda i, j: (i, j))
    out_shape = jax.ShapeDtypeStruct((m, n), out_dtype)
    if not plan:
        return pl.pallas_call(
            body, grid=(m // tm, n // tn), in_specs=in_specs, out_specs=out_spec, out_shape=out_shape, name=name,
            compiler_params=_cp(("parallel", "parallel")))(*args)
    assert steps >= 2
    res = pl.pallas_call(
        body, grid=(m // tm, n // tn), in_specs=in_specs + [ANY] * c_in, out_specs=[out_spec] + [ANY] * c_out,
        out_shape=[out_shape] + plan["out_shape"], scratch_shapes=plan["sems"], name=name,
        compiler_params=_cp(("arbitrary", "arbitrary")))(*args, *plan["args"])
    return res[0], res[1:]


PW_TM = 544


def _pw2_fwd(s, w, z0):
    def body(s_ref, w_ref, g_ref, t_ref, c_ref):
        t = _dot(s_ref[...], w_ref[...])
        gv = g_ref[...]
        t_ref[...] = t
        c_ref[...] = (t * (gv * _sig(gv))).astype(BF16)

    blk = pl.BlockSpec((PW_TM, CC), lambda i: (i, 0))
    return pl.pallas_call(
        body, grid=(R // PW_TM,),
        in_specs=[blk, pl.BlockSpec((CC, CC), lambda i: (0, 0)), _cols_spec(PW_TM, CC, lambda i: (i, C_GB))],
        out_specs=[blk, blk],
        out_shape=[jax.ShapeDtypeStruct((R, CC), F32), jax.ShapeDtypeStruct((R, CC), BF16)],
        name="pw2_fwd", compiler_params=_cp(("parallel",)))(s, w, z0)


def _pw2_bwd(dmix, t, z0, w, s):
    def body(d_ref, t_ref, g_ref, w_ref, s_ref, dg_ref, ds_ref, dw_ref):
        @pl.when(pl.program_id(0) == 0)
        def _():
            dw_ref[...] = jnp.zeros_like(dw_ref)

        gv, dv = g_ref[...], d_ref[...]
        sg = _sig(gv)
        dg_ref[...] = (dv * t_ref[...] * (sg * (1.0 + gv * (1.0 - sg)))).astype(BF16)
        dt = (dv * (gv * sg)).astype(BF16)
        ds_ref[...] = _dot_nt(dt, w_ref[...])
        dw_ref[...] += _dot_tn(s_ref[...], dt)

    blk = pl.BlockSpec((PW_TM, CC), lambda i: (i, 0))
    full = pl.BlockSpec((CC, CC), lambda i: (0, 0))
    return pl.pallas_call(
        body, grid=(R // PW_TM,),
        in_specs=[_cols_spec(PW_TM, CC, lambda i: (i, CC)), blk, _cols_spec(PW_TM, CC, lambda i: (i, C_GB)), full, blk],
        out_specs=[blk, blk, full],
        out_shape=[jax.ShapeDtypeStruct((R, CC), BF16), jax.ShapeDtypeStruct((R, CC), F32),
                   jax.ShapeDtypeStruct((CC, CC), F32)],
        name="pw2_bwd", compiler_params=_cp(("arbitrary",)))(dmix, t, z0, w, s)


def _rms_fwd(h, g, name):
    def body(h_ref, g_ref, o_ref):
        x = h_ref[...]
        r = lax.rsqrt(jnp.mean(x * x, axis=1, keepdims=True) + NORM_EPS)
        o_ref[...] = (x * r * g_ref[...]).astype(BF16)

    return pl.pallas_call(
        body, grid=(R // TR,),
        in_specs=[pl.BlockSpec((TR, D), lambda n: (n, 0)), pl.BlockSpec((1, D), lambda n: (0, 0))],
        out_specs=pl.BlockSpec((TR, D), lambda n: (n, 0)),
        out_shape=jax.ShapeDtypeStruct((R, D), BF16), name=name, compiler_params=_cp(("parallel",)))(h, g)


def _rms_bwd(dout, x, g, res, out_dtype, name, split=False):
    has_res = res is not None

    def body(*refs):
        if split:
            refs = list(refs)
            dx_rest_ref = refs.pop(-2)
        if has_res:
            d_ref, x_ref, g_ref, r_ref, dx_ref, dg_ref = refs
        else:
            d_ref, x_ref, g_ref, dx_ref, dg_ref = refs
        n = pl.program_id(0)
        xv = x_ref[...]
        dv = d_ref[...]
        r = lax.rsqrt(jnp.mean(xv * xv, axis=1, keepdims=True) + NORM_EPS)
        xh = xv * r
        dxh = dv * g_ref[...]
        dx = r * (dxh - xh * jnp.mean(dxh * xh, axis=1, keepdims=True))
        if has_res:
            dx = dx + r_ref[...]
        row = lax.broadcasted_iota(jnp.int32, (TB, D), 0) + n * TB
        dx = jnp.where(row >= PAD, dx, 0.0).astype(out_dtype)
        if split:
            @pl.when(n == 0)
            def _():
                dx_ref[...] = dx

            @pl.when(n > 0)
            def _():
                dx_rest_ref[...] = dx
        else:
            dx_ref[...] = dx

        @pl.when(n == 0)
        def _():
            dg_ref[...] = jnp.zeros_like(dg_ref)

        dg_ref[...] += jnp.sum(dv * xh, axis=0, keepdims=True)

    blk = pl.BlockSpec((TB, D), lambda n: (n, 0))
    vec = pl.BlockSpec((1, D), lambda n: (0, 0))
    ins = [dout, x, g] + ([res] if has_res else [])
    in_specs = [blk, blk, vec] + ([blk] if has_res else [])
    if split:
        out_specs = [pl.BlockSpec((TB, D), lambda n: (0, 0)), pl.BlockSpec((TB, D), lambda n: (jnp.maximum(n - 1, 0), 0)), vec]
        out_shape = [jax.ShapeDtypeStruct((TB, D), out_dtype), jax.ShapeDtypeStruct((SEQ, D), out_dtype),
                     jax.ShapeDtypeStruct((1, D), F32)]
    else:
        out_specs = [blk, vec]
        out_shape = [jax.ShapeDtypeStruct((R, D), out_dtype), jax.ShapeDtypeStruct((1, D), F32)]
    return pl.pallas_call(
        body, grid=(NB,), in_specs=in_specs, out_specs=out_specs, out_shape=out_shape,
        name=name, compiler_params=_cp(("arbitrary",)))(*ins)


def _post_rms_fwd(xs, w, h, g_post, g_next, name):
    nx = len(xs)

    def body(*refs):
        x_refs, (w_ref, h_ref, gp_ref, gn_ref, y_ref, o_ref, xn_ref) = refs[:nx], refs[nx:]
        yv = _dot(x_refs[0][...], w_ref[0])
        for j in range(1, nx):
            yv = yv + _dot(x_refs[j][...], w_ref[j])
        y_ref[...] = yv
        r = lax.rsqrt(jnp.mean(yv * yv, axis=1, keepdims=True) + NORM_EPS)
        hn = h_ref[...] + yv * r * gp_ref[...]
        o_ref[...] = hn
        r2 = lax.rsqrt(jnp.mean(hn * hn, axis=1, keepdims=True) + NORM_EPS)
        xn_ref[...] = (hn * r2 * gn_ref[...]).astype(BF16)

    blk = pl.BlockSpec((TR, D), lambda n: (n, 0))
    vec = pl.BlockSpec((1, D), lambda n: (0, 0))
    xblk = [pl.BlockSpec((TR, x.shape[1]), lambda n: (n, 0)) for x in xs]
    return pl.pallas_call(
        body, grid=(R // TR,), in_specs=xblk + [pl.BlockSpec(w.shape, lambda n: (0, 0, 0)), blk, vec, vec],
        out_specs=[blk, blk, blk],
        out_shape=[jax.ShapeDtypeStruct((R, D), F32), jax.ShapeDtypeStruct((R, D), F32),
                   jax.ShapeDtypeStruct((R, D), BF16)],
        name=name, compiler_params=_cp(("parallel",)))(*xs, w, h, g_post, g_next)


def _rms_post_bwd(dzs, w, h, g, res, y, g_post, name):
    nz = len(dzs)

    def body(*refs):
        dz_refs, (w_ref, h_ref, g_ref, r_ref, y_ref, gp_ref, dh_ref, dg_ref, dy_ref, dgp_ref) = refs[:nz], refs[nz:]
        n = pl.program_id(0)

        @pl.when(n == 0)
        def _():
            dg_ref[...] = jnp.zeros_like(dg_ref)
            dgp_ref[...] = jnp.zeros_like(dgp_ref)

        dv = _dot_nt(dz_refs[0][...], w_ref[0])
        for j in range(1, nz):
            dv = dv + _dot_nt(dz_refs[j][...], w_ref[j])
        hv = h_ref[...]
        r = lax.rsqrt(jnp.mean(hv * hv, axis=1, keepdims=True) + NORM_EPS)
        xh = hv * r
        dxh = dv * g_ref[...]
        dh = r * (dxh - xh * jnp.mean(dxh * xh, axis=1, keepdims=True)) + r_ref[...]
        row = lax.broadcasted_iota(jnp.int32, (TR, D), 0) + n * TR
        dh = jnp.where(row >= PAD, dh, 0.0)
        dh_ref[...] = dh
        dg_ref[...] += jnp.sum(dv * xh, axis=0, keepdims=True)
        yv = y_ref[...]
        ry = lax.rsqrt(jnp.mean(yv * yv, axis=1, keepdims=True) + NORM_EPS)
        yh = yv * ry
        dyh = dh * gp_ref[...]
        dy_ref[...] = (ry * (dyh - yh * jnp.mean(dyh * yh, axis=1, keepdims=True))).astype(BF16)
        dgp_ref[...] += jnp.sum(dh * yh, axis=0, keepdims=True)

    blk = pl.BlockSpec((TR, D), lambda n: (n, 0))
    vec = pl.BlockSpec((1, D), lambda n: (0, 0))
    return pl.pallas_call(
        body, grid=(R // TR,),
        in_specs=[blk] * nz + [pl.BlockSpec(w.shape, lambda n: (0, 0, 0)), blk, vec, blk, blk, vec],
        out_specs=[blk, vec, blk, vec],
        out_shape=[jax.ShapeDtypeStruct((R, D), F32), jax.ShapeDtypeStruct((1, D), F32),
                   jax.ShapeDtypeStruct((R, D), BF16), jax.ShapeDtypeStruct((1, D), F32)],
        name=name, compiler_params=_cp(("arbitrary",)))(*dzs, w, h, g, res, y, g_post)


def _cols_spec(rows, width, where):
    def index(*g):
        r, c = where(*g)
        return r * rows, (c if isinstance(c, int) else pl.multiple_of(c, 128))
    return pl.BlockSpec((pl.Element(rows), pl.Element(width)), index)


def _tail(h, y, g, target):
    def body(h_ref, y_ref, g_ref, t_ref, d_ref, dy_ref, dg_ref, l_ref):
        n = pl.program_id(0)

        @pl.when(n == 0)
        def _():
            d_ref[...] = jnp.zeros_like(d_ref)
            dy_ref[...] = jnp.zeros_like(dy_ref)
            dg_ref[...] = jnp.zeros_like(dg_ref)
            l_ref[...] = jnp.zeros_like(l_ref)

        @pl.when(n > 0)
        def _():
            yv = y_ref[...]
            r = lax.rsqrt(jnp.mean(yv * yv, axis=1, keepdims=True) + NORM_EPS)
            yh = yv * r
            err = (h_ref[...] + yh * g_ref[...]) - t_ref[...]
            dv = err * (1.0 / D)
            d_ref[...] = dv
            l_ref[...] += jnp.sum(err * err, axis=0, keepdims=True)
            dyh = dv * g_ref[...]
            dy_ref[...] = (r * (dyh - yh * jnp.mean(dyh * yh, axis=1, keepdims=True))).astype(BF16)
            dg_ref[...] += jnp.sum(dv * yh, axis=0, keepdims=True)

        @pl.when(n == NB - 1)
        def _():
            tot = jnp.sum(l_ref[...], axis=1, keepdims=True) * (0.5 / D)
            l_ref[...] = jnp.broadcast_to(tot, (1, D))

    blk = pl.BlockSpec((TB, D), lambda n: (n, 0))
    vec = pl.BlockSpec((1, D), lambda n: (0, 0))
    return pl.pallas_call(
        body, grid=(NB,),
        in_specs=[blk, blk, vec, pl.BlockSpec((TB, D), lambda n: (jnp.maximum(n - 1, 0), 0))],
        out_specs=[blk, blk, vec, vec],
        out_shape=[jax.ShapeDtypeStruct((R, D), F32), jax.ShapeDtypeStruct((R, D), BF16),
                   jax.ShapeDtypeStruct((1, D), F32), jax.ShapeDtypeStruct((1, D), F32)],
        name="tail", compiler_params=_cp(("arbitrary",)))(h, y, g, target)


def _lane_row(shape):
    return lax.broadcasted_iota(jnp.int32, shape, 1), lax.broadcasted_iota(jnp.int32, shape, 0)


def _rot_half(x, lane):
    return jnp.where(lane % HD < HD // 2, pltpu.roll(x, 128 - HD // 2, 1), pltpu.roll(x, HD // 2, 1))


def _swa_blocks(n):
    return (0, jnp.maximum(n - 1, 0), n)


SWA_STACKS = ((0, 0), (0, 1), (1, 0), (1, 1))


def _swa_masks(n, lane, row):
    qpos = n * TB + (row & (TB - 1))
    kp = (n - 1) * TB + lane
    kc = n * TB + lane
    m0 = (lane >= PAD) & (qpos - lane >= TB)
    mp = (kp >= PAD) & (qpos >= kp) & (qpos - kp < TB)
    mc = (kc >= PAD) & (qpos >= kc)
    return (m0, mp, mc)


def _stack_pair(xa, xb, par):
    lane = lax.broadcasted_iota(jnp.int32, (TB, 128), 1)
    keep = (lane < HD) if par == 0 else (lane >= HD)
    return jnp.concatenate([jnp.where(keep, xa, 0.0), jnp.where(keep, xb, 0.0)], axis=0)


def _per_head(a, b):
    row = lax.broadcasted_iota(jnp.int32, (2 * TB, 1), 0)
    return jnp.where(row < TB, a, b)


def _swa_load(n, zq_ref, zkv_ref, cs_ref, sn_ref, lane):
    r0 = pl.multiple_of(n * TB, TB)
    csq, snq = cs_ref[pl.ds(r0, TB), :], sn_ref[pl.ds(r0, TB), :]
    qc = []
    for c in range(4):
        x = zq_ref[:, c * 128:(c + 1) * 128]
        qc.append((x * csq + _rot_half(x, lane) * snq) * SCALE)
    qst = [_stack_pair(qc[2 * g], qc[2 * g + 1], par).astype(BF16) for g, par in SWA_STACKS]
    kvs = []
    for b in _swa_blocks(n):
        b0 = pl.multiple_of(b * TB, TB)
        csb, snb = cs_ref[pl.ds(b0, TB), :], sn_ref[pl.ds(b0, TB), :]
        kx = zkv_ref[pl.ds(b0, TB), 0:128]
        kr = kx * csb + _rot_half(kx, lane) * snb
        vx = zkv_ref[pl.ds(b0, TB), 128:256]
        kvs.append((kr.astype(BF16), pltpu.roll(kr, HD, 1).astype(BF16),
                    vx.astype(BF16), pltpu.roll(vx, HD, 1).astype(BF16), csb, snb, b0))
    return qst, (csq, snq), kvs


def _swa_fwd(z0, cs, sn, sinks):
    def body(zq_ref, zkv_ref, cs_ref, sn_ref, sk_ref, ga_ref, o_ref, a_ref, lse_ref):
        n = pl.program_id(0)
        lane, row = _lane_row((TB, 128))
        lo = lane < HD
        masks = _swa_masks(n, *_lane_row((2 * TB, 128)))
        qst, _, kvs = _swa_load(n, zq_ref, zkv_ref, cs_ref, sn_ref, lane)
        ss = [[jnp.where(m, _dot_nt(qst[si], k if par == g else ka), NEG)
               for (k, ka, _, _, _, _, _), m in zip(kvs, masks)] for si, (g, par) in enumerate(SWA_STACKS)]
        o2, lse2 = [], []
        for si, (g, par) in enumerate(SWA_STACKS):
            sink = _per_head(sk_ref[0, 4 * g + par], sk_ref[0, 4 * g + 2 + par])
            s = ss[si]
            mx = jnp.maximum(jnp.maximum(jnp.max(s[0], axis=1, keepdims=True), jnp.max(s[1], axis=1, keepdims=True)),
                             jnp.max(s[2], axis=1, keepdims=True))
            mx = jnp.maximum(mx, sink)
            es = [jnp.exp(sb - mx) for sb in s]
            den = (jnp.sum(es[0], axis=1, keepdims=True) + jnp.sum(es[1], axis=1, keepdims=True)
                   + jnp.sum(es[2], axis=1, keepdims=True) + jnp.exp(sink - mx))
            inv = 1.0 / den
            t = jnp.zeros((2 * TB, 128), F32)
            for (_, _, v, va, _, _, _), e in zip(kvs, es):
                t = t + _dot((e * inv).astype(BF16), v if par == g else va)
            o2.append(t)
            lse2.append(mx + jnp.log(den))
        lse_t = jnp.zeros((TB, 128), F32)
        for g in range(2):
            for t in range(2):
                rows = slice(t * TB, (t + 1) * TB)
                c = 2 * g + t
                oc = jnp.where(lo, o2[2 * g][rows], o2[2 * g + 1][rows])
                o_ref[:, c * 128:(c + 1) * 128] = oc
                gv = ga_ref[:, c * 128:(c + 1) * 128]
                a_ref[:, c * 128:(c + 1) * 128] = (oc * (gv * _sig(gv))).astype(BF16)
                for par in range(2):
                    lse_t = jnp.where(lane == 4 * g + 2 * t + par, lse2[2 * g + par][rows], lse_t)
        lse_ref[...] = lse_t

    full = pl.BlockSpec((R, 128), lambda n: (0, 0))
    return dict(
        body=body,
        in_specs=[pl.BlockSpec((TB, 512), lambda n: (n, C_Q // 512)),
                  pl.BlockSpec((R, 256), lambda n: (0, C_K // 256)), full, full,
                  pl.BlockSpec(memory_space=pltpu.SMEM), _cols_spec(TB, 512, lambda n: (n, C_GA))],
        args=[z0, z0, cs, sn, sinks, z0],
        out_specs=[pl.BlockSpec((TB, 512), lambda n: (n, 0)), pl.BlockSpec((TB, 512), lambda n: (n, 0)),
                   pl.BlockSpec((TB, 128), lambda n: (n, 0))],
        out_shape=[jax.ShapeDtypeStruct((R, 512), F32), jax.ShapeDtypeStruct((R, 512), BF16),
                   jax.ShapeDtypeStruct((R, 128), F32)],
        scratch=[])


def _swa_bwd(z0, cs, sn, sinks, o, dmix, lse):
    def body(zq_ref, zkv_ref, cs_ref, sn_ref, sk_ref, ga_ref, o_ref, dm_ref, lse_ref,
             dq_ref, dga_ref, dkv_ref, dsk_ref, do_ref, acc_ref):
        n = pl.program_id(0)

        @pl.when(n == 0)
        def _():
            acc_ref[...] = jnp.zeros_like(acc_ref)
            dsk_ref[...] = jnp.zeros_like(dsk_ref)

        gv, dmv = ga_ref[...], dm_ref[...]
        sg = _sig(gv)
        dga_ref[...] = (dmv * o_ref[...] * (sg * (1.0 + gv * (1.0 - sg)))).astype(BF16)
        do_ref[...] = dmv * (gv * sg)
        lane, row = _lane_row((TB, 128))
        lo = lane < HD
        masks = _swa_masks(n, *_lane_row((2 * TB, 128)))
        qst, (csq, snq), kvs = _swa_load(n, zq_ref, zkv_ref, cs_ref, sn_ref, lane)
        lse_t = lse_ref[...]
        ss = [[jnp.where(m, _dot_nt(qst[si], k if par == g else ka), NEG)
               for (k, ka, _, _, _, _, _), m in zip(kvs, masks)] for si, (g, par) in enumerate(SWA_STACKS)]
        dobs, deltas, lses, dps = [], [], [], []
        for g, par in SWA_STACKS:
            ca, cb = slice(2 * g * 128, (2 * g + 1) * 128), slice((2 * g + 1) * 128, (2 * g + 2) * 128)
            dom = _stack_pair(do_ref[:, ca], do_ref[:, cb], par)
            deltas.append(jnp.sum(dom * jnp.concatenate([o_ref[:, ca], o_ref[:, cb]], axis=0), axis=1, keepdims=True))
            dob = dom.astype(BF16)
            dobs.append(dob)
            lses.append(jnp.concatenate(
                [jnp.sum(jnp.where(lane == 4 * g + 2 * t + par, lse_t, 0.0), axis=1, keepdims=True) for t in range(2)],
                axis=0))
            dps.append([_dot_nt(dob, v if par == g else va) for (_, _, v, va, _, _, _) in kvs])
        dk_al = [jnp.zeros((TB, 128), F32) for _ in range(3)]
        dk_mis = [jnp.zeros((TB, 128), F32) for _ in range(3)]
        dv_al = [jnp.zeros((TB, 128), F32) for _ in range(3)]
        dv_mis = [jnp.zeros((TB, 128), F32) for _ in range(3)]
        dsk_t = jnp.zeros((TB, 128), F32)
        dq2 = []
        for si, (g, par) in enumerate(SWA_STACKS):
            dqt = jnp.zeros((2 * TB, 128), F32)
            for bi, (k, ka, _, _, _, _, _) in enumerate(kvs):
                p = jnp.exp(ss[si][bi] - lses[si])
                ds = (p * (dps[si][bi] - deltas[si])).astype(BF16)
                dqt = dqt + _dot(ds, k if par == g else ka)
                dkh = _dot_tn(ds, qst[si])
                dvh = _dot_tn(p.astype(BF16), dobs[si])
                if par == g:
                    dk_al[bi] = dk_al[bi] + dkh
                    dv_al[bi] = dv_al[bi] + dvh
                else:
                    dk_mis[bi] = dk_mis[bi] + dkh
                    dv_mis[bi] = dv_mis[bi] + dvh
            dq2.append(dqt)
            sink = _per_head(sk_ref[0, 4 * g + par], sk_ref[0, 4 * g + 2 + par])
            dsk = -jnp.exp(sink - lses[si]) * deltas[si]
            for t in range(2):
                dsk_t = jnp.where(lane == 4 * g + 2 * t + par, dsk[t * TB:(t + 1) * TB], dsk_t)
        for g in range(2):
            for t in range(2):
                rows = slice(t * TB, (t + 1) * TB)
                c = 2 * g + t
                dqc = jnp.where(lo, dq2[2 * g][rows], dq2[2 * g + 1][rows]) * SCALE
                dq_ref[:, c * 128:(c + 1) * 128] = (dqc * csq + _rot_half(dqc * snq, lane)).astype(BF16)
        for bi, (_, _, _, _, csb, snb, b0) in enumerate(kvs):
            dk = dk_al[bi] + pltpu.roll(dk_mis[bi], HD, 1)
            dv = dv_al[bi] + pltpu.roll(dv_mis[bi], HD, 1)
            acc_ref[pl.ds(b0, TB), 0:128] += dk * csb + _rot_half(dk * snb, lane)
            acc_ref[pl.ds(b0, TB), 128:256] += dv
        dsk_ref[0:1, :] += jnp.sum(dsk_t, axis=0, keepdims=True)

        @pl.when(n == NB - 1)
        def _():
            dkv_ref[...] = acc_ref[...].astype(BF16)

    full = pl.BlockSpec((R, 128), lambda n: (0, 0))
    b512 = pl.BlockSpec((TB, 512), lambda n: (n, 0))
    return dict(
        body=body,
        in_specs=[pl.BlockSpec((TB, 512), lambda n: (n, C_Q // 512)),
                  pl.BlockSpec((R, 256), lambda n: (0, C_K // 256)), full, full,
                  pl.BlockSpec(memory_space=pltpu.SMEM), _cols_spec(TB, 512, lambda n: (n, C_GA)),
                  b512, b512, pl.BlockSpec((TB, 128), lambda n: (n, 0))],
        args=[z0, z0, cs, sn, sinks, z0, o, dmix, lse],
        out_specs=[b512, b512, pl.BlockSpec((R, 256), lambda n: (0, 0)), pl.BlockSpec((8, 128), lambda n: (0, 0))],
        out_shape=[jax.ShapeDtypeStruct((R, 512), BF16), jax.ShapeDtypeStruct((R, 512), BF16),
                   jax.ShapeDtypeStruct((R, 256), BF16), jax.ShapeDtypeStruct((8, 128), F32)],
        scratch=[pltpu.VMEM((TB, 512), F32), pltpu.VMEM((R, 256), F32)])


CC = 512
HALO = CONV_W - 1


def _conv_fwd(z0, conv_w, conv_b, ln_g, ln_b):
    def body(g_ref, w_ref, cb_ref, lg_ref, lb_ref, cv_ref, s_ref, ubuf):
        n = pl.program_id(0)

        @pl.when(n == 0)
        def _():
            ubuf[...] = jnp.zeros_like(ubuf)

        u = g_ref[:, 0:CC] * _sig(g_ref[:, CC:2 * CC])
        for k in range(8):
            ubuf[k, 0:TB + 8, :] = ubuf[k, TB:2 * TB + 8, :]
            ubuf[k, pl.ds(TB + 8 - k, TB), :] = u
        acc = jnp.zeros((TB, CC), F32)
        for w in range(CONV_W):
            off = TB - HALO + w
            acc = acc + ubuf[off % 8, pl.ds(off + 8 - off % 8, TB), :] * w_ref[w:w + 1, :]
        cv = acc + cb_ref[...]
        cv_ref[...] = cv
        xc = cv - jnp.mean(cv, axis=1, keepdims=True)
        rs = lax.rsqrt(jnp.mean(xc * xc, axis=1, keepdims=True) + LN_EPS)
        ln = xc * rs * lg_ref[...] + lb_ref[...]
        s_ref[...] = (ln * _sig(ln)).astype(BF16)

    vec = pl.BlockSpec((1, CC), lambda n: (0, 0))
    blk = pl.BlockSpec((TB, CC), lambda n: (n, 0))
    return dict(
        body=body,
        in_specs=[_cols_spec(TB, 2 * CC, lambda n: (n, C_GLU)),
                  pl.BlockSpec((32, CC), lambda n: (0, 0)), vec, vec, vec],
        args=[z0, conv_w, conv_b, ln_g, ln_b],
        out_specs=[blk, blk],
        out_shape=[jax.ShapeDtypeStruct((R, CC), F32), jax.ShapeDtypeStruct((R, CC), BF16)],
        scratch=[pltpu.VMEM((8, 2 * TB + 8, CC), F32)])


def _conv_bwd(ds, cv, z0, conv_w, ln_g, ln_b):
    def body(ds_ref, cv_ref, g_ref, w_ref, lg_ref, lb_ref, dglu_ref, dw_ref, dsm_ref, dbuf):
        n = pl.program_id(0)

        @pl.when(n == 0)
        def _():
            dbuf[...] = jnp.zeros_like(dbuf)
            dw_ref[...] = jnp.zeros_like(dw_ref)
            dsm_ref[...] = jnp.zeros_like(dsm_ref)

        cv = cv_ref[...]
        xc = cv - jnp.mean(cv, axis=1, keepdims=True)
        rs = lax.rsqrt(jnp.mean(xc * xc, axis=1, keepdims=True) + LN_EPS)
        xh = xc * rs
        ln = xh * lg_ref[...] + lb_ref[...]
        sg = _sig(ln)
        dln = ds_ref[...] * (sg * (1.0 + ln * (1.0 - sg)))
        dxh = dln * lg_ref[...]
        dcv = rs * (dxh - jnp.mean(dxh, axis=1, keepdims=True) - xh * jnp.mean(dxh * xh, axis=1, keepdims=True))
        dsm_ref[0:1, :] += jnp.sum(dcv, axis=0, keepdims=True)
        dsm_ref[1:2, :] += jnp.sum(dln * xh, axis=0, keepdims=True)
        dsm_ref[2:3, :] += jnp.sum(dln, axis=0, keepdims=True)
        for k in range(8):
            dbuf[k, TB:2 * TB + 8, :] = dbuf[k, 0:TB + 8, :]
            dbuf[k, pl.ds(8 - k, TB), :] = dcv
        a = g_ref[:, 0:CC]
        sb = _sig(g_ref[:, CC:2 * CC])
        u = a * sb
        du = jnp.zeros((TB, CC), F32)
        for w in range(CONV_W):
            off = HALO - w
            sh = dbuf[off % 8, pl.ds(off + 8 - off % 8, TB), :]
            du = du + sh * w_ref[w:w + 1, :]
            dw_ref[w:w + 1, :] += jnp.sum(u * sh, axis=0, keepdims=True)
        dglu_ref[:, 0:CC] = (du * sb).astype(BF16)
        dglu_ref[:, CC:2 * CC] = (du * a * sb * (1.0 - sb)).astype(BF16)

    rev = lambda n: (NB - 1 - n, 0)
    vec = pl.BlockSpec((1, CC), lambda n: (0, 0))
    blk = pl.BlockSpec((TB, CC), rev)
    return dict(
        body=body,
        in_specs=[blk, blk, _cols_spec(TB, 2 * CC, lambda n: (NB - 1 - n, C_GLU)),
                  pl.BlockSpec((32, CC), lambda n: (0, 0)), vec, vec],
        args=[ds, cv, z0, conv_w, ln_g, ln_b],
        out_specs=[pl.BlockSpec((TB, 2 * CC), rev), pl.BlockSpec((32, CC), lambda n: (0, 0)),
                   pl.BlockSpec((8, CC), lambda n: (0, 0))],
        out_shape=[jax.ShapeDtypeStruct((R, 2 * CC), BF16), jax.ShapeDtypeStruct((32, CC), F32),
                   jax.ShapeDtypeStruct((8, CC), F32)],
        scratch=[pltpu.VMEM((8, 2 * TB + 8, CC), F32)])


def _split_dot(x, t):
    hi = x.astype(BF16)
    lo = (x - hi.astype(F32)).astype(BF16)
    return _dot(hi, t) + _dot(lo, t)


def _stack_heads(x):
    lane = lax.broadcasted_iota(jnp.int32, (TB, 128), 1)
    return jnp.concatenate([jnp.where(lane < HD, x, 0.0), jnp.where(lane < HD, 0.0, x)], axis=0).astype(BF16)


def _sb_stack(qv, i):
    lane2, row2 = _lane_row((2 * TB, 128))
    qpos2 = i * TB + (row2 & (TB - 1))
    lane, row = _lane_row((TB, 128))
    return _stack_heads(qv), lane2, qpos2, (row > lane).astype(BF16)


SB_U = 3
SB_DEAD = -104.0
SB_P = 4


def _sb_fwd(q, k, v, g):
    def body(q_ref, k_ref, v_ref, g_ref, o_ref, m_ref, c_ref, n_ref):
        p, i = pl.program_id(0), pl.program_id(1)
        lane, row = _lane_row((TB, 128))
        lo = lane < HD
        slabs = [slice(s * 128, (s + 1) * 128) for s in range(SB_P)]
        q2s = []
        for sl in slabs:
            q2, lane2, qpos2, tri_gt = _sb_stack(q_ref[:, sl].astype(F32) * SCALE, i)
            q2s.append(q2)

        def cond(st):
            t, _, c2s = st
            alive = jnp.max(c2s[0])
            for c2 in c2s[1:]:
                alive = jnp.maximum(alive, jnp.max(c2))
            return jnp.logical_and(i - SB_U * t >= 0, alive > SB_DEAD)

        def step(st):
            t, accs, c2s = st
            accs, c2s = list(accs), list(c2s)
            jrs = [i - SB_U * t - u for u in range(SB_U)]
            j0s = [pl.multiple_of(jnp.maximum(jr, 0) * TB, TB) for jr in jrs]
            valids = []
            for jr in jrs:
                kpos = jr * TB + lane2
                valids.append((kpos >= PAD) & (kpos < qpos2))
            zs = [[jnp.where(valid, _dot_nt(q2s[s], k_ref[pl.ds(j0, TB), slabs[s]]), NEG)
                   for j0, valid in zip(j0s, valids)] for s in range(SB_P)]
            lbs, l1s = [], []
            for s in range(SB_P):
                lbs.append([jnp.minimum(z, 0.0) - jnp.log(1.0 + jnp.exp(-jnp.abs(z))) for z in zs[s]])
                l1s.append([lb - z for lb, z in zip(lbs[s], zs[s])])
            sfxs = [[_split_dot(l1, tri_gt) for l1 in l1s[s]] for s in range(SB_P)]
            carries = []
            for s in range(SB_P):
                cs, c2 = [], c2s[s]
                for jr, l1 in zip(jrs, l1s[s]):
                    cs.append(c2)
                    c_ref[:, slabs[s]] = jnp.where(lane == 2 * jr, c2[0:TB],
                                                   jnp.where(lane == 2 * jr + 1, c2[TB:2 * TB], c_ref[:, slabs[s]]))
                    c2 = c2 + jnp.sum(l1, axis=1, keepdims=True)
                carries.append(cs)
                c2s[s] = c2
            for s in range(SB_P):
                for j0, valid, lb, sfx, cu in zip(j0s, valids, lbs[s], sfxs[s], carries[s]):
                    a = jnp.exp(lb + sfx + cu).astype(BF16)
                    av = _dot(a, v_ref[pl.ds(j0, TB), slabs[s]])
                    accs[s] = accs[s] + jnp.where(lo, av[0:TB], av[TB:2 * TB])
            return t + 1, tuple(accs), tuple(c2s)

        c_ref[...] = jnp.zeros_like(c_ref)
        init = (jnp.int32(0), tuple(jnp.zeros((TB, 128), F32) for _ in slabs),
                tuple(jnp.zeros((2 * TB, 1), F32) for _ in slabs))
        t, accs, _ = lax.while_loop(cond, step, init)
        for sl, acc in zip(slabs, accs):
            o_ref[:, sl] = acc
            gv = g_ref[:, sl]
            m_ref[:, sl] = (acc * (gv * _sig(gv))).astype(BF16)
        n_ref[p, i] = t

    wide = SB_P * 128
    slab = pl.BlockSpec((R, wide), lambda p, i: (0, p))
    blk = pl.BlockSpec((TB, wide), lambda p, i: (i, p))
    sd = jax.ShapeDtypeStruct((R, D), F32)
    return pl.pallas_call(
        body, grid=(D // wide, NB), in_specs=[blk, slab, slab, blk],
        out_specs=[blk, blk, blk, pl.BlockSpec(memory_space=pltpu.SMEM)],
        out_shape=[sd, jax.ShapeDtypeStruct((R, D), BF16), sd, jax.ShapeDtypeStruct((D // wide, NB), jnp.int32)],
        name="sb_fwd", compiler_params=_cp(("arbitrary", "arbitrary")))(q, k, v, g)


def _sb_bwd(trips, q, k, v, car, dm, g, o):
    def body(n_ref, q_ref, k_ref, v_ref, c_ref, dm_ref, g_ref, o_ref, dq_ref, dko_ref, dvo_ref, dg_ref,
             dk_ref, dv_ref):
        p, i = pl.program_id(0), pl.program_id(1)

        @pl.when(i == 0)
        def _():
            dk_ref[...] = jnp.zeros_like(dk_ref)
            dv_ref[...] = jnp.zeros_like(dv_ref)

        lane, row = _lane_row((TB, 128))
        lo = lane < HD
        tri_lt = (row < lane).astype(BF16)
        slabs = [slice(s * 128, (s + 1) * 128) for s in range(SB_P)]
        q2s, do2s, cts = [], [], []
        for sl in slabs:
            q2, lane2, qpos2, tri_gt = _sb_stack(q_ref[:, sl].astype(F32) * SCALE, i)
            q2s.append(q2)
            gv, dmv = g_ref[:, sl], dm_ref[:, sl]
            sg = _sig(gv)
            dg_ref[:, sl] = (dmv * o_ref[:, sl] * (sg * (1.0 + gv * (1.0 - sg)))).astype(BF16)
            do2s.append(_stack_heads(dmv * (gv * sg)))
            cts.append(c_ref[:, sl])
        trips_i = n_ref[p, i]
        first = jnp.maximum(i + 1 - SB_U * trips_i, 0)

        def step(t, carry):
            dqs, g2s = carry
            dqs, g2s = list(dqs), list(g2s)
            jrs = [first + SB_U * t + u for u in range(SB_U)]
            j0s = [pl.multiple_of(jnp.minimum(jr, i) * TB, TB) for jr in jrs]
            valids = []
            for jr in jrs:
                kpos = jr * TB + lane2
                valids.append((kpos >= PAD) & (kpos < qpos2))
            ks = [[k_ref[pl.ds(j0, TB), sl] for j0 in j0s] for sl in slabs]
            zs = [[jnp.where(valid, _dot_nt(q2s[s], kj), NEG) for kj, valid in zip(ks[s], valids)] for s in range(SB_P)]
            das = [[_dot_nt(do2s[s], v_ref[pl.ds(j0, TB), slabs[s]]) for j0 in j0s] for s in range(SB_P)]
            es = [[jnp.exp(-jnp.abs(z)) for z in zs[s]] for s in range(SB_P)]
            lbs = [[jnp.minimum(z, 0.0) - jnp.log(1.0 + e) for z, e in zip(zs[s], es[s])] for s in range(SB_P)]
            l1s = [[lb - z for lb, z in zip(lbs[s], zs[s])] for s in range(SB_P)]
            sfxs = [[_split_dot(l1, tri_gt) for l1 in l1s[s]] for s in range(SB_P)]
            a_s, gmats, gpres = [], [], []
            for s in range(SB_P):
                a_l, gm_l, gp_l, g2 = [], [], [], g2s[s]
                for jr, valid, lb, sfx, da in zip(jrs, valids, lbs[s], sfxs[s], das[s]):
                    later = jnp.concatenate(
                        [jnp.sum(jnp.where(lane == 2 * jr + hh, cts[s], 0.0), axis=1, keepdims=True) for hh in range(2)],
                        axis=0)
                    a = jnp.exp(lb + sfx + later)
                    gmat = da * a
                    a_l.append(a.astype(BF16))
                    gm_l.append(gmat)
                    gp_l.append(g2)
                    g2 = g2 + jnp.sum(gmat, axis=1, keepdims=True)
                a_s.append(a_l)
                gmats.append(gm_l)
                gpres.append(gp_l)
                g2s[s] = g2
            pres = [[gp + _split_dot(gmat, tri_lt) for gp, gmat in zip(gpres[s], gmats[s])] for s in range(SB_P)]
            for s in range(SB_P):
                for j0, kj, valid, z, e, gmat, pre, a in zip(j0s, ks[s], valids, zs[s], es[s], gmats[s], pres[s], a_s[s]):
                    r = 1.0 / (1.0 + e)
                    big = z >= 0.0
                    beta = jnp.where(big, r, e * r)
                    omb = jnp.where(big, e * r, r)
                    dz = (gmat * omb - beta * pre).astype(BF16)
                    dq2 = _dot(dz, kj)
                    dqs[s] = dqs[s] + jnp.where(lo, dq2[0:TB], dq2[TB:2 * TB])
                    dk_ref[pl.ds(j0, TB), slabs[s]] += _dot_tn(dz, q2s[s])
                    dv_ref[pl.ds(j0, TB), slabs[s]] += _dot_tn(a, do2s[s])
            return tuple(dqs), tuple(g2s)

        init = (tuple(jnp.zeros((TB, 128), F32) for _ in slabs), tuple(jnp.zeros((2 * TB, 1), F32) for _ in slabs))
        dqs, _ = lax.fori_loop(0, trips_i, step, init)
        for sl, dq in zip(slabs, dqs):
            dq_ref[:, sl] = (dq * SCALE).astype(BF16)

        @pl.when(i == NB - 1)
        def _():
            dko_ref[...] = dk_ref[...].astype(BF16)
            dvo_ref[...] = dv_ref[...].astype(BF16)

    wide = SB_P * 128
    slab = pl.BlockSpec((R, wide), lambda p, i: (0, p))
    blk = pl.BlockSpec((TB, wide), lambda p, i: (i, p))
    sd = jax.ShapeDtypeStruct((R, D), BF16)
    return pl.pallas_call(
        body, grid=(D // wide, NB),
        in_specs=[pl.BlockSpec(memory_space=pltpu.SMEM), blk, slab, slab, blk, blk, blk, blk],
        out_specs=[blk, slab, slab, blk], out_shape=[sd, sd, sd, sd],
        scratch_shapes=[pltpu.VMEM((R, wide), F32), pltpu.VMEM((R, wide), F32)], name="sb_bwd",
        compiler_params=_cp(("arbitrary", "arbitrary")))(trips, q, k, v, car, dm, g, o)


def _adamw(w, parts, m, v, name):
    rows, cols = w.shape
    tr = next((t for t in (256, 176) if rows % t == 0), rows)
    nparts = len(parts)

    def body(*refs):
        w_ref = refs[0]
        p_refs = refs[1:1 + nparts]
        m_ref, v_ref, g_ref, d_ref, nm_ref, nv_ref = refs[1 + nparts:]
        g = p_refs[0][...]
        for p_ref in p_refs[1:]:
            g = g + p_ref[...]
        nm = ADAM_B1 * m_ref[...] + (1.0 - ADAM_B1) * g
        nv = ADAM_B2 * v_ref[...] + (1.0 - ADAM_B2) * (g * g)
        m_hat = nm / (1.0 - ADAM_B1 ** ADAM_STEP)
        v_hat = nv / (1.0 - ADAM_B2 ** ADAM_STEP)
        g_ref[...] = g
        d_ref[...] = -ADAM_LR * (m_hat / (jnp.sqrt(v_hat) + ADAM_EPS) + ADAM_WD * w_ref[...])
        nm_ref[...] = nm
        nv_ref[...] = nv

    blk = pl.BlockSpec((tr, cols), lambda i: (i, 0))
    sd = jax.ShapeDtypeStruct((rows, cols), F32)
    return pl.pallas_call(
        body, grid=(rows // tr,), in_specs=[blk] * (3 + nparts), out_specs=[blk] * 4, out_shape=[sd] * 4,
        name=name, compiler_params=_cp(("parallel",)))(w, *parts, m, v)


def _sum8(buf, name):
    _, rows, cols = buf.shape

    def body(b_ref, o_ref):
        acc = b_ref[0]
        for i in range(1, 8):
            acc = acc + b_ref[i]
        o_ref[...] = acc

    return pl.pallas_call(
        body, out_shape=jax.ShapeDtypeStruct((rows, cols), F32), name=name,
        compiler_params=pltpu.CompilerParams(vmem_limit_bytes=VMEM_LIMIT))(buf)


MESH = pl.DeviceIdType.MESH
ANY = pl.BlockSpec(memory_space=pl.ANY)


def _chip_peers():
    x, y = lax.axis_index("x"), lax.axis_index("y")
    return [(1 - x, y), (x, 1 - y), (1 - x, 1 - y)]


def _gather_chips(shards):
    plan = _gather_plan(shards)

    def body(*refs):
        n = len(shards)
        ins, outs, sems = refs[:n], refs[n:2 * n], refs[2 * n:]
        plan["start"](ins, outs, sems)
        plan["mid"](ins, outs, sems)
        plan["finish"](ins, outs, sems)

    n = len(shards)
    res = pl.pallas_call(
        body, in_specs=[ANY] * n, out_specs=[ANY] * n, out_shape=plan["out_shape"],
        scratch_shapes=plan["sems"], name="gather_chips")(*plan["args"])
    return plan["post"](res)


def _gather_plan(shards):
    n = len(shards)
    shards = [s.reshape((2, s.shape[0] // 2) + s.shape[1:]) for s in shards]

    def copies(kind, ins, outs, sems):
        s1, r1, s2, r2 = sems
        x, y, c = lax.axis_index("x"), lax.axis_index("y"), lax.axis_index("c")
        me = 2 * x + y
        out = []
        for j, (px, py) in enumerate(_chip_peers()):
            for a in range(n):
                k = j * n + a
                got = outs[a].at[2 * px + py].at[c]
                other = outs[a].at[2 * px + py].at[1 - c]
                src, dst, ss, rs, dev = {
                    "first": (ins[a].at[c], outs[a].at[me].at[c], s1, r1, (px, py, c)),
                    "landed": (got, got, s1, r1, (px, py, c)),
                    "passed": (got, got, s2, r2, (x, y, 1 - c)),
                    "theirs": (other, other, s2, r2, (x, y, 1 - c)),
                }[kind]
                out.append(pltpu.make_async_remote_copy(
                    src_ref=src, dst_ref=dst, send_sem=ss.at[k], recv_sem=rs.at[k], device_id=dev, device_id_type=MESH))
        return out

    def start(ins, outs, sems):
        for cp in copies("first", ins, outs, sems):
            cp.start()

    def mid(ins, outs, sems):
        for got, fwd in zip(copies("landed", ins, outs, sems), copies("passed", ins, outs, sems)):
            got.wait_recv()
            fwd.start()

    def finish(ins, outs, sems):
        for cp in copies("theirs", ins, outs, sems):
            cp.wait_recv()
        for cp in copies("first", ins, outs, sems) + copies("passed", ins, outs, sems):
            cp.wait_send()

    def post(res):
        me = 2 * lax.axis_index("x") + lax.axis_index("y")
        res = [lax.dynamic_update_index_in_dim(r, s, me, 0) for r, s in zip(res, shards)]
        return [r.reshape((N_CHIPS, 2 * r.shape[2]) + r.shape[3:]) for r in res]

    return dict(args=shards, out_shape=[jax.ShapeDtypeStruct((N_CHIPS,) + s.shape, s.dtype) for s in shards],
                sems=[pltpu.SemaphoreType.DMA((3 * n,))] * 4, start=start, mid=mid, finish=finish, post=post)


def _rows_call(name, parts, plan):
    n_in = [len(p["args"]) for p in parts]
    n_out = [len(p["out_shape"]) for p in parts]
    n_scr = [len(p["scratch"]) for p in parts]
    c_in, c_out = len(plan["args"]), len(plan["out_shape"])

    def split(refs, sizes):
        out, pos = [], 0
        for k in sizes:
            out.append(refs[pos:pos + k])
            pos += k
        return out

    def body(*refs):
        ins, outs, scr = split(refs, [sum(n_in) + c_in, sum(n_out) + c_out, sum(n_scr) + len(plan["sems"])])
        p_in, p_out, p_scr = split(ins, n_in + [c_in]), split(outs, n_out + [c_out]), split(scr, n_scr + [len(plan["sems"])])
        comm = (p_in[-1], p_out[-1], p_scr[-1])
        step = pl.program_id(0)

        @pl.when(step == 0)
        def _():
            plan["start"](*comm)

        for p, i, o, s in zip(parts, p_in, p_out, p_scr):
            p["body"](*i, *o, *s)

        @pl.when(step == NB - 2)
        def _():
            plan["mid"](*comm)

        @pl.when(step == NB - 1)
        def _():
            plan["finish"](*comm)

    flat = lambda key: [v for p in parts for v in p[key]]
    res = pl.pallas_call(
        body, grid=(NB,), in_specs=flat("in_specs") + [ANY] * c_in, out_specs=flat("out_specs") + [ANY] * c_out,
        out_shape=flat("out_shape") + plan["out_shape"], scratch_shapes=flat("scratch") + plan["sems"],
        name=name, compiler_params=_cp(("arbitrary",)))(*flat("args"), *plan["args"])
    outs = split(res, n_out + [c_out])
    return outs[:-1], outs[-1]


def _pair_exchange(grads, name):
    n = len(grads)
    hs = [g.shape[1] // 2 for g in grads]
    grads = [g.reshape((N_CHIPS, 2, h) + g.shape[2:]) for g, h in zip(grads, hs)]

    def body(*refs):
        ins, got = refs[:n], refs[n:2 * n]
        ssem, rsem = refs[2 * n:]
        x, y, c = lax.axis_index("x"), lax.axis_index("y"), lax.axis_index("c")
        sends = [pltpu.make_async_remote_copy(
            src_ref=ins[a].at[:, 1 - c], dst_ref=got[a], send_sem=ssem.at[a],
            recv_sem=rsem.at[a], device_id=(x, y, 1 - c), device_id_type=MESH) for a in range(n)]
        for cp in sends:
            cp.start()
        for cp in sends:
            cp.wait()

    half_shapes = [jax.ShapeDtypeStruct((N_CHIPS, h) + g.shape[3:], g.dtype) for g, h in zip(grads, hs)]
    got = pl.pallas_call(
        body, in_specs=[ANY] * n, out_specs=[ANY] * n, out_shape=half_shapes,
        scratch_shapes=[pltpu.SemaphoreType.DMA((n,))] * 2, name=name)(*grads)
    return grads, got


def _sum_pair(both, got, send_dtype, name):
    _, _, rows, cols = both.shape
    tr = 256 if rows % 256 == 0 else rows

    def body(c_ref, a_ref, b_ref, f_ref, s_ref):
        t = a_ref[...].astype(F32) + b_ref[...].astype(F32)
        f_ref[...] = t
        s_ref[...] = t.astype(send_dtype)

    blk = pl.BlockSpec((N_CHIPS, tr, cols), lambda i, c: (0, i, 0))
    mine = pl.BlockSpec((N_CHIPS, None, tr, cols), lambda i, c: (0, c[0], i, 0))
    return pl.pallas_call(
        body, grid_spec=pltpu.PrefetchScalarGridSpec(
            num_scalar_prefetch=1, grid=(rows // tr,), in_specs=[mine, blk], out_specs=[blk, blk]),
        out_shape=[jax.ShapeDtypeStruct(got.shape, F32), jax.ShapeDtypeStruct(got.shape, send_dtype)],
        name=name, compiler_params=_cp(("parallel",)))(lax.axis_index("c").reshape(1), both, got)


def _scatter_plan(send):
    n = len(send)

    def copies(sin, land, sems):
        ssem, rsem = sems
        c = lax.axis_index("c")
        return [pltpu.make_async_remote_copy(
            src_ref=sin[a].at[2 * px + py], dst_ref=land[a].at[j], send_sem=ssem.at[j * n + a],
            recv_sem=rsem.at[j * n + a], device_id=(px, py, c), device_id_type=MESH)
            for j, (px, py) in enumerate(_chip_peers()) for a in range(n)]

    def start(sin, land, sems):
        for cp in copies(sin, land, sems):
            cp.start()

    def finish(sin, land, sems):
        for cp in copies(sin, land, sems):
            cp.wait()

    return dict(args=list(send), out_shape=[jax.ShapeDtypeStruct((3,) + s.shape[1:], s.dtype) for s in send],
                sems=[pltpu.SemaphoreType.DMA((3 * n,))] * 2, start=start, mid=lambda *a: None, finish=finish)


def _sum_shard(keep, land, name):
    _, rows, cols = keep.shape
    tr = 256 if rows % 256 == 0 else rows

    def body(me_ref, m_ref, l_ref, o_ref):
        o_ref[...] = ((m_ref[...] + l_ref[0].astype(F32)) + l_ref[1].astype(F32)) + l_ref[2].astype(F32)

    own = pl.BlockSpec((None, tr, cols), lambda i, me: (me[0], i, 0))
    me = (2 * lax.axis_index("x") + lax.axis_index("y")).reshape(1)
    return pl.pallas_call(
        body, grid_spec=pltpu.PrefetchScalarGridSpec(
            num_scalar_prefetch=1, grid=(rows // tr,),
            in_specs=[own, pl.BlockSpec((3, tr, cols), lambda i, me: (0, i, 0))],
            out_specs=pl.BlockSpec((tr, cols), lambda i, me: (i, 0))),
        out_shape=jax.ShapeDtypeStruct((rows, cols), F32),
        name=name, compiler_params=_cp(("parallel",)))(me, keep, land)


def _join_cores(halves):
    n = len(halves)

    def body(*refs):
        ins, outs = refs[:n], refs[n:2 * n]
        ssem, rsem = refs[2 * n:]
        x, y, c = lax.axis_index("x"), lax.axis_index("y"), lax.axis_index("c")
        sends = [pltpu.make_async_remote_copy(
            src_ref=ins[a], dst_ref=outs[a].at[c], send_sem=ssem.at[a], recv_sem=rsem.at[a],
            device_id=(x, y, 1 - c), device_id_type=MESH) for a in range(n)]
        for cp in sends:
            cp.start()
        for a in range(n):
            sends[a].wait_send()
            pltpu.make_async_remote_copy(
                src_ref=ins[a], dst_ref=outs[a].at[1 - c], send_sem=ssem.at[a], recv_sem=rsem.at[a],
                device_id=(x, y, 1 - c), device_id_type=MESH).wait_recv()

    res = pl.pallas_call(
        body, in_specs=[ANY] * n, out_specs=[ANY] * n,
        out_shape=[jax.ShapeDtypeStruct((2,) + h.shape, h.dtype) for h in halves],
        scratch_shapes=[pltpu.SemaphoreType.DMA((n,))] * 2, name="join_cores")(*halves)
    c = lax.axis_index("c")
    res = [lax.dynamic_update_index_in_dim(r, h, c, 0) for r, h in zip(res, halves)]
    return [r.reshape((2 * r.shape[1],) + r.shape[2:]) for r in res]


def _gather_all(vec):
    def copies(kind, ins, outs, sems):
        ssem, rsem = sems
        x, y, c = lax.axis_index("x"), lax.axis_index("y"), lax.axis_index("c")
        me = 4 * x + 2 * y + c
        out = []
        for k in range(1, 8):
            px, py, pc = x ^ (k >> 2), y ^ ((k >> 1) & 1), c ^ (k & 1)
            dst = outs[0].at[me] if kind == "send" else outs[0].at[4 * px + 2 * py + pc]
            out.append(pltpu.make_async_remote_copy(
                src_ref=ins[0], dst_ref=dst, send_sem=ssem.at[k - 1], recv_sem=rsem.at[k - 1],
                device_id=(px, py, pc), device_id_type=MESH))
        return out

    def body(v_ref, o_ref, ssem, rsem):
        refs = ([v_ref], [o_ref], (ssem, rsem))
        for cp in copies("send", *refs):
            cp.start()
        for cp in copies("recv", *refs):
            cp.wait_recv()
        for cp in copies("send", *refs):
            cp.wait_send()

    res = pl.pallas_call(
        body, in_specs=[ANY], out_specs=ANY, out_shape=jax.ShapeDtypeStruct((8,) + vec.shape, vec.dtype),
        scratch_shapes=[pltpu.SemaphoreType.DMA((7,))] * 2, name="gather_all")(vec)
    me = 4 * lax.axis_index("x") + 2 * lax.axis_index("y") + lax.axis_index("c")
    return lax.dynamic_update_index_in_dim(res, vec, me, 0)


def _rope_tables():
    pos = (jnp.arange(R, dtype=jnp.int32) - PAD).astype(F32)
    half = HD // 2
    inv = ROPE_THETA ** (-jnp.arange(half, dtype=F32) / half)
    ang = pos[:, None] * inv[None, :]
    cos, sin = jnp.cos(ang), jnp.sin(ang)
    cs = jnp.tile(cos, (1, 4))
    sn = jnp.tile(jnp.concatenate([-sin, sin], axis=1), (1, 2))
    return cs, sn


def _local_step(x, target, p):
    w0t = p["ab_w_in"]
    conv_w = jnp.concatenate([p["ab_conv_w"], jnp.zeros((1, CC), F32)], axis=0)
    cs, sn = _rope_tables()

    h0 = jnp.concatenate([jnp.zeros((PAD, D), F32), p["meta_tokens"], x], axis=0)

    xn0 = _rms_fwd(h0, p["ab_pre_norm"], "rms_fwd0")
    plan = _gather_plan([p["sb_w_out"], p["ab_w_out"], p["ab_w_pw2"]])
    z0, gathered = _mm([(xn0, w0t)], F32, "in_proj0", 544, 1408, tb=True, plan=plan)
    wo1, wo0, wpw = plan["post"](gathered)
    wo1, wo0, wpw = wo1.reshape(D, D), wo0.reshape(D, D), wpw.reshape(CC, CC)
    plan = _gather_plan([p["sb_w_in"]])
    ((o0, a0, lse0), (cv0, s0)), gathered = _rows_call(
        "fwd0", [_swa_fwd(z0, cs, sn, p["ab_sinks"]),
                 _conv_fwd(z0, conv_w, p["ab_conv_b"], p["ab_conv_ln_g"], p["ab_conv_ln_b"])], plan)
    (w1,) = plan["post"](gathered)
    t0, c0 = _pw2_fwd(s0, wpw, z0)
    wo0h = wo0.reshape(2, CC, D)
    y0, h1, xn1 = _post_rms_fwd([a0, c0], wo0h, h0, p["ab_post_norm"], p["sb_pre_norm"], "out_proj0_norms")

    q1 =_mm([(xn1, (w1, 0))], BF16, "in_proj1_q", 544, 1024)
    k1 = _mm([(xn1, (w1, 1))], BF16, "in_proj1_k", 544, 1024)
    v1 = _mm([(xn1, (w1, 2))], BF16, "in_proj1_v", 544, 1024)
    g1 = _mm([(xn1, (w1, 3))], F32, "in_proj1_g", 544, 1024)
    o1, m1, car1, trips1 = _sb_fwd(q1, k1, v1, g1)
    y1 = _mm([(m1, wo1)], F32, "out_proj1", 544, 1024)

    dh2, dy1, d_sb_post, loss_row = _tail(h1, y1, p["sb_post_norm"], target)

    dm1 = _mm([(dy1, wo1)], F32, "out_proj1_dx", 544, 1024, tb=True)
    d_wo1 = _mm([(m1, dy1)], BF16, "out_proj1_dw", 512, 1024, ta=True)
    dq1, dk1, dv1, dg1 = _sb_bwd(trips1, q1, k1, v1, car1, dm1, g1, o1)
    dz1 = [dq1, dk1, dv1, dg1]
    d_w1 = jnp.stack([_mm([(xn1, dz1[j])], BF16, "in_proj1_dw%d" % j, 512, 1024, ta=True) for j in range(4)])

    dh1, d_sb_pre, dy0, d_ab_post = _rms_post_bwd(dz1, w1, h1, p["sb_pre_norm"], dh2, y0, p["ab_post_norm"],
                                                  "in_proj1_dx_norms")
    dmix0 = _mm([(dy0, wo0)], F32, "out_proj0_dx", 544, 1024, tb=True)
    d_wo0 = jnp.concatenate([_mm([(a0, dy0)], BF16, "out_proj0_dw_a", 512, 1024, ta=True),
                             _mm([(c0, dy0)], BF16, "out_proj0_dw_b", 512, 1024, ta=True)], axis=0)
    dgb0, ds0, d_wpw = _pw2_bwd(dmix0, t0, z0, wpw, s0)
    d_wpw = d_wpw.astype(BF16)
    early = ("sb_w_in", "sb_w_out", "ab_w_out", "ab_w_pw2")
    own1, got1 = _pair_exchange([d_w1, d_wo1.reshape(N_CHIPS, 256, D), d_wo0.reshape(N_CHIPS, 256, D),
                                 d_wpw.reshape(N_CHIPS, 128, CC)], "pair_exchange1")
    pair1 = [_sum_pair(o, t, BF16, "sum_pair_" + nm) for o, t, nm in zip(own1, got1, early)]
    plan = _scatter_plan([pr[1] for pr in pair1])
    ((dglu0, d_convw, d_small), (dq0, dga0, dkv0, d_sinks)), land1 = _rows_call(
        "bwd0", [_conv_bwd(ds0, cv0, z0, conv_w, p["ab_conv_ln_g"], p["ab_conv_ln_b"]),
                 _swa_bwd(z0, cs, sn, p["ab_sinks"], o0, dmix0, lse0)], plan)
    halves1 = [_sum_shard(pr[0], la, "sum_shard_" + nm) for pr, la, nm in zip(pair1, land1, early)]
    dz0 = jnp.concatenate([dq0, dkv0, dga0, dglu0, dgb0], axis=1)
    d_w0t = _mm([(dz0, xn0)], BF16, "in_proj0_dw", 1408, 512, ta=True)
    own0, got0 = _pair_exchange([d_w0t.reshape(N_CHIPS, 704, D)], "pair_exchange0")
    keep0, send0 = _sum_pair(own0[0], got0[0], BF16, "sum_pair_ab_w_in")
    plan = _scatter_plan([send0])
    dxn0, land0 = _mm([(dz0, w0t)], F32, "in_proj0_dx", 544, 1024, plan=plan)
    half0 = _sum_shard(keep0, land0[0], "sum_shard_ab_w_in")
    dh0_first, grad_x, d_ab_pre = _rms_bwd(dxn0, h0, p["ab_pre_norm"], dh1, F32, "rms_bwd0", split=True)

    grads = {
        "meta_tokens": dh0_first[PAD:TB], "ab_pre_norm": d_ab_pre, "ab_sinks": d_sinks[0:1, 0:8],
        "ab_conv_w": d_convw[0:CONV_W], "ab_conv_b": d_small[0:1], "ab_conv_ln_g": d_small[1:2],
        "ab_conv_ln_b": d_small[2:3], "ab_post_norm": d_ab_post, "sb_pre_norm": d_sb_pre, "sb_post_norm": d_sb_post,
    }
    h_sb_in, h_sb_out, h_ab_out, h_pw2 = halves1
    return loss_row, grad_x, grads, [half0, h_ab_out, h_pw2, h_sb_in, h_sb_out]


SMALL_ROWS = 80
REP_ROWS = 32

WEIGHTS = ["meta_tokens", "ab_pre_norm", "ab_w_in", "ab_sinks", "ab_conv_w", "ab_conv_b", "ab_conv_ln_g",
           "ab_conv_ln_b", "ab_w_pw2", "ab_w_out", "ab_post_norm", "sb_pre_norm", "sb_w_in", "sb_w_out",
           "sb_post_norm"]
BIG = ["ab_w_in", "ab_w_out", "ab_w_pw2", "sb_w_in", "sb_w_out"]


def _pack_small(conv_w, meta, sb_pre, sb_post):
    pad = lambda a, rows: jnp.pad(a, ((0, rows - a.shape[0]), (0, 0)))
    return jnp.concatenate([pad(conv_w, 32), meta.reshape(32, 128), pad(sb_pre.reshape(2, 128), 8),
                            pad(sb_post.reshape(2, 128), 8)], axis=0)


def _unpack_small(s):
    return s[0:31], s[32:64].reshape(16, 256), s[64:66].reshape(1, 256), s[72:74].reshape(1, 256)


REP_LOSS = 3592


def _pack_rep(pre, post, conv_b, ln_g, ln_b, sinks, extra=None):
    flat = jnp.concatenate([pre.reshape(-1), post.reshape(-1), conv_b.reshape(-1), ln_g.reshape(-1),
                            ln_b.reshape(-1), sinks.reshape(-1)] + ([] if extra is None else [extra.reshape(-1)]))
    flat = jnp.concatenate([flat, jnp.zeros((REP_ROWS * 128 - flat.shape[0],), F32)])
    return flat.reshape(REP_ROWS, 128)


def _unpack_rep(r):
    f = r.reshape(-1)
    return (f[0:1024].reshape(1, 1024), f[1024:2048].reshape(1, 1024), f[2048:2560].reshape(1, 512),
            f[2560:3072].reshape(1, 512), f[3072:3584].reshape(1, 512), f[3584:3592].reshape(1, 8))


def _chips_to_cols(w):
    return w.transpose(1, 0, 2).reshape(w.shape[1], -1)


def kernel(x, meta_tokens, ab_pre_norm, ab_w_in, ab_sinks, ab_conv_w, ab_conv_b, ab_conv_ln_g, ab_conv_ln_b, ab_w_pw2, ab_w_out, ab_post_norm, sb_pre_norm, sb_w_in, sb_w_out, sb_post_norm, loss_target, m_meta_tokens, m_ab_pre_norm, m_ab_w_in, m_ab_sinks, m_ab_conv_w, m_ab_conv_b, m_ab_conv_ln_g, m_ab_conv_ln_b, m_ab_w_pw2, m_ab_w_out, m_ab_post_norm, m_sb_pre_norm, m_sb_w_in, m_sb_w_out, m_sb_post_norm, v_meta_tokens, v_ab_pre_norm, v_ab_w_in, v_ab_sinks, v_ab_conv_w, v_ab_conv_b, v_ab_conv_ln_g, v_ab_conv_ln_b, v_ab_w_pw2, v_ab_w_out, v_ab_post_norm, v_sb_pre_norm, v_sb_w_in, v_sb_w_out, v_sb_post_norm):
    w = dict(meta_tokens=meta_tokens, ab_pre_norm=ab_pre_norm, ab_w_in=ab_w_in, ab_sinks=ab_sinks,
             ab_conv_w=ab_conv_w, ab_conv_b=ab_conv_b, ab_conv_ln_g=ab_conv_ln_g, ab_conv_ln_b=ab_conv_ln_b,
             ab_w_pw2=ab_w_pw2, ab_w_out=ab_w_out, ab_post_norm=ab_post_norm, sb_pre_norm=sb_pre_norm,
             sb_w_in=sb_w_in, sb_w_out=sb_w_out, sb_post_norm=sb_post_norm)
    m = dict(meta_tokens=m_meta_tokens, ab_pre_norm=m_ab_pre_norm, ab_w_in=m_ab_w_in, ab_sinks=m_ab_sinks,
             ab_conv_w=m_ab_conv_w, ab_conv_b=m_ab_conv_b, ab_conv_ln_g=m_ab_conv_ln_g,
             ab_conv_ln_b=m_ab_conv_ln_b, ab_w_pw2=m_ab_w_pw2, ab_w_out=m_ab_w_out, ab_post_norm=m_ab_post_norm,
             sb_pre_norm=m_sb_pre_norm, sb_w_in=m_sb_w_in, sb_w_out=m_sb_w_out, sb_post_norm=m_sb_post_norm)
    v = dict(meta_tokens=v_meta_tokens, ab_pre_norm=v_ab_pre_norm, ab_w_in=v_ab_w_in, ab_sinks=v_ab_sinks,
             ab_conv_w=v_ab_conv_w, ab_conv_b=v_ab_conv_b, ab_conv_ln_g=v_ab_conv_ln_g,
             ab_conv_ln_b=v_ab_conv_ln_b, ab_w_pw2=v_ab_w_pw2, ab_w_out=v_ab_w_out, ab_post_norm=v_ab_post_norm,
             sb_pre_norm=v_sb_pre_norm, sb_w_in=v_sb_w_in, sb_w_out=v_sb_w_out, sb_post_norm=v_sb_post_norm)

    def small_of(d):
        return _pack_small(d["ab_conv_w"][0], d["meta_tokens"], d["sb_pre_norm"], d["sb_post_norm"])

    def rep_of(d):
        return _pack_rep(d["ab_pre_norm"], d["ab_post_norm"], d["ab_conv_b"], d["ab_conv_ln_g"], d["ab_conv_ln_b"],
                         d["ab_sinks"])

    g_in0, g_small = _gather_chips([ab_w_in[0].T.astype(BF16), small_of(w)])
    conv_w_f = _chips_to_cols(g_small[:, 0:31])
    meta_f = _chips_to_cols(g_small[:, 32:64].reshape(N_CHIPS, 16, 256))
    sb_pre_f = g_small[:, 64:66].reshape(1, D)
    sb_post_f = g_small[:, 72:74].reshape(1, D)
    full = {
        "meta_tokens": meta_f, "ab_pre_norm": ab_pre_norm, "ab_w_in": g_in0.reshape(AB_IN, D),
        "ab_sinks": ab_sinks, "ab_conv_w": conv_w_f, "ab_conv_b": ab_conv_b, "ab_conv_ln_g": ab_conv_ln_g,
        "ab_conv_ln_b": ab_conv_ln_b, "ab_w_pw2": ab_w_pw2[0].astype(BF16), "ab_w_out": ab_w_out[0].astype(BF16),
        "ab_post_norm": ab_post_norm, "sb_pre_norm": sb_pre_f, "sb_w_in": sb_w_in[0].astype(BF16),
        "sb_w_out": sb_w_out[0].astype(BF16), "sb_post_norm": sb_post_f,
    }

    loss_row, grad_x, g, halves = _local_step(x[0], loss_target[0], full)

    total = _join_cores(halves)

    rep_g = _pack_rep(g["ab_pre_norm"], g["ab_post_norm"], g["ab_conv_b"], g["ab_conv_ln_g"], g["ab_conv_ln_b"],
                      g["ab_sinks"], loss_row[0:1, 0:1])
    vec = jnp.concatenate([rep_g, jnp.pad(g["ab_conv_w"].reshape(124, 128), ((0, 4), (0, 0))),
                           g["meta_tokens"].reshape(128, 128), g["sb_pre_norm"].reshape(8, 128),
                           g["sb_post_norm"].reshape(8, 128)], axis=0)
    vec_sum = _sum8(_gather_all(vec), "sum8_small")

    out_g, out_d, out_m, out_v = {}, {}, {}, {}
    for i, k in enumerate(BIG):
        shp = w[k].shape
        if k == "ab_w_in":
            res = _adamw(w[k][0].T, [total[i]], m[k][0].T, v[k][0].T, "adamw_" + k)
            out_g[k], out_d[k], out_m[k], out_v[k] = [r.T.reshape(shp) for r in res]
            continue
        res = _adamw(w[k][0], [total[i]], m[k][0], v[k][0], "adamw_" + k)
        out_g[k], out_d[k], out_m[k], out_v[k] = [r.reshape(shp) for r in res]

    rep_sum = vec_sum[0:REP_ROWS]
    loss = rep_sum.reshape(-1)[REP_LOSS]
    me = 2 * lax.axis_index("x") + lax.axis_index("y")
    small_sum = _pack_small(
        lax.dynamic_slice_in_dim(vec_sum[32:156].reshape(CONV_W, CC), me * 128, 128, axis=1),
        lax.dynamic_slice_in_dim(vec_sum[160:288].reshape(N_META, D), me * 256, 256, axis=1),
        lax.dynamic_slice_in_dim(vec_sum[288:296].reshape(1, D), me * 256, 256, axis=1),
        lax.dynamic_slice_in_dim(vec_sum[296:304].reshape(1, D), me * 256, 256, axis=1))
    res = _adamw(small_of(w), [small_sum], small_of(m), small_of(v), "adamw_small")
    for dst, r in zip((out_g, out_d, out_m, out_v), res):
        cw, mt, pre, post = _unpack_small(r)
        dst["ab_conv_w"], dst["meta_tokens"], dst["sb_pre_norm"], dst["sb_post_norm"] = cw[None], mt, pre, post
    res = _adamw(rep_of(w), [rep_sum], rep_of(m), rep_of(v), "adamw_rep")
    for dst, r in zip((out_g, out_d, out_m, out_v), res):
        (dst["ab_pre_norm"], dst["ab_post_norm"], dst["ab_conv_b"], dst["ab_conv_ln_g"], dst["ab_conv_ln_b"],
         dst["ab_sinks"]) = _unpack_rep(r)

    return (loss, grad_x[None], *[out_g[k] for k in WEIGHTS], *[out_d[k] for k in WEIGHTS],
            *[out_m[k] for k in WEIGHTS], *[out_v[k] for k in WEIGHTS])
```

```python
import functools

import jax
import jax.numpy as jnp
from jax import lax
from jax.experimental import pallas as pl
from jax.experimental.pallas import tpu as pltpu

F32 = jnp.float32
BF16 = jnp.bfloat16

D = 1024
SEQ = 2048
N_META = 16
TB = 128
TR = 272
PAD = TB - N_META
R = SEQ + TB
NB = R // TB
HD = 64
ROPE_THETA = 10000.0
NORM_EPS = 1e-6
LN_EPS = 1e-5
NEG = -1e30
CONV_W = 31
SCALE = HD ** -0.5
N_CHIPS = 4

C_Q, C_K, C_V, C_GA, C_GLU, C_GB = 0, 512, 640, 768, 1280, 2304
AB_IN = 2816

ADAM_LR, ADAM_B1, ADAM_B2, ADAM_EPS, ADAM_WD, ADAM_STEP = 0.001, 0.9, 0.999, 1e-08, 0.01, 10

VMEM_LIMIT = 56 * 1024 * 1024


def _cp(sem):
    return pltpu.CompilerParams(dimension_semantics=sem, vmem_limit_bytes=VMEM_LIMIT)


def _sig(x):
    return 1.0 / (1.0 + jnp.exp(-x))


def _dot(a, b):
    return lax.dot_general(a, b, (((1,), (0,)), ((), ())), preferred_element_type=F32)


def _dot_nt(a, b):
    return lax.dot_general(a, b, (((1,), (1,)), ((), ())), preferred_element_type=F32)


def _dot_tn(a, b):
    return lax.dot_general(a, b, (((0,), (0,)), ((), ())), preferred_element_type=F32)


def _mm(pairs, out_dtype, name, tm, tn, ta=False, tb=False, plan=None):
    pairs = [(a, b if isinstance(b, tuple) else (b, None)) for a, b in pairs]
    a0, (b0, _) = pairs[0]
    m = a0.shape[1] if ta else a0.shape[0]
    n = b0.shape[-2] if tb else b0.shape[-1]
    npairs = len(pairs)
    dims = (((0 if ta else 1,), (1 if tb else 0,)), ((), ()))
    c_in = len(plan["args"]) if plan else 0
    c_out = len(plan["out_shape"]) if plan else 0
    steps = (m // tm) * (n // tn)

    def body(*refs):
        o_ref = refs[2 * npairs + c_in]
        if plan:
            comm = (refs[2 * npairs:2 * npairs + c_in], refs[2 * npairs + c_in + 1:2 * npairs + c_in + 1 + c_out],
                    refs[2 * npairs + c_in + 1 + c_out:])
            step = pl.program_id(0) * (n // tn) + pl.program_id(1)

            @pl.when(step == 0)
            def _():
                plan["start"](*comm)

        acc = None
        for i in range(npairs):
            t = lax.dot_general(refs[2 * i][...].astype(BF16), refs[2 * i + 1][...].astype(BF16), dims,
                                preferred_element_type=F32)
            acc = t if acc is None else acc + t
        o_ref[...] = acc.astype(out_dtype)
        if plan:
            @pl.when(step == steps - 2)
            def _():
                plan["mid"](*comm)

            @pl.when(step == steps - 1)
            def _():
                plan["finish"](*comm)

    in_specs, args = [], []
    for a, (b, sel) in pairs:
        k = a.shape[0] if ta else a.shape[1]
        in_specs.append(pl.BlockSpec((k, tm), lambda i, j: (0, i)) if ta else pl.BlockSpec((tm, k), lambda i, j: (i, 0)))
        bshape, bidx = ((tn, k), lambda i, j: (j, 0)) if tb else ((k, tn), lambda i, j: (0, j))
        if sel is None:
            in_specs.append(pl.BlockSpec(bshape, bidx))
        else:
            in_specs.append(pl.BlockSpec((None,) + bshape, functools.partial(lambda i, j, f, s: (s,) + f(i, j), f=bidx, s=sel)))
        args += [a, b]
    out_spec = pl.BlockSpec((tm, tn), lambda i, j: (i, j))
    out_shape = jax.ShapeDtypeStruct((m, n), out_dtype)
    if not plan:
        return pl.pallas_call(
            body, grid=(m // tm, n // tn), in_specs=in_specs, out_specs=out_spec, out_shape=out_shape, name=name,
            compiler_params=_cp(("parallel", "parallel")))(*args)
    assert steps >= 2
    res = pl.pallas_call(
        body, grid=(m // tm, n // tn), in_specs=in_specs + [ANY] * c_in, out_specs=[out_spec] + [ANY] * c_out,
        out_shape=[out_shape] + plan["out_shape"], scratch_shapes=plan["sems"], name=name,
        compiler_params=_cp(("arbitrary", "arbitrary")))(*args, *plan["args"])
    return res[0], res[1:]


PW_TM = 544


def _pw2_fwd(s, w, z0):
    def body(s_ref, w_ref, g_ref, t_ref, c_ref):
        t = _dot(s_ref[...], w_ref[...])
        gv = g_ref[...]
        t_ref[...] = t
        c_ref[...] = (t * (gv * _sig(gv))).astype(BF16)

    blk = pl.BlockSpec((PW_TM, CC), lambda i: (i, 0))
    return pl.pallas_call(
        body, grid=(R // PW_TM,),
        in_specs=[blk, pl.BlockSpec((CC, CC), lambda i: (0, 0)), _cols_spec(PW_TM, CC, lambda i: (i, C_GB))],
        out_specs=[blk, blk],
        out_shape=[jax.ShapeDtypeStruct((R, CC), F32), jax.ShapeDtypeStruct((R, CC), BF16)],
        name="pw2_fwd", compiler_params=_cp(("parallel",)))(s, w, z0)


def _pw2_bwd(dmix, t, z0, w, s):
    def body(d_ref, t_ref, g_ref, w_ref, s_ref, dg_ref, ds_ref, dw_ref):
        @pl.when(pl.program_id(0) == 0)
        def _():
            dw_ref[...] = jnp.zeros_like(dw_ref)

        gv, dv = g_ref[...], d_ref[...]
        sg = _sig(gv)
        dg_ref[...] = (dv * t_ref[...] * (sg * (1.0 + gv * (1.0 - sg)))).astype(BF16)
        dt = (dv * (gv * sg)).astype(BF16)
        ds_ref[...] = _dot_nt(dt, w_ref[...])
        dw_ref[...] += _dot_tn(s_ref[...], dt)

    blk = pl.BlockSpec((PW_TM, CC), lambda i: (i, 0))
    full = pl.BlockSpec((CC, CC), lambda i: (0, 0))
    return pl.pallas_call(
        body, grid=(R // PW_TM,),
        in_specs=[_cols_spec(PW_TM, CC, lambda i: (i, CC)), blk, _cols_spec(PW_TM, CC, lambda i: (i, C_GB)), full, blk],
        out_specs=[blk, blk, full],
        out_shape=[jax.ShapeDtypeStruct((R, CC), BF16), jax.ShapeDtypeStruct((R, CC), F32),
                   jax.ShapeDtypeStruct((CC, CC), F32)],
        name="pw2_bwd", compiler_params=_cp(("arbitrary",)))(dmix, t, z0, w, s)


def _rms_fwd(h, g, name):
    def body(h_ref, g_ref, o_ref):
        x = h_ref[...]
        r = lax.rsqrt(jnp.mean(x * x, axis=1, keepdims=True) + NORM_EPS)
        o_ref[...] = (x * r * g_ref[...]).astype(BF16)

    return pl.pallas_call(
        body, grid=(R // TR,),
        in_specs=[pl.BlockSpec((TR, D), lambda n: (n, 0)), pl.BlockSpec((1, D), lambda n: (0, 0))],
        out_specs=pl.BlockSpec((TR, D), lambda n: (n, 0)),
        out_shape=jax.ShapeDtypeStruct((R, D), BF16), name=name, compiler_params=_cp(("parallel",)))(h, g)


def _rms_bwd(dout, x, g, res, out_dtype, name, split=False):
    has_res = res is not None

    def body(*refs):
        if split:
            refs = list(refs)
            dx_rest_ref = refs.pop(-2)
        if has_res:
            d_ref, x_ref, g_ref, r_ref, dx_ref, dg_ref = refs
        else:
            d_ref, x_ref, g_ref, dx_ref, dg_ref = refs
        n = pl.program_id(0)
        xv = x_ref[...]
        dv = d_ref[...]
        r = lax.rsqrt(jnp.mean(xv * xv, axis=1, keepdims=True) + NORM_EPS)
        xh = xv * r
        dxh = dv * g_ref[...]
        dx = r * (dxh - xh * jnp.mean(dxh * xh, axis=1, keepdims=True))
        if has_res:
            dx = dx + r_ref[...]
        row = lax.broadcasted_iota(jnp.int32, (TB, D), 0) + n * TB
        dx = jnp.where(row >= PAD, dx, 0.0).astype(out_dtype)
        if split:
            @pl.when(n == 0)
            def _():
                dx_ref[...] = dx

            @pl.when(n > 0)
            def _():
                dx_rest_ref[...] = dx
        else:
            dx_ref[...] = dx

        @pl.when(n == 0)
        def _():
            dg_ref[...] = jnp.zeros_like(dg_ref)

        dg_ref[...] += jnp.sum(dv * xh, axis=0, keepdims=True)

    blk = pl.BlockSpec((TB, D), lambda n: (n, 0))
    vec = pl.BlockSpec((1, D), lambda n: (0, 0))
    ins = [dout, x, g] + ([res] if has_res else [])
    in_specs = [blk, blk, vec] + ([blk] if has_res else [])
    if split:
        out_specs = [pl.BlockSpec((TB, D), lambda n: (0, 0)), pl.BlockSpec((TB, D), lambda n: (jnp.maximum(n - 1, 0), 0)), vec]
        out_shape = [jax.ShapeDtypeStruct((TB, D), out_dtype), jax.ShapeDtypeStruct((SEQ, D), out_dtype),
                     jax.ShapeDtypeStruct((1, D), F32)]
    else:
        out_specs = [blk, vec]
        out_shape = [jax.ShapeDtypeStruct((R, D), out_dtype), jax.ShapeDtypeStruct((1, D), F32)]
    return pl.pallas_call(
        body, grid=(NB,), in_specs=in_specs, out_specs=out_specs, out_shape=out_shape,
        name=name, compiler_params=_cp(("arbitrary",)))(*ins)


def _post_rms_fwd(xs, w, h, g_post, g_next, name):
    nx = len(xs)

    def body(*refs):
        x_refs, (w_ref, h_ref, gp_ref, gn_ref, y_ref, o_ref, xn_ref) = refs[:nx], refs[nx:]
        yv = _dot(x_refs[0][...], w_ref[0])
        for j in range(1, nx):
            yv = yv + _dot(x_refs[j][...], w_ref[j])
        y_ref[...] = yv
        r = lax.rsqrt(jnp.mean(yv * yv, axis=1, keepdims=True) + NORM_EPS)
        hn = h_ref[...] + yv * r * gp_ref[...]
        o_ref[...] = hn
        r2 = lax.rsqrt(jnp.mean(hn * hn, axis=1, keepdims=True) + NORM_EPS)
        xn_ref[...] = (hn * r2 * gn_ref[...]).astype(BF16)

    blk = pl.BlockSpec((TR, D), lambda n: (n, 0))
    vec = pl.BlockSpec((1, D), lambda n: (0, 0))
    xblk = [pl.BlockSpec((TR, x.shape[1]), lambda n: (n, 0)) for x in xs]
    return pl.pallas_call(
        body, grid=(R // TR,), in_specs=xblk + [pl.BlockSpec(w.shape, lambda n: (0, 0, 0)), blk, vec, vec],
        out_specs=[blk, blk, blk],
        out_shape=[jax.ShapeDtypeStruct((R, D), F32), jax.ShapeDtypeStruct((R, D), F32),
                   jax.ShapeDtypeStruct((R, D), BF16)],
        name=name, compiler_params=_cp(("parallel",)))(*xs, w, h, g_post, g_next)


def _rms_post_bwd(dzs, w, h, g, res, y, g_post, name):
    nz = len(dzs)

    def body(*refs):
        dz_refs, (w_ref, h_ref, g_ref, r_ref, y_ref, gp_ref, dh_ref, dg_ref, dy_ref, dgp_ref) = refs[:nz], refs[nz:]
        n = pl.program_id(0)

        @pl.when(n == 0)
        def _():
            dg_ref[...] = jnp.zeros_like(dg_ref)
            dgp_ref[...] = jnp.zeros_like(dgp_ref)

        dv = _dot_nt(dz_refs[0][...], w_ref[0])
        for j in range(1, nz):
            dv = dv + _dot_nt(dz_refs[j][...], w_ref[j])
        hv = h_ref[...]
        r = lax.rsqrt(jnp.mean(hv * hv, axis=1, keepdims=True) + NORM_EPS)
        xh = hv * r
        dxh = dv * g_ref[...]
        dh = r * (dxh - xh * jnp.mean(dxh * xh, axis=1, keepdims=True)) + r_ref[...]
        row = lax.broadcasted_iota(jnp.int32, (TR, D), 0) + n * TR
        dh = jnp.where(row >= PAD, dh, 0.0)
        dh_ref[...] = dh
        dg_ref[...] += jnp.sum(dv * xh, axis=0, keepdims=True)
        yv = y_ref[...]
        ry = lax.rsqrt(jnp.mean(yv * yv, axis=1, keepdims=True) + NORM_EPS)
        yh = yv * ry
        dyh = dh * gp_ref[...]
        dy_ref[...] = (ry * (dyh - yh * jnp.mean(dyh * yh, axis=1, keepdims=True))).astype(BF16)
        dgp_ref[...] += jnp.sum(dh * yh, axis=0, keepdims=True)

    blk = pl.BlockSpec((TR, D), lambda n: (n, 0))
    vec = pl.BlockSpec((1, D), lambda n: (0, 0))
    return pl.pallas_call(
        body, grid=(R // TR,),
        in_specs=[blk] * nz + [pl.BlockSpec(w.shape, lambda n: (0, 0, 0)), blk, vec, blk, blk, vec],
        out_specs=[blk, vec, blk, vec],
        out_shape=[jax.ShapeDtypeStruct((R, D), F32), jax.ShapeDtypeStruct((1, D), F32),
                   jax.ShapeDtypeStruct((R, D), BF16), jax.ShapeDtypeStruct((1, D), F32)],
        name=name, compiler_params=_cp(("arbitrary",)))(*dzs, w, h, g, res, y, g_post)


def _cols_spec(rows, width, where):
    def index(*g):
        r, c = where(*g)
        return r * rows, (c if isinstance(c, int) else pl.multiple_of(c, 128))
    return pl.BlockSpec((pl.Element(rows), pl.Element(width)), index)


def _tail(x, w, h, g, target):
    steps = R // TR

    def body(x_ref, w_ref, h_ref, g_ref, t_ref, d_ref, dy_ref, dg_ref, l_ref):
        n = pl.program_id(0)

        @pl.when(n == 0)
        def _():
            dg_ref[...] = jnp.zeros_like(dg_ref)
            l_ref[...] = jnp.zeros_like(l_ref)

        yv = _dot(x_ref[...], w_ref[...])
        r = lax.rsqrt(jnp.mean(yv * yv, axis=1, keepdims=True) + NORM_EPS)
        yh = yv * r
        tv = t_ref[...]
        tv = jnp.where(n == 0, jnp.concatenate([jnp.zeros((TB, D), F32), tv[0:TR - TB]], axis=0), tv)
        row = lax.broadcasted_iota(jnp.int32, (TR, D), 0) + n * TR
        err = jnp.where(row >= TB, (h_ref[...] + yh * g_ref[...]) - tv, 0.0)
        dv = err * (1.0 / D)
        d_ref[...] = dv
        l_ref[...] += jnp.sum(err * err, axis=0, keepdims=True)
        dyh = dv * g_ref[...]
        dy_ref[...] = (r * (dyh - yh * jnp.mean(dyh * yh, axis=1, keepdims=True))).astype(BF16)
        dg_ref[...] += jnp.sum(dv * yh, axis=0, keepdims=True)

        @pl.when(n == steps - 1)
        def _():
            tot = jnp.sum(l_ref[...], axis=1, keepdims=True) * (0.5 / D)
            l_ref[...] = jnp.broadcast_to(tot, (1, D))

    blk = pl.BlockSpec((TR, D), lambda n: (n, 0))
    vec = pl.BlockSpec((1, D), lambda n: (0, 0))
    tgt = pl.BlockSpec((pl.Element(TR), pl.Element(D)),
                       lambda n: (pl.multiple_of(jnp.maximum(TR * n - TB, 0), 8), 0))
    return pl.pallas_call(
        body, grid=(steps,),
        in_specs=[pl.BlockSpec((TR, x.shape[1]), lambda n: (n, 0)), pl.BlockSpec(w.shape, lambda n: (0, 0)), blk, vec, tgt],
        out_specs=[blk, blk, vec, vec],
        out_shape=[jax.ShapeDtypeStruct((R, D), F32), jax.ShapeDtypeStruct((R, D), BF16),
                   jax.ShapeDtypeStruct((1, D), F32), jax.ShapeDtypeStruct((1, D), F32)],
        name="out_proj1_tail", compiler_params=_cp(("arbitrary",)))(x, w, h, g, target)


def _lane_row(shape):
    return lax.broadcasted_iota(jnp.int32, shape, 1), lax.broadcasted_iota(jnp.int32, shape, 0)


def _rot_half(x, lane):
    return jnp.where(lane % HD < HD // 2, pltpu.roll(x, 128 - HD // 2, 1), pltpu.roll(x, HD // 2, 1))


def _swa_blocks(n):
    return (0, jnp.maximum(n - 1, 0), n)


SWA_STACKS = ((0, 0), (0, 1), (1, 0), (1, 1))


def _swa_masks(n, lane, row):
    qpos = n * TB + (row & (TB - 1))
    kp = (n - 1) * TB + lane
    kc = n * TB + lane
    m0 = (lane >= PAD) & (qpos - lane >= TB)
    mp = (kp >= PAD) & (qpos >= kp) & (qpos - kp < TB)
    mc = (kc >= PAD) & (qpos >= kc)
    return (m0, mp, mc)


def _stack_pair(xa, xb, par):
    lane = lax.broadcasted_iota(jnp.int32, (TB, 128), 1)
    keep = (lane < HD) if par == 0 else (lane >= HD)
    return jnp.concatenate([jnp.where(keep, xa, 0.0), jnp.where(keep, xb, 0.0)], axis=0)


def _per_head(a, b):
    row = lax.broadcasted_iota(jnp.int32, (2 * TB, 1), 0)
    return jnp.where(row < TB, a, b)


def _swa_load(n, zq_ref, zkv_ref, cs_ref, sn_ref, lane):
    r0 = pl.multiple_of(n * TB, TB)
    csq, snq = cs_ref[pl.ds(r0, TB), :], sn_ref[pl.ds(r0, TB), :]
    qc = []
    for c in range(4):
        x = zq_ref[:, c * 128:(c + 1) * 128]
        qc.append((x * csq + _rot_half(x, lane) * snq) * SCALE)
    qst = [_stack_pair(qc[2 * g], qc[2 * g + 1], par).astype(BF16) for g, par in SWA_STACKS]
    kvs = []
    for b in _swa_blocks(n):
        b0 = pl.multiple_of(b * TB, TB)
        csb, snb = cs_ref[pl.ds(b0, TB), :], sn_ref[pl.ds(b0, TB), :]
        kx = zkv_ref[pl.ds(b0, TB), 0:128]
        kr = kx * csb + _rot_half(kx, lane) * snb
        vx = zkv_ref[pl.ds(b0, TB), 128:256]
        kvs.append((kr.astype(BF16), pltpu.roll(kr, HD, 1).astype(BF16),
                    vx.astype(BF16), pltpu.roll(vx, HD, 1).astype(BF16), csb, snb, b0))
    return qst, (csq, snq), kvs


def _swa_fwd(z0, cs, sn, sinks):
    def body(zq_ref, zkv_ref, cs_ref, sn_ref, sk_ref, ga_ref, o_ref, a_ref, lse_ref):
        n = pl.program_id(0)
        lane, row = _lane_row((TB, 128))
        lo = lane < HD
        masks = _swa_masks(n, *_lane_row((2 * TB, 128)))
        qst, _, kvs = _swa_load(n, zq_ref, zkv_ref, cs_ref, sn_ref, lane)
        ss = [[jnp.where(m, _dot_nt(qst[si], k if par == g else ka), NEG)
               for (k, ka, _, _, _, _, _), m in zip(kvs, masks)] for si, (g, par) in enumerate(SWA_STACKS)]
        o2, lse2 = [], []
        for si, (g, par) in enumerate(SWA_STACKS):
            sink = _per_head(sk_ref[0, 4 * g + par], sk_ref[0, 4 * g + 2 + par])
            s = ss[si]
            mx = jnp.maximum(jnp.maximum(jnp.max(s[0], axis=1, keepdims=True), jnp.max(s[1], axis=1, keepdims=True)),
                             jnp.max(s[2], axis=1, keepdims=True))
            mx = jnp.maximum(mx, sink)
            es = [jnp.exp(sb - mx) for sb in s]
            den = (jnp.sum(es[0], axis=1, keepdims=True) + jnp.sum(es[1], axis=1, keepdims=True)
                   + jnp.sum(es[2], axis=1, keepdims=True) + jnp.exp(sink - mx))
            inv = 1.0 / den
            t = jnp.zeros((2 * TB, 128), F32)
            for (_, _, v, va, _, _, _), e in zip(kvs, es):
                t = t + _dot((e * inv).astype(BF16), v if par == g else va)
            o2.append(t)
            lse2.append(mx + jnp.log(den))
        lse_t = jnp.zeros((TB, 128), F32)
        for g in range(2):
            for t in range(2):
                rows = slice(t * TB, (t + 1) * TB)
                c = 2 * g + t
                oc = jnp.where(lo, o2[2 * g][rows], o2[2 * g + 1][rows])
                o_ref[:, c * 128:(c + 1) * 128] = oc
                gv = ga_ref[:, c * 128:(c + 1) * 128]
                a_ref[:, c * 128:(c + 1) * 128] = (oc * (gv * _sig(gv))).astype(BF16)
                for par in range(2):
                    lse_t = jnp.where(lane == 4 * g + 2 * t + par, lse2[2 * g + par][rows], lse_t)
        lse_ref[...] = lse_t

    full = pl.BlockSpec((R, 128), lambda n: (0, 0))
    return dict(
        body=body,
        in_specs=[pl.BlockSpec((TB, 512), lambda n: (n, C_Q // 512)),
                  pl.BlockSpec((R, 256), lambda n: (0, C_K // 256)), full, full,
                  pl.BlockSpec(memory_space=pltpu.SMEM), _cols_spec(TB, 512, lambda n: (n, C_GA))],
        args=[z0, z0, cs, sn, sinks, z0],
        out_specs=[pl.BlockSpec((TB, 512), lambda n: (n, 0)), pl.BlockSpec((TB, 512), lambda n: (n, 0)),
                   pl.BlockSpec((TB, 128), lambda n: (n, 0))],
        out_shape=[jax.ShapeDtypeStruct((R, 512), F32), jax.ShapeDtypeStruct((R, 512), BF16),
                   jax.ShapeDtypeStruct((R, 128), F32)],
        scratch=[])


def _swa_bwd(z0, cs, sn, sinks, o, dmix, lse):
    def body(zq_ref, zkv_ref, cs_ref, sn_ref, sk_ref, ga_ref, o_ref, dm_ref, lse_ref,
             dq_ref, dga_ref, dkv_ref, dsk_ref, do_ref, acc_ref):
        n = pl.program_id(0)

        @pl.when(n == 0)
        def _():
            acc_ref[...] = jnp.zeros_like(acc_ref)
            dsk_ref[...] = jnp.zeros_like(dsk_ref)

        gv, dmv = ga_ref[...], dm_ref[...]
        sg = _sig(gv)
        dga_ref[...] = (dmv * o_ref[...] * (sg * (1.0 + gv * (1.0 - sg)))).astype(BF16)
        do_ref[...] = dmv * (gv * sg)
        lane, row = _lane_row((TB, 128))
        lo = lane < HD
        masks = _swa_masks(n, *_lane_row((2 * TB, 128)))
        qst, (csq, snq), kvs = _swa_load(n, zq_ref, zkv_ref, cs_ref, sn_ref, lane)
        lse_t = lse_ref[...]
        ss = [[jnp.where(m, _dot_nt(qst[si], k if par == g else ka), NEG)
               for (k, ka, _, _, _, _, _), m in zip(kvs, masks)] for si, (g, par) in enumerate(SWA_STACKS)]
        dobs, deltas, lses, dps = [], [], [], []
        for g, par in SWA_STACKS:
            ca, cb = slice(2 * g * 128, (2 * g + 1) * 128), slice((2 * g + 1) * 128, (2 * g + 2) * 128)
            dom = _stack_pair(do_ref[:, ca], do_ref[:, cb], par)
            deltas.append(jnp.sum(dom * jnp.concatenate([o_ref[:, ca], o_ref[:, cb]], axis=0), axis=1, keepdims=True))
            dob = dom.astype(BF16)
            dobs.append(dob)
            lses.append(jnp.concatenate(
                [jnp.sum(jnp.where(lane == 4 * g + 2 * t + par, lse_t, 0.0), axis=1, keepdims=True) for t in range(2)],
                axis=0))
            dps.append([_dot_nt(dob, v if par == g else va) for (_, _, v, va, _, _, _) in kvs])
        dk_al = [jnp.zeros((TB, 128), F32) for _ in range(3)]
        dk_mis = [jnp.zeros((TB, 128), F32) for _ in range(3)]
        dv_al = [jnp.zeros((TB, 128), F32) for _ in range(3)]
        dv_mis = [jnp.zeros((TB, 128), F32) for _ in range(3)]
        dsk_t = jnp.zeros((TB, 128), F32)
        dq2 = []
        for si, (g, par) in enumerate(SWA_STACKS):
            dqt = jnp.zeros((2 * TB, 128), F32)
            for bi, (k, ka, _, _, _, _, _) in enumerate(kvs):
                p = jnp.exp(ss[si][bi] - lses[si])
                ds = (p * (dps[si][bi] - deltas[si])).astype(BF16)
                dqt = dqt + _dot(ds, k if par == g else ka)
                dkh = _dot_tn(ds, qst[si])
                dvh = _dot_tn(p.astype(BF16), dobs[si])
                if par == g:
                    dk_al[bi] = dk_al[bi] + dkh
                    dv_al[bi] = dv_al[bi] + dvh
                else:
                    dk_mis[bi] = dk_mis[bi] + dkh
                    dv_mis[bi] = dv_mis[bi] + dvh
            dq2.append(dqt)
            sink = _per_head(sk_ref[0, 4 * g + par], sk_ref[0, 4 * g + 2 + par])
            dsk = -jnp.exp(sink - lses[si]) * deltas[si]
            for t in range(2):
                dsk_t = jnp.where(lane == 4 * g + 2 * t + par, dsk[t * TB:(t + 1) * TB], dsk_t)
        for g in range(2):
            for t in range(2):
                rows = slice(t * TB, (t + 1) * TB)
                c = 2 * g + t
                dqc = jnp.where(lo, dq2[2 * g][rows], dq2[2 * g + 1][rows]) * SCALE
                dq_ref[:, c * 128:(c + 1) * 128] = (dqc * csq + _rot_half(dqc * snq, lane)).astype(BF16)
        for bi, (_, _, _, _, csb, snb, b0) in enumerate(kvs):
            dk = dk_al[bi] + pltpu.roll(dk_mis[bi], HD, 1)
            dv = dv_al[bi] + pltpu.roll(dv_mis[bi], HD, 1)
            acc_ref[pl.ds(b0, TB), 0:128] += dk * csb + _rot_half(dk * snb, lane)
            acc_ref[pl.ds(b0, TB), 128:256] += dv
        dsk_ref[0:1, :] += jnp.sum(dsk_t, axis=0, keepdims=True)

        @pl.when(n == NB - 1)
        def _():
            dkv_ref[...] = acc_ref[...].astype(BF16)

    full = pl.BlockSpec((R, 128), lambda n: (0, 0))
    b512 = pl.BlockSpec((TB, 512), lambda n: (n, 0))
    return dict(
        body=body,
        in_specs=[pl.BlockSpec((TB, 512), lambda n: (n, C_Q // 512)),
                  pl.BlockSpec((R, 256), lambda n: (0, C_K // 256)), full, full,
                  pl.BlockSpec(memory_space=pltpu.SMEM), _cols_spec(TB, 512, lambda n: (n, C_GA)),
                  b512, b512, pl.BlockSpec((TB, 128), lambda n: (n, 0))],
        args=[z0, z0, cs, sn, sinks, z0, o, dmix, lse],
        out_specs=[b512, b512, pl.BlockSpec((R, 256), lambda n: (0, 0)), pl.BlockSpec((8, 128), lambda n: (0, 0))],
        out_shape=[jax.ShapeDtypeStruct((R, 512), BF16), jax.ShapeDtypeStruct((R, 512), BF16),
                   jax.ShapeDtypeStruct((R, 256), BF16), jax.ShapeDtypeStruct((8, 128), F32)],
        scratch=[pltpu.VMEM((TB, 512), F32), pltpu.VMEM((R, 256), F32)])


CC = 512
HALO = CONV_W - 1


def _conv_fwd(z0, conv_w, conv_b, ln_g, ln_b):
    def body(g_ref, w_ref, cb_ref, lg_ref, lb_ref, cv_ref, s_ref, ubuf):
        n = pl.program_id(0)

        @pl.when(n == 0)
        def _():
            ubuf[...] = jnp.zeros_like(ubuf)

        u = g_ref[:, 0:CC] * _sig(g_ref[:, CC:2 * CC])
        for k in range(8):
            ubuf[k, 0:TB + 8, :] = ubuf[k, TB:2 * TB + 8, :]
            ubuf[k, pl.ds(TB + 8 - k, TB), :] = u
        acc = jnp.zeros((TB, CC), F32)
        for w in range(CONV_W):
            off = TB - HALO + w
            acc = acc + ubuf[off % 8, pl.ds(off + 8 - off % 8, TB), :] * w_ref[w:w + 1, :]
        cv = acc + cb_ref[...]
        cv_ref[...] = cv
        xc = cv - jnp.mean(cv, axis=1, keepdims=True)
        rs = lax.rsqrt(jnp.mean(xc * xc, axis=1, keepdims=True) + LN_EPS)
        ln = xc * rs * lg_ref[...] + lb_ref[...]
        s_ref[...] = (ln * _sig(ln)).astype(BF16)

    vec = pl.BlockSpec((1, CC), lambda n: (0, 0))
    blk = pl.BlockSpec((TB, CC), lambda n: (n, 0))
    return dict(
        body=body,
        in_specs=[_cols_spec(TB, 2 * CC, lambda n: (n, C_GLU)),
                  pl.BlockSpec((32, CC), lambda n: (0, 0)), vec, vec, vec],
        args=[z0, conv_w, conv_b, ln_g, ln_b],
        out_specs=[blk, blk],
        out_shape=[jax.ShapeDtypeStruct((R, CC), F32), jax.ShapeDtypeStruct((R, CC), BF16)],
        scratch=[pltpu.VMEM((8, 2 * TB + 8, CC), F32)])


def _conv_bwd(ds, cv, z0, conv_w, ln_g, ln_b):
    def body(ds_ref, cv_ref, g_ref, w_ref, lg_ref, lb_ref, dglu_ref, dw_ref, dsm_ref, dbuf):
        n = pl.program_id(0)

        @pl.when(n == 0)
        def _():
            dbuf[...] = jnp.zeros_like(dbuf)
            dw_ref[...] = jnp.zeros_like(dw_ref)
            dsm_ref[...] = jnp.zeros_like(dsm_ref)

        cv = cv_ref[...]
        xc = cv - jnp.mean(cv, axis=1, keepdims=True)
        rs = lax.rsqrt(jnp.mean(xc * xc, axis=1, keepdims=True) + LN_EPS)
        xh = xc * rs
        ln = xh * lg_ref[...] + lb_ref[...]
        sg = _sig(ln)
        dln = ds_ref[...] * (sg * (1.0 + ln * (1.0 - sg)))
        dxh = dln * lg_ref[...]
        dcv = rs * (dxh - jnp.mean(dxh, axis=1, keepdims=True) - xh * jnp.mean(dxh * xh, axis=1, keepdims=True))
        dsm_ref[0:1, :] += jnp.sum(dcv, axis=0, keepdims=True)
        dsm_ref[1:2, :] += jnp.sum(dln * xh, axis=0, keepdims=True)
        dsm_ref[2:3, :] += jnp.sum(dln, axis=0, keepdims=True)
        for k in range(8):
            dbuf[k, TB:2 * TB + 8, :] = dbuf[k, 0:TB + 8, :]
            dbuf[k, pl.ds(8 - k, TB), :] = dcv
        a = g_ref[:, 0:CC]
        sb = _sig(g_ref[:, CC:2 * CC])
        u = a * sb
        du = jnp.zeros((TB, CC), F32)
        for w in range(CONV_W):
            off = HALO - w
            sh = dbuf[off % 8, pl.ds(off + 8 - off % 8, TB), :]
            du = du + sh * w_ref[w:w + 1, :]
            dw_ref[w:w + 1, :] += jnp.sum(u * sh, axis=0, keepdims=True)
        dglu_ref[:, 0:CC] = (du * sb).astype(BF16)
        dglu_ref[:, CC:2 * CC] = (du * a * sb * (1.0 - sb)).astype(BF16)

    rev = lambda n: (NB - 1 - n, 0)
    vec = pl.BlockSpec((1, CC), lambda n: (0, 0))
    blk = pl.BlockSpec((TB, CC), rev)
    return dict(
        body=body,
        in_specs=[blk, blk, _cols_spec(TB, 2 * CC, lambda n: (NB - 1 - n, C_GLU)),
                  pl.BlockSpec((32, CC), lambda n: (0, 0)), vec, vec],
        args=[ds, cv, z0, conv_w, ln_g, ln_b],
        out_specs=[pl.BlockSpec((TB, 2 * CC), rev), pl.BlockSpec((32, CC), lambda n: (0, 0)),
                   pl.BlockSpec((8, CC), lambda n: (0, 0))],
        out_shape=[jax.ShapeDtypeStruct((R, 2 * CC), BF16), jax.ShapeDtypeStruct((32, CC), F32),
                   jax.ShapeDtypeStruct((8, CC), F32)],
        scratch=[pltpu.VMEM((8, 2 * TB + 8, CC), F32)])


def _split_dot(x, t):
    hi = x.astype(BF16)
    lo = (x - hi.astype(F32)).astype(BF16)
    return _dot(hi, t) + _dot(lo, t)


def _stack_heads(x):
    lane = lax.broadcasted_iota(jnp.int32, (TB, 128), 1)
    return jnp.concatenate([jnp.where(lane < HD, x, 0.0), jnp.where(lane < HD, 0.0, x)], axis=0).astype(BF16)


def _sb_stack(qv, i):
    lane2, row2 = _lane_row((2 * TB, 128))
    qpos2 = i * TB + (row2 & (TB - 1))
    lane, row = _lane_row((TB, 128))
    return _stack_heads(qv), lane2, qpos2, (row > lane).astype(BF16)


SB_U = 3
SB_DEAD = -104.0
SB_P = 4


def _sb_fwd(q, k, v, g):
    def body(q_ref, k_ref, v_ref, g_ref, o_ref, m_ref, c_ref, n_ref):
        p, i = pl.program_id(0), pl.program_id(1)
        lane, row = _lane_row((TB, 128))
        lo = lane < HD
        slabs = [slice(s * 128, (s + 1) * 128) for s in range(SB_P)]
        q2s = []
        for sl in slabs:
            q2, lane2, qpos2, tri_gt = _sb_stack(q_ref[:, sl].astype(F32) * SCALE, i)
            q2s.append(q2)

        def cond(st):
            t, _, c2s = st
            alive = jnp.max(c2s[0])
            for c2 in c2s[1:]:
                alive = jnp.maximum(alive, jnp.max(c2))
            return jnp.logical_and(i - SB_U * t >= 0, alive > SB_DEAD)

        def step(st):
            t, accs, c2s = st
            accs, c2s = list(accs), list(c2s)
            jrs = [i - SB_U * t - u for u in range(SB_U)]
            j0s = [pl.multiple_of(jnp.maximum(jr, 0) * TB, TB) for jr in jrs]
            valids = []
            for jr in jrs:
                kpos = jr * TB + lane2
                valids.append((kpos >= PAD) & (kpos < qpos2))
            zs = [[jnp.where(valid, _dot_nt(q2s[s], k_ref[pl.ds(j0, TB), slabs[s]]), NEG)
                   for j0, valid in zip(j0s, valids)] for s in range(SB_P)]
            lbs, l1s = [], []
            for s in range(SB_P):
                lbs.append([jnp.minimum(z, 0.0) - jnp.log(1.0 + jnp.exp(-jnp.abs(z))) for z in zs[s]])
                l1s.append([lb - z for lb, z in zip(lbs[s], zs[s])])
            sfxs = [[_split_dot(l1, tri_gt) for l1 in l1s[s]] for s in range(SB_P)]
            carries = []
            for s in range(SB_P):
                cs, c2 = [], c2s[s]
                for jr, l1 in zip(jrs, l1s[s]):
                    cs.append(c2)
                    c_ref[:, slabs[s]] = jnp.where(lane == 2 * jr, c2[0:TB],
                                                   jnp.where(lane == 2 * jr + 1, c2[TB:2 * TB], c_ref[:, slabs[s]]))
                    c2 = c2 + jnp.sum(l1, axis=1, keepdims=True)
                carries.append(cs)
                c2s[s] = c2
            for s in range(SB_P):
                for j0, valid, lb, sfx, cu in zip(j0s, valids, lbs[s], sfxs[s], carries[s]):
                    a = jnp.exp(lb + sfx + cu).astype(BF16)
                    av = _dot(a, v_ref[pl.ds(j0, TB), slabs[s]])
                    accs[s] = accs[s] + jnp.where(lo, av[0:TB], av[TB:2 * TB])
            return t + 1, tuple(accs), tuple(c2s)

        c_ref[...] = jnp.zeros_like(c_ref)
        init = (jnp.int32(0), tuple(jnp.zeros((TB, 128), F32) for _ in slabs),
                tuple(jnp.zeros((2 * TB, 1), F32) for _ in slabs))
        t, accs, _ = lax.while_loop(cond, step, init)
        for sl, acc in zip(slabs, accs):
            o_ref[:, sl] = acc
            gv = g_ref[:, sl]
            m_ref[:, sl] = (acc * (gv * _sig(gv))).astype(BF16)
        n_ref[p, i] = t

    wide = SB_P * 128
    slab = pl.BlockSpec((R, wide), lambda p, i: (0, p))
    blk = pl.BlockSpec((TB, wide), lambda p, i: (i, p))
    sd = jax.ShapeDtypeStruct((R, D), F32)
    return pl.pallas_call(
        body, grid=(D // wide, NB), in_specs=[blk, slab, slab, blk],
        out_specs=[blk, blk, blk, pl.BlockSpec(memory_space=pltpu.SMEM)],
        out_shape=[sd, jax.ShapeDtypeStruct((R, D), BF16), sd, jax.ShapeDtypeStruct((D // wide, NB), jnp.int32)],
        name="sb_fwd", compiler_params=_cp(("arbitrary", "arbitrary")))(q, k, v, g)


def _sb_bwd(trips, q, k, v, car, dm, g, o):
    def body(n_ref, q_ref, k_ref, v_ref, c_ref, dm_ref, g_ref, o_ref, dq_ref, dko_ref, dvo_ref, dg_ref,
             dk_ref, dv_ref):
        p, i = pl.program_id(0), pl.program_id(1)

        @pl.when(i == 0)
        def _():
            dk_ref[...] = jnp.zeros_like(dk_ref)
            dv_ref[...] = jnp.zeros_like(dv_ref)

        lane, row = _lane_row((TB, 128))
        lo = lane < HD
        tri_lt = (row < lane).astype(BF16)
        slabs = [slice(s * 128, (s + 1) * 128) for s in range(SB_P)]
        q2s, do2s, cts = [], [], []
        for sl in slabs:
            q2, lane2, qpos2, tri_gt = _sb_stack(q_ref[:, sl].astype(F32) * SCALE, i)
            q2s.append(q2)
            gv, dmv = g_ref[:, sl], dm_ref[:, sl]
            sg = _sig(gv)
            dg_ref[:, sl] = (dmv * o_ref[:, sl] * (sg * (1.0 + gv * (1.0 - sg)))).astype(BF16)
            do2s.append(_stack_heads(dmv * (gv * sg)))
            cts.append(c_ref[:, sl])
        trips_i = n_ref[p, i]
        first = jnp.maximum(i + 1 - SB_U * trips_i, 0)

        def step(t, carry):
            dqs, g2s = carry
            dqs, g2s = list(dqs), list(g2s)
            jrs = [first + SB_U * t + u for u in range(SB_U)]
            j0s = [pl.multiple_of(jnp.minimum(jr, i) * TB, TB) for jr in jrs]
            valids = []
            for jr in jrs:
                kpos = jr * TB + lane2
                valids.append((kpos >= PAD) & (kpos < qpos2))
            ks = [[k_ref[pl.ds(j0, TB), sl] for j0 in j0s] for sl in slabs]
            zs = [[jnp.where(valid, _dot_nt(q2s[s], kj), NEG) for kj, valid in zip(ks[s], valids)] for s in range(SB_P)]
            das = [[_dot_nt(do2s[s], v_ref[pl.ds(j0, TB), slabs[s]]) for j0 in j0s] for s in range(SB_P)]
            es = [[jnp.exp(-jnp.abs(z)) for z in zs[s]] for s in range(SB_P)]
            lbs = [[jnp.minimum(z, 0.0) - jnp.log(1.0 + e) for z, e in zip(zs[s], es[s])] for s in range(SB_P)]
            l1s = [[lb - z for lb, z in zip(lbs[s], zs[s])] for s in range(SB_P)]
            sfxs = [[_split_dot(l1, tri_gt) for l1 in l1s[s]] for s in range(SB_P)]
            a_s, gmats, gpres = [], [], []
            for s in range(SB_P):
                a_l, gm_l, gp_l, g2 = [], [], [], g2s[s]
                for jr, valid, lb, sfx, da in zip(jrs, valids, lbs[s], sfxs[s], das[s]):
                    later = jnp.concatenate(
                        [jnp.sum(jnp.where(lane == 2 * jr + hh, cts[s], 0.0), axis=1, keepdims=True) for hh in range(2)],
                        axis=0)
                    a = jnp.exp(lb + sfx + later)
                    gmat = da * a
                    a_l.append(a.astype(BF16))
                    gm_l.append(gmat)
                    gp_l.append(g2)
                    g2 = g2 + jnp.sum(gmat, axis=1, keepdims=True)
                a_s.append(a_l)
                gmats.append(gm_l)
                gpres.append(gp_l)
                g2s[s] = g2
            pres = [[gp + _split_dot(gmat, tri_lt) for gp, gmat in zip(gpres[s], gmats[s])] for s in range(SB_P)]
            for s in range(SB_P):
                for j0, kj, valid, z, e, gmat, pre, a in zip(j0s, ks[s], valids, zs[s], es[s], gmats[s], pres[s], a_s[s]):
                    r = 1.0 / (1.0 + e)
                    big = z >= 0.0
                    beta = jnp.where(big, r, e * r)
                    omb = jnp.where(big, e * r, r)
                    dz = (gmat * omb - beta * pre).astype(BF16)
                    dq2 = _dot(dz, kj)
                    dqs[s] = dqs[s] + jnp.where(lo, dq2[0:TB], dq2[TB:2 * TB])
                    dk_ref[pl.ds(j0, TB), slabs[s]] += _dot_tn(dz, q2s[s])
                    dv_ref[pl.ds(j0, TB), slabs[s]] += _dot_tn(a, do2s[s])
            return tuple(dqs), tuple(g2s)

        init = (tuple(jnp.zeros((TB, 128), F32) for _ in slabs), tuple(jnp.zeros((2 * TB, 1), F32) for _ in slabs))
        dqs, _ = lax.fori_loop(0, trips_i, step, init)
        for sl, dq in zip(slabs, dqs):
            dq_ref[:, sl] = (dq * SCALE).astype(BF16)

        @pl.when(i == NB - 1)
        def _():
            dko_ref[...] = dk_ref[...].astype(BF16)
            dvo_ref[...] = dv_ref[...].astype(BF16)

    wide = SB_P * 128
    slab = pl.BlockSpec((R, wide), lambda p, i: (0, p))
    blk = pl.BlockSpec((TB, wide), lambda p, i: (i, p))
    sd = jax.ShapeDtypeStruct((R, D), BF16)
    return pl.pallas_call(
        body, grid=(D // wide, NB),
        in_specs=[pl.BlockSpec(memory_space=pltpu.SMEM), blk, slab, slab, blk, blk, blk, blk],
        out_specs=[blk, slab, slab, blk], out_shape=[sd, sd, sd, sd],
        scratch_shapes=[pltpu.VMEM((R, wide), F32), pltpu.VMEM((R, wide), F32)], name="sb_bwd",
        compiler_params=_cp(("arbitrary", "arbitrary")))(trips, q, k, v, car, dm, g, o)


def _adamw(w, parts, m, v, name):
    rows, cols = w.shape
    tr = next((t for t in (256, 176) if rows % t == 0), rows)
    nparts = len(parts)

    def body(*refs):
        w_ref = refs[0]
        p_refs = refs[1:1 + nparts]
        m_ref, v_ref, g_ref, d_ref, nm_ref, nv_ref = refs[1 + nparts:]
        g = p_refs[0][...]
        for p_ref in p_refs[1:]:
            g = g + p_ref[...]
        nm = ADAM_B1 * m_ref[...] + (1.0 - ADAM_B1) * g
        nv = ADAM_B2 * v_ref[...] + (1.0 - ADAM_B2) * (g * g)
        m_hat = nm / (1.0 - ADAM_B1 ** ADAM_STEP)
        v_hat = nv / (1.0 - ADAM_B2 ** ADAM_STEP)
        g_ref[...] = g
        d_ref[...] = -ADAM_LR * (m_hat / (jnp.sqrt(v_hat) + ADAM_EPS) + ADAM_WD * w_ref[...])
        nm_ref[...] = nm
        nv_ref[...] = nv

    blk = pl.BlockSpec((tr, cols), lambda i: (i, 0))
    sd = jax.ShapeDtypeStruct((rows, cols), F32)
    return pl.pallas_call(
        body, grid=(rows // tr,), in_specs=[blk] * (3 + nparts), out_specs=[blk] * 4, out_shape=[sd] * 4,
        name=name, compiler_params=_cp(("parallel",)))(w, *parts, m, v)


def _sum8(buf, name):
    _, rows, cols = buf.shape

    def body(b_ref, o_ref):
        acc = b_ref[0]
        for i in range(1, 8):
            acc = acc + b_ref[i]
        o_ref[...] = acc

    return pl.pallas_call(
        body, out_shape=jax.ShapeDtypeStruct((rows, cols), F32), name=name,
        compiler_params=pltpu.CompilerParams(vmem_limit_bytes=VMEM_LIMIT))(buf)


MESH = pl.DeviceIdType.MESH
ANY = pl.BlockSpec(memory_space=pl.ANY)


def _chip_peers():
    x, y = lax.axis_index("x"), lax.axis_index("y")
    return [(1 - x, y), (x, 1 - y), (1 - x, 1 - y)]


def _gather_chips(shards):
    plan = _gather_plan(shards)

    def body(*refs):
        n = len(shards)
        ins, outs, sems = refs[:n], refs[n:2 * n], refs[2 * n:]
        plan["start"](ins, outs, sems)
        plan["mid"](ins, outs, sems)
        plan["finish"](ins, outs, sems)

    n = len(shards)
    res = pl.pallas_call(
        body, in_specs=[ANY] * n, out_specs=[ANY] * n, out_shape=plan["out_shape"],
        scratch_shapes=plan["sems"], name="gather_chips")(*plan["args"])
    return plan["post"](res)


def _gather_plan(shards):
    n = len(shards)
    shards = [s.reshape((2, s.shape[0] // 2) + s.shape[1:]) for s in shards]

    def copies(kind, ins, outs, sems):
        s1, r1, s2, r2 = sems
        x, y, c = lax.axis_index("x"), lax.axis_index("y"), lax.axis_index("c")
        me = 2 * x + y
        out = []
        for j, (px, py) in enumerate(_chip_peers()):
            for a in range(n):
                k = j * n + a
                got = outs[a].at[2 * px + py].at[c]
                other = outs[a].at[2 * px + py].at[1 - c]
                src, dst, ss, rs, dev = {
                    "first": (ins[a].at[c], outs[a].at[me].at[c], s1, r1, (px, py, c)),
                    "landed": (got, got, s1, r1, (px, py, c)),
                    "passed": (got, got, s2, r2, (x, y, 1 - c)),
                    "theirs": (other, other, s2, r2, (x, y, 1 - c)),
                }[kind]
                out.append(pltpu.make_async_remote_copy(
                    src_ref=src, dst_ref=dst, send_sem=ss.at[k], recv_sem=rs.at[k], device_id=dev, device_id_type=MESH))
        return out

    def start(ins, outs, sems):
        for cp in copies("first", ins, outs, sems):
            cp.start()

    def mid(ins, outs, sems):
        for got, fwd in zip(copies("landed", ins, outs, sems), copies("passed", ins, outs, sems)):
            got.wait_recv()
            fwd.start()

    def finish(ins, outs, sems):
        for cp in copies("theirs", ins, outs, sems):
            cp.wait_recv()
        for cp in copies("first", ins, outs, sems) + copies("passed", ins, outs, sems):
            cp.wait_send()

    def post(res):
        me = 2 * lax.axis_index("x") + lax.axis_index("y")
        res = [lax.dynamic_update_index_in_dim(r, s, me, 0) for r, s in zip(res, shards)]
        return [r.reshape((N_CHIPS, 2 * r.shape[2]) + r.shape[3:]) for r in res]

    return dict(args=shards, out_shape=[jax.ShapeDtypeStruct((N_CHIPS,) + s.shape, s.dtype) for s in shards],
                sems=[pltpu.SemaphoreType.DMA((3 * n,))] * 4, start=start, mid=mid, finish=finish, post=post)


def _rows_call(name, parts, plan):
    n_in = [len(p["args"]) for p in parts]
    n_out = [len(p["out_shape"]) for p in parts]
    n_scr = [len(p["scratch"]) for p in parts]
    c_in, c_out = len(plan["args"]), len(plan["out_shape"])

    def split(refs, sizes):
        out, pos = [], 0
        for k in sizes:
            out.append(refs[pos:pos + k])
            pos += k
        return out

    def body(*refs):
        ins, outs, scr = split(refs, [sum(n_in) + c_in, sum(n_out) + c_out, sum(n_scr) + len(plan["sems"])])
        p_in, p_out, p_scr = split(ins, n_in + [c_in]), split(outs, n_out + [c_out]), split(scr, n_scr + [len(plan["sems"])])
        comm = (p_in[-1], p_out[-1], p_scr[-1])
        step = pl.program_id(0)

        @pl.when(step == 0)
        def _():
            plan["start"](*comm)

        for p, i, o, s in zip(parts, p_in, p_out, p_scr):
            p["body"](*i, *o, *s)

        @pl.when(step == NB - 2)
        def _():
            plan["mid"](*comm)

        @pl.when(step == NB - 1)
        def _():
            plan["finish"](*comm)

    flat = lambda key: [v for p in parts for v in p[key]]
    res = pl.pallas_call(
        body, grid=(NB,), in_specs=flat("in_specs") + [ANY] * c_in, out_specs=flat("out_specs") + [ANY] * c_out,
        out_shape=flat("out_shape") + plan["out_shape"], scratch_shapes=flat("scratch") + plan["sems"],
        name=name, compiler_params=_cp(("arbitrary",)))(*flat("args"), *plan["args"])
    outs = split(res, n_out + [c_out])
    return outs[:-1], outs[-1]


def _pair_exchange(grads, name):
    n = len(grads)
    hs = [g.shape[1] // 2 for g in grads]
    grads = [g.reshape((N_CHIPS, 2, h) + g.shape[2:]) for g, h in zip(grads, hs)]

    def body(*refs):
        ins, got = refs[:n], refs[n:2 * n]
        ssem, rsem = refs[2 * n:]
        x, y, c = lax.axis_index("x"), lax.axis_index("y"), lax.axis_index("c")
        sends = [pltpu.make_async_remote_copy(
            src_ref=ins[a].at[:, 1 - c], dst_ref=got[a], send_sem=ssem.at[a],
            recv_sem=rsem.at[a], device_id=(x, y, 1 - c), device_id_type=MESH) for a in range(n)]
        for cp in sends:
            cp.start()
        for cp in sends:
            cp.wait()

    half_shapes = [jax.ShapeDtypeStruct((N_CHIPS, h) + g.shape[3:], g.dtype) for g, h in zip(grads, hs)]
    got = pl.pallas_call(
        body, in_specs=[ANY] * n, out_specs=[ANY] * n, out_shape=half_shapes,
        scratch_shapes=[pltpu.SemaphoreType.DMA((n,))] * 2, name=name)(*grads)
    return grads, got


def _sum_pair(both, got, send_dtype, name):
    _, _, rows, cols = both.shape
    tr = 256 if rows % 256 == 0 else rows

    def body(c_ref, a_ref, b_ref, f_ref, s_ref):
        t = a_ref[...].astype(F32) + b_ref[...].astype(F32)
        f_ref[...] = t
        s_ref[...] = t.astype(send_dtype)

    blk = pl.BlockSpec((N_CHIPS, tr, cols), lambda i, c: (0, i, 0))
    mine = pl.BlockSpec((N_CHIPS, None, tr, cols), lambda i, c: (0, c[0], i, 0))
    return pl.pallas_call(
        body, grid_spec=pltpu.PrefetchScalarGridSpec(
            num_scalar_prefetch=1, grid=(rows // tr,), in_specs=[mine, blk], out_specs=[blk, blk]),
        out_shape=[jax.ShapeDtypeStruct(got.shape, F32), jax.ShapeDtypeStruct(got.shape, send_dtype)],
        name=name, compiler_params=_cp(("parallel",)))(lax.axis_index("c").reshape(1), both, got)


def _scatter_plan(send):
    n = len(send)

    def copies(sin, land, sems):
        ssem, rsem = sems
        c = lax.axis_index("c")
        return [pltpu.make_async_remote_copy(
            src_ref=sin[a].at[2 * px + py], dst_ref=land[a].at[j], send_sem=ssem.at[j * n + a],
            recv_sem=rsem.at[j * n + a], device_id=(px, py, c), device_id_type=MESH)
            for j, (px, py) in enumerate(_chip_peers()) for a in range(n)]

    def start(sin, land, sems):
        for cp in copies(sin, land, sems):
            cp.start()

    def finish(sin, land, sems):
        for cp in copies(sin, land, sems):
            cp.wait()

    return dict(args=list(send), out_shape=[jax.ShapeDtypeStruct((3,) + s.shape[1:], s.dtype) for s in send],
                sems=[pltpu.SemaphoreType.DMA((3 * n,))] * 2, start=start, mid=lambda *a: None, finish=finish)


def _sum_shard(keep, land, name):
    _, rows, cols = keep.shape
    tr = 256 if rows % 256 == 0 else rows

    def body(me_ref, m_ref, l_ref, o_ref):
        o_ref[...] = ((m_ref[...] + l_ref[0].astype(F32)) + l_ref[1].astype(F32)) + l_ref[2].astype(F32)

    own = pl.BlockSpec((None, tr, cols), lambda i, me: (me[0], i, 0))
    me = (2 * lax.axis_index("x") + lax.axis_index("y")).reshape(1)
    return pl.pallas_call(
        body, grid_spec=pltpu.PrefetchScalarGridSpec(
            num_scalar_prefetch=1, grid=(rows // tr,),
            in_specs=[own, pl.BlockSpec((3, tr, cols), lambda i, me: (0, i, 0))],
            out_specs=pl.BlockSpec((tr, cols), lambda i, me: (i, 0))),
        out_shape=jax.ShapeDtypeStruct((rows, cols), F32),
        name=name, compiler_params=_cp(("parallel",)))(me, keep, land)


def _join_cores(halves):
    n = len(halves)

    def body(*refs):
        ins, outs = refs[:n], refs[n:2 * n]
        ssem, rsem = refs[2 * n:]
        x, y, c = lax.axis_index("x"), lax.axis_index("y"), lax.axis_index("c")
        sends = [pltpu.make_async_remote_copy(
            src_ref=ins[a], dst_ref=outs[a].at[c], send_sem=ssem.at[a], recv_sem=rsem.at[a],
            device_id=(x, y, 1 - c), device_id_type=MESH) for a in range(n)]
        for cp in sends:
            cp.start()
        for a in range(n):
            sends[a].wait_send()
            pltpu.make_async_remote_copy(
                src_ref=ins[a], dst_ref=outs[a].at[1 - c], send_sem=ssem.at[a], recv_sem=rsem.at[a],
                device_id=(x, y, 1 - c), device_id_type=MESH).wait_recv()

    res = pl.pallas_call(
        body, in_specs=[ANY] * n, out_specs=[ANY] * n,
        out_shape=[jax.ShapeDtypeStruct((2,) + h.shape, h.dtype) for h in halves],
        scratch_shapes=[pltpu.SemaphoreType.DMA((n,))] * 2, name="join_cores")(*halves)
    c = lax.axis_index("c")
    res = [lax.dynamic_update_index_in_dim(r, h, c, 0) for r, h in zip(res, halves)]
    return [r.reshape((2 * r.shape[1],) + r.shape[2:]) for r in res]


def _gather_all(vec):
    def copies(kind, ins, outs, sems):
        ssem, rsem = sems
        x, y, c = lax.axis_index("x"), lax.axis_index("y"), lax.axis_index("c")
        me = 4 * x + 2 * y + c
        out = []
        for k in range(1, 8):
            px, py, pc = x ^ (k >> 2), y ^ ((k >> 1) & 1), c ^ (k & 1)
            dst = outs[0].at[me] if kind == "send" else outs[0].at[4 * px + 2 * py + pc]
            out.append(pltpu.make_async_remote_copy(
                src_ref=ins[0], dst_ref=dst, send_sem=ssem.at[k - 1], recv_sem=rsem.at[k - 1],
                device_id=(px, py, pc), device_id_type=MESH))
        return out

    def body(v_ref, o_ref, ssem, rsem):
        refs = ([v_ref], [o_ref], (ssem, rsem))
        for cp in copies("send", *refs):
            cp.start()
        for cp in copies("recv", *refs):
            cp.wait_recv()
        for cp in copies("send", *refs):
            cp.wait_send()

    res = pl.pallas_call(
        body, in_specs=[ANY], out_specs=ANY, out_shape=jax.ShapeDtypeStruct((8,) + vec.shape, vec.dtype),
        scratch_shapes=[pltpu.SemaphoreType.DMA((7,))] * 2, name="gather_all")(vec)
    me = 4 * lax.axis_index("x") + 2 * lax.axis_index("y") + lax.axis_index("c")
    return lax.dynamic_update_index_in_dim(res, vec, me, 0)


def _rope_tables():
    pos = (jnp.arange(R, dtype=jnp.int32) - PAD).astype(F32)
    half = HD // 2
    inv = ROPE_THETA ** (-jnp.arange(half, dtype=F32) / half)
    ang = pos[:, None] * inv[None, :]
    cos, sin = jnp.cos(ang), jnp.sin(ang)
    cs = jnp.tile(cos, (1, 4))
    sn = jnp.tile(jnp.concatenate([-sin, sin], axis=1), (1, 2))
    return cs, sn


def _local_step(x, target, p):
    w0t = p["ab_w_in"]
    conv_w = jnp.concatenate([p["ab_conv_w"], jnp.zeros((1, CC), F32)], axis=0)
    cs, sn = _rope_tables()

    h0 = jnp.concatenate([jnp.zeros((PAD, D), F32), p["meta_tokens"], x], axis=0)

    xn0 = _rms_fwd(h0, p["ab_pre_norm"], "rms_fwd0")
    plan = _gather_plan([p["sb_w_out"], p["ab_w_out"], p["ab_w_pw2"]])
    z0, gathered = _mm([(xn0, w0t)], F32, "in_proj0", 544, 1408, tb=True, plan=plan)
    wo1, wo0, wpw = plan["post"](gathered)
    wo1, wo0, wpw = wo1.reshape(D, D), wo0.reshape(D, D), wpw.reshape(CC, CC)
    plan = _gather_plan([p["sb_w_in"]])
    ((o0, a0, lse0), (cv0, s0)), gathered = _rows_call(
        "fwd0", [_swa_fwd(z0, cs, sn, p["ab_sinks"]),
                 _conv_fwd(z0, conv_w, p["ab_conv_b"], p["ab_conv_ln_g"], p["ab_conv_ln_b"])], plan)
    (w1,) = plan["post"](gathered)
    t0, c0 = _pw2_fwd(s0, wpw, z0)
    wo0h = wo0.reshape(2, CC, D)
    y0, h1, xn1 = _post_rms_fwd([a0, c0], wo0h, h0, p["ab_post_norm"], p["sb_pre_norm"], "out_proj0_norms")

    q1 =_mm([(xn1, (w1, 0))], BF16, "in_proj1_q", 544, 1024)
    k1 = _mm([(xn1, (w1, 1))], BF16, "in_proj1_k", 544, 1024)
    v1 = _mm([(xn1, (w1, 2))], BF16, "in_proj1_v", 544, 1024)
    g1 = _mm([(xn1, (w1, 3))], F32, "in_proj1_g", 544, 1024)
    o1, m1, car1, trips1 = _sb_fwd(q1, k1, v1, g1)

    dh2, dy1, d_sb_post, loss_row = _tail(m1, wo1, h1, p["sb_post_norm"], target)

    dm1 = _mm([(dy1, wo1)], F32, "out_proj1_dx", 544, 1024, tb=True)
    d_wo1 = _mm([(m1, dy1)], BF16, "out_proj1_dw", 512, 1024, ta=True)
    dq1, dk1, dv1, dg1 = _sb_bwd(trips1, q1, k1, v1, car1, dm1, g1, o1)
    dz1 = [dq1, dk1, dv1, dg1]
    d_w1 = jnp.stack([_mm([(xn1, dz1[j])], BF16, "in_proj1_dw%d" % j, 512, 1024, ta=True) for j in range(4)])

    dh1, d_sb_pre, dy0, d_ab_post = _rms_post_bwd(dz1, w1, h1, p["sb_pre_norm"], dh2, y0, p["ab_post_norm"],
                                                  "in_proj1_dx_norms")
    dmix0 = _mm([(dy0, wo0)], F32, "out_proj0_dx", 544, 1024, tb=True)
    d_wo0 = jnp.concatenate([_mm([(a0, dy0)], BF16, "out_proj0_dw_a", 512, 1024, ta=True),
                             _mm([(c0, dy0)], BF16, "out_proj0_dw_b", 512, 1024, ta=True)], axis=0)
    dgb0, ds0, d_wpw = _pw2_bwd(dmix0, t0, z0, wpw, s0)
    d_wpw = d_wpw.astype(BF16)
    early = ("sb_w_in", "sb_w_out", "ab_w_out", "ab_w_pw2")
    own1, got1 = _pair_exchange([d_w1, d_wo1.reshape(N_CHIPS, 256, D), d_wo0.reshape(N_CHIPS, 256, D),
                                 d_wpw.reshape(N_CHIPS, 128, CC)], "pair_exchange1")
    pair1 = [_sum_pair(o, t, BF16, "sum_pair_" + nm) for o, t, nm in zip(own1, got1, early)]
    plan = _scatter_plan([pr[1] for pr in pair1])
    ((dglu0, d_convw, d_small), (dq0, dga0, dkv0, d_sinks)), land1 = _rows_call(
        "bwd0", [_conv_bwd(ds0, cv0, z0, conv_w, p["ab_conv_ln_g"], p["ab_conv_ln_b"]),
                 _swa_bwd(z0, cs, sn, p["ab_sinks"], o0, dmix0, lse0)], plan)
    halves1 = [_sum_shard(pr[0], la, "sum_shard_" + nm) for pr, la, nm in zip(pair1, land1, early)]
    dz0 = jnp.concatenate([dq0, dkv0, dga0, dglu0, dgb0], axis=1)
    d_w0t = _mm([(dz0, xn0)], BF16, "in_proj0_dw", 1408, 512, ta=True)
    own0, got0 = _pair_exchange([d_w0t.reshape(N_CHIPS, 704, D)], "pair_exchange0")
    keep0, send0 = _sum_pair(own0[0], got0[0], BF16, "sum_pair_ab_w_in")
    plan = _scatter_plan([send0])
    dxn0, land0 = _mm([(dz0, w0t)], F32, "in_proj0_dx", 544, 1024, plan=plan)
    half0 = _sum_shard(keep0, land0[0], "sum_shard_ab_w_in")
    dh0_first, grad_x, d_ab_pre = _rms_bwd(dxn0, h0, p["ab_pre_norm"], dh1, F32, "rms_bwd0", split=True)

    grads = {
        "meta_tokens": dh0_first[PAD:TB], "ab_pre_norm": d_ab_pre, "ab_sinks": d_sinks[0:1, 0:8],
        "ab_conv_w": d_convw[0:CONV_W], "ab_conv_b": d_small[0:1], "ab_conv_ln_g": d_small[1:2],
        "ab_conv_ln_b": d_small[2:3], "ab_post_norm": d_ab_post, "sb_pre_norm": d_sb_pre, "sb_post_norm": d_sb_post,
    }
    h_sb_in, h_sb_out, h_ab_out, h_pw2 = halves1
    return loss_row, grad_x, grads, [half0, h_ab_out, h_pw2, h_sb_in, h_sb_out]


SMALL_ROWS = 80
REP_ROWS = 32

WEIGHTS = ["meta_tokens", "ab_pre_norm", "ab_w_in", "ab_sinks", "ab_conv_w", "ab_conv_b", "ab_conv_ln_g",
           "ab_conv_ln_b", "ab_w_pw2", "ab_w_out", "ab_post_norm", "sb_pre_norm", "sb_w_in", "sb_w_out",
           "sb_post_norm"]
BIG = ["ab_w_in", "ab_w_out", "ab_w_pw2", "sb_w_in", "sb_w_out"]


def _pack_small(conv_w, meta, sb_pre, sb_post):
    pad = lambda a, rows: jnp.pad(a, ((0, rows - a.shape[0]), (0, 0)))
    return jnp.concatenate([pad(conv_w, 32), meta.reshape(32, 128), pad(sb_pre.reshape(2, 128), 8),
                            pad(sb_post.reshape(2, 128), 8)], axis=0)


def _unpack_small(s):
    return s[0:31], s[32:64].reshape(16, 256), s[64:66].reshape(1, 256), s[72:74].reshape(1, 256)


REP_LOSS = 3592


def _pack_rep(pre, post, conv_b, ln_g, ln_b, sinks, extra=None):
    flat = jnp.concatenate([pre.reshape(-1), post.reshape(-1), conv_b.reshape(-1), ln_g.reshape(-1),
                            ln_b.reshape(-1), sinks.reshape(-1)] + ([] if extra is None else [extra.reshape(-1)]))
    flat = jnp.concatenate([flat, jnp.zeros((REP_ROWS * 128 - flat.shape[0],), F32)])
    return flat.reshape(REP_ROWS, 128)


def _unpack_rep(r):
    f = r.reshape(-1)
    return (f[0:1024].reshape(1, 1024), f[1024:2048].reshape(1, 1024), f[2048:2560].reshape(1, 512),
            f[2560:3072].reshape(1, 512), f[3072:3584].reshape(1, 512), f[3584:3592].reshape(1, 8))


def _chips_to_cols(w):
    return w.transpose(1, 0, 2).reshape(w.shape[1], -1)


def kernel(x, meta_tokens, ab_pre_norm, ab_w_in, ab_sinks, ab_conv_w, ab_conv_b, ab_conv_ln_g, ab_conv_ln_b, ab_w_pw2, ab_w_out, ab_post_norm, sb_pre_norm, sb_w_in, sb_w_out, sb_post_norm, loss_target, m_meta_tokens, m_ab_pre_norm, m_ab_w_in, m_ab_sinks, m_ab_conv_w, m_ab_conv_b, m_ab_conv_ln_g, m_ab_conv_ln_b, m_ab_w_pw2, m_ab_w_out, m_ab_post_norm, m_sb_pre_norm, m_sb_w_in, m_sb_w_out, m_sb_post_norm, v_meta_tokens, v_ab_pre_norm, v_ab_w_in, v_ab_sinks, v_ab_conv_w, v_ab_conv_b, v_ab_conv_ln_g, v_ab_conv_ln_b, v_ab_w_pw2, v_ab_w_out, v_ab_post_norm, v_sb_pre_norm, v_sb_w_in, v_sb_w_out, v_sb_post_norm):
    w = dict(meta_tokens=meta_tokens, ab_pre_norm=ab_pre_norm, ab_w_in=ab_w_in, ab_sinks=ab_sinks,
             ab_conv_w=ab_conv_w, ab_conv_b=ab_conv_b, ab_conv_ln_g=ab_conv_ln_g, ab_conv_ln_b=ab_conv_ln_b,
             ab_w_pw2=ab_w_pw2, ab_w_out=ab_w_out, ab_post_norm=ab_post_norm, sb_pre_norm=sb_pre_norm,
             sb_w_in=sb_w_in, sb_w_out=sb_w_out, sb_post_norm=sb_post_norm)
    m = dict(meta_tokens=m_meta_tokens, ab_pre_norm=m_ab_pre_norm, ab_w_in=m_ab_w_in, ab_sinks=m_ab_sinks,
             ab_conv_w=m_ab_conv_w, ab_conv_b=m_ab_conv_b, ab_conv_ln_g=m_ab_conv_ln_g,
             ab_conv_ln_b=m_ab_conv_ln_b, ab_w_pw2=m_ab_w_pw2, ab_w_out=m_ab_w_out, ab_post_norm=m_ab_post_norm,
             sb_pre_norm=m_sb_pre_norm, sb_w_in=m_sb_w_in, sb_w_out=m_sb_w_out, sb_post_norm=m_sb_post_norm)
    v = dict(meta_tokens=v_meta_tokens, ab_pre_norm=v_ab_pre_norm, ab_w_in=v_ab_w_in, ab_sinks=v_ab_sinks,
             ab_conv_w=v_ab_conv_w, ab_conv_b=v_ab_conv_b, ab_conv_ln_g=v_ab_conv_ln_g,
             ab_conv_ln_b=v_ab_conv_ln_b, ab_w_pw2=v_ab_w_pw2, ab_w_out=v_ab_w_out, ab_post_norm=v_ab_post_norm,
             sb_pre_norm=v_sb_pre_norm, sb_w_in=v_sb_w_in, sb_w_out=v_sb_w_out, sb_post_norm=v_sb_post_norm)

    def small_of(d):
        return _pack_small(d["ab_conv_w"][0], d["meta_tokens"], d["sb_pre_norm"], d["sb_post_norm"])

    def rep_of(d):
        return _pack_rep(d["ab_pre_norm"], d["ab_post_norm"], d["ab_conv_b"], d["ab_conv_ln_g"], d["ab_conv_ln_b"],
                         d["ab_sinks"])

    g_in0, g_small = _gather_chips([ab_w_in[0].T.astype(BF16), small_of(w)])
    conv_w_f = _chips_to_cols(g_small[:, 0:31])
    meta_f = _chips_to_cols(g_small[:, 32:64].reshape(N_CHIPS, 16, 256))
    sb_pre_f = g_small[:, 64:66].reshape(1, D)
    sb_post_f = g_small[:, 72:74].reshape(1, D)
    full = {
        "meta_tokens": meta_f, "ab_pre_norm": ab_pre_norm, "ab_w_in": g_in0.reshape(AB_IN, D),
        "ab_sinks": ab_sinks, "ab_conv_w": conv_w_f, "ab_conv_b": ab_conv_b, "ab_conv_ln_g": ab_conv_ln_g,
        "ab_conv_ln_b": ab_conv_ln_b, "ab_w_pw2": ab_w_pw2[0].astype(BF16), "ab_w_out": ab_w_out[0].astype(BF16),
        "ab_post_norm": ab_post_norm, "sb_pre_norm": sb_pre_f, "sb_w_in": sb_w_in[0].astype(BF16),
        "sb_w_out": sb_w_out[0].astype(BF16), "sb_post_norm": sb_post_f,
    }

    loss_row, grad_x, g, halves = _local_step(x[0], loss_target[0], full)

    total = _join_cores(halves)

    rep_g = _pack_rep(g["ab_pre_norm"], g["ab_post_norm"], g["ab_conv_b"], g["ab_conv_ln_g"], g["ab_conv_ln_b"],
                      g["ab_sinks"], loss_row[0:1, 0:1])
    vec = jnp.concatenate([rep_g, jnp.pad(g["ab_conv_w"].reshape(124, 128), ((0, 4), (0, 0))),
                           g["meta_tokens"].reshape(128, 128), g["sb_pre_norm"].reshape(8, 128),
                           g["sb_post_norm"].reshape(8, 128)], axis=0)
    vec_sum = _sum8(_gather_all(vec), "sum8_small")

    out_g, out_d, out_m, out_v = {}, {}, {}, {}
    for i, k in enumerate(BIG):
        shp = w[k].shape
        if k == "ab_w_in":
            res = _adamw(w[k][0].T, [total[i]], m[k][0].T, v[k][0].T, "adamw_" + k)
            out_g[k], out_d[k], out_m[k], out_v[k] = [r.T.reshape(shp) for r in res]
            continue
        res = _adamw(w[k][0], [total[i]], m[k][0], v[k][0], "adamw_" + k)
        out_g[k], out_d[k], out_m[k], out_v[k] = [r.reshape(shp) for r in res]

    rep_sum = vec_sum[0:REP_ROWS]
    loss = rep_sum.reshape(-1)[REP_LOSS]
    me = 2 * lax.axis_index("x") + lax.axis_index("y")
    small_sum = _pack_small(
        lax.dynamic_slice_in_dim(vec_sum[32:156].reshape(CONV_W, CC), me * 128, 128, axis=1),
        lax.dynamic_slice_in_dim(vec_sum[160:288].reshape(N_META, D), me * 256, 256, axis=1),
        lax.dynamic_slice_in_dim(vec_sum[288:296].reshape(1, D), me * 256, 256, axis=1),
        lax.dynamic_slice_in_dim(vec_sum[296:304].reshape(1, D), me * 256, 256, axis=1))
    res = _adamw(small_of(w), [small_sum], small_of(m), small_of(v), "adamw_small")
    for dst, r in zip((out_g, out_d, out_m, out_v), res):
        cw, mt, pre, post = _unpack_small(r)
        dst["ab_conv_w"], dst["meta_tokens"], dst["sb_pre_norm"], dst["sb_post_norm"] = cw[None], mt, pre, post
    res = _adamw(rep_of(w), [rep_sum], rep_of(m), rep_of(v), "adamw_rep")
    for dst, r in zip((out_g, out_d, out_m, out_v), res):
        (dst["ab_pre_norm"], dst["ab_post_norm"], dst["ab_conv_b"], dst["ab_conv_ln_g"], dst["ab_conv_ln_b"],
         dst["ab_sinks"]) = _unpack_rep(r)

    return (loss, grad_x[None], *[out_g[k] for k in WEIGHTS], *[out_d[k] for k in WEIGHTS],
            *[out_m[k] for k in WEIGHTS], *[out_v[k] for k in WEIGHTS])
```

```python
import functools

import jax
import jax.numpy as jnp
from jax import lax
from jax.experimental import pallas as pl
from jax.experimental.pallas import tpu as pltpu

F32 = jnp.float32
BF16 = jnp.bfloat16

D = 1024
SEQ = 2048
N_META = 16
TB = 128
TR = 272
PAD = TB - N_META
R = SEQ + TB
NB = R // TB
HD = 64
ROPE_THETA = 10000.0
NORM_EPS = 1e-6
LN_EPS = 1e-5
NEG = -1e30
CONV_W = 31
SCALE = HD ** -0.5
N_CHIPS = 4

C_Q, C_K, C_V, C_GA, C_GLU, C_GB = 0, 512, 640, 768, 1280, 2304
AB_IN = 2816

ADAM_LR, ADAM_B1, ADAM_B2, ADAM_EPS, ADAM_WD, ADAM_STEP = 0.001, 0.9, 0.999, 1e-08, 0.01, 10

VMEM_LIMIT = 56 * 1024 * 1024


def _cp(sem):
    return pltpu.CompilerParams(dimension_semantics=sem, vmem_limit_bytes=VMEM_LIMIT)


def _sig(x):
    return 1.0 / (1.0 + jnp.exp(-x))


def _dot(a, b):
    return lax.dot_general(a, b, (((1,), (0,)), ((), ())), preferred_element_type=F32)


def _dot_nt(a, b):
    return lax.dot_general(a, b, (((1,), (1,)), ((), ())), preferred_element_type=F32)


def _dot_tn(a, b):
    return lax.dot_general(a, b, (((0,), (0,)), ((), ())), preferred_element_type=F32)


def _mm(pairs, out_dtype, name, tm, tn, ta=False, tb=False, plan=None):
    pairs = [(a, b if isinstance(b, tuple) else (b, None)) for a, b in pairs]
    a0, (b0, _) = pairs[0]
    m = a0.shape[1] if ta else a0.shape[0]
    n = b0.shape[-2] if tb else b0.shape[-1]
    npairs = len(pairs)
    dims = (((0 if ta else 1,), (1 if tb else 0,)), ((), ()))
    c_in = len(plan["args"]) if plan else 0
    c_out = len(plan["out_shape"]) if plan else 0
    steps = (m // tm) * (n // tn)

    def body(*refs):
        o_ref = refs[2 * npairs + c_in]
        if plan:
            comm = (refs[2 * npairs:2 * npairs + c_in], refs[2 * npairs + c_in + 1:2 * npairs + c_in + 1 + c_out],
                    refs[2 * npairs + c_in + 1 + c_out:])
            step = pl.program_id(0) * (n // tn) + pl.program_id(1)

            @pl.when(step == 0)
            def _():
                plan["start"](*comm)

        acc = None
        for i in range(npairs):
            t = lax.dot_general(refs[2 * i][...].astype(BF16), refs[2 * i + 1][...].astype(BF16), dims,
                                preferred_element_type=F32)
            acc = t if acc is None else acc + t
        o_ref[...] = acc.astype(out_dtype)
        if plan:
            @pl.when(step == steps - 2)
            def _():
                plan["mid"](*comm)

            @pl.when(step == steps - 1)
            def _():
                plan["finish"](*comm)

    in_specs, args = [], []
    for a, (b, sel) in pairs:
        k = a.shape[0] if ta else a.shape[1]
        in_specs.append(pl.BlockSpec((k, tm), lambda i, j: (0, i)) if ta else pl.BlockSpec((tm, k), lambda i, j: (i, 0)))
        bshape, bidx = ((tn, k), lambda i, j: (j, 0)) if tb else ((k, tn), lambda i, j: (0, j))
        if sel is None:
            in_specs.append(pl.BlockSpec(bshape, bidx))
        else:
            in_specs.append(pl.BlockSpec((None,) + bshape, functools.partial(lambda i, j, f, s: (s,) + f(i, j), f=bidx, s=sel)))
        args += [a, b]
    out_spec = pl.BlockSpec((tm, tn), lambda i, j: (i, j))
    out_shape = jax.ShapeDtypeStruct((m, n), out_dtype)
    if not plan:
        return pl.pallas_call(
            body, grid=(m // tm, n // tn), in_specs=in_specs, out_specs=out_spec, out_shape=out_shape, name=name,
            compiler_params=_cp(("parallel", "parallel")))(*args)
    assert steps >= 2
    res = pl.pallas_call(
        body, grid=(m // tm, n // tn), in_specs=in_specs + [ANY] * c_in, out_specs=[out_spec] + [ANY] * c_out,
        out_shape=[out_shape] + plan["out_shape"], scratch_shapes=plan["sems"], name=name,
        compiler_params=_cp(("arbitrary", "arbitrary")))(*args, *plan["args"])
    return res[0], res[1:]


PW_TM = 544


def _pw2_fwd(s, w, z0):
    def body(s_ref, w_ref, g_ref, t_ref, c_ref):
        t = _dot(s_ref[...], w_ref[...])
        gv = g_ref[...]
        t_ref[...] = t
        c_ref[...] = (t * (gv * _sig(gv))).astype(BF16)

    blk = pl.BlockSpec((PW_TM, CC), lambda i: (i, 0))
    return pl.pallas_call(
        body, grid=(R // PW_TM,),
        in_specs=[blk, pl.BlockSpec((CC, CC), lambda i: (0, 0)), _cols_spec(PW_TM, CC, lambda i: (i, C_GB))],
        out_specs=[blk, blk],
        out_shape=[jax.ShapeDtypeStruct((R, CC), F32), jax.ShapeDtypeStruct((R, CC), BF16)],
        name="pw2_fwd", compiler_params=_cp(("parallel",)))(s, w, z0)


def _pw2_bwd(dmix, t, z0, w, s):
    def body(d_ref, t_ref, g_ref, w_ref, s_ref, dg_ref, ds_ref, dw_ref):
        @pl.when(pl.program_id(0) == 0)
        def _():
            dw_ref[...] = jnp.zeros_like(dw_ref)

        gv, dv = g_ref[...], d_ref[...]
        sg = _sig(gv)
        dg_ref[...] = (dv * t_ref[...] * (sg * (1.0 + gv * (1.0 - sg)))).astype(BF16)
        dt = (dv * (gv * sg)).astype(BF16)
        ds_ref[...] = _dot_nt(dt, w_ref[...])
        dw_ref[...] += _dot_tn(s_ref[...], dt)

    blk = pl.BlockSpec((PW_TM, CC), lambda i: (i, 0))
    full = pl.BlockSpec((CC, CC), lambda i: (0, 0))
    return pl.pallas_call(
        body, grid=(R // PW_TM,),
        in_specs=[_cols_spec(PW_TM, CC, lambda i: (i, CC)), blk, _cols_spec(PW_TM, CC, lambda i: (i, C_GB)), full, blk],
        out_specs=[blk, blk, full],
        out_shape=[jax.ShapeDtypeStruct((R, CC), BF16), jax.ShapeDtypeStruct((R, CC), F32),
                   jax.ShapeDtypeStruct((CC, CC), F32)],
        name="pw2_bwd", compiler_params=_cp(("arbitrary",)))(dmix, t, z0, w, s)


def _in_proj1(xn, w):
    def body(x_ref, w_ref, q_ref, k_ref, v_ref, g_ref):
        xv = x_ref[...]
        q_ref[...] = _dot(xv, w_ref[0]).astype(BF16)
        k_ref[...] = _dot(xv, w_ref[1]).astype(BF16)
        v_ref[...] = _dot(xv, w_ref[2]).astype(BF16)
        g_ref[...] = _dot(xv, w_ref[3])

    blk = pl.BlockSpec((TR, D), lambda n: (n, 0))
    sd = jax.ShapeDtypeStruct((R, D), BF16)
    return pl.pallas_call(
        body, grid=(R // TR,), in_specs=[blk, pl.BlockSpec(w.shape, lambda n: (0, 0, 0))],
        out_specs=[blk, blk, blk, blk], out_shape=[sd, sd, sd, jax.ShapeDtypeStruct((R, D), F32)],
        name="in_proj1", compiler_params=_cp(("parallel",)))(xn, w)


def _rms_fwd(h, g, name):
    def body(h_ref, g_ref, o_ref):
        x = h_ref[...]
        r = lax.rsqrt(jnp.mean(x * x, axis=1, keepdims=True) + NORM_EPS)
        o_ref[...] = (x * r * g_ref[...]).astype(BF16)

    return pl.pallas_call(
        body, grid=(R // TR,),
        in_specs=[pl.BlockSpec((TR, D), lambda n: (n, 0)), pl.BlockSpec((1, D), lambda n: (0, 0))],
        out_specs=pl.BlockSpec((TR, D), lambda n: (n, 0)),
        out_shape=jax.ShapeDtypeStruct((R, D), BF16), name=name, compiler_params=_cp(("parallel",)))(h, g)


def _rms_bwd(dout, x, g, res, out_dtype, name, split=False):
    has_res = res is not None

    def body(*refs):
        if split:
            refs = list(refs)
            dx_rest_ref = refs.pop(-2)
        if has_res:
            d_ref, x_ref, g_ref, r_ref, dx_ref, dg_ref = refs
        else:
            d_ref, x_ref, g_ref, dx_ref, dg_ref = refs
        n = pl.program_id(0)
        xv = x_ref[...]
        dv = d_ref[...]
        r = lax.rsqrt(jnp.mean(xv * xv, axis=1, keepdims=True) + NORM_EPS)
        xh = xv * r
        dxh = dv * g_ref[...]
        dx = r * (dxh - xh * jnp.mean(dxh * xh, axis=1, keepdims=True))
        if has_res:
            dx = dx + r_ref[...]
        row = lax.broadcasted_iota(jnp.int32, (TB, D), 0) + n * TB
        dx = jnp.where(row >= PAD, dx, 0.0).astype(out_dtype)
        if split:
            @pl.when(n == 0)
            def _():
                dx_ref[...] = dx

            @pl.when(n > 0)
            def _():
                dx_rest_ref[...] = dx
        else:
            dx_ref[...] = dx

        @pl.when(n == 0)
        def _():
            dg_ref[...] = jnp.zeros_like(dg_ref)

        dg_ref[...] += jnp.sum(dv * xh, axis=0, keepdims=True)

    blk = pl.BlockSpec((TB, D), lambda n: (n, 0))
    vec = pl.BlockSpec((1, D), lambda n: (0, 0))
    ins = [dout, x, g] + ([res] if has_res else [])
    in_specs = [blk, blk, vec] + ([blk] if has_res else [])
    if split:
        out_specs = [pl.BlockSpec((TB, D), lambda n: (0, 0)), pl.BlockSpec((TB, D), lambda n: (jnp.maximum(n - 1, 0), 0)), vec]
        out_shape = [jax.ShapeDtypeStruct((TB, D), out_dtype), jax.ShapeDtypeStruct((SEQ, D), out_dtype),
                     jax.ShapeDtypeStruct((1, D), F32)]
    else:
        out_specs = [blk, vec]
        out_shape = [jax.ShapeDtypeStruct((R, D), out_dtype), jax.ShapeDtypeStruct((1, D), F32)]
    return pl.pallas_call(
        body, grid=(NB,), in_specs=in_specs, out_specs=out_specs, out_shape=out_shape,
        name=name, compiler_params=_cp(("arbitrary",)))(*ins)


def _post_rms_fwd(xs, w, h, g_post, g_next, name):
    nx = len(xs)

    def body(*refs):
        x_refs, (w_ref, h_ref, gp_ref, gn_ref, y_ref, o_ref, xn_ref) = refs[:nx], refs[nx:]
        yv = _dot(x_refs[0][...], w_ref[0])
        for j in range(1, nx):
            yv = yv + _dot(x_refs[j][...], w_ref[j])
        y_ref[...] = yv
        r = lax.rsqrt(jnp.mean(yv * yv, axis=1, keepdims=True) + NORM_EPS)
        hn = h_ref[...] + yv * r * gp_ref[...]
        o_ref[...] = hn
        r2 = lax.rsqrt(jnp.mean(hn * hn, axis=1, keepdims=True) + NORM_EPS)
        xn_ref[...] = (hn * r2 * gn_ref[...]).astype(BF16)

    blk = pl.BlockSpec((TR, D), lambda n: (n, 0))
    vec = pl.BlockSpec((1, D), lambda n: (0, 0))
    xblk = [pl.BlockSpec((TR, x.shape[1]), lambda n: (n, 0)) for x in xs]
    return pl.pallas_call(
        body, grid=(R // TR,), in_specs=xblk + [pl.BlockSpec(w.shape, lambda n: (0, 0, 0)), blk, vec, vec],
        out_specs=[blk, blk, blk],
        out_shape=[jax.ShapeDtypeStruct((R, D), F32), jax.ShapeDtypeStruct((R, D), F32),
                   jax.ShapeDtypeStruct((R, D), BF16)],
        name=name, compiler_params=_cp(("parallel",)))(*xs, w, h, g_post, g_next)


def _rms_post_bwd(dzs, w, h, g, res, y, g_post, name):
    nz = len(dzs)

    def body(*refs):
        dz_refs, (w_ref, h_ref, g_ref, r_ref, y_ref, gp_ref, dh_ref, dg_ref, dy_ref, dgp_ref) = refs[:nz], refs[nz:]
        n = pl.program_id(0)

        @pl.when(n == 0)
        def _():
            dg_ref[...] = jnp.zeros_like(dg_ref)
            dgp_ref[...] = jnp.zeros_like(dgp_ref)

        dv = _dot_nt(dz_refs[0][...], w_ref[0])
        for j in range(1, nz):
            dv = dv + _dot_nt(dz_refs[j][...], w_ref[j])
        hv = h_ref[...]
        r = lax.rsqrt(jnp.mean(hv * hv, axis=1, keepdims=True) + NORM_EPS)
        xh = hv * r
        dxh = dv * g_ref[...]
        dh = r * (dxh - xh * jnp.mean(dxh * xh, axis=1, keepdims=True)) + r_ref[...]
        row = lax.broadcasted_iota(jnp.int32, (TR, D), 0) + n * TR
        dh = jnp.where(row >= PAD, dh, 0.0)
        dh_ref[...] = dh
        dg_ref[...] += jnp.sum(dv * xh, axis=0, keepdims=True)
        yv = y_ref[...]
        ry = lax.rsqrt(jnp.mean(yv * yv, axis=1, keepdims=True) + NORM_EPS)
        yh = yv * ry
        dyh = dh * gp_ref[...]
        dy_ref[...] = (ry * (dyh - yh * jnp.mean(dyh * yh, axis=1, keepdims=True))).astype(BF16)
        dgp_ref[...] += jnp.sum(dh * yh, axis=0, keepdims=True)

    blk = pl.BlockSpec((TR, D), lambda n: (n, 0))
    vec = pl.BlockSpec((1, D), lambda n: (0, 0))
    return pl.pallas_call(
        body, grid=(R // TR,),
        in_specs=[blk] * nz + [pl.BlockSpec(w.shape, lambda n: (0, 0, 0)), blk, vec, blk, blk, vec],
        out_specs=[blk, vec, blk, vec],
        out_shape=[jax.ShapeDtypeStruct((R, D), F32), jax.ShapeDtypeStruct((1, D), F32),
                   jax.ShapeDtypeStruct((R, D), BF16), jax.ShapeDtypeStruct((1, D), F32)],
        name=name, compiler_params=_cp(("arbitrary",)))(*dzs, w, h, g, res, y, g_post)


def _cols_spec(rows, width, where):
    def index(*g):
        r, c = where(*g)
        return r * rows, (c if isinstance(c, int) else pl.multiple_of(c, 128))
    return pl.BlockSpec((pl.Element(rows), pl.Element(width)), index)


def _tail(x, w, h, g, target):
    steps = R // TR

    def body(x_ref, w_ref, h_ref, g_ref, t_ref, d_ref, dy_ref, dg_ref, l_ref):
        n = pl.program_id(0)

        @pl.when(n == 0)
        def _():
            dg_ref[...] = jnp.zeros_like(dg_ref)
            l_ref[...] = jnp.zeros_like(l_ref)

        yv = _dot(x_ref[...], w_ref[...])
        r = lax.rsqrt(jnp.mean(yv * yv, axis=1, keepdims=True) + NORM_EPS)
        yh = yv * r
        tv = t_ref[...]
        tv = jnp.where(n == 0, jnp.concatenate([jnp.zeros((TB, D), F32), tv[0:TR - TB]], axis=0), tv)
        row = lax.broadcasted_iota(jnp.int32, (TR, D), 0) + n * TR
        err = jnp.where(row >= TB, (h_ref[...] + yh * g_ref[...]) - tv, 0.0)
        dv = err * (1.0 / D)
        d_ref[...] = dv
        l_ref[...] += jnp.sum(err * err, axis=0, keepdims=True)
        dyh = dv * g_ref[...]
        dy_ref[...] = (r * (dyh - yh * jnp.mean(dyh * yh, axis=1, keepdims=True))).astype(BF16)
        dg_ref[...] += jnp.sum(dv * yh, axis=0, keepdims=True)

        @pl.when(n == steps - 1)
        def _():
            tot = jnp.sum(l_ref[...], axis=1, keepdims=True) * (0.5 / D)
            l_ref[...] = jnp.broadcast_to(tot, (1, D))

    blk = pl.BlockSpec((TR, D), lambda n: (n, 0))
    vec = pl.BlockSpec((1, D), lambda n: (0, 0))
    tgt = pl.BlockSpec((pl.Element(TR), pl.Element(D)),
                       lambda n: (pl.multiple_of(jnp.maximum(TR * n - TB, 0), 8), 0))
    return pl.pallas_call(
        body, grid=(steps,),
        in_specs=[pl.BlockSpec((TR, x.shape[1]), lambda n: (n, 0)), pl.BlockSpec(w.shape, lambda n: (0, 0)), blk, vec, tgt],
        out_specs=[blk, blk, vec, vec],
        out_shape=[jax.ShapeDtypeStruct((R, D), F32), jax.ShapeDtypeStruct((R, D), BF16),
                   jax.ShapeDtypeStruct((1, D), F32), jax.ShapeDtypeStruct((1, D), F32)],
        name="out_proj1_tail", compiler_params=_cp(("arbitrary",)))(x, w, h, g, target)


def _lane_row(shape):
    return lax.broadcasted_iota(jnp.int32, shape, 1), lax.broadcasted_iota(jnp.int32, shape, 0)


def _rot_half(x, lane):
    return jnp.where(lane % HD < HD // 2, pltpu.roll(x, 128 - HD // 2, 1), pltpu.roll(x, HD // 2, 1))


def _swa_blocks(n):
    return (0, jnp.maximum(n - 1, 0), n)


SWA_STACKS = ((0, 0), (0, 1), (1, 0), (1, 1))


def _swa_masks(n, lane, row):
    qpos = n * TB + (row & (TB - 1))
    kp = (n - 1) * TB + lane
    kc = n * TB + lane
    m0 = (lane >= PAD) & (qpos - lane >= TB)
    mp = (kp >= PAD) & (qpos >= kp) & (qpos - kp < TB)
    mc = (kc >= PAD) & (qpos >= kc)
    return (m0, mp, mc)


def _stack_pair(xa, xb, par):
    lane = lax.broadcasted_iota(jnp.int32, (TB, 128), 1)
    keep = (lane < HD) if par == 0 else (lane >= HD)
    return jnp.concatenate([jnp.where(keep, xa, 0.0), jnp.where(keep, xb, 0.0)], axis=0)


def _per_head(a, b):
    row = lax.broadcasted_iota(jnp.int32, (2 * TB, 1), 0)
    return jnp.where(row < TB, a, b)


def _swa_load(n, zq_ref, zkv_ref, cs_ref, sn_ref, lane):
    r0 = pl.multiple_of(n * TB, TB)
    csq, snq = cs_ref[pl.ds(r0, TB), :], sn_ref[pl.ds(r0, TB), :]
    qc = []
    for c in range(4):
        x = zq_ref[:, c * 128:(c + 1) * 128]
        qc.append((x * csq + _rot_half(x, lane) * snq) * SCALE)
    qst = [_stack_pair(qc[2 * g], qc[2 * g + 1], par).astype(BF16) for g, par in SWA_STACKS]
    kvs = []
    for b in _swa_blocks(n):
        b0 = pl.multiple_of(b * TB, TB)
        csb, snb = cs_ref[pl.ds(b0, TB), :], sn_ref[pl.ds(b0, TB), :]
        kx = zkv_ref[pl.ds(b0, TB), 0:128]
        kr = kx * csb + _rot_half(kx, lane) * snb
        vx = zkv_ref[pl.ds(b0, TB), 128:256]
        kvs.append((kr.astype(BF16), pltpu.roll(kr, HD, 1).astype(BF16),
                    vx.astype(BF16), pltpu.roll(vx, HD, 1).astype(BF16), csb, snb, b0))
    return qst, (csq, snq), kvs


def _swa_fwd(z0, cs, sn, sinks):
    def body(zq_ref, zkv_ref, cs_ref, sn_ref, sk_ref, ga_ref, o_ref, a_ref, lse_ref):
        n = pl.program_id(0)
        lane, row = _lane_row((TB, 128))
        lo = lane < HD
        masks = _swa_masks(n, *_lane_row((2 * TB, 128)))
        qst, _, kvs = _swa_load(n, zq_ref, zkv_ref, cs_ref, sn_ref, lane)
        ss = [[jnp.where(m, _dot_nt(qst[si], k if par == g else ka), NEG)
               for (k, ka, _, _, _, _, _), m in zip(kvs, masks)] for si, (g, par) in enumerate(SWA_STACKS)]
        o2, lse2 = [], []
        for si, (g, par) in enumerate(SWA_STACKS):
            sink = _per_head(sk_ref[0, 4 * g + par], sk_ref[0, 4 * g + 2 + par])
            s = ss[si]
            mx = jnp.maximum(jnp.maximum(jnp.max(s[0], axis=1, keepdims=True), jnp.max(s[1], axis=1, keepdims=True)),
                             jnp.max(s[2], axis=1, keepdims=True))
            mx = jnp.maximum(mx, sink)
            es = [jnp.exp(sb - mx) for sb in s]
            den = (jnp.sum(es[0], axis=1, keepdims=True) + jnp.sum(es[1], axis=1, keepdims=True)
                   + jnp.sum(es[2], axis=1, keepdims=True) + jnp.exp(sink - mx))
            inv = 1.0 / den
            t = jnp.zeros((2 * TB, 128), F32)
            for (_, _, v, va, _, _, _), e in zip(kvs, es):
                t = t + _dot((e * inv).astype(BF16), v if par == g else va)
            o2.append(t)
            lse2.append(mx + jnp.log(den))
        lse_t = jnp.zeros((TB, 128), F32)
        for g in range(2):
            for t in range(2):
                rows = slice(t * TB, (t + 1) * TB)
                c = 2 * g + t
                oc = jnp.where(lo, o2[2 * g][rows], o2[2 * g + 1][rows])
                o_ref[:, c * 128:(c + 1) * 128] = oc
                gv = ga_ref[:, c * 128:(c + 1) * 128]
                a_ref[:, c * 128:(c + 1) * 128] = (oc * (gv * _sig(gv))).astype(BF16)
                for par in range(2):
                    lse_t = jnp.where(lane == 4 * g + 2 * t + par, lse2[2 * g + par][rows], lse_t)
        lse_ref[...] = lse_t

    full = pl.BlockSpec((R, 128), lambda n: (0, 0))
    return dict(
        body=body,
        in_specs=[pl.BlockSpec((TB, 512), lambda n: (n, C_Q // 512)),
                  pl.BlockSpec((R, 256), lambda n: (0, C_K // 256)), full, full,
                  pl.BlockSpec(memory_space=pltpu.SMEM), _cols_spec(TB, 512, lambda n: (n, C_GA))],
        args=[z0, z0, cs, sn, sinks, z0],
        out_specs=[pl.BlockSpec((TB, 512), lambda n: (n, 0)), pl.BlockSpec((TB, 512), lambda n: (n, 0)),
                   pl.BlockSpec((TB, 128), lambda n: (n, 0))],
        out_shape=[jax.ShapeDtypeStruct((R, 512), F32), jax.ShapeDtypeStruct((R, 512), BF16),
                   jax.ShapeDtypeStruct((R, 128), F32)],
        scratch=[])


def _swa_bwd(z0, cs, sn, sinks, o, dmix, lse):
    def body(zq_ref, zkv_ref, cs_ref, sn_ref, sk_ref, ga_ref, o_ref, dm_ref, lse_ref,
             dq_ref, dga_ref, dkv_ref, dsk_ref, do_ref, acc_ref):
        n = pl.program_id(0)

        @pl.when(n == 0)
        def _():
            acc_ref[...] = jnp.zeros_like(acc_ref)
            dsk_ref[...] = jnp.zeros_like(dsk_ref)

        gv, dmv = ga_ref[...], dm_ref[...]
        sg = _sig(gv)
        dga_ref[...] = (dmv * o_ref[...] * (sg * (1.0 + gv * (1.0 - sg)))).astype(BF16)
        do_ref[...] = dmv * (gv * sg)
        lane, row = _lane_row((TB, 128))
        lo = lane < HD
        masks = _swa_masks(n, *_lane_row((2 * TB, 128)))
        qst, (csq, snq), kvs = _swa_load(n, zq_ref, zkv_ref, cs_ref, sn_ref, lane)
        lse_t = lse_ref[...]
        ss = [[jnp.where(m, _dot_nt(qst[si], k if par == g else ka), NEG)
               for (k, ka, _, _, _, _, _), m in zip(kvs, masks)] for si, (g, par) in enumerate(SWA_STACKS)]
        dobs, deltas, lses, dps = [], [], [], []
        for g, par in SWA_STACKS:
            ca, cb = slice(2 * g * 128, (2 * g + 1) * 128), slice((2 * g + 1) * 128, (2 * g + 2) * 128)
            dom = _stack_pair(do_ref[:, ca], do_ref[:, cb], par)
            deltas.append(jnp.sum(dom * jnp.concatenate([o_ref[:, ca], o_ref[:, cb]], axis=0), axis=1, keepdims=True))
            dob = dom.astype(BF16)
            dobs.append(dob)
            lses.append(jnp.concatenate(
                [jnp.sum(jnp.where(lane == 4 * g + 2 * t + par, lse_t, 0.0), axis=1, keepdims=True) for t in range(2)],
                axis=0))
            dps.append([_dot_nt(dob, v if par == g else va) for (_, _, v, va, _, _, _) in kvs])
        dk_al = [jnp.zeros((TB, 128), F32) for _ in range(3)]
        dk_mis = [jnp.zeros((TB, 128), F32) for _ in range(3)]
        dv_al = [jnp.zeros((TB, 128), F32) for _ in range(3)]
        dv_mis = [jnp.zeros((TB, 128), F32) for _ in range(3)]
        dsk_t = jnp.zeros((TB, 128), F32)
        dq2 = []
        for si, (g, par) in enumerate(SWA_STACKS):
            dqt = jnp.zeros((2 * TB, 128), F32)
            for bi, (k, ka, _, _, _, _, _) in enumerate(kvs):
                p = jnp.exp(ss[si][bi] - lses[si])
                ds = (p * (dps[si][bi] - deltas[si])).astype(BF16)
                dqt = dqt + _dot(ds, k if par == g else ka)
                dkh = _dot_tn(ds, qst[si])
                dvh = _dot_tn(p.astype(BF16), dobs[si])
                if par == g:
                    dk_al[bi] = dk_al[bi] + dkh
                    dv_al[bi] = dv_al[bi] + dvh
                else:
                    dk_mis[bi] = dk_mis[bi] + dkh
                    dv_mis[bi] = dv_mis[bi] + dvh
            dq2.append(dqt)
            sink = _per_head(sk_ref[0, 4 * g + par], sk_ref[0, 4 * g + 2 + par])
            dsk = -jnp.exp(sink - lses[si]) * deltas[si]
            for t in range(2):
                dsk_t = jnp.where(lane == 4 * g + 2 * t + par, dsk[t * TB:(t + 1) * TB], dsk_t)
        for g in range(2):
            for t in range(2):
                rows = slice(t * TB, (t + 1) * TB)
                c = 2 * g + t
                dqc = jnp.where(lo, dq2[2 * g][rows], dq2[2 * g + 1][rows]) * SCALE
                dq_ref[:, c * 128:(c + 1) * 128] = (dqc * csq + _rot_half(dqc * snq, lane)).astype(BF16)
        for bi, (_, _, _, _, csb, snb, b0) in enumerate(kvs):
            dk = dk_al[bi] + pltpu.roll(dk_mis[bi], HD, 1)
            dv = dv_al[bi] + pltpu.roll(dv_mis[bi], HD, 1)
            acc_ref[pl.ds(b0, TB), 0:128] += dk * csb + _rot_half(dk * snb, lane)
            acc_ref[pl.ds(b0, TB), 128:256] += dv
        dsk_ref[0:1, :] += jnp.sum(dsk_t, axis=0, keepdims=True)

        @pl.when(n == NB - 1)
        def _():
            dkv_ref[...] = acc_ref[...].astype(BF16)

    full = pl.BlockSpec((R, 128), lambda n: (0, 0))
    b512 = pl.BlockSpec((TB, 512), lambda n: (n, 0))
    return dict(
        body=body,
        in_specs=[pl.BlockSpec((TB, 512), lambda n: (n, C_Q // 512)),
                  pl.BlockSpec((R, 256), lambda n: (0, C_K // 256)), full, full,
                  pl.BlockSpec(memory_space=pltpu.SMEM), _cols_spec(TB, 512, lambda n: (n, C_GA)),
                  b512, b512, pl.BlockSpec((TB, 128), lambda n: (n, 0))],
        args=[z0, z0, cs, sn, sinks, z0, o, dmix, lse],
        out_specs=[b512, b512, pl.BlockSpec((R, 256), lambda n: (0, 0)), pl.BlockSpec((8, 128), lambda n: (0, 0))],
        out_shape=[jax.ShapeDtypeStruct((R, 512), BF16), jax.ShapeDtypeStruct((R, 512), BF16),
                   jax.ShapeDtypeStruct((R, 256), BF16), jax.ShapeDtypeStruct((8, 128), F32)],
        scratch=[pltpu.VMEM((TB, 512), F32), pltpu.VMEM((R, 256), F32)])


CC = 512
HALO = CONV_W - 1


def _conv_fwd(z0, conv_w, conv_b, ln_g, ln_b):
    def body(g_ref, w_ref, cb_ref, lg_ref, lb_ref, cv_ref, s_ref, ubuf):
        n = pl.program_id(0)

        @pl.when(n == 0)
        def _():
            ubuf[...] = jnp.zeros_like(ubuf)

        u = g_ref[:, 0:CC] * _sig(g_ref[:, CC:2 * CC])
        for k in range(8):
            ubuf[k, 0:TB + 8, :] = ubuf[k, TB:2 * TB + 8, :]
            ubuf[k, pl.ds(TB + 8 - k, TB), :] = u
        acc = jnp.zeros((TB, CC), F32)
        for w in range(CONV_W):
            off = TB - HALO + w
            acc = acc + ubuf[off % 8, pl.ds(off + 8 - off % 8, TB), :] * w_ref[w:w + 1, :]
        cv = acc + cb_ref[...]
        cv_ref[...] = cv
        xc = cv - jnp.mean(cv, axis=1, keepdims=True)
        rs = lax.rsqrt(jnp.mean(xc * xc, axis=1, keepdims=True) + LN_EPS)
        ln = xc * rs * lg_ref[...] + lb_ref[...]
        s_ref[...] = (ln * _sig(ln)).astype(BF16)

    vec = pl.BlockSpec((1, CC), lambda n: (0, 0))
    blk = pl.BlockSpec((TB, CC), lambda n: (n, 0))
    return dict(
        body=body,
        in_specs=[_cols_spec(TB, 2 * CC, lambda n: (n, C_GLU)),
                  pl.BlockSpec((32, CC), lambda n: (0, 0)), vec, vec, vec],
        args=[z0, conv_w, conv_b, ln_g, ln_b],
        out_specs=[blk, blk],
        out_shape=[jax.ShapeDtypeStruct((R, CC), F32), jax.ShapeDtypeStruct((R, CC), BF16)],
        scratch=[pltpu.VMEM((8, 2 * TB + 8, CC), F32)])


def _conv_bwd(ds, cv, z0, conv_w, ln_g, ln_b):
    def body(ds_ref, cv_ref, g_ref, w_ref, lg_ref, lb_ref, dglu_ref, dw_ref, dsm_ref, dbuf):
        n = pl.program_id(0)

        @pl.when(n == 0)
        def _():
            dbuf[...] = jnp.zeros_like(dbuf)
            dw_ref[...] = jnp.zeros_like(dw_ref)
            dsm_ref[...] = jnp.zeros_like(dsm_ref)

        cv = cv_ref[...]
        xc = cv - jnp.mean(cv, axis=1, keepdims=True)
        rs = lax.rsqrt(jnp.mean(xc * xc, axis=1, keepdims=True) + LN_EPS)
        xh = xc * rs
        ln = xh * lg_ref[...] + lb_ref[...]
        sg = _sig(ln)
        dln = ds_ref[...] * (sg * (1.0 + ln * (1.0 - sg)))
        dxh = dln * lg_ref[...]
        dcv = rs * (dxh - jnp.mean(dxh, axis=1, keepdims=True) - xh * jnp.mean(dxh * xh, axis=1, keepdims=True))
        dsm_ref[0:1, :] += jnp.sum(dcv, axis=0, keepdims=True)
        dsm_ref[1:2, :] += jnp.sum(dln * xh, axis=0, keepdims=True)
        dsm_ref[2:3, :] += jnp.sum(dln, axis=0, keepdims=True)
        for k in range(8):
            dbuf[k, TB:2 * TB + 8, :] = dbuf[k, 0:TB + 8, :]
            dbuf[k, pl.ds(8 - k, TB), :] = dcv
        a = g_ref[:, 0:CC]
        sb = _sig(g_ref[:, CC:2 * CC])
        u = a * sb
        du = jnp.zeros((TB, CC), F32)
        for w in range(CONV_W):
            off = HALO - w
            sh = dbuf[off % 8, pl.ds(off + 8 - off % 8, TB), :]
            du = du + sh * w_ref[w:w + 1, :]
            dw_ref[w:w + 1, :] += jnp.sum(u * sh, axis=0, keepdims=True)
        dglu_ref[:, 0:CC] = (du * sb).astype(BF16)
        dglu_ref[:, CC:2 * CC] = (du * a * sb * (1.0 - sb)).astype(BF16)

    rev = lambda n: (NB - 1 - n, 0)
    vec = pl.BlockSpec((1, CC), lambda n: (0, 0))
    blk = pl.BlockSpec((TB, CC), rev)
    return dict(
        body=body,
        in_specs=[blk, blk, _cols_spec(TB, 2 * CC, lambda n: (NB - 1 - n, C_GLU)),
                  pl.BlockSpec((32, CC), lambda n: (0, 0)), vec, vec],
        args=[ds, cv, z0, conv_w, ln_g, ln_b],
        out_specs=[pl.BlockSpec((TB, 2 * CC), rev), pl.BlockSpec((32, CC), lambda n: (0, 0)),
                   pl.BlockSpec((8, CC), lambda n: (0, 0))],
        out_shape=[jax.ShapeDtypeStruct((R, 2 * CC), BF16), jax.ShapeDtypeStruct((32, CC), F32),
                   jax.ShapeDtypeStruct((8, CC), F32)],
        scratch=[pltpu.VMEM((8, 2 * TB + 8, CC), F32)])


def _split_dot(x, t):
    hi = x.astype(BF16)
    lo = (x - hi.astype(F32)).astype(BF16)
    return _dot(hi, t) + _dot(lo, t)


def _stack_heads(x):
    lane = lax.broadcasted_iota(jnp.int32, (TB, 128), 1)
    return jnp.concatenate([jnp.where(lane < HD, x, 0.0), jnp.where(lane < HD, 0.0, x)], axis=0).astype(BF16)


def _sb_stack(qv, i):
    lane2, row2 = _lane_row((2 * TB, 128))
    qpos2 = i * TB + (row2 & (TB - 1))
    lane, row = _lane_row((TB, 128))
    return _stack_heads(qv), lane2, qpos2, (row > lane).astype(BF16)


SB_U = 3
SB_DEAD = -104.0
SB_P = 4


def _sb_fwd(q, k, v, g):
    def body(q_ref, k_ref, v_ref, g_ref, o_ref, m_ref, c_ref, n_ref):
        p, i = pl.program_id(0), pl.program_id(1)
        lane, row = _lane_row((TB, 128))
        lo = lane < HD
        slabs = [slice(s * 128, (s + 1) * 128) for s in range(SB_P)]
        q2s = []
        for sl in slabs:
            q2, lane2, qpos2, tri_gt = _sb_stack(q_ref[:, sl].astype(F32) * SCALE, i)
            q2s.append(q2)

        def cond(st):
            t, _, c2s = st
            alive = jnp.max(c2s[0])
            for c2 in c2s[1:]:
                alive = jnp.maximum(alive, jnp.max(c2))
            return jnp.logical_and(i - SB_U * t >= 0, alive > SB_DEAD)

        def step(st):
            t, accs, c2s = st
            accs, c2s = list(accs), list(c2s)
            jrs = [i - SB_U * t - u for u in range(SB_U)]
            j0s = [pl.multiple_of(jnp.maximum(jr, 0) * TB, TB) for jr in jrs]
            valids = []
            for jr in jrs:
                kpos = jr * TB + lane2
                valids.append((kpos >= PAD) & (kpos < qpos2))
            zs = [[jnp.where(valid, _dot_nt(q2s[s], k_ref[pl.ds(j0, TB), slabs[s]]), NEG)
                   for j0, valid in zip(j0s, valids)] for s in range(SB_P)]
            lbs, l1s = [], []
            for s in range(SB_P):
                lbs.append([jnp.minimum(z, 0.0) - jnp.log(1.0 + jnp.exp(-jnp.abs(z))) for z in zs[s]])
                l1s.append([lb - z for lb, z in zip(lbs[s], zs[s])])
            sfxs = [[_split_dot(l1, tri_gt) for l1 in l1s[s]] for s in range(SB_P)]
            carries = []
            for s in range(SB_P):
                cs, c2 = [], c2s[s]
                for jr, l1 in zip(jrs, l1s[s]):
                    cs.append(c2)
                    c_ref[:, slabs[s]] = jnp.where(lane == 2 * jr, c2[0:TB],
                                                   jnp.where(lane == 2 * jr + 1, c2[TB:2 * TB], c_ref[:, slabs[s]]))
                    c2 = c2 + jnp.sum(l1, axis=1, keepdims=True)
                carries.append(cs)
                c2s[s] = c2
            for s in range(SB_P):
                for j0, valid, lb, sfx, cu in zip(j0s, valids, lbs[s], sfxs[s], carries[s]):
                    a = jnp.exp(lb + sfx + cu).astype(BF16)
                    av = _dot(a, v_ref[pl.ds(j0, TB), slabs[s]])
                    accs[s] = accs[s] + jnp.where(lo, av[0:TB], av[TB:2 * TB])
            return t + 1, tuple(accs), tuple(c2s)

        c_ref[...] = jnp.zeros_like(c_ref)
        init = (jnp.int32(0), tuple(jnp.zeros((TB, 128), F32) for _ in slabs),
                tuple(jnp.zeros((2 * TB, 1), F32) for _ in slabs))
        t, accs, _ = lax.while_loop(cond, step, init)
        for sl, acc in zip(slabs, accs):
            o_ref[:, sl] = acc
            gv = g_ref[:, sl]
            m_ref[:, sl] = (acc * (gv * _sig(gv))).astype(BF16)
        n_ref[p, i] = t

    wide = SB_P * 128
    slab = pl.BlockSpec((R, wide), lambda p, i: (0, p))
    blk = pl.BlockSpec((TB, wide), lambda p, i: (i, p))
    sd = jax.ShapeDtypeStruct((R, D), F32)
    return pl.pallas_call(
        body, grid=(D // wide, NB), in_specs=[blk, slab, slab, blk],
        out_specs=[blk, blk, blk, pl.BlockSpec(memory_space=pltpu.SMEM)],
        out_shape=[sd, jax.ShapeDtypeStruct((R, D), BF16), sd, jax.ShapeDtypeStruct((D // wide, NB), jnp.int32)],
        name="sb_fwd", compiler_params=_cp(("arbitrary", "arbitrary")))(q, k, v, g)


def _sb_bwd(trips, q, k, v, car, dm, g, o):
    def body(n_ref, q_ref, k_ref, v_ref, c_ref, dm_ref, g_ref, o_ref, dq_ref, dko_ref, dvo_ref, dg_ref,
             dk_ref, dv_ref):
        p, i = pl.program_id(0), pl.program_id(1)

        @pl.when(i == 0)
        def _():
            dk_ref[...] = jnp.zeros_like(dk_ref)
            dv_ref[...] = jnp.zeros_like(dv_ref)

        lane, row = _lane_row((TB, 128))
        lo = lane < HD
        tri_lt = (row < lane).astype(BF16)
        slabs = [slice(s * 128, (s + 1) * 128) for s in range(SB_P)]
        q2s, do2s, cts = [], [], []
        for sl in slabs:
            q2, lane2, qpos2, tri_gt = _sb_stack(q_ref[:, sl].astype(F32) * SCALE, i)
            q2s.append(q2)
            gv, dmv = g_ref[:, sl], dm_ref[:, sl]
            sg = _sig(gv)
            dg_ref[:, sl] = (dmv * o_ref[:, sl] * (sg * (1.0 + gv * (1.0 - sg)))).astype(BF16)
            do2s.append(_stack_heads(dmv * (gv * sg)))
            cts.append(c_ref[:, sl])
        trips_i = n_ref[p, i]
        first = jnp.maximum(i + 1 - SB_U * trips_i, 0)

        def step(t, carry):
            dqs, g2s = carry
            dqs, g2s = list(dqs), list(g2s)
            jrs = [first + SB_U * t + u for u in range(SB_U)]
            j0s = [pl.multiple_of(jnp.minimum(jr, i) * TB, TB) for jr in jrs]
            valids = []
            for jr in jrs:
                kpos = jr * TB + lane2
                valids.append((kpos >= PAD) & (kpos < qpos2))
            ks = [[k_ref[pl.ds(j0, TB), sl] for j0 in j0s] for sl in slabs]
            zs = [[jnp.where(valid, _dot_nt(q2s[s], kj), NEG) for kj, valid in zip(ks[s], valids)] for s in range(SB_P)]
            das = [[_dot_nt(do2s[s], v_ref[pl.ds(j0, TB), slabs[s]]) for j0 in j0s] for s in range(SB_P)]
            es = [[jnp.exp(-jnp.abs(z)) for z in zs[s]] for s in range(SB_P)]
            lbs = [[jnp.minimum(z, 0.0) - jnp.log(1.0 + e) for z, e in zip(zs[s], es[s])] for s in range(SB_P)]
            l1s = [[lb - z for lb, z in zip(lbs[s], zs[s])] for s in range(SB_P)]
            sfxs = [[_split_dot(l1, tri_gt) for l1 in l1s[s]] for s in range(SB_P)]
            a_s, gmats, gpres = [], [], []
            for s in range(SB_P):
                a_l, gm_l, gp_l, g2 = [], [], [], g2s[s]
                for jr, valid, lb, sfx, da in zip(jrs, valids, lbs[s], sfxs[s], das[s]):
                    later = jnp.concatenate(
                        [jnp.sum(jnp.where(lane == 2 * jr + hh, cts[s], 0.0), axis=1, keepdims=True) for hh in range(2)],
                        axis=0)
                    a = jnp.exp(lb + sfx + later)
                    gmat = da * a
                    a_l.append(a.astype(BF16))
                    gm_l.append(gmat)
                    gp_l.append(g2)
                    g2 = g2 + jnp.sum(gmat, axis=1, keepdims=True)
                a_s.append(a_l)
                gmats.append(gm_l)
                gpres.append(gp_l)
                g2s[s] = g2
            pres = [[gp + _split_dot(gmat, tri_lt) for gp, gmat in zip(gpres[s], gmats[s])] for s in range(SB_P)]
            for s in range(SB_P):
                for j0, kj, valid, z, e, gmat, pre, a in zip(j0s, ks[s], valids, zs[s], es[s], gmats[s], pres[s], a_s[s]):
                    r = 1.0 / (1.0 + e)
                    big = z >= 0.0
                    beta = jnp.where(big, r, e * r)
                    omb = jnp.where(big, e * r, r)
                    dz = (gmat * omb - beta * pre).astype(BF16)
                    dq2 = _dot(dz, kj)
                    dqs[s] = dqs[s] + jnp.where(lo, dq2[0:TB], dq2[TB:2 * TB])
                    dk_ref[pl.ds(j0, TB), slabs[s]] += _dot_tn(dz, q2s[s])
                    dv_ref[pl.ds(j0, TB), slabs[s]] += _dot_tn(a, do2s[s])
            return tuple(dqs), tuple(g2s)

        init = (tuple(jnp.zeros((TB, 128), F32) for _ in slabs), tuple(jnp.zeros((2 * TB, 1), F32) for _ in slabs))
        dqs, _ = lax.fori_loop(0, trips_i, step, init)
        for sl, dq in zip(slabs, dqs):
            dq_ref[:, sl] = (dq * SCALE).astype(BF16)

        @pl.when(i == NB - 1)
        def _():
            dko_ref[...] = dk_ref[...].astype(BF16)
            dvo_ref[...] = dv_ref[...].astype(BF16)

    wide = SB_P * 128
    slab = pl.BlockSpec((R, wide), lambda p, i: (0, p))
    blk = pl.BlockSpec((TB, wide), lambda p, i: (i, p))
    sd = jax.ShapeDtypeStruct((R, D), BF16)
    return pl.pallas_call(
        body, grid=(D // wide, NB),
        in_specs=[pl.BlockSpec(memory_space=pltpu.SMEM), blk, slab, slab, blk, blk, blk, blk],
        out_specs=[blk, slab, slab, blk], out_shape=[sd, sd, sd, sd],
        scratch_shapes=[pltpu.VMEM((R, wide), F32), pltpu.VMEM((R, wide), F32)], name="sb_bwd",
        compiler_params=_cp(("arbitrary", "arbitrary")))(trips, q, k, v, car, dm, g, o)


def _adamw(w, parts, m, v, name):
    rows, cols = w.shape
    tr = next((t for t in (256, 176) if rows % t == 0), rows)
    nparts = len(parts)

    def body(*refs):
        w_ref = refs[0]
        p_refs = refs[1:1 + nparts]
        m_ref, v_ref, g_ref, d_ref, nm_ref, nv_ref = refs[1 + nparts:]
        g = p_refs[0][...]
        for p_ref in p_refs[1:]:
            g = g + p_ref[...]
        nm = ADAM_B1 * m_ref[...] + (1.0 - ADAM_B1) * g
        nv = ADAM_B2 * v_ref[...] + (1.0 - ADAM_B2) * (g * g)
        m_hat = nm / (1.0 - ADAM_B1 ** ADAM_STEP)
        v_hat = nv / (1.0 - ADAM_B2 ** ADAM_STEP)
        g_ref[...] = g
        d_ref[...] = -ADAM_LR * (m_hat / (jnp.sqrt(v_hat) + ADAM_EPS) + ADAM_WD * w_ref[...])
        nm_ref[...] = nm
        nv_ref[...] = nv

    blk = pl.BlockSpec((tr, cols), lambda i: (i, 0))
    sd = jax.ShapeDtypeStruct((rows, cols), F32)
    return pl.pallas_call(
        body, grid=(rows // tr,), in_specs=[blk] * (3 + nparts), out_specs=[blk] * 4, out_shape=[sd] * 4,
        name=name, compiler_params=_cp(("parallel",)))(w, *parts, m, v)


def _sum8(buf, name):
    _, rows, cols = buf.shape

    def body(b_ref, o_ref):
        acc = b_ref[0]
        for i in range(1, 8):
            acc = acc + b_ref[i]
        o_ref[...] = acc

    return pl.pallas_call(
        body, out_shape=jax.ShapeDtypeStruct((rows, cols), F32), name=name,
        compiler_params=pltpu.CompilerParams(vmem_limit_bytes=VMEM_LIMIT))(buf)


MESH = pl.DeviceIdType.MESH
ANY = pl.BlockSpec(memory_space=pl.ANY)


def _chip_peers():
    x, y = lax.axis_index("x"), lax.axis_index("y")
    return [(1 - x, y), (x, 1 - y), (1 - x, 1 - y)]


def _gather_chips(shards):
    plan = _gather_plan(shards)

    def body(*refs):
        n = len(shards)
        ins, outs, sems = refs[:n], refs[n:2 * n], refs[2 * n:]
        plan["start"](ins, outs, sems)
        plan["mid"](ins, outs, sems)
        plan["finish"](ins, outs, sems)

    n = len(shards)
    res = pl.pallas_call(
        body, in_specs=[ANY] * n, out_specs=[ANY] * n, out_shape=plan["out_shape"],
        scratch_shapes=plan["sems"], name="gather_chips")(*plan["args"])
    return plan["post"](res)


def _gather_plan(shards):
    n = len(shards)
    shards = [s.reshape((2, s.shape[0] // 2) + s.shape[1:]) for s in shards]

    def copies(kind, ins, outs, sems):
        s1, r1, s2, r2 = sems
        x, y, c = lax.axis_index("x"), lax.axis_index("y"), lax.axis_index("c")
        me = 2 * x + y
        out = []
        for j, (px, py) in enumerate(_chip_peers()):
            for a in range(n):
                k = j * n + a
                got = outs[a].at[2 * px + py].at[c]
                other = outs[a].at[2 * px + py].at[1 - c]
                src, dst, ss, rs, dev = {
                    "first": (ins[a].at[c], outs[a].at[me].at[c], s1, r1, (px, py, c)),
                    "landed": (got, got, s1, r1, (px, py, c)),
                    "passed": (got, got, s2, r2, (x, y, 1 - c)),
                    "theirs": (other, other, s2, r2, (x, y, 1 - c)),
                }[kind]
                out.append(pltpu.make_async_remote_copy(
                    src_ref=src, dst_ref=dst, send_sem=ss.at[k], recv_sem=rs.at[k], device_id=dev, device_id_type=MESH))
        return out

    def start(ins, outs, sems):
        for cp in copies("first", ins, outs, sems):
            cp.start()

    def mid(ins, outs, sems):
        for got, fwd in zip(copies("landed", ins, outs, sems), copies("passed", ins, outs, sems)):
            got.wait_recv()
            fwd.start()

    def finish(ins, outs, sems):
        for cp in copies("theirs", ins, outs, sems):
            cp.wait_recv()
        for cp in copies("first", ins, outs, sems) + copies("passed", ins, outs, sems):
            cp.wait_send()

    def post(res):
        me = 2 * lax.axis_index("x") + lax.axis_index("y")
        res = [lax.dynamic_update_index_in_dim(r, s, me, 0) for r, s in zip(res, shards)]
        return [r.reshape((N_CHIPS, 2 * r.shape[2]) + r.shape[3:]) for r in res]

    return dict(args=shards, out_shape=[jax.ShapeDtypeStruct((N_CHIPS,) + s.shape, s.dtype) for s in shards],
                sems=[pltpu.SemaphoreType.DMA((3 * n,))] * 4, start=start, mid=mid, finish=finish, post=post)


def _rows_call(name, parts, plan):
    n_in = [len(p["args"]) for p in parts]
    n_out = [len(p["out_shape"]) for p in parts]
    n_scr = [len(p["scratch"]) for p in parts]
    c_in, c_out = len(plan["args"]), len(plan["out_shape"])

    def split(refs, sizes):
        out, pos = [], 0
        for k in sizes:
            out.append(refs[pos:pos + k])
            pos += k
        return out

    def body(*refs):
        ins, outs, scr = split(refs, [sum(n_in) + c_in, sum(n_out) + c_out, sum(n_scr) + len(plan["sems"])])
        p_in, p_out, p_scr = split(ins, n_in + [c_in]), split(outs, n_out + [c_out]), split(scr, n_scr + [len(plan["sems"])])
        comm = (p_in[-1], p_out[-1], p_scr[-1])
        step = pl.program_id(0)

        @pl.when(step == 0)
        def _():
            plan["start"](*comm)

        for p, i, o, s in zip(parts, p_in, p_out, p_scr):
            p["body"](*i, *o, *s)

        @pl.when(step == NB - 2)
        def _():
            plan["mid"](*comm)

        @pl.when(step == NB - 1)
        def _():
            plan["finish"](*comm)

    flat = lambda key: [v for p in parts for v in p[key]]
    res = pl.pallas_call(
        body, grid=(NB,), in_specs=flat("in_specs") + [ANY] * c_in, out_specs=flat("out_specs") + [ANY] * c_out,
        out_shape=flat("out_shape") + plan["out_shape"], scratch_shapes=flat("scratch") + plan["sems"],
        name=name, compiler_params=_cp(("arbitrary",)))(*flat("args"), *plan["args"])
    outs = split(res, n_out + [c_out])
    return outs[:-1], outs[-1]


def _pair_exchange(grads, name):
    n = len(grads)
    hs = [g.shape[1] // 2 for g in grads]
    grads = [g.reshape((N_CHIPS, 2, h) + g.shape[2:]) for g, h in zip(grads, hs)]

    def body(*refs):
        ins, got = refs[:n], refs[n:2 * n]
        ssem, rsem = refs[2 * n:]
        x, y, c = lax.axis_index("x"), lax.axis_index("y"), lax.axis_index("c")
        sends = [pltpu.make_async_remote_copy(
            src_ref=ins[a].at[:, 1 - c], dst_ref=got[a], send_sem=ssem.at[a],
            recv_sem=rsem.at[a], device_id=(x, y, 1 - c), device_id_type=MESH) for a in range(n)]
        for cp in sends:
            cp.start()
        for cp in sends:
            cp.wait()

    half_shapes = [jax.ShapeDtypeStruct((N_CHIPS, h) + g.shape[3:], g.dtype) for g, h in zip(grads, hs)]
    got = pl.pallas_call(
        body, in_specs=[ANY] * n, out_specs=[ANY] * n, out_shape=half_shapes,
        scratch_shapes=[pltpu.SemaphoreType.DMA((n,))] * 2, name=name)(*grads)
    return grads, got


def _sum_pair(both, got, send_dtype, name):
    _, _, rows, cols = both.shape
    tr = 256 if rows % 256 == 0 else rows

    def body(c_ref, a_ref, b_ref, f_ref, s_ref):
        t = a_ref[...].astype(F32) + b_ref[...].astype(F32)
        f_ref[...] = t
        s_ref[...] = t.astype(send_dtype)

    blk = pl.BlockSpec((N_CHIPS, tr, cols), lambda i, c: (0, i, 0))
    mine = pl.BlockSpec((N_CHIPS, None, tr, cols), lambda i, c: (0, c[0], i, 0))
    return pl.pallas_call(
        body, grid_spec=pltpu.PrefetchScalarGridSpec(
            num_scalar_prefetch=1, grid=(rows // tr,), in_specs=[mine, blk], out_specs=[blk, blk]),
        out_shape=[jax.ShapeDtypeStruct(got.shape, F32), jax.ShapeDtypeStruct(got.shape, send_dtype)],
        name=name, compiler_params=_cp(("parallel",)))(lax.axis_index("c").reshape(1), both, got)


def _scatter_plan(send):
    n = len(send)

    def copies(sin, land, sems):
        ssem, rsem = sems
        c = lax.axis_index("c")
        return [pltpu.make_async_remote_copy(
            src_ref=sin[a].at[2 * px + py], dst_ref=land[a].at[j], send_sem=ssem.at[j * n + a],
            recv_sem=rsem.at[j * n + a], device_id=(px, py, c), device_id_type=MESH)
            for j, (px, py) in enumerate(_chip_peers()) for a in range(n)]

    def start(sin, land, sems):
        for cp in copies(sin, land, sems):
            cp.start()

    def finish(sin, land, sems):
        for cp in copies(sin, land, sems):
            cp.wait()

    return dict(args=list(send), out_shape=[jax.ShapeDtypeStruct((3,) + s.shape[1:], s.dtype) for s in send],
                sems=[pltpu.SemaphoreType.DMA((3 * n,))] * 2, start=start, mid=lambda *a: None, finish=finish)


def _sum_shard(keep, land, name):
    _, rows, cols = keep.shape
    tr = 256 if rows % 256 == 0 else rows

    def body(me_ref, m_ref, l_ref, o_ref):
        o_ref[...] = ((m_ref[...] + l_ref[0].astype(F32)) + l_ref[1].astype(F32)) + l_ref[2].astype(F32)

    own = pl.BlockSpec((None, tr, cols), lambda i, me: (me[0], i, 0))
    me = (2 * lax.axis_index("x") + lax.axis_index("y")).reshape(1)
    return pl.pallas_call(
        body, grid_spec=pltpu.PrefetchScalarGridSpec(
            num_scalar_prefetch=1, grid=(rows // tr,),
            in_specs=[own, pl.BlockSpec((3, tr, cols), lambda i, me: (0, i, 0))],
            out_specs=pl.BlockSpec((tr, cols), lambda i, me: (i, 0))),
        out_shape=jax.ShapeDtypeStruct((rows, cols), F32),
        name=name, compiler_params=_cp(("parallel",)))(me, keep, land)


def _join_cores(halves):
    n = len(halves)

    def body(*refs):
        ins, outs = refs[:n], refs[n:2 * n]
        ssem, rsem = refs[2 * n:]
        x, y, c = lax.axis_index("x"), lax.axis_index("y"), lax.axis_index("c")
        sends = [pltpu.make_async_remote_copy(
            src_ref=ins[a], dst_ref=outs[a].at[c], send_sem=ssem.at[a], recv_sem=rsem.at[a],
            device_id=(x, y, 1 - c), device_id_type=MESH) for a in range(n)]
        for cp in sends:
            cp.start()
        for a in range(n):
            sends[a].wait_send()
            pltpu.make_async_remote_copy(
                src_ref=ins[a], dst_ref=outs[a].at[1 - c], send_sem=ssem.at[a], recv_sem=rsem.at[a],
                device_id=(x, y, 1 - c), device_id_type=MESH).wait_recv()

    res = pl.pallas_call(
        body, in_specs=[ANY] * n, out_specs=[ANY] * n,
        out_shape=[jax.ShapeDtypeStruct((2,) + h.shape, h.dtype) for h in halves],
        scratch_shapes=[pltpu.SemaphoreType.DMA((n,))] * 2, name="join_cores")(*halves)
    c = lax.axis_index("c")
    res = [lax.dynamic_update_index_in_dim(r, h, c, 0) for r, h in zip(res, halves)]
    return [r.reshape((2 * r.shape[1],) + r.shape[2:]) for r in res]


def _gather_all(vec):
    def copies(kind, ins, outs, sems):
        ssem, rsem = sems
        x, y, c = lax.axis_index("x"), lax.axis_index("y"), lax.axis_index("c")
        me = 4 * x + 2 * y + c
        out = []
        for k in range(1, 8):
            px, py, pc = x ^ (k >> 2), y ^ ((k >> 1) & 1), c ^ (k & 1)
            dst = outs[0].at[me] if kind == "send" else outs[0].at[4 * px + 2 * py + pc]
            out.append(pltpu.make_async_remote_copy(
                src_ref=ins[0], dst_ref=dst, send_sem=ssem.at[k - 1], recv_sem=rsem.at[k - 1],
                device_id=(px, py, pc), device_id_type=MESH))
        return out

    def body(v_ref, o_ref, ssem, rsem):
        refs = ([v_ref], [o_ref], (ssem, rsem))
        for cp in copies("send", *refs):
            cp.start()
        for cp in copies("recv", *refs):
            cp.wait_recv()
        for cp in copies("send", *refs):
            cp.wait_send()

    res = pl.pallas_call(
        body, in_specs=[ANY], out_specs=ANY, out_shape=jax.ShapeDtypeStruct((8,) + vec.shape, vec.dtype),
        scratch_shapes=[pltpu.SemaphoreType.DMA((7,))] * 2, name="gather_all")(vec)
    me = 4 * lax.axis_index("x") + 2 * lax.axis_index("y") + lax.axis_index("c")
    return lax.dynamic_update_index_in_dim(res, vec, me, 0)


def _rope_tables():
    pos = (jnp.arange(R, dtype=jnp.int32) - PAD).astype(F32)
    half = HD // 2
    inv = ROPE_THETA ** (-jnp.arange(half, dtype=F32) / half)
    ang = pos[:, None] * inv[None, :]
    cos, sin = jnp.cos(ang), jnp.sin(ang)
    cs = jnp.tile(cos, (1, 4))
    sn = jnp.tile(jnp.concatenate([-sin, sin], axis=1), (1, 2))
    return cs, sn


def _local_step(x, target, p):
    w0t = p["ab_w_in"]
    conv_w = jnp.concatenate([p["ab_conv_w"], jnp.zeros((1, CC), F32)], axis=0)
    cs, sn = _rope_tables()

    h0 = jnp.concatenate([jnp.zeros((PAD, D), F32), p["meta_tokens"], x], axis=0)

    xn0 = _rms_fwd(h0, p["ab_pre_norm"], "rms_fwd0")
    plan = _gather_plan([p["sb_w_out"], p["ab_w_out"], p["ab_w_pw2"]])
    z0, gathered = _mm([(xn0, w0t)], F32, "in_proj0", 544, 1408, tb=True, plan=plan)
    wo1, wo0, wpw = plan["post"](gathered)
    wo1, wo0, wpw = wo1.reshape(D, D), wo0.reshape(D, D), wpw.reshape(CC, CC)
    plan = _gather_plan([p["sb_w_in"]])
    ((o0, a0, lse0), (cv0, s0)), gathered = _rows_call(
        "fwd0", [_swa_fwd(z0, cs, sn, p["ab_sinks"]),
                 _conv_fwd(z0, conv_w, p["ab_conv_b"], p["ab_conv_ln_g"], p["ab_conv_ln_b"])], plan)
    (w1,) = plan["post"](gathered)
    t0, c0 = _pw2_fwd(s0, wpw, z0)
    wo0h = wo0.reshape(2, CC, D)
    y0, h1, xn1 = _post_rms_fwd([a0, c0], wo0h, h0, p["ab_post_norm"], p["sb_pre_norm"], "out_proj0_norms")

    q1, k1, v1, g1 = _in_proj1(xn1, w1)
    o1, m1, car1, trips1 = _sb_fwd(q1, k1, v1, g1)

    dh2, dy1, d_sb_post, loss_row = _tail(m1, wo1, h1, p["sb_post_norm"], target)

    dm1 = _mm([(dy1, wo1)], F32, "out_proj1_dx", 544, 1024, tb=True)
    d_wo1 = _mm([(m1, dy1)], BF16, "out_proj1_dw", 512, 1024, ta=True)
    dq1, dk1, dv1, dg1 = _sb_bwd(trips1, q1, k1, v1, car1, dm1, g1, o1)
    dz1 = [dq1, dk1, dv1, dg1]
    d_w1 = jnp.stack([_mm([(xn1, dz1[j])], BF16, "in_proj1_dw%d" % j, 512, 1024, ta=True) for j in range(4)])

    dh1, d_sb_pre, dy0, d_ab_post = _rms_post_bwd(dz1, w1, h1, p["sb_pre_norm"], dh2, y0, p["ab_post_norm"],
                                                  "in_proj1_dx_norms")
    dmix0 = _mm([(dy0, wo0)], F32, "out_proj0_dx", 544, 1024, tb=True)
    d_wo0 = jnp.concatenate([_mm([(a0, dy0)], BF16, "out_proj0_dw_a", 512, 1024, ta=True),
                             _mm([(c0, dy0)], BF16, "out_proj0_dw_b", 512, 1024, ta=True)], axis=0)
    dgb0, ds0, d_wpw = _pw2_bwd(dmix0, t0, z0, wpw, s0)
    d_wpw = d_wpw.astype(BF16)
    early = ("sb_w_in", "sb_w_out", "ab_w_out", "ab_w_pw2")
    own1, got1 = _pair_exchange([d_w1, d_wo1.reshape(N_CHIPS, 256, D), d_wo0.reshape(N_CHIPS, 256, D),
                                 d_wpw.reshape(N_CHIPS, 128, CC)], "pair_exchange1")
    pair1 = [_sum_pair(o, t, BF16, "sum_pair_" + nm) for o, t, nm in zip(own1, got1, early)]
    plan = _scatter_plan([pr[1] for pr in pair1])
    ((dglu0, d_convw, d_small), (dq0, dga0, dkv0, d_sinks)), land1 = _rows_call(
        "bwd0", [_conv_bwd(ds0, cv0, z0, conv_w, p["ab_conv_ln_g"], p["ab_conv_ln_b"]),
                 _swa_bwd(z0, cs, sn, p["ab_sinks"], o0, dmix0, lse0)], plan)
    halves1 = [_sum_shard(pr[0], la, "sum_shard_" + nm) for pr, la, nm in zip(pair1, land1, early)]
    dz0 = jnp.concatenate([dq0, dkv0, dga0, dglu0, dgb0], axis=1)
    d_w0t = _mm([(dz0, xn0)], BF16, "in_proj0_dw", 1408, 512, ta=True)
    own0, got0 = _pair_exchange([d_w0t.reshape(N_CHIPS, 704, D)], "pair_exchange0")
    keep0, send0 = _sum_pair(own0[0], got0[0], BF16, "sum_pair_ab_w_in")
    plan = _scatter_plan([send0])
    dxn0, land0 = _mm([(dz0, w0t)], F32, "in_proj0_dx", 544, 1024, plan=plan)
    half0 = _sum_shard(keep0, land0[0], "sum_shard_ab_w_in")
    dh0_first, grad_x, d_ab_pre = _rms_bwd(dxn0, h0, p["ab_pre_norm"], dh1, F32, "rms_bwd0", split=True)

    grads = {
        "meta_tokens": dh0_first[PAD:TB], "ab_pre_norm": d_ab_pre, "ab_sinks": d_sinks[0:1, 0:8],
        "ab_conv_w": d_convw[0:CONV_W], "ab_conv_b": d_small[0:1], "ab_conv_ln_g": d_small[1:2],
        "ab_conv_ln_b": d_small[2:3], "ab_post_norm": d_ab_post, "sb_pre_norm": d_sb_pre, "sb_post_norm": d_sb_post,
    }
    h_sb_in, h_sb_out, h_ab_out, h_pw2 = halves1
    return loss_row, grad_x, grads, [half0, h_ab_out, h_pw2, h_sb_in, h_sb_out]


SMALL_ROWS = 80
REP_ROWS = 32

WEIGHTS = ["meta_tokens", "ab_pre_norm", "ab_w_in", "ab_sinks", "ab_conv_w", "ab_conv_b", "ab_conv_ln_g",
           "ab_conv_ln_b", "ab_w_pw2", "ab_w_out", "ab_post_norm", "sb_pre_norm", "sb_w_in", "sb_w_out",
           "sb_post_norm"]
BIG = ["ab_w_in", "ab_w_out", "ab_w_pw2", "sb_w_in", "sb_w_out"]


def _pack_small(conv_w, meta, sb_pre, sb_post):
    pad = lambda a, rows: jnp.pad(a, ((0, rows - a.shape[0]), (0, 0)))
    return jnp.concatenate([pad(conv_w, 32), meta.reshape(32, 128), pad(sb_pre.reshape(2, 128), 8),
                            pad(sb_post.reshape(2, 128), 8)], axis=0)


def _unpack_small(s):
    return s[0:31], s[32:64].reshape(16, 256), s[64:66].reshape(1, 256), s[72:74].reshape(1, 256)


REP_LOSS = 3592


def _pack_rep(pre, post, conv_b, ln_g, ln_b, sinks, extra=None):
    flat = jnp.concatenate([pre.reshape(-1), post.reshape(-1), conv_b.reshape(-1), ln_g.reshape(-1),
                            ln_b.reshape(-1), sinks.reshape(-1)] + ([] if extra is None else [extra.reshape(-1)]))
    flat = jnp.concatenate([flat, jnp.zeros((REP_ROWS * 128 - flat.shape[0],), F32)])
    return flat.reshape(REP_ROWS, 128)


def _unpack_rep(r):
    f = r.reshape(-1)
    return (f[0:1024].reshape(1, 1024), f[1024:2048].reshape(1, 1024), f[2048:2560].reshape(1, 512),
            f[2560:3072].reshape(1, 512), f[3072:3584].reshape(1, 512), f[3584:3592].reshape(1, 8))


def _chips_to_cols(w):
    return w.transpose(1, 0, 2).reshape(w.shape[1], -1)


def kernel(x, meta_tokens, ab_pre_norm, ab_w_in, ab_sinks, ab_conv_w, ab_conv_b, ab_conv_ln_g, ab_conv_ln_b, ab_w_pw2, ab_w_out, ab_post_norm, sb_pre_norm, sb_w_in, sb_w_out, sb_post_norm, loss_target, m_meta_tokens, m_ab_pre_norm, m_ab_w_in, m_ab_sinks, m_ab_conv_w, m_ab_conv_b, m_ab_conv_ln_g, m_ab_conv_ln_b, m_ab_w_pw2, m_ab_w_out, m_ab_post_norm, m_sb_pre_norm, m_sb_w_in, m_sb_w_out, m_sb_post_norm, v_meta_tokens, v_ab_pre_norm, v_ab_w_in, v_ab_sinks, v_ab_conv_w, v_ab_conv_b, v_ab_conv_ln_g, v_ab_conv_ln_b, v_ab_w_pw2, v_ab_w_out, v_ab_post_norm, v_sb_pre_norm, v_sb_w_in, v_sb_w_out, v_sb_post_norm):
    w = dict(meta_tokens=meta_tokens, ab_pre_norm=ab_pre_norm, ab_w_in=ab_w_in, ab_sinks=ab_sinks,
             ab_conv_w=ab_conv_w, ab_conv_b=ab_conv_b, ab_conv_ln_g=ab_conv_ln_g, ab_conv_ln_b=ab_conv_ln_b,
             ab_w_pw2=ab_w_pw2, ab_w_out=ab_w_out, ab_post_norm=ab_post_norm, sb_pre_norm=sb_pre_norm,
             sb_w_in=sb_w_in, sb_w_out=sb_w_out, sb_post_norm=sb_post_norm)
    m = dict(meta_tokens=m_meta_tokens, ab_pre_norm=m_ab_pre_norm, ab_w_in=m_ab_w_in, ab_sinks=m_ab_sinks,
             ab_conv_w=m_ab_conv_w, ab_conv_b=m_ab_conv_b, ab_conv_ln_g=m_ab_conv_ln_g,
             ab_conv_ln_b=m_ab_conv_ln_b, ab_w_pw2=m_ab_w_pw2, ab_w_out=m_ab_w_out, ab_post_norm=m_ab_post_norm,
             sb_pre_norm=m_sb_pre_norm, sb_w_in=m_sb_w_in, sb_w_out=m_sb_w_out, sb_post_norm=m_sb_post_norm)
    v = dict(meta_tokens=v_meta_tokens, ab_pre_norm=v_ab_pre_norm, ab_w_in=v_ab_w_in, ab_sinks=v_ab_sinks,
             ab_conv_w=v_ab_conv_w, ab_conv_b=v_ab_conv_b, ab_conv_ln_g=v_ab_conv_ln_g,
             ab_conv_ln_b=v_ab_conv_ln_b, ab_w_pw2=v_ab_w_pw2, ab_w_out=v_ab_w_out, ab_post_norm=v_ab_post_norm,
             sb_pre_norm=v_sb_pre_norm, sb_w_in=v_sb_w_in, sb_w_out=v_sb_w_out, sb_post_norm=v_sb_post_norm)

    def small_of(d):
        return _pack_small(d["ab_conv_w"][0], d["meta_tokens"], d["sb_pre_norm"], d["sb_post_norm"])

    def rep_of(d):
        return _pack_rep(d["ab_pre_norm"], d["ab_post_norm"], d["ab_conv_b"], d["ab_conv_ln_g"], d["ab_conv_ln_b"],
                         d["ab_sinks"])

    g_in0, g_small = _gather_chips([ab_w_in[0].T.astype(BF16), small_of(w)])
    conv_w_f = _chips_to_cols(g_small[:, 0:31])
    meta_f = _chips_to_cols(g_small[:, 32:64].reshape(N_CHIPS, 16, 256))
    sb_pre_f = g_small[:, 64:66].reshape(1, D)
    sb_post_f = g_small[:, 72:74].reshape(1, D)
    full = {
        "meta_tokens": meta_f, "ab_pre_norm": ab_pre_norm, "ab_w_in": g_in0.reshape(AB_IN, D),
        "ab_sinks": ab_sinks, "ab_conv_w": conv_w_f, "ab_conv_b": ab_conv_b, "ab_conv_ln_g": ab_conv_ln_g,
        "ab_conv_ln_b": ab_conv_ln_b, "ab_w_pw2": ab_w_pw2[0].astype(BF16), "ab_w_out": ab_w_out[0].astype(BF16),
        "ab_post_norm": ab_post_norm, "sb_pre_norm": sb_pre_f, "sb_w_in": sb_w_in[0].astype(BF16),
        "sb_w_out": sb_w_out[0].astype(BF16), "sb_post_norm": sb_post_f,
    }

    loss_row, grad_x, g, halves = _local_step(x[0], loss_target[0], full)

    total = _join_cores(halves)

    rep_g = _pack_rep(g["ab_pre_norm"], g["ab_post_norm"], g["ab_conv_b"], g["ab_conv_ln_g"], g["ab_conv_ln_b"],
                      g["ab_sinks"], loss_row[0:1, 0:1])
    vec = jnp.concatenate([rep_g, jnp.pad(g["ab_conv_w"].reshape(124, 128), ((0, 4), (0, 0))),
                           g["meta_tokens"].reshape(128, 128), g["sb_pre_norm"].reshape(8, 128),
                           g["sb_post_norm"].reshape(8, 128)], axis=0)
    vec_sum = _sum8(_gather_all(vec), "sum8_small")

    out_g, out_d, out_m, out_v = {}, {}, {}, {}
    for i, k in enumerate(BIG):
        shp = w[k].shape
        if k == "ab_w_in":
            res = _adamw(w[k][0].T, [total[i]], m[k][0].T, v[k][0].T, "adamw_" + k)
            out_g[k], out_d[k], out_m[k], out_v[k] = [r.T.reshape(shp) for r in res]
            continue
        res = _adamw(w[k][0], [total[i]], m[k][0], v[k][0], "adamw_" + k)
        out_g[k], out_d[k], out_m[k], out_v[k] = [r.reshape(shp) for r in res]

    rep_sum = vec_sum[0:REP_ROWS]
    loss = rep_sum.reshape(-1)[REP_LOSS]
    me = 2 * lax.axis_index("x") + lax.axis_index("y")
    small_sum = _pack_small(
        lax.dynamic_slice_in_dim(vec_sum[32:156].reshape(CONV_W, CC), me * 128, 128, axis=1),
        lax.dynamic_slice_in_dim(vec_sum[160:288].reshape(N_META, D), me * 256, 256, axis=1),
        lax.dynamic_slice_in_dim(vec_sum[288:296].reshape(1, D), me * 256, 256, axis=1),
        lax.dynamic_slice_in_dim(vec_sum[296:304].reshape(1, D), me * 256, 256, axis=1))
    res = _adamw(small_of(w), [small_sum], small_of(m), small_of(v), "adamw_small")
    for dst, r in zip((out_g, out_d, out_m, out_v), res):
        cw, mt, pre, post = _unpack_small(r)
        dst["ab_conv_w"], dst["meta_tokens"], dst["sb_pre_norm"], dst["sb_post_norm"] = cw[None], mt, pre, post
    res = _adamw(rep_of(w), [rep_sum], rep_of(m), rep_of(v), "adamw_rep")
    for dst, r in zip((out_g, out_d, out_m, out_v), res):
        (dst["ab_pre_norm"], dst["ab_post_norm"], dst["ab_conv_b"], dst["ab_conv_ln_g"], dst["ab_conv_ln_b"],
         dst["ab_sinks"]) = _unpack_rep(r)

    return (loss, grad_x[None], *[out_g[k] for k in WEIGHTS], *[out_d[k] for k in WEIGHTS],
            *[out_m[k] for k in WEIGHTS], *[out_v[k] for k in WEIGHTS])
```

```python
import functools

import jax
import jax.numpy as jnp
from jax import lax
from jax.experimental import pallas as pl
from jax.experimental.pallas import tpu as pltpu

F32 = jnp.float32
BF16 = jnp.bfloat16

D = 1024
SEQ = 2048
N_META = 16
TB = 128
TR = 272
PAD = TB - N_META
R = SEQ + TB
NB = R // TB
HD = 64
ROPE_THETA = 10000.0
NORM_EPS = 1e-6
LN_EPS = 1e-5
NEG = -1e30
CONV_W = 31
SCALE = HD ** -0.5
N_CHIPS = 4

C_Q, C_K, C_V, C_GA, C_GLU, C_GB = 0, 512, 640, 768, 1280, 2304
AB_IN = 2816

ADAM_LR, ADAM_B1, ADAM_B2, ADAM_EPS, ADAM_WD, ADAM_STEP = 0.001, 0.9, 0.999, 1e-08, 0.01, 10

VMEM_LIMIT = 56 * 1024 * 1024


def _cp(sem):
    return pltpu.CompilerParams(dimension_semantics=sem, vmem_limit_bytes=VMEM_LIMIT)


def _sig(x):
    return 1.0 / (1.0 + jnp.exp(-x))


def _dot(a, b):
    return lax.dot_general(a, b, (((1,), (0,)), ((), ())), preferred_element_type=F32)


def _dot_nt(a, b):
    return lax.dot_general(a, b, (((1,), (1,)), ((), ())), preferred_element_type=F32)


def _dot_tn(a, b):
    return lax.dot_general(a, b, (((0,), (0,)), ((), ())), preferred_element_type=F32)


def _mm(pairs, out_dtype, name, tm, tn, ta=False, tb=False, plan=None):
    pairs = [(a, b if isinstance(b, tuple) else (b, None)) for a, b in pairs]
    a0, (b0, _) = pairs[0]
    m = a0.shape[1] if ta else a0.shape[0]
    n = b0.shape[-2] if tb else b0.shape[-1]
    npairs = len(pairs)
    dims = (((0 if ta else 1,), (1 if tb else 0,)), ((), ()))
    c_in = len(plan["args"]) if plan else 0
    c_out = len(plan["out_shape"]) if plan else 0
    steps = (m // tm) * (n // tn)

    def body(*refs):
        o_ref = refs[2 * npairs + c_in]
        if plan:
            comm = (refs[2 * npairs:2 * npairs + c_in], refs[2 * npairs + c_in + 1:2 * npairs + c_in + 1 + c_out],
                    refs[2 * npairs + c_in + 1 + c_out:])
            step = pl.program_id(0) * (n // tn) + pl.program_id(1)

            @pl.when(step == 0)
            def _():
                plan["start"](*comm)

        acc = None
        for i in range(npairs):
            t = lax.dot_general(refs[2 * i][...].astype(BF16), refs[2 * i + 1][...].astype(BF16), dims,
                                preferred_element_type=F32)
            acc = t if acc is None else acc + t
        o_ref[...] = acc.astype(out_dtype)
        if plan:
            @pl.when(step == steps - 2)
            def _():
                plan["mid"](*comm)

            @pl.when(step == steps - 1)
            def _():
                plan["finish"](*comm)

    in_specs, args = [], []
    for a, (b, sel) in pairs:
        k = a.shape[0] if ta else a.shape[1]
        in_specs.append(pl.BlockSpec((k, tm), lambda i, j: (0, i)) if ta else pl.BlockSpec((tm, k), lambda i, j: (i, 0)))
        bshape, bidx = ((tn, k), lambda i, j: (j, 0)) if tb else ((k, tn), lambda i, j: (0, j))
        if sel is None:
            in_specs.append(pl.BlockSpec(bshape, bidx))
        else:
            in_specs.append(pl.BlockSpec((None,) + bshape, functools.partial(lambda i, j, f, s: (s,) + f(i, j), f=bidx, s=sel)))
        args += [a, b]
    out_spec = pl.BlockSpec((tm, tn), lambda i, j: (i, j))
    out_shape = jax.ShapeDtypeStruct((m, n), out_dtype)
    if not plan:
        return pl.pallas_call(
            body, grid=(m // tm, n // tn), in_specs=in_specs, out_specs=out_spec, out_shape=out_shape, name=name,
            compiler_params=_cp(("parallel", "parallel")))(*args)
    assert steps >= 2
    res = pl.pallas_call(
        body, grid=(m // tm, n // tn), in_specs=in_specs + [ANY] * c_in, out_specs=[out_spec] + [ANY] * c_out,
        out_shape=[out_shape] + plan["out_shape"], scratch_shapes=plan["sems"], name=name,
        compiler_params=_cp(("arbitrary", "arbitrary")))(*args, *plan["args"])
    return res[0], res[1:]


PW_TM = 544


def _pw2_fwd(s, w, z0):
    def body(s_ref, w_ref, g_ref, t_ref, c_ref):
        t = _dot(s_ref[...], w_ref[...])
        gv = g_ref[...]
        t_ref[...] = t
        c_ref[...] = (t * (gv * _sig(gv))).astype(BF16)

    blk = pl.BlockSpec((PW_TM, CC), lambda i: (i, 0))
    return pl.pallas_call(
        body, grid=(R // PW_TM,),
        in_specs=[blk, pl.BlockSpec((CC, CC), lambda i: (0, 0)), _cols_spec(PW_TM, CC, lambda i: (i, C_GB))],
        out_specs=[blk, blk],
        out_shape=[jax.ShapeDtypeStruct((R, CC), F32), jax.ShapeDtypeStruct((R, CC), BF16)],
        name="pw2_fwd", compiler_params=_cp(("parallel",)))(s, w, z0)


def _pw2_bwd(dmix, t, z0, w, s):
    def body(d_ref, t_ref, g_ref, w_ref, s_ref, dg_ref, ds_ref, dw_ref):
        @pl.when(pl.program_id(0) == 0)
        def _():
            dw_ref[...] = jnp.zeros_like(dw_ref)

        gv, dv = g_ref[...], d_ref[...]
        sg = _sig(gv)
        dg_ref[...] = (dv * t_ref[...] * (sg * (1.0 + gv * (1.0 - sg)))).astype(BF16)
        dt = (dv * (gv * sg)).astype(BF16)
        ds_ref[...] = _dot_nt(dt, w_ref[...])
        dw_ref[...] += _dot_tn(s_ref[...], dt)

    blk = pl.BlockSpec((PW_TM, CC), lambda i: (i, 0))
    full = pl.BlockSpec((CC, CC), lambda i: (0, 0))
    return pl.pallas_call(
        body, grid=(R // PW_TM,),
        in_specs=[_cols_spec(PW_TM, CC, lambda i: (i, CC)), blk, _cols_spec(PW_TM, CC, lambda i: (i, C_GB)), full, blk],
        out_specs=[blk, blk, full],
        out_shape=[jax.ShapeDtypeStruct((R, CC), BF16), jax.ShapeDtypeStruct((R, CC), F32),
                   jax.ShapeDtypeStruct((CC, CC), F32)],
        name="pw2_bwd", compiler_params=_cp(("arbitrary",)))(dmix, t, z0, w, s)


def _embed_proj(x, meta, g, wt):
    def body(x_ref, m_ref, g_ref, w_ref, h_ref, xn_ref, z_ref):
        n = pl.program_id(0)
        top = jnp.concatenate([jnp.zeros((PAD, D), F32), m_ref[...]], axis=0)
        hv = jnp.where(n == 0, top, x_ref[...])
        h_ref[...] = hv
        r = lax.rsqrt(jnp.mean(hv * hv, axis=1, keepdims=True) + NORM_EPS)
        xn = (hv * r * g_ref[...]).astype(BF16)
        xn_ref[...] = xn
        z_ref[...] = _dot_nt(xn, w_ref[...])

    blk = pl.BlockSpec((TB, D), lambda n: (n, 0))
    return dict(
        body=body,
        in_specs=[pl.BlockSpec((TB, D), lambda n: (jnp.maximum(n - 1, 0), 0)), pl.BlockSpec((N_META, D), lambda n: (0, 0)),
                  pl.BlockSpec((1, D), lambda n: (0, 0)), pl.BlockSpec(wt.shape, lambda n: (0, 0))],
        args=[x, meta, g, wt],
        out_specs=[blk, blk, pl.BlockSpec((TB, AB_IN), lambda n: (n, 0))],
        out_shape=[jax.ShapeDtypeStruct((R, D), F32), jax.ShapeDtypeStruct((R, D), BF16),
                   jax.ShapeDtypeStruct((R, AB_IN), F32)],
        scratch=[])


def _in_proj1(xn, w):
    def body(x_ref, w_ref, q_ref, k_ref, v_ref, g_ref):
        xv = x_ref[...]
        q_ref[...] = _dot(xv, w_ref[0]).astype(BF16)
        k_ref[...] = _dot(xv, w_ref[1]).astype(BF16)
        v_ref[...] = _dot(xv, w_ref[2]).astype(BF16)
        g_ref[...] = _dot(xv, w_ref[3])

    blk = pl.BlockSpec((TR, D), lambda n: (n, 0))
    sd = jax.ShapeDtypeStruct((R, D), BF16)
    return pl.pallas_call(
        body, grid=(R // TR,), in_specs=[blk, pl.BlockSpec(w.shape, lambda n: (0, 0, 0))],
        out_specs=[blk, blk, blk, blk], out_shape=[sd, sd, sd, jax.ShapeDtypeStruct((R, D), F32)],
        name="in_proj1", compiler_params=_cp(("parallel",)))(xn, w)


def _rms_fwd(h, g, name):
    def body(h_ref, g_ref, o_ref):
        x = h_ref[...]
        r = lax.rsqrt(jnp.mean(x * x, axis=1, keepdims=True) + NORM_EPS)
        o_ref[...] = (x * r * g_ref[...]).astype(BF16)

    return pl.pallas_call(
        body, grid=(R // TR,),
        in_specs=[pl.BlockSpec((TR, D), lambda n: (n, 0)), pl.BlockSpec((1, D), lambda n: (0, 0))],
        out_specs=pl.BlockSpec((TR, D), lambda n: (n, 0)),
        out_shape=jax.ShapeDtypeStruct((R, D), BF16), name=name, compiler_params=_cp(("parallel",)))(h, g)


def _rms_bwd(dout, x, g, res, out_dtype, name, split=False):
    has_res = res is not None

    def body(*refs):
        if split:
            refs = list(refs)
            dx_rest_ref = refs.pop(-2)
        if has_res:
            d_ref, x_ref, g_ref, r_ref, dx_ref, dg_ref = refs
        else:
            d_ref, x_ref, g_ref, dx_ref, dg_ref = refs
        n = pl.program_id(0)
        xv = x_ref[...]
        dv = d_ref[...]
        r = lax.rsqrt(jnp.mean(xv * xv, axis=1, keepdims=True) + NORM_EPS)
        xh = xv * r
        dxh = dv * g_ref[...]
        dx = r * (dxh - xh * jnp.mean(dxh * xh, axis=1, keepdims=True))
        if has_res:
            dx = dx + r_ref[...]
        row = lax.broadcasted_iota(jnp.int32, (TB, D), 0) + n * TB
        dx = jnp.where(row >= PAD, dx, 0.0).astype(out_dtype)
        if split:
            @pl.when(n == 0)
            def _():
                dx_ref[...] = dx

            @pl.when(n > 0)
            def _():
                dx_rest_ref[...] = dx
        else:
            dx_ref[...] = dx

        @pl.when(n == 0)
        def _():
            dg_ref[...] = jnp.zeros_like(dg_ref)

        dg_ref[...] += jnp.sum(dv * xh, axis=0, keepdims=True)

    blk = pl.BlockSpec((TB, D), lambda n: (n, 0))
    vec = pl.BlockSpec((1, D), lambda n: (0, 0))
    ins = [dout, x, g] + ([res] if has_res else [])
    in_specs = [blk, blk, vec] + ([blk] if has_res else [])
    if split:
        out_specs = [pl.BlockSpec((TB, D), lambda n: (0, 0)), pl.BlockSpec((TB, D), lambda n: (jnp.maximum(n - 1, 0), 0)), vec]
        out_shape = [jax.ShapeDtypeStruct((TB, D), out_dtype), jax.ShapeDtypeStruct((SEQ, D), out_dtype),
                     jax.ShapeDtypeStruct((1, D), F32)]
    else:
        out_specs = [blk, vec]
        out_shape = [jax.ShapeDtypeStruct((R, D), out_dtype), jax.ShapeDtypeStruct((1, D), F32)]
    return pl.pallas_call(
        body, grid=(NB,), in_specs=in_specs, out_specs=out_specs, out_shape=out_shape,
        name=name, compiler_params=_cp(("arbitrary",)))(*ins)


def _post_rms_fwd(xs, w, h, g_post, g_next, name):
    nx = len(xs)

    def body(*refs):
        x_refs, (w_ref, h_ref, gp_ref, gn_ref, y_ref, o_ref, xn_ref) = refs[:nx], refs[nx:]
        yv = _dot(x_refs[0][...], w_ref[0])
        for j in range(1, nx):
            yv = yv + _dot(x_refs[j][...], w_ref[j])
        y_ref[...] = yv
        r = lax.rsqrt(jnp.mean(yv * yv, axis=1, keepdims=True) + NORM_EPS)
        hn = h_ref[...] + yv * r * gp_ref[...]
        o_ref[...] = hn
        r2 = lax.rsqrt(jnp.mean(hn * hn, axis=1, keepdims=True) + NORM_EPS)
        xn_ref[...] = (hn * r2 * gn_ref[...]).astype(BF16)

    blk = pl.BlockSpec((TR, D), lambda n: (n, 0))
    vec = pl.BlockSpec((1, D), lambda n: (0, 0))
    xblk = [pl.BlockSpec((TR, x.shape[1]), lambda n: (n, 0)) for x in xs]
    return pl.pallas_call(
        body, grid=(R // TR,), in_specs=xblk + [pl.BlockSpec(w.shape, lambda n: (0, 0, 0)), blk, vec, vec],
        out_specs=[blk, blk, blk],
        out_shape=[jax.ShapeDtypeStruct((R, D), F32), jax.ShapeDtypeStruct((R, D), F32),
                   jax.ShapeDtypeStruct((R, D), BF16)],
        name=name, compiler_params=_cp(("parallel",)))(*xs, w, h, g_post, g_next)


def _rms_post_bwd(dzs, w, h, g, res, y, g_post, name):
    nz = len(dzs)

    def body(*refs):
        dz_refs, (w_ref, h_ref, g_ref, r_ref, y_ref, gp_ref, dh_ref, dg_ref, dy_ref, dgp_ref) = refs[:nz], refs[nz:]
        n = pl.program_id(0)

        @pl.when(n == 0)
        def _():
            dg_ref[...] = jnp.zeros_like(dg_ref)
            dgp_ref[...] = jnp.zeros_like(dgp_ref)

        dv = _dot_nt(dz_refs[0][...], w_ref[0])
        for j in range(1, nz):
            dv = dv + _dot_nt(dz_refs[j][...], w_ref[j])
        hv = h_ref[...]
        r = lax.rsqrt(jnp.mean(hv * hv, axis=1, keepdims=True) + NORM_EPS)
        xh = hv * r
        dxh = dv * g_ref[...]
        dh = r * (dxh - xh * jnp.mean(dxh * xh, axis=1, keepdims=True)) + r_ref[...]
        row = lax.broadcasted_iota(jnp.int32, (TR, D), 0) + n * TR
        dh = jnp.where(row >= PAD, dh, 0.0)
        dh_ref[...] = dh
        dg_ref[...] += jnp.sum(dv * xh, axis=0, keepdims=True)
        yv = y_ref[...]
        ry = lax.rsqrt(jnp.mean(yv * yv, axis=1, keepdims=True) + NORM_EPS)
        yh = yv * ry
        dyh = dh * gp_ref[...]
        dy_ref[...] = (ry * (dyh - yh * jnp.mean(dyh * yh, axis=1, keepdims=True))).astype(BF16)
        dgp_ref[...] += jnp.sum(dh * yh, axis=0, keepdims=True)

    blk = pl.BlockSpec((TR, D), lambda n: (n, 0))
    vec = pl.BlockSpec((1, D), lambda n: (0, 0))
    return pl.pallas_call(
        body, grid=(R // TR,),
        in_specs=[blk] * nz + [pl.BlockSpec(w.shape, lambda n: (0, 0, 0)), blk, vec, blk, blk, vec],
        out_specs=[blk, vec, blk, vec],
        out_shape=[jax.ShapeDtypeStruct((R, D), F32), jax.ShapeDtypeStruct((1, D), F32),
                   jax.ShapeDtypeStruct((R, D), BF16), jax.ShapeDtypeStruct((1, D), F32)],
        name=name, compiler_params=_cp(("arbitrary",)))(*dzs, w, h, g, res, y, g_post)


def _cols_spec(rows, width, where):
    def index(*g):
        r, c = where(*g)
        return r * rows, (c if isinstance(c, int) else pl.multiple_of(c, 128))
    return pl.BlockSpec((pl.Element(rows), pl.Element(width)), index)


def _tail(x, w, h, g, target):
    steps = R // TR

    def body(x_ref, w_ref, h_ref, g_ref, t_ref, d_ref, dy_ref, dg_ref, l_ref):
        n = pl.program_id(0)

        @pl.when(n == 0)
        def _():
            dg_ref[...] = jnp.zeros_like(dg_ref)
            l_ref[...] = jnp.zeros_like(l_ref)

        yv = _dot(x_ref[...], w_ref[...])
        r = lax.rsqrt(jnp.mean(yv * yv, axis=1, keepdims=True) + NORM_EPS)
        yh = yv * r
        tv = t_ref[...]
        tv = jnp.where(n == 0, jnp.concatenate([jnp.zeros((TB, D), F32), tv[0:TR - TB]], axis=0), tv)
        row = lax.broadcasted_iota(jnp.int32, (TR, D), 0) + n * TR
        err = jnp.where(row >= TB, (h_ref[...] + yh * g_ref[...]) - tv, 0.0)
        dv = err * (1.0 / D)
        d_ref[...] = dv
        l_ref[...] += jnp.sum(err * err, axis=0, keepdims=True)
        dyh = dv * g_ref[...]
        dy_ref[...] = (r * (dyh - yh * jnp.mean(dyh * yh, axis=1, keepdims=True))).astype(BF16)
        dg_ref[...] += jnp.sum(dv * yh, axis=0, keepdims=True)

        @pl.when(n == steps - 1)
        def _():
            tot = jnp.sum(l_ref[...], axis=1, keepdims=True) * (0.5 / D)
            l_ref[...] = jnp.broadcast_to(tot, (1, D))

    blk = pl.BlockSpec((TR, D), lambda n: (n, 0))
    vec = pl.BlockSpec((1, D), lambda n: (0, 0))
    tgt = pl.BlockSpec((pl.Element(TR), pl.Element(D)),
                       lambda n: (pl.multiple_of(jnp.maximum(TR * n - TB, 0), 8), 0))
    return pl.pallas_call(
        body, grid=(steps,),
        in_specs=[pl.BlockSpec((TR, x.shape[1]), lambda n: (n, 0)), pl.BlockSpec(w.shape, lambda n: (0, 0)), blk, vec, tgt],
        out_specs=[blk, blk, vec, vec],
        out_shape=[jax.ShapeDtypeStruct((R, D), F32), jax.ShapeDtypeStruct((R, D), BF16),
                   jax.ShapeDtypeStruct((1, D), F32), jax.ShapeDtypeStruct((1, D), F32)],
        name="out_proj1_tail", compiler_params=_cp(("arbitrary",)))(x, w, h, g, target)


def _lane_row(shape):
    return lax.broadcasted_iota(jnp.int32, shape, 1), lax.broadcasted_iota(jnp.int32, shape, 0)


def _rot_half(x, lane):
    return jnp.where(lane % HD < HD // 2, pltpu.roll(x, 128 - HD // 2, 1), pltpu.roll(x, HD // 2, 1))


def _swa_blocks(n):
    return (0, jnp.maximum(n - 1, 0), n)


SWA_STACKS = ((0, 0), (0, 1), (1, 0), (1, 1))


def _swa_masks(n, lane, row):
    qpos = n * TB + (row & (TB - 1))
    kp = (n - 1) * TB + lane
    kc = n * TB + lane
    m0 = (lane >= PAD) & (qpos - lane >= TB)
    mp = (kp >= PAD) & (qpos >= kp) & (qpos - kp < TB)
    mc = (kc >= PAD) & (qpos >= kc)
    return (m0, mp, mc)


def _stack_pair(xa, xb, par):
    lane = lax.broadcasted_iota(jnp.int32, (TB, 128), 1)
    keep = (lane < HD) if par == 0 else (lane >= HD)
    return jnp.concatenate([jnp.where(keep, xa, 0.0), jnp.where(keep, xb, 0.0)], axis=0)


def _per_head(a, b):
    row = lax.broadcasted_iota(jnp.int32, (2 * TB, 1), 0)
    return jnp.where(row < TB, a, b)


def _swa_load(n, zq_ref, zkv_ref, cs_ref, sn_ref, lane):
    r0 = pl.multiple_of(n * TB, TB)
    csq, snq = cs_ref[pl.ds(r0, TB), :], sn_ref[pl.ds(r0, TB), :]
    qc = []
    for c in range(4):
        x = zq_ref[:, c * 128:(c + 1) * 128]
        qc.append((x * csq + _rot_half(x, lane) * snq) * SCALE)
    qst = [_stack_pair(qc[2 * g], qc[2 * g + 1], par).astype(BF16) for g, par in SWA_STACKS]
    kvs = []
    for b in _swa_blocks(n):
        b0 = pl.multiple_of(b * TB, TB)
        csb, snb = cs_ref[pl.ds(b0, TB), :], sn_ref[pl.ds(b0, TB), :]
        kx = zkv_ref[pl.ds(b0, TB), 0:128]
        kr = kx * csb + _rot_half(kx, lane) * snb
        vx = zkv_ref[pl.ds(b0, TB), 128:256]
        kvs.append((kr.astype(BF16), pltpu.roll(kr, HD, 1).astype(BF16),
                    vx.astype(BF16), pltpu.roll(vx, HD, 1).astype(BF16), csb, snb, b0))
    return qst, (csq, snq), kvs


def _swa_fwd(z0, cs, sn, sinks):
    def body(zq_ref, zkv_ref, cs_ref, sn_ref, sk_ref, ga_ref, o_ref, a_ref, lse_ref):
        n = pl.program_id(0)
        lane, row = _lane_row((TB, 128))
        lo = lane < HD
        masks = _swa_masks(n, *_lane_row((2 * TB, 128)))
        qst, _, kvs = _swa_load(n, zq_ref, zkv_ref, cs_ref, sn_ref, lane)
        ss = [[jnp.where(m, _dot_nt(qst[si], k if par == g else ka), NEG)
               for (k, ka, _, _, _, _, _), m in zip(kvs, masks)] for si, (g, par) in enumerate(SWA_STACKS)]
        o2, lse2 = [], []
        for si, (g, par) in enumerate(SWA_STACKS):
            sink = _per_head(sk_ref[0, 4 * g + par], sk_ref[0, 4 * g + 2 + par])
            s = ss[si]
            mx = jnp.maximum(jnp.maximum(jnp.max(s[0], axis=1, keepdims=True), jnp.max(s[1], axis=1, keepdims=True)),
                             jnp.max(s[2], axis=1, keepdims=True))
            mx = jnp.maximum(mx, sink)
            es = [jnp.exp(sb - mx) for sb in s]
            den = (jnp.sum(es[0], axis=1, keepdims=True) + jnp.sum(es[1], axis=1, keepdims=True)
                   + jnp.sum(es[2], axis=1, keepdims=True) + jnp.exp(sink - mx))
            inv = 1.0 / den
            t = jnp.zeros((2 * TB, 128), F32)
            for (_, _, v, va, _, _, _), e in zip(kvs, es):
                t = t + _dot((e * inv).astype(BF16), v if par == g else va)
            o2.append(t)
            lse2.append(mx + jnp.log(den))
        lse_t = jnp.zeros((TB, 128), F32)
        for g in range(2):
            for t in range(2):
                rows = slice(t * TB, (t + 1) * TB)
                c = 2 * g + t
                oc = jnp.where(lo, o2[2 * g][rows], o2[2 * g + 1][rows])
                o_ref[:, c * 128:(c + 1) * 128] = oc
                gv = ga_ref[:, c * 128:(c + 1) * 128]
                a_ref[:, c * 128:(c + 1) * 128] = (oc * (gv * _sig(gv))).astype(BF16)
                for par in range(2):
                    lse_t = jnp.where(lane == 4 * g + 2 * t + par, lse2[2 * g + par][rows], lse_t)
        lse_ref[...] = lse_t

    full = pl.BlockSpec((R, 128), lambda n: (0, 0))
    return dict(
        body=body,
        in_specs=[pl.BlockSpec((TB, 512), lambda n: (n, C_Q // 512)),
                  pl.BlockSpec((R, 256), lambda n: (0, C_K // 256)), full, full,
                  pl.BlockSpec(memory_space=pltpu.SMEM), _cols_spec(TB, 512, lambda n: (n, C_GA))],
        args=[z0, z0, cs, sn, sinks, z0],
        out_specs=[pl.BlockSpec((TB, 512), lambda n: (n, 0)), pl.BlockSpec((TB, 512), lambda n: (n, 0)),
                   pl.BlockSpec((TB, 128), lambda n: (n, 0))],
        out_shape=[jax.ShapeDtypeStruct((R, 512), F32), jax.ShapeDtypeStruct((R, 512), BF16),
                   jax.ShapeDtypeStruct((R, 128), F32)],
        scratch=[])


def _swa_bwd(z0, cs, sn, sinks, o, dmix, lse):
    def body(zq_ref, zkv_ref, cs_ref, sn_ref, sk_ref, ga_ref, o_ref, dm_ref, lse_ref,
             dq_ref, dga_ref, dkv_ref, dsk_ref, do_ref, acc_ref):
        n = pl.program_id(0)

        @pl.when(n == 0)
        def _():
            acc_ref[...] = jnp.zeros_like(acc_ref)
            dsk_ref[...] = jnp.zeros_like(dsk_ref)

        gv, dmv = ga_ref[...], dm_ref[...]
        sg = _sig(gv)
        dga_ref[...] = (dmv * o_ref[...] * (sg * (1.0 + gv * (1.0 - sg)))).astype(BF16)
        do_ref[...] = dmv * (gv * sg)
        lane, row = _lane_row((TB, 128))
        lo = lane < HD
        masks = _swa_masks(n, *_lane_row((2 * TB, 128)))
        qst, (csq, snq), kvs = _swa_load(n, zq_ref, zkv_ref, cs_ref, sn_ref, lane)
        lse_t = lse_ref[...]
        ss = [[jnp.where(m, _dot_nt(qst[si], k if par == g else ka), NEG)
               for (k, ka, _, _, _, _, _), m in zip(kvs, masks)] for si, (g, par) in enumerate(SWA_STACKS)]
        dobs, deltas, lses, dps = [], [], [], []
        for g, par in SWA_STACKS:
            ca, cb = slice(2 * g * 128, (2 * g + 1) * 128), slice((2 * g + 1) * 128, (2 * g + 2) * 128)
            dom = _stack_pair(do_ref[:, ca], do_ref[:, cb], par)
            deltas.append(jnp.sum(dom * jnp.concatenate([o_ref[:, ca], o_ref[:, cb]], axis=0), axis=1, keepdims=True))
            dob = dom.astype(BF16)
            dobs.append(dob)
            lses.append(jnp.concatenate(
                [jnp.sum(jnp.where(lane == 4 * g + 2 * t + par, lse_t, 0.0), axis=1, keepdims=True) for t in range(2)],
                axis=0))
            dps.append([_dot_nt(dob, v if par == g else va) for (_, _, v, va, _, _, _) in kvs])
        dk_al = [jnp.zeros((TB, 128), F32) for _ in range(3)]
        dk_mis = [jnp.zeros((TB, 128), F32) for _ in range(3)]
        dv_al = [jnp.zeros((TB, 128), F32) for _ in range(3)]
        dv_mis = [jnp.zeros((TB, 128), F32) for _ in range(3)]
        dsk_t = jnp.zeros((TB, 128), F32)
        dq2 = []
        for si, (g, par) in enumerate(SWA_STACKS):
            dqt = jnp.zeros((2 * TB, 128), F32)
            for bi, (k, ka, _, _, _, _, _) in enumerate(kvs):
                p = jnp.exp(ss[si][bi] - lses[si])
                ds = (p * (dps[si][bi] - deltas[si])).astype(BF16)
                dqt = dqt + _dot(ds, k if par == g else ka)
                dkh = _dot_tn(ds, qst[si])
                dvh = _dot_tn(p.astype(BF16), dobs[si])
                if par == g:
                    dk_al[bi] = dk_al[bi] + dkh
                    dv_al[bi] = dv_al[bi] + dvh
                else:
                    dk_mis[bi] = dk_mis[bi] + dkh
                    dv_mis[bi] = dv_mis[bi] + dvh
            dq2.append(dqt)
            sink = _per_head(sk_ref[0, 4 * g + par], sk_ref[0, 4 * g + 2 + par])
            dsk = -jnp.exp(sink - lses[si]) * deltas[si]
            for t in range(2):
                dsk_t = jnp.where(lane == 4 * g + 2 * t + par, dsk[t * TB:(t + 1) * TB], dsk_t)
        for g in range(2):
            for t in range(2):
                rows = slice(t * TB, (t + 1) * TB)
                c = 2 * g + t
                dqc = jnp.where(lo, dq2[2 * g][rows], dq2[2 * g + 1][rows]) * SCALE
                dq_ref[:, c * 128:(c + 1) * 128] = (dqc * csq + _rot_half(dqc * snq, lane)).astype(BF16)
        for bi, (_, _, _, _, csb, snb, b0) in enumerate(kvs):
            dk = dk_al[bi] + pltpu.roll(dk_mis[bi], HD, 1)
            dv = dv_al[bi] + pltpu.roll(dv_mis[bi], HD, 1)
            acc_ref[pl.ds(b0, TB), 0:128] += dk * csb + _rot_half(dk * snb, lane)
            acc_ref[pl.ds(b0, TB), 128:256] += dv
        dsk_ref[0:1, :] += jnp.sum(dsk_t, axis=0, keepdims=True)

        @pl.when(n == NB - 1)
        def _():
            dkv_ref[...] = acc_ref[...].astype(BF16)

    full = pl.BlockSpec((R, 128), lambda n: (0, 0))
    b512 = pl.BlockSpec((TB, 512), lambda n: (n, 0))
    return dict(
        body=body,
        in_specs=[pl.BlockSpec((TB, 512), lambda n: (n, C_Q // 512)),
                  pl.BlockSpec((R, 256), lambda n: (0, C_K // 256)), full, full,
                  pl.BlockSpec(memory_space=pltpu.SMEM), _cols_spec(TB, 512, lambda n: (n, C_GA)),
                  b512, b512, pl.BlockSpec((TB, 128), lambda n: (n, 0))],
        args=[z0, z0, cs, sn, sinks, z0, o, dmix, lse],
        out_specs=[b512, b512, pl.BlockSpec((R, 256), lambda n: (0, 0)), pl.BlockSpec((8, 128), lambda n: (0, 0))],
        out_shape=[jax.ShapeDtypeStruct((R, 512), BF16), jax.ShapeDtypeStruct((R, 512), BF16),
                   jax.ShapeDtypeStruct((R, 256), BF16), jax.ShapeDtypeStruct((8, 128), F32)],
        scratch=[pltpu.VMEM((TB, 512), F32), pltpu.VMEM((R, 256), F32)])


CC = 512
HALO = CONV_W - 1


def _conv_fwd(z0, conv_w, conv_b, ln_g, ln_b):
    def body(g_ref, w_ref, cb_ref, lg_ref, lb_ref, cv_ref, s_ref, ubuf):
        n = pl.program_id(0)

        @pl.when(n == 0)
        def _():
            ubuf[...] = jnp.zeros_like(ubuf)

        u = g_ref[:, 0:CC] * _sig(g_ref[:, CC:2 * CC])
        for k in range(8):
            ubuf[k, 0:TB + 8, :] = ubuf[k, TB:2 * TB + 8, :]
            ubuf[k, pl.ds(TB + 8 - k, TB), :] = u
        acc = jnp.zeros((TB, CC), F32)
        for w in range(CONV_W):
            off = TB - HALO + w
            acc = acc + ubuf[off % 8, pl.ds(off + 8 - off % 8, TB), :] * w_ref[w:w + 1, :]
        cv = acc + cb_ref[...]
        cv_ref[...] = cv
        xc = cv - jnp.mean(cv, axis=1, keepdims=True)
        rs = lax.rsqrt(jnp.mean(xc * xc, axis=1, keepdims=True) + LN_EPS)
        ln = xc * rs * lg_ref[...] + lb_ref[...]
        s_ref[...] = (ln * _sig(ln)).astype(BF16)

    vec = pl.BlockSpec((1, CC), lambda n: (0, 0))
    blk = pl.BlockSpec((TB, CC), lambda n: (n, 0))
    return dict(
        body=body,
        in_specs=[_cols_spec(TB, 2 * CC, lambda n: (n, C_GLU)),
                  pl.BlockSpec((32, CC), lambda n: (0, 0)), vec, vec, vec],
        args=[z0, conv_w, conv_b, ln_g, ln_b],
        out_specs=[blk, blk],
        out_shape=[jax.ShapeDtypeStruct((R, CC), F32), jax.ShapeDtypeStruct((R, CC), BF16)],
        scratch=[pltpu.VMEM((8, 2 * TB + 8, CC), F32)])


def _conv_bwd(ds, cv, z0, conv_w, ln_g, ln_b):
    def body(ds_ref, cv_ref, g_ref, w_ref, lg_ref, lb_ref, dglu_ref, dw_ref, dsm_ref, dbuf):
        n = pl.program_id(0)

        @pl.when(n == 0)
        def _():
            dbuf[...] = jnp.zeros_like(dbuf)
            dw_ref[...] = jnp.zeros_like(dw_ref)
            dsm_ref[...] = jnp.zeros_like(dsm_ref)

        cv = cv_ref[...]
        xc = cv - jnp.mean(cv, axis=1, keepdims=True)
        rs = lax.rsqrt(jnp.mean(xc * xc, axis=1, keepdims=True) + LN_EPS)
        xh = xc * rs
        ln = xh * lg_ref[...] + lb_ref[...]
        sg = _sig(ln)
        dln = ds_ref[...] * (sg * (1.0 + ln * (1.0 - sg)))
        dxh = dln * lg_ref[...]
        dcv = rs * (dxh - jnp.mean(dxh, axis=1, keepdims=True) - xh * jnp.mean(dxh * xh, axis=1, keepdims=True))
        dsm_ref[0:1, :] += jnp.sum(dcv, axis=0, keepdims=True)
        dsm_ref[1:2, :] += jnp.sum(dln * xh, axis=0, keepdims=True)
        dsm_ref[2:3, :] += jnp.sum(dln, axis=0, keepdims=True)
        for k in range(8):
            dbuf[k, TB:2 * TB + 8, :] = dbuf[k, 0:TB + 8, :]
            dbuf[k, pl.ds(8 - k, TB), :] = dcv
        a = g_ref[:, 0:CC]
        sb = _sig(g_ref[:, CC:2 * CC])
        u = a * sb
        du = jnp.zeros((TB, CC), F32)
        for w in range(CONV_W):
            off = HALO - w
            sh = dbuf[off % 8, pl.ds(off + 8 - off % 8, TB), :]
            du = du + sh * w_ref[w:w + 1, :]
            dw_ref[w:w + 1, :] += jnp.sum(u * sh, axis=0, keepdims=True)
        dglu_ref[:, 0:CC] = (du * sb).astype(BF16)
        dglu_ref[:, CC:2 * CC] = (du * a * sb * (1.0 - sb)).astype(BF16)

    rev = lambda n: (NB - 1 - n, 0)
    vec = pl.BlockSpec((1, CC), lambda n: (0, 0))
    blk = pl.BlockSpec((TB, CC), rev)
    return dict(
        body=body,
        in_specs=[blk, blk, _cols_spec(TB, 2 * CC, lambda n: (NB - 1 - n, C_GLU)),
                  pl.BlockSpec((32, CC), lambda n: (0, 0)), vec, vec],
        args=[ds, cv, z0, conv_w, ln_g, ln_b],
        out_specs=[pl.BlockSpec((TB, 2 * CC), rev), pl.BlockSpec((32, CC), lambda n: (0, 0)),
                   pl.BlockSpec((8, CC), lambda n: (0, 0))],
        out_shape=[jax.ShapeDtypeStruct((R, 2 * CC), BF16), jax.ShapeDtypeStruct((32, CC), F32),
                   jax.ShapeDtypeStruct((8, CC), F32)],
        scratch=[pltpu.VMEM((8, 2 * TB + 8, CC), F32)])


def _split_dot(x, t):
    hi = x.astype(BF16)
    lo = (x - hi.astype(F32)).astype(BF16)
    return _dot(hi, t) + _dot(lo, t)


def _stack_heads(x):
    lane = lax.broadcasted_iota(jnp.int32, (TB, 128), 1)
    return jnp.concatenate([jnp.where(lane < HD, x, 0.0), jnp.where(lane < HD, 0.0, x)], axis=0).astype(BF16)


def _sb_stack(qv, i):
    lane2, row2 = _lane_row((2 * TB, 128))
    qpos2 = i * TB + (row2 & (TB - 1))
    lane, row = _lane_row((TB, 128))
    return _stack_heads(qv), lane2, qpos2, (row > lane).astype(BF16)


SB_U = 3
SB_DEAD = -104.0
SB_P = 4


def _sb_fwd(q, k, v, g):
    def body(q_ref, k_ref, v_ref, g_ref, o_ref, m_ref, c_ref, n_ref):
        p, i = pl.program_id(0), pl.program_id(1)
        lane, row = _lane_row((TB, 128))
        lo = lane < HD
        slabs = [slice(s * 128, (s + 1) * 128) for s in range(SB_P)]
        q2s = []
        for sl in slabs:
            q2, lane2, qpos2, tri_gt = _sb_stack(q_ref[:, sl].astype(F32) * SCALE, i)
            q2s.append(q2)

        def cond(st):
            t, _, c2s = st
            alive = jnp.max(c2s[0])
            for c2 in c2s[1:]:
                alive = jnp.maximum(alive, jnp.max(c2))
            return jnp.logical_and(i - SB_U * t >= 0, alive > SB_DEAD)

        def step(st):
            t, accs, c2s = st
            accs, c2s = list(accs), list(c2s)
            jrs = [i - SB_U * t - u for u in range(SB_U)]
            j0s = [pl.multiple_of(jnp.maximum(jr, 0) * TB, TB) for jr in jrs]
            valids = []
            for jr in jrs:
                kpos = jr * TB + lane2
                valids.append((kpos >= PAD) & (kpos < qpos2))
            zs = [[jnp.where(valid, _dot_nt(q2s[s], k_ref[pl.ds(j0, TB), slabs[s]]), NEG)
                   for j0, valid in zip(j0s, valids)] for s in range(SB_P)]
            lbs, l1s = [], []
            for s in range(SB_P):
                lbs.append([jnp.minimum(z, 0.0) - jnp.log(1.0 + jnp.exp(-jnp.abs(z))) for z in zs[s]])
                l1s.append([lb - z for lb, z in zip(lbs[s], zs[s])])
            sfxs = [[_split_dot(l1, tri_gt) for l1 in l1s[s]] for s in range(SB_P)]
            carries = []
            for s in range(SB_P):
                cs, c2 = [], c2s[s]
                for jr, l1 in zip(jrs, l1s[s]):
                    cs.append(c2)
                    c_ref[:, slabs[s]] = jnp.where(lane == 2 * jr, c2[0:TB],
                                                   jnp.where(lane == 2 * jr + 1, c2[TB:2 * TB], c_ref[:, slabs[s]]))
                    c2 = c2 + jnp.sum(l1, axis=1, keepdims=True)
                carries.append(cs)
                c2s[s] = c2
            for s in range(SB_P):
                for j0, valid, lb, sfx, cu in zip(j0s, valids, lbs[s], sfxs[s], carries[s]):
                    a = jnp.exp(lb + sfx + cu).astype(BF16)
                    av = _dot(a, v_ref[pl.ds(j0, TB), slabs[s]])
                    accs[s] = accs[s] + jnp.where(lo, av[0:TB], av[TB:2 * TB])
            return t + 1, tuple(accs), tuple(c2s)

        c_ref[...] = jnp.zeros_like(c_ref)
        init = (jnp.int32(0), tuple(jnp.zeros((TB, 128), F32) for _ in slabs),
                tuple(jnp.zeros((2 * TB, 1), F32) for _ in slabs))
        t, accs, _ = lax.while_loop(cond, step, init)
        for sl, acc in zip(slabs, accs):
            o_ref[:, sl] = acc
            gv = g_ref[:, sl]
            m_ref[:, sl] = (acc * (gv * _sig(gv))).astype(BF16)
        n_ref[p, i] = t

    wide = SB_P * 128
    slab = pl.BlockSpec((R, wide), lambda p, i: (0, p))
    blk = pl.BlockSpec((TB, wide), lambda p, i: (i, p))
    sd = jax.ShapeDtypeStruct((R, D), F32)
    return pl.pallas_call(
        body, grid=(D // wide, NB), in_specs=[blk, slab, slab, blk],
        out_specs=[blk, blk, blk, pl.BlockSpec(memory_space=pltpu.SMEM)],
        out_shape=[sd, jax.ShapeDtypeStruct((R, D), BF16), sd, jax.ShapeDtypeStruct((D // wide, NB), jnp.int32)],
        name="sb_fwd", compiler_params=_cp(("arbitrary", "arbitrary")))(q, k, v, g)


def _sb_bwd(trips, q, k, v, car, dm, g, o):
    def body(n_ref, q_ref, k_ref, v_ref, c_ref, dm_ref, g_ref, o_ref, dq_ref, dko_ref, dvo_ref, dg_ref,
             dk_ref, dv_ref):
        p, i = pl.program_id(0), pl.program_id(1)

        @pl.when(i == 0)
        def _():
            dk_ref[...] = jnp.zeros_like(dk_ref)
            dv_ref[...] = jnp.zeros_like(dv_ref)

        lane, row = _lane_row((TB, 128))
        lo = lane < HD
        tri_lt = (row < lane).astype(BF16)
        slabs = [slice(s * 128, (s + 1) * 128) for s in range(SB_P)]
        q2s, do2s, cts = [], [], []
        for sl in slabs:
            q2, lane2, qpos2, tri_gt = _sb_stack(q_ref[:, sl].astype(F32) * SCALE, i)
            q2s.append(q2)
            gv, dmv = g_ref[:, sl], dm_ref[:, sl]
            sg = _sig(gv)
            dg_ref[:, sl] = (dmv * o_ref[:, sl] * (sg * (1.0 + gv * (1.0 - sg)))).astype(BF16)
            do2s.append(_stack_heads(dmv * (gv * sg)))
            cts.append(c_ref[:, sl])
        trips_i = n_ref[p, i]
        first = jnp.maximum(i + 1 - SB_U * trips_i, 0)

        def step(t, carry):
            dqs, g2s = carry
            dqs, g2s = list(dqs), list(g2s)
            jrs = [first + SB_U * t + u for u in range(SB_U)]
            j0s = [pl.multiple_of(jnp.minimum(jr, i) * TB, TB) for jr in jrs]
            valids = []
            for jr in jrs:
                kpos = jr * TB + lane2
                valids.append((kpos >= PAD) & (kpos < qpos2))
            ks = [[k_ref[pl.ds(j0, TB), sl] for j0 in j0s] for sl in slabs]
            zs = [[jnp.where(valid, _dot_nt(q2s[s], kj), NEG) for kj, valid in zip(ks[s], valids)] for s in range(SB_P)]
            das = [[_dot_nt(do2s[s], v_ref[pl.ds(j0, TB), slabs[s]]) for j0 in j0s] for s in range(SB_P)]
            es = [[jnp.exp(-jnp.abs(z)) for z in zs[s]] for s in range(SB_P)]
            lbs = [[jnp.minimum(z, 0.0) - jnp.log(1.0 + e) for z, e in zip(zs[s], es[s])] for s in range(SB_P)]
            l1s = [[lb - z for lb, z in zip(lbs[s], zs[s])] for s in range(SB_P)]
            sfxs = [[_split_dot(l1, tri_gt) for l1 in l1s[s]] for s in range(SB_P)]
            a_s, gmats, gpres = [], [], []
            for s in range(SB_P):
                a_l, gm_l, gp_l, g2 = [], [], [], g2s[s]
                for jr, valid, lb, sfx, da in zip(jrs, valids, lbs[s], sfxs[s], das[s]):
                    later = jnp.concatenate(
                        [jnp.sum(jnp.where(lane == 2 * jr + hh, cts[s], 0.0), axis=1, keepdims=True) for hh in range(2)],
                        axis=0)
                    a = jnp.exp(lb + sfx + later)
                    gmat = da * a
                    a_l.append(a.astype(BF16))
                    gm_l.append(gmat)
                    gp_l.append(g2)
                    g2 = g2 + jnp.sum(gmat, axis=1, keepdims=True)
                a_s.append(a_l)
                gmats.append(gm_l)
                gpres.append(gp_l)
                g2s[s] = g2
            pres = [[gp + _split_dot(gmat, tri_lt) for gp, gmat in zip(gpres[s], gmats[s])] for s in range(SB_P)]
            for s in range(SB_P):
                for j0, kj, valid, z, e, gmat, pre, a in zip(j0s, ks[s], valids, zs[s], es[s], gmats[s], pres[s], a_s[s]):
                    r = 1.0 / (1.0 + e)
                    big = z >= 0.0
                    beta = jnp.where(big, r, e * r)
                    omb = jnp.where(big, e * r, r)
                    dz = (gmat * omb - beta * pre).astype(BF16)
                    dq2 = _dot(dz, kj)
                    dqs[s] = dqs[s] + jnp.where(lo, dq2[0:TB], dq2[TB:2 * TB])
                    dk_ref[pl.ds(j0, TB), slabs[s]] += _dot_tn(dz, q2s[s])
                    dv_ref[pl.ds(j0, TB), slabs[s]] += _dot_tn(a, do2s[s])
            return tuple(dqs), tuple(g2s)

        init = (tuple(jnp.zeros((TB, 128), F32) for _ in slabs), tuple(jnp.zeros((2 * TB, 1), F32) for _ in slabs))
        dqs, _ = lax.fori_loop(0, trips_i, step, init)
        for sl, dq in zip(slabs, dqs):
            dq_ref[:, sl] = (dq * SCALE).astype(BF16)

        @pl.when(i == NB - 1)
        def _():
            dko_ref[...] = dk_ref[...].astype(BF16)
            dvo_ref[...] = dv_ref[...].astype(BF16)

    wide = SB_P * 128
    slab = pl.BlockSpec((R, wide), lambda p, i: (0, p))
    blk = pl.BlockSpec((TB, wide), lambda p, i: (i, p))
    sd = jax.ShapeDtypeStruct((R, D), BF16)
    return pl.pallas_call(
        body, grid=(D // wide, NB),
        in_specs=[pl.BlockSpec(memory_space=pltpu.SMEM), blk, slab, slab, blk, blk, blk, blk],
        out_specs=[blk, slab, slab, blk], out_shape=[sd, sd, sd, sd],
        scratch_shapes=[pltpu.VMEM((R, wide), F32), pltpu.VMEM((R, wide), F32)], name="sb_bwd",
        compiler_params=_cp(("arbitrary", "arbitrary")))(trips, q, k, v, car, dm, g, o)


def _adamw(w, parts, m, v, name):
    rows, cols = w.shape
    tr = next((t for t in (256, 176) if rows % t == 0), rows)
    nparts = len(parts)

    def body(*refs):
        w_ref = refs[0]
        p_refs = refs[1:1 + nparts]
        m_ref, v_ref, g_ref, d_ref, nm_ref, nv_ref = refs[1 + nparts:]
        g = p_refs[0][...]
        for p_ref in p_refs[1:]:
            g = g + p_ref[...]
        nm = ADAM_B1 * m_ref[...] + (1.0 - ADAM_B1) * g
        nv = ADAM_B2 * v_ref[...] + (1.0 - ADAM_B2) * (g * g)
        m_hat = nm / (1.0 - ADAM_B1 ** ADAM_STEP)
        v_hat = nv / (1.0 - ADAM_B2 ** ADAM_STEP)
        g_ref[...] = g
        d_ref[...] = -ADAM_LR * (m_hat / (jnp.sqrt(v_hat) + ADAM_EPS) + ADAM_WD * w_ref[...])
        nm_ref[...] = nm
        nv_ref[...] = nv

    blk = pl.BlockSpec((tr, cols), lambda i: (i, 0))
    sd = jax.ShapeDtypeStruct((rows, cols), F32)
    return pl.pallas_call(
        body, grid=(rows // tr,), in_specs=[blk] * (3 + nparts), out_specs=[blk] * 4, out_shape=[sd] * 4,
        name=name, compiler_params=_cp(("parallel",)))(w, *parts, m, v)


def _sum8(buf, name):
    _, rows, cols = buf.shape

    def body(b_ref, o_ref):
        acc = b_ref[0]
        for i in range(1, 8):
            acc = acc + b_ref[i]
        o_ref[...] = acc

    return pl.pallas_call(
        body, out_shape=jax.ShapeDtypeStruct((rows, cols), F32), name=name,
        compiler_params=pltpu.CompilerParams(vmem_limit_bytes=VMEM_LIMIT))(buf)


MESH = pl.DeviceIdType.MESH
ANY = pl.BlockSpec(memory_space=pl.ANY)


def _chip_peers():
    x, y = lax.axis_index("x"), lax.axis_index("y")
    return [(1 - x, y), (x, 1 - y), (1 - x, 1 - y)]


def _gather_chips(shards):
    plan = _gather_plan(shards)

    def body(*refs):
        n = len(shards)
        ins, outs, sems = refs[:n], refs[n:2 * n], refs[2 * n:]
        plan["start"](ins, outs, sems)
        plan["mid"](ins, outs, sems)
        plan["finish"](ins, outs, sems)

    n = len(shards)
    res = pl.pallas_call(
        body, in_specs=[ANY] * n, out_specs=[ANY] * n, out_shape=plan["out_shape"],
        scratch_shapes=plan["sems"], name="gather_chips")(*plan["args"])
    return plan["post"](res)


def _gather_plan(shards):
    n = len(shards)
    shards = [s.reshape((2, s.shape[0] // 2) + s.shape[1:]) for s in shards]

    def copies(kind, ins, outs, sems):
        s1, r1, s2, r2 = sems
        x, y, c = lax.axis_index("x"), lax.axis_index("y"), lax.axis_index("c")
        me = 2 * x + y
        out = []
        for j, (px, py) in enumerate(_chip_peers()):
            for a in range(n):
                k = j * n + a
                got = outs[a].at[2 * px + py].at[c]
                other = outs[a].at[2 * px + py].at[1 - c]
                src, dst, ss, rs, dev = {
                    "first": (ins[a].at[c], outs[a].at[me].at[c], s1, r1, (px, py, c)),
                    "landed": (got, got, s1, r1, (px, py, c)),
                    "passed": (got, got, s2, r2, (x, y, 1 - c)),
                    "theirs": (other, other, s2, r2, (x, y, 1 - c)),
                }[kind]
                out.append(pltpu.make_async_remote_copy(
                    src_ref=src, dst_ref=dst, send_sem=ss.at[k], recv_sem=rs.at[k], device_id=dev, device_id_type=MESH))
        return out

    def start(ins, outs, sems):
        for cp in copies("first", ins, outs, sems):
            cp.start()

    def mid(ins, outs, sems):
        for got, fwd in zip(copies("landed", ins, outs, sems), copies("passed", ins, outs, sems)):
            got.wait_recv()
            fwd.start()

    def finish(ins, outs, sems):
        for cp in copies("theirs", ins, outs, sems):
            cp.wait_recv()
        for cp in copies("first", ins, outs, sems) + copies("passed", ins, outs, sems):
            cp.wait_send()

    def post(res):
        me = 2 * lax.axis_index("x") + lax.axis_index("y")
        res = [lax.dynamic_update_index_in_dim(r, s, me, 0) for r, s in zip(res, shards)]
        return [r.reshape((N_CHIPS, 2 * r.shape[2]) + r.shape[3:]) for r in res]

    return dict(args=shards, out_shape=[jax.ShapeDtypeStruct((N_CHIPS,) + s.shape, s.dtype) for s in shards],
                sems=[pltpu.SemaphoreType.DMA((3 * n,))] * 4, start=start, mid=mid, finish=finish, post=post)


def _rows_call(name, parts, plan):
    n_in = [len(p["args"]) for p in parts]
    n_out = [len(p["out_shape"]) for p in parts]
    n_scr = [len(p["scratch"]) for p in parts]
    c_in, c_out = len(plan["args"]), len(plan["out_shape"])

    def split(refs, sizes):
        out, pos = [], 0
        for k in sizes:
            out.append(refs[pos:pos + k])
            pos += k
        return out

    def body(*refs):
        ins, outs, scr = split(refs, [sum(n_in) + c_in, sum(n_out) + c_out, sum(n_scr) + len(plan["sems"])])
        p_in, p_out, p_scr = split(ins, n_in + [c_in]), split(outs, n_out + [c_out]), split(scr, n_scr + [len(plan["sems"])])
        comm = (p_in[-1], p_out[-1], p_scr[-1])
        step = pl.program_id(0)

        @pl.when(step == 0)
        def _():
            plan["start"](*comm)

        for p, i, o, s in zip(parts, p_in, p_out, p_scr):
            p["body"](*i, *o, *s)

        @pl.when(step == NB - 2)
        def _():
            plan["mid"](*comm)

        @pl.when(step == NB - 1)
        def _():
            plan["finish"](*comm)

    flat = lambda key: [v for p in parts for v in p[key]]
    res = pl.pallas_call(
        body, grid=(NB,), in_specs=flat("in_specs") + [ANY] * c_in, out_specs=flat("out_specs") + [ANY] * c_out,
        out_shape=flat("out_shape") + plan["out_shape"], scratch_shapes=flat("scratch") + plan["sems"],
        name=name, compiler_params=_cp(("arbitrary",)))(*flat("args"), *plan["args"])
    outs = split(res, n_out + [c_out])
    return outs[:-1], outs[-1]


def _pair_exchange(grads, name):
    n = len(grads)
    hs = [g.shape[1] // 2 for g in grads]
    grads = [g.reshape((N_CHIPS, 2, h) + g.shape[2:]) for g, h in zip(grads, hs)]

    def body(*refs):
        ins, got = refs[:n], refs[n:2 * n]
        ssem, rsem = refs[2 * n:]
        x, y, c = lax.axis_index("x"), lax.axis_index("y"), lax.axis_index("c")
        sends = [pltpu.make_async_remote_copy(
            src_ref=ins[a].at[:, 1 - c], dst_ref=got[a], send_sem=ssem.at[a],
            recv_sem=rsem.at[a], device_id=(x, y, 1 - c), device_id_type=MESH) for a in range(n)]
        for cp in sends:
            cp.start()
        for cp in sends:
            cp.wait()

    half_shapes = [jax.ShapeDtypeStruct((N_CHIPS, h) + g.shape[3:], g.dtype) for g, h in zip(grads, hs)]
    got = pl.pallas_call(
        body, in_specs=[ANY] * n, out_specs=[ANY] * n, out_shape=half_shapes,
        scratch_shapes=[pltpu.SemaphoreType.DMA((n,))] * 2, name=name)(*grads)
    return grads, got


def _sum_pair(both, got, send_dtype, name):
    _, _, rows, cols = both.shape
    tr = 256 if rows % 256 == 0 else rows

    def body(c_ref, a_ref, b_ref, f_ref, s_ref):
        t = a_ref[...].astype(F32) + b_ref[...].astype(F32)
        f_ref[...] = t
        s_ref[...] = t.astype(send_dtype)

    blk = pl.BlockSpec((N_CHIPS, tr, cols), lambda i, c: (0, i, 0))
    mine = pl.BlockSpec((N_CHIPS, None, tr, cols), lambda i, c: (0, c[0], i, 0))
    return pl.pallas_call(
        body, grid_spec=pltpu.PrefetchScalarGridSpec(
            num_scalar_prefetch=1, grid=(rows // tr,), in_specs=[mine, blk], out_specs=[blk, blk]),
        out_shape=[jax.ShapeDtypeStruct(got.shape, F32), jax.ShapeDtypeStruct(got.shape, send_dtype)],
        name=name, compiler_params=_cp(("parallel",)))(lax.axis_index("c").reshape(1), both, got)


def _scatter_plan(send):
    n = len(send)

    def copies(sin, land, sems):
        ssem, rsem = sems
        c = lax.axis_index("c")
        return [pltpu.make_async_remote_copy(
            src_ref=sin[a].at[2 * px + py], dst_ref=land[a].at[j], send_sem=ssem.at[j * n + a],
            recv_sem=rsem.at[j * n + a], device_id=(px, py, c), device_id_type=MESH)
            for j, (px, py) in enumerate(_chip_peers()) for a in range(n)]

    def start(sin, land, sems):
        for cp in copies(sin, land, sems):
            cp.start()

    def finish(sin, land, sems):
        for cp in copies(sin, land, sems):
            cp.wait()

    return dict(args=list(send), out_shape=[jax.ShapeDtypeStruct((3,) + s.shape[1:], s.dtype) for s in send],
                sems=[pltpu.SemaphoreType.DMA((3 * n,))] * 2, start=start, mid=lambda *a: None, finish=finish)


def _sum_shard(keep, land, name):
    _, rows, cols = keep.shape
    tr = 256 if rows % 256 == 0 else rows

    def body(me_ref, m_ref, l_ref, o_ref):
        o_ref[...] = ((m_ref[...] + l_ref[0].astype(F32)) + l_ref[1].astype(F32)) + l_ref[2].astype(F32)

    own = pl.BlockSpec((None, tr, cols), lambda i, me: (me[0], i, 0))
    me = (2 * lax.axis_index("x") + lax.axis_index("y")).reshape(1)
    return pl.pallas_call(
        body, grid_spec=pltpu.PrefetchScalarGridSpec(
            num_scalar_prefetch=1, grid=(rows // tr,),
            in_specs=[own, pl.BlockSpec((3, tr, cols), lambda i, me: (0, i, 0))],
            out_specs=pl.BlockSpec((tr, cols), lambda i, me: (i, 0))),
        out_shape=jax.ShapeDtypeStruct((rows, cols), F32),
        name=name, compiler_params=_cp(("parallel",)))(me, keep, land)


def _join_cores(halves):
    n = len(halves)

    def body(*refs):
        ins, outs = refs[:n], refs[n:2 * n]
        ssem, rsem = refs[2 * n:]
        x, y, c = lax.axis_index("x"), lax.axis_index("y"), lax.axis_index("c")
        sends = [pltpu.make_async_remote_copy(
            src_ref=ins[a], dst_ref=outs[a].at[c], send_sem=ssem.at[a], recv_sem=rsem.at[a],
            device_id=(x, y, 1 - c), device_id_type=MESH) for a in range(n)]
        for cp in sends:
            cp.start()
        for a in range(n):
            sends[a].wait_send()
            pltpu.make_async_remote_copy(
                src_ref=ins[a], dst_ref=outs[a].at[1 - c], send_sem=ssem.at[a], recv_sem=rsem.at[a],
                device_id=(x, y, 1 - c), device_id_type=MESH).wait_recv()

    res = pl.pallas_call(
        body, in_specs=[ANY] * n, out_specs=[ANY] * n,
        out_shape=[jax.ShapeDtypeStruct((2,) + h.shape, h.dtype) for h in halves],
        scratch_shapes=[pltpu.SemaphoreType.DMA((n,))] * 2, name="join_cores")(*halves)
    c = lax.axis_index("c")
    res = [lax.dynamic_update_index_in_dim(r, h, c, 0) for r, h in zip(res, halves)]
    return [r.reshape((2 * r.shape[1],) + r.shape[2:]) for r in res]


def _gather_all(vec):
    def copies(kind, ins, outs, sems):
        ssem, rsem = sems
        x, y, c = lax.axis_index("x"), lax.axis_index("y"), lax.axis_index("c")
        me = 4 * x + 2 * y + c
        out = []
        for k in range(1, 8):
            px, py, pc = x ^ (k >> 2), y ^ ((k >> 1) & 1), c ^ (k & 1)
            dst = outs[0].at[me] if kind == "send" else outs[0].at[4 * px + 2 * py + pc]
            out.append(pltpu.make_async_remote_copy(
                src_ref=ins[0], dst_ref=dst, send_sem=ssem.at[k - 1], recv_sem=rsem.at[k - 1],
                device_id=(px, py, pc), device_id_type=MESH))
        return out

    def body(v_ref, o_ref, ssem, rsem):
        refs = ([v_ref], [o_ref], (ssem, rsem))
        for cp in copies("send", *refs):
            cp.start()
        for cp in copies("recv", *refs):
            cp.wait_recv()
        for cp in copies("send", *refs):
            cp.wait_send()

    res = pl.pallas_call(
        body, in_specs=[ANY], out_specs=ANY, out_shape=jax.ShapeDtypeStruct((8,) + vec.shape, vec.dtype),
        scratch_shapes=[pltpu.SemaphoreType.DMA((7,))] * 2, name="gather_all")(vec)
    me = 4 * lax.axis_index("x") + 2 * lax.axis_index("y") + lax.axis_index("c")
    return lax.dynamic_update_index_in_dim(res, vec, me, 0)


def _rope_tables():
    pos = (jnp.arange(R, dtype=jnp.int32) - PAD).astype(F32)
    half = HD // 2
    inv = ROPE_THETA ** (-jnp.arange(half, dtype=F32) / half)
    ang = pos[:, None] * inv[None, :]
    cos, sin = jnp.cos(ang), jnp.sin(ang)
    cs = jnp.tile(cos, (1, 4))
    sn = jnp.tile(jnp.concatenate([-sin, sin], axis=1), (1, 2))
    return cs, sn


def _local_step(x, target, p):
    w0t = p["ab_w_in"]
    conv_w = jnp.concatenate([p["ab_conv_w"], jnp.zeros((1, CC), F32)], axis=0)
    cs, sn = _rope_tables()

    plan = _gather_plan([p["sb_w_out"], p["ab_w_out"], p["ab_w_pw2"]])
    ((h0, xn0, z0),), gathered = _rows_call("pre0", [_embed_proj(x, p["meta_tokens"], p["ab_pre_norm"], w0t)], plan)
    wo1, wo0, wpw = plan["post"](gathered)
    wo1, wo0, wpw = wo1.reshape(D, D), wo0.reshape(D, D), wpw.reshape(CC, CC)
    plan = _gather_plan([p["sb_w_in"]])
    ((o0, a0, lse0), (cv0, s0)), gathered = _rows_call(
        "fwd0", [_swa_fwd(z0, cs, sn, p["ab_sinks"]),
                 _conv_fwd(z0, conv_w, p["ab_conv_b"], p["ab_conv_ln_g"], p["ab_conv_ln_b"])], plan)
    (w1,) = plan["post"](gathered)
    t0, c0 = _pw2_fwd(s0, wpw, z0)
    wo0h = wo0.reshape(2, CC, D)
    y0, h1, xn1 = _post_rms_fwd([a0, c0], wo0h, h0, p["ab_post_norm"], p["sb_pre_norm"], "out_proj0_norms")

    q1, k1, v1, g1 = _in_proj1(xn1, w1)
    o1, m1, car1, trips1 = _sb_fwd(q1, k1, v1, g1)

    dh2, dy1, d_sb_post, loss_row = _tail(m1, wo1, h1, p["sb_post_norm"], target)

    dm1 = _mm([(dy1, wo1)], F32, "out_proj1_dx", 544, 1024, tb=True)
    d_wo1 = _mm([(m1, dy1)], BF16, "out_proj1_dw", 512, 1024, ta=True)
    dq1, dk1, dv1, dg1 = _sb_bwd(trips1, q1, k1, v1, car1, dm1, g1, o1)
    dz1 = [dq1, dk1, dv1, dg1]
    d_w1 = jnp.stack([_mm([(xn1, dz1[j])], BF16, "in_proj1_dw%d" % j, 512, 1024, ta=True) for j in range(4)])

    dh1, d_sb_pre, dy0, d_ab_post = _rms_post_bwd(dz1, w1, h1, p["sb_pre_norm"], dh2, y0, p["ab_post_norm"],
                                                  "in_proj1_dx_norms")
    dmix0 = _mm([(dy0, wo0)], F32, "out_proj0_dx", 544, 1024, tb=True)
    d_wo0 = jnp.concatenate([_mm([(a0, dy0)], BF16, "out_proj0_dw_a", 512, 1024, ta=True),
                             _mm([(c0, dy0)], BF16, "out_proj0_dw_b", 512, 1024, ta=True)], axis=0)
    dgb0, ds0, d_wpw = _pw2_bwd(dmix0, t0, z0, wpw, s0)
    d_wpw = d_wpw.astype(BF16)
    early = ("sb_w_in", "sb_w_out", "ab_w_out", "ab_w_pw2")
    own1, got1 = _pair_exchange([d_w1, d_wo1.reshape(N_CHIPS, 256, D), d_wo0.reshape(N_CHIPS, 256, D),
                                 d_wpw.reshape(N_CHIPS, 128, CC)], "pair_exchange1")
    pair1 = [_sum_pair(o, t, BF16, "sum_pair_" + nm) for o, t, nm in zip(own1, got1, early)]
    plan = _scatter_plan([pr[1] for pr in pair1])
    ((dglu0, d_convw, d_small), (dq0, dga0, dkv0, d_sinks)), land1 = _rows_call(
        "bwd0", [_conv_bwd(ds0, cv0, z0, conv_w, p["ab_conv_ln_g"], p["ab_conv_ln_b"]),
                 _swa_bwd(z0, cs, sn, p["ab_sinks"], o0, dmix0, lse0)], plan)
    halves1 = [_sum_shard(pr[0], la, "sum_shard_" + nm) for pr, la, nm in zip(pair1, land1, early)]
    dz0 = jnp.concatenate([dq0, dkv0, dga0, dglu0, dgb0], axis=1)
    d_w0t = _mm([(dz0, xn0)], BF16, "in_proj0_dw", 1408, 512, ta=True)
    own0, got0 = _pair_exchange([d_w0t.reshape(N_CHIPS, 704, D)], "pair_exchange0")
    keep0, send0 = _sum_pair(own0[0], got0[0], BF16, "sum_pair_ab_w_in")
    plan = _scatter_plan([send0])
    dxn0, land0 = _mm([(dz0, w0t)], F32, "in_proj0_dx", 544, 1024, plan=plan)
    half0 = _sum_shard(keep0, land0[0], "sum_shard_ab_w_in")
    dh0_first, grad_x, d_ab_pre = _rms_bwd(dxn0, h0, p["ab_pre_norm"], dh1, F32, "rms_bwd0", split=True)

    grads = {
        "meta_tokens": dh0_first[PAD:TB], "ab_pre_norm": d_ab_pre, "ab_sinks": d_sinks[0:1, 0:8],
        "ab_conv_w": d_convw[0:CONV_W], "ab_conv_b": d_small[0:1], "ab_conv_ln_g": d_small[1:2],
        "ab_conv_ln_b": d_small[2:3], "ab_post_norm": d_ab_post, "sb_pre_norm": d_sb_pre, "sb_post_norm": d_sb_post,
    }
    h_sb_in, h_sb_out, h_ab_out, h_pw2 = halves1
    return loss_row, grad_x, grads, [half0, h_ab_out, h_pw2, h_sb_in, h_sb_out]


SMALL_ROWS = 80
REP_ROWS = 32

WEIGHTS = ["meta_tokens", "ab_pre_norm", "ab_w_in", "ab_sinks", "ab_conv_w", "ab_conv_b", "ab_conv_ln_g",
           "ab_conv_ln_b", "ab_w_pw2", "ab_w_out", "ab_post_norm", "sb_pre_norm", "sb_w_in", "sb_w_out",
           "sb_post_norm"]
BIG = ["ab_w_in", "ab_w_out", "ab_w_pw2", "sb_w_in", "sb_w_out"]


def _pack_small(conv_w, meta, sb_pre, sb_post):
    pad = lambda a, rows: jnp.pad(a, ((0, rows - a.shape[0]), (0, 0)))
    return jnp.concatenate([pad(conv_w, 32), meta.reshape(32, 128), pad(sb_pre.reshape(2, 128), 8),
                            pad(sb_post.reshape(2, 128), 8)], axis=0)


def _unpack_small(s):
    return s[0:31], s[32:64].reshape(16, 256), s[64:66].reshape(1, 256), s[72:74].reshape(1, 256)


REP_LOSS = 3592


def _pack_rep(pre, post, conv_b, ln_g, ln_b, sinks, extra=None):
    flat = jnp.concatenate([pre.reshape(-1), post.reshape(-1), conv_b.reshape(-1), ln_g.reshape(-1),
                            ln_b.reshape(-1), sinks.reshape(-1)] + ([] if extra is None else [extra.reshape(-1)]))
    flat = jnp.concatenate([flat, jnp.zeros((REP_ROWS * 128 - flat.shape[0],), F32)])
    return flat.reshape(REP_ROWS, 128)


def _unpack_rep(r):
    f = r.reshape(-1)
    return (f[0:1024].reshape(1, 1024), f[1024:2048].reshape(1, 1024), f[2048:2560].reshape(1, 512),
            f[2560:3072].reshape(1, 512), f[3072:3584].reshape(1, 512), f[3584:3592].reshape(1, 8))


def _chips_to_cols(w):
    return w.transpose(1, 0, 2).reshape(w.shape[1], -1)


def kernel(x, meta_tokens, ab_pre_norm, ab_w_in, ab_sinks, ab_conv_w, ab_conv_b, ab_conv_ln_g, ab_conv_ln_b, ab_w_pw2, ab_w_out, ab_post_norm, sb_pre_norm, sb_w_in, sb_w_out, sb_post_norm, loss_target, m_meta_tokens, m_ab_pre_norm, m_ab_w_in, m_ab_sinks, m_ab_conv_w, m_ab_conv_b, m_ab_conv_ln_g, m_ab_conv_ln_b, m_ab_w_pw2, m_ab_w_out, m_ab_post_norm, m_sb_pre_norm, m_sb_w_in, m_sb_w_out, m_sb_post_norm, v_meta_tokens, v_ab_pre_norm, v_ab_w_in, v_ab_sinks, v_ab_conv_w, v_ab_conv_b, v_ab_conv_ln_g, v_ab_conv_ln_b, v_ab_w_pw2, v_ab_w_out, v_ab_post_norm, v_sb_pre_norm, v_sb_w_in, v_sb_w_out, v_sb_post_norm):
    w = dict(meta_tokens=meta_tokens, ab_pre_norm=ab_pre_norm, ab_w_in=ab_w_in, ab_sinks=ab_sinks,
             ab_conv_w=ab_conv_w, ab_conv_b=ab_conv_b, ab_conv_ln_g=ab_conv_ln_g, ab_conv_ln_b=ab_conv_ln_b,
             ab_w_pw2=ab_w_pw2, ab_w_out=ab_w_out, ab_post_norm=ab_post_norm, sb_pre_norm=sb_pre_norm,
             sb_w_in=sb_w_in, sb_w_out=sb_w_out, sb_post_norm=sb_post_norm)
    m = dict(meta_tokens=m_meta_tokens, ab_pre_norm=m_ab_pre_norm, ab_w_in=m_ab_w_in, ab_sinks=m_ab_sinks,
             ab_conv_w=m_ab_conv_w, ab_conv_b=m_ab_conv_b, ab_conv_ln_g=m_ab_conv_ln_g,
             ab_conv_ln_b=m_ab_conv_ln_b, ab_w_pw2=m_ab_w_pw2, ab_w_out=m_ab_w_out, ab_post_norm=m_ab_post_norm,
             sb_pre_norm=m_sb_pre_norm, sb_w_in=m_sb_w_in, sb_w_out=m_sb_w_out, sb_post_norm=m_sb_post_norm)
    v = dict(meta_tokens=v_meta_tokens, ab_pre_norm=v_ab_pre_norm, ab_w_in=v_ab_w_in, ab_sinks=v_ab_sinks,
             ab_conv_w=v_ab_conv_w, ab_conv_b=v_ab_conv_b, ab_conv_ln_g=v_ab_conv_ln_g,
             ab_conv_ln_b=v_ab_conv_ln_b, ab_w_pw2=v_ab_w_pw2, ab_w_out=v_ab_w_out, ab_post_norm=v_ab_post_norm,
             sb_pre_norm=v_sb_pre_norm, sb_w_in=v_sb_w_in, sb_w_out=v_sb_w_out, sb_post_norm=v_sb_post_norm)

    def small_of(d):
        return _pack_small(d["ab_conv_w"][0], d["meta_tokens"], d["sb_pre_norm"], d["sb_post_norm"])

    def rep_of(d):
        return _pack_rep(d["ab_pre_norm"], d["ab_post_norm"], d["ab_conv_b"], d["ab_conv_ln_g"], d["ab_conv_ln_b"],
                         d["ab_sinks"])

    g_in0, g_small = _gather_chips([ab_w_in[0].T.astype(BF16), small_of(w)])
    conv_w_f = _chips_to_cols(g_small[:, 0:31])
    meta_f = _chips_to_cols(g_small[:, 32:64].reshape(N_CHIPS, 16, 256))
    sb_pre_f = g_small[:, 64:66].reshape(1, D)
    sb_post_f = g_small[:, 72:74].reshape(1, D)
    full = {
        "meta_tokens": meta_f, "ab_pre_norm": ab_pre_norm, "ab_w_in": g_in0.reshape(AB_IN, D),
        "ab_sinks": ab_sinks, "ab_conv_w": conv_w_f, "ab_conv_b": ab_conv_b, "ab_conv_ln_g": ab_conv_ln_g,
        "ab_conv_ln_b": ab_conv_ln_b, "ab_w_pw2": ab_w_pw2[0].astype(BF16), "ab_w_out": ab_w_out[0].astype(BF16),
        "ab_post_norm": ab_post_norm, "sb_pre_norm": sb_pre_f, "sb_w_in": sb_w_in[0].astype(BF16),
        "sb_w_out": sb_w_out[0].astype(BF16), "sb_post_norm": sb_post_f,
    }

    loss_row, grad_x, g, halves = _local_step(x[0], loss_target[0], full)

    total = _join_cores(halves)

    rep_g = _pack_rep(g["ab_pre_norm"], g["ab_post_norm"], g["ab_conv_b"], g["ab_conv_ln_g"], g["ab_conv_ln_b"],
                      g["ab_sinks"], loss_row[0:1, 0:1])
    vec = jnp.concatenate([rep_g, jnp.pad(g["ab_conv_w"].reshape(124, 128), ((0, 4), (0, 0))),
                           g["meta_tokens"].reshape(128, 128), g["sb_pre_norm"].reshape(8, 128),
                           g["sb_post_norm"].reshape(8, 128)], axis=0)
    vec_sum = _sum8(_gather_all(vec), "sum8_small")

    out_g, out_d, out_m, out_v = {}, {}, {}, {}
    for i, k in enumerate(BIG):
        shp = w[k].shape
        if k == "ab_w_in":
            res = _adamw(w[k][0].T, [total[i]], m[k][0].T, v[k][0].T, "adamw_" + k)
            out_g[k], out_d[k], out_m[k], out_v[k] = [r.T.reshape(shp) for r in res]
            continue
        res = _adamw(w[k][0], [total[i]], m[k][0], v[k][0], "adamw_" + k)
        out_g[k], out_d[k], out_m[k], out_v[k] = [r.reshape(shp) for r in res]

    rep_sum = vec_sum[0:REP_ROWS]
    loss = rep_sum.reshape(-1)[REP_LOSS]
    me = 2 * lax.axis_index("x") + lax.axis_index("y")
    small_sum = _pack_small(
        lax.dynamic_slice_in_dim(vec_sum[32:156].reshape(CONV_W, CC), me * 128, 128, axis=1),
        lax.dynamic_slice_in_dim(vec_sum[160:288].reshape(N_META, D), me * 256, 256, axis=1),
        lax.dynamic_slice_in_dim(vec_sum[288:296].reshape(1, D), me * 256, 256, axis=1),
        lax.dynamic_slice_in_dim(vec_sum[296:304].reshape(1, D), me * 256, 256, axis=1))
    res = _adamw(small_of(w), [small_sum], small_of(m), small_of(v), "adamw_small")
    for dst, r in zip((out_g, out_d, out_m, out_v), res):
        cw, mt, pre, post = _unpack_small(r)
        dst["ab_conv_w"], dst["meta_tokens"], dst["sb_pre_norm"], dst["sb_post_norm"] = cw[None], mt, pre, post
    res = _adamw(rep_of(w), [rep_sum], rep_of(m), rep_of(v), "adamw_rep")
    for dst, r in zip((out_g, out_d, out_m, out_v), res):
        (dst["ab_pre_norm"], dst["ab_post_norm"], dst["ab_conv_b"], dst["ab_conv_ln_g"], dst["ab_conv_ln_b"],
         dst["ab_sinks"]) = _unpack_rep(r)

    return (loss, grad_x[None], *[out_g[k] for k in WEIGHTS], *[out_d[k] for k in WEIGHTS],
            *[out_m[k] for k in WEIGHTS], *[out_v[k] for k in WEIGHTS])
```

```python
import functools

import jax
import jax.numpy as jnp
from jax import lax
from jax.experimental import pallas as pl
from jax.experimental.pallas import tpu as pltpu

F32 = jnp.float32
BF16 = jnp.bfloat16

D = 1024
SEQ = 2048
N_META = 16
TB = 128
TR = 272
PAD = TB - N_META
R = SEQ + TB
NB = R // TB
HD = 64
ROPE_THETA = 10000.0
NORM_EPS = 1e-6
LN_EPS = 1e-5
NEG = -1e30
CONV_W = 31
SCALE = HD ** -0.5
N_CHIPS = 4

C_Q, C_K, C_V, C_GA, C_GLU, C_GB = 0, 512, 640, 768, 1280, 2304
AB_IN = 2816

ADAM_LR, ADAM_B1, ADAM_B2, ADAM_EPS, ADAM_WD, ADAM_STEP = 0.001, 0.9, 0.999, 1e-08, 0.01, 10

VMEM_LIMIT = 56 * 1024 * 1024


def _cp(sem):
    return pltpu.CompilerParams(dimension_semantics=sem, vmem_limit_bytes=VMEM_LIMIT)


def _sig(x):
    return 1.0 / (1.0 + jnp.exp(-x))


def _dot(a, b):
    return lax.dot_general(a, b, (((1,), (0,)), ((), ())), preferred_element_type=F32)


def _dot_nt(a, b):
    return lax.dot_general(a, b, (((1,), (1,)), ((), ())), preferred_element_type=F32)


def _dot_tn(a, b):
    return lax.dot_general(a, b, (((0,), (0,)), ((), ())), preferred_element_type=F32)


def _mm(pairs, out_dtype, name, tm, tn, ta=False, tb=False, plan=None):
    pairs = [(a, b if isinstance(b, tuple) else (b, None)) for a, b in pairs]
    a0, (b0, _) = pairs[0]
    m = a0.shape[1] if ta else a0.shape[0]
    n = b0.shape[-2] if tb else b0.shape[-1]
    npairs = len(pairs)
    dims = (((0 if ta else 1,), (1 if tb else 0,)), ((), ()))
    c_in = len(plan["args"]) if plan else 0
    c_out = len(plan["out_shape"]) if plan else 0
    steps = (m // tm) * (n // tn)

    def body(*refs):
        o_ref = refs[2 * npairs + c_in]
        if plan:
            comm = (refs[2 * npairs:2 * npairs + c_in], refs[2 * npairs + c_in + 1:2 * npairs + c_in + 1 + c_out],
                    refs[2 * npairs + c_in + 1 + c_out:])
            step = pl.program_id(0) * (n // tn) + pl.program_id(1)

            @pl.when(step == 0)
            def _():
                plan["start"](*comm)

        acc = None
        for i in range(npairs):
            t = lax.dot_general(refs[2 * i][...].astype(BF16), refs[2 * i + 1][...].astype(BF16), dims,
                                preferred_element_type=F32)
            acc = t if acc is None else acc + t
        o_ref[...] = acc.astype(out_dtype)
        if plan:
            @pl.when(step == steps - 2)
            def _():
                plan["mid"](*comm)

            @pl.when(step == steps - 1)
            def _():
                plan["finish"](*comm)

    in_specs, args = [], []
    for a, (b, sel) in pairs:
        k = a.shape[0] if ta else a.shape[1]
        in_specs.append(pl.BlockSpec((k, tm), lambda i, j: (0, i)) if ta else pl.BlockSpec((tm, k), lambda i, j: (i, 0)))
        bshape, bidx = ((tn, k), lambda i, j: (j, 0)) if tb else ((k, tn), lambda i, j: (0, j))
        if sel is None:
            in_specs.append(pl.BlockSpec(bshape, bidx))
        else:
            in_specs.append(pl.BlockSpec((None,) + bshape, functools.partial(lambda i, j, f, s: (s,) + f(i, j), f=bidx, s=sel)))
        args += [a, b]
    out_spec = pl.BlockSpec((tm, tn), lambda i, j: (i, j))
    out_shape = jax.ShapeDtypeStruct((m, n), out_dtype)
    if not plan:
        return pl.pallas_call(
            body, grid=(m // tm, n // tn), in_specs=in_specs, out_specs=out_spec, out_shape=out_shape, name=name,
            compiler_params=_cp(("parallel", "parallel")))(*args)
    assert steps >= 2
    res = pl.pallas_call(
        body, grid=(m // tm, n // tn), in_specs=in_specs + [ANY] * c_in, out_specs=[out_spec] + [ANY] * c_out,
        out_shape=[out_shape] + plan["out_shape"], scratch_shapes=plan["sems"], name=name,
        compiler_params=_cp(("arbitrary", "arbitrary")))(*args, *plan["args"])
    return res[0], res[1:]


PW_TM = 544


def _pw2_fwd(s, w, z0):
    def body(s_ref, w_ref, g_ref, t_ref, c_ref):
        t = _dot(s_ref[...], w_ref[...])
        gv = g_ref[...]
        t_ref[...] = t
        c_ref[...] = (t * (gv * _sig(gv))).astype(BF16)

    blk = pl.BlockSpec((PW_TM, CC), lambda i: (i, 0))
    return pl.pallas_call(
        body, grid=(R // PW_TM,),
        in_specs=[blk, pl.BlockSpec((CC, CC), lambda i: (0, 0)), _cols_spec(PW_TM, CC, lambda i: (i, C_GB))],
        out_specs=[blk, blk],
        out_shape=[jax.ShapeDtypeStruct((R, CC), F32), jax.ShapeDtypeStruct((R, CC), BF16)],
        name="pw2_fwd", compiler_params=_cp(("parallel",)))(s, w, z0)


def _pw2_bwd(dmix, t, z0, w, s):
    def body(d_ref, t_ref, g_ref, w_ref, s_ref, dg_ref, ds_ref, dw_ref):
        @pl.when(pl.program_id(0) == 0)
        def _():
            dw_ref[...] = jnp.zeros_like(dw_ref)

        gv, dv = g_ref[...], d_ref[...]
        sg = _sig(gv)
        dg_ref[...] = (dv * t_ref[...] * (sg * (1.0 + gv * (1.0 - sg)))).astype(BF16)
        dt = (dv * (gv * sg)).astype(BF16)
        ds_ref[...] = _dot_nt(dt, w_ref[...])
        dw_ref[...] += _dot_tn(s_ref[...], dt)

    blk = pl.BlockSpec((PW_TM, CC), lambda i: (i, 0))
    full = pl.BlockSpec((CC, CC), lambda i: (0, 0))
    return pl.pallas_call(
        body, grid=(R // PW_TM,),
        in_specs=[_cols_spec(PW_TM, CC, lambda i: (i, CC)), blk, _cols_spec(PW_TM, CC, lambda i: (i, C_GB)), full, blk],
        out_specs=[blk, blk, full],
        out_shape=[jax.ShapeDtypeStruct((R, CC), BF16), jax.ShapeDtypeStruct((R, CC), F32),
                   jax.ShapeDtypeStruct((CC, CC), F32)],
        name="pw2_bwd", compiler_params=_cp(("arbitrary",)))(dmix, t, z0, w, s)


def _embed_proj(x, meta, g, wt):
    def body(x_ref, m_ref, g_ref, w_ref, h_ref, xn_ref, z_ref):
        n = pl.program_id(0)
        top = jnp.concatenate([jnp.zeros((PAD, D), F32), m_ref[...]], axis=0)
        hv = jnp.where(n == 0, top, x_ref[...])
        h_ref[...] = hv
        r = lax.rsqrt(jnp.mean(hv * hv, axis=1, keepdims=True) + NORM_EPS)
        xn = (hv * r * g_ref[...]).astype(BF16)
        xn_ref[...] = xn
        z_ref[...] = _dot_nt(xn, w_ref[...])

    blk = pl.BlockSpec((TB, D), lambda n: (n, 0))
    return dict(
        body=body,
        in_specs=[pl.BlockSpec((TB, D), lambda n: (jnp.maximum(n - 1, 0), 0)), pl.BlockSpec((N_META, D), lambda n: (0, 0)),
                  pl.BlockSpec((1, D), lambda n: (0, 0)), pl.BlockSpec(wt.shape, lambda n: (0, 0))],
        args=[x, meta, g, wt],
        out_specs=[blk, blk, pl.BlockSpec((TB, AB_IN), lambda n: (n, 0))],
        out_shape=[jax.ShapeDtypeStruct((R, D), F32), jax.ShapeDtypeStruct((R, D), BF16),
                   jax.ShapeDtypeStruct((R, AB_IN), F32)],
        scratch=[])


def _in_proj1_dw(xn, dzs):
    t = 512

    def body(x_ref, *refs):
        o_ref = refs[len(dzs)]
        xv = x_ref[...]
        for j in range(len(dzs)):
            o_ref[j] = _dot_tn(xv, refs[j][...]).astype(BF16)

    return pl.pallas_call(
        body, grid=(D // t, D // t),
        in_specs=[pl.BlockSpec((R, t), lambda i, n: (0, i))] + [pl.BlockSpec((R, t), lambda i, n: (0, n))] * len(dzs),
        out_specs=pl.BlockSpec((len(dzs), t, t), lambda i, n: (0, i, n)),
        out_shape=jax.ShapeDtypeStruct((len(dzs), D, D), BF16), name="in_proj1_dw",
        compiler_params=_cp(("parallel", "parallel")))(xn, *dzs)


def _in_proj1(xn, w):
    def body(x_ref, w_ref, q_ref, k_ref, v_ref, g_ref):
        xv = x_ref[...]
        q_ref[...] = _dot(xv, w_ref[0]).astype(BF16)
        k_ref[...] = _dot(xv, w_ref[1]).astype(BF16)
        v_ref[...] = _dot(xv, w_ref[2]).astype(BF16)
        g_ref[...] = _dot(xv, w_ref[3])

    blk = pl.BlockSpec((TR, D), lambda n: (n, 0))
    sd = jax.ShapeDtypeStruct((R, D), BF16)
    return pl.pallas_call(
        body, grid=(R // TR,), in_specs=[blk, pl.BlockSpec(w.shape, lambda n: (0, 0, 0))],
        out_specs=[blk, blk, blk, blk], out_shape=[sd, sd, sd, jax.ShapeDtypeStruct((R, D), F32)],
        name="in_proj1", compiler_params=_cp(("parallel",)))(xn, w)


def _rms_fwd(h, g, name):
    def body(h_ref, g_ref, o_ref):
        x = h_ref[...]
        r = lax.rsqrt(jnp.mean(x * x, axis=1, keepdims=True) + NORM_EPS)
        o_ref[...] = (x * r * g_ref[...]).astype(BF16)

    return pl.pallas_call(
        body, grid=(R // TR,),
        in_specs=[pl.BlockSpec((TR, D), lambda n: (n, 0)), pl.BlockSpec((1, D), lambda n: (0, 0))],
        out_specs=pl.BlockSpec((TR, D), lambda n: (n, 0)),
        out_shape=jax.ShapeDtypeStruct((R, D), BF16), name=name, compiler_params=_cp(("parallel",)))(h, g)


def _rms_bwd(dout, x, g, res, out_dtype, name, split=False):
    has_res = res is not None

    def body(*refs):
        if split:
            refs = list(refs)
            dx_rest_ref = refs.pop(-2)
        if has_res:
            d_ref, x_ref, g_ref, r_ref, dx_ref, dg_ref = refs
        else:
            d_ref, x_ref, g_ref, dx_ref, dg_ref = refs
        n = pl.program_id(0)
        xv = x_ref[...]
        dv = d_ref[...]
        r = lax.rsqrt(jnp.mean(xv * xv, axis=1, keepdims=True) + NORM_EPS)
        xh = xv * r
        dxh = dv * g_ref[...]
        dx = r * (dxh - xh * jnp.mean(dxh * xh, axis=1, keepdims=True))
        if has_res:
            dx = dx + r_ref[...]
        row = lax.broadcasted_iota(jnp.int32, (TB, D), 0) + n * TB
        dx = jnp.where(row >= PAD, dx, 0.0).astype(out_dtype)
        if split:
            @pl.when(n == 0)
            def _():
                dx_ref[...] = dx

            @pl.when(n > 0)
            def _():
                dx_rest_ref[...] = dx
        else:
            dx_ref[...] = dx

        @pl.when(n == 0)
        def _():
            dg_ref[...] = jnp.zeros_like(dg_ref)

        dg_ref[...] += jnp.sum(dv * xh, axis=0, keepdims=True)

    blk = pl.BlockSpec((TB, D), lambda n: (n, 0))
    vec = pl.BlockSpec((1, D), lambda n: (0, 0))
    ins = [dout, x, g] + ([res] if has_res else [])
    in_specs = [blk, blk, vec] + ([blk] if has_res else [])
    if split:
        out_specs = [pl.BlockSpec((TB, D), lambda n: (0, 0)), pl.BlockSpec((TB, D), lambda n: (jnp.maximum(n - 1, 0), 0)), vec]
        out_shape = [jax.ShapeDtypeStruct((TB, D), out_dtype), jax.ShapeDtypeStruct((SEQ, D), out_dtype),
                     jax.ShapeDtypeStruct((1, D), F32)]
    else:
        out_specs = [blk, vec]
        out_shape = [jax.ShapeDtypeStruct((R, D), out_dtype), jax.ShapeDtypeStruct((1, D), F32)]
    return pl.pallas_call(
        body, grid=(NB,), in_specs=in_specs, out_specs=out_specs, out_shape=out_shape,
        name=name, compiler_params=_cp(("arbitrary",)))(*ins)


def _post_rms_fwd(xs, w, h, g_post, g_next, name):
    nx = len(xs)

    def body(*refs):
        x_refs, (w_ref, h_ref, gp_ref, gn_ref, y_ref, o_ref, xn_ref) = refs[:nx], refs[nx:]
        yv = _dot(x_refs[0][...], w_ref[0])
        for j in range(1, nx):
            yv = yv + _dot(x_refs[j][...], w_ref[j])
        y_ref[...] = yv
        r = lax.rsqrt(jnp.mean(yv * yv, axis=1, keepdims=True) + NORM_EPS)
        hn = h_ref[...] + yv * r * gp_ref[...]
        o_ref[...] = hn
        r2 = lax.rsqrt(jnp.mean(hn * hn, axis=1, keepdims=True) + NORM_EPS)
        xn_ref[...] = (hn * r2 * gn_ref[...]).astype(BF16)

    blk = pl.BlockSpec((TR, D), lambda n: (n, 0))
    vec = pl.BlockSpec((1, D), lambda n: (0, 0))
    xblk = [pl.BlockSpec((TR, x.shape[1]), lambda n: (n, 0)) for x in xs]
    return pl.pallas_call(
        body, grid=(R // TR,), in_specs=xblk + [pl.BlockSpec(w.shape, lambda n: (0, 0, 0)), blk, vec, vec],
        out_specs=[blk, blk, blk],
        out_shape=[jax.ShapeDtypeStruct((R, D), F32), jax.ShapeDtypeStruct((R, D), F32),
                   jax.ShapeDtypeStruct((R, D), BF16)],
        name=name, compiler_params=_cp(("parallel",)))(*xs, w, h, g_post, g_next)


def _rms_post_bwd(dzs, w, h, g, res, y, g_post, name):
    nz = len(dzs)

    def body(*refs):
        dz_refs, (w_ref, h_ref, g_ref, r_ref, y_ref, gp_ref, dh_ref, dg_ref, dy_ref, dgp_ref) = refs[:nz], refs[nz:]
        n = pl.program_id(0)

        @pl.when(n == 0)
        def _():
            dg_ref[...] = jnp.zeros_like(dg_ref)
            dgp_ref[...] = jnp.zeros_like(dgp_ref)

        dv = _dot_nt(dz_refs[0][...], w_ref[0])
        for j in range(1, nz):
            dv = dv + _dot_nt(dz_refs[j][...], w_ref[j])
        hv = h_ref[...]
        r = lax.rsqrt(jnp.mean(hv * hv, axis=1, keepdims=True) + NORM_EPS)
        xh = hv * r
        dxh = dv * g_ref[...]
        dh = r * (dxh - xh * jnp.mean(dxh * xh, axis=1, keepdims=True)) + r_ref[...]
        row = lax.broadcasted_iota(jnp.int32, (TR, D), 0) + n * TR
        dh = jnp.where(row >= PAD, dh, 0.0)
        dh_ref[...] = dh
        dg_ref[...] += jnp.sum(dv * xh, axis=0, keepdims=True)
        yv = y_ref[...]
        ry = lax.rsqrt(jnp.mean(yv * yv, axis=1, keepdims=True) + NORM_EPS)
        yh = yv * ry
        dyh = dh * gp_ref[...]
        dy_ref[...] = (ry * (dyh - yh * jnp.mean(dyh * yh, axis=1, keepdims=True))).astype(BF16)
        dgp_ref[...] += jnp.sum(dh * yh, axis=0, keepdims=True)

    blk = pl.BlockSpec((TR, D), lambda n: (n, 0))
    vec = pl.BlockSpec((1, D), lambda n: (0, 0))
    return pl.pallas_call(
        body, grid=(R // TR,),
        in_specs=[blk] * nz + [pl.BlockSpec(w.shape, lambda n: (0, 0, 0)), blk, vec, blk, blk, vec],
        out_specs=[blk, vec, blk, vec],
        out_shape=[jax.ShapeDtypeStruct((R, D), F32), jax.ShapeDtypeStruct((1, D), F32),
                   jax.ShapeDtypeStruct((R, D), BF16), jax.ShapeDtypeStruct((1, D), F32)],
        name=name, compiler_params=_cp(("arbitrary",)))(*dzs, w, h, g, res, y, g_post)


def _cols_spec(rows, width, where):
    def index(*g):
        r, c = where(*g)
        return r * rows, (c if isinstance(c, int) else pl.multiple_of(c, 128))
    return pl.BlockSpec((pl.Element(rows), pl.Element(width)), index)


def _tail(x, w, h, g, target):
    steps = R // TR

    def body(x_ref, w_ref, h_ref, g_ref, t_ref, d_ref, dy_ref, dg_ref, l_ref):
        n = pl.program_id(0)

        @pl.when(n == 0)
        def _():
            dg_ref[...] = jnp.zeros_like(dg_ref)
            l_ref[...] = jnp.zeros_like(l_ref)

        yv = _dot(x_ref[...], w_ref[...])
        r = lax.rsqrt(jnp.mean(yv * yv, axis=1, keepdims=True) + NORM_EPS)
        yh = yv * r
        tv = t_ref[...]
        tv = jnp.where(n == 0, jnp.concatenate([jnp.zeros((TB, D), F32), tv[0:TR - TB]], axis=0), tv)
        row = lax.broadcasted_iota(jnp.int32, (TR, D), 0) + n * TR
        err = jnp.where(row >= TB, (h_ref[...] + yh * g_ref[...]) - tv, 0.0)
        dv = err * (1.0 / D)
        d_ref[...] = dv
        l_ref[...] += jnp.sum(err * err, axis=0, keepdims=True)
        dyh = dv * g_ref[...]
        dy_ref[...] = (r * (dyh - yh * jnp.mean(dyh * yh, axis=1, keepdims=True))).astype(BF16)
        dg_ref[...] += jnp.sum(dv * yh, axis=0, keepdims=True)

        @pl.when(n == steps - 1)
        def _():
            tot = jnp.sum(l_ref[...], axis=1, keepdims=True) * (0.5 / D)
            l_ref[...] = jnp.broadcast_to(tot, (1, D))

    blk = pl.BlockSpec((TR, D), lambda n: (n, 0))
    vec = pl.BlockSpec((1, D), lambda n: (0, 0))
    tgt = pl.BlockSpec((pl.Element(TR), pl.Element(D)),
                       lambda n: (pl.multiple_of(jnp.maximum(TR * n - TB, 0), 8), 0))
    return pl.pallas_call(
        body, grid=(steps,),
        in_specs=[pl.BlockSpec((TR, x.shape[1]), lambda n: (n, 0)), pl.BlockSpec(w.shape, lambda n: (0, 0)), blk, vec, tgt],
        out_specs=[blk, blk, vec, vec],
        out_shape=[jax.ShapeDtypeStruct((R, D), F32), jax.ShapeDtypeStruct((R, D), BF16),
                   jax.ShapeDtypeStruct((1, D), F32), jax.ShapeDtypeStruct((1, D), F32)],
        name="out_proj1_tail", compiler_params=_cp(("arbitrary",)))(x, w, h, g, target)


def _lane_row(shape):
    return lax.broadcasted_iota(jnp.int32, shape, 1), lax.broadcasted_iota(jnp.int32, shape, 0)


def _rot_half(x, lane):
    return jnp.where(lane % HD < HD // 2, pltpu.roll(x, 128 - HD // 2, 1), pltpu.roll(x, HD // 2, 1))


def _swa_blocks(n):
    return (0, jnp.maximum(n - 1, 0), n)


SWA_STACKS = ((0, 0), (0, 1), (1, 0), (1, 1))


def _swa_masks(n, lane, row):
    qpos = n * TB + (row & (TB - 1))
    kp = (n - 1) * TB + lane
    kc = n * TB + lane
    m0 = (lane >= PAD) & (qpos - lane >= TB)
    mp = (kp >= PAD) & (qpos >= kp) & (qpos - kp < TB)
    mc = (kc >= PAD) & (qpos >= kc)
    return (m0, mp, mc)


def _stack_pair(xa, xb, par):
    lane = lax.broadcasted_iota(jnp.int32, (TB, 128), 1)
    keep = (lane < HD) if par == 0 else (lane >= HD)
    return jnp.concatenate([jnp.where(keep, xa, 0.0), jnp.where(keep, xb, 0.0)], axis=0)


def _per_head(a, b):
    row = lax.broadcasted_iota(jnp.int32, (2 * TB, 1), 0)
    return jnp.where(row < TB, a, b)


def _swa_load(n, zq_ref, zkv_ref, cs_ref, sn_ref, lane):
    r0 = pl.multiple_of(n * TB, TB)
    csq, snq = cs_ref[pl.ds(r0, TB), :], sn_ref[pl.ds(r0, TB), :]
    qc = []
    for c in range(4):
        x = zq_ref[:, c * 128:(c + 1) * 128]
        qc.append((x * csq + _rot_half(x, lane) * snq) * SCALE)
    qst = [_stack_pair(qc[2 * g], qc[2 * g + 1], par).astype(BF16) for g, par in SWA_STACKS]
    kvs = []
    for b in _swa_blocks(n):
        b0 = pl.multiple_of(b * TB, TB)
        csb, snb = cs_ref[pl.ds(b0, TB), :], sn_ref[pl.ds(b0, TB), :]
        kx = zkv_ref[pl.ds(b0, TB), 0:128]
        kr = kx * csb + _rot_half(kx, lane) * snb
        vx = zkv_ref[pl.ds(b0, TB), 128:256]
        kvs.append((kr.astype(BF16), pltpu.roll(kr, HD, 1).astype(BF16),
                    vx.astype(BF16), pltpu.roll(vx, HD, 1).astype(BF16), csb, snb, b0))
    return qst, (csq, snq), kvs


def _swa_fwd(z0, cs, sn, sinks):
    def body(zq_ref, zkv_ref, cs_ref, sn_ref, sk_ref, ga_ref, o_ref, a_ref, lse_ref):
        n = pl.program_id(0)
        lane, row = _lane_row((TB, 128))
        lo = lane < HD
        masks = _swa_masks(n, *_lane_row((2 * TB, 128)))
        qst, _, kvs = _swa_load(n, zq_ref, zkv_ref, cs_ref, sn_ref, lane)
        ss = [[jnp.where(m, _dot_nt(qst[si], k if par == g else ka), NEG)
               for (k, ka, _, _, _, _, _), m in zip(kvs, masks)] for si, (g, par) in enumerate(SWA_STACKS)]
        o2, lse2 = [], []
        for si, (g, par) in enumerate(SWA_STACKS):
            sink = _per_head(sk_ref[0, 4 * g + par], sk_ref[0, 4 * g + 2 + par])
            s = ss[si]
            mx = jnp.maximum(jnp.maximum(jnp.max(s[0], axis=1, keepdims=True), jnp.max(s[1], axis=1, keepdims=True)),
                             jnp.max(s[2], axis=1, keepdims=True))
            mx = jnp.maximum(mx, sink)
            es = [jnp.exp(sb - mx) for sb in s]
            den = (jnp.sum(es[0], axis=1, keepdims=True) + jnp.sum(es[1], axis=1, keepdims=True)
                   + jnp.sum(es[2], axis=1, keepdims=True) + jnp.exp(sink - mx))
            inv = 1.0 / den
            t = jnp.zeros((2 * TB, 128), F32)
            for (_, _, v, va, _, _, _), e in zip(kvs, es):
                t = t + _dot((e * inv).astype(BF16), v if par == g else va)
            o2.append(t)
            lse2.append(mx + jnp.log(den))
        lse_t = jnp.zeros((TB, 128), F32)
        for g in range(2):
            for t in range(2):
                rows = slice(t * TB, (t + 1) * TB)
                c = 2 * g + t
                oc = jnp.where(lo, o2[2 * g][rows], o2[2 * g + 1][rows])
                o_ref[:, c * 128:(c + 1) * 128] = oc
                gv = ga_ref[:, c * 128:(c + 1) * 128]
                a_ref[:, c * 128:(c + 1) * 128] = (oc * (gv * _sig(gv))).astype(BF16)
                for par in range(2):
                    lse_t = jnp.where(lane == 4 * g + 2 * t + par, lse2[2 * g + par][rows], lse_t)
        lse_ref[...] = lse_t

    full = pl.BlockSpec((R, 128), lambda n: (0, 0))
    return dict(
        body=body,
        in_specs=[pl.BlockSpec((TB, 512), lambda n: (n, C_Q // 512)),
                  pl.BlockSpec((R, 256), lambda n: (0, C_K // 256)), full, full,
                  pl.BlockSpec(memory_space=pltpu.SMEM), _cols_spec(TB, 512, lambda n: (n, C_GA))],
        args=[z0, z0, cs, sn, sinks, z0],
        out_specs=[pl.BlockSpec((TB, 512), lambda n: (n, 0)), pl.BlockSpec((TB, 512), lambda n: (n, 0)),
                   pl.BlockSpec((TB, 128), lambda n: (n, 0))],
        out_shape=[jax.ShapeDtypeStruct((R, 512), F32), jax.ShapeDtypeStruct((R, 512), BF16),
                   jax.ShapeDtypeStruct((R, 128), F32)],
        scratch=[])


def _swa_bwd(z0, cs, sn, sinks, o, dmix, lse):
    def body(zq_ref, zkv_ref, cs_ref, sn_ref, sk_ref, ga_ref, o_ref, dm_ref, lse_ref,
             dq_ref, dga_ref, dkv_ref, dsk_ref, do_ref, acc_ref):
        n = pl.program_id(0)

        @pl.when(n == 0)
        def _():
            acc_ref[...] = jnp.zeros_like(acc_ref)
            dsk_ref[...] = jnp.zeros_like(dsk_ref)

        gv, dmv = ga_ref[...], dm_ref[...]
        sg = _sig(gv)
        dga_ref[...] = (dmv * o_ref[...] * (sg * (1.0 + gv * (1.0 - sg)))).astype(BF16)
        do_ref[...] = dmv * (gv * sg)
        lane, row = _lane_row((TB, 128))
        lo = lane < HD
        masks = _swa_masks(n, *_lane_row((2 * TB, 128)))
        qst, (csq, snq), kvs = _swa_load(n, zq_ref, zkv_ref, cs_ref, sn_ref, lane)
        lse_t = lse_ref[...]
        ss = [[jnp.where(m, _dot_nt(qst[si], k if par == g else ka), NEG)
               for (k, ka, _, _, _, _, _), m in zip(kvs, masks)] for si, (g, par) in enumerate(SWA_STACKS)]
        dobs, deltas, lses, dps = [], [], [], []
        for g, par in SWA_STACKS:
            ca, cb = slice(2 * g * 128, (2 * g + 1) * 128), slice((2 * g + 1) * 128, (2 * g + 2) * 128)
            dom = _stack_pair(do_ref[:, ca], do_ref[:, cb], par)
            deltas.append(jnp.sum(dom * jnp.concatenate([o_ref[:, ca], o_ref[:, cb]], axis=0), axis=1, keepdims=True))
            dob = dom.astype(BF16)
            dobs.append(dob)
            lses.append(jnp.concatenate(
                [jnp.sum(jnp.where(lane == 4 * g + 2 * t + par, lse_t, 0.0), axis=1, keepdims=True) for t in range(2)],
                axis=0))
            dps.append([_dot_nt(dob, v if par == g else va) for (_, _, v, va, _, _, _) in kvs])
        dk_al = [jnp.zeros((TB, 128), F32) for _ in range(3)]
        dk_mis = [jnp.zeros((TB, 128), F32) for _ in range(3)]
        dv_al = [jnp.zeros((TB, 128), F32) for _ in range(3)]
        dv_mis = [jnp.zeros((TB, 128), F32) for _ in range(3)]
        dsk_t = jnp.zeros((TB, 128), F32)
        dq2 = []
        for si, (g, par) in enumerate(SWA_STACKS):
            dqt = jnp.zeros((2 * TB, 128), F32)
            for bi, (k, ka, _, _, _, _, _) in enumerate(kvs):
                p = jnp.exp(ss[si][bi] - lses[si])
                ds = (p * (dps[si][bi] - deltas[si])).astype(BF16)
                dqt = dqt + _dot(ds, k if par == g else ka)
                dkh = _dot_tn(ds, qst[si])
                dvh = _dot_tn(p.astype(BF16), dobs[si])
                if par == g:
                    dk_al[bi] = dk_al[bi] + dkh
                    dv_al[bi] = dv_al[bi] + dvh
                else:
                    dk_mis[bi] = dk_mis[bi] + dkh
                    dv_mis[bi] = dv_mis[bi] + dvh
            dq2.append(dqt)
            sink = _per_head(sk_ref[0, 4 * g + par], sk_ref[0, 4 * g + 2 + par])
            dsk = -jnp.exp(sink - lses[si]) * deltas[si]
            for t in range(2):
                dsk_t = jnp.where(lane == 4 * g + 2 * t + par, dsk[t * TB:(t + 1) * TB], dsk_t)
        for g in range(2):
            for t in range(2):
                rows = slice(t * TB, (t + 1) * TB)
                c = 2 * g + t
                dqc = jnp.where(lo, dq2[2 * g][rows], dq2[2 * g + 1][rows]) * SCALE
                dq_ref[:, c * 128:(c + 1) * 128] = (dqc * csq + _rot_half(dqc * snq, lane)).astype(BF16)
        for bi, (_, _, _, _, csb, snb, b0) in enumerate(kvs):
            dk = dk_al[bi] + pltpu.roll(dk_mis[bi], HD, 1)
            dv = dv_al[bi] + pltpu.roll(dv_mis[bi], HD, 1)
            acc_ref[pl.ds(b0, TB), 0:128] += dk * csb + _rot_half(dk * snb, lane)
            acc_ref[pl.ds(b0, TB), 128:256] += dv
        dsk_ref[0:1, :] += jnp.sum(dsk_t, axis=0, keepdims=True)

        @pl.when(n == NB - 1)
        def _():
            dkv_ref[...] = acc_ref[...].astype(BF16)

    full = pl.BlockSpec((R, 128), lambda n: (0, 0))
    b512 = pl.BlockSpec((TB, 512), lambda n: (n, 0))
    return dict(
        body=body,
        in_specs=[pl.BlockSpec((TB, 512), lambda n: (n, C_Q // 512)),
                  pl.BlockSpec((R, 256), lambda n: (0, C_K // 256)), full, full,
                  pl.BlockSpec(memory_space=pltpu.SMEM), _cols_spec(TB, 512, lambda n: (n, C_GA)),
                  b512, b512, pl.BlockSpec((TB, 128), lambda n: (n, 0))],
        args=[z0, z0, cs, sn, sinks, z0, o, dmix, lse],
        out_specs=[b512, b512, pl.BlockSpec((R, 256), lambda n: (0, 0)), pl.BlockSpec((8, 128), lambda n: (0, 0))],
        out_shape=[jax.ShapeDtypeStruct((R, 512), BF16), jax.ShapeDtypeStruct((R, 512), BF16),
                   jax.ShapeDtypeStruct((R, 256), BF16), jax.ShapeDtypeStruct((8, 128), F32)],
        scratch=[pltpu.VMEM((TB, 512), F32), pltpu.VMEM((R, 256), F32)])


CC = 512
HALO = CONV_W - 1


def _conv_fwd(z0, conv_w, conv_b, ln_g, ln_b):
    def body(g_ref, w_ref, cb_ref, lg_ref, lb_ref, cv_ref, s_ref, ubuf):
        n = pl.program_id(0)

        @pl.when(n == 0)
        def _():
            ubuf[...] = jnp.zeros_like(ubuf)

        u = g_ref[:, 0:CC] * _sig(g_ref[:, CC:2 * CC])
        for k in range(8):
            ubuf[k, 0:TB + 8, :] = ubuf[k, TB:2 * TB + 8, :]
            ubuf[k, pl.ds(TB + 8 - k, TB), :] = u
        acc = jnp.zeros((TB, CC), F32)
        for w in range(CONV_W):
            off = TB - HALO + w
            acc = acc + ubuf[off % 8, pl.ds(off + 8 - off % 8, TB), :] * w_ref[w:w + 1, :]
        cv = acc + cb_ref[...]
        cv_ref[...] = cv
        xc = cv - jnp.mean(cv, axis=1, keepdims=True)
        rs = lax.rsqrt(jnp.mean(xc * xc, axis=1, keepdims=True) + LN_EPS)
        ln = xc * rs * lg_ref[...] + lb_ref[...]
        s_ref[...] = (ln * _sig(ln)).astype(BF16)

    vec = pl.BlockSpec((1, CC), lambda n: (0, 0))
    blk = pl.BlockSpec((TB, CC), lambda n: (n, 0))
    return dict(
        body=body,
        in_specs=[_cols_spec(TB, 2 * CC, lambda n: (n, C_GLU)),
                  pl.BlockSpec((32, CC), lambda n: (0, 0)), vec, vec, vec],
        args=[z0, conv_w, conv_b, ln_g, ln_b],
        out_specs=[blk, blk],
        out_shape=[jax.ShapeDtypeStruct((R, CC), F32), jax.ShapeDtypeStruct((R, CC), BF16)],
        scratch=[pltpu.VMEM((8, 2 * TB + 8, CC), F32)])


def _conv_bwd(ds, cv, z0, conv_w, ln_g, ln_b):
    def body(ds_ref, cv_ref, g_ref, w_ref, lg_ref, lb_ref, dglu_ref, dw_ref, dsm_ref, dbuf):
        n = pl.program_id(0)

        @pl.when(n == 0)
        def _():
            dbuf[...] = jnp.zeros_like(dbuf)
            dw_ref[...] = jnp.zeros_like(dw_ref)
            dsm_ref[...] = jnp.zeros_like(dsm_ref)

        cv = cv_ref[...]
        xc = cv - jnp.mean(cv, axis=1, keepdims=True)
        rs = lax.rsqrt(jnp.mean(xc * xc, axis=1, keepdims=True) + LN_EPS)
        xh = xc * rs
        ln = xh * lg_ref[...] + lb_ref[...]
        sg = _sig(ln)
        dln = ds_ref[...] * (sg * (1.0 + ln * (1.0 - sg)))
        dxh = dln * lg_ref[...]
        dcv = rs * (dxh - jnp.mean(dxh, axis=1, keepdims=True) - xh * jnp.mean(dxh * xh, axis=1, keepdims=True))
        dsm_ref[0:1, :] += jnp.sum(dcv, axis=0, keepdims=True)
        dsm_ref[1:2, :] += jnp.sum(dln * xh, axis=0, keepdims=True)
        dsm_ref[2:3, :] += jnp.sum(dln, axis=0, keepdims=True)
        for k in range(8):
            dbuf[k, TB:2 * TB + 8, :] = dbuf[k, 0:TB + 8, :]
            dbuf[k, pl.ds(8 - k, TB), :] = dcv
        a = g_ref[:, 0:CC]
        sb = _sig(g_ref[:, CC:2 * CC])
        u = a * sb
        du = jnp.zeros((TB, CC), F32)
        for w in range(CONV_W):
            off = HALO - w
            sh = dbuf[off % 8, pl.ds(off + 8 - off % 8, TB), :]
            du = du + sh * w_ref[w:w + 1, :]
            dw_ref[w:w + 1, :] += jnp.sum(u * sh, axis=0, keepdims=True)
        dglu_ref[:, 0:CC] = (du * sb).astype(BF16)
        dglu_ref[:, CC:2 * CC] = (du * a * sb * (1.0 - sb)).astype(BF16)

    rev = lambda n: (NB - 1 - n, 0)
    vec = pl.BlockSpec((1, CC), lambda n: (0, 0))
    blk = pl.BlockSpec((TB, CC), rev)
    return dict(
        body=body,
        in_specs=[blk, blk, _cols_spec(TB, 2 * CC, lambda n: (NB - 1 - n, C_GLU)),
                  pl.BlockSpec((32, CC), lambda n: (0, 0)), vec, vec],
        args=[ds, cv, z0, conv_w, ln_g, ln_b],
        out_specs=[pl.BlockSpec((TB, 2 * CC), rev), pl.BlockSpec((32, CC), lambda n: (0, 0)),
                   pl.BlockSpec((8, CC), lambda n: (0, 0))],
        out_shape=[jax.ShapeDtypeStruct((R, 2 * CC), BF16), jax.ShapeDtypeStruct((32, CC), F32),
                   jax.ShapeDtypeStruct((8, CC), F32)],
        scratch=[pltpu.VMEM((8, 2 * TB + 8, CC), F32)])


def _split_dot(x, t):
    hi = x.astype(BF16)
    lo = (x - hi.astype(F32)).astype(BF16)
    return _dot(hi, t) + _dot(lo, t)


def _stack_heads(x):
    lane = lax.broadcasted_iota(jnp.int32, (TB, 128), 1)
    return jnp.concatenate([jnp.where(lane < HD, x, 0.0), jnp.where(lane < HD, 0.0, x)], axis=0).astype(BF16)


def _sb_stack(qv, i):
    lane2, row2 = _lane_row((2 * TB, 128))
    qpos2 = i * TB + (row2 & (TB - 1))
    lane, row = _lane_row((TB, 128))
    return _stack_heads(qv), lane2, qpos2, (row > lane).astype(BF16)


SB_U = 3
SB_DEAD = -104.0
SB_P = 4


def _sb_fwd(q, k, v, g):
    def body(q_ref, k_ref, v_ref, g_ref, o_ref, m_ref, c_ref, n_ref):
        p, i = pl.program_id(0), pl.program_id(1)
        lane, row = _lane_row((TB, 128))
        lo = lane < HD
        slabs = [slice(s * 128, (s + 1) * 128) for s in range(SB_P)]
        q2s = []
        for sl in slabs:
            q2, lane2, qpos2, tri_gt = _sb_stack(q_ref[:, sl].astype(F32) * SCALE, i)
            q2s.append(q2)

        def cond(st):
            t, _, c2s = st
            alive = jnp.max(c2s[0])
            for c2 in c2s[1:]:
                alive = jnp.maximum(alive, jnp.max(c2))
            return jnp.logical_and(i - SB_U * t >= 0, alive > SB_DEAD)

        def step(st):
            t, accs, c2s = st
            accs, c2s = list(accs), list(c2s)
            jrs = [i - SB_U * t - u for u in range(SB_U)]
            j0s = [pl.multiple_of(jnp.maximum(jr, 0) * TB, TB) for jr in jrs]
            valids = []
            for jr in jrs:
                kpos = jr * TB + lane2
                valids.append((kpos >= PAD) & (kpos < qpos2))
            zs = [[jnp.where(valid, _dot_nt(q2s[s], k_ref[pl.ds(j0, TB), slabs[s]]), NEG)
                   for j0, valid in zip(j0s, valids)] for s in range(SB_P)]
            lbs, l1s = [], []
            for s in range(SB_P):
                lbs.append([jnp.minimum(z, 0.0) - jnp.log(1.0 + jnp.exp(-jnp.abs(z))) for z in zs[s]])
                l1s.append([lb - z for lb, z in zip(lbs[s], zs[s])])
            sfxs = [[_split_dot(l1, tri_gt) for l1 in l1s[s]] for s in range(SB_P)]
            carries = []
            for s in range(SB_P):
                cs, c2 = [], c2s[s]
                for jr, l1 in zip(jrs, l1s[s]):
                    cs.append(c2)
                    c_ref[:, slabs[s]] = jnp.where(lane == 2 * jr, c2[0:TB],
                                                   jnp.where(lane == 2 * jr + 1, c2[TB:2 * TB], c_ref[:, slabs[s]]))
                    c2 = c2 + jnp.sum(l1, axis=1, keepdims=True)
                carries.append(cs)
                c2s[s] = c2
            for s in range(SB_P):
                for j0, valid, lb, sfx, cu in zip(j0s, valids, lbs[s], sfxs[s], carries[s]):
                    a = jnp.exp(lb + sfx + cu).astype(BF16)
                    av = _dot(a, v_ref[pl.ds(j0, TB), slabs[s]])
                    accs[s] = accs[s] + jnp.where(lo, av[0:TB], av[TB:2 * TB])
            return t + 1, tuple(accs), tuple(c2s)

        c_ref[...] = jnp.zeros_like(c_ref)
        init = (jnp.int32(0), tuple(jnp.zeros((TB, 128), F32) for _ in slabs),
                tuple(jnp.zeros((2 * TB, 1), F32) for _ in slabs))
        t, accs, _ = lax.while_loop(cond, step, init)
        for sl, acc in zip(slabs, accs):
            o_ref[:, sl] = acc
            gv = g_ref[:, sl]
            m_ref[:, sl] = (acc * (gv * _sig(gv))).astype(BF16)
        n_ref[p, i] = t

    wide = SB_P * 128
    slab = pl.BlockSpec((R, wide), lambda p, i: (0, p))
    blk = pl.BlockSpec((TB, wide), lambda p, i: (i, p))
    sd = jax.ShapeDtypeStruct((R, D), F32)
    return pl.pallas_call(
        body, grid=(D // wide, NB), in_specs=[blk, slab, slab, blk],
        out_specs=[blk, blk, blk, pl.BlockSpec(memory_space=pltpu.SMEM)],
        out_shape=[sd, jax.ShapeDtypeStruct((R, D), BF16), sd, jax.ShapeDtypeStruct((D // wide, NB), jnp.int32)],
        name="sb_fwd", compiler_params=_cp(("arbitrary", "arbitrary")))(q, k, v, g)


def _sb_bwd(trips, q, k, v, car, dm, g, o):
    def body(n_ref, q_ref, k_ref, v_ref, c_ref, dm_ref, g_ref, o_ref, dq_ref, dko_ref, dvo_ref, dg_ref,
             dk_ref, dv_ref):
        p, i = pl.program_id(0), pl.program_id(1)

        @pl.when(i == 0)
        def _():
            dk_ref[...] = jnp.zeros_like(dk_ref)
            dv_ref[...] = jnp.zeros_like(dv_ref)

        lane, row = _lane_row((TB, 128))
        lo = lane < HD
        tri_lt = (row < lane).astype(BF16)
        slabs = [slice(s * 128, (s + 1) * 128) for s in range(SB_P)]
        q2s, do2s, cts = [], [], []
        for sl in slabs:
            q2, lane2, qpos2, tri_gt = _sb_stack(q_ref[:, sl].astype(F32) * SCALE, i)
            q2s.append(q2)
            gv, dmv = g_ref[:, sl], dm_ref[:, sl]
            sg = _sig(gv)
            dg_ref[:, sl] = (dmv * o_ref[:, sl] * (sg * (1.0 + gv * (1.0 - sg)))).astype(BF16)
            do2s.append(_stack_heads(dmv * (gv * sg)))
            cts.append(c_ref[:, sl])
        trips_i = n_ref[p, i]
        first = jnp.maximum(i + 1 - SB_U * trips_i, 0)

        def step(t, carry):
            dqs, g2s = carry
            dqs, g2s = list(dqs), list(g2s)
            jrs = [first + SB_U * t + u for u in range(SB_U)]
            j0s = [pl.multiple_of(jnp.minimum(jr, i) * TB, TB) for jr in jrs]
            valids = []
            for jr in jrs:
                kpos = jr * TB + lane2
                valids.append((kpos >= PAD) & (kpos < qpos2))
            ks = [[k_ref[pl.ds(j0, TB), sl] for j0 in j0s] for sl in slabs]
            zs = [[jnp.where(valid, _dot_nt(q2s[s], kj), NEG) for kj, valid in zip(ks[s], valids)] for s in range(SB_P)]
            das = [[_dot_nt(do2s[s], v_ref[pl.ds(j0, TB), slabs[s]]) for j0 in j0s] for s in range(SB_P)]
            es = [[jnp.exp(-jnp.abs(z)) for z in zs[s]] for s in range(SB_P)]
            lbs = [[jnp.minimum(z, 0.0) - jnp.log(1.0 + e) for z, e in zip(zs[s], es[s])] for s in range(SB_P)]
            l1s = [[lb - z for lb, z in zip(lbs[s], zs[s])] for s in range(SB_P)]
            sfxs = [[_split_dot(l1, tri_gt) for l1 in l1s[s]] for s in range(SB_P)]
            a_s, gmats, gpres = [], [], []
            for s in range(SB_P):
                a_l, gm_l, gp_l, g2 = [], [], [], g2s[s]
                for jr, valid, lb, sfx, da in zip(jrs, valids, lbs[s], sfxs[s], das[s]):
                    later = jnp.concatenate(
                        [jnp.sum(jnp.where(lane == 2 * jr + hh, cts[s], 0.0), axis=1, keepdims=True) for hh in range(2)],
                        axis=0)
                    a = jnp.exp(lb + sfx + later)
                    gmat = da * a
                    a_l.append(a.astype(BF16))
                    gm_l.append(gmat)
                    gp_l.append(g2)
                    g2 = g2 + jnp.sum(gmat, axis=1, keepdims=True)
                a_s.append(a_l)
                gmats.append(gm_l)
                gpres.append(gp_l)
                g2s[s] = g2
            pres = [[gp + _split_dot(gmat, tri_lt) for gp, gmat in zip(gpres[s], gmats[s])] for s in range(SB_P)]
            for s in range(SB_P):
                for j0, kj, valid, z, e, gmat, pre, a in zip(j0s, ks[s], valids, zs[s], es[s], gmats[s], pres[s], a_s[s]):
                    r = 1.0 / (1.0 + e)
                    big = z >= 0.0
                    beta = jnp.where(big, r, e * r)
                    omb = jnp.where(big, e * r, r)
                    dz = (gmat * omb - beta * pre).astype(BF16)
                    dq2 = _dot(dz, kj)
                    dqs[s] = dqs[s] + jnp.where(lo, dq2[0:TB], dq2[TB:2 * TB])
                    dk_ref[pl.ds(j0, TB), slabs[s]] += _dot_tn(dz, q2s[s])
                    dv_ref[pl.ds(j0, TB), slabs[s]] += _dot_tn(a, do2s[s])
            return tuple(dqs), tuple(g2s)

        init = (tuple(jnp.zeros((TB, 128), F32) for _ in slabs), tuple(jnp.zeros((2 * TB, 1), F32) for _ in slabs))
        dqs, _ = lax.fori_loop(0, trips_i, step, init)
        for sl, dq in zip(slabs, dqs):
            dq_ref[:, sl] = (dq * SCALE).astype(BF16)

        @pl.when(i == NB - 1)
        def _():
            dko_ref[...] = dk_ref[...].astype(BF16)
            dvo_ref[...] = dv_ref[...].astype(BF16)

    wide = SB_P * 128
    slab = pl.BlockSpec((R, wide), lambda p, i: (0, p))
    blk = pl.BlockSpec((TB, wide), lambda p, i: (i, p))
    sd = jax.ShapeDtypeStruct((R, D), BF16)
    return pl.pallas_call(
        body, grid=(D // wide, NB),
        in_specs=[pl.BlockSpec(memory_space=pltpu.SMEM), blk, slab, slab, blk, blk, blk, blk],
        out_specs=[blk, slab, slab, blk], out_shape=[sd, sd, sd, sd],
        scratch_shapes=[pltpu.VMEM((R, wide), F32), pltpu.VMEM((R, wide), F32)], name="sb_bwd",
        compiler_params=_cp(("arbitrary", "arbitrary")))(trips, q, k, v, car, dm, g, o)


def _adamw(w, parts, m, v, name):
    rows, cols = w.shape
    tr = next((t for t in (256, 176) if rows % t == 0), rows)
    nparts = len(parts)

    def body(*refs):
        w_ref = refs[0]
        p_refs = refs[1:1 + nparts]
        m_ref, v_ref, g_ref, d_ref, nm_ref, nv_ref = refs[1 + nparts:]
        g = p_refs[0][...]
        for p_ref in p_refs[1:]:
            g = g + p_ref[...]
        nm = ADAM_B1 * m_ref[...] + (1.0 - ADAM_B1) * g
        nv = ADAM_B2 * v_ref[...] + (1.0 - ADAM_B2) * (g * g)
        m_hat = nm / (1.0 - ADAM_B1 ** ADAM_STEP)
        v_hat = nv / (1.0 - ADAM_B2 ** ADAM_STEP)
        g_ref[...] = g
        d_ref[...] = -ADAM_LR * (m_hat / (jnp.sqrt(v_hat) + ADAM_EPS) + ADAM_WD * w_ref[...])
        nm_ref[...] = nm
        nv_ref[...] = nv

    blk = pl.BlockSpec((tr, cols), lambda i: (i, 0))
    sd = jax.ShapeDtypeStruct((rows, cols), F32)
    return pl.pallas_call(
        body, grid=(rows // tr,), in_specs=[blk] * (3 + nparts), out_specs=[blk] * 4, out_shape=[sd] * 4,
        name=name, compiler_params=_cp(("parallel",)))(w, *parts, m, v)


def _sum8(buf, name):
    _, rows, cols = buf.shape

    def body(b_ref, o_ref):
        acc = b_ref[0]
        for i in range(1, 8):
            acc = acc + b_ref[i]
        o_ref[...] = acc

    return pl.pallas_call(
        body, out_shape=jax.ShapeDtypeStruct((rows, cols), F32), name=name,
        compiler_params=pltpu.CompilerParams(vmem_limit_bytes=VMEM_LIMIT))(buf)


MESH = pl.DeviceIdType.MESH
ANY = pl.BlockSpec(memory_space=pl.ANY)


def _chip_peers():
    x, y = lax.axis_index("x"), lax.axis_index("y")
    return [(1 - x, y), (x, 1 - y), (1 - x, 1 - y)]


def _gather_chips(shards):
    plan = _gather_plan(shards)

    def body(*refs):
        n = len(shards)
        ins, outs, sems = refs[:n], refs[n:2 * n], refs[2 * n:]
        plan["start"](ins, outs, sems)
        plan["mid"](ins, outs, sems)
        plan["finish"](ins, outs, sems)

    n = len(shards)
    res = pl.pallas_call(
        body, in_specs=[ANY] * n, out_specs=[ANY] * n, out_shape=plan["out_shape"],
        scratch_shapes=plan["sems"], name="gather_chips")(*plan["args"])
    return plan["post"](res)


def _gather_plan(shards):
    n = len(shards)
    shards = [s.reshape((2, s.shape[0] // 2) + s.shape[1:]) for s in shards]

    def copies(kind, ins, outs, sems):
        s1, r1, s2, r2 = sems
        x, y, c = lax.axis_index("x"), lax.axis_index("y"), lax.axis_index("c")
        me = 2 * x + y
        out = []
        for j, (px, py) in enumerate(_chip_peers()):
            for a in range(n):
                k = j * n + a
                got = outs[a].at[2 * px + py].at[c]
                other = outs[a].at[2 * px + py].at[1 - c]
                src, dst, ss, rs, dev = {
                    "first": (ins[a].at[c], outs[a].at[me].at[c], s1, r1, (px, py, c)),
                    "landed": (got, got, s1, r1, (px, py, c)),
                    "passed": (got, got, s2, r2, (x, y, 1 - c)),
                    "theirs": (other, other, s2, r2, (x, y, 1 - c)),
                }[kind]
                out.append(pltpu.make_async_remote_copy(
                    src_ref=src, dst_ref=dst, send_sem=ss.at[k], recv_sem=rs.at[k], device_id=dev, device_id_type=MESH))
        return out

    def start(ins, outs, sems):
        for cp in copies("first", ins, outs, sems):
            cp.start()

    def mid(ins, outs, sems):
        for got, fwd in zip(copies("landed", ins, outs, sems), copies("passed", ins, outs, sems)):
            got.wait_recv()
            fwd.start()

    def finish(ins, outs, sems):
        for cp in copies("theirs", ins, outs, sems):
            cp.wait_recv()
        for cp in copies("first", ins, outs, sems) + copies("passed", ins, outs, sems):
            cp.wait_send()

    def post(res):
        me = 2 * lax.axis_index("x") + lax.axis_index("y")
        res = [lax.dynamic_update_index_in_dim(r, s, me, 0) for r, s in zip(res, shards)]
        return [r.reshape((N_CHIPS, 2 * r.shape[2]) + r.shape[3:]) for r in res]

    return dict(args=shards, out_shape=[jax.ShapeDtypeStruct((N_CHIPS,) + s.shape, s.dtype) for s in shards],
                sems=[pltpu.SemaphoreType.DMA((3 * n,))] * 4, start=start, mid=mid, finish=finish, post=post)


def _rows_call(name, parts, plan):
    n_in = [len(p["args"]) for p in parts]
    n_out = [len(p["out_shape"]) for p in parts]
    n_scr = [len(p["scratch"]) for p in parts]
    c_in, c_out = len(plan["args"]), len(plan["out_shape"])

    def split(refs, sizes):
        out, pos = [], 0
        for k in sizes:
            out.append(refs[pos:pos + k])
            pos += k
        return out

    def body(*refs):
        ins, outs, scr = split(refs, [sum(n_in) + c_in, sum(n_out) + c_out, sum(n_scr) + len(plan["sems"])])
        p_in, p_out, p_scr = split(ins, n_in + [c_in]), split(outs, n_out + [c_out]), split(scr, n_scr + [len(plan["sems"])])
        comm = (p_in[-1], p_out[-1], p_scr[-1])
        step = pl.program_id(0)

        @pl.when(step == 0)
        def _():
            plan["start"](*comm)

        for p, i, o, s in zip(parts, p_in, p_out, p_scr):
            p["body"](*i, *o, *s)

        @pl.when(step == NB - 2)
        def _():
            plan["mid"](*comm)

        @pl.when(step == NB - 1)
        def _():
            plan["finish"](*comm)

    flat = lambda key: [v for p in parts for v in p[key]]
    res = pl.pallas_call(
        body, grid=(NB,), in_specs=flat("in_specs") + [ANY] * c_in, out_specs=flat("out_specs") + [ANY] * c_out,
        out_shape=flat("out_shape") + plan["out_shape"], scratch_shapes=flat("scratch") + plan["sems"],
        name=name, compiler_params=_cp(("arbitrary",)))(*flat("args"), *plan["args"])
    outs = split(res, n_out + [c_out])
    return outs[:-1], outs[-1]


def _pair_exchange(grads, name):
    n = len(grads)
    hs = [g.shape[1] // 2 for g in grads]
    grads = [g.reshape((N_CHIPS, 2, h) + g.shape[2:]) for g, h in zip(grads, hs)]

    def body(*refs):
        ins, got = refs[:n], refs[n:2 * n]
        ssem, rsem = refs[2 * n:]
        x, y, c = lax.axis_index("x"), lax.axis_index("y"), lax.axis_index("c")
        sends = [pltpu.make_async_remote_copy(
            src_ref=ins[a].at[:, 1 - c], dst_ref=got[a], send_sem=ssem.at[a],
            recv_sem=rsem.at[a], device_id=(x, y, 1 - c), device_id_type=MESH) for a in range(n)]
        for cp in sends:
            cp.start()
        for cp in sends:
            cp.wait()

    half_shapes = [jax.ShapeDtypeStruct((N_CHIPS, h) + g.shape[3:], g.dtype) for g, h in zip(grads, hs)]
    got = pl.pallas_call(
        body, in_specs=[ANY] * n, out_specs=[ANY] * n, out_shape=half_shapes,
        scratch_shapes=[pltpu.SemaphoreType.DMA((n,))] * 2, name=name)(*grads)
    return grads, got


def _sum_pair(both, got, send_dtype, name):
    _, _, rows, cols = both.shape
    tr = 256 if rows % 256 == 0 else rows

    def body(c_ref, a_ref, b_ref, f_ref, s_ref):
        t = a_ref[...].astype(F32) + b_ref[...].astype(F32)
        f_ref[...] = t
        s_ref[...] = t.astype(send_dtype)

    blk = pl.BlockSpec((N_CHIPS, tr, cols), lambda i, c: (0, i, 0))
    mine = pl.BlockSpec((N_CHIPS, None, tr, cols), lambda i, c: (0, c[0], i, 0))
    return pl.pallas_call(
        body, grid_spec=pltpu.PrefetchScalarGridSpec(
            num_scalar_prefetch=1, grid=(rows // tr,), in_specs=[mine, blk], out_specs=[blk, blk]),
        out_shape=[jax.ShapeDtypeStruct(got.shape, F32), jax.ShapeDtypeStruct(got.shape, send_dtype)],
        name=name, compiler_params=_cp(("parallel",)))(lax.axis_index("c").reshape(1), both, got)


def _scatter_plan(send):
    n = len(send)

    def copies(sin, land, sems):
        ssem, rsem = sems
        c = lax.axis_index("c")
        return [pltpu.make_async_remote_copy(
            src_ref=sin[a].at[2 * px + py], dst_ref=land[a].at[j], send_sem=ssem.at[j * n + a],
            recv_sem=rsem.at[j * n + a], device_id=(px, py, c), device_id_type=MESH)
            for j, (px, py) in enumerate(_chip_peers()) for a in range(n)]

    def start(sin, land, sems):
        for cp in copies(sin, land, sems):
            cp.start()

    def finish(sin, land, sems):
        for cp in copies(sin, land, sems):
            cp.wait()

    return dict(args=list(send), out_shape=[jax.ShapeDtypeStruct((3,) + s.shape[1:], s.dtype) for s in send],
                sems=[pltpu.SemaphoreType.DMA((3 * n,))] * 2, start=start, mid=lambda *a: None, finish=finish)


def _sum_shard(keep, land, name):
    _, rows, cols = keep.shape
    tr = 256 if rows % 256 == 0 else rows

    def body(me_ref, m_ref, l_ref, o_ref):
        o_ref[...] = ((m_ref[...] + l_ref[0].astype(F32)) + l_ref[1].astype(F32)) + l_ref[2].astype(F32)

    own = pl.BlockSpec((None, tr, cols), lambda i, me: (me[0], i, 0))
    me = (2 * lax.axis_index("x") + lax.axis_index("y")).reshape(1)
    return pl.pallas_call(
        body, grid_spec=pltpu.PrefetchScalarGridSpec(
            num_scalar_prefetch=1, grid=(rows // tr,),
            in_specs=[own, pl.BlockSpec((3, tr, cols), lambda i, me: (0, i, 0))],
            out_specs=pl.BlockSpec((tr, cols), lambda i, me: (i, 0))),
        out_shape=jax.ShapeDtypeStruct((rows, cols), F32),
        name=name, compiler_params=_cp(("parallel",)))(me, keep, land)


def _join_cores(halves):
    n = len(halves)

    def body(*refs):
        ins, outs = refs[:n], refs[n:2 * n]
        ssem, rsem = refs[2 * n:]
        x, y, c = lax.axis_index("x"), lax.axis_index("y"), lax.axis_index("c")
        sends = [pltpu.make_async_remote_copy(
            src_ref=ins[a], dst_ref=outs[a].at[c], send_sem=ssem.at[a], recv_sem=rsem.at[a],
            device_id=(x, y, 1 - c), device_id_type=MESH) for a in range(n)]
        for cp in sends:
            cp.start()
        for a in range(n):
            sends[a].wait_send()
            pltpu.make_async_remote_copy(
                src_ref=ins[a], dst_ref=outs[a].at[1 - c], send_sem=ssem.at[a], recv_sem=rsem.at[a],
                device_id=(x, y, 1 - c), device_id_type=MESH).wait_recv()

    res = pl.pallas_call(
        body, in_specs=[ANY] * n, out_specs=[ANY] * n,
        out_shape=[jax.ShapeDtypeStruct((2,) + h.shape, h.dtype) for h in halves],
        scratch_shapes=[pltpu.SemaphoreType.DMA((n,))] * 2, name="join_cores")(*halves)
    c = lax.axis_index("c")
    res = [lax.dynamic_update_index_in_dim(r, h, c, 0) for r, h in zip(res, halves)]
    return [r.reshape((2 * r.shape[1],) + r.shape[2:]) for r in res]


def _gather_all(vec):
    def copies(kind, ins, outs, sems):
        ssem, rsem = sems
        x, y, c = lax.axis_index("x"), lax.axis_index("y"), lax.axis_index("c")
        me = 4 * x + 2 * y + c
        out = []
        for k in range(1, 8):
            px, py, pc = x ^ (k >> 2), y ^ ((k >> 1) & 1), c ^ (k & 1)
            dst = outs[0].at[me] if kind == "send" else outs[0].at[4 * px + 2 * py + pc]
            out.append(pltpu.make_async_remote_copy(
                src_ref=ins[0], dst_ref=dst, send_sem=ssem.at[k - 1], recv_sem=rsem.at[k - 1],
                device_id=(px, py, pc), device_id_type=MESH))
        return out

    def body(v_ref, o_ref, ssem, rsem):
        refs = ([v_ref], [o_ref], (ssem, rsem))
        for cp in copies("send", *refs):
            cp.start()
        for cp in copies("recv", *refs):
            cp.wait_recv()
        for cp in copies("send", *refs):
            cp.wait_send()

    res = pl.pallas_call(
        body, in_specs=[ANY], out_specs=ANY, out_shape=jax.ShapeDtypeStruct((8,) + vec.shape, vec.dtype),
        scratch_shapes=[pltpu.SemaphoreType.DMA((7,))] * 2, name="gather_all")(vec)
    me = 4 * lax.axis_index("x") + 2 * lax.axis_index("y") + lax.axis_index("c")
    return lax.dynamic_update_index_in_dim(res, vec, me, 0)


def _rope_tables():
    pos = (jnp.arange(R, dtype=jnp.int32) - PAD).astype(F32)
    half = HD // 2
    inv = ROPE_THETA ** (-jnp.arange(half, dtype=F32) / half)
    ang = pos[:, None] * inv[None, :]
    cos, sin = jnp.cos(ang), jnp.sin(ang)
    cs = jnp.tile(cos, (1, 4))
    sn = jnp.tile(jnp.concatenate([-sin, sin], axis=1), (1, 2))
    return cs, sn


def _local_step(x, target, p):
    w0t = p["ab_w_in"]
    conv_w = jnp.concatenate([p["ab_conv_w"], jnp.zeros((1, CC), F32)], axis=0)
    cs, sn = _rope_tables()

    plan = _gather_plan([p["sb_w_out"], p["ab_w_out"], p["ab_w_pw2"]])
    ((h0, xn0, z0),), gathered = _rows_call("pre0", [_embed_proj(x, p["meta_tokens"], p["ab_pre_norm"], w0t)], plan)
    wo1, wo0, wpw = plan["post"](gathered)
    wo1, wo0, wpw = wo1.reshape(D, D), wo0.reshape(D, D), wpw.reshape(CC, CC)
    plan = _gather_plan([p["sb_w_in"]])
    ((o0, a0, lse0), (cv0, s0)), gathered = _rows_call(
        "fwd0", [_swa_fwd(z0, cs, sn, p["ab_sinks"]),
                 _conv_fwd(z0, conv_w, p["ab_conv_b"], p["ab_conv_ln_g"], p["ab_conv_ln_b"])], plan)
    (w1,) = plan["post"](gathered)
    t0, c0 = _pw2_fwd(s0, wpw, z0)
    wo0h = wo0.reshape(2, CC, D)
    y0, h1, xn1 = _post_rms_fwd([a0, c0], wo0h, h0, p["ab_post_norm"], p["sb_pre_norm"], "out_proj0_norms")

    q1, k1, v1, g1 = _in_proj1(xn1, w1)
    o1, m1, car1, trips1 = _sb_fwd(q1, k1, v1, g1)

    dh2, dy1, d_sb_post, loss_row = _tail(m1, wo1, h1, p["sb_post_norm"], target)

    dm1 = _mm([(dy1, wo1)], F32, "out_proj1_dx", 544, 1024, tb=True)
    d_wo1 = _mm([(m1, dy1)], BF16, "out_proj1_dw", 512, 1024, ta=True)
    dq1, dk1, dv1, dg1 = _sb_bwd(trips1, q1, k1, v1, car1, dm1, g1, o1)
    dz1 = [dq1, dk1, dv1, dg1]
    d_w1 = _in_proj1_dw(xn1, dz1)

    dh1, d_sb_pre, dy0, d_ab_post = _rms_post_bwd(dz1, w1, h1, p["sb_pre_norm"], dh2, y0, p["ab_post_norm"],
                                                  "in_proj1_dx_norms")
    dmix0 = _mm([(dy0, wo0)], F32, "out_proj0_dx", 544, 1024, tb=True)
    d_wo0 = jnp.concatenate([_mm([(a0, dy0)], BF16, "out_proj0_dw_a", 512, 1024, ta=True),
                             _mm([(c0, dy0)], BF16, "out_proj0_dw_b", 512, 1024, ta=True)], axis=0)
    dgb0, ds0, d_wpw = _pw2_bwd(dmix0, t0, z0, wpw, s0)
    d_wpw = d_wpw.astype(BF16)
    early = ("sb_w_in", "sb_w_out", "ab_w_out", "ab_w_pw2")
    own1, got1 = _pair_exchange([d_w1, d_wo1.reshape(N_CHIPS, 256, D), d_wo0.reshape(N_CHIPS, 256, D),
                                 d_wpw.reshape(N_CHIPS, 128, CC)], "pair_exchange1")
    pair1 = [_sum_pair(o, t, BF16, "sum_pair_" + nm) for o, t, nm in zip(own1, got1, early)]
    plan = _scatter_plan([pr[1] for pr in pair1])
    ((dglu0, d_convw, d_small), (dq0, dga0, dkv0, d_sinks)), land1 = _rows_call(
        "bwd0", [_conv_bwd(ds0, cv0, z0, conv_w, p["ab_conv_ln_g"], p["ab_conv_ln_b"]),
                 _swa_bwd(z0, cs, sn, p["ab_sinks"], o0, dmix0, lse0)], plan)
    halves1 = [_sum_shard(pr[0], la, "sum_shard_" + nm) for pr, la, nm in zip(pair1, land1, early)]
    dz0 = jnp.concatenate([dq0, dkv0, dga0, dglu0, dgb0], axis=1)
    d_w0t = _mm([(dz0, xn0)], BF16, "in_proj0_dw", 1408, 512, ta=True)
    own0, got0 = _pair_exchange([d_w0t.reshape(N_CHIPS, 704, D)], "pair_exchange0")
    keep0, send0 = _sum_pair(own0[0], got0[0], BF16, "sum_pair_ab_w_in")
    plan = _scatter_plan([send0])
    dxn0, land0 = _mm([(dz0, w0t)], F32, "in_proj0_dx", 544, 1024, plan=plan)
    half0 = _sum_shard(keep0, land0[0], "sum_shard_ab_w_in")
    dh0_first, grad_x, d_ab_pre = _rms_bwd(dxn0, h0, p["ab_pre_norm"], dh1, F32, "rms_bwd0", split=True)

    grads = {
        "meta_tokens": dh0_first[PAD:TB], "ab_pre_norm": d_ab_pre, "ab_sinks": d_sinks[0:1, 0:8],
        "ab_conv_w": d_convw[0:CONV_W], "ab_conv_b": d_small[0:1], "ab_conv_ln_g": d_small[1:2],
        "ab_conv_ln_b": d_small[2:3], "ab_post_norm": d_ab_post, "sb_pre_norm": d_sb_pre, "sb_post_norm": d_sb_post,
    }
    h_sb_in, h_sb_out, h_ab_out, h_pw2 = halves1
    return loss_row, grad_x, grads, [half0, h_ab_out, h_pw2, h_sb_in, h_sb_out]


SMALL_ROWS = 80
REP_ROWS = 32

WEIGHTS = ["meta_tokens", "ab_pre_norm", "ab_w_in", "ab_sinks", "ab_conv_w", "ab_conv_b", "ab_conv_ln_g",
           "ab_conv_ln_b", "ab_w_pw2", "ab_w_out", "ab_post_norm", "sb_pre_norm", "sb_w_in", "sb_w_out",
           "sb_post_norm"]
BIG = ["ab_w_in", "ab_w_out", "ab_w_pw2", "sb_w_in", "sb_w_out"]


def _pack_small(conv_w, meta, sb_pre, sb_post):
    pad = lambda a, rows: jnp.pad(a, ((0, rows - a.shape[0]), (0, 0)))
    return jnp.concatenate([pad(conv_w, 32), meta.reshape(32, 128), pad(sb_pre.reshape(2, 128), 8),
                            pad(sb_post.reshape(2, 128), 8)], axis=0)


def _unpack_small(s):
    return s[0:31], s[32:64].reshape(16, 256), s[64:66].reshape(1, 256), s[72:74].reshape(1, 256)


REP_LOSS = 3592


def _pack_rep(pre, post, conv_b, ln_g, ln_b, sinks, extra=None):
    flat = jnp.concatenate([pre.reshape(-1), post.reshape(-1), conv_b.reshape(-1), ln_g.reshape(-1),
                            ln_b.reshape(-1), sinks.reshape(-1)] + ([] if extra is None else [extra.reshape(-1)]))
    flat = jnp.concatenate([flat, jnp.zeros((REP_ROWS * 128 - flat.shape[0],), F32)])
    return flat.reshape(REP_ROWS, 128)


def _unpack_rep(r):
    f = r.reshape(-1)
    return (f[0:1024].reshape(1, 1024), f[1024:2048].reshape(1, 1024), f[2048:2560].reshape(1, 512),
            f[2560:3072].reshape(1, 512), f[3072:3584].reshape(1, 512), f[3584:3592].reshape(1, 8))


def _chips_to_cols(w):
    return w.transpose(1, 0, 2).reshape(w.shape[1], -1)


def kernel(x, meta_tokens, ab_pre_norm, ab_w_in, ab_sinks, ab_conv_w, ab_conv_b, ab_conv_ln_g, ab_conv_ln_b, ab_w_pw2, ab_w_out, ab_post_norm, sb_pre_norm, sb_w_in, sb_w_out, sb_post_norm, loss_target, m_meta_tokens, m_ab_pre_norm, m_ab_w_in, m_ab_sinks, m_ab_conv_w, m_ab_conv_b, m_ab_conv_ln_g, m_ab_conv_ln_b, m_ab_w_pw2, m_ab_w_out, m_ab_post_norm, m_sb_pre_norm, m_sb_w_in, m_sb_w_out, m_sb_post_norm, v_meta_tokens, v_ab_pre_norm, v_ab_w_in, v_ab_sinks, v_ab_conv_w, v_ab_conv_b, v_ab_conv_ln_g, v_ab_conv_ln_b, v_ab_w_pw2, v_ab_w_out, v_ab_post_norm, v_sb_pre_norm, v_sb_w_in, v_sb_w_out, v_sb_post_norm):
    w = dict(meta_tokens=meta_tokens, ab_pre_norm=ab_pre_norm, ab_w_in=ab_w_in, ab_sinks=ab_sinks,
             ab_conv_w=ab_conv_w, ab_conv_b=ab_conv_b, ab_conv_ln_g=ab_conv_ln_g, ab_conv_ln_b=ab_conv_ln_b,
             ab_w_pw2=ab_w_pw2, ab_w_out=ab_w_out, ab_post_norm=ab_post_norm, sb_pre_norm=sb_pre_norm,
             sb_w_in=sb_w_in, sb_w_out=sb_w_out, sb_post_norm=sb_post_norm)
    m = dict(meta_tokens=m_meta_tokens, ab_pre_norm=m_ab_pre_norm, ab_w_in=m_ab_w_in, ab_sinks=m_ab_sinks,
             ab_conv_w=m_ab_conv_w, ab_conv_b=m_ab_conv_b, ab_conv_ln_g=m_ab_conv_ln_g,
             ab_conv_ln_b=m_ab_conv_ln_b, ab_w_pw2=m_ab_w_pw2, ab_w_out=m_ab_w_out, ab_post_norm=m_ab_post_norm,
             sb_pre_norm=m_sb_pre_norm, sb_w_in=m_sb_w_in, sb_w_out=m_sb_w_out, sb_post_norm=m_sb_post_norm)
    v = dict(meta_tokens=v_meta_tokens, ab_pre_norm=v_ab_pre_norm, ab_w_in=v_ab_w_in, ab_sinks=v_ab_sinks,
             ab_conv_w=v_ab_conv_w, ab_conv_b=v_ab_conv_b, ab_conv_ln_g=v_ab_conv_ln_g,
             ab_conv_ln_b=v_ab_conv_ln_b, ab_w_pw2=v_ab_w_pw2, ab_w_out=v_ab_w_out, ab_post_norm=v_ab_post_norm,
             sb_pre_norm=v_sb_pre_norm, sb_w_in=v_sb_w_in, sb_w_out=v_sb_w_out, sb_post_norm=v_sb_post_norm)

    def small_of(d):
        return _pack_small(d["ab_conv_w"][0], d["meta_tokens"], d["sb_pre_norm"], d["sb_post_norm"])

    def rep_of(d):
        return _pack_rep(d["ab_pre_norm"], d["ab_post_norm"], d["ab_conv_b"], d["ab_conv_ln_g"], d["ab_conv_ln_b"],
                         d["ab_sinks"])

    g_in0, g_small = _gather_chips([ab_w_in[0].T.astype(BF16), small_of(w)])
    conv_w_f = _chips_to_cols(g_small[:, 0:31])
    meta_f = _chips_to_cols(g_small[:, 32:64].reshape(N_CHIPS, 16, 256))
    sb_pre_f = g_small[:, 64:66].reshape(1, D)
    sb_post_f = g_small[:, 72:74].reshape(1, D)
    full = {
        "meta_tokens": meta_f, "ab_pre_norm": ab_pre_norm, "ab_w_in": g_in0.reshape(AB_IN, D),
        "ab_sinks": ab_sinks, "ab_conv_w": conv_w_f, "ab_conv_b": ab_conv_b, "ab_conv_ln_g": ab_conv_ln_g,
        "ab_conv_ln_b": ab_conv_ln_b, "ab_w_pw2": ab_w_pw2[0].astype(BF16), "ab_w_out": ab_w_out[0].astype(BF16),
        "ab_post_norm": ab_post_norm, "sb_pre_norm": sb_pre_f, "sb_w_in": sb_w_in[0].astype(BF16),
        "sb_w_out": sb_w_out[0].astype(BF16), "sb_post_norm": sb_post_f,
    }

    loss_row, grad_x, g, halves = _local_step(x[0], loss_target[0], full)

    total = _join_cores(halves)

    rep_g = _pack_rep(g["ab_pre_norm"], g["ab_post_norm"], g["ab_conv_b"], g["ab_conv_ln_g"], g["ab_conv_ln_b"],
                      g["ab_sinks"], loss_row[0:1, 0:1])
    vec = jnp.concatenate([rep_g, jnp.pad(g["ab_conv_w"].reshape(124, 128), ((0, 4), (0, 0))),
                           g["meta_tokens"].reshape(128, 128), g["sb_pre_norm"].reshape(8, 128),
                           g["sb_post_norm"].reshape(8, 128)], axis=0)
    vec_sum = _sum8(_gather_all(vec), "sum8_small")

    out_g, out_d, out_m, out_v = {}, {}, {}, {}
    for i, k in enumerate(BIG):
        shp = w[k].shape
        if k == "ab_w_in":
            res = _adamw(w[k][0].T, [total[i]], m[k][0].T, v[k][0].T, "adamw_" + k)
            out_g[k], out_d[k], out_m[k], out_v[k] = [r.T.reshape(shp) for r in res]
            continue
        res = _adamw(w[k][0], [total[i]], m[k][0], v[k][0], "adamw_" + k)
        out_g[k], out_d[k], out_m[k], out_v[k] = [r.reshape(shp) for r in res]

    rep_sum = vec_sum[0:REP_ROWS]
    loss = rep_sum.reshape(-1)[REP_LOSS]
    me = 2 * lax.axis_index("x") + lax.axis_index("y")
    small_sum = _pack_small(
        lax.dynamic_slice_in_dim(vec_sum[32:156].reshape(CONV_W, CC), me * 128, 128, axis=1),
        lax.dynamic_slice_in_dim(vec_sum[160:288].reshape(N_META, D), me * 256, 256, axis=1),
        lax.dynamic_slice_in_dim(vec_sum[288:296].reshape(1, D), me * 256, 256, axis=1),
        lax.dynamic_slice_in_dim(vec_sum[296:304].reshape(1, D), me * 256, 256, axis=1))
    res = _adamw(small_of(w), [small_sum], small_of(m), small_of(v), "adamw_small")
    for dst, r in zip((out_g, out_d, out_m, out_v), res):
        cw, mt, pre, post = _unpack_small(r)
        dst["ab_conv_w"], dst["meta_tokens"], dst["sb_pre_norm"], dst["sb_post_norm"] = cw[None], mt, pre, post
    res = _adamw(rep_of(w), [rep_sum], rep_of(m), rep_of(v), "adamw_rep")
    for dst, r in zip((out_g, out_d, out_m, out_v), res):
        (dst["ab_pre_norm"], dst["ab_post_norm"], dst["ab_conv_b"], dst["ab_conv_ln_g"], dst["ab_conv_ln_b"],
         dst["ab_sinks"]) = _unpack_rep(r)

    return (loss, grad_x[None], *[out_g[k] for k in WEIGHTS], *[out_d[k] for k in WEIGHTS],
            *[out_m[k] for k in WEIGHTS], *[out_v[k] for k in WEIGHTS])
```

```python
import functools

import jax
import jax.numpy as jnp
from jax import lax
from jax.experimental import pallas as pl
from jax.experimental.pallas import tpu as pltpu

F32 = jnp.float32
BF16 = jnp.bfloat16

D = 1024
SEQ = 2048
N_META = 16
TB = 128
TR = 272
PAD = TB - N_META
R = SEQ + TB
NB = R // TB
HD = 64
ROPE_THETA = 10000.0
NORM_EPS = 1e-6
LN_EPS = 1e-5
NEG = -1e30
CONV_W = 31
SCALE = HD ** -0.5
N_CHIPS = 4

C_Q, C_K, C_V, C_GA, C_GLU, C_GB = 0, 512, 640, 768, 1280, 2304
AB_IN = 2816

ADAM_LR, ADAM_B1, ADAM_B2, ADAM_EPS, ADAM_WD, ADAM_STEP = 0.001, 0.9, 0.999, 1e-08, 0.01, 10

VMEM_LIMIT = 56 * 1024 * 1024


def _cp(sem):
    return pltpu.CompilerParams(dimension_semantics=sem, vmem_limit_bytes=VMEM_LIMIT)


def _sig(x):
    return 1.0 / (1.0 + jnp.exp(-x))


def _dot(a, b):
    return lax.dot_general(a, b, (((1,), (0,)), ((), ())), preferred_element_type=F32)


def _dot_nt(a, b):
    return lax.dot_general(a, b, (((1,), (1,)), ((), ())), preferred_element_type=F32)


def _dot_tn(a, b):
    return lax.dot_general(a, b, (((0,), (0,)), ((), ())), preferred_element_type=F32)


def _mm(pairs, out_dtype, name, tm, tn, ta=False, tb=False, plan=None):
    pairs = [(a, b if isinstance(b, tuple) else (b, None)) for a, b in pairs]
    a0, (b0, _) = pairs[0]
    m = a0.shape[1] if ta else a0.shape[0]
    n = b0.shape[-2] if tb else b0.shape[-1]
    npairs = len(pairs)
    dims = (((0 if ta else 1,), (1 if tb else 0,)), ((), ()))
    c_in = len(plan["args"]) if plan else 0
    c_out = len(plan["out_shape"]) if plan else 0
    steps = (m // tm) * (n // tn)

    def body(*refs):
        o_ref = refs[2 * npairs + c_in]
        if plan:
            comm = (refs[2 * npairs:2 * npairs + c_in], refs[2 * npairs + c_in + 1:2 * npairs + c_in + 1 + c_out],
                    refs[2 * npairs + c_in + 1 + c_out:])
            step = pl.program_id(0) * (n // tn) + pl.program_id(1)

            @pl.when(step == 0)
            def _():
                plan["start"](*comm)

        acc = None
        for i in range(npairs):
            t = lax.dot_general(refs[2 * i][...].astype(BF16), refs[2 * i + 1][...].astype(BF16), dims,
                                preferred_element_type=F32)
            acc = t if acc is None else acc + t
        o_ref[...] = acc.astype(out_dtype)
        if plan:
            @pl.when(step == steps - 2)
            def _():
                plan["mid"](*comm)

            @pl.when(step == steps - 1)
            def _():
                plan["finish"](*comm)

    in_specs, args = [], []
    for a, (b, sel) in pairs:
        k = a.shape[0] if ta else a.shape[1]
        in_specs.append(pl.BlockSpec((k, tm), lambda i, j: (0, i)) if ta else pl.BlockSpec((tm, k), lambda i, j: (i, 0)))
        bshape, bidx = ((tn, k), lambda i, j: (j, 0)) if tb else ((k, tn), lambda i, j: (0, j))
        if sel is None:
            in_specs.append(pl.BlockSpec(bshape, bidx))
        else:
            in_specs.append(pl.BlockSpec((None,) + bshape, functools.partial(lambda i, j, f, s: (s,) + f(i, j), f=bidx, s=sel)))
        args += [a, b]
    out_spec = pl.BlockSpec((tm, tn), lambda i, j: (i, j))
    out_shape = jax.ShapeDtypeStruct((m, n), out_dtype)
    if not plan:
        return pl.pallas_call(
            body, grid=(m // tm, n // tn), in_specs=in_specs, out_specs=out_spec, out_shape=out_shape, name=name,
            compiler_params=_cp(("parallel", "parallel")))(*args)
    assert steps >= 2
    res = pl.pallas_call(
        body, grid=(m // tm, n // tn), in_specs=in_specs + [ANY] * c_in, out_specs=[out_spec] + [ANY] * c_out,
        out_shape=[out_shape] + plan["out_shape"], scratch_shapes=plan["sems"], name=name,
        compiler_params=_cp(("arbitrary", "arbitrary")))(*args, *plan["args"])
    return res[0], res[1:]


PW_TM = 544


def _pw2_fwd(s, w, z0):
    def body(s_ref, w_ref, g_ref, t_ref, c_ref):
        t = _dot(s_ref[...], w_ref[...])
        gv = g_ref[...]
        t_ref[...] = t
        c_ref[...] = (t * (gv * _sig(gv))).astype(BF16)

    blk = pl.BlockSpec((PW_TM, CC), lambda i: (i, 0))
    return pl.pallas_call(
        body, grid=(R // PW_TM,),
        in_specs=[blk, pl.BlockSpec((CC, CC), lambda i: (0, 0)), _cols_spec(PW_TM, CC, lambda i: (i, C_GB))],
        out_specs=[blk, blk],
        out_shape=[jax.ShapeDtypeStruct((R, CC), F32), jax.ShapeDtypeStruct((R, CC), BF16)],
        name="pw2_fwd", compiler_params=_cp(("parallel",)))(s, w, z0)


def _pw2_bwd(dmix, t, z0, w, s):
    def body(d_ref, t_ref, g_ref, w_ref, s_ref, dg_ref, ds_ref, dw_ref):
        @pl.when(pl.program_id(0) == 0)
        def _():
            dw_ref[...] = jnp.zeros_like(dw_ref)

        gv, dv = g_ref[...], d_ref[...]
        sg = _sig(gv)
        dg_ref[...] = (dv * t_ref[...] * (sg * (1.0 + gv * (1.0 - sg)))).astype(BF16)
        dt = (dv * (gv * sg)).astype(BF16)
        ds_ref[...] = _dot_nt(dt, w_ref[...])
        dw_ref[...] += _dot_tn(s_ref[...], dt)

    blk = pl.BlockSpec((PW_TM, CC), lambda i: (i, 0))
    full = pl.BlockSpec((CC, CC), lambda i: (0, 0))
    return pl.pallas_call(
        body, grid=(R // PW_TM,),
        in_specs=[_cols_spec(PW_TM, CC, lambda i: (i, CC)), blk, _cols_spec(PW_TM, CC, lambda i: (i, C_GB)), full, blk],
        out_specs=[blk, blk, full],
        out_shape=[jax.ShapeDtypeStruct((R, CC), BF16), jax.ShapeDtypeStruct((R, CC), F32),
                   jax.ShapeDtypeStruct((CC, CC), F32)],
        name="pw2_bwd", compiler_params=_cp(("arbitrary",)))(dmix, t, z0, w, s)


def _embed_proj(x, meta, g, wt):
    def body(x_ref, m_ref, g_ref, w_ref, h_ref, xn_ref, z_ref):
        n = pl.program_id(0)
        top = jnp.concatenate([jnp.zeros((PAD, D), F32), m_ref[...]], axis=0)
        hv = jnp.where(n == 0, top, x_ref[...])
        h_ref[...] = hv
        r = lax.rsqrt(jnp.mean(hv * hv, axis=1, keepdims=True) + NORM_EPS)
        xn = (hv * r * g_ref[...]).astype(BF16)
        xn_ref[...] = xn
        z_ref[...] = _dot_nt(xn, w_ref[...])

    blk = pl.BlockSpec((TB, D), lambda n: (n, 0))
    return dict(
        body=body,
        in_specs=[pl.BlockSpec((TB, D), lambda n: (jnp.maximum(n - 1, 0), 0)), pl.BlockSpec((N_META, D), lambda n: (0, 0)),
                  pl.BlockSpec((1, D), lambda n: (0, 0)), pl.BlockSpec(wt.shape, lambda n: (0, 0))],
        args=[x, meta, g, wt],
        out_specs=[blk, blk, pl.BlockSpec((TB, AB_IN), lambda n: (n, 0))],
        out_shape=[jax.ShapeDtypeStruct((R, D), F32), jax.ShapeDtypeStruct((R, D), BF16),
                   jax.ShapeDtypeStruct((R, AB_IN), F32)],
        scratch=[])


def _in_proj1_dw(xn, dzs):
    t = 512

    def body(x_ref, *refs):
        o_ref = refs[len(dzs)]
        xv = x_ref[...]
        for j in range(len(dzs)):
            o_ref[j] = _dot_tn(xv, refs[j][...]).astype(BF16)

    return pl.pallas_call(
        body, grid=(D // t, D // t),
        in_specs=[pl.BlockSpec((R, t), lambda i, n: (0, i))] + [pl.BlockSpec((R, t), lambda i, n: (0, n))] * len(dzs),
        out_specs=pl.BlockSpec((len(dzs), t, t), lambda i, n: (0, i, n)),
        out_shape=jax.ShapeDtypeStruct((len(dzs), D, D), BF16), name="in_proj1_dw",
        compiler_params=_cp(("parallel", "parallel")))(xn, *dzs)


def _out_proj0_dw(xs, dy):
    def body(a_ref, c_ref, d_ref, o_ref):
        dv = d_ref[...]
        o_ref[0] = _dot_tn(a_ref[...], dv).astype(BF16)
        o_ref[1] = _dot_tn(c_ref[...], dv).astype(BF16)

    xin = pl.BlockSpec((R, CC), lambda n: (0, 0))
    return pl.pallas_call(
        body, grid=(2,), in_specs=[xin, xin, pl.BlockSpec((R, 512), lambda n: (0, n))],
        out_specs=pl.BlockSpec((2, CC, 512), lambda n: (0, 0, n)),
        out_shape=jax.ShapeDtypeStruct((2, CC, D), BF16), name="out_proj0_dw",
        compiler_params=_cp(("parallel",)))(xs[0], xs[1], dy)


def _in_proj1(xn, w):
    def body(x_ref, w_ref, q_ref, k_ref, v_ref, g_ref):
        xv = x_ref[...]
        q_ref[...] = _dot(xv, w_ref[0]).astype(BF16)
        k_ref[...] = _dot(xv, w_ref[1]).astype(BF16)
        v_ref[...] = _dot(xv, w_ref[2]).astype(BF16)
        g_ref[...] = _dot(xv, w_ref[3])

    blk = pl.BlockSpec((TR, D), lambda n: (n, 0))
    sd = jax.ShapeDtypeStruct((R, D), BF16)
    return pl.pallas_call(
        body, grid=(R // TR,), in_specs=[blk, pl.BlockSpec(w.shape, lambda n: (0, 0, 0))],
        out_specs=[blk, blk, blk, blk], out_shape=[sd, sd, sd, jax.ShapeDtypeStruct((R, D), F32)],
        name="in_proj1", compiler_params=_cp(("parallel",)))(xn, w)


def _rms_fwd(h, g, name):
    def body(h_ref, g_ref, o_ref):
        x = h_ref[...]
        r = lax.rsqrt(jnp.mean(x * x, axis=1, keepdims=True) + NORM_EPS)
        o_ref[...] = (x * r * g_ref[...]).astype(BF16)

    return pl.pallas_call(
        body, grid=(R // TR,),
        in_specs=[pl.BlockSpec((TR, D), lambda n: (n, 0)), pl.BlockSpec((1, D), lambda n: (0, 0))],
        out_specs=pl.BlockSpec((TR, D), lambda n: (n, 0)),
        out_shape=jax.ShapeDtypeStruct((R, D), BF16), name=name, compiler_params=_cp(("parallel",)))(h, g)


def _rms_bwd(dout, x, g, res, out_dtype, name, split=False):
    has_res = res is not None

    def body(*refs):
        if split:
            refs = list(refs)
            dx_rest_ref = refs.pop(-2)
        if has_res:
            d_ref, x_ref, g_ref, r_ref, dx_ref, dg_ref = refs
        else:
            d_ref, x_ref, g_ref, dx_ref, dg_ref = refs
        n = pl.program_id(0)
        xv = x_ref[...]
        dv = d_ref[...]
        r = lax.rsqrt(jnp.mean(xv * xv, axis=1, keepdims=True) + NORM_EPS)
        xh = xv * r
        dxh = dv * g_ref[...]
        dx = r * (dxh - xh * jnp.mean(dxh * xh, axis=1, keepdims=True))
        if has_res:
            dx = dx + r_ref[...]
        row = lax.broadcasted_iota(jnp.int32, (TB, D), 0) + n * TB
        dx = jnp.where(row >= PAD, dx, 0.0).astype(out_dtype)
        if split:
            @pl.when(n == 0)
            def _():
                dx_ref[...] = dx

            @pl.when(n > 0)
            def _():
                dx_rest_ref[...] = dx
        else:
            dx_ref[...] = dx

        @pl.when(n == 0)
        def _():
            dg_ref[...] = jnp.zeros_like(dg_ref)

        dg_ref[...] += jnp.sum(dv * xh, axis=0, keepdims=True)

    blk = pl.BlockSpec((TB, D), lambda n: (n, 0))
    vec = pl.BlockSpec((1, D), lambda n: (0, 0))
    ins = [dout, x, g] + ([res] if has_res else [])
    in_specs = [blk, blk, vec] + ([blk] if has_res else [])
    if split:
        out_specs = [pl.BlockSpec((TB, D), lambda n: (0, 0)), pl.BlockSpec((TB, D), lambda n: (jnp.maximum(n - 1, 0), 0)), vec]
        out_shape = [jax.ShapeDtypeStruct((TB, D), out_dtype), jax.ShapeDtypeStruct((SEQ, D), out_dtype),
                     jax.ShapeDtypeStruct((1, D), F32)]
    else:
        out_specs = [blk, vec]
        out_shape = [jax.ShapeDtypeStruct((R, D), out_dtype), jax.ShapeDtypeStruct((1, D), F32)]
    return pl.pallas_call(
        body, grid=(NB,), in_specs=in_specs, out_specs=out_specs, out_shape=out_shape,
        name=name, compiler_params=_cp(("arbitrary",)))(*ins)


def _post_rms_fwd(xs, w, h, g_post, g_next, name):
    nx = len(xs)

    def body(*refs):
        x_refs, (w_ref, h_ref, gp_ref, gn_ref, y_ref, o_ref, xn_ref) = refs[:nx], refs[nx:]
        yv = _dot(x_refs[0][...], w_ref[0])
        for j in range(1, nx):
            yv = yv + _dot(x_refs[j][...], w_ref[j])
        y_ref[...] = yv
        r = lax.rsqrt(jnp.mean(yv * yv, axis=1, keepdims=True) + NORM_EPS)
        hn = h_ref[...] + yv * r * gp_ref[...]
        o_ref[...] = hn
        r2 = lax.rsqrt(jnp.mean(hn * hn, axis=1, keepdims=True) + NORM_EPS)
        xn_ref[...] = (hn * r2 * gn_ref[...]).astype(BF16)

    blk = pl.BlockSpec((TR, D), lambda n: (n, 0))
    vec = pl.BlockSpec((1, D), lambda n: (0, 0))
    xblk = [pl.BlockSpec((TR, x.shape[1]), lambda n: (n, 0)) for x in xs]
    return pl.pallas_call(
        body, grid=(R // TR,), in_specs=xblk + [pl.BlockSpec(w.shape, lambda n: (0, 0, 0)), blk, vec, vec],
        out_specs=[blk, blk, blk],
        out_shape=[jax.ShapeDtypeStruct((R, D), F32), jax.ShapeDtypeStruct((R, D), F32),
                   jax.ShapeDtypeStruct((R, D), BF16)],
        name=name, compiler_params=_cp(("parallel",)))(*xs, w, h, g_post, g_next)


def _rms_post_bwd(dzs, w, h, g, res, y, g_post, name):
    nz = len(dzs)

    def body(*refs):
        dz_refs, (w_ref, h_ref, g_ref, r_ref, y_ref, gp_ref, dh_ref, dg_ref, dy_ref, dgp_ref) = refs[:nz], refs[nz:]
        n = pl.program_id(0)

        @pl.when(n == 0)
        def _():
            dg_ref[...] = jnp.zeros_like(dg_ref)
            dgp_ref[...] = jnp.zeros_like(dgp_ref)

        dv = _dot_nt(dz_refs[0][...], w_ref[0])
        for j in range(1, nz):
            dv = dv + _dot_nt(dz_refs[j][...], w_ref[j])
        hv = h_ref[...]
        r = lax.rsqrt(jnp.mean(hv * hv, axis=1, keepdims=True) + NORM_EPS)
        xh = hv * r
        dxh = dv * g_ref[...]
        dh = r * (dxh - xh * jnp.mean(dxh * xh, axis=1, keepdims=True)) + r_ref[...]
        row = lax.broadcasted_iota(jnp.int32, (TR, D), 0) + n * TR
        dh = jnp.where(row >= PAD, dh, 0.0)
        dh_ref[...] = dh
        dg_ref[...] += jnp.sum(dv * xh, axis=0, keepdims=True)
        yv = y_ref[...]
        ry = lax.rsqrt(jnp.mean(yv * yv, axis=1, keepdims=True) + NORM_EPS)
        yh = yv * ry
        dyh = dh * gp_ref[...]
        dy_ref[...] = (ry * (dyh - yh * jnp.mean(dyh * yh, axis=1, keepdims=True))).astype(BF16)
        dgp_ref[...] += jnp.sum(dh * yh, axis=0, keepdims=True)

    blk = pl.BlockSpec((TR, D), lambda n: (n, 0))
    vec = pl.BlockSpec((1, D), lambda n: (0, 0))
    return pl.pallas_call(
        body, grid=(R // TR,),
        in_specs=[blk] * nz + [pl.BlockSpec(w.shape, lambda n: (0, 0, 0)), blk, vec, blk, blk, vec],
        out_specs=[blk, vec, blk, vec],
        out_shape=[jax.ShapeDtypeStruct((R, D), F32), jax.ShapeDtypeStruct((1, D), F32),
                   jax.ShapeDtypeStruct((R, D), BF16), jax.ShapeDtypeStruct((1, D), F32)],
        name=name, compiler_params=_cp(("arbitrary",)))(*dzs, w, h, g, res, y, g_post)


def _cols_spec(rows, width, where):
    def index(*g):
        r, c = where(*g)
        return r * rows, (c if isinstance(c, int) else pl.multiple_of(c, 128))
    return pl.BlockSpec((pl.Element(rows), pl.Element(width)), index)


def _tail(x, w, h, g, target):
    steps = R // TR

    def body(x_ref, w_ref, h_ref, g_ref, t_ref, d_ref, dy_ref, dg_ref, l_ref):
        n = pl.program_id(0)

        @pl.when(n == 0)
        def _():
            dg_ref[...] = jnp.zeros_like(dg_ref)
            l_ref[...] = jnp.zeros_like(l_ref)

        yv = _dot(x_ref[...], w_ref[...])
        r = lax.rsqrt(jnp.mean(yv * yv, axis=1, keepdims=True) + NORM_EPS)
        yh = yv * r
        tv = t_ref[...]
        tv = jnp.where(n == 0, jnp.concatenate([jnp.zeros((TB, D), F32), tv[0:TR - TB]], axis=0), tv)
        row = lax.broadcasted_iota(jnp.int32, (TR, D), 0) + n * TR
        err = jnp.where(row >= TB, (h_ref[...] + yh * g_ref[...]) - tv, 0.0)
        dv = err * (1.0 / D)
        d_ref[...] = dv
        l_ref[...] += jnp.sum(err * err, axis=0, keepdims=True)
        dyh = dv * g_ref[...]
        dy_ref[...] = (r * (dyh - yh * jnp.mean(dyh * yh, axis=1, keepdims=True))).astype(BF16)
        dg_ref[...] += jnp.sum(dv * yh, axis=0, keepdims=True)

        @pl.when(n == steps - 1)
        def _():
            tot = jnp.sum(l_ref[...], axis=1, keepdims=True) * (0.5 / D)
            l_ref[...] = jnp.broadcast_to(tot, (1, D))

    blk = pl.BlockSpec((TR, D), lambda n: (n, 0))
    vec = pl.BlockSpec((1, D), lambda n: (0, 0))
    tgt = pl.BlockSpec((pl.Element(TR), pl.Element(D)),
                       lambda n: (pl.multiple_of(jnp.maximum(TR * n - TB, 0), 8), 0))
    return pl.pallas_call(
        body, grid=(steps,),
        in_specs=[pl.BlockSpec((TR, x.shape[1]), lambda n: (n, 0)), pl.BlockSpec(w.shape, lambda n: (0, 0)), blk, vec, tgt],
        out_specs=[blk, blk, vec, vec],
        out_shape=[jax.ShapeDtypeStruct((R, D), F32), jax.ShapeDtypeStruct((R, D), BF16),
                   jax.ShapeDtypeStruct((1, D), F32), jax.ShapeDtypeStruct((1, D), F32)],
        name="out_proj1_tail", compiler_params=_cp(("arbitrary",)))(x, w, h, g, target)


def _lane_row(shape):
    return lax.broadcasted_iota(jnp.int32, shape, 1), lax.broadcasted_iota(jnp.int32, shape, 0)


def _rot_half(x, lane):
    return jnp.where(lane % HD < HD // 2, pltpu.roll(x, 128 - HD // 2, 1), pltpu.roll(x, HD // 2, 1))


def _swa_blocks(n):
    return (0, jnp.maximum(n - 1, 0), n)


SWA_STACKS = ((0, 0), (0, 1), (1, 0), (1, 1))


def _swa_masks(n, lane, row):
    qpos = n * TB + (row & (TB - 1))
    kp = (n - 1) * TB + lane
    kc = n * TB + lane
    m0 = (lane >= PAD) & (qpos - lane >= TB)
    mp = (kp >= PAD) & (qpos >= kp) & (qpos - kp < TB)
    mc = (kc >= PAD) & (qpos >= kc)
    return (m0, mp, mc)


def _stack_pair(xa, xb, par):
    lane = lax.broadcasted_iota(jnp.int32, (TB, 128), 1)
    keep = (lane < HD) if par == 0 else (lane >= HD)
    return jnp.concatenate([jnp.where(keep, xa, 0.0), jnp.where(keep, xb, 0.0)], axis=0)


def _per_head(a, b):
    row = lax.broadcasted_iota(jnp.int32, (2 * TB, 1), 0)
    return jnp.where(row < TB, a, b)


def _swa_load(n, zq_ref, zkv_ref, cs_ref, sn_ref, lane):
    r0 = pl.multiple_of(n * TB, TB)
    csq, snq = cs_ref[pl.ds(r0, TB), :], sn_ref[pl.ds(r0, TB), :]
    qc = []
    for c in range(4):
        x = zq_ref[:, c * 128:(c + 1) * 128]
        qc.append((x * csq + _rot_half(x, lane) * snq) * SCALE)
    qst = [_stack_pair(qc[2 * g], qc[2 * g + 1], par).astype(BF16) for g, par in SWA_STACKS]
    kvs = []
    for b in _swa_blocks(n):
        b0 = pl.multiple_of(b * TB, TB)
        csb, snb = cs_ref[pl.ds(b0, TB), :], sn_ref[pl.ds(b0, TB), :]
        kx = zkv_ref[pl.ds(b0, TB), 0:128]
        kr = kx * csb + _rot_half(kx, lane) * snb
        vx = zkv_ref[pl.ds(b0, TB), 128:256]
        kvs.append((kr.astype(BF16), pltpu.roll(kr, HD, 1).astype(BF16),
                    vx.astype(BF16), pltpu.roll(vx, HD, 1).astype(BF16), csb, snb, b0))
    return qst, (csq, snq), kvs


def _swa_fwd(z0, cs, sn, sinks):
    def body(zq_ref, zkv_ref, cs_ref, sn_ref, sk_ref, ga_ref, o_ref, a_ref, lse_ref):
        n = pl.program_id(0)
        lane, row = _lane_row((TB, 128))
        lo = lane < HD
        masks = _swa_masks(n, *_lane_row((2 * TB, 128)))
        qst, _, kvs = _swa_load(n, zq_ref, zkv_ref, cs_ref, sn_ref, lane)
        ss = [[jnp.where(m, _dot_nt(qst[si], k if par == g else ka), NEG)
               for (k, ka, _, _, _, _, _), m in zip(kvs, masks)] for si, (g, par) in enumerate(SWA_STACKS)]
        o2, lse2 = [], []
        for si, (g, par) in enumerate(SWA_STACKS):
            sink = _per_head(sk_ref[0, 4 * g + par], sk_ref[0, 4 * g + 2 + par])
            s = ss[si]
            mx = jnp.maximum(jnp.maximum(jnp.max(s[0], axis=1, keepdims=True), jnp.max(s[1], axis=1, keepdims=True)),
                             jnp.max(s[2], axis=1, keepdims=True))
            mx = jnp.maximum(mx, sink)
            es = [jnp.exp(sb - mx) for sb in s]
            den = (jnp.sum(es[0], axis=1, keepdims=True) + jnp.sum(es[1], axis=1, keepdims=True)
                   + jnp.sum(es[2], axis=1, keepdims=True) + jnp.exp(sink - mx))
            inv = 1.0 / den
            t = jnp.zeros((2 * TB, 128), F32)
            for (_, _, v, va, _, _, _), e in zip(kvs, es):
                t = t + _dot((e * inv).astype(BF16), v if par == g else va)
            o2.append(t)
            lse2.append(mx + jnp.log(den))
        lse_t = jnp.zeros((TB, 128), F32)
        for g in range(2):
            for t in range(2):
                rows = slice(t * TB, (t + 1) * TB)
                c = 2 * g + t
                oc = jnp.where(lo, o2[2 * g][rows], o2[2 * g + 1][rows])
                o_ref[:, c * 128:(c + 1) * 128] = oc
                gv = ga_ref[:, c * 128:(c + 1) * 128]
                a_ref[:, c * 128:(c + 1) * 128] = (oc * (gv * _sig(gv))).astype(BF16)
                for par in range(2):
                    lse_t = jnp.where(lane == 4 * g + 2 * t + par, lse2[2 * g + par][rows], lse_t)
        lse_ref[...] = lse_t

    full = pl.BlockSpec((R, 128), lambda n: (0, 0))
    return dict(
        body=body,
        in_specs=[pl.BlockSpec((TB, 512), lambda n: (n, C_Q // 512)),
                  pl.BlockSpec((R, 256), lambda n: (0, C_K // 256)), full, full,
                  pl.BlockSpec(memory_space=pltpu.SMEM), _cols_spec(TB, 512, lambda n: (n, C_GA))],
        args=[z0, z0, cs, sn, sinks, z0],
        out_specs=[pl.BlockSpec((TB, 512), lambda n: (n, 0)), pl.BlockSpec((TB, 512), lambda n: (n, 0)),
                   pl.BlockSpec((TB, 128), lambda n: (n, 0))],
        out_shape=[jax.ShapeDtypeStruct((R, 512), F32), jax.ShapeDtypeStruct((R, 512), BF16),
                   jax.ShapeDtypeStruct((R, 128), F32)],
        scratch=[])


def _swa_bwd(z0, cs, sn, sinks, o, dmix, lse):
    def body(zq_ref, zkv_ref, cs_ref, sn_ref, sk_ref, ga_ref, o_ref, dm_ref, lse_ref,
             dq_ref, dga_ref, dkv_ref, dsk_ref, do_ref, acc_ref):
        n = pl.program_id(0)

        @pl.when(n == 0)
        def _():
            acc_ref[...] = jnp.zeros_like(acc_ref)
            dsk_ref[...] = jnp.zeros_like(dsk_ref)

        gv, dmv = ga_ref[...], dm_ref[...]
        sg = _sig(gv)
        dga_ref[...] = (dmv * o_ref[...] * (sg * (1.0 + gv * (1.0 - sg)))).astype(BF16)
        do_ref[...] = dmv * (gv * sg)
        lane, row = _lane_row((TB, 128))
        lo = lane < HD
        masks = _swa_masks(n, *_lane_row((2 * TB, 128)))
        qst, (csq, snq), kvs = _swa_load(n, zq_ref, zkv_ref, cs_ref, sn_ref, lane)
        lse_t = lse_ref[...]
        ss = [[jnp.where(m, _dot_nt(qst[si], k if par == g else ka), NEG)
               for (k, ka, _, _, _, _, _), m in zip(kvs, masks)] for si, (g, par) in enumerate(SWA_STACKS)]
        dobs, deltas, lses, dps = [], [], [], []
        for g, par in SWA_STACKS:
            ca, cb = slice(2 * g * 128, (2 * g + 1) * 128), slice((2 * g + 1) * 128, (2 * g + 2) * 128)
            dom = _stack_pair(do_ref[:, ca], do_ref[:, cb], par)
            deltas.append(jnp.sum(dom * jnp.concatenate([o_ref[:, ca], o_ref[:, cb]], axis=0), axis=1, keepdims=True))
            dob = dom.astype(BF16)
            dobs.append(dob)
            lses.append(jnp.concatenate(
                [jnp.sum(jnp.where(lane == 4 * g + 2 * t + par, lse_t, 0.0), axis=1, keepdims=True) for t in range(2)],
                axis=0))
            dps.append([_dot_nt(dob, v if par == g else va) for (_, _, v, va, _, _, _) in kvs])
        dk_al = [jnp.zeros((TB, 128), F32) for _ in range(3)]
        dk_mis = [jnp.zeros((TB, 128), F32) for _ in range(3)]
        dv_al = [jnp.zeros((TB, 128), F32) for _ in range(3)]
        dv_mis = [jnp.zeros((TB, 128), F32) for _ in range(3)]
        dsk_t = jnp.zeros((TB, 128), F32)
        dq2 = []
        for si, (g, par) in enumerate(SWA_STACKS):
            dqt = jnp.zeros((2 * TB, 128), F32)
            for bi, (k, ka, _, _, _, _, _) in enumerate(kvs):
                p = jnp.exp(ss[si][bi] - lses[si])
                ds = (p * (dps[si][bi] - deltas[si])).astype(BF16)
                dqt = dqt + _dot(ds, k if par == g else ka)
                dkh = _dot_tn(ds, qst[si])
                dvh = _dot_tn(p.astype(BF16), dobs[si])
                if par == g:
                    dk_al[bi] = dk_al[bi] + dkh
                    dv_al[bi] = dv_al[bi] + dvh
                else:
                    dk_mis[bi] = dk_mis[bi] + dkh
                    dv_mis[bi] = dv_mis[bi] + dvh
            dq2.append(dqt)
            sink = _per_head(sk_ref[0, 4 * g + par], sk_ref[0, 4 * g + 2 + par])
            dsk = -jnp.exp(sink - lses[si]) * deltas[si]
            for t in range(2):
                dsk_t = jnp.where(lane == 4 * g + 2 * t + par, dsk[t * TB:(t + 1) * TB], dsk_t)
        for g in range(2):
            for t in range(2):
                rows = slice(t * TB, (t + 1) * TB)
                c = 2 * g + t
                dqc = jnp.where(lo, dq2[2 * g][rows], dq2[2 * g + 1][rows]) * SCALE
                dq_ref[:, c * 128:(c + 1) * 128] = (dqc * csq + _rot_half(dqc * snq, lane)).astype(BF16)
        for bi, (_, _, _, _, csb, snb, b0) in enumerate(kvs):
            dk = dk_al[bi] + pltpu.roll(dk_mis[bi], HD, 1)
            dv = dv_al[bi] + pltpu.roll(dv_mis[bi], HD, 1)
            acc_ref[pl.ds(b0, TB), 0:128] += dk * csb + _rot_half(dk * snb, lane)
            acc_ref[pl.ds(b0, TB), 128:256] += dv
        dsk_ref[0:1, :] += jnp.sum(dsk_t, axis=0, keepdims=True)

        @pl.when(n == NB - 1)
        def _():
            dkv_ref[...] = acc_ref[...].astype(BF16)

    full = pl.BlockSpec((R, 128), lambda n: (0, 0))
    b512 = pl.BlockSpec((TB, 512), lambda n: (n, 0))
    return dict(
        body=body,
        in_specs=[pl.BlockSpec((TB, 512), lambda n: (n, C_Q // 512)),
                  pl.BlockSpec((R, 256), lambda n: (0, C_K // 256)), full, full,
                  pl.BlockSpec(memory_space=pltpu.SMEM), _cols_spec(TB, 512, lambda n: (n, C_GA)),
                  b512, b512, pl.BlockSpec((TB, 128), lambda n: (n, 0))],
        args=[z0, z0, cs, sn, sinks, z0, o, dmix, lse],
        out_specs=[b512, b512, pl.BlockSpec((R, 256), lambda n: (0, 0)), pl.BlockSpec((8, 128), lambda n: (0, 0))],
        out_shape=[jax.ShapeDtypeStruct((R, 512), BF16), jax.ShapeDtypeStruct((R, 512), BF16),
                   jax.ShapeDtypeStruct((R, 256), BF16), jax.ShapeDtypeStruct((8, 128), F32)],
        scratch=[pltpu.VMEM((TB, 512), F32), pltpu.VMEM((R, 256), F32)])


CC = 512
HALO = CONV_W - 1


def _conv_fwd(z0, conv_w, conv_b, ln_g, ln_b):
    def body(g_ref, w_ref, cb_ref, lg_ref, lb_ref, cv_ref, s_ref, ubuf):
        n = pl.program_id(0)

        @pl.when(n == 0)
        def _():
            ubuf[...] = jnp.zeros_like(ubuf)

        u = g_ref[:, 0:CC] * _sig(g_ref[:, CC:2 * CC])
        for k in range(8):
            ubuf[k, 0:TB + 8, :] = ubuf[k, TB:2 * TB + 8, :]
            ubuf[k, pl.ds(TB + 8 - k, TB), :] = u
        acc = jnp.zeros((TB, CC), F32)
        for w in range(CONV_W):
            off = TB - HALO + w
            acc = acc + ubuf[off % 8, pl.ds(off + 8 - off % 8, TB), :] * w_ref[w:w + 1, :]
        cv = acc + cb_ref[...]
        cv_ref[...] = cv
        xc = cv - jnp.mean(cv, axis=1, keepdims=True)
        rs = lax.rsqrt(jnp.mean(xc * xc, axis=1, keepdims=True) + LN_EPS)
        ln = xc * rs * lg_ref[...] + lb_ref[...]
        s_ref[...] = (ln * _sig(ln)).astype(BF16)

    vec = pl.BlockSpec((1, CC), lambda n: (0, 0))
    blk = pl.BlockSpec((TB, CC), lambda n: (n, 0))
    return dict(
        body=body,
        in_specs=[_cols_spec(TB, 2 * CC, lambda n: (n, C_GLU)),
                  pl.BlockSpec((32, CC), lambda n: (0, 0)), vec, vec, vec],
        args=[z0, conv_w, conv_b, ln_g, ln_b],
        out_specs=[blk, blk],
        out_shape=[jax.ShapeDtypeStruct((R, CC), F32), jax.ShapeDtypeStruct((R, CC), BF16)],
        scratch=[pltpu.VMEM((8, 2 * TB + 8, CC), F32)])


def _conv_bwd(ds, cv, z0, conv_w, ln_g, ln_b):
    def body(ds_ref, cv_ref, g_ref, w_ref, lg_ref, lb_ref, dglu_ref, dw_ref, dsm_ref, dbuf):
        n = pl.program_id(0)

        @pl.when(n == 0)
        def _():
            dbuf[...] = jnp.zeros_like(dbuf)
            dw_ref[...] = jnp.zeros_like(dw_ref)
            dsm_ref[...] = jnp.zeros_like(dsm_ref)

        cv = cv_ref[...]
        xc = cv - jnp.mean(cv, axis=1, keepdims=True)
        rs = lax.rsqrt(jnp.mean(xc * xc, axis=1, keepdims=True) + LN_EPS)
        xh = xc * rs
        ln = xh * lg_ref[...] + lb_ref[...]
        sg = _sig(ln)
        dln = ds_ref[...] * (sg * (1.0 + ln * (1.0 - sg)))
        dxh = dln * lg_ref[...]
        dcv = rs * (dxh - jnp.mean(dxh, axis=1, keepdims=True) - xh * jnp.mean(dxh * xh, axis=1, keepdims=True))
        dsm_ref[0:1, :] += jnp.sum(dcv, axis=0, keepdims=True)
        dsm_ref[1:2, :] += jnp.sum(dln * xh, axis=0, keepdims=True)
        dsm_ref[2:3, :] += jnp.sum(dln, axis=0, keepdims=True)
        for k in range(8):
            dbuf[k, TB:2 * TB + 8, :] = dbuf[k, 0:TB + 8, :]
            dbuf[k, pl.ds(8 - k, TB), :] = dcv
        a = g_ref[:, 0:CC]
        sb = _sig(g_ref[:, CC:2 * CC])
        u = a * sb
        du = jnp.zeros((TB, CC), F32)
        for w in range(CONV_W):
            off = HALO - w
            sh = dbuf[off % 8, pl.ds(off + 8 - off % 8, TB), :]
            du = du + sh * w_ref[w:w + 1, :]
            dw_ref[w:w + 1, :] += jnp.sum(u * sh, axis=0, keepdims=True)
        dglu_ref[:, 0:CC] = (du * sb).astype(BF16)
        dglu_ref[:, CC:2 * CC] = (du * a * sb * (1.0 - sb)).astype(BF16)

    rev = lambda n: (NB - 1 - n, 0)
    vec = pl.BlockSpec((1, CC), lambda n: (0, 0))
    blk = pl.BlockSpec((TB, CC), rev)
    return dict(
        body=body,
        in_specs=[blk, blk, _cols_spec(TB, 2 * CC, lambda n: (NB - 1 - n, C_GLU)),
                  pl.BlockSpec((32, CC), lambda n: (0, 0)), vec, vec],
        args=[ds, cv, z0, conv_w, ln_g, ln_b],
        out_specs=[pl.BlockSpec((TB, 2 * CC), rev), pl.BlockSpec((32, CC), lambda n: (0, 0)),
                   pl.BlockSpec((8, CC), lambda n: (0, 0))],
        out_shape=[jax.ShapeDtypeStruct((R, 2 * CC), BF16), jax.ShapeDtypeStruct((32, CC), F32),
                   jax.ShapeDtypeStruct((8, CC), F32)],
        scratch=[pltpu.VMEM((8, 2 * TB + 8, CC), F32)])


def _split_dot(x, t):
    hi = x.astype(BF16)
    lo = (x - hi.astype(F32)).astype(BF16)
    return _dot(hi, t) + _dot(lo, t)


def _stack_heads(x):
    lane = lax.broadcasted_iota(jnp.int32, (TB, 128), 1)
    return jnp.concatenate([jnp.where(lane < HD, x, 0.0), jnp.where(lane < HD, 0.0, x)], axis=0).astype(BF16)


def _sb_stack(qv, i):
    lane2, row2 = _lane_row((2 * TB, 128))
    qpos2 = i * TB + (row2 & (TB - 1))
    lane, row = _lane_row((TB, 128))
    return _stack_heads(qv), lane2, qpos2, (row > lane).astype(BF16)


SB_U = 3
SB_DEAD = -104.0
SB_P = 4


def _sb_fwd(q, k, v, g):
    def body(q_ref, k_ref, v_ref, g_ref, o_ref, m_ref, c_ref, n_ref):
        p, i = pl.program_id(0), pl.program_id(1)
        lane, row = _lane_row((TB, 128))
        lo = lane < HD
        slabs = [slice(s * 128, (s + 1) * 128) for s in range(SB_P)]
        q2s = []
        for sl in slabs:
            q2, lane2, qpos2, tri_gt = _sb_stack(q_ref[:, sl].astype(F32) * SCALE, i)
            q2s.append(q2)

        def cond(st):
            t, _, c2s = st
            alive = jnp.max(c2s[0])
            for c2 in c2s[1:]:
                alive = jnp.maximum(alive, jnp.max(c2))
            return jnp.logical_and(i - SB_U * t >= 0, alive > SB_DEAD)

        def step(st):
            t, accs, c2s = st
            accs, c2s = list(accs), list(c2s)
            jrs = [i - SB_U * t - u for u in range(SB_U)]
            j0s = [pl.multiple_of(jnp.maximum(jr, 0) * TB, TB) for jr in jrs]
            valids = []
            for jr in jrs:
                kpos = jr * TB + lane2
                valids.append((kpos >= PAD) & (kpos < qpos2))
            zs = [[jnp.where(valid, _dot_nt(q2s[s], k_ref[pl.ds(j0, TB), slabs[s]]), NEG)
                   for j0, valid in zip(j0s, valids)] for s in range(SB_P)]
            lbs, l1s = [], []
            for s in range(SB_P):
                lbs.append([jnp.minimum(z, 0.0) - jnp.log(1.0 + jnp.exp(-jnp.abs(z))) for z in zs[s]])
                l1s.append([lb - z for lb, z in zip(lbs[s], zs[s])])
            sfxs = [[_split_dot(l1, tri_gt) for l1 in l1s[s]] for s in range(SB_P)]
            carries = []
            for s in range(SB_P):
                cs, c2 = [], c2s[s]
                for jr, l1 in zip(jrs, l1s[s]):
                    cs.append(c2)
                    c_ref[:, slabs[s]] = jnp.where(lane == 2 * jr, c2[0:TB],
                                                   jnp.where(lane == 2 * jr + 1, c2[TB:2 * TB], c_ref[:, slabs[s]]))
                    c2 = c2 + jnp.sum(l1, axis=1, keepdims=True)
                carries.append(cs)
                c2s[s] = c2
            for s in range(SB_P):
                for j0, valid, lb, sfx, cu in zip(j0s, valids, lbs[s], sfxs[s], carries[s]):
                    a = jnp.exp(lb + sfx + cu).astype(BF16)
                    av = _dot(a, v_ref[pl.ds(j0, TB), slabs[s]])
                    accs[s] = accs[s] + jnp.where(lo, av[0:TB], av[TB:2 * TB])
            return t + 1, tuple(accs), tuple(c2s)

        c_ref[...] = jnp.zeros_like(c_ref)
        init = (jnp.int32(0), tuple(jnp.zeros((TB, 128), F32) for _ in slabs),
                tuple(jnp.zeros((2 * TB, 1), F32) for _ in slabs))
        t, accs, _ = lax.while_loop(cond, step, init)
        for sl, acc in zip(slabs, accs):
            o_ref[:, sl] = acc
            gv = g_ref[:, sl]
            m_ref[:, sl] = (acc * (gv * _sig(gv))).astype(BF16)
        n_ref[p, i] = t

    wide = SB_P * 128
    slab = pl.BlockSpec((R, wide), lambda p, i: (0, p))
    blk = pl.BlockSpec((TB, wide), lambda p, i: (i, p))
    sd = jax.ShapeDtypeStruct((R, D), F32)
    return pl.pallas_call(
        body, grid=(D // wide, NB), in_specs=[blk, slab, slab, blk],
        out_specs=[blk, blk, blk, pl.BlockSpec(memory_space=pltpu.SMEM)],
        out_shape=[sd, jax.ShapeDtypeStruct((R, D), BF16), sd, jax.ShapeDtypeStruct((D // wide, NB), jnp.int32)],
        name="sb_fwd", compiler_params=_cp(("arbitrary", "arbitrary")))(q, k, v, g)


def _sb_bwd(trips, q, k, v, car, dm, g, o):
    def body(n_ref, q_ref, k_ref, v_ref, c_ref, dm_ref, g_ref, o_ref, dq_ref, dko_ref, dvo_ref, dg_ref,
             dk_ref, dv_ref):
        p, i = pl.program_id(0), pl.program_id(1)

        @pl.when(i == 0)
        def _():
            dk_ref[...] = jnp.zeros_like(dk_ref)
            dv_ref[...] = jnp.zeros_like(dv_ref)

        lane, row = _lane_row((TB, 128))
        lo = lane < HD
        tri_lt = (row < lane).astype(BF16)
        slabs = [slice(s * 128, (s + 1) * 128) for s in range(SB_P)]
        q2s, do2s, cts = [], [], []
        for sl in slabs:
            q2, lane2, qpos2, tri_gt = _sb_stack(q_ref[:, sl].astype(F32) * SCALE, i)
            q2s.append(q2)
            gv, dmv = g_ref[:, sl], dm_ref[:, sl]
            sg = _sig(gv)
            dg_ref[:, sl] = (dmv * o_ref[:, sl] * (sg * (1.0 + gv * (1.0 - sg)))).astype(BF16)
            do2s.append(_stack_heads(dmv * (gv * sg)))
            cts.append(c_ref[:, sl])
        trips_i = n_ref[p, i]
        first = jnp.maximum(i + 1 - SB_U * trips_i, 0)

        def step(t, carry):
            dqs, g2s = carry
            dqs, g2s = list(dqs), list(g2s)
            jrs = [first + SB_U * t + u for u in range(SB_U)]
            j0s = [pl.multiple_of(jnp.minimum(jr, i) * TB, TB) for jr in jrs]
            valids = []
            for jr in jrs:
                kpos = jr * TB + lane2
                valids.append((kpos >= PAD) & (kpos < qpos2))
            ks = [[k_ref[pl.ds(j0, TB), sl] for j0 in j0s] for sl in slabs]
            zs = [[jnp.where(valid, _dot_nt(q2s[s], kj), NEG) for kj, valid in zip(ks[s], valids)] for s in range(SB_P)]
            das = [[_dot_nt(do2s[s], v_ref[pl.ds(j0, TB), slabs[s]]) for j0 in j0s] for s in range(SB_P)]
            es = [[jnp.exp(-jnp.abs(z)) for z in zs[s]] for s in range(SB_P)]
            lbs = [[jnp.minimum(z, 0.0) - jnp.log(1.0 + e) for z, e in zip(zs[s], es[s])] for s in range(SB_P)]
            l1s = [[lb - z for lb, z in zip(lbs[s], zs[s])] for s in range(SB_P)]
            sfxs = [[_split_dot(l1, tri_gt) for l1 in l1s[s]] for s in range(SB_P)]
            a_s, gmats, gpres = [], [], []
            for s in range(SB_P):
                a_l, gm_l, gp_l, g2 = [], [], [], g2s[s]
                for jr, valid, lb, sfx, da in zip(jrs, valids, lbs[s], sfxs[s], das[s]):
                    later = jnp.concatenate(
                        [jnp.sum(jnp.where(lane == 2 * jr + hh, cts[s], 0.0), axis=1, keepdims=True) for hh in range(2)],
                        axis=0)
                    a = jnp.exp(lb + sfx + later)
                    gmat = da * a
                    a_l.append(a.astype(BF16))
                    gm_l.append(gmat)
                    gp_l.append(g2)
                    g2 = g2 + jnp.sum(gmat, axis=1, keepdims=True)
                a_s.append(a_l)
                gmats.append(gm_l)
                gpres.append(gp_l)
                g2s[s] = g2
            pres = [[gp + _split_dot(gmat, tri_lt) for gp, gmat in zip(gpres[s], gmats[s])] for s in range(SB_P)]
            for s in range(SB_P):
                for j0, kj, valid, z, e, gmat, pre, a in zip(j0s, ks[s], valids, zs[s], es[s], gmats[s], pres[s], a_s[s]):
                    r = 1.0 / (1.0 + e)
                    big = z >= 0.0
                    beta = jnp.where(big, r, e * r)
                    omb = jnp.where(big, e * r, r)
                    dz = (gmat * omb - beta * pre).astype(BF16)
                    dq2 = _dot(dz, kj)
                    dqs[s] = dqs[s] + jnp.where(lo, dq2[0:TB], dq2[TB:2 * TB])
                    dk_ref[pl.ds(j0, TB), slabs[s]] += _dot_tn(dz, q2s[s])
                    dv_ref[pl.ds(j0, TB), slabs[s]] += _dot_tn(a, do2s[s])
            return tuple(dqs), tuple(g2s)

        init = (tuple(jnp.zeros((TB, 128), F32) for _ in slabs), tuple(jnp.zeros((2 * TB, 1), F32) for _ in slabs))
        dqs, _ = lax.fori_loop(0, trips_i, step, init)
        for sl, dq in zip(slabs, dqs):
            dq_ref[:, sl] = (dq * SCALE).astype(BF16)

        @pl.when(i == NB - 1)
        def _():
            dko_ref[...] = dk_ref[...].astype(BF16)
            dvo_ref[...] = dv_ref[...].astype(BF16)

    wide = SB_P * 128
    slab = pl.BlockSpec((R, wide), lambda p, i: (0, p))
    blk = pl.BlockSpec((TB, wide), lambda p, i: (i, p))
    sd = jax.ShapeDtypeStruct((R, D), BF16)
    return pl.pallas_call(
        body, grid=(D // wide, NB),
        in_specs=[pl.BlockSpec(memory_space=pltpu.SMEM), blk, slab, slab, blk, blk, blk, blk],
        out_specs=[blk, slab, slab, blk], out_shape=[sd, sd, sd, sd],
        scratch_shapes=[pltpu.VMEM((R, wide), F32), pltpu.VMEM((R, wide), F32)], name="sb_bwd",
        compiler_params=_cp(("arbitrary", "arbitrary")))(trips, q, k, v, car, dm, g, o)


def _adamw(w, parts, m, v, name):
    rows, cols = w.shape
    tr = next((t for t in (256, 176) if rows % t == 0), rows)
    nparts = len(parts)

    def body(*refs):
        w_ref = refs[0]
        p_refs = refs[1:1 + nparts]
        m_ref, v_ref, g_ref, d_ref, nm_ref, nv_ref = refs[1 + nparts:]
        g = p_refs[0][...]
        for p_ref in p_refs[1:]:
            g = g + p_ref[...]
        nm = ADAM_B1 * m_ref[...] + (1.0 - ADAM_B1) * g
        nv = ADAM_B2 * v_ref[...] + (1.0 - ADAM_B2) * (g * g)
        m_hat = nm / (1.0 - ADAM_B1 ** ADAM_STEP)
        v_hat = nv / (1.0 - ADAM_B2 ** ADAM_STEP)
        g_ref[...] = g
        d_ref[...] = -ADAM_LR * (m_hat / (jnp.sqrt(v_hat) + ADAM_EPS) + ADAM_WD * w_ref[...])
        nm_ref[...] = nm
        nv_ref[...] = nv

    blk = pl.BlockSpec((tr, cols), lambda i: (i, 0))
    sd = jax.ShapeDtypeStruct((rows, cols), F32)
    return pl.pallas_call(
        body, grid=(rows // tr,), in_specs=[blk] * (3 + nparts), out_specs=[blk] * 4, out_shape=[sd] * 4,
        name=name, compiler_params=_cp(("parallel",)))(w, *parts, m, v)


def _sum8(buf, name):
    _, rows, cols = buf.shape

    def body(b_ref, o_ref):
        acc = b_ref[0]
        for i in range(1, 8):
            acc = acc + b_ref[i]
        o_ref[...] = acc

    return pl.pallas_call(
        body, out_shape=jax.ShapeDtypeStruct((rows, cols), F32), name=name,
        compiler_params=pltpu.CompilerParams(vmem_limit_bytes=VMEM_LIMIT))(buf)


MESH = pl.DeviceIdType.MESH
ANY = pl.BlockSpec(memory_space=pl.ANY)


def _chip_peers():
    x, y = lax.axis_index("x"), lax.axis_index("y")
    return [(1 - x, y), (x, 1 - y), (1 - x, 1 - y)]


def _gather_chips(shards):
    plan = _gather_plan(shards)

    def body(*refs):
        n = len(shards)
        ins, outs, sems = refs[:n], refs[n:2 * n], refs[2 * n:]
        plan["start"](ins, outs, sems)
        plan["mid"](ins, outs, sems)
        plan["finish"](ins, outs, sems)

    n = len(shards)
    res = pl.pallas_call(
        body, in_specs=[ANY] * n, out_specs=[ANY] * n, out_shape=plan["out_shape"],
        scratch_shapes=plan["sems"], name="gather_chips")(*plan["args"])
    return plan["post"](res)


def _gather_plan(shards):
    n = len(shards)
    shards = [s.reshape((2, s.shape[0] // 2) + s.shape[1:]) for s in shards]

    def copies(kind, ins, outs, sems):
        s1, r1, s2, r2 = sems
        x, y, c = lax.axis_index("x"), lax.axis_index("y"), lax.axis_index("c")
        me = 2 * x + y
        out = []
        for j, (px, py) in enumerate(_chip_peers()):
            for a in range(n):
                k = j * n + a
                got = outs[a].at[2 * px + py].at[c]
                other = outs[a].at[2 * px + py].at[1 - c]
                src, dst, ss, rs, dev = {
                    "first": (ins[a].at[c], outs[a].at[me].at[c], s1, r1, (px, py, c)),
                    "landed": (got, got, s1, r1, (px, py, c)),
                    "passed": (got, got, s2, r2, (x, y, 1 - c)),
                    "theirs": (other, other, s2, r2, (x, y, 1 - c)),
                }[kind]
                out.append(pltpu.make_async_remote_copy(
                    src_ref=src, dst_ref=dst, send_sem=ss.at[k], recv_sem=rs.at[k], device_id=dev, device_id_type=MESH))
        return out

    def start(ins, outs, sems):
        for cp in copies("first", ins, outs, sems):
            cp.start()

    def mid(ins, outs, sems):
        for got, fwd in zip(copies("landed", ins, outs, sems), copies("passed", ins, outs, sems)):
            got.wait_recv()
            fwd.start()

    def finish(ins, outs, sems):
        for cp in copies("theirs", ins, outs, sems):
            cp.wait_recv()
        for cp in copies("first", ins, outs, sems) + copies("passed", ins, outs, sems):
            cp.wait_send()

    def post(res):
        me = 2 * lax.axis_index("x") + lax.axis_index("y")
        res = [lax.dynamic_update_index_in_dim(r, s, me, 0) for r, s in zip(res, shards)]
        return [r.reshape((N_CHIPS, 2 * r.shape[2]) + r.shape[3:]) for r in res]

    return dict(args=shards, out_shape=[jax.ShapeDtypeStruct((N_CHIPS,) + s.shape, s.dtype) for s in shards],
                sems=[pltpu.SemaphoreType.DMA((3 * n,))] * 4, start=start, mid=mid, finish=finish, post=post)


def _rows_call(name, parts, plan):
    n_in = [len(p["args"]) for p in parts]
    n_out = [len(p["out_shape"]) for p in parts]
    n_scr = [len(p["scratch"]) for p in parts]
    c_in, c_out = len(plan["args"]), len(plan["out_shape"])

    def split(refs, sizes):
        out, pos = [], 0
        for k in sizes:
            out.append(refs[pos:pos + k])
            pos += k
        return out

    def body(*refs):
        ins, outs, scr = split(refs, [sum(n_in) + c_in, sum(n_out) + c_out, sum(n_scr) + len(plan["sems"])])
        p_in, p_out, p_scr = split(ins, n_in + [c_in]), split(outs, n_out + [c_out]), split(scr, n_scr + [len(plan["sems"])])
        comm = (p_in[-1], p_out[-1], p_scr[-1])
        step = pl.program_id(0)

        @pl.when(step == 0)
        def _():
            plan["start"](*comm)

        for p, i, o, s in zip(parts, p_in, p_out, p_scr):
            p["body"](*i, *o, *s)

        @pl.when(step == NB - 2)
        def _():
            plan["mid"](*comm)

        @pl.when(step == NB - 1)
        def _():
            plan["finish"](*comm)

    flat = lambda key: [v for p in parts for v in p[key]]
    res = pl.pallas_call(
        body, grid=(NB,), in_specs=flat("in_specs") + [ANY] * c_in, out_specs=flat("out_specs") + [ANY] * c_out,
        out_shape=flat("out_shape") + plan["out_shape"], scratch_shapes=flat("scratch") + plan["sems"],
        name=name, compiler_params=_cp(("arbitrary",)))(*flat("args"), *plan["args"])
    outs = split(res, n_out + [c_out])
    return outs[:-1], outs[-1]


def _pair_exchange(grads, name):
    n = len(grads)
    hs = [g.shape[1] // 2 for g in grads]
    grads = [g.reshape((N_CHIPS, 2, h) + g.shape[2:]) for g, h in zip(grads, hs)]

    def body(*refs):
        ins, got = refs[:n], refs[n:2 * n]
        ssem, rsem = refs[2 * n:]
        x, y, c = lax.axis_index("x"), lax.axis_index("y"), lax.axis_index("c")
        sends = [pltpu.make_async_remote_copy(
            src_ref=ins[a].at[:, 1 - c], dst_ref=got[a], send_sem=ssem.at[a],
            recv_sem=rsem.at[a], device_id=(x, y, 1 - c), device_id_type=MESH) for a in range(n)]
        for cp in sends:
            cp.start()
        for cp in sends:
            cp.wait()

    half_shapes = [jax.ShapeDtypeStruct((N_CHIPS, h) + g.shape[3:], g.dtype) for g, h in zip(grads, hs)]
    got = pl.pallas_call(
        body, in_specs=[ANY] * n, out_specs=[ANY] * n, out_shape=half_shapes,
        scratch_shapes=[pltpu.SemaphoreType.DMA((n,))] * 2, name=name)(*grads)
    return grads, got


def _sum_pair(both, got, send_dtype, name):
    _, _, rows, cols = both.shape
    tr = 256 if rows % 256 == 0 else rows

    def body(c_ref, a_ref, b_ref, f_ref, s_ref):
        t = a_ref[...].astype(F32) + b_ref[...].astype(F32)
        f_ref[...] = t
        s_ref[...] = t.astype(send_dtype)

    blk = pl.BlockSpec((N_CHIPS, tr, cols), lambda i, c: (0, i, 0))
    mine = pl.BlockSpec((N_CHIPS, None, tr, cols), lambda i, c: (0, c[0], i, 0))
    return pl.pallas_call(
        body, grid_spec=pltpu.PrefetchScalarGridSpec(
            num_scalar_prefetch=1, grid=(rows // tr,), in_specs=[mine, blk], out_specs=[blk, blk]),
        out_shape=[jax.ShapeDtypeStruct(got.shape, F32), jax.ShapeDtypeStruct(got.shape, send_dtype)],
        name=name, compiler_params=_cp(("parallel",)))(lax.axis_index("c").reshape(1), both, got)


def _scatter_plan(send):
    n = len(send)

    def copies(sin, land, sems):
        ssem, rsem = sems
        c = lax.axis_index("c")
        return [pltpu.make_async_remote_copy(
            src_ref=sin[a].at[2 * px + py], dst_ref=land[a].at[j], send_sem=ssem.at[j * n + a],
            recv_sem=rsem.at[j * n + a], device_id=(px, py, c), device_id_type=MESH)
            for j, (px, py) in enumerate(_chip_peers()) for a in range(n)]

    def start(sin, land, sems):
        for cp in copies(sin, land, sems):
            cp.start()

    def finish(sin, land, sems):
        for cp in copies(sin, land, sems):
            cp.wait()

    return dict(args=list(send), out_shape=[jax.ShapeDtypeStruct((3,) + s.shape[1:], s.dtype) for s in send],
                sems=[pltpu.SemaphoreType.DMA((3 * n,))] * 2, start=start, mid=lambda *a: None, finish=finish)


def _sum_shard(keep, land, name):
    _, rows, cols = keep.shape
    tr = 256 if rows % 256 == 0 else rows

    def body(me_ref, m_ref, l_ref, o_ref):
        o_ref[...] = ((m_ref[...] + l_ref[0].astype(F32)) + l_ref[1].astype(F32)) + l_ref[2].astype(F32)

    own = pl.BlockSpec((None, tr, cols), lambda i, me: (me[0], i, 0))
    me = (2 * lax.axis_index("x") + lax.axis_index("y")).reshape(1)
    return pl.pallas_call(
        body, grid_spec=pltpu.PrefetchScalarGridSpec(
            num_scalar_prefetch=1, grid=(rows // tr,),
            in_specs=[own, pl.BlockSpec((3, tr, cols), lambda i, me: (0, i, 0))],
            out_specs=pl.BlockSpec((tr, cols), lambda i, me: (i, 0))),
        out_shape=jax.ShapeDtypeStruct((rows, cols), F32),
        name=name, compiler_params=_cp(("parallel",)))(me, keep, land)


def _join_cores(halves):
    n = len(halves)

    def body(*refs):
        ins, outs = refs[:n], refs[n:2 * n]
        ssem, rsem = refs[2 * n:]
        x, y, c = lax.axis_index("x"), lax.axis_index("y"), lax.axis_index("c")
        sends = [pltpu.make_async_remote_copy(
            src_ref=ins[a], dst_ref=outs[a].at[c], send_sem=ssem.at[a], recv_sem=rsem.at[a],
            device_id=(x, y, 1 - c), device_id_type=MESH) for a in range(n)]
        for cp in sends:
            cp.start()
        for a in range(n):
            sends[a].wait_send()
            pltpu.make_async_remote_copy(
                src_ref=ins[a], dst_ref=outs[a].at[1 - c], send_sem=ssem.at[a], recv_sem=rsem.at[a],
                device_id=(x, y, 1 - c), device_id_type=MESH).wait_recv()

    res = pl.pallas_call(
        body, in_specs=[ANY] * n, out_specs=[ANY] * n,
        out_shape=[jax.ShapeDtypeStruct((2,) + h.shape, h.dtype) for h in halves],
        scratch_shapes=[pltpu.SemaphoreType.DMA((n,))] * 2, name="join_cores")(*halves)
    c = lax.axis_index("c")
    res = [lax.dynamic_update_index_in_dim(r, h, c, 0) for r, h in zip(res, halves)]
    return [r.reshape((2 * r.shape[1],) + r.shape[2:]) for r in res]


def _gather_all(vec):
    def copies(kind, ins, outs, sems):
        ssem, rsem = sems
        x, y, c = lax.axis_index("x"), lax.axis_index("y"), lax.axis_index("c")
        me = 4 * x + 2 * y + c
        out = []
        for k in range(1, 8):
            px, py, pc = x ^ (k >> 2), y ^ ((k >> 1) & 1), c ^ (k & 1)
            dst = outs[0].at[me] if kind == "send" else outs[0].at[4 * px + 2 * py + pc]
            out.append(pltpu.make_async_remote_copy(
                src_ref=ins[0], dst_ref=dst, send_sem=ssem.at[k - 1], recv_sem=rsem.at[k - 1],
                device_id=(px, py, pc), device_id_type=MESH))
        return out

    def body(v_ref, o_ref, ssem, rsem):
        refs = ([v_ref], [o_ref], (ssem, rsem))
        for cp in copies("send", *refs):
            cp.start()
        for cp in copies("recv", *refs):
            cp.wait_recv()
        for cp in copies("send", *refs):
            cp.wait_send()

    res = pl.pallas_call(
        body, in_specs=[ANY], out_specs=ANY, out_shape=jax.ShapeDtypeStruct((8,) + vec.shape, vec.dtype),
        scratch_shapes=[pltpu.SemaphoreType.DMA((7,))] * 2, name="gather_all")(vec)
    me = 4 * lax.axis_index("x") + 2 * lax.axis_index("y") + lax.axis_index("c")
    return lax.dynamic_update_index_in_dim(res, vec, me, 0)


def _rope_tables():
    pos = (jnp.arange(R, dtype=jnp.int32) - PAD).astype(F32)
    half = HD // 2
    inv = ROPE_THETA ** (-jnp.arange(half, dtype=F32) / half)
    ang = pos[:, None] * inv[None, :]
    cos, sin = jnp.cos(ang), jnp.sin(ang)
    cs = jnp.tile(cos, (1, 4))
    sn = jnp.tile(jnp.concatenate([-sin, sin], axis=1), (1, 2))
    return cs, sn


def _local_step(x, target, p):
    w0t = p["ab_w_in"]
    conv_w = jnp.concatenate([p["ab_conv_w"], jnp.zeros((1, CC), F32)], axis=0)
    cs, sn = _rope_tables()

    plan = _gather_plan([p["sb_w_out"], p["ab_w_out"], p["ab_w_pw2"]])
    ((h0, xn0, z0),), gathered = _rows_call("pre0", [_embed_proj(x, p["meta_tokens"], p["ab_pre_norm"], w0t)], plan)
    wo1, wo0, wpw = plan["post"](gathered)
    wo1, wo0, wpw = wo1.reshape(D, D), wo0.reshape(D, D), wpw.reshape(CC, CC)
    plan = _gather_plan([p["sb_w_in"]])
    ((o0, a0, lse0), (cv0, s0)), gathered = _rows_call(
        "fwd0", [_swa_fwd(z0, cs, sn, p["ab_sinks"]),
                 _conv_fwd(z0, conv_w, p["ab_conv_b"], p["ab_conv_ln_g"], p["ab_conv_ln_b"])], plan)
    (w1,) = plan["post"](gathered)
    t0, c0 = _pw2_fwd(s0, wpw, z0)
    wo0h = wo0.reshape(2, CC, D)
    y0, h1, xn1 = _post_rms_fwd([a0, c0], wo0h, h0, p["ab_post_norm"], p["sb_pre_norm"], "out_proj0_norms")

    q1, k1, v1, g1 = _in_proj1(xn1, w1)
    o1, m1, car1, trips1 = _sb_fwd(q1, k1, v1, g1)

    dh2, dy1, d_sb_post, loss_row = _tail(m1, wo1, h1, p["sb_post_norm"], target)

    dm1 = _mm([(dy1, wo1)], F32, "out_proj1_dx", 544, 1024, tb=True)
    d_wo1 = _mm([(m1, dy1)], BF16, "out_proj1_dw", 512, 1024, ta=True)
    dq1, dk1, dv1, dg1 = _sb_bwd(trips1, q1, k1, v1, car1, dm1, g1, o1)
    dz1 = [dq1, dk1, dv1, dg1]
    d_w1 = _in_proj1_dw(xn1, dz1)

    dh1, d_sb_pre, dy0, d_ab_post = _rms_post_bwd(dz1, w1, h1, p["sb_pre_norm"], dh2, y0, p["ab_post_norm"],
                                                  "in_proj1_dx_norms")
    dmix0 = _mm([(dy0, wo0)], F32, "out_proj0_dx", 544, 1024, tb=True)
    d_wo0 = _out_proj0_dw([a0, c0], dy0).reshape(D, D)
    dgb0, ds0, d_wpw = _pw2_bwd(dmix0, t0, z0, wpw, s0)
    d_wpw = d_wpw.astype(BF16)
    early = ("sb_w_in", "sb_w_out", "ab_w_out", "ab_w_pw2")
    own1, got1 = _pair_exchange([d_w1, d_wo1.reshape(N_CHIPS, 256, D), d_wo0.reshape(N_CHIPS, 256, D),
                                 d_wpw.reshape(N_CHIPS, 128, CC)], "pair_exchange1")
    pair1 = [_sum_pair(o, t, BF16, "sum_pair_" + nm) for o, t, nm in zip(own1, got1, early)]
    plan = _scatter_plan([pr[1] for pr in pair1])
    ((dglu0, d_convw, d_small), (dq0, dga0, dkv0, d_sinks)), land1 = _rows_call(
        "bwd0", [_conv_bwd(ds0, cv0, z0, conv_w, p["ab_conv_ln_g"], p["ab_conv_ln_b"]),
                 _swa_bwd(z0, cs, sn, p["ab_sinks"], o0, dmix0, lse0)], plan)
    halves1 = [_sum_shard(pr[0], la, "sum_shard_" + nm) for pr, la, nm in zip(pair1, land1, early)]
    dz0 = jnp.concatenate([dq0, dkv0, dga0, dglu0, dgb0], axis=1)
    d_w0t = _mm([(dz0, xn0)], BF16, "in_proj0_dw", 1408, 512, ta=True)
    own0, got0 = _pair_exchange([d_w0t.reshape(N_CHIPS, 704, D)], "pair_exchange0")
    keep0, send0 = _sum_pair(own0[0], got0[0], BF16, "sum_pair_ab_w_in")
    plan = _scatter_plan([send0])
    dxn0, land0 = _mm([(dz0, w0t)], F32, "in_proj0_dx", 544, 1024, plan=plan)
    half0 = _sum_shard(keep0, land0[0], "sum_shard_ab_w_in")
    dh0_first, grad_x, d_ab_pre = _rms_bwd(dxn0, h0, p["ab_pre_norm"], dh1, F32, "rms_bwd0", split=True)

    grads = {
        "meta_tokens": dh0_first[PAD:TB], "ab_pre_norm": d_ab_pre, "ab_sinks": d_sinks[0:1, 0:8],
        "ab_conv_w": d_convw[0:CONV_W], "ab_conv_b": d_small[0:1], "ab_conv_ln_g": d_small[1:2],
        "ab_conv_ln_b": d_small[2:3], "ab_post_norm": d_ab_post, "sb_pre_norm": d_sb_pre, "sb_post_norm": d_sb_post,
    }
    h_sb_in, h_sb_out, h_ab_out, h_pw2 = halves1
    return loss_row, grad_x, grads, [half0, h_ab_out, h_pw2, h_sb_in, h_sb_out]


SMALL_ROWS = 80
REP_ROWS = 32

WEIGHTS = ["meta_tokens", "ab_pre_norm", "ab_w_in", "ab_sinks", "ab_conv_w", "ab_conv_b", "ab_conv_ln_g",
           "ab_conv_ln_b", "ab_w_pw2", "ab_w_out", "ab_post_norm", "sb_pre_norm", "sb_w_in", "sb_w_out",
           "sb_post_norm"]
BIG = ["ab_w_in", "ab_w_out", "ab_w_pw2", "sb_w_in", "sb_w_out"]


def _pack_small(conv_w, meta, sb_pre, sb_post):
    pad = lambda a, rows: jnp.pad(a, ((0, rows - a.shape[0]), (0, 0)))
    return jnp.concatenate([pad(conv_w, 32), meta.reshape(32, 128), pad(sb_pre.reshape(2, 128), 8),
                            pad(sb_post.reshape(2, 128), 8)], axis=0)


def _unpack_small(s):
    return s[0:31], s[32:64].reshape(16, 256), s[64:66].reshape(1, 256), s[72:74].reshape(1, 256)


REP_LOSS = 3592


def _pack_rep(pre, post, conv_b, ln_g, ln_b, sinks, extra=None):
    flat = jnp.concatenate([pre.reshape(-1), post.reshape(-1), conv_b.reshape(-1), ln_g.reshape(-1),
                            ln_b.reshape(-1), sinks.reshape(-1)] + ([] if extra is None else [extra.reshape(-1)]))
    flat = jnp.concatenate([flat, jnp.zeros((REP_ROWS * 128 - flat.shape[0],), F32)])
    return flat.reshape(REP_ROWS, 128)


def _unpack_rep(r):
    f = r.reshape(-1)
    return (f[0:1024].reshape(1, 1024), f[1024:2048].reshape(1, 1024), f[2048:2560].reshape(1, 512),
            f[2560:3072].reshape(1, 512), f[3072:3584].reshape(1, 512), f[3584:3592].reshape(1, 8))


def _chips_to_cols(w):
    return w.transpose(1, 0, 2).reshape(w.shape[1], -1)


def kernel(x, meta_tokens, ab_pre_norm, ab_w_in, ab_sinks, ab_conv_w, ab_conv_b, ab_conv_ln_g, ab_conv_ln_b, ab_w_pw2, ab_w_out, ab_post_norm, sb_pre_norm, sb_w_in, sb_w_out, sb_post_norm, loss_target, m_meta_tokens, m_ab_pre_norm, m_ab_w_in, m_ab_sinks, m_ab_conv_w, m_ab_conv_b, m_ab_conv_ln_g, m_ab_conv_ln_b, m_ab_w_pw2, m_ab_w_out, m_ab_post_norm, m_sb_pre_norm, m_sb_w_in, m_sb_w_out, m_sb_post_norm, v_meta_tokens, v_ab_pre_norm, v_ab_w_in, v_ab_sinks, v_ab_conv_w, v_ab_conv_b, v_ab_conv_ln_g, v_ab_conv_ln_b, v_ab_w_pw2, v_ab_w_out, v_ab_post_norm, v_sb_pre_norm, v_sb_w_in, v_sb_w_out, v_sb_post_norm):
    w = dict(meta_tokens=meta_tokens, ab_pre_norm=ab_pre_norm, ab_w_in=ab_w_in, ab_sinks=ab_sinks,
             ab_conv_w=ab_conv_w, ab_conv_b=ab_conv_b, ab_conv_ln_g=ab_conv_ln_g, ab_conv_ln_b=ab_conv_ln_b,
             ab_w_pw2=ab_w_pw2, ab_w_out=ab_w_out, ab_post_norm=ab_post_norm, sb_pre_norm=sb_pre_norm,
             sb_w_in=sb_w_in, sb_w_out=sb_w_out, sb_post_norm=sb_post_norm)
    m = dict(meta_tokens=m_meta_tokens, ab_pre_norm=m_ab_pre_norm, ab_w_in=m_ab_w_in, ab_sinks=m_ab_sinks,
             ab_conv_w=m_ab_conv_w, ab_conv_b=m_ab_conv_b, ab_conv_ln_g=m_ab_conv_ln_g,
             ab_conv_ln_b=m_ab_conv_ln_b, ab_w_pw2=m_ab_w_pw2, ab_w_out=m_ab_w_out, ab_post_norm=m_ab_post_norm,
             sb_pre_norm=m_sb_pre_norm, sb_w_in=m_sb_w_in, sb_w_out=m_sb_w_out, sb_post_norm=m_sb_post_norm)
    v = dict(meta_tokens=v_meta_tokens, ab_pre_norm=v_ab_pre_norm, ab_w_in=v_ab_w_in, ab_sinks=v_ab_sinks,
             ab_conv_w=v_ab_conv_w, ab_conv_b=v_ab_conv_b, ab_conv_ln_g=v_ab_conv_ln_g,
             ab_conv_ln_b=v_ab_conv_ln_b, ab_w_pw2=v_ab_w_pw2, ab_w_out=v_ab_w_out, ab_post_norm=v_ab_post_norm,
             sb_pre_norm=v_sb_pre_norm, sb_w_in=v_sb_w_in, sb_w_out=v_sb_w_out, sb_post_norm=v_sb_post_norm)

    def small_of(d):
        return _pack_small(d["ab_conv_w"][0], d["meta_tokens"], d["sb_pre_norm"], d["sb_post_norm"])

    def rep_of(d):
        return _pack_rep(d["ab_pre_norm"], d["ab_post_norm"], d["ab_conv_b"], d["ab_conv_ln_g"], d["ab_conv_ln_b"],
                         d["ab_sinks"])

    g_in0, g_small = _gather_chips([ab_w_in[0].T.astype(BF16), small_of(w)])
    conv_w_f = _chips_to_cols(g_small[:, 0:31])
    meta_f = _chips_to_cols(g_small[:, 32:64].reshape(N_CHIPS, 16, 256))
    sb_pre_f = g_small[:, 64:66].reshape(1, D)
    sb_post_f = g_small[:, 72:74].reshape(1, D)
    full = {
        "meta_tokens": meta_f, "ab_pre_norm": ab_pre_norm, "ab_w_in": g_in0.reshape(AB_IN, D),
        "ab_sinks": ab_sinks, "ab_conv_w": conv_w_f, "ab_conv_b": ab_conv_b, "ab_conv_ln_g": ab_conv_ln_g,
        "ab_conv_ln_b": ab_conv_ln_b, "ab_w_pw2": ab_w_pw2[0].astype(BF16), "ab_w_out": ab_w_out[0].astype(BF16),
        "ab_post_norm": ab_post_norm, "sb_pre_norm": sb_pre_f, "sb_w_in": sb_w_in[0].astype(BF16),
        "sb_w_out": sb_w_out[0].astype(BF16), "sb_post_norm": sb_post_f,
    }

    loss_row, grad_x, g, halves = _local_step(x[0], loss_target[0], full)

    total = _join_cores(halves)

    rep_g = _pack_rep(g["ab_pre_norm"], g["ab_post_norm"], g["ab_conv_b"], g["ab_conv_ln_g"], g["ab_conv_ln_b"],
                      g["ab_sinks"], loss_row[0:1, 0:1])
    vec = jnp.concatenate([rep_g, jnp.pad(g["ab_conv_w"].reshape(124, 128), ((0, 4), (0, 0))),
                           g["meta_tokens"].reshape(128, 128), g["sb_pre_norm"].reshape(8, 128),
                           g["sb_post_norm"].reshape(8, 128)], axis=0)
    vec_sum = _sum8(_gather_all(vec), "sum8_small")

    out_g, out_d, out_m, out_v = {}, {}, {}, {}
    for i, k in enumerate(BIG):
        shp = w[k].shape
        if k == "ab_w_in":
            res = _adamw(w[k][0].T, [total[i]], m[k][0].T, v[k][0].T, "adamw_" + k)
            out_g[k], out_d[k], out_m[k], out_v[k] = [r.T.reshape(shp) for r in res]
            continue
        res = _adamw(w[k][0], [total[i]], m[k][0], v[k][0], "adamw_" + k)
        out_g[k], out_d[k], out_m[k], out_v[k] = [r.reshape(shp) for r in res]

    rep_sum = vec_sum[0:REP_ROWS]
    loss = rep_sum.reshape(-1)[REP_LOSS]
    me = 2 * lax.axis_index("x") + lax.axis_index("y")
    small_sum = _pack_small(
        lax.dynamic_slice_in_dim(vec_sum[32:156].reshape(CONV_W, CC), me * 128, 128, axis=1),
        lax.dynamic_slice_in_dim(vec_sum[160:288].reshape(N_META, D), me * 256, 256, axis=1),
        lax.dynamic_slice_in_dim(vec_sum[288:296].reshape(1, D), me * 256, 256, axis=1),
        lax.dynamic_slice_in_dim(vec_sum[296:304].reshape(1, D), me * 256, 256, axis=1))
    res = _adamw(small_of(w), [small_sum], small_of(m), small_of(v), "adamw_small")
    for dst, r in zip((out_g, out_d, out_m, out_v), res):
        cw, mt, pre, post = _unpack_small(r)
        dst["ab_conv_w"], dst["meta_tokens"], dst["sb_pre_norm"], dst["sb_post_norm"] = cw[None], mt, pre, post
    res = _adamw(rep_of(w), [rep_sum], rep_of(m), rep_of(v), "adamw_rep")
    for dst, r in zip((out_g, out_d, out_m, out_v), res):
        (dst["ab_pre_norm"], dst["ab_post_norm"], dst["ab_conv_b"], dst["ab_conv_ln_g"], dst["ab_conv_ln_b"],
         dst["ab_sinks"]) = _unpack_rep(r)

    return (loss, grad_x[None], *[out_g[k] for k in WEIGHTS], *[out_d[k] for k in WEIGHTS],
            *[out_m[k] for k in WEIGHTS], *[out_v[k] for k in WEIGHTS])
```
